```python
import math
import jax, jax.numpy as jnp
from jax import lax
import numpy as np

D_MODEL = 1024
BATCH = 8
SEQ = 2048
DEPTH = 1
DEC_BATCH = 128
DEC_SEQ = 4
PAST_LEN = 16384
PAGE_SIZE = 128

LRU_WIDTH = D_MODEL
N_LRU_HEADS = 16
LRU_BLOCK = LRU_WIDTH // N_LRU_HEADS
LRU_C = 8.0
SSD_WIDTH = D_MODEL
SSD_HEAD_DIM = 64
N_SSD_HEADS = SSD_WIDTH // SSD_HEAD_DIM
N_SSD_GROUPS = 2
HEADS_PER_GROUP = N_SSD_HEADS // N_SSD_GROUPS
D_STATE = 128
SSD_CHUNK = 128
CONV_WIDTH = 4
SSD_CONV_DIM = SSD_WIDTH + 2 * N_SSD_GROUPS * D_STATE
MIX_WIDTH = LRU_WIDTH + SSD_WIDTH
IN_PROJ_DIM = 2 * LRU_WIDTH + SSD_WIDTH + SSD_CONV_DIM + N_SSD_HEADS
D_FF = 4 * D_MODEL
EPS = 1e-6

kernel_name = "hymba_rglru_ssd_hybrid_step"


def rmsnorm(x, g):
    xf = x.astype(jnp.float32)
    y = xf * lax.rsqrt(jnp.mean(xf * xf, axis=-1, keepdims=True) + EPS)
    return (y * g.astype(jnp.float32)).astype(x.dtype)


def causal_conv(x, buf, w, b):
    t = x.shape[1]
    xp = jnp.concatenate([buf.astype(x.dtype), x], axis=1)
    y = b + sum(xp[:, k:k + t] * w[k] for k in range(CONV_WIDTH))
    return y, xp[:, -(CONV_WIDTH - 1):]


def linear_scan(a, b, h0):
    b = b.at[:, 0].add(a[:, 0] * h0)
    def combine(left, right):
        a1, b1 = left
        a2, b2 = right
        return a1 * a2, a2 * b1 + b2
    _, h = lax.associative_scan(combine, (a, b), axis=1)
    return h


def rg_lru(x, h0, w_a, b_a, w_x, b_x, lam):
    bsz, t, _ = x.shape
    xb = x.reshape(bsz, t, N_LRU_HEADS, LRU_BLOCK)
    r = jax.nn.sigmoid(jnp.einsum('bthi,hij->bthj', xb, w_a) + b_a).reshape(bsz, t, LRU_WIDTH)
    i = jax.nn.sigmoid(jnp.einsum('bthi,hij->bthj', xb, w_x) + b_x).reshape(bsz, t, LRU_WIDTH)
    log_a = -LRU_C * r.astype(jnp.float32) * jax.nn.softplus(-lam.astype(jnp.float32))
    a = jnp.exp(log_a)
    mult = jnp.sqrt(-jnp.expm1(2.0 * log_a))
    h = linear_scan(a, mult * (i * x).astype(jnp.float32), h0.astype(jnp.float32))
    return h, h[:, -1]


def ssd_scan(x, dt, a, bm, cm, h0):
    bsz, t = x.shape[0], x.shape[1]
    q = min(SSD_CHUNK, t)
    pad = (-t) % q
    if pad:
        pw = lambda z: jnp.pad(z, [(0, 0), (0, pad)] + [(0, 0)] * (z.ndim - 2))
        x, dt, bm, cm = pw(x), pw(dt), pw(bm), pw(cm)
    tp = t + pad
    nc = tp // q
    g, e = N_SSD_GROUPS, HEADS_PER_GROUP
    xdt = (x * dt[..., None]).reshape(bsz, nc, q, g, e, SSD_HEAD_DIM)
    da = (dt * a).reshape(bsz, nc, q, g, e)
    bc = bm.reshape(bsz, nc, q, g, D_STATE)
    cc = cm.reshape(bsz, nc, q, g, D_STATE)
    cum = jnp.cumsum(da, axis=2)
    causal = jnp.tril(jnp.ones((q, q), dtype=bool))[None, None, :, :, None, None]
    diff = cum[:, :, :, None] - cum[:, :, None, :]
    lmat = jnp.where(causal, jnp.exp(jnp.where(causal, diff, 0.0)), 0.0)
    cb = jnp.einsum('bctgn,bcsgn->bctsg', cc, bc)
    y_diag = jnp.einsum('bctsg,bctsge,bcsgep->bctgep', cb, lmat, xdt)
    decay_states = jnp.exp(cum[:, :, -1:] - cum)
    states = jnp.einsum('bclgn,bclge,bclgep->bcgepn', bc, decay_states, xdt)
    chunk_decay = jnp.exp(cum[:, :, -1])
    h0g = h0.reshape(bsz, g, e, SSD_HEAD_DIM, D_STATE)

    def step(h, inp):
        dec, st = inp
        return dec[..., None, None] * h + st, h

    h_last, h_in = lax.scan(step, h0g, (jnp.moveaxis(chunk_decay, 1, 0), jnp.moveaxis(states, 1, 0)))
    h_in = jnp.moveaxis(h_in, 0, 1)
    y_off = jnp.einsum('bclgn,bcgepn,bclge->bclgep', cc, h_in, jnp.exp(cum))
    y = (y_diag + y_off).reshape(bsz, tp, N_SSD_HEADS, SSD_HEAD_DIM)[:, :t]
    return y, h_last.reshape(bsz, N_SSD_HEADS, SSD_HEAD_DIM, D_STATE)


def hybrid_layer(x, lru_conv, lru_h, ssd_conv, ssd_h,
                 g_mix, w_in, lru_conv_w, lru_conv_b, w_a, b_a, w_x, b_x, lam, g_lru_out,
                 ssd_conv_w, ssd_conv_b, dt_bias, a_log, d_skip, g_ssd_out, w_out,
                 g_mlp, w_up, w_down):
    bsz, t, _ = x.shape
    h = rmsnorm(x, g_mix)
    proj = h @ w_in
    o1 = LRU_WIDTH
    o2 = o1 + LRU_WIDTH
    o3 = o2 + SSD_WIDTH
    o4 = o3 + SSD_CONV_DIM
    lru_x, lru_gate, ssd_z, ssd_xbc, ssd_dt = jnp.split(proj, [o1, o2, o3, o4], axis=-1)

    u, new_lru_conv = causal_conv(lru_x, lru_conv, lru_conv_w, lru_conv_b)
    hseq, new_lru_h = rg_lru(u, lru_h, w_a, b_a, w_x, b_x, lam)
    y_lru = rmsnorm(hseq.astype(x.dtype) * jax.nn.gelu(lru_gate), g_lru_out)

    xbc, new_ssd_conv = causal_conv(ssd_xbc, ssd_conv, ssd_conv_w, ssd_conv_b)
    xbc = jax.nn.silu(xbc)
    xs, bm, cm = jnp.split(xbc, [SSD_WIDTH, SSD_WIDTH + N_SSD_GROUPS * D_STATE], axis=-1)
    xs = xs.reshape(bsz, t, N_SSD_HEADS, SSD_HEAD_DIM).astype(jnp.float32)
    bm = bm.reshape(bsz, t, N_SSD_GROUPS, D_STATE).astype(jnp.float32)
    cm = cm.reshape(bsz, t, N_SSD_GROUPS, D_STATE).astype(jnp.float32)
    dt = jax.nn.softplus(ssd_dt.astype(jnp.float32) + dt_bias.astype(jnp.float32))
    a = -jnp.exp(a_log.astype(jnp.float32))
    ys, new_ssd_h = ssd_scan(xs, dt, a, bm, cm, ssd_h.astype(jnp.float32))
    ys = ys + d_skip.astype(jnp.float32)[:, None] * xs
    ys = ys.reshape(bsz, t, SSD_WIDTH).astype(x.dtype)
    gated = (ys * jax.nn.silu(ssd_z)).reshape(bsz, t, N_SSD_GROUPS, SSD_WIDTH // N_SSD_GROUPS)
    y_ssd = rmsnorm(gated, g_ssd_out.reshape(N_SSD_GROUPS, -1)).reshape(bsz, t, SSD_WIDTH)

    x = x + jnp.concatenate([y_lru, y_ssd], axis=-1) @ w_out
    m = rmsnorm(x, g_mlp)
    x = x + jnp.square(jax.nn.relu(m @ w_up)) @ w_down
    dt_out = x.dtype
    return (x, new_lru_conv.astype(dt_out), new_lru_h.astype(dt_out),
            new_ssd_conv.astype(dt_out), new_ssd_h.astype(dt_out))


def setup_inputs(seed: int = 0) -> dict:
    key = jax.random.key(seed)
    ks = jax.random.split(key, 32)
    f32 = jnp.float32
    nrm = lambda k, shape, s: jax.random.normal(k, shape, f32) * s
    gain = lambda k, shape: 1.0 + 0.01 * jax.random.normal(k, shape, f32)
    a0 = jax.random.uniform(ks[10], (DEPTH, LRU_WIDTH), f32, 0.9, 0.999)
    lam = jnp.log(a0) - jnp.log1p(-a0)
    dt0 = jnp.exp(jax.random.uniform(ks[13], (DEPTH, N_SSD_HEADS), f32, math.log(1e-3), math.log(1e-1)))
    dt_bias = dt0 + jnp.log(-jnp.expm1(-dt0))
    a_log = jnp.log(jax.random.uniform(ks[14], (DEPTH, N_SSD_HEADS), f32, 1.0, 16.0))
    return {
        "x_prompt": nrm(ks[0], (BATCH, SEQ, D_MODEL), 1.0),
        "x_sample": nrm(ks[1], (DEC_BATCH, DEC_SEQ, D_MODEL), 1.0),
        "state_lru_conv": nrm(ks[2], (DEPTH, DEC_BATCH, CONV_WIDTH - 1, LRU_WIDTH), 1.0),
        "state_lru_h": nrm(ks[3], (DEPTH, DEC_BATCH, LRU_WIDTH), 0.5),
        "state_ssd_conv": nrm(ks[4], (DEPTH, DEC_BATCH, CONV_WIDTH - 1, SSD_CONV_DIM), 1.0),
        "state_ssd_h": nrm(ks[5], (DEPTH, DEC_BATCH, N_SSD_HEADS, SSD_HEAD_DIM, D_STATE), 0.1),
        "g_mix": gain(ks[6], (DEPTH, D_MODEL)),
        "w_in": nrm(ks[7], (DEPTH, D_MODEL, IN_PROJ_DIM), D_MODEL ** -0.5),
        "lru_conv_w": nrm(ks[8], (DEPTH, CONV_WIDTH, LRU_WIDTH), CONV_WIDTH ** -0.5),
        "lru_conv_b": nrm(ks[9], (DEPTH, LRU_WIDTH), 0.01),
        "w_a": nrm(ks[11], (DEPTH, N_LRU_HEADS, LRU_BLOCK, LRU_BLOCK), LRU_BLOCK ** -0.5),
        "b_a": nrm(ks[12], (DEPTH, N_LRU_HEADS, LRU_BLOCK), 0.01),
        "w_x": nrm(ks[15], (DEPTH, N_LRU_HEADS, LRU_BLOCK, LRU_BLOCK), LRU_BLOCK ** -0.5),
        "b_x": nrm(ks[16], (DEPTH, N_LRU_HEADS, LRU_BLOCK), 0.01),
        "lam": lam,
        "g_lru_out": gain(ks[17], (DEPTH, LRU_WIDTH)),
        "ssd_conv_w": nrm(ks[18], (DEPTH, CONV_WIDTH, SSD_CONV_DIM), CONV_WIDTH ** -0.5),
        "ssd_conv_b": nrm(ks[19], (DEPTH, SSD_CONV_DIM), 0.01),
        "dt_bias": dt_bias,
        "a_log": a_log,
        "d_skip": gain(ks[20], (DEPTH, N_SSD_HEADS)),
        "g_ssd_out": gain(ks[21], (DEPTH, SSD_WIDTH)),
        "w_out": nrm(ks[22], (DEPTH, MIX_WIDTH, D_MODEL), MIX_WIDTH ** -0.5),
        "g_mlp": gain(ks[23], (DEPTH, D_MODEL)),
        "w_up": nrm(ks[24], (DEPTH, D_MODEL, D_FF), D_MODEL ** -0.5),
        "w_down": nrm(ks[25], (DEPTH, D_FF, D_MODEL), D_FF ** -0.5),
        "g_final": gain(ks[26], (D_MODEL,)),
    }


def reference(x_prompt, x_sample, state_lru_conv, state_lru_h, state_ssd_conv, state_ssd_h,
              g_mix, w_in, lru_conv_w, lru_conv_b, w_a, b_a, w_x, b_x, lam, g_lru_out,
              ssd_conv_w, ssd_conv_b, dt_bias, a_log, d_skip, g_ssd_out, w_out,
              g_mlp, w_up, w_down, g_final):
    layer_params = (g_mix, w_in, lru_conv_w, lru_conv_b, w_a, b_a, w_x, b_x, lam, g_lru_out,
                    ssd_conv_w, ssd_conv_b, dt_bias, a_log, d_skip, g_ssd_out, w_out,
                    g_mlp, w_up, w_down)
    bp = x_prompt.shape[0]
    xp, xs = x_prompt, x_sample
    p_lc, p_lh, p_sc, p_sh = [], [], [], []
    s_lc, s_lh, s_sc, s_sh = [], [], [], []
    for l in range(DEPTH):
        lp = [p[l] for p in layer_params]
        xp, a1, a2, a3, a4 = hybrid_layer(
            xp,
            jnp.zeros((bp, CONV_WIDTH - 1, LRU_WIDTH), xp.dtype),
            jnp.zeros((bp, LRU_WIDTH), xp.dtype),
            jnp.zeros((bp, CONV_WIDTH - 1, SSD_CONV_DIM), xp.dtype),
            jnp.zeros((bp, N_SSD_HEADS, SSD_HEAD_DIM, D_STATE), xp.dtype),
            *lp)
        p_lc.append(a1); p_lh.append(a2); p_sc.append(a3); p_sh.append(a4)
        xs, b1, b2, b3, b4 = hybrid_layer(
            xs, state_lru_conv[l], state_lru_h[l], state_ssd_conv[l], state_ssd_h[l], *lp)
        s_lc.append(b1); s_lh.append(b2); s_sc.append(b3); s_sh.append(b4)
    y_prompt = rmsnorm(xp, g_final)
    y_sample = rmsnorm(xs, g_final)
    return (y_prompt, y_sample,
            jnp.stack(p_lc), jnp.stack(p_lh), jnp.stack(p_sc), jnp.stack(p_sh),
            jnp.stack(s_lc), jnp.stack(s_lh), jnp.stack(s_sc), jnp.stack(s_sh))
```

```python
import functools
import math

import jax
import jax.numpy as jnp
from jax import lax
from jax.experimental import pallas as pl
from jax.experimental.pallas import tpu as pltpu

F32 = jnp.float32
BF16 = jnp.bfloat16

D_MODEL = 1024
LRU_WIDTH = 1024
N_LRU_HEADS = 16
LRU_BLOCK = 64
LRU_C = 8.0
SSD_WIDTH = 1024
SSD_HEAD_DIM = 64
N_SSD_HEADS = 16
N_SSD_GROUPS = 2
D_STATE = 128
CONV_WIDTH = 4
SSD_CONV_DIM = SSD_WIDTH + 2 * N_SSD_GROUPS * D_STATE
D_FF = 4 * D_MODEL
EPS = 1e-6

LANES = 128
SUBLANES = 8
MXU_DIM = 256
DT_PAD = LANES
PROJ_MAIN = 2 * LRU_WIDTH + SSD_WIDTH + SSD_CONV_DIM
PROJ_PAD = PROJ_MAIN + DT_PAD
MIX_WIDTH = LRU_WIDTH + SSD_WIDTH
SSD_CHUNK = 128
PROMPT_TC = 256
ROW_TILE = 256
SAMPLE_SEQS = SSD_CHUNK // SUBLANES
NEG_BIG = -1e30
VMEM_LIMIT = 56 * 1024 * 1024
HI = lax.Precision.HIGHEST


def _rms(x, g):
    ms = jnp.mean(x * x, axis=-1, keepdims=True)
    return x * lax.rsqrt(ms + EPS) * g


def _sigmoid(x):
    return 1.0 / (1.0 + jnp.exp(-x))


def _softplus(x):
    return jnp.maximum(x, 0.0) + jnp.log1p(jnp.exp(-jnp.abs(x)))


def _gelu_tanh(x):
    c = math.sqrt(2.0 / math.pi)
    return 0.5 * x * (1.0 + jnp.tanh(c * (x + 0.044715 * (x * x * x))))


def _lru_coeffs(u, wg_ref, b_a, b_x, sp_lam):
    ub = u.astype(BF16)
    r_parts, i_parts = [], []
    for j in range(LRU_WIDTH // MXU_DIM):
        g = jnp.dot(ub[:, MXU_DIM * j:MXU_DIM * (j + 1)], wg_ref[j], preferred_element_type=F32)
        r_parts.append(g[:, :MXU_DIM])
        i_parts.append(g[:, MXU_DIM:])
    r = _sigmoid(jnp.concatenate(r_parts, axis=1) + b_a)
    i = _sigmoid(jnp.concatenate(i_parts, axis=1) + b_x)
    log_a = (-LRU_C) * r * sp_lam
    a = jnp.exp(log_a)
    th = jnp.tanh(-log_a)
    mult = jnp.sqrt(2.0 * th / (1.0 + th))
    return a, mult * (i * u)


def _scan_within_8(a, b):
    rows = a.shape[0]
    ridx = lax.broadcasted_iota(jnp.int32, a.shape, 0) & (SUBLANES - 1)
    for k in (1, 2, 4):
        a_s = pltpu.roll(a, k, axis=0)
        b_s = pltpu.roll(b, k, axis=0)
        m = ridx >= k
        b = jnp.where(m, a * b_s + b, b)
        a = jnp.where(m, a * a_s, a)
    del rows
    return a, b


def _ssd_intra(xs, bm, cm, dt, a_row, tri, mask_add):
    L = xs.shape[0]
    da = dt * a_row
    cum = jnp.dot(tri, da, precision=HI, preferred_element_type=F32)
    cum_t = cum.T
    dt_t = dt.T
    cum_end = jnp.dot(tri, da, precision=HI, preferred_element_type=F32) if False else None
    del cum_end
    lane = lax.broadcasted_iota(jnp.int32, (L, LANES), 1)
    lo = lane < SSD_HEAD_DIM
    y_parts, e_parts, w_parts = [], [], []
    for g in range(N_SSD_GROUPS):
        bg = bm[:, D_STATE * g:D_STATE * (g + 1)].astype(BF16)
        cg = cm[:, D_STATE * g:D_STATE * (g + 1)].astype(BF16)
        cb = lax.dot_general(cg, bg, (((1,), (1,)), ((), ())), preferred_element_type=F32)
        for jj in range(N_SSD_HEADS // N_SSD_GROUPS // 2):
            j = (N_SSD_HEADS // N_SSD_GROUPS // 2) * g + jj
            h0, h1 = 2 * j, 2 * j + 1
            col0 = jnp.broadcast_to(cum[:, h0:h0 + 1], (L, LANES))
            col1 = jnp.broadcast_to(cum[:, h1:h1 + 1], (L, LANES))
            m0 = cb * jnp.exp(col0 - cum_t[h0:h0 + 1, :] + mask_add) * dt_t[h0:h0 + 1, :]
            m1 = cb * jnp.exp(col1 - cum_t[h1:h1 + 1, :] + mask_add) * dt_t[h1:h1 + 1, :]
            lhs = jnp.concatenate([m0, m1], axis=1).astype(BF16)
            xp = xs[:, LANES * j:LANES * (j + 1)]
            rhs = jnp.concatenate([jnp.where(lo, xp, 0.0), jnp.where(lo, 0.0, xp)], axis=0).astype(BF16)
            y_parts.append(jnp.dot(lhs, rhs, preferred_element_type=F32))
            e_parts.append(jnp.where(lo, jnp.exp(col0), jnp.exp(col1)))
            w_parts.append(jnp.where(lo, col0, col1))
    y_diag = jnp.concatenate(y_parts, axis=1)
    ecol = jnp.concatenate(e_parts, axis=1)
    colx = jnp.concatenate(w_parts, axis=1)
    return y_diag, ecol, colx, cum, cum_t


def _expand_heads(v, L):
    lane = lax.broadcasted_iota(jnp.int32, (L, LANES), 1)
    lo = lane < SSD_HEAD_DIM
    parts = []
    for j in range(N_SSD_HEADS // 2):
        c0 = jnp.broadcast_to(v[:, 2 * j:2 * j + 1], (L, LANES))
        c1 = jnp.broadcast_to(v[:, 2 * j + 1:2 * j + 2], (L, LANES))
        parts.append(jnp.where(lo, c0, c1))
    return jnp.concatenate(parts, axis=1)


def _ssd_gate_norm(ys, xs, z, dskip, g_ssd):
    ys = ys + dskip * xs
    gated = ys * (z * _sigmoid(z))
    half = SSD_WIDTH // N_SSD_GROUPS
    outs = []
    for g in range(N_SSD_GROUPS):
        outs.append(_rms(gated[:, half * g:half * (g + 1)], g_ssd[:, half * g:half * (g + 1)]))
    return jnp.concatenate(outs, axis=1)


def _inproj_kernel(x_ref, g_ref, w_ref, o_ref):
    hn = _rms(x_ref[...], g_ref[...]).astype(BF16)
    o_ref[...] = jnp.dot(hn, w_ref[...], preferred_element_type=F32)


def _in_proj(x2d, g_mix, w_in_p):
    n = x2d.shape[0]
    return pl.pallas_call(
        _inproj_kernel,
        out_shape=jax.ShapeDtypeStruct((n, PROJ_PAD), F32),
        grid=(n // ROW_TILE,),
        in_specs=[
            pl.BlockSpec((ROW_TILE, D_MODEL), lambda i: (i, 0)),
            pl.BlockSpec((1, D_MODEL), lambda i: (0, 0)),
            pl.BlockSpec((D_MODEL, PROJ_PAD), lambda i: (0, 0), pipeline_mode=pl.Buffered(1)),
        ],
        out_specs=pl.BlockSpec((ROW_TILE, PROJ_PAD), lambda i: (i, 0)),
        compiler_params=pltpu.CompilerParams(
            dimension_semantics=("parallel",), vmem_limit_bytes=VMEM_LIMIT),
        name="in_proj",
    )(x2d, g_mix, w_in_p)


def _outmlp_kernel(x_ref, y_ref, wo_ref, gm_ref, wu_ref, wd_ref, gf_ref, o_ref):
    x1 = x_ref[...] + jnp.dot(y_ref[...].astype(BF16), wo_ref[...], preferred_element_type=F32)
    m = _rms(x1, gm_ref[...]).astype(BF16)
    u = jnp.dot(m, wu_ref[...], preferred_element_type=F32)
    u = jnp.square(jnp.maximum(u, 0.0)).astype(BF16)
    x2 = x1 + jnp.dot(u, wd_ref[...], preferred_element_type=F32)
    o_ref[...] = _rms(x2, gf_ref[...])


def _out_mlp(x2d, ymix2d, w_out_b, g_mlp, w_up_b, w_down_b, g_final):
    n = x2d.shape[0]
    const = lambda i: (0, 0)
    return pl.pallas_call(
        _outmlp_kernel,
        out_shape=jax.ShapeDtypeStruct((n, D_MODEL), F32),
        grid=(n // ROW_TILE,),
        in_specs=[
            pl.BlockSpec((ROW_TILE, D_MODEL), lambda i: (i, 0)),
            pl.BlockSpec((ROW_TILE, MIX_WIDTH), lambda i: (i, 0)),
            pl.BlockSpec((MIX_WIDTH, D_MODEL), const, pipeline_mode=pl.Buffered(1)),
            pl.BlockSpec((1, D_MODEL), const),
            pl.BlockSpec((D_MODEL, D_FF), const, pipeline_mode=pl.Buffered(1)),
            pl.BlockSpec((D_FF, D_MODEL), const, pipeline_mode=pl.Buffered(1)),
            pl.BlockSpec((1, D_MODEL), const),
        ],
        out_specs=pl.BlockSpec((ROW_TILE, D_MODEL), lambda i: (i, 0)),
        compiler_params=pltpu.CompilerParams(
            dimension_semantics=("parallel",), vmem_limit_bytes=VMEM_LIMIT),
        name="out_mlp",
    )(x2d, ymix2d, w_out_b, g_mlp, w_up_b, w_down_b, g_final)


def _mixer_prompt_kernel(lx_ref, gate_ref, z_ref, xbc_ref, dt_ref,
                         lcw_ref, lcb_ref, wg_ref, ba_ref, bx_ref, lam_ref, glru_ref,
                         scw_ref, scb_ref, dtb_ref, alog_ref, dskip_ref, gssd_ref,
                         y_ref, olc_ref, olh_ref, osc_ref, osh_ref,
                         ext_l, ext_s, a_scr, b_scr, hcar, ht):
    t = pl.program_id(1)
    nt = pl.num_programs(1)
    tc = PROMPT_TC
    hist = SUBLANES

    @pl.when(t == 0)
    def _init():
        ext_l[0:hist, :] = jnp.zeros((hist, LRU_WIDTH), F32)
        ext_s[0:hist, :] = jnp.zeros((hist, SSD_CONV_DIM), F32)
        hcar[...] = jnp.zeros_like(hcar)
        ht[...] = jnp.zeros_like(ht)

    ext_l[hist:hist + tc, :] = lx_ref[...]
    ext_s[hist:hist + tc, :] = xbc_ref[...]

    u = lcb_ref[...]
    for k in range(CONV_WIDTH):
        u = u + ext_l[pl.ds(hist - (CONV_WIDTH - 1) + k, tc), :] * lcw_ref[k:k + 1, :]
    sp_lam = _softplus(-lam_ref[...])
    a, b = _lru_coeffs(u, wg_ref, ba_ref[...], bx_ref[...], sp_lam)
    a, b = _scan_within_8(a, b)
    a_scr[...] = a
    b_scr[...] = b

    def carry_step(i, carry):
        r0 = pl.multiple_of(i * SUBLANES, SUBLANES)
        h = a_scr[pl.ds(r0, SUBLANES), :] * carry + b_scr[pl.ds(r0, SUBLANES), :]
        b_scr[pl.ds(r0, SUBLANES), :] = h
        return jnp.broadcast_to(h[SUBLANES - 1:SUBLANES, :], (SUBLANES, LRU_WIDTH))

    carry = lax.fori_loop(0, tc // SUBLANES, carry_step, hcar[...], unroll=8)
    hcar[...] = carry
    y_ref[:, 0:LRU_WIDTH] = _rms(b_scr[...] * _gelu_tanh(gate_ref[...]), glru_ref[...])

    xbc = scb_ref[...]
    for k in range(CONV_WIDTH):
        xbc = xbc + ext_s[pl.ds(hist - (CONV_WIDTH - 1) + k, tc), :] * scw_ref[k:k + 1, :]
    xbc = xbc * _sigmoid(xbc)
    dt_all = _softplus(dt_ref[...] + dtb_ref[...])
    lane1 = lax.broadcasted_iota(jnp.int32, (1, LANES), 1)
    a_row = jnp.where(lane1 < N_SSD_HEADS, -jnp.exp(alog_ref[...]), 0.0)

    L = SSD_CHUNK
    rr = lax.broadcasted_iota(jnp.int32, (L, L), 0)
    cc = lax.broadcasted_iota(jnp.int32, (L, L), 1)
    causal = cc <= rr
    tri = jnp.where(causal, 1.0, 0.0).astype(F32)
    mask_add = jnp.where(causal, 0.0, NEG_BIG).astype(F32)

    for c in range(tc // L):
        rows = slice(L * c, L * (c + 1))
        xs = xbc[rows, 0:SSD_WIDTH]
        bm = xbc[rows, SSD_WIDTH:SSD_WIDTH + N_SSD_GROUPS * D_STATE]
        cm = xbc[rows, SSD_WIDTH + N_SSD_GROUPS * D_STATE:]
        dt = dt_all[rows, :]
        y_diag, ecol, colx, cum, _ = _ssd_intra(xs, bm, cm, dt, a_row, tri, mask_add)
        dtx = _expand_heads(dt, L)
        end_row = colx[L - 1:L, :]
        xw = xs * (jnp.exp(end_row - colx) * dtx)
        dec = ecol[L - 1:L, :]
        y_off_parts = []
        half = SSD_WIDTH // N_SSD_GROUPS
        for g in range(N_SSD_GROUPS):
            htg = ht[g]
            cg = cm[:, D_STATE * g:D_STATE * (g + 1)].astype(BF16)
            y_off_parts.append(jnp.dot(cg, htg.astype(BF16), preferred_element_type=F32))
            bg_t = bm[:, D_STATE * g:D_STATE * (g + 1)].T.astype(BF16)
            st = jnp.dot(bg_t, xw[:, half * g:half * (g + 1)].astype(BF16), preferred_element_type=F32)
            ht[g] = htg * dec[:, half * g:half * (g + 1)] + st
        ys = y_diag + jnp.concatenate(y_off_parts, axis=1) * ecol
        y_ref[rows, LRU_WIDTH:MIX_WIDTH] = _ssd_gate_norm(
            ys, xs, z_ref[rows, :], dskip_ref[...], gssd_ref[...])

    @pl.when(t == nt - 1)
    def _final():
        olc_ref[...] = ext_l[hist + tc - (CONV_WIDTH - 1):hist + tc, :]
        osc_ref[...] = ext_s[hist + tc - (CONV_WIDTH - 1):hist + tc, :]
        olh_ref[...] = hcar[0:1, :]
        half = SSD_WIDTH // N_SSD_GROUPS
        for g in range(N_SSD_GROUPS):
            osh_ref[half * g:half * (g + 1), :] = ht[g].T

    tail_l = ext_l[tc:tc + hist, :]
    tail_s = ext_s[tc:tc + hist, :]
    ext_l[0:hist, :] = tail_l
    ext_s[0:hist, :] = tail_s


def _param_specs(const):
    return [
        pl.BlockSpec((CONV_WIDTH, LRU_WIDTH), const),
        pl.BlockSpec((1, LRU_WIDTH), const),
        pl.BlockSpec((LRU_WIDTH // MXU_DIM, MXU_DIM, 2 * MXU_DIM), lambda *_: (0, 0, 0)),
        pl.BlockSpec((1, LRU_WIDTH), const),
        pl.BlockSpec((1, LRU_WIDTH), const),
        pl.BlockSpec((1, LRU_WIDTH), const),
        pl.BlockSpec((1, LRU_WIDTH), const),
        pl.BlockSpec((CONV_WIDTH, SSD_CONV_DIM), const),
        pl.BlockSpec((1, SSD_CONV_DIM), const),
        pl.BlockSpec((1, DT_PAD), const),
        pl.BlockSpec((1, DT_PAD), const),
        pl.BlockSpec((1, SSD_WIDTH), const),
        pl.BlockSpec((1, SSD_WIDTH), const),
    ]


def _mixer_prompt(proj, params):
    bsz, seq, _ = proj.shape
    tc = PROMPT_TC
    const = lambda b, t: (0, 0)
    in_specs = [
        pl.BlockSpec((None, tc, LRU_WIDTH), lambda b, t: (b, t, 0)),
        pl.BlockSpec((None, tc, LRU_WIDTH), lambda b, t: (b, t, 1)),
        pl.BlockSpec((None, tc, SSD_WIDTH), lambda b, t: (b, t, 2)),
        pl.BlockSpec((None, tc, SSD_CONV_DIM), lambda b, t: (b, t, 2)),
        pl.BlockSpec((None, tc, DT_PAD), lambda b, t: (b, t, PROJ_MAIN // DT_PAD)),
    ] + _param_specs(const)
    out_shape = (
        jax.ShapeDtypeStruct((bsz, seq, MIX_WIDTH), F32),
        jax.ShapeDtypeStruct((bsz, CONV_WIDTH - 1, LRU_WIDTH), F32),
        jax.ShapeDtypeStruct((bsz, 1, LRU_WIDTH), F32),
        jax.ShapeDtypeStruct((bsz, CONV_WIDTH - 1, SSD_CONV_DIM), F32),
        jax.ShapeDtypeStruct((bsz, SSD_WIDTH, D_STATE), F32),
    )
    out_specs = (
        pl.BlockSpec((None, tc, MIX_WIDTH), lambda b, t: (b, t, 0)),
        pl.BlockSpec((None, CONV_WIDTH - 1, LRU_WIDTH), lambda b, t: (b, 0, 0)),
        pl.BlockSpec((None, 1, LRU_WIDTH), lambda b, t: (b, 0, 0)),
        pl.BlockSpec((None, CONV_WIDTH - 1, SSD_CONV_DIM), lambda b, t: (b, 0, 0)),
        pl.BlockSpec((None, SSD_WIDTH, D_STATE), lambda b, t: (b, 0, 0)),
    )
    scratch = [
        pltpu.VMEM((SUBLANES + tc, LRU_WIDTH), F32),
        pltpu.VMEM((SUBLANES + tc, SSD_CONV_DIM), F32),
        pltpu.VMEM((tc, LRU_WIDTH), F32),
        pltpu.VMEM((tc, LRU_WIDTH), F32),
        pltpu.VMEM((SUBLANES, LRU_WIDTH), F32),
        pltpu.VMEM((N_SSD_GROUPS, D_STATE, SSD_WIDTH // N_SSD_GROUPS), F32),
    ]
    return pl.pallas_call(
        _mixer_prompt_kernel,
        out_shape=out_shape,
        grid=(bsz, seq // tc),
        in_specs=in_specs,
        out_specs=out_specs,
        scratch_shapes=scratch,
        compiler_params=pltpu.CompilerParams(
            dimension_semantics=("parallel", "arbitrary"), vmem_limit_bytes=VMEM_LIMIT),
        name="mixer_prompt",
    )(proj, proj, proj, proj, proj, *params)


def _mixer_sample_kernel(lx_ref, gate_ref, z_ref, xbc_ref, dt_ref,
                         slc_ref, slh_ref, ssc_ref, ssh_ref,
                         lcw_ref, lcb_ref, wg_ref, ba_ref, bx_ref, lam_ref, glru_ref,
                         scw_ref, scb_ref, dtb_ref, alog_ref, dskip_ref, gssd_ref,
                         y_ref, olc_ref, olh_ref, osc_ref, osh_ref,
                         ext_l, ext_s, pad_scr, yoff_scr):
    S = SAMPLE_SEQS
    P = SUBLANES
    T = lx_ref.shape[1]
    K1 = CONV_WIDTH - 1
    R = S * P
    row_i = lax.broadcasted_iota(jnp.int32, (R, 1), 0) & (P - 1)
    valid = row_i < T

    def pad_rows(ref):
        width = ref.shape[-1]
        pad_scr[:, :, 0:width] = jnp.zeros((S, P, width), F32)
        pad_scr[:, 0:T, 0:width] = ref[...]
        return pad_scr[:, :, 0:width].reshape(R, width)

    ext_l[...] = jnp.zeros_like(ext_l)
    ext_s[...] = jnp.zeros_like(ext_s)
    ext_l[:, 0:K1, :] = slc_ref[...]
    ext_l[:, K1:K1 + T, :] = lx_ref[...]
    ext_s[:, 0:K1, :] = ssc_ref[...]
    ext_s[:, K1:K1 + T, :] = xbc_ref[...]
    olc_ref[...] = ext_l[:, T:T + K1, :]
    osc_ref[...] = ext_s[:, T:T + K1, :]

    el = ext_l[...].reshape(R, LRU_WIDTH)
    es = ext_s[...].reshape(R, SSD_CONV_DIM)

    def conv(e, w_ref, b_ref):
        out = b_ref[...] + e * w_ref[0:1, :]
        for k in range(1, CONV_WIDTH):
            out = out + pltpu.roll(e, R - k, axis=0) * w_ref[k:k + 1, :]
        return out

    u = conv(el, lcw_ref, lcb_ref)
    sp_lam = _softplus(-lam_ref[...])
    a, b = _lru_coeffs(u, wg_ref, ba_ref[...], bx_ref[...], sp_lam)
    a, b = _scan_within_8(a, b)
    h0 = jnp.broadcast_to(slh_ref[...], (S, P, LRU_WIDTH)).reshape(R, LRU_WIDTH)
    hseq = a * h0 + b
    olh_ref[...] = hseq.reshape(S, P, LRU_WIDTH)[:, T - 1:T, :]
    gate = pad_rows(gate_ref)
    y_lru = _rms(hseq * _gelu_tanh(gate), glru_ref[...])

    xbc = conv(es, scw_ref, scb_ref)
    xbc = xbc * _sigmoid(xbc)
    xs = xbc[:, 0:SSD_WIDTH]
    bm = xbc[:, SSD_WIDTH:SSD_WIDTH + N_SSD_GROUPS * D_STATE]
    cm = xbc[:, SSD_WIDTH + N_SSD_GROUPS * D_STATE:]
    dt_raw = pad_rows(dt_ref)
    dt = jnp.where(valid, _softplus(dt_raw + dtb_ref[...]), 0.0)
    lane1 = lax.broadcasted_iota(jnp.int32, (1, LANES), 1)
    a_row = jnp.where(lane1 < N_SSD_HEADS, -jnp.exp(alog_ref[...]), 0.0)

    rr = lax.broadcasted_iota(jnp.int32, (R, R), 0)
    cc = lax.broadcasted_iota(jnp.int32, (R, R), 1)
    allowed = (cc <= rr) & ((rr - cc) <= (rr & (P - 1)))
    tri = jnp.where(allowed, 1.0, 0.0).astype(F32)
    mask_add = jnp.where(allowed, 0.0, NEG_BIG).astype(F32)

    y_diag, ecol, colx, cum, cum_t = _ssd_intra(xs, bm, cm, dt, a_row, tri, mask_add)
    dtx = _expand_heads(dt, R)
    colx3 = colx.reshape(S, P, SSD_WIDTH)
    end_rows = jnp.broadcast_to(colx3[:, P - 1:P, :], (S, P, SSD_WIDTH)).reshape(R, SSD_WIDTH)
    xw = xs * (jnp.exp(end_rows - colx) * dtx)
    ecum_t = jnp.exp(cum_t[0:N_SSD_HEADS, :])

    half = SSD_WIDTH // N_SSD_GROUPS
    for q in range(S):
        r0 = P * q
        vq = jnp.broadcast_to(ecum_t[:, r0 + P - 1:r0 + P], (N_SSD_HEADS, LANES))
        for g in range(N_SSD_GROUPS):
            hqg = ssh_ref[q, half * g:half * (g + 1), :]
            cq = cm[r0:r0 + P, D_STATE * g:D_STATE * (g + 1)].astype(BF16)
            yoff_scr[r0:r0 + P, half * g:half * (g + 1)] = lax.dot_general(
                cq, hqg.astype(BF16), (((1,), (1,)), ((), ())), preferred_element_type=F32)
            bq = bm[r0:r0 + P, D_STATE * g:D_STATE * (g + 1)].astype(BF16)
            xq = xw[r0:r0 + P, half * g:half * (g + 1)].astype(BF16)
            st = lax.dot_general(xq, bq, (((0,), (0,)), ((), ())), preferred_element_type=F32)
            for e in range(N_SSD_HEADS // N_SSD_GROUPS):
                h = (N_SSD_HEADS // N_SSD_GROUPS) * g + e
                lo_r = SSD_HEAD_DIM * e
                osh_ref[q, SSD_HEAD_DIM * h:SSD_HEAD_DIM * (h + 1), :] = (
                    vq[h:h + 1, :] * hqg[lo_r:lo_r + SSD_HEAD_DIM, :] + st[lo_r:lo_r + SSD_HEAD_DIM, :])

    ys = y_diag + yoff_scr[...] * ecol
    z = pad_rows(z_ref)
    y_ssd = _ssd_gate_norm(ys, xs, z, dskip_ref[...], gssd_ref[...])
    y_ref[:, :, 0:LRU_WIDTH] = y_lru.reshape(S, P, LRU_WIDTH)[:, 0:T, :]
    y_ref[:, :, LRU_WIDTH:MIX_WIDTH] = y_ssd.reshape(S, P, SSD_WIDTH)[:, 0:T, :]


def _mixer_sample(proj, st_lc, st_lh, st_sc, st_sh, params):
    nseq, T, _ = proj.shape
    S = SAMPLE_SEQS
    const = lambda i: (0, 0)
    in_specs = [
        pl.BlockSpec((S, T, LRU_WIDTH), lambda i: (i, 0, 0)),
        pl.BlockSpec((S, T, LRU_WIDTH), lambda i: (i, 0, 1)),
        pl.BlockSpec((S, T, SSD_WIDTH), lambda i: (i, 0, 2)),
        pl.BlockSpec((S, T, SSD_CONV_DIM), lambda i: (i, 0, 2)),
        pl.BlockSpec((S, T, DT_PAD), lambda i: (i, 0, PROJ_MAIN // DT_PAD)),
        pl.BlockSpec((S, CONV_WIDTH - 1, LRU_WIDTH), lambda i: (i, 0, 0)),
        pl.BlockSpec((S, 1, LRU_WIDTH), lambda i: (i, 0, 0)),
        pl.BlockSpec((S, CONV_WIDTH - 1, SSD_CONV_DIM), lambda i: (i, 0, 0)),
        pl.BlockSpec((S, SSD_WIDTH, D_STATE), lambda i: (i, 0, 0)),
    ] + _param_specs(const)
    out_shape = (
        jax.ShapeDtypeStruct((nseq, T, MIX_WIDTH), F32),
        jax.ShapeDtypeStruct((nseq, CONV_WIDTH - 1, LRU_WIDTH), F32),
        jax.ShapeDtypeStruct((nseq, 1, LRU_WIDTH), F32),
        jax.ShapeDtypeStruct((nseq, CONV_WIDTH - 1, SSD_CONV_DIM), F32),
        jax.ShapeDtypeStruct((nseq, SSD_WIDTH, D_STATE), F32),
    )
    out_specs = (
        pl.BlockSpec((S, T, MIX_WIDTH), lambda i: (i, 0, 0)),
        pl.BlockSpec((S, CONV_WIDTH - 1, LRU_WIDTH), lambda i: (i, 0, 0)),
        pl.BlockSpec((S, 1, LRU_WIDTH), lambda i: (i, 0, 0)),
        pl.BlockSpec((S, CONV_WIDTH - 1, SSD_CONV_DIM), lambda i: (i, 0, 0)),
        pl.BlockSpec((S, SSD_WIDTH, D_STATE), lambda i: (i, 0, 0)),
    )
    scratch = [
        pltpu.VMEM((S, SUBLANES, LRU_WIDTH), F32),
        pltpu.VMEM((S, SUBLANES, SSD_CONV_DIM), F32),
        pltpu.VMEM((S, SUBLANES, LRU_WIDTH), F32),
        pltpu.VMEM((S * SUBLANES, SSD_WIDTH), F32),
    ]
    return pl.pallas_call(
        _mixer_sample_kernel,
        out_shape=out_shape,
        grid=(nseq // S,),
        in_specs=in_specs,
        out_specs=out_specs,
        scratch_shapes=scratch,
        compiler_params=pltpu.CompilerParams(
            dimension_semantics=("parallel",), vmem_limit_bytes=VMEM_LIMIT),
        name="mixer_sample",
    )(proj, proj, proj, proj, proj, st_lc, st_lh, st_sc, st_sh, *params)


def _gate_weights(w_a, w_x):
    def tiles(w):
        per = MXU_DIM // LRU_BLOCK
        w4 = w.reshape(N_LRU_HEADS // per, per, LRU_BLOCK, LRU_BLOCK)
        eye = jnp.eye(per, dtype=w.dtype)
        t = jnp.einsum('jaik,ab->jaibk', w4, eye)
        return t.reshape(N_LRU_HEADS // per, MXU_DIM, MXU_DIM)
    return jnp.concatenate([tiles(w_a), tiles(w_x)], axis=2).astype(BF16)


def kernel(x_prompt, x_sample, state_lru_conv, state_lru_h, state_ssd_conv, state_ssd_h, g_mix, w_in,
           lru_conv_w, lru_conv_b, w_a, b_a, w_x, b_x, lam, g_lru_out, ssd_conv_w, ssd_conv_b, dt_bias,
           a_log, d_skip, g_ssd_out, w_out, g_mlp, w_up, w_down, g_final):
    depth = w_in.shape[0]
    assert depth == 1
    bp, seq, _ = x_prompt.shape
    bs, dseq, _ = x_sample.shape
    l = 0
    row = lambda v: v.reshape(1, -1)
    w_in_p = jnp.pad(w_in[l], ((0, 0), (0, PROJ_PAD - w_in.shape[2]))).astype(BF16)
    params = (
        lru_conv_w[l], row(lru_conv_b[l]), _gate_weights(w_a[l], w_x[l]),
        row(b_a[l]), row(b_x[l]), row(lam[l]), row(g_lru_out[l]),
        ssd_conv_w[l], row(ssd_conv_b[l]),
        jnp.pad(row(dt_bias[l]), ((0, 0), (0, DT_PAD - N_SSD_HEADS))),
        jnp.pad(row(a_log[l]), ((0, 0), (0, DT_PAD - N_SSD_HEADS))),
        row(jnp.repeat(d_skip[l], SSD_HEAD_DIM)), row(g_ssd_out[l]),
    )
    w_out_b = w_out[l].astype(BF16)
    w_up_b = w_up[l].astype(BF16)
    w_down_b = w_down[l].astype(BF16)
    gmix = row(g_mix[l])
    gmlp = row(g_mlp[l])
    gfin = row(g_final)

    xp2 = x_prompt.reshape(bp * seq, D_MODEL)
    proj_p = _in_proj(xp2, gmix, w_in_p).reshape(bp, seq, PROJ_PAD)
    ymix_p, p_lc, p_lh, p_sc, p_sh = _mixer_prompt(proj_p, params)
    y_prompt = _out_mlp(xp2, ymix_p.reshape(bp * seq, MIX_WIDTH), w_out_b, gmlp, w_up_b, w_down_b, gfin)

    xs2 = x_sample.reshape(bs * dseq, D_MODEL)
    proj_s = _in_proj(xs2, gmix, w_in_p).reshape(bs, dseq, PROJ_PAD)
    ymix_s, s_lc, s_lh, s_sc, s_sh = _mixer_sample(
        proj_s, state_lru_conv[l], state_lru_h[l].reshape(bs, 1, LRU_WIDTH), state_ssd_conv[l],
        state_ssd_h[l].reshape(bs, SSD_WIDTH, D_STATE), params)
    y_sample = _out_mlp(xs2, ymix_s.reshape(bs * dseq, MIX_WIDTH), w_out_b, gmlp, w_up_b, w_down_b, gfin)

    hshape = (N_SSD_HEADS, SSD_HEAD_DIM, D_STATE)
    return (
        y_prompt.reshape(bp, seq, D_MODEL), y_sample.reshape(bs, dseq, D_MODEL),
        p_lc[None], p_lh.reshape(1, bp, LRU_WIDTH), p_sc[None], p_sh.reshape(1, bp, *hshape),
        s_lc[None], s_lh.reshape(1, bs, LRU_WIDTH), s_sc[None], s_sh.reshape(1, bs, *hshape),
    )
```

```python
import functools
import math

import jax
import jax.numpy as jnp
from jax import lax
from jax.experimental import pallas as pl
from jax.experimental.pallas import tpu as pltpu

F32 = jnp.float32
BF16 = jnp.bfloat16

D_MODEL = 1024
LRU_WIDTH = 1024
N_LRU_HEADS = 16
LRU_BLOCK = 64
LRU_C = 8.0
SSD_WIDTH = 1024
SSD_HEAD_DIM = 64
N_SSD_HEADS = 16
N_SSD_GROUPS = 2
D_STATE = 128
CONV_WIDTH = 4
SSD_CONV_DIM = SSD_WIDTH + 2 * N_SSD_GROUPS * D_STATE
D_FF = 4 * D_MODEL
EPS = 1e-6

LANES = 128
SUBLANES = 8
MXU_DIM = 256
DT_PAD = LANES
PROJ_MAIN = 2 * LRU_WIDTH + SSD_WIDTH + SSD_CONV_DIM
PROJ_PAD = PROJ_MAIN + DT_PAD
MIX_WIDTH = LRU_WIDTH + SSD_WIDTH
SSD_CHUNK = 128
PROMPT_TC = 256
ROW_TILE = 256
SAMPLE_SEQS = SSD_CHUNK // SUBLANES
SCAN_RUN = PROMPT_TC // SUBLANES
SCAN_PITCH = SCAN_RUN + 4
NEG_BIG = -1e30
VMEM_LIMIT = 56 * 1024 * 1024
HI = lax.Precision.HIGHEST


def _rms(x, g):
    ms = jnp.mean(x * x, axis=-1, keepdims=True)
    return x * lax.rsqrt(ms + EPS) * g


def _sigmoid(x):
    return 1.0 / (1.0 + jnp.exp(-x))


def _softplus(x):
    return jnp.maximum(x, 0.0) + jnp.log1p(jnp.exp(-jnp.abs(x)))


def _gelu_tanh(x):
    c = math.sqrt(2.0 / math.pi)
    return 0.5 * x * (1.0 + jnp.tanh(c * (x + 0.044715 * (x * x * x))))


def _lru_coeffs(u, wg_ref, b_a, b_x, sp_lam):
    ub = u.astype(BF16)
    r_parts, i_parts = [], []
    for j in range(LRU_WIDTH // MXU_DIM):
        g = jnp.dot(ub[:, MXU_DIM * j:MXU_DIM * (j + 1)], wg_ref[j], preferred_element_type=F32)
        r_parts.append(g[:, :MXU_DIM])
        i_parts.append(g[:, MXU_DIM:])
    r = _sigmoid(jnp.concatenate(r_parts, axis=1) + b_a)
    i = _sigmoid(jnp.concatenate(i_parts, axis=1) + b_x)
    log_a = (-LRU_C) * r * sp_lam
    a = jnp.exp(log_a)
    th = jnp.tanh(-log_a)
    mult = jnp.sqrt(2.0 * th / (1.0 + th))
    return a, mult * (i * u)


def _scan_within_8(a, b):
    ridx = lax.broadcasted_iota(jnp.int32, a.shape, 0) & (SUBLANES - 1)
    for k in (1, 2, 4):
        a_s = pltpu.roll(a, k, axis=0)
        b_s = pltpu.roll(b, k, axis=0)
        m = ridx >= k
        b = jnp.where(m, a * b_s + b, b)
        a = jnp.where(m, a * a_s, a)
    return a, b


def _conv_slabs(ext, w_ref, b_ref, rows, first):
    parts = []
    for s in range(ext.shape[0]):
        cols = slice(LANES * s, LANES * (s + 1))
        acc = b_ref[:, cols] + ext[s, pl.ds(first, rows), :] * w_ref[0:1, cols]
        for k in range(1, CONV_WIDTH):
            acc = acc + ext[s, pl.ds(first + k, rows), :] * w_ref[k:k + 1, cols]
        parts.append(acc)
    return jnp.concatenate(parts, axis=1)


def _lru_scan_strided(a, b, hcar, a_pad, b_pad, h_pad):
    rows = a.shape[0]
    S = rows // SUBLANES
    ridx = lax.broadcasted_iota(jnp.int32, (SUBLANES, LANES), 0)
    outs = []
    for s in range(LRU_WIDTH // LANES):
        cols = slice(LANES * s, LANES * (s + 1))
        for j in range(SUBLANES):
            a_pad[s, SCAN_PITCH * j:SCAN_PITCH * j + S, :] = a[S * j:S * (j + 1), cols]
            b_pad[s, SCAN_PITCH * j:SCAN_PITCH * j + S, :] = b[S * j:S * (j + 1), cols]
        h = jnp.zeros((SUBLANES, LANES), F32)
        prod = jnp.ones((SUBLANES, LANES), F32)
        for i in range(S):
            av = a_pad[s, pl.ds(i, SUBLANES, stride=SCAN_PITCH), :]
            h = av * h + b_pad[s, pl.ds(i, SUBLANES, stride=SCAN_PITCH), :]
            prod = av * prod
        pcum, hcum = _scan_within_8(prod, h)
        cin = hcar[:, cols]
        ends = hcum + pcum * cin
        h = jnp.where(ridx == 0, cin, pltpu.roll(ends, 1, axis=0))
        for i in range(S):
            av = a_pad[s, pl.ds(i, SUBLANES, stride=SCAN_PITCH), :]
            h = av * h + b_pad[s, pl.ds(i, SUBLANES, stride=SCAN_PITCH), :]
            h_pad[s, pl.ds(i, SUBLANES, stride=SCAN_PITCH), :] = h
        hcar[:, cols] = jnp.broadcast_to(ends[SUBLANES - 1:SUBLANES, :], (SUBLANES, LANES))
        outs.append(jnp.concatenate(
            [h_pad[s, SCAN_PITCH * j:SCAN_PITCH * j + S, :] for j in range(SUBLANES)], axis=0))
    return jnp.concatenate(outs, axis=1)


def _ssd_intra(xs, bm, cm, dt, a_row, tri, mask_add):
    L = xs.shape[0]
    da = dt * a_row
    cum = jnp.dot(tri, da, precision=HI, preferred_element_type=F32)
    cum_t = cum.T
    dt_t = dt.T
    lane = lax.broadcasted_iota(jnp.int32, (L, LANES), 1)
    lo = lane < SSD_HEAD_DIM
    y_parts, e_parts, w_parts = [], [], []
    for g in range(N_SSD_GROUPS):
        bg = bm[:, D_STATE * g:D_STATE * (g + 1)].astype(BF16)
        cg = cm[:, D_STATE * g:D_STATE * (g + 1)].astype(BF16)
        cb = lax.dot_general(cg, bg, (((1,), (1,)), ((), ())), preferred_element_type=F32)
        for jj in range(N_SSD_HEADS // N_SSD_GROUPS // 2):
            j = (N_SSD_HEADS // N_SSD_GROUPS // 2) * g + jj
            h0, h1 = 2 * j, 2 * j + 1
            col0 = jnp.broadcast_to(cum[:, h0:h0 + 1], (L, LANES))
            col1 = jnp.broadcast_to(cum[:, h1:h1 + 1], (L, LANES))
            m0 = cb * jnp.exp(col0 - cum_t[h0:h0 + 1, :] + mask_add) * dt_t[h0:h0 + 1, :]
            m1 = cb * jnp.exp(col1 - cum_t[h1:h1 + 1, :] + mask_add) * dt_t[h1:h1 + 1, :]
            lhs = jnp.concatenate([m0, m1], axis=1).astype(BF16)
            xp = xs[:, LANES * j:LANES * (j + 1)]
            rhs = jnp.concatenate([jnp.where(lo, xp, 0.0), jnp.where(lo, 0.0, xp)], axis=0).astype(BF16)
            y_parts.append(jnp.dot(lhs, rhs, preferred_element_type=F32))
            e_parts.append(jnp.where(lo, jnp.exp(col0), jnp.exp(col1)))
            w_parts.append(jnp.where(lo, col0, col1))
    y_diag = jnp.concatenate(y_parts, axis=1)
    ecol = jnp.concatenate(e_parts, axis=1)
    colx = jnp.concatenate(w_parts, axis=1)
    return y_diag, ecol, colx, cum, cum_t


def _expand_heads(v, L):
    lane = lax.broadcasted_iota(jnp.int32, (L, LANES), 1)
    lo = lane < SSD_HEAD_DIM
    parts = []
    for j in range(N_SSD_HEADS // 2):
        c0 = jnp.broadcast_to(v[:, 2 * j:2 * j + 1], (L, LANES))
        c1 = jnp.broadcast_to(v[:, 2 * j + 1:2 * j + 2], (L, LANES))
        parts.append(jnp.where(lo, c0, c1))
    return jnp.concatenate(parts, axis=1)


def _ssd_gate_norm(ys, xs, z, dskip, g_ssd):
    ys = ys + dskip * xs
    gated = ys * (z * _sigmoid(z))
    half = SSD_WIDTH // N_SSD_GROUPS
    outs = []
    for g in range(N_SSD_GROUPS):
        outs.append(_rms(gated[:, half * g:half * (g + 1)], g_ssd[:, half * g:half * (g + 1)]))
    return jnp.concatenate(outs, axis=1)


def _inproj_kernel(x_ref, g_ref, w_ref, o_ref):
    hn = _rms(x_ref[...], g_ref[...]).astype(BF16)
    o_ref[...] = jnp.dot(hn, w_ref[...], preferred_element_type=F32)


def _in_proj(x2d, g_mix, w_in_p):
    n = x2d.shape[0]
    return pl.pallas_call(
        _inproj_kernel,
        out_shape=jax.ShapeDtypeStruct((n, PROJ_PAD), F32),
        grid=(n // ROW_TILE,),
        in_specs=[
            pl.BlockSpec((ROW_TILE, D_MODEL), lambda i: (i, 0)),
            pl.BlockSpec((1, D_MODEL), lambda i: (0, 0)),
            pl.BlockSpec((D_MODEL, PROJ_PAD), lambda i: (0, 0), pipeline_mode=pl.Buffered(1)),
        ],
        out_specs=pl.BlockSpec((ROW_TILE, PROJ_PAD), lambda i: (i, 0)),
        compiler_params=pltpu.CompilerParams(
            dimension_semantics=("parallel",), vmem_limit_bytes=VMEM_LIMIT),
        name="in_proj",
    )(x2d, g_mix, w_in_p)


def _outmlp_kernel(x_ref, y_ref, wo_ref, gm_ref, wu_ref, wd_ref, gf_ref, o_ref):
    x1 = x_ref[...] + jnp.dot(y_ref[...].astype(BF16), wo_ref[...], preferred_element_type=F32)
    m = _rms(x1, gm_ref[...]).astype(BF16)
    u = jnp.dot(m, wu_ref[...], preferred_element_type=F32)
    u = jnp.square(jnp.maximum(u, 0.0)).astype(BF16)
    x2 = x1 + jnp.dot(u, wd_ref[...], preferred_element_type=F32)
    o_ref[...] = _rms(x2, gf_ref[...])


def _out_mlp(x2d, ymix2d, w_out_b, g_mlp, w_up_b, w_down_b, g_final):
    n = x2d.shape[0]
    const = lambda i: (0, 0)
    return pl.pallas_call(
        _outmlp_kernel,
        out_shape=jax.ShapeDtypeStruct((n, D_MODEL), F32),
        grid=(n // ROW_TILE,),
        in_specs=[
            pl.BlockSpec((ROW_TILE, D_MODEL), lambda i: (i, 0)),
            pl.BlockSpec((ROW_TILE, MIX_WIDTH), lambda i: (i, 0)),
            pl.BlockSpec((MIX_WIDTH, D_MODEL), const, pipeline_mode=pl.Buffered(1)),
            pl.BlockSpec((1, D_MODEL), const),
            pl.BlockSpec((D_MODEL, D_FF), const, pipeline_mode=pl.Buffered(1)),
            pl.BlockSpec((D_FF, D_MODEL), const, pipeline_mode=pl.Buffered(1)),
            pl.BlockSpec((1, D_MODEL), const),
        ],
        out_specs=pl.BlockSpec((ROW_TILE, D_MODEL), lambda i: (i, 0)),
        compiler_params=pltpu.CompilerParams(
            dimension_semantics=("parallel",), vmem_limit_bytes=VMEM_LIMIT),
        name="out_mlp",
    )(x2d, ymix2d, w_out_b, g_mlp, w_up_b, w_down_b, g_final)


def _mixer_prompt_kernel(lx_ref, gate_ref, z_ref, xbc_ref, dt_ref,
                         lcw_ref, lcb_ref, wg_ref, ba_ref, bx_ref, lam_ref, glru_ref,
                         scw_ref, scb_ref, dtb_ref, alog_ref, dskip_ref, gssd_ref,
                         y_ref, olc_ref, olh_ref, osc_ref, osh_ref,
                         ext_l, ext_s, a_pad, b_pad, h_pad, hcar, ht):
    t = pl.program_id(1)
    nt = pl.num_programs(1)
    tc = PROMPT_TC
    hist = SUBLANES

    @pl.when(t == 0)
    def _init():
        ext_l[:, 0:hist, :] = jnp.zeros((ext_l.shape[0], hist, LANES), F32)
        ext_s[:, 0:hist, :] = jnp.zeros((ext_s.shape[0], hist, LANES), F32)
        hcar[...] = jnp.zeros_like(hcar)
        ht[...] = jnp.zeros_like(ht)

    for s in range(ext_l.shape[0]):
        ext_l[s, hist:hist + tc, :] = lx_ref[:, LANES * s:LANES * (s + 1)]
    for s in range(ext_s.shape[0]):
        ext_s[s, hist:hist + tc, :] = xbc_ref[:, LANES * s:LANES * (s + 1)]

    u = _conv_slabs(ext_l, lcw_ref, lcb_ref, tc, hist - (CONV_WIDTH - 1))
    sp_lam = _softplus(-lam_ref[...])
    a, b = _lru_coeffs(u, wg_ref, ba_ref[...], bx_ref[...], sp_lam)
    hseq = _lru_scan_strided(a, b, hcar, a_pad, b_pad, h_pad)
    y_ref[:, 0:LRU_WIDTH] = _rms(hseq * _gelu_tanh(gate_ref[...]), glru_ref[...])

    xbc = _conv_slabs(ext_s, scw_ref, scb_ref, tc, hist - (CONV_WIDTH - 1))
    xbc = xbc * _sigmoid(xbc)
    dt_all = _softplus(dt_ref[...] + dtb_ref[...])
    lane1 = lax.broadcasted_iota(jnp.int32, (1, LANES), 1)
    a_row = jnp.where(lane1 < N_SSD_HEADS, -jnp.exp(alog_ref[...]), 0.0)

    L = SSD_CHUNK
    rr = lax.broadcasted_iota(jnp.int32, (L, L), 0)
    cc = lax.broadcasted_iota(jnp.int32, (L, L), 1)
    causal = cc <= rr
    tri = jnp.where(causal, 1.0, 0.0).astype(F32)
    mask_add = jnp.where(causal, 0.0, NEG_BIG).astype(F32)

    for c in range(tc // L):
        rows = slice(L * c, L * (c + 1))
        xs = xbc[rows, 0:SSD_WIDTH]
        bm = xbc[rows, SSD_WIDTH:SSD_WIDTH + N_SSD_GROUPS * D_STATE]
        cm = xbc[rows, SSD_WIDTH + N_SSD_GROUPS * D_STATE:]
        dt = dt_all[rows, :]
        y_diag, ecol, colx, cum, _ = _ssd_intra(xs, bm, cm, dt, a_row, tri, mask_add)
        dtx = _expand_heads(dt, L)
        end_row = colx[L - 1:L, :]
        xw = xs * (jnp.exp(end_row - colx) * dtx)
        dec = ecol[L - 1:L, :]
        y_off_parts = []
        half = SSD_WIDTH // N_SSD_GROUPS
        for g in range(N_SSD_GROUPS):
            htg = ht[g]
            cg = cm[:, D_STATE * g:D_STATE * (g + 1)].astype(BF16)
            y_off_parts.append(jnp.dot(cg, htg.astype(BF16), preferred_element_type=F32))
            bg_t = bm[:, D_STATE * g:D_STATE * (g + 1)].T.astype(BF16)
            st = jnp.dot(bg_t, xw[:, half * g:half * (g + 1)].astype(BF16), preferred_element_type=F32)
            ht[g] = htg * dec[:, half * g:half * (g + 1)] + st
        ys = y_diag + jnp.concatenate(y_off_parts, axis=1) * ecol
        y_ref[rows, LRU_WIDTH:MIX_WIDTH] = _ssd_gate_norm(
            ys, xs, z_ref[rows, :], dskip_ref[...], gssd_ref[...])

    @pl.when(t == nt - 1)
    def _final():
        last = slice(hist + tc - (CONV_WIDTH - 1), hist + tc)
        for s in range(ext_l.shape[0]):
            olc_ref[:, LANES * s:LANES * (s + 1)] = ext_l[s, last, :]
        for s in range(ext_s.shape[0]):
            osc_ref[:, LANES * s:LANES * (s + 1)] = ext_s[s, last, :]
        olh_ref[...] = hcar[0:1, :]
        half = SSD_WIDTH // N_SSD_GROUPS
        for g in range(N_SSD_GROUPS):
            osh_ref[half * g:half * (g + 1), :] = ht[g].T

    tail_l = ext_l[:, tc:tc + hist, :]
    tail_s = ext_s[:, tc:tc + hist, :]
    ext_l[:, 0:hist, :] = tail_l
    ext_s[:, 0:hist, :] = tail_s


def _param_specs(const):
    return [
        pl.BlockSpec((CONV_WIDTH, LRU_WIDTH), const),
        pl.BlockSpec((1, LRU_WIDTH), const),
        pl.BlockSpec((LRU_WIDTH // MXU_DIM, MXU_DIM, 2 * MXU_DIM), lambda *_: (0, 0, 0)),
        pl.BlockSpec((1, LRU_WIDTH), const),
        pl.BlockSpec((1, LRU_WIDTH), const),
        pl.BlockSpec((1, LRU_WIDTH), const),
        pl.BlockSpec((1, LRU_WIDTH), const),
        pl.BlockSpec((CONV_WIDTH, SSD_CONV_DIM), const),
        pl.BlockSpec((1, SSD_CONV_DIM), const),
        pl.BlockSpec((1, DT_PAD), const),
        pl.BlockSpec((1, DT_PAD), const),
        pl.BlockSpec((1, SSD_WIDTH), const),
        pl.BlockSpec((1, SSD_WIDTH), const),
    ]


def _mixer_prompt(proj, params):
    bsz, seq, _ = proj.shape
    tc = PROMPT_TC
    const = lambda b, t: (0, 0)
    in_specs = [
        pl.BlockSpec((None, tc, LRU_WIDTH), lambda b, t: (b, t, 0)),
        pl.BlockSpec((None, tc, LRU_WIDTH), lambda b, t: (b, t, 1)),
        pl.BlockSpec((None, tc, SSD_WIDTH), lambda b, t: (b, t, 2)),
        pl.BlockSpec((None, tc, SSD_CONV_DIM), lambda b, t: (b, t, 2)),
        pl.BlockSpec((None, tc, DT_PAD), lambda b, t: (b, t, PROJ_MAIN // DT_PAD)),
    ] + _param_specs(const)
    out_shape = (
        jax.ShapeDtypeStruct((bsz, seq, MIX_WIDTH), F32),
        jax.ShapeDtypeStruct((bsz, CONV_WIDTH - 1, LRU_WIDTH), F32),
        jax.ShapeDtypeStruct((bsz, 1, LRU_WIDTH), F32),
        jax.ShapeDtypeStruct((bsz, CONV_WIDTH - 1, SSD_CONV_DIM), F32),
        jax.ShapeDtypeStruct((bsz, SSD_WIDTH, D_STATE), F32),
    )
    out_specs = (
        pl.BlockSpec((None, tc, MIX_WIDTH), lambda b, t: (b, t, 0)),
        pl.BlockSpec((None, CONV_WIDTH - 1, LRU_WIDTH), lambda b, t: (b, 0, 0)),
        pl.BlockSpec((None, 1, LRU_WIDTH), lambda b, t: (b, 0, 0)),
        pl.BlockSpec((None, CONV_WIDTH - 1, SSD_CONV_DIM), lambda b, t: (b, 0, 0)),
        pl.BlockSpec((None, SSD_WIDTH, D_STATE), lambda b, t: (b, 0, 0)),
    )
    scratch = [
        pltpu.VMEM((LRU_WIDTH // LANES, SUBLANES + tc, LANES), F32),
        pltpu.VMEM((SSD_CONV_DIM // LANES, SUBLANES + tc, LANES), F32),
        pltpu.VMEM((LRU_WIDTH // LANES, SUBLANES * SCAN_PITCH, LANES), F32),
        pltpu.VMEM((LRU_WIDTH // LANES, SUBLANES * SCAN_PITCH, LANES), F32),
        pltpu.VMEM((LRU_WIDTH // LANES, SUBLANES * SCAN_PITCH, LANES), F32),
        pltpu.VMEM((SUBLANES, LRU_WIDTH), F32),
        pltpu.VMEM((N_SSD_GROUPS, D_STATE, SSD_WIDTH // N_SSD_GROUPS), F32),
    ]
    return pl.pallas_call(
        _mixer_prompt_kernel,
        out_shape=out_shape,
        grid=(bsz, seq // tc),
        in_specs=in_specs,
        out_specs=out_specs,
        scratch_shapes=scratch,
        compiler_params=pltpu.CompilerParams(
            dimension_semantics=("parallel", "arbitrary"), vmem_limit_bytes=VMEM_LIMIT),
        name="mixer_prompt",
    )(proj, proj, proj, proj, proj, *params)


def _mixer_sample_kernel(lx_ref, gate_ref, z_ref, xbc_ref, dt_ref,
                         slc_ref, slh_ref, ssc_ref, ssh_ref,
                         lcw_ref, lcb_ref, wg_ref, ba_ref, bx_ref, lam_ref, glru_ref,
                         scw_ref, scb_ref, dtb_ref, alog_ref, dskip_ref, gssd_ref,
                         y_ref, olc_ref, olh_ref, osc_ref, osh_ref,
                         ext_l, ext_s, pad_scr, yoff_scr):
    S = SAMPLE_SEQS
    P = SUBLANES
    T = lx_ref.shape[1]
    K1 = CONV_WIDTH - 1
    R = S * P
    row_i = lax.broadcasted_iota(jnp.int32, (R, 1), 0) & (P - 1)
    valid = row_i < T

    def pad_rows(ref):
        width = ref.shape[-1]
        pad_scr[:, :, 0:width] = jnp.zeros((S, P, width), F32)
        pad_scr[:, 0:T, 0:width] = ref[...]
        return pad_scr[:, :, 0:width].reshape(R, width)

    ext_l[...] = jnp.zeros_like(ext_l)
    ext_s[...] = jnp.zeros_like(ext_s)
    ext_l[:, 0:K1, :] = slc_ref[...]
    ext_l[:, K1:K1 + T, :] = lx_ref[...]
    ext_s[:, 0:K1, :] = ssc_ref[...]
    ext_s[:, K1:K1 + T, :] = xbc_ref[...]
    olc_ref[...] = ext_l[:, T:T + K1, :]
    osc_ref[...] = ext_s[:, T:T + K1, :]

    el = ext_l[...].reshape(R, LRU_WIDTH)
    es = ext_s[...].reshape(R, SSD_CONV_DIM)

    def conv(e, w_ref, b_ref):
        out = b_ref[...] + e * w_ref[0:1, :]
        for k in range(1, CONV_WIDTH):
            out = out + pltpu.roll(e, R - k, axis=0) * w_ref[k:k + 1, :]
        return out

    u = conv(el, lcw_ref, lcb_ref)
    sp_lam = _softplus(-lam_ref[...])
    a, b = _lru_coeffs(u, wg_ref, ba_ref[...], bx_ref[...], sp_lam)
    a, b = _scan_within_8(a, b)
    h0 = jnp.broadcast_to(slh_ref[...], (S, P, LRU_WIDTH)).reshape(R, LRU_WIDTH)
    hseq = a * h0 + b
    olh_ref[...] = hseq.reshape(S, P, LRU_WIDTH)[:, T - 1:T, :]
    gate = pad_rows(gate_ref)
    y_lru = _rms(hseq * _gelu_tanh(gate), glru_ref[...])

    xbc = conv(es, scw_ref, scb_ref)
    xbc = xbc * _sigmoid(xbc)
    xs = xbc[:, 0:SSD_WIDTH]
    bm = xbc[:, SSD_WIDTH:SSD_WIDTH + N_SSD_GROUPS * D_STATE]
    cm = xbc[:, SSD_WIDTH + N_SSD_GROUPS * D_STATE:]
    dt_raw = pad_rows(dt_ref)
    dt = jnp.where(valid, _softplus(dt_raw + dtb_ref[...]), 0.0)
    lane1 = lax.broadcasted_iota(jnp.int32, (1, LANES), 1)
    a_row = jnp.where(lane1 < N_SSD_HEADS, -jnp.exp(alog_ref[...]), 0.0)

    rr = lax.broadcasted_iota(jnp.int32, (R, R), 0)
    cc = lax.broadcasted_iota(jnp.int32, (R, R), 1)
    allowed = (cc <= rr) & ((rr - cc) <= (rr & (P - 1)))
    tri = jnp.where(allowed, 1.0, 0.0).astype(F32)
    mask_add = jnp.where(allowed, 0.0, NEG_BIG).astype(F32)

    y_diag, ecol, colx, cum, cum_t = _ssd_intra(xs, bm, cm, dt, a_row, tri, mask_add)
    dtx = _expand_heads(dt, R)
    colx3 = colx.reshape(S, P, SSD_WIDTH)
    end_rows = jnp.broadcast_to(colx3[:, P - 1:P, :], (S, P, SSD_WIDTH)).reshape(R, SSD_WIDTH)
    xw = xs * (jnp.exp(end_rows - colx) * dtx)
    ecum_t = jnp.exp(cum_t[0:N_SSD_HEADS, :])

    half = SSD_WIDTH // N_SSD_GROUPS
    for q in range(S):
        r0 = P * q
        vq = jnp.broadcast_to(ecum_t[:, r0 + P - 1:r0 + P], (N_SSD_HEADS, LANES))
        for g in range(N_SSD_GROUPS):
            hqg = ssh_ref[q, half * g:half * (g + 1), :]
            cq = cm[r0:r0 + P, D_STATE * g:D_STATE * (g + 1)].astype(BF16)
            yoff_scr[r0:r0 + P, half * g:half * (g + 1)] = lax.dot_general(
                cq, hqg.astype(BF16), (((1,), (1,)), ((), ())), preferred_element_type=F32)
            bq = bm[r0:r0 + P, D_STATE * g:D_STATE * (g + 1)].astype(BF16)
            xq = xw[r0:r0 + P, half * g:half * (g + 1)].astype(BF16)
            st = lax.dot_general(xq, bq, (((0,), (0,)), ((), ())), preferred_element_type=F32)
            for e in range(N_SSD_HEADS // N_SSD_GROUPS):
                h = (N_SSD_HEADS // N_SSD_GROUPS) * g + e
                lo_r = SSD_HEAD_DIM * e
                osh_ref[q, SSD_HEAD_DIM * h:SSD_HEAD_DIM * (h + 1), :] = (
                    vq[h:h + 1, :] * hqg[lo_r:lo_r + SSD_HEAD_DIM, :] + st[lo_r:lo_r + SSD_HEAD_DIM, :])

    ys = y_diag + yoff_scr[...] * ecol
    z = pad_rows(z_ref)
    y_ssd = _ssd_gate_norm(ys, xs, z, dskip_ref[...], gssd_ref[...])
    y_ref[:, :, 0:LRU_WIDTH] = y_lru.reshape(S, P, LRU_WIDTH)[:, 0:T, :]
    y_ref[:, :, LRU_WIDTH:MIX_WIDTH] = y_ssd.reshape(S, P, SSD_WIDTH)[:, 0:T, :]


def _mixer_sample(proj, st_lc, st_lh, st_sc, st_sh, params):
    nseq, T, _ = proj.shape
    S = SAMPLE_SEQS
    const = lambda i: (0, 0)
    in_specs = [
        pl.BlockSpec((S, T, LRU_WIDTH), lambda i: (i, 0, 0)),
        pl.BlockSpec((S, T, LRU_WIDTH), lambda i: (i, 0, 1)),
        pl.BlockSpec((S, T, SSD_WIDTH), lambda i: (i, 0, 2)),
        pl.BlockSpec((S, T, SSD_CONV_DIM), lambda i: (i, 0, 2)),
        pl.BlockSpec((S, T, DT_PAD), lambda i: (i, 0, PROJ_MAIN // DT_PAD)),
        pl.BlockSpec((S, CONV_WIDTH - 1, LRU_WIDTH), lambda i: (i, 0, 0)),
        pl.BlockSpec((S, 1, LRU_WIDTH), lambda i: (i, 0, 0)),
        pl.BlockSpec((S, CONV_WIDTH - 1, SSD_CONV_DIM), lambda i: (i, 0, 0)),
        pl.BlockSpec((S, SSD_WIDTH, D_STATE), lambda i: (i, 0, 0)),
    ] + _param_specs(const)
    out_shape = (
        jax.ShapeDtypeStruct((nseq, T, MIX_WIDTH), F32),
        jax.ShapeDtypeStruct((nseq, CONV_WIDTH - 1, LRU_WIDTH), F32),
        jax.ShapeDtypeStruct((nseq, 1, LRU_WIDTH), F32),
        jax.ShapeDtypeStruct((nseq, CONV_WIDTH - 1, SSD_CONV_DIM), F32),
        jax.ShapeDtypeStruct((nseq, SSD_WIDTH, D_STATE), F32),
    )
    out_specs = (
        pl.BlockSpec((S, T, MIX_WIDTH), lambda i: (i, 0, 0)),
        pl.BlockSpec((S, CONV_WIDTH - 1, LRU_WIDTH), lambda i: (i, 0, 0)),
        pl.BlockSpec((S, 1, LRU_WIDTH), lambda i: (i, 0, 0)),
        pl.BlockSpec((S, CONV_WIDTH - 1, SSD_CONV_DIM), lambda i: (i, 0, 0)),
        pl.BlockSpec((S, SSD_WIDTH, D_STATE), lambda i: (i, 0, 0)),
    )
    scratch = [
        pltpu.VMEM((S, SUBLANES, LRU_WIDTH), F32),
        pltpu.VMEM((S, SUBLANES, SSD_CONV_DIM), F32),
        pltpu.VMEM((S, SUBLANES, LRU_WIDTH), F32),
        pltpu.VMEM((S * SUBLANES, SSD_WIDTH), F32),
    ]
    return pl.pallas_call(
        _mixer_sample_kernel,
        out_shape=out_shape,
        grid=(nseq // S,),
        in_specs=in_specs,
        out_specs=out_specs,
        scratch_shapes=scratch,
        compiler_params=pltpu.CompilerParams(
            dimension_semantics=("parallel",), vmem_limit_bytes=VMEM_LIMIT),
        name="mixer_sample",
    )(proj, proj, proj, proj, proj, st_lc, st_lh, st_sc, st_sh, *params)


def _gate_weights(w_a, w_x):
    def tiles(w):
        per = MXU_DIM // LRU_BLOCK
        w4 = w.reshape(N_LRU_HEADS // per, per, LRU_BLOCK, LRU_BLOCK)
        eye = jnp.eye(per, dtype=w.dtype)
        t = jnp.einsum('jaik,ab->jaibk', w4, eye)
        return t.reshape(N_LRU_HEADS // per, MXU_DIM, MXU_DIM)
    return jnp.concatenate([tiles(w_a), tiles(w_x)], axis=2).astype(BF16)


def kernel(x_prompt, x_sample, state_lru_conv, state_lru_h, state_ssd_conv, state_ssd_h, g_mix, w_in,
           lru_conv_w, lru_conv_b, w_a, b_a, w_x, b_x, lam, g_lru_out, ssd_conv_w, ssd_conv_b, dt_bias,
           a_log, d_skip, g_ssd_out, w_out, g_mlp, w_up, w_down, g_final):
    depth = w_in.shape[0]
    assert depth == 1
    bp, seq, _ = x_prompt.shape
    bs, dseq, _ = x_sample.shape
    l = 0
    row = lambda v: v.reshape(1, -1)
    w_in_p = jnp.pad(w_in[l], ((0, 0), (0, PROJ_PAD - w_in.shape[2]))).astype(BF16)
    params = (
        lru_conv_w[l], row(lru_conv_b[l]), _gate_weights(w_a[l], w_x[l]),
        row(b_a[l]), row(b_x[l]), row(lam[l]), row(g_lru_out[l]),
        ssd_conv_w[l], row(ssd_conv_b[l]),
        jnp.pad(row(dt_bias[l]), ((0, 0), (0, DT_PAD - N_SSD_HEADS))),
        jnp.pad(row(a_log[l]), ((0, 0), (0, DT_PAD - N_SSD_HEADS))),
        row(jnp.repeat(d_skip[l], SSD_HEAD_DIM)), row(g_ssd_out[l]),
    )
    w_out_b = w_out[l].astype(BF16)
    w_up_b = w_up[l].astype(BF16)
    w_down_b = w_down[l].astype(BF16)
    gmix = row(g_mix[l])
    gmlp = row(g_mlp[l])
    gfin = row(g_final)

    xp2 = x_prompt.reshape(bp * seq, D_MODEL)
    proj_p = _in_proj(xp2, gmix, w_in_p).reshape(bp, seq, PROJ_PAD)
    ymix_p, p_lc, p_lh, p_sc, p_sh = _mixer_prompt(proj_p, params)
    y_prompt = _out_mlp(xp2, ymix_p.reshape(bp * seq, MIX_WIDTH), w_out_b, gmlp, w_up_b, w_down_b, gfin)

    xs2 = x_sample.reshape(bs * dseq, D_MODEL)
    proj_s = _in_proj(xs2, gmix, w_in_p).reshape(bs, dseq, PROJ_PAD)
    ymix_s, s_lc, s_lh, s_sc, s_sh = _mixer_sample(
        proj_s, state_lru_conv[l], state_lru_h[l].reshape(bs, 1, LRU_WIDTH), state_ssd_conv[l],
        state_ssd_h[l].reshape(bs, SSD_WIDTH, D_STATE), params)
    y_sample = _out_mlp(xs2, ymix_s.reshape(bs * dseq, MIX_WIDTH), w_out_b, gmlp, w_up_b, w_down_b, gfin)

    hshape = (N_SSD_HEADS, SSD_HEAD_DIM, D_STATE)
    return (
        y_prompt.reshape(bp, seq, D_MODEL), y_sample.reshape(bs, dseq, D_MODEL),
        p_lc[None], p_lh.reshape(1, bp, LRU_WIDTH), p_sc[None], p_sh.reshape(1, bp, *hshape),
        s_lc[None], s_lh.reshape(1, bs, LRU_WIDTH), s_sc[None], s_sh.reshape(1, bs, *hshape),
    )
```

```python
import functools
import math

import jax
import jax.numpy as jnp
from jax import lax
from jax.experimental import pallas as pl
from jax.experimental.pallas import tpu as pltpu

F32 = jnp.float32
BF16 = jnp.bfloat16

D_MODEL = 1024
LRU_WIDTH = 1024
N_LRU_HEADS = 16
LRU_BLOCK = 64
LRU_C = 8.0
SSD_WIDTH = 1024
SSD_HEAD_DIM = 64
N_SSD_HEADS = 16
N_SSD_GROUPS = 2
D_STATE = 128
CONV_WIDTH = 4
SSD_CONV_DIM = SSD_WIDTH + 2 * N_SSD_GROUPS * D_STATE
D_FF = 4 * D_MODEL
EPS = 1e-6

LANES = 128
SUBLANES = 8
MXU_DIM = 256
DT_PAD = LANES
PROJ_MAIN = 2 * LRU_WIDTH + SSD_WIDTH + SSD_CONV_DIM
PROJ_PAD = PROJ_MAIN + DT_PAD
MIX_WIDTH = LRU_WIDTH + SSD_WIDTH
SSD_CHUNK = 128
PROMPT_TC = 256
ROW_TILE = 512
SAMPLE_SEQS = SSD_CHUNK // SUBLANES
SCAN_RUN = PROMPT_TC // SUBLANES
SCAN_PITCH = SCAN_RUN + 4
NEG_BIG = -1e30
LOG2E = 1.4426950408889634
VMEM_LIMIT = 56 * 1024 * 1024
HI = lax.Precision.HIGHEST


def _rms(x, g):
    ms = jnp.mean(x * x, axis=-1, keepdims=True)
    return x * lax.rsqrt(ms + EPS) * g


def _sigmoid(x):
    return 1.0 / (1.0 + jnp.exp(-x))


def _softplus(x):
    return jnp.maximum(x, 0.0) + jnp.log1p(jnp.exp(-jnp.abs(x)))


def _gelu_tanh(x):
    c = math.sqrt(2.0 / math.pi)
    return 0.5 * x * (1.0 + jnp.tanh(c * (x + 0.044715 * (x * x * x))))


def _lru_coeffs(u, wg_ref, b_a, b_x, sp_lam):
    ub = u.astype(BF16)
    r_parts, i_parts = [], []
    for j in range(LRU_WIDTH // MXU_DIM):
        g = jnp.dot(ub[:, MXU_DIM * j:MXU_DIM * (j + 1)], wg_ref[j], preferred_element_type=F32)
        r_parts.append(g[:, :MXU_DIM])
        i_parts.append(g[:, MXU_DIM:])
    r = _sigmoid(jnp.concatenate(r_parts, axis=1) + b_a)
    i = _sigmoid(jnp.concatenate(i_parts, axis=1) + b_x)
    log_a = (-LRU_C) * r * sp_lam
    a = jnp.exp(log_a)
    th = jnp.tanh(-log_a)
    v = 2.0 * th / (1.0 + th)
    mult = jnp.where(v > 0.0, v * lax.rsqrt(v), 0.0)
    return a, mult * (i * u)


def _scan_within_8(a, b):
    ridx = lax.broadcasted_iota(jnp.int32, a.shape, 0) & (SUBLANES - 1)
    for k in (1, 2, 4):
        a_s = pltpu.roll(a, k, axis=0)
        b_s = pltpu.roll(b, k, axis=0)
        m = ridx >= k
        b = jnp.where(m, a * b_s + b, b)
        a = jnp.where(m, a * a_s, a)
    return a, b


def _conv_slabs(ext, w_ref, b_ref, rows, first):
    parts = []
    for s in range(ext.shape[0]):
        cols = slice(LANES * s, LANES * (s + 1))
        acc = b_ref[:, cols] + ext[s, pl.ds(first, rows), :] * w_ref[0:1, cols]
        for k in range(1, CONV_WIDTH):
            acc = acc + ext[s, pl.ds(first + k, rows), :] * w_ref[k:k + 1, cols]
        parts.append(acc)
    return jnp.concatenate(parts, axis=1)


def _lru_scan_strided(a, b, hcar, a_pad, b_pad, h_pad):
    rows = a.shape[0]
    S = rows // SUBLANES
    ridx = lax.broadcasted_iota(jnp.int32, (SUBLANES, LANES), 0)
    outs = []
    for s in range(LRU_WIDTH // LANES):
        cols = slice(LANES * s, LANES * (s + 1))
        for j in range(SUBLANES):
            a_pad[s, SCAN_PITCH * j:SCAN_PITCH * j + S, :] = a[S * j:S * (j + 1), cols]
            b_pad[s, SCAN_PITCH * j:SCAN_PITCH * j + S, :] = b[S * j:S * (j + 1), cols]
        h = jnp.zeros((SUBLANES, LANES), F32)
        prod = jnp.ones((SUBLANES, LANES), F32)
        for i in range(S):
            av = a_pad[s, pl.ds(i, SUBLANES, stride=SCAN_PITCH), :]
            h = av * h + b_pad[s, pl.ds(i, SUBLANES, stride=SCAN_PITCH), :]
            prod = av * prod
        pcum, hcum = _scan_within_8(prod, h)
        cin = hcar[:, cols]
        ends = hcum + pcum * cin
        h = jnp.where(ridx == 0, cin, pltpu.roll(ends, 1, axis=0))
        for i in range(S):
            av = a_pad[s, pl.ds(i, SUBLANES, stride=SCAN_PITCH), :]
            h = av * h + b_pad[s, pl.ds(i, SUBLANES, stride=SCAN_PITCH), :]
            h_pad[s, pl.ds(i, SUBLANES, stride=SCAN_PITCH), :] = h
        hcar[:, cols] = jnp.broadcast_to(ends[SUBLANES - 1:SUBLANES, :], (SUBLANES, LANES))
        outs.append(jnp.concatenate(
            [h_pad[s, SCAN_PITCH * j:SCAN_PITCH * j + S, :] for j in range(SUBLANES)], axis=0))
    return jnp.concatenate(outs, axis=1)


def _ssd_intra(xs, bm, cm, dt, a2_row, tri, mask_add):
    L = xs.shape[0]
    cum2 = jnp.dot(tri, dt * a2_row, precision=HI, preferred_element_type=F32)
    cum2_t = cum2.T[0:N_SSD_HEADS, :]
    c2_t = cum2_t - jnp.log2(dt.T[0:N_SSD_HEADS, :])
    lane = lax.broadcasted_iota(jnp.int32, (L, LANES), 1)
    lo = lane < SSD_HEAD_DIM
    y_parts = []
    for g in range(N_SSD_GROUPS):
        bg = bm[:, D_STATE * g:D_STATE * (g + 1)].astype(BF16)
        cg = cm[:, D_STATE * g:D_STATE * (g + 1)].astype(BF16)
        cb = lax.dot_general(cg, bg, (((1,), (1,)), ((), ())), preferred_element_type=F32)
        for jj in range(N_SSD_HEADS // N_SSD_GROUPS // 2):
            j = (N_SSD_HEADS // N_SSD_GROUPS // 2) * g + jj
            h0, h1 = 2 * j, 2 * j + 1
            col0 = jnp.broadcast_to(cum2[:, h0:h0 + 1], (L, LANES))
            col1 = jnp.broadcast_to(cum2[:, h1:h1 + 1], (L, LANES))
            m0 = cb * jnp.exp2(col0 - c2_t[h0:h0 + 1, :] + mask_add)
            m1 = cb * jnp.exp2(col1 - c2_t[h1:h1 + 1, :] + mask_add)
            lhs = jnp.concatenate([m0, m1], axis=1).astype(BF16)
            xp = xs[:, LANES * j:LANES * (j + 1)]
            rhs = jnp.concatenate([jnp.where(lo, xp, 0.0), jnp.where(lo, 0.0, xp)], axis=0).astype(BF16)
            y_parts.append(jnp.dot(lhs, rhs, preferred_element_type=F32))
    return jnp.concatenate(y_parts, axis=1), cum2, cum2_t


def _expand_heads(v, L):
    lane = lax.broadcasted_iota(jnp.int32, (L, LANES), 1)
    lo = lane < SSD_HEAD_DIM
    parts = []
    for j in range(N_SSD_HEADS // 2):
        c0 = jnp.broadcast_to(v[:, 2 * j:2 * j + 1], (L, LANES))
        c1 = jnp.broadcast_to(v[:, 2 * j + 1:2 * j + 2], (L, LANES))
        parts.append(jnp.where(lo, c0, c1))
    return jnp.concatenate(parts, axis=1)


def _ssd_gate_norm(ys, xs, z, dskip, g_ssd):
    ys = ys + dskip * xs
    gated = ys * (z * _sigmoid(z))
    half = SSD_WIDTH // N_SSD_GROUPS
    outs = []
    for g in range(N_SSD_GROUPS):
        outs.append(_rms(gated[:, half * g:half * (g + 1)], g_ssd[:, half * g:half * (g + 1)]))
    return jnp.concatenate(outs, axis=1)


def _inproj_kernel(x_ref, g_ref, w_ref, wdt_ref, o_ref):
    hn = _rms(x_ref[...], g_ref[...]).astype(BF16)
    o_ref[:, 0:PROJ_MAIN] = jnp.dot(hn, w_ref[...], preferred_element_type=F32)
    o_ref[:, PROJ_MAIN:PROJ_PAD] = jnp.dot(hn, wdt_ref[...], preferred_element_type=F32)


def _in_proj(x2d, g_mix, w_main, w_dt):
    n = x2d.shape[0]
    return pl.pallas_call(
        _inproj_kernel,
        out_shape=jax.ShapeDtypeStruct((n, PROJ_PAD), F32),
        grid=(n // ROW_TILE,),
        in_specs=[
            pl.BlockSpec((ROW_TILE, D_MODEL), lambda i: (i, 0)),
            pl.BlockSpec((1, D_MODEL), lambda i: (0, 0)),
            pl.BlockSpec((D_MODEL, PROJ_MAIN), lambda i: (0, 0), pipeline_mode=pl.Buffered(1)),
            pl.BlockSpec((D_MODEL, DT_PAD), lambda i: (0, 0), pipeline_mode=pl.Buffered(1)),
        ],
        out_specs=pl.BlockSpec((ROW_TILE, PROJ_PAD), lambda i: (i, 0)),
        compiler_params=pltpu.CompilerParams(
            dimension_semantics=("parallel",), vmem_limit_bytes=VMEM_LIMIT),
        name="in_proj",
    )(x2d, g_mix, w_main, w_dt)


def _outmlp_kernel(x_ref, y_ref, wo_ref, gm_ref, wu_ref, wd_ref, gf_ref, o_ref):
    x1 = x_ref[...] + jnp.dot(y_ref[...].astype(BF16), wo_ref[...], preferred_element_type=F32)
    m = _rms(x1, gm_ref[...]).astype(BF16)
    u = jnp.dot(m, wu_ref[...], preferred_element_type=F32)
    u = jnp.square(jnp.maximum(u, 0.0)).astype(BF16)
    x2 = x1 + jnp.dot(u, wd_ref[...], preferred_element_type=F32)
    o_ref[...] = _rms(x2, gf_ref[...])


def _out_mlp(x2d, ymix2d, w_out_b, g_mlp, w_up_b, w_down_b, g_final):
    n = x2d.shape[0]
    const = lambda i: (0, 0)
    return pl.pallas_call(
        _outmlp_kernel,
        out_shape=jax.ShapeDtypeStruct((n, D_MODEL), F32),
        grid=(n // ROW_TILE,),
        in_specs=[
            pl.BlockSpec((ROW_TILE, D_MODEL), lambda i: (i, 0)),
            pl.BlockSpec((ROW_TILE, MIX_WIDTH), lambda i: (i, 0)),
            pl.BlockSpec((MIX_WIDTH, D_MODEL), const, pipeline_mode=pl.Buffered(1)),
            pl.BlockSpec((1, D_MODEL), const),
            pl.BlockSpec((D_MODEL, D_FF), const, pipeline_mode=pl.Buffered(1)),
            pl.BlockSpec((D_FF, D_MODEL), const, pipeline_mode=pl.Buffered(1)),
            pl.BlockSpec((1, D_MODEL), const),
        ],
        out_specs=pl.BlockSpec((ROW_TILE, D_MODEL), lambda i: (i, 0)),
        compiler_params=pltpu.CompilerParams(
            dimension_semantics=("parallel",), vmem_limit_bytes=VMEM_LIMIT),
        name="out_mlp",
    )(x2d, ymix2d, w_out_b, g_mlp, w_up_b, w_down_b, g_final)


def _mixer_prompt_kernel(lx_ref, gate_ref, z_ref, xbc_ref, dt_ref,
                         lcw_ref, lcb_ref, wg_ref, ba_ref, bx_ref, lam_ref, glru_ref,
                         scw_ref, scb_ref, dtb_ref, alog_ref, dskip_ref, gssd_ref,
                         y_ref, olc_ref, olh_ref, osc_ref, osh_ref,
                         ext_l, ext_s, a_pad, b_pad, h_pad, hcar, ht):
    t = pl.program_id(1)
    nt = pl.num_programs(1)
    tc = PROMPT_TC
    hist = SUBLANES

    @pl.when(t == 0)
    def _init():
        ext_l[:, 0:hist, :] = jnp.zeros((ext_l.shape[0], hist, LANES), F32)
        ext_s[:, 0:hist, :] = jnp.zeros((ext_s.shape[0], hist, LANES), F32)
        hcar[...] = jnp.zeros_like(hcar)
        ht[...] = jnp.zeros_like(ht)

    for s in range(ext_l.shape[0]):
        ext_l[s, hist:hist + tc, :] = lx_ref[:, LANES * s:LANES * (s + 1)]
    for s in range(ext_s.shape[0]):
        ext_s[s, hist:hist + tc, :] = xbc_ref[:, LANES * s:LANES * (s + 1)]

    u = _conv_slabs(ext_l, lcw_ref, lcb_ref, tc, hist - (CONV_WIDTH - 1))
    sp_lam = _softplus(-lam_ref[...])
    a, b = _lru_coeffs(u, wg_ref, ba_ref[...], bx_ref[...], sp_lam)
    hseq = _lru_scan_strided(a, b, hcar, a_pad, b_pad, h_pad)
    y_ref[:, 0:LRU_WIDTH] = _rms(hseq * _gelu_tanh(gate_ref[...]), glru_ref[...])

    xbc = _conv_slabs(ext_s, scw_ref, scb_ref, tc, hist - (CONV_WIDTH - 1))
    xbc = xbc * _sigmoid(xbc)
    dt_all = _softplus(dt_ref[...] + dtb_ref[...])
    lane1 = lax.broadcasted_iota(jnp.int32, (1, LANES), 1)
    a2_row = jnp.where(lane1 < N_SSD_HEADS, -LOG2E * jnp.exp(alog_ref[...]), 0.0)

    L = SSD_CHUNK
    rr = lax.broadcasted_iota(jnp.int32, (L, L), 0)
    cc = lax.broadcasted_iota(jnp.int32, (L, L), 1)
    causal = cc <= rr
    tri = jnp.where(causal, 1.0, 0.0).astype(F32)
    mask_add = jnp.where(causal, 0.0, NEG_BIG).astype(F32)

    for c in range(tc // L):
        rows = slice(L * c, L * (c + 1))
        xs = xbc[rows, 0:SSD_WIDTH]
        bm = xbc[rows, SSD_WIDTH:SSD_WIDTH + N_SSD_GROUPS * D_STATE]
        cm = xbc[rows, SSD_WIDTH + N_SSD_GROUPS * D_STATE:]
        dt = dt_all[rows, :]
        y_diag, cum2, _ = _ssd_intra(xs, bm, cm, dt, a2_row, tri, mask_add)
        ecol = _expand_heads(jnp.exp2(cum2), L)
        xw = xs * _expand_heads(jnp.exp2(cum2[L - 1:L, :] - cum2) * dt, L)
        dec = ecol[L - 1:L, :]
        y_off_parts = []
        half = SSD_WIDTH // N_SSD_GROUPS
        for g in range(N_SSD_GROUPS):
            htg = ht[g]
            cg = cm[:, D_STATE * g:D_STATE * (g + 1)].astype(BF16)
            y_off_parts.append(jnp.dot(cg, htg.astype(BF16), preferred_element_type=F32))
            bg_t = bm[:, D_STATE * g:D_STATE * (g + 1)].T.astype(BF16)
            st = jnp.dot(bg_t, xw[:, half * g:half * (g + 1)].astype(BF16), preferred_element_type=F32)
            ht[g] = htg * dec[:, half * g:half * (g + 1)] + st
        ys = y_diag + jnp.concatenate(y_off_parts, axis=1) * ecol
        y_ref[rows, LRU_WIDTH:MIX_WIDTH] = _ssd_gate_norm(
            ys, xs, z_ref[rows, :], dskip_ref[...], gssd_ref[...])

    @pl.when(t == nt - 1)
    def _final():
        last = slice(hist + tc - (CONV_WIDTH - 1), hist + tc)
        for s in range(ext_l.shape[0]):
            olc_ref[:, LANES * s:LANES * (s + 1)] = ext_l[s, last, :]
        for s in range(ext_s.shape[0]):
            osc_ref[:, LANES * s:LANES * (s + 1)] = ext_s[s, last, :]
        olh_ref[...] = hcar[0:1, :]
        half = SSD_WIDTH // N_SSD_GROUPS
        for g in range(N_SSD_GROUPS):
            osh_ref[half * g:half * (g + 1), :] = ht[g].T

    tail_l = ext_l[:, tc:tc + hist, :]
    tail_s = ext_s[:, tc:tc + hist, :]
    ext_l[:, 0:hist, :] = tail_l
    ext_s[:, 0:hist, :] = tail_s


def _param_specs(const):
    return [
        pl.BlockSpec((CONV_WIDTH, LRU_WIDTH), const),
        pl.BlockSpec((1, LRU_WIDTH), const),
        pl.BlockSpec((LRU_WIDTH // MXU_DIM, MXU_DIM, 2 * MXU_DIM), lambda *_: (0, 0, 0)),
        pl.BlockSpec((1, LRU_WIDTH), const),
        pl.BlockSpec((1, LRU_WIDTH), const),
        pl.BlockSpec((1, LRU_WIDTH), const),
        pl.BlockSpec((1, LRU_WIDTH), const),
        pl.BlockSpec((CONV_WIDTH, SSD_CONV_DIM), const),
        pl.BlockSpec((1, SSD_CONV_DIM), const),
        pl.BlockSpec((1, DT_PAD), const),
        pl.BlockSpec((1, DT_PAD), const),
        pl.BlockSpec((1, SSD_WIDTH), const),
        pl.BlockSpec((1, SSD_WIDTH), const),
    ]


def _mixer_prompt(proj, params):
    bsz, seq, _ = proj.shape
    tc = PROMPT_TC
    const = lambda b, t: (0, 0)
    in_specs = [
        pl.BlockSpec((None, tc, LRU_WIDTH), lambda b, t: (b, t, 0)),
        pl.BlockSpec((None, tc, LRU_WIDTH), lambda b, t: (b, t, 1)),
        pl.BlockSpec((None, tc, SSD_WIDTH), lambda b, t: (b, t, 2)),
        pl.BlockSpec((None, tc, SSD_CONV_DIM), lambda b, t: (b, t, 2)),
        pl.BlockSpec((None, tc, DT_PAD), lambda b, t: (b, t, PROJ_MAIN // DT_PAD)),
    ] + _param_specs(const)
    out_shape = (
        jax.ShapeDtypeStruct((bsz, seq, MIX_WIDTH), F32),
        jax.ShapeDtypeStruct((bsz, CONV_WIDTH - 1, LRU_WIDTH), F32),
        jax.ShapeDtypeStruct((bsz, 1, LRU_WIDTH), F32),
        jax.ShapeDtypeStruct((bsz, CONV_WIDTH - 1, SSD_CONV_DIM), F32),
        jax.ShapeDtypeStruct((bsz, SSD_WIDTH, D_STATE), F32),
    )
    out_specs = (
        pl.BlockSpec((None, tc, MIX_WIDTH), lambda b, t: (b, t, 0)),
        pl.BlockSpec((None, CONV_WIDTH - 1, LRU_WIDTH), lambda b, t: (b, 0, 0)),
        pl.BlockSpec((None, 1, LRU_WIDTH), lambda b, t: (b, 0, 0)),
        pl.BlockSpec((None, CONV_WIDTH - 1, SSD_CONV_DIM), lambda b, t: (b, 0, 0)),
        pl.BlockSpec((None, SSD_WIDTH, D_STATE), lambda b, t: (b, 0, 0)),
    )
    scratch = [
        pltpu.VMEM((LRU_WIDTH // LANES, SUBLANES + tc, LANES), F32),
        pltpu.VMEM((SSD_CONV_DIM // LANES, SUBLANES + tc, LANES), F32),
        pltpu.VMEM((LRU_WIDTH // LANES, SUBLANES * SCAN_PITCH, LANES), F32),
        pltpu.VMEM((LRU_WIDTH // LANES, SUBLANES * SCAN_PITCH, LANES), F32),
        pltpu.VMEM((LRU_WIDTH // LANES, SUBLANES * SCAN_PITCH, LANES), F32),
        pltpu.VMEM((SUBLANES, LRU_WIDTH), F32),
        pltpu.VMEM((N_SSD_GROUPS, D_STATE, SSD_WIDTH // N_SSD_GROUPS), F32),
    ]
    return pl.pallas_call(
        _mixer_prompt_kernel,
        out_shape=out_shape,
        grid=(bsz, seq // tc),
        in_specs=in_specs,
        out_specs=out_specs,
        scratch_shapes=scratch,
        compiler_params=pltpu.CompilerParams(
            dimension_semantics=("parallel", "arbitrary"), vmem_limit_bytes=VMEM_LIMIT),
        name="mixer_prompt",
    )(proj, proj, proj, proj, proj, *params)


def _mixer_sample_kernel(lx_ref, gate_ref, z_ref, xbc_ref, dt_ref,
                         slc_ref, slh_ref, ssc_ref, ssh_ref,
                         lcw_ref, lcb_ref, wg_ref, ba_ref, bx_ref, lam_ref, glru_ref,
                         scw_ref, scb_ref, dtb_ref, alog_ref, dskip_ref, gssd_ref,
                         y_ref, olc_ref, olh_ref, osc_ref, osh_ref,
                         ext_l, ext_s, pad_scr, yoff_scr):
    S = SAMPLE_SEQS
    P = SUBLANES
    T = lx_ref.shape[1]
    K1 = CONV_WIDTH - 1
    R = S * P
    row_i = lax.broadcasted_iota(jnp.int32, (R, 1), 0) & (P - 1)
    valid = row_i < T

    def pad_rows(ref):
        width = ref.shape[-1]
        pad_scr[:, :, 0:width] = jnp.zeros((S, P, width), F32)
        pad_scr[:, 0:T, 0:width] = ref[...]
        return pad_scr[:, :, 0:width].reshape(R, width)

    ext_l[...] = jnp.zeros_like(ext_l)
    ext_s[...] = jnp.zeros_like(ext_s)
    ext_l[:, 0:K1, :] = slc_ref[...]
    ext_l[:, K1:K1 + T, :] = lx_ref[...]
    ext_s[:, 0:K1, :] = ssc_ref[...]
    ext_s[:, K1:K1 + T, :] = xbc_ref[...]
    olc_ref[...] = ext_l[:, T:T + K1, :]
    osc_ref[...] = ext_s[:, T:T + K1, :]

    el = ext_l[...].reshape(R, LRU_WIDTH)
    es = ext_s[...].reshape(R, SSD_CONV_DIM)

    def conv(e, w_ref, b_ref):
        out = b_ref[...] + e * w_ref[0:1, :]
        for k in range(1, CONV_WIDTH):
            out = out + pltpu.roll(e, R - k, axis=0) * w_ref[k:k + 1, :]
        return out

    u = conv(el, lcw_ref, lcb_ref)
    sp_lam = _softplus(-lam_ref[...])
    a, b = _lru_coeffs(u, wg_ref, ba_ref[...], bx_ref[...], sp_lam)
    a, b = _scan_within_8(a, b)
    h0 = jnp.broadcast_to(slh_ref[...], (S, P, LRU_WIDTH)).reshape(R, LRU_WIDTH)
    hseq = a * h0 + b
    olh_ref[...] = hseq.reshape(S, P, LRU_WIDTH)[:, T - 1:T, :]
    gate = pad_rows(gate_ref)
    y_lru = _rms(hseq * _gelu_tanh(gate), glru_ref[...])

    xbc = conv(es, scw_ref, scb_ref)
    xbc = xbc * _sigmoid(xbc)
    xs = xbc[:, 0:SSD_WIDTH]
    bm = xbc[:, SSD_WIDTH:SSD_WIDTH + N_SSD_GROUPS * D_STATE]
    cm = xbc[:, SSD_WIDTH + N_SSD_GROUPS * D_STATE:]
    dt_raw = pad_rows(dt_ref)
    dt = jnp.where(valid, _softplus(dt_raw + dtb_ref[...]), 0.0)
    lane1 = lax.broadcasted_iota(jnp.int32, (1, LANES), 1)
    a2_row = jnp.where(lane1 < N_SSD_HEADS, -LOG2E * jnp.exp(alog_ref[...]), 0.0)

    rr = lax.broadcasted_iota(jnp.int32, (R, R), 0)
    cc = lax.broadcasted_iota(jnp.int32, (R, R), 1)
    allowed = (cc <= rr) & ((rr - cc) <= (rr & (P - 1)))
    tri = jnp.where(allowed, 1.0, 0.0).astype(F32)
    mask_add = jnp.where(allowed, 0.0, NEG_BIG).astype(F32)

    y_diag, cum2, cum2_t = _ssd_intra(xs, bm, cm, dt, a2_row, tri, mask_add)
    ecol = _expand_heads(jnp.exp2(cum2), R)
    end2 = jnp.broadcast_to(cum2.reshape(S, P, LANES)[:, P - 1:P, :], (S, P, LANES)).reshape(R, LANES)
    xw = xs * _expand_heads(jnp.exp2(end2 - cum2) * dt, R)
    ecum_t = jnp.exp2(cum2_t)

    half = SSD_WIDTH // N_SSD_GROUPS
    for q in range(S):
        r0 = P * q
        vq = jnp.broadcast_to(ecum_t[:, r0 + P - 1:r0 + P], (N_SSD_HEADS, LANES))
        for g in range(N_SSD_GROUPS):
            hqg = ssh_ref[q, half * g:half * (g + 1), :]
            cq = cm[r0:r0 + P, D_STATE * g:D_STATE * (g + 1)].astype(BF16)
            yoff_scr[r0:r0 + P, half * g:half * (g + 1)] = lax.dot_general(
                cq, hqg.astype(BF16), (((1,), (1,)), ((), ())), preferred_element_type=F32)
            bq = bm[r0:r0 + P, D_STATE * g:D_STATE * (g + 1)].astype(BF16)
            xq = xw[r0:r0 + P, half * g:half * (g + 1)].astype(BF16)
            st = lax.dot_general(xq, bq, (((0,), (0,)), ((), ())), preferred_element_type=F32)
            for e in range(N_SSD_HEADS // N_SSD_GROUPS):
                h = (N_SSD_HEADS // N_SSD_GROUPS) * g + e
                lo_r = SSD_HEAD_DIM * e
                osh_ref[q, SSD_HEAD_DIM * h:SSD_HEAD_DIM * (h + 1), :] = (
                    vq[h:h + 1, :] * hqg[lo_r:lo_r + SSD_HEAD_DIM, :] + st[lo_r:lo_r + SSD_HEAD_DIM, :])

    ys = y_diag + yoff_scr[...] * ecol
    z = pad_rows(z_ref)
    y_ssd = _ssd_gate_norm(ys, xs, z, dskip_ref[...], gssd_ref[...])
    y_ref[:, :, 0:LRU_WIDTH] = y_lru.reshape(S, P, LRU_WIDTH)[:, 0:T, :]
    y_ref[:, :, LRU_WIDTH:MIX_WIDTH] = y_ssd.reshape(S, P, SSD_WIDTH)[:, 0:T, :]


def _mixer_sample(proj, st_lc, st_lh, st_sc, st_sh, params):
    nseq, T, _ = proj.shape
    S = SAMPLE_SEQS
    const = lambda i: (0, 0)
    in_specs = [
        pl.BlockSpec((S, T, LRU_WIDTH), lambda i: (i, 0, 0)),
        pl.BlockSpec((S, T, LRU_WIDTH), lambda i: (i, 0, 1)),
        pl.BlockSpec((S, T, SSD_WIDTH), lambda i: (i, 0, 2)),
        pl.BlockSpec((S, T, SSD_CONV_DIM), lambda i: (i, 0, 2)),
        pl.BlockSpec((S, T, DT_PAD), lambda i: (i, 0, PROJ_MAIN // DT_PAD)),
        pl.BlockSpec((S, CONV_WIDTH - 1, LRU_WIDTH), lambda i: (i, 0, 0)),
        pl.BlockSpec((S, 1, LRU_WIDTH), lambda i: (i, 0, 0)),
        pl.BlockSpec((S, CONV_WIDTH - 1, SSD_CONV_DIM), lambda i: (i, 0, 0)),
        pl.BlockSpec((S, SSD_WIDTH, D_STATE), lambda i: (i, 0, 0)),
    ] + _param_specs(const)
    out_shape = (
        jax.ShapeDtypeStruct((nseq, T, MIX_WIDTH), F32),
        jax.ShapeDtypeStruct((nseq, CONV_WIDTH - 1, LRU_WIDTH), F32),
        jax.ShapeDtypeStruct((nseq, 1, LRU_WIDTH), F32),
        jax.ShapeDtypeStruct((nseq, CONV_WIDTH - 1, SSD_CONV_DIM), F32),
        jax.ShapeDtypeStruct((nseq, SSD_WIDTH, D_STATE), F32),
    )
    out_specs = (
        pl.BlockSpec((S, T, MIX_WIDTH), lambda i: (i, 0, 0)),
        pl.BlockSpec((S, CONV_WIDTH - 1, LRU_WIDTH), lambda i: (i, 0, 0)),
        pl.BlockSpec((S, 1, LRU_WIDTH), lambda i: (i, 0, 0)),
        pl.BlockSpec((S, CONV_WIDTH - 1, SSD_CONV_DIM), lambda i: (i, 0, 0)),
        pl.BlockSpec((S, SSD_WIDTH, D_STATE), lambda i: (i, 0, 0)),
    )
    scratch = [
        pltpu.VMEM((S, SUBLANES, LRU_WIDTH), F32),
        pltpu.VMEM((S, SUBLANES, SSD_CONV_DIM), F32),
        pltpu.VMEM((S, SUBLANES, LRU_WIDTH), F32),
        pltpu.VMEM((S * SUBLANES, SSD_WIDTH), F32),
    ]
    return pl.pallas_call(
        _mixer_sample_kernel,
        out_shape=out_shape,
        grid=(nseq // S,),
        in_specs=in_specs,
        out_specs=out_specs,
        scratch_shapes=scratch,
        compiler_params=pltpu.CompilerParams(
            dimension_semantics=("parallel",), vmem_limit_bytes=VMEM_LIMIT),
        name="mixer_sample",
    )(proj, proj, proj, proj, proj, st_lc, st_lh, st_sc, st_sh, *params)


def _gate_weights(w_a, w_x):
    def tiles(w):
        per = MXU_DIM // LRU_BLOCK
        w4 = w.reshape(N_LRU_HEADS // per, per, LRU_BLOCK, LRU_BLOCK)
        eye = jnp.eye(per, dtype=w.dtype)
        t = jnp.einsum('jaik,ab->jaibk', w4, eye)
        return t.reshape(N_LRU_HEADS // per, MXU_DIM, MXU_DIM)
    return jnp.concatenate([tiles(w_a), tiles(w_x)], axis=2).astype(BF16)


def kernel(x_prompt, x_sample, state_lru_conv, state_lru_h, state_ssd_conv, state_ssd_h, g_mix, w_in,
           lru_conv_w, lru_conv_b, w_a, b_a, w_x, b_x, lam, g_lru_out, ssd_conv_w, ssd_conv_b, dt_bias,
           a_log, d_skip, g_ssd_out, w_out, g_mlp, w_up, w_down, g_final):
    depth = w_in.shape[0]
    assert depth == 1
    bp, seq, _ = x_prompt.shape
    bs, dseq, _ = x_sample.shape
    l = 0
    row = lambda v: v.reshape(1, -1)
    w_main = w_in[l, :, 0:PROJ_MAIN].astype(BF16)
    w_dt = jnp.pad(w_in[l, :, PROJ_MAIN:], ((0, 0), (0, DT_PAD - N_SSD_HEADS))).astype(BF16)
    params = (
        lru_conv_w[l], row(lru_conv_b[l]), _gate_weights(w_a[l], w_x[l]),
        row(b_a[l]), row(b_x[l]), row(lam[l]), row(g_lru_out[l]),
        ssd_conv_w[l], row(ssd_conv_b[l]),
        jnp.pad(row(dt_bias[l]), ((0, 0), (0, DT_PAD - N_SSD_HEADS))),
        jnp.pad(row(a_log[l]), ((0, 0), (0, DT_PAD - N_SSD_HEADS))),
        row(jnp.repeat(d_skip[l], SSD_HEAD_DIM)), row(g_ssd_out[l]),
    )
    w_out_b = w_out[l].astype(BF16)
    w_up_b = w_up[l].astype(BF16)
    w_down_b = w_down[l].astype(BF16)
    gmix = row(g_mix[l])
    gmlp = row(g_mlp[l])
    gfin = row(g_final)

    xp2 = x_prompt.reshape(bp * seq, D_MODEL)
    proj_p = _in_proj(xp2, gmix, w_main, w_dt).reshape(bp, seq, PROJ_PAD)
    ymix_p, p_lc, p_lh, p_sc, p_sh = _mixer_prompt(proj_p, params)
    y_prompt = _out_mlp(xp2, ymix_p.reshape(bp * seq, MIX_WIDTH), w_out_b, gmlp, w_up_b, w_down_b, gfin)

    xs2 = x_sample.reshape(bs * dseq, D_MODEL)
    proj_s = _in_proj(xs2, gmix, w_main, w_dt).reshape(bs, dseq, PROJ_PAD)
    ymix_s, s_lc, s_lh, s_sc, s_sh = _mixer_sample(
        proj_s, state_lru_conv[l], state_lru_h[l].reshape(bs, 1, LRU_WIDTH), state_ssd_conv[l],
        state_ssd_h[l].reshape(bs, SSD_WIDTH, D_STATE), params)
    y_sample = _out_mlp(xs2, ymix_s.reshape(bs * dseq, MIX_WIDTH), w_out_b, gmlp, w_up_b, w_down_b, gfin)

    hshape = (N_SSD_HEADS, SSD_HEAD_DIM, D_STATE)
    return (
        y_prompt.reshape(bp, seq, D_MODEL), y_sample.reshape(bs, dseq, D_MODEL),
        p_lc[None], p_lh.reshape(1, bp, LRU_WIDTH), p_sc[None], p_sh.reshape(1, bp, *hshape),
        s_lc[None], s_lh.reshape(1, bs, LRU_WIDTH), s_sc[None], s_sh.reshape(1, bs, *hshape),
    )
```

```python
import functools
import math

import jax
import jax.numpy as jnp
from jax import lax
from jax.experimental import pallas as pl
from jax.experimental.pallas import tpu as pltpu

F32 = jnp.float32
BF16 = jnp.bfloat16

D_MODEL = 1024
LRU_WIDTH = 1024
N_LRU_HEADS = 16
LRU_BLOCK = 64
LRU_C = 8.0
SSD_WIDTH = 1024
SSD_HEAD_DIM = 64
N_SSD_HEADS = 16
N_SSD_GROUPS = 2
D_STATE = 128
CONV_WIDTH = 4
SSD_CONV_DIM = SSD_WIDTH + 2 * N_SSD_GROUPS * D_STATE
D_FF = 4 * D_MODEL
EPS = 1e-6

LANES = 128
SUBLANES = 8
MXU_DIM = 256
DT_PAD = LANES
PROJ_MAIN = 2 * LRU_WIDTH + SSD_WIDTH + SSD_CONV_DIM
PROJ_PAD = PROJ_MAIN + DT_PAD
MIX_WIDTH = LRU_WIDTH + SSD_WIDTH
SSD_CHUNK = 128
PROMPT_TC = 256
ROW_TILE = 512
SAMPLE_SEQS = SSD_CHUNK // SUBLANES
SCAN_RUN = PROMPT_TC // SUBLANES
SCAN_PITCH = SCAN_RUN + 4
NEG_BIG = -1e30
LOG2E = 1.4426950408889634
VMEM_LIMIT = 56 * 1024 * 1024
VMEM_LIMIT_FUSED = 56 * 1024 * 1024
HI = lax.Precision.HIGHEST


def _rms(x, g):
    ms = jnp.mean(x * x, axis=-1, keepdims=True)
    return x * lax.rsqrt(ms + EPS) * g


def _sigmoid(x):
    return 1.0 / (1.0 + jnp.exp(-x))


def _softplus(x):
    return jnp.maximum(x, 0.0) + jnp.log1p(jnp.exp(-jnp.abs(x)))


def _gelu_tanh(x):
    c = math.sqrt(2.0 / math.pi)
    return 0.5 * x * (1.0 + jnp.tanh(c * (x + 0.044715 * (x * x * x))))


def _lru_coeffs(u, wg_ref, b_a, b_x, sp_lam):
    ub = u.astype(BF16)
    r_parts, i_parts = [], []
    for j in range(LRU_WIDTH // MXU_DIM):
        g = jnp.dot(ub[:, MXU_DIM * j:MXU_DIM * (j + 1)], wg_ref[j], preferred_element_type=F32)
        r_parts.append(g[:, :MXU_DIM])
        i_parts.append(g[:, MXU_DIM:])
    r = _sigmoid(jnp.concatenate(r_parts, axis=1) + b_a)
    i = _sigmoid(jnp.concatenate(i_parts, axis=1) + b_x)
    log_a = (-LRU_C) * r * sp_lam
    a = jnp.exp(log_a)
    th = jnp.tanh(-log_a)
    v = 2.0 * th / (1.0 + th)
    mult = jnp.where(v > 0.0, v * lax.rsqrt(v), 0.0)
    return a, mult * (i * u)


def _scan_within_8(a, b):
    ridx = lax.broadcasted_iota(jnp.int32, a.shape, 0) & (SUBLANES - 1)
    for k in (1, 2, 4):
        a_s = pltpu.roll(a, k, axis=0)
        b_s = pltpu.roll(b, k, axis=0)
        m = ridx >= k
        b = jnp.where(m, a * b_s + b, b)
        a = jnp.where(m, a * a_s, a)
    return a, b


def _conv_slabs(ext, w_ref, b_ref, rows, first):
    parts = []
    for s in range(ext.shape[0]):
        cols = slice(LANES * s, LANES * (s + 1))
        acc = b_ref[:, cols] + ext[s, pl.ds(first, rows), :] * w_ref[0:1, cols]
        for k in range(1, CONV_WIDTH):
            acc = acc + ext[s, pl.ds(first + k, rows), :] * w_ref[k:k + 1, cols]
        parts.append(acc)
    return jnp.concatenate(parts, axis=1)


def _lru_scan_strided(a, b, hcar, a_pad, b_pad, h_pad):
    rows = a.shape[0]
    S = rows // SUBLANES
    ridx = lax.broadcasted_iota(jnp.int32, (SUBLANES, LANES), 0)
    outs = []
    for s in range(LRU_WIDTH // LANES):
        cols = slice(LANES * s, LANES * (s + 1))
        for j in range(SUBLANES):
            a_pad[s, SCAN_PITCH * j:SCAN_PITCH * j + S, :] = a[S * j:S * (j + 1), cols]
            b_pad[s, SCAN_PITCH * j:SCAN_PITCH * j + S, :] = b[S * j:S * (j + 1), cols]
        h = jnp.zeros((SUBLANES, LANES), F32)
        prod = jnp.ones((SUBLANES, LANES), F32)
        for i in range(S):
            av = a_pad[s, pl.ds(i, SUBLANES, stride=SCAN_PITCH), :]
            h = av * h + b_pad[s, pl.ds(i, SUBLANES, stride=SCAN_PITCH), :]
            prod = av * prod
        pcum, hcum = _scan_within_8(prod, h)
        cin = hcar[:, cols]
        ends = hcum + pcum * cin
        h = jnp.where(ridx == 0, cin, pltpu.roll(ends, 1, axis=0))
        for i in range(S):
            av = a_pad[s, pl.ds(i, SUBLANES, stride=SCAN_PITCH), :]
            h = av * h + b_pad[s, pl.ds(i, SUBLANES, stride=SCAN_PITCH), :]
            h_pad[s, pl.ds(i, SUBLANES, stride=SCAN_PITCH), :] = h
        hcar[:, cols] = jnp.broadcast_to(ends[SUBLANES - 1:SUBLANES, :], (SUBLANES, LANES))
        outs.append(jnp.concatenate(
            [h_pad[s, SCAN_PITCH * j:SCAN_PITCH * j + S, :] for j in range(SUBLANES)], axis=0))
    return jnp.concatenate(outs, axis=1)


def _ssd_intra(xs, bm, cm, dt, a2_row, tri, mask_add):
    L = xs.shape[0]
    cum2 = jnp.dot(tri, dt * a2_row, precision=HI, preferred_element_type=F32)
    cum2_t = cum2.T[0:N_SSD_HEADS, :]
    c2_t = cum2_t - jnp.log2(dt.T[0:N_SSD_HEADS, :])
    lane = lax.broadcasted_iota(jnp.int32, (L, LANES), 1)
    lo = lane < SSD_HEAD_DIM
    y_parts = []
    for g in range(N_SSD_GROUPS):
        bg = bm[:, D_STATE * g:D_STATE * (g + 1)].astype(BF16)
        cg = cm[:, D_STATE * g:D_STATE * (g + 1)].astype(BF16)
        cb = lax.dot_general(cg, bg, (((1,), (1,)), ((), ())), preferred_element_type=F32)
        for jj in range(N_SSD_HEADS // N_SSD_GROUPS // 2):
            j = (N_SSD_HEADS // N_SSD_GROUPS // 2) * g + jj
            h0, h1 = 2 * j, 2 * j + 1
            col0 = jnp.broadcast_to(cum2[:, h0:h0 + 1], (L, LANES))
            col1 = jnp.broadcast_to(cum2[:, h1:h1 + 1], (L, LANES))
            m0 = cb * jnp.exp2(col0 - c2_t[h0:h0 + 1, :] + mask_add)
            m1 = cb * jnp.exp2(col1 - c2_t[h1:h1 + 1, :] + mask_add)
            lhs = jnp.concatenate([m0, m1], axis=1).astype(BF16)
            xp = xs[:, LANES * j:LANES * (j + 1)]
            rhs = jnp.concatenate([jnp.where(lo, xp, 0.0), jnp.where(lo, 0.0, xp)], axis=0).astype(BF16)
            y_parts.append(jnp.dot(lhs, rhs, preferred_element_type=F32))
    return jnp.concatenate(y_parts, axis=1), cum2, cum2_t


def _expand_heads(v, L):
    lane = lax.broadcasted_iota(jnp.int32, (L, LANES), 1)
    lo = lane < SSD_HEAD_DIM
    parts = []
    for j in range(N_SSD_HEADS // 2):
        c0 = jnp.broadcast_to(v[:, 2 * j:2 * j + 1], (L, LANES))
        c1 = jnp.broadcast_to(v[:, 2 * j + 1:2 * j + 2], (L, LANES))
        parts.append(jnp.where(lo, c0, c1))
    return jnp.concatenate(parts, axis=1)


def _ssd_gate_norm(ys, xs, z, dskip, g_ssd):
    ys = ys + dskip * xs
    gated = ys * (z * _sigmoid(z))
    half = SSD_WIDTH // N_SSD_GROUPS
    outs = []
    for g in range(N_SSD_GROUPS):
        outs.append(_rms(gated[:, half * g:half * (g + 1)], g_ssd[:, half * g:half * (g + 1)]))
    return jnp.concatenate(outs, axis=1)


def _inproj_kernel(x_ref, g_ref, w_ref, wdt_ref, o_ref):
    hn = _rms(x_ref[...], g_ref[...]).astype(BF16)
    o_ref[:, 0:PROJ_MAIN] = jnp.dot(hn, w_ref[...], preferred_element_type=F32)
    o_ref[:, PROJ_MAIN:PROJ_PAD] = jnp.dot(hn, wdt_ref[...], preferred_element_type=F32)


def _in_proj(x2d, g_mix, w_main, w_dt):
    n = x2d.shape[0]
    return pl.pallas_call(
        _inproj_kernel,
        out_shape=jax.ShapeDtypeStruct((n, PROJ_PAD), F32),
        grid=(n // ROW_TILE,),
        in_specs=[
            pl.BlockSpec((ROW_TILE, D_MODEL), lambda i: (i, 0)),
            pl.BlockSpec((1, D_MODEL), lambda i: (0, 0)),
            pl.BlockSpec((D_MODEL, PROJ_MAIN), lambda i: (0, 0), pipeline_mode=pl.Buffered(1)),
            pl.BlockSpec((D_MODEL, DT_PAD), lambda i: (0, 0), pipeline_mode=pl.Buffered(1)),
        ],
        out_specs=pl.BlockSpec((ROW_TILE, PROJ_PAD), lambda i: (i, 0)),
        compiler_params=pltpu.CompilerParams(
            dimension_semantics=("parallel",), vmem_limit_bytes=VMEM_LIMIT),
        name="in_proj",
    )(x2d, g_mix, w_main, w_dt)


def _outmlp_kernel(x_ref, y_ref, wo_ref, gm_ref, wu_ref, wd_ref, gf_ref, o_ref):
    x1 = x_ref[...] + jnp.dot(y_ref[...].astype(BF16), wo_ref[...], preferred_element_type=F32)
    m = _rms(x1, gm_ref[...]).astype(BF16)
    u = jnp.dot(m, wu_ref[...], preferred_element_type=F32)
    u = jnp.square(jnp.maximum(u, 0.0)).astype(BF16)
    x2 = x1 + jnp.dot(u, wd_ref[...], preferred_element_type=F32)
    o_ref[...] = _rms(x2, gf_ref[...])


def _out_mlp(x2d, ymix2d, w_out_b, g_mlp, w_up_b, w_down_b, g_final):
    n = x2d.shape[0]
    const = lambda i: (0, 0)
    return pl.pallas_call(
        _outmlp_kernel,
        out_shape=jax.ShapeDtypeStruct((n, D_MODEL), F32),
        grid=(n // ROW_TILE,),
        in_specs=[
            pl.BlockSpec((ROW_TILE, D_MODEL), lambda i: (i, 0)),
            pl.BlockSpec((ROW_TILE, MIX_WIDTH), lambda i: (i, 0)),
            pl.BlockSpec((MIX_WIDTH, D_MODEL), const, pipeline_mode=pl.Buffered(1)),
            pl.BlockSpec((1, D_MODEL), const),
            pl.BlockSpec((D_MODEL, D_FF), const, pipeline_mode=pl.Buffered(1)),
            pl.BlockSpec((D_FF, D_MODEL), const, pipeline_mode=pl.Buffered(1)),
            pl.BlockSpec((1, D_MODEL), const),
        ],
        out_specs=pl.BlockSpec((ROW_TILE, D_MODEL), lambda i: (i, 0)),
        compiler_params=pltpu.CompilerParams(
            dimension_semantics=("parallel",), vmem_limit_bytes=VMEM_LIMIT),
        name="out_mlp",
    )(x2d, ymix2d, w_out_b, g_mlp, w_up_b, w_down_b, g_final)


def _layer_prompt_kernel(x_ref, gmix_ref, w_ref, wdt_ref,
                         lcw_ref, lcb_ref, wg_ref, ba_ref, bx_ref, lam_ref, glru_ref,
                         scw_ref, scb_ref, dtb_ref, alog_ref, dskip_ref, gssd_ref,
                         wo_ref, gmlp_ref, wu_ref, wd_ref, gfin_ref,
                         o_ref, olc_ref, olh_ref, osc_ref, osh_ref,
                         ext_l, ext_s, gate_scr, z_scr, a_pad, b_pad, h_pad, hcar, ht):
    t = pl.program_id(1)
    nt = pl.num_programs(1)
    tc = PROMPT_TC
    hist = SUBLANES

    @pl.when(t == 0)
    def _init():
        ext_l[:, 0:hist, :] = jnp.zeros((ext_l.shape[0], hist, LANES), F32)
        ext_s[:, 0:hist, :] = jnp.zeros((ext_s.shape[0], hist, LANES), F32)
        hcar[...] = jnp.zeros_like(hcar)
        ht[...] = jnp.zeros_like(ht)

    hn = _rms(x_ref[...], gmix_ref[...]).astype(BF16)
    o1, o2, o3 = LRU_WIDTH, 2 * LRU_WIDTH, 2 * LRU_WIDTH + SSD_WIDTH
    lx = jnp.dot(hn, w_ref[:, 0:o1], preferred_element_type=F32)
    for s in range(ext_l.shape[0]):
        ext_l[s, hist:hist + tc, :] = lx[:, LANES * s:LANES * (s + 1)]
    gate_scr[...] = jnp.dot(hn, w_ref[:, o1:o2], preferred_element_type=F32)
    xbc_in = jnp.dot(hn, w_ref[:, o3:PROJ_MAIN], preferred_element_type=F32)
    for s in range(ext_s.shape[0]):
        ext_s[s, hist:hist + tc, :] = xbc_in[:, LANES * s:LANES * (s + 1)]
    z_scr[...] = jnp.dot(hn, w_ref[:, o2:o3], preferred_element_type=F32)
    dt_raw = jnp.dot(hn, wdt_ref[...], preferred_element_type=F32)

    u = _conv_slabs(ext_l, lcw_ref, lcb_ref, tc, hist - (CONV_WIDTH - 1))
    sp_lam = _softplus(-lam_ref[...])
    a, b = _lru_coeffs(u, wg_ref, ba_ref[...], bx_ref[...], sp_lam)
    hseq = _lru_scan_strided(a, b, hcar, a_pad, b_pad, h_pad)
    y_lru = _rms(hseq * _gelu_tanh(gate_scr[...]), glru_ref[...]).astype(BF16)

    xbc = _conv_slabs(ext_s, scw_ref, scb_ref, tc, hist - (CONV_WIDTH - 1))
    xbc = xbc * _sigmoid(xbc)
    dt_all = _softplus(dt_raw + dtb_ref[...])
    lane1 = lax.broadcasted_iota(jnp.int32, (1, LANES), 1)
    a2_row = jnp.where(lane1 < N_SSD_HEADS, -LOG2E * jnp.exp(alog_ref[...]), 0.0)

    L = SSD_CHUNK
    rr = lax.broadcasted_iota(jnp.int32, (L, L), 0)
    cc = lax.broadcasted_iota(jnp.int32, (L, L), 1)
    causal = cc <= rr
    tri = jnp.where(causal, 1.0, 0.0).astype(F32)
    mask_add = jnp.where(causal, 0.0, NEG_BIG).astype(F32)

    y_ssd_parts = []
    for c in range(tc // L):
        rows = slice(L * c, L * (c + 1))
        xs = xbc[rows, 0:SSD_WIDTH]
        bm = xbc[rows, SSD_WIDTH:SSD_WIDTH + N_SSD_GROUPS * D_STATE]
        cm = xbc[rows, SSD_WIDTH + N_SSD_GROUPS * D_STATE:]
        dt = dt_all[rows, :]
        y_diag, cum2, _ = _ssd_intra(xs, bm, cm, dt, a2_row, tri, mask_add)
        ecol = _expand_heads(jnp.exp2(cum2), L)
        xw = xs * _expand_heads(jnp.exp2(cum2[L - 1:L, :] - cum2) * dt, L)
        dec = ecol[L - 1:L, :]
        y_off_parts = []
        half = SSD_WIDTH // N_SSD_GROUPS
        for g in range(N_SSD_GROUPS):
            htg = ht[g]
            cg = cm[:, D_STATE * g:D_STATE * (g + 1)].astype(BF16)
            y_off_parts.append(jnp.dot(cg, htg.astype(BF16), preferred_element_type=F32))
            bg_t = bm[:, D_STATE * g:D_STATE * (g + 1)].T.astype(BF16)
            st = jnp.dot(bg_t, xw[:, half * g:half * (g + 1)].astype(BF16), preferred_element_type=F32)
            ht[g] = htg * dec[:, half * g:half * (g + 1)] + st
        ys = y_diag + jnp.concatenate(y_off_parts, axis=1) * ecol
        y_ssd_parts.append(_ssd_gate_norm(
            ys, xs, z_scr[rows, :], dskip_ref[...], gssd_ref[...]).astype(BF16))
    y_ssd = jnp.concatenate(y_ssd_parts, axis=0)

    x1 = (x_ref[...] + jnp.dot(y_lru, wo_ref[0:LRU_WIDTH, :], preferred_element_type=F32)
          + jnp.dot(y_ssd, wo_ref[LRU_WIDTH:MIX_WIDTH, :], preferred_element_type=F32))
    m = _rms(x1, gmlp_ref[...]).astype(BF16)
    up = jnp.dot(m, wu_ref[...], preferred_element_type=F32)
    up = jnp.square(jnp.maximum(up, 0.0)).astype(BF16)
    x2 = x1 + jnp.dot(up, wd_ref[...], preferred_element_type=F32)
    o_ref[...] = _rms(x2, gfin_ref[...])

    @pl.when(t == nt - 1)
    def _final():
        last = slice(hist + tc - (CONV_WIDTH - 1), hist + tc)
        for s in range(ext_l.shape[0]):
            olc_ref[:, LANES * s:LANES * (s + 1)] = ext_l[s, last, :]
        for s in range(ext_s.shape[0]):
            osc_ref[:, LANES * s:LANES * (s + 1)] = ext_s[s, last, :]
        olh_ref[...] = hcar[0:1, :]
        half = SSD_WIDTH // N_SSD_GROUPS
        for g in range(N_SSD_GROUPS):
            osh_ref[half * g:half * (g + 1), :] = ht[g].T

    tail_l = ext_l[:, tc:tc + hist, :]
    tail_s = ext_s[:, tc:tc + hist, :]
    ext_l[:, 0:hist, :] = tail_l
    ext_s[:, 0:hist, :] = tail_s


def _param_specs(const):
    return [
        pl.BlockSpec((CONV_WIDTH, LRU_WIDTH), const),
        pl.BlockSpec((1, LRU_WIDTH), const),
        pl.BlockSpec((LRU_WIDTH // MXU_DIM, MXU_DIM, 2 * MXU_DIM), lambda *_: (0, 0, 0)),
        pl.BlockSpec((1, LRU_WIDTH), const),
        pl.BlockSpec((1, LRU_WIDTH), const),
        pl.BlockSpec((1, LRU_WIDTH), const),
        pl.BlockSpec((1, LRU_WIDTH), const),
        pl.BlockSpec((CONV_WIDTH, SSD_CONV_DIM), const),
        pl.BlockSpec((1, SSD_CONV_DIM), const),
        pl.BlockSpec((1, DT_PAD), const),
        pl.BlockSpec((1, DT_PAD), const),
        pl.BlockSpec((1, SSD_WIDTH), const),
        pl.BlockSpec((1, SSD_WIDTH), const),
    ]


def _layer_prompt(x, g_mix, w_main, w_dt, params, w_out_b, g_mlp, w_up_b, w_down_b, g_final):
    bsz, seq, _ = x.shape
    tc = PROMPT_TC
    const = lambda b, t: (0, 0)
    resident = lambda shape: pl.BlockSpec(shape, const, pipeline_mode=pl.Buffered(1))
    in_specs = [
        pl.BlockSpec((None, tc, D_MODEL), lambda b, t: (b, t, 0)),
        pl.BlockSpec((1, D_MODEL), const),
        resident((D_MODEL, PROJ_MAIN)),
        resident((D_MODEL, DT_PAD)),
    ] + _param_specs(const) + [
        resident((MIX_WIDTH, D_MODEL)),
        pl.BlockSpec((1, D_MODEL), const),
        resident((D_MODEL, D_FF)),
        resident((D_FF, D_MODEL)),
        pl.BlockSpec((1, D_MODEL), const),
    ]
    out_shape = (
        jax.ShapeDtypeStruct((bsz, seq, D_MODEL), F32),
        jax.ShapeDtypeStruct((bsz, CONV_WIDTH - 1, LRU_WIDTH), F32),
        jax.ShapeDtypeStruct((bsz, 1, LRU_WIDTH), F32),
        jax.ShapeDtypeStruct((bsz, CONV_WIDTH - 1, SSD_CONV_DIM), F32),
        jax.ShapeDtypeStruct((bsz, SSD_WIDTH, D_STATE), F32),
    )
    out_specs = (
        pl.BlockSpec((None, tc, D_MODEL), lambda b, t: (b, t, 0)),
        pl.BlockSpec((None, CONV_WIDTH - 1, LRU_WIDTH), lambda b, t: (b, 0, 0)),
        pl.BlockSpec((None, 1, LRU_WIDTH), lambda b, t: (b, 0, 0)),
        pl.BlockSpec((None, CONV_WIDTH - 1, SSD_CONV_DIM), lambda b, t: (b, 0, 0)),
        pl.BlockSpec((None, SSD_WIDTH, D_STATE), lambda b, t: (b, 0, 0)),
    )
    scratch = [
        pltpu.VMEM((LRU_WIDTH // LANES, SUBLANES + tc, LANES), F32),
        pltpu.VMEM((SSD_CONV_DIM // LANES, SUBLANES + tc, LANES), F32),
        pltpu.VMEM((tc, LRU_WIDTH), F32),
        pltpu.VMEM((tc, SSD_WIDTH), F32),
        pltpu.VMEM((LRU_WIDTH // LANES, SUBLANES * SCAN_PITCH, LANES), F32),
        pltpu.VMEM((LRU_WIDTH // LANES, SUBLANES * SCAN_PITCH, LANES), F32),
        pltpu.VMEM((LRU_WIDTH // LANES, SUBLANES * SCAN_PITCH, LANES), F32),
        pltpu.VMEM((SUBLANES, LRU_WIDTH), F32),
        pltpu.VMEM((N_SSD_GROUPS, D_STATE, SSD_WIDTH // N_SSD_GROUPS), F32),
    ]
    return pl.pallas_call(
        _layer_prompt_kernel,
        out_shape=out_shape,
        grid=(bsz, seq // tc),
        in_specs=in_specs,
        out_specs=out_specs,
        scratch_shapes=scratch,
        compiler_params=pltpu.CompilerParams(
            dimension_semantics=("parallel", "arbitrary"), vmem_limit_bytes=VMEM_LIMIT_FUSED),
        name="layer_prompt",
    )(x, g_mix, w_main, w_dt, *params, w_out_b, g_mlp, w_up_b, w_down_b, g_final)


def _mixer_sample_kernel(lx_ref, gate_ref, z_ref, xbc_ref, dt_ref,
                         slc_ref, slh_ref, ssc_ref, ssh_ref,
                         lcw_ref, lcb_ref, wg_ref, ba_ref, bx_ref, lam_ref, glru_ref,
                         scw_ref, scb_ref, dtb_ref, alog_ref, dskip_ref, gssd_ref,
                         y_ref, olc_ref, olh_ref, osc_ref, osh_ref,
                         ext_l, ext_s, pad_scr, yoff_scr):
    S = SAMPLE_SEQS
    P = SUBLANES
    T = lx_ref.shape[1]
    K1 = CONV_WIDTH - 1
    R = S * P
    row_i = lax.broadcasted_iota(jnp.int32, (R, 1), 0) & (P - 1)
    valid = row_i < T

    def pad_rows(ref):
        width = ref.shape[-1]
        pad_scr[:, :, 0:width] = jnp.zeros((S, P, width), F32)
        pad_scr[:, 0:T, 0:width] = ref[...]
        return pad_scr[:, :, 0:width].reshape(R, width)

    ext_l[...] = jnp.zeros_like(ext_l)
    ext_s[...] = jnp.zeros_like(ext_s)
    ext_l[:, 0:K1, :] = slc_ref[...]
    ext_l[:, K1:K1 + T, :] = lx_ref[...]
    ext_s[:, 0:K1, :] = ssc_ref[...]
    ext_s[:, K1:K1 + T, :] = xbc_ref[...]
    olc_ref[...] = ext_l[:, T:T + K1, :]
    osc_ref[...] = ext_s[:, T:T + K1, :]

    el = ext_l[...].reshape(R, LRU_WIDTH)
    es = ext_s[...].reshape(R, SSD_CONV_DIM)

    def conv(e, w_ref, b_ref):
        out = b_ref[...] + e * w_ref[0:1, :]
        for k in range(1, CONV_WIDTH):
            out = out + pltpu.roll(e, R - k, axis=0) * w_ref[k:k + 1, :]
        return out

    u = conv(el, lcw_ref, lcb_ref)
    sp_lam = _softplus(-lam_ref[...])
    a, b = _lru_coeffs(u, wg_ref, ba_ref[...], bx_ref[...], sp_lam)
    a, b = _scan_within_8(a, b)
    h0 = jnp.broadcast_to(slh_ref[...], (S, P, LRU_WIDTH)).reshape(R, LRU_WIDTH)
    hseq = a * h0 + b
    olh_ref[...] = hseq.reshape(S, P, LRU_WIDTH)[:, T - 1:T, :]
    gate = pad_rows(gate_ref)
    y_lru = _rms(hseq * _gelu_tanh(gate), glru_ref[...])

    xbc = conv(es, scw_ref, scb_ref)
    xbc = xbc * _sigmoid(xbc)
    xs = xbc[:, 0:SSD_WIDTH]
    bm = xbc[:, SSD_WIDTH:SSD_WIDTH + N_SSD_GROUPS * D_STATE]
    cm = xbc[:, SSD_WIDTH + N_SSD_GROUPS * D_STATE:]
    dt_raw = pad_rows(dt_ref)
    dt = jnp.where(valid, _softplus(dt_raw + dtb_ref[...]), 0.0)
    lane1 = lax.broadcasted_iota(jnp.int32, (1, LANES), 1)
    a2_row = jnp.where(lane1 < N_SSD_HEADS, -LOG2E * jnp.exp(alog_ref[...]), 0.0)

    rr = lax.broadcasted_iota(jnp.int32, (R, R), 0)
    cc = lax.broadcasted_iota(jnp.int32, (R, R), 1)
    allowed = (cc <= rr) & ((rr - cc) <= (rr & (P - 1)))
    tri = jnp.where(allowed, 1.0, 0.0).astype(F32)
    mask_add = jnp.where(allowed, 0.0, NEG_BIG).astype(F32)

    y_diag, cum2, cum2_t = _ssd_intra(xs, bm, cm, dt, a2_row, tri, mask_add)
    ecol = _expand_heads(jnp.exp2(cum2), R)
    end2 = jnp.broadcast_to(cum2.reshape(S, P, LANES)[:, P - 1:P, :], (S, P, LANES)).reshape(R, LANES)
    xw = xs * _expand_heads(jnp.exp2(end2 - cum2) * dt, R)
    ecum_t = jnp.exp2(cum2_t)

    half = SSD_WIDTH // N_SSD_GROUPS
    for q in range(S):
        r0 = P * q
        vq = jnp.broadcast_to(ecum_t[:, r0 + P - 1:r0 + P], (N_SSD_HEADS, LANES))
        for g in range(N_SSD_GROUPS):
            hqg = ssh_ref[q, half * g:half * (g + 1), :]
            cq = cm[r0:r0 + P, D_STATE * g:D_STATE * (g + 1)].astype(BF16)
            yoff_scr[r0:r0 + P, half * g:half * (g + 1)] = lax.dot_general(
                cq, hqg.astype(BF16), (((1,), (1,)), ((), ())), preferred_element_type=F32)
            bq = bm[r0:r0 + P, D_STATE * g:D_STATE * (g + 1)].astype(BF16)
            xq = xw[r0:r0 + P, half * g:half * (g + 1)].astype(BF16)
            st = lax.dot_general(xq, bq, (((0,), (0,)), ((), ())), preferred_element_type=F32)
            for e in range(N_SSD_HEADS // N_SSD_GROUPS):
                h = (N_SSD_HEADS // N_SSD_GROUPS) * g + e
                lo_r = SSD_HEAD_DIM * e
                osh_ref[q, SSD_HEAD_DIM * h:SSD_HEAD_DIM * (h + 1), :] = (
                    vq[h:h + 1, :] * hqg[lo_r:lo_r + SSD_HEAD_DIM, :] + st[lo_r:lo_r + SSD_HEAD_DIM, :])

    ys = y_diag + yoff_scr[...] * ecol
    z = pad_rows(z_ref)
    y_ssd = _ssd_gate_norm(ys, xs, z, dskip_ref[...], gssd_ref[...])
    y_ref[:, :, 0:LRU_WIDTH] = y_lru.reshape(S, P, LRU_WIDTH)[:, 0:T, :]
    y_ref[:, :, LRU_WIDTH:MIX_WIDTH] = y_ssd.reshape(S, P, SSD_WIDTH)[:, 0:T, :]


def _mixer_sample(proj, st_lc, st_lh, st_sc, st_sh, params):
    nseq, T, _ = proj.shape
    S = SAMPLE_SEQS
    const = lambda i: (0, 0)
    in_specs = [
        pl.BlockSpec((S, T, LRU_WIDTH), lambda i: (i, 0, 0)),
        pl.BlockSpec((S, T, LRU_WIDTH), lambda i: (i, 0, 1)),
        pl.BlockSpec((S, T, SSD_WIDTH), lambda i: (i, 0, 2)),
        pl.BlockSpec((S, T, SSD_CONV_DIM), lambda i: (i, 0, 2)),
        pl.BlockSpec((S, T, DT_PAD), lambda i: (i, 0, PROJ_MAIN // DT_PAD)),
        pl.BlockSpec((S, CONV_WIDTH - 1, LRU_WIDTH), lambda i: (i, 0, 0)),
        pl.BlockSpec((S, 1, LRU_WIDTH), lambda i: (i, 0, 0)),
        pl.BlockSpec((S, CONV_WIDTH - 1, SSD_CONV_DIM), lambda i: (i, 0, 0)),
        pl.BlockSpec((S, SSD_WIDTH, D_STATE), lambda i: (i, 0, 0)),
    ] + _param_specs(const)
    out_shape = (
        jax.ShapeDtypeStruct((nseq, T, MIX_WIDTH), F32),
        jax.ShapeDtypeStruct((nseq, CONV_WIDTH - 1, LRU_WIDTH), F32),
        jax.ShapeDtypeStruct((nseq, 1, LRU_WIDTH), F32),
        jax.ShapeDtypeStruct((nseq, CONV_WIDTH - 1, SSD_CONV_DIM), F32),
        jax.ShapeDtypeStruct((nseq, SSD_WIDTH, D_STATE), F32),
    )
    out_specs = (
        pl.BlockSpec((S, T, MIX_WIDTH), lambda i: (i, 0, 0)),
        pl.BlockSpec((S, CONV_WIDTH - 1, LRU_WIDTH), lambda i: (i, 0, 0)),
        pl.BlockSpec((S, 1, LRU_WIDTH), lambda i: (i, 0, 0)),
        pl.BlockSpec((S, CONV_WIDTH - 1, SSD_CONV_DIM), lambda i: (i, 0, 0)),
        pl.BlockSpec((S, SSD_WIDTH, D_STATE), lambda i: (i, 0, 0)),
    )
    scratch = [
        pltpu.VMEM((S, SUBLANES, LRU_WIDTH), F32),
        pltpu.VMEM((S, SUBLANES, SSD_CONV_DIM), F32),
        pltpu.VMEM((S, SUBLANES, LRU_WIDTH), F32),
        pltpu.VMEM((S * SUBLANES, SSD_WIDTH), F32),
    ]
    return pl.pallas_call(
        _mixer_sample_kernel,
        out_shape=out_shape,
        grid=(nseq // S,),
        in_specs=in_specs,
        out_specs=out_specs,
        scratch_shapes=scratch,
        compiler_params=pltpu.CompilerParams(
            dimension_semantics=("parallel",), vmem_limit_bytes=VMEM_LIMIT),
        name="mixer_sample",
    )(proj, proj, proj, proj, proj, st_lc, st_lh, st_sc, st_sh, *params)


def _gate_weights(w_a, w_x):
    def tiles(w):
        per = MXU_DIM // LRU_BLOCK
        w4 = w.reshape(N_LRU_HEADS // per, per, LRU_BLOCK, LRU_BLOCK)
        eye = jnp.eye(per, dtype=w.dtype)
        t = jnp.einsum('jaik,ab->jaibk', w4, eye)
        return t.reshape(N_LRU_HEADS // per, MXU_DIM, MXU_DIM)
    return jnp.concatenate([tiles(w_a), tiles(w_x)], axis=2).astype(BF16)


def kernel(x_prompt, x_sample, state_lru_conv, state_lru_h, state_ssd_conv, state_ssd_h, g_mix, w_in,
           lru_conv_w, lru_conv_b, w_a, b_a, w_x, b_x, lam, g_lru_out, ssd_conv_w, ssd_conv_b, dt_bias,
           a_log, d_skip, g_ssd_out, w_out, g_mlp, w_up, w_down, g_final):
    depth = w_in.shape[0]
    assert depth == 1
    bp, seq, _ = x_prompt.shape
    bs, dseq, _ = x_sample.shape
    l = 0
    row = lambda v: v.reshape(1, -1)
    w_main = w_in[l, :, 0:PROJ_MAIN].astype(BF16)
    w_dt = jnp.pad(w_in[l, :, PROJ_MAIN:], ((0, 0), (0, DT_PAD - N_SSD_HEADS))).astype(BF16)
    params = (
        lru_conv_w[l], row(lru_conv_b[l]), _gate_weights(w_a[l], w_x[l]),
        row(b_a[l]), row(b_x[l]), row(lam[l]), row(g_lru_out[l]),
        ssd_conv_w[l], row(ssd_conv_b[l]),
        jnp.pad(row(dt_bias[l]), ((0, 0), (0, DT_PAD - N_SSD_HEADS))),
        jnp.pad(row(a_log[l]), ((0, 0), (0, DT_PAD - N_SSD_HEADS))),
        row(jnp.repeat(d_skip[l], SSD_HEAD_DIM)), row(g_ssd_out[l]),
    )
    w_out_b = w_out[l].astype(BF16)
    w_up_b = w_up[l].astype(BF16)
    w_down_b = w_down[l].astype(BF16)
    gmix = row(g_mix[l])
    gmlp = row(g_mlp[l])
    gfin = row(g_final)

    y_prompt, p_lc, p_lh, p_sc, p_sh = _layer_prompt(
        x_prompt, gmix, w_main, w_dt, params, w_out_b, gmlp, w_up_b, w_down_b, gfin)

    xs2 = x_sample.reshape(bs * dseq, D_MODEL)
    proj_s = _in_proj(xs2, gmix, w_main, w_dt).reshape(bs, dseq, PROJ_PAD)
    ymix_s, s_lc, s_lh, s_sc, s_sh = _mixer_sample(
        proj_s, state_lru_conv[l], state_lru_h[l].reshape(bs, 1, LRU_WIDTH), state_ssd_conv[l],
        state_ssd_h[l].reshape(bs, SSD_WIDTH, D_STATE), params)
    y_sample = _out_mlp(xs2, ymix_s.reshape(bs * dseq, MIX_WIDTH), w_out_b, gmlp, w_up_b, w_down_b, gfin)

    hshape = (N_SSD_HEADS, SSD_HEAD_DIM, D_STATE)
    return (
        y_prompt.reshape(bp, seq, D_MODEL), y_sample.reshape(bs, dseq, D_MODEL),
        p_lc[None], p_lh.reshape(1, bp, LRU_WIDTH), p_sc[None], p_sh.reshape(1, bp, *hshape),
        s_lc[None], s_lh.reshape(1, bs, LRU_WIDTH), s_sc[None], s_sh.reshape(1, bs, *hshape),
    )
```

```python
import functools
import math

import jax
import jax.numpy as jnp
from jax import lax
from jax.experimental import pallas as pl
from jax.experimental.pallas import tpu as pltpu

F32 = jnp.float32
BF16 = jnp.bfloat16

D_MODEL = 1024
LRU_WIDTH = 1024
N_LRU_HEADS = 16
LRU_BLOCK = 64
LRU_C = 8.0
SSD_WIDTH = 1024
SSD_HEAD_DIM = 64
N_SSD_HEADS = 16
N_SSD_GROUPS = 2
D_STATE = 128
CONV_WIDTH = 4
SSD_CONV_DIM = SSD_WIDTH + 2 * N_SSD_GROUPS * D_STATE
D_FF = 4 * D_MODEL
EPS = 1e-6

LANES = 128
SUBLANES = 8
MXU_DIM = 256
DT_PAD = LANES
PROJ_MAIN = 2 * LRU_WIDTH + SSD_WIDTH + SSD_CONV_DIM
PROJ_PAD = PROJ_MAIN + DT_PAD
MIX_WIDTH = LRU_WIDTH + SSD_WIDTH
SSD_CHUNK = 128
PROMPT_TC = 256
PROMPT_NB = 2
ROW_TILE = 512
SAMPLE_SEQS = SSD_CHUNK // SUBLANES
SCAN_RUN = PROMPT_TC // SUBLANES
SCAN_PITCH = SCAN_RUN + 4
NEG_BIG = -1e30
LOG2E = 1.4426950408889634
VMEM_LIMIT = 56 * 1024 * 1024
HI = lax.Precision.HIGHEST


def _rms(x, g):
    ms = jnp.mean(x * x, axis=-1, keepdims=True)
    return x * lax.rsqrt(ms + EPS) * g


def _sigmoid(x):
    return 1.0 / (1.0 + jnp.exp(-x))


def _softplus(x):
    return jnp.maximum(x, 0.0) + jnp.log1p(jnp.exp(-jnp.abs(x)))


def _gelu_tanh(x):
    c = math.sqrt(2.0 / math.pi)
    return 0.5 * x * (1.0 + jnp.tanh(c * (x + 0.044715 * (x * x * x))))


def _lru_coeffs(u, wg_ref, b_a, b_x, sp_lam):
    ub = u.astype(BF16)
    r_parts, i_parts = [], []
    for j in range(LRU_WIDTH // MXU_DIM):
        g = jnp.dot(ub[:, MXU_DIM * j:MXU_DIM * (j + 1)], wg_ref[j], preferred_element_type=F32)
        r_parts.append(g[:, :MXU_DIM])
        i_parts.append(g[:, MXU_DIM:])
    r = _sigmoid(jnp.concatenate(r_parts, axis=1) + b_a)
    i = _sigmoid(jnp.concatenate(i_parts, axis=1) + b_x)
    log_a = (-LRU_C) * r * sp_lam
    a = jnp.exp(log_a)
    th = jnp.tanh(-log_a)
    v = 2.0 * th / (1.0 + th)
    mult = jnp.where(v > 0.0, v * lax.rsqrt(v), 0.0)
    return a, mult * (i * u)


def _scan_within_8(a, b):
    ridx = lax.broadcasted_iota(jnp.int32, a.shape, 0) & (SUBLANES - 1)
    for k in (1, 2, 4):
        a_s = pltpu.roll(a, k, axis=0)
        b_s = pltpu.roll(b, k, axis=0)
        m = ridx >= k
        b = jnp.where(m, a * b_s + b, b)
        a = jnp.where(m, a * a_s, a)
    return a, b


def _conv_slabs(ext, w_ref, b_ref, rows, first):
    parts = []
    for s in range(ext.shape[0]):
        cols = slice(LANES * s, LANES * (s + 1))
        acc = b_ref[:, cols] + ext[s, pl.ds(first, rows), :] * w_ref[0:1, cols]
        for k in range(1, CONV_WIDTH):
            acc = acc + ext[s, pl.ds(first + k, rows), :] * w_ref[k:k + 1, cols]
        parts.append(acc)
    return jnp.concatenate(parts, axis=1)


def _lru_scan_strided(a, b, hcar, a_pad, b_pad, h_pad):
    rows = a.shape[0]
    S = rows // SUBLANES
    nslab = LRU_WIDTH // LANES
    ridx = lax.broadcasted_iota(jnp.int32, (SUBLANES, LANES), 0)
    step = lambda ref, s, i: ref[s, pl.ds(i, SUBLANES, stride=SCAN_PITCH), :]
    for s in range(nslab):
        cols = slice(LANES * s, LANES * (s + 1))
        for j in range(SUBLANES):
            a_pad[s, SCAN_PITCH * j:SCAN_PITCH * j + S, :] = a[S * j:S * (j + 1), cols]
            b_pad[s, SCAN_PITCH * j:SCAN_PITCH * j + S, :] = b[S * j:S * (j + 1), cols]
    h = [jnp.zeros((SUBLANES, LANES), F32)] * nslab
    prod = [jnp.ones((SUBLANES, LANES), F32)] * nslab
    for i in range(S):
        for s in range(nslab):
            av = step(a_pad, s, i)
            h[s] = av * h[s] + step(b_pad, s, i)
            prod[s] = av * prod[s]
    for s in range(nslab):
        cols = slice(LANES * s, LANES * (s + 1))
        pcum, hcum = _scan_within_8(prod[s], h[s])
        cin = hcar[:, cols]
        ends = hcum + pcum * cin
        h[s] = jnp.where(ridx == 0, cin, pltpu.roll(ends, 1, axis=0))
        hcar[:, cols] = jnp.broadcast_to(ends[SUBLANES - 1:SUBLANES, :], (SUBLANES, LANES))
    for i in range(S):
        for s in range(nslab):
            h[s] = step(a_pad, s, i) * h[s] + step(b_pad, s, i)
            h_pad[s, pl.ds(i, SUBLANES, stride=SCAN_PITCH), :] = h[s]
    return jnp.concatenate(
        [jnp.concatenate([h_pad[s, SCAN_PITCH * j:SCAN_PITCH * j + S, :] for j in range(SUBLANES)], axis=0)
         for s in range(nslab)], axis=1)


def _ssd_intra(xs, bm, cm, dt, a2_row, tri, mask_add, selt_ref):
    L = xs.shape[0]
    cum2 = jnp.dot(tri, dt * a2_row, precision=HI, preferred_element_type=F32)
    cum2_t = cum2.T[0:N_SSD_HEADS, :]
    c2_t = cum2_t - jnp.log2(dt.T[0:N_SSD_HEADS, :])
    cols = _spread(cum2, selt_ref)
    lane = lax.broadcasted_iota(jnp.int32, (L, LANES), 1)
    lo = lane < SSD_HEAD_DIM
    y_parts = []
    for g in range(N_SSD_GROUPS):
        bg = bm[:, D_STATE * g:D_STATE * (g + 1)].astype(BF16)
        cg = cm[:, D_STATE * g:D_STATE * (g + 1)].astype(BF16)
        cb = lax.dot_general(cg, bg, (((1,), (1,)), ((), ())), preferred_element_type=F32)
        for jj in range(N_SSD_HEADS // N_SSD_GROUPS // 2):
            j = (N_SSD_HEADS // N_SSD_GROUPS // 2) * g + jj
            h0, h1 = 2 * j, 2 * j + 1
            col0 = cols[:, LANES * h0:LANES * (h0 + 1)]
            col1 = cols[:, LANES * h1:LANES * (h1 + 1)]
            m0 = cb * jnp.exp2(col0 - c2_t[h0:h0 + 1, :] + mask_add)
            m1 = cb * jnp.exp2(col1 - c2_t[h1:h1 + 1, :] + mask_add)
            lhs = jnp.concatenate([m0, m1], axis=1).astype(BF16)
            xp = xs[:, LANES * j:LANES * (j + 1)]
            rhs = jnp.concatenate([jnp.where(lo, xp, 0.0), jnp.where(lo, 0.0, xp)], axis=0).astype(BF16)
            y_parts.append(jnp.dot(lhs, rhs, preferred_element_type=F32))
    return jnp.concatenate(y_parts, axis=1), cum2, cum2_t


def _spread(v, sel_ref):
    p0 = v.astype(BF16)
    p1 = (v - p0.astype(F32)).astype(BF16)
    return jnp.dot(jnp.concatenate([p0, p1], axis=1), sel_ref[...], preferred_element_type=F32)


def _ssd_gate_norm(ys, xs, z_act, dskip, g_ssd):
    ys = ys + dskip * xs
    gated = ys * z_act
    half = SSD_WIDTH // N_SSD_GROUPS
    outs = []
    for g in range(N_SSD_GROUPS):
        outs.append(_rms(gated[:, half * g:half * (g + 1)], g_ssd[:, half * g:half * (g + 1)]))
    return jnp.concatenate(outs, axis=1)


def _inproj_kernel(x_ref, g_ref, w_ref, wdt_ref, o_ref):
    hn = _rms(x_ref[...], g_ref[...]).astype(BF16)
    o_ref[:, 0:PROJ_MAIN] = jnp.dot(hn, w_ref[...], preferred_element_type=F32)
    o_ref[:, PROJ_MAIN:PROJ_PAD] = jnp.dot(hn, wdt_ref[...], preferred_element_type=F32)


def _in_proj(x2d, g_mix, w_main, w_dt):
    n = x2d.shape[0]
    return pl.pallas_call(
        _inproj_kernel,
        out_shape=jax.ShapeDtypeStruct((n, PROJ_PAD), F32),
        grid=(n // ROW_TILE,),
        in_specs=[
            pl.BlockSpec((ROW_TILE, D_MODEL), lambda i: (i, 0)),
            pl.BlockSpec((1, D_MODEL), lambda i: (0, 0)),
            pl.BlockSpec((D_MODEL, PROJ_MAIN), lambda i: (0, 0), pipeline_mode=pl.Buffered(1)),
            pl.BlockSpec((D_MODEL, DT_PAD), lambda i: (0, 0), pipeline_mode=pl.Buffered(1)),
        ],
        out_specs=pl.BlockSpec((ROW_TILE, PROJ_PAD), lambda i: (i, 0)),
        compiler_params=pltpu.CompilerParams(
            dimension_semantics=("parallel",), vmem_limit_bytes=VMEM_LIMIT),
        name="in_proj",
    )(x2d, g_mix, w_main, w_dt)


def _inproj_prompt_kernel(x_ref, g_ref, w_ref, wdt_ref, lcw_ref, lcb_ref, scw_ref, scb_ref, dtb_ref,
                          o_ref, olc_ref, osc_ref, ext_l, ext_s, *, steps_per_seq):
    t = lax.rem(pl.program_id(0), steps_per_seq)
    rows = ROW_TILE
    hist = SUBLANES
    o1, o2, o3 = LRU_WIDTH, 2 * LRU_WIDTH, 2 * LRU_WIDTH + SSD_WIDTH

    @pl.when(t == 0)
    def _init():
        ext_l[:, 0:hist, :] = jnp.zeros((ext_l.shape[0], hist, LANES), F32)
        ext_s[:, 0:hist, :] = jnp.zeros((ext_s.shape[0], hist, LANES), F32)

    hn = _rms(x_ref[...], g_ref[...]).astype(BF16)
    lx = jnp.dot(hn, w_ref[:, 0:o1], preferred_element_type=F32)
    for s in range(ext_l.shape[0]):
        ext_l[s, hist:hist + rows, :] = lx[:, LANES * s:LANES * (s + 1)]
    xbc_in = jnp.dot(hn, w_ref[:, o3:PROJ_MAIN], preferred_element_type=F32)
    for s in range(ext_s.shape[0]):
        ext_s[s, hist:hist + rows, :] = xbc_in[:, LANES * s:LANES * (s + 1)]
    o_ref[:, o1:o2] = _gelu_tanh(jnp.dot(hn, w_ref[:, o1:o2], preferred_element_type=F32))
    z = jnp.dot(hn, w_ref[:, o2:o3], preferred_element_type=F32)
    o_ref[:, o2:o3] = z * _sigmoid(z)
    o_ref[:, PROJ_MAIN:PROJ_PAD] = _softplus(
        jnp.dot(hn, wdt_ref[...], preferred_element_type=F32) + dtb_ref[...])
    o_ref[:, 0:o1] = _conv_slabs(ext_l, lcw_ref, lcb_ref, rows, hist - (CONV_WIDTH - 1))
    xbc = _conv_slabs(ext_s, scw_ref, scb_ref, rows, hist - (CONV_WIDTH - 1))
    o_ref[:, o3:PROJ_MAIN] = xbc * _sigmoid(xbc)

    @pl.when(t == steps_per_seq - 1)
    def _final():
        last = slice(hist + rows - (CONV_WIDTH - 1), hist + rows)
        for s in range(ext_l.shape[0]):
            olc_ref[:, LANES * s:LANES * (s + 1)] = ext_l[s, last, :]
        for s in range(ext_s.shape[0]):
            osc_ref[:, LANES * s:LANES * (s + 1)] = ext_s[s, last, :]

    tail_l = ext_l[:, rows:rows + hist, :]
    tail_s = ext_s[:, rows:rows + hist, :]
    ext_l[:, 0:hist, :] = tail_l
    ext_s[:, 0:hist, :] = tail_s


def _in_proj_prompt(x2d, bsz, g_mix, w_main, w_dt, lcw, lcb, scw, scb, dtb):
    n = x2d.shape[0]
    steps_per_seq = n // bsz // ROW_TILE
    const = lambda i: (0, 0)
    return pl.pallas_call(
        functools.partial(_inproj_prompt_kernel, steps_per_seq=steps_per_seq),
        out_shape=(
            jax.ShapeDtypeStruct((n, PROJ_PAD), F32),
            jax.ShapeDtypeStruct((bsz, CONV_WIDTH - 1, LRU_WIDTH), F32),
            jax.ShapeDtypeStruct((bsz, CONV_WIDTH - 1, SSD_CONV_DIM), F32),
        ),
        grid=(n // ROW_TILE,),
        in_specs=[
            pl.BlockSpec((ROW_TILE, D_MODEL), lambda i: (i, 0)),
            pl.BlockSpec((1, D_MODEL), const),
            pl.BlockSpec((D_MODEL, PROJ_MAIN), const, pipeline_mode=pl.Buffered(1)),
            pl.BlockSpec((D_MODEL, DT_PAD), const, pipeline_mode=pl.Buffered(1)),
            pl.BlockSpec((CONV_WIDTH, LRU_WIDTH), const),
            pl.BlockSpec((1, LRU_WIDTH), const),
            pl.BlockSpec((CONV_WIDTH, SSD_CONV_DIM), const),
            pl.BlockSpec((1, SSD_CONV_DIM), const),
            pl.BlockSpec((1, DT_PAD), const),
        ],
        out_specs=(
            pl.BlockSpec((ROW_TILE, PROJ_PAD), lambda i: (i, 0)),
            pl.BlockSpec((None, CONV_WIDTH - 1, LRU_WIDTH), lambda i: (i // steps_per_seq, 0, 0)),
            pl.BlockSpec((None, CONV_WIDTH - 1, SSD_CONV_DIM), lambda i: (i // steps_per_seq, 0, 0)),
        ),
        scratch_shapes=[
            pltpu.VMEM((LRU_WIDTH // LANES, SUBLANES + ROW_TILE, LANES), F32),
            pltpu.VMEM((SSD_CONV_DIM // LANES, SUBLANES + ROW_TILE, LANES), F32),
        ],
        compiler_params=pltpu.CompilerParams(
            dimension_semantics=("arbitrary",), vmem_limit_bytes=VMEM_LIMIT),
        name="in_proj_prompt",
    )(x2d, g_mix, w_main, w_dt, lcw, lcb, scw, scb, dtb)


def _outmlp_kernel(x_ref, y_ref, wo_ref, gm_ref, wu_ref, wd_ref, gf_ref, o_ref):
    x1 = x_ref[...] + jnp.dot(y_ref[...].astype(BF16), wo_ref[...], preferred_element_type=F32)
    m = _rms(x1, gm_ref[...]).astype(BF16)
    u = jnp.dot(m, wu_ref[...], preferred_element_type=F32)
    u = jnp.square(jnp.maximum(u, 0.0)).astype(BF16)
    x2 = x1 + jnp.dot(u, wd_ref[...], preferred_element_type=F32)
    o_ref[...] = _rms(x2, gf_ref[...])


def _out_mlp(x2d, ymix2d, w_out_b, g_mlp, w_up_b, w_down_b, g_final):
    n = x2d.shape[0]
    const = lambda i: (0, 0)
    return pl.pallas_call(
        _outmlp_kernel,
        out_shape=jax.ShapeDtypeStruct((n, D_MODEL), F32),
        grid=(n // ROW_TILE,),
        in_specs=[
            pl.BlockSpec((ROW_TILE, D_MODEL), lambda i: (i, 0)),
            pl.BlockSpec((ROW_TILE, MIX_WIDTH), lambda i: (i, 0)),
            pl.BlockSpec((MIX_WIDTH, D_MODEL), const, pipeline_mode=pl.Buffered(1)),
            pl.BlockSpec((1, D_MODEL), const),
            pl.BlockSpec((D_MODEL, D_FF), const, pipeline_mode=pl.Buffered(1)),
            pl.BlockSpec((D_FF, D_MODEL), const, pipeline_mode=pl.Buffered(1)),
            pl.BlockSpec((1, D_MODEL), const),
        ],
        out_specs=pl.BlockSpec((ROW_TILE, D_MODEL), lambda i: (i, 0)),
        compiler_params=pltpu.CompilerParams(
            dimension_semantics=("parallel",), vmem_limit_bytes=VMEM_LIMIT),
        name="out_mlp",
    )(x2d, ymix2d, w_out_b, g_mlp, w_up_b, w_down_b, g_final)


def _mixer_prompt_kernel(u_ref, gl_ref, zact_ref, xbc_ref, dt_ref,
                         wg_ref, ba_ref, bx_ref, lam_ref, glru_ref, alog_ref, dskip_ref, gssd_ref,
                         selt_ref, selp_ref,
                         y_ref, olh_ref, osh_ref,
                         a_pad, b_pad, h_pad, hcar, ht):
    t = pl.program_id(1)
    nt = pl.num_programs(1)
    tc = PROMPT_TC

    @pl.when(t == 0)
    def _init():
        hcar[...] = jnp.zeros_like(hcar)
        ht[...] = jnp.zeros_like(ht)

    sp_lam = _softplus(-lam_ref[...])
    lane1 = lax.broadcasted_iota(jnp.int32, (1, LANES), 1)
    a2_row = jnp.where(lane1 < N_SSD_HEADS, -LOG2E * jnp.exp(alog_ref[...]), 0.0)
    L = SSD_CHUNK
    rr = lax.broadcasted_iota(jnp.int32, (L, L), 0)
    cc = lax.broadcasted_iota(jnp.int32, (L, L), 1)
    causal = cc <= rr
    tri = jnp.where(causal, 1.0, 0.0).astype(F32)
    mask_add = jnp.where(causal, 0.0, NEG_BIG).astype(F32)
    half = SSD_WIDTH // N_SSD_GROUPS

    for n in range(PROMPT_NB):
        a, b = _lru_coeffs(u_ref[n], wg_ref, ba_ref[...], bx_ref[...], sp_lam)
        hseq = _lru_scan_strided(a, b, hcar.at[n], a_pad.at[n], b_pad.at[n], h_pad.at[n])
        y_ref[n, :, 0:LRU_WIDTH] = _rms(hseq * gl_ref[n], glru_ref[...])

        for c in range(tc // L):
            rows = slice(L * c, L * (c + 1))
            xs = xbc_ref[n, rows, 0:SSD_WIDTH]
            bm = xbc_ref[n, rows, SSD_WIDTH:SSD_WIDTH + N_SSD_GROUPS * D_STATE]
            cm = xbc_ref[n, rows, SSD_WIDTH + N_SSD_GROUPS * D_STATE:SSD_CONV_DIM]
            dt = dt_ref[n, rows, :]
            y_diag, cum2, _ = _ssd_intra(xs, bm, cm, dt, a2_row, tri, mask_add, selt_ref)
            ecol = _spread(jnp.exp2(cum2), selp_ref)
            xw = xs * _spread(jnp.exp2(cum2[L - 1:L, :] - cum2) * dt, selp_ref)
            dec = ecol[L - 1:L, :]
            y_off_parts = []
            for g in range(N_SSD_GROUPS):
                htg = ht[n, g]
                cg = cm[:, D_STATE * g:D_STATE * (g + 1)].astype(BF16)
                y_off_parts.append(jnp.dot(cg, htg.astype(BF16), preferred_element_type=F32))
                bg_t = bm[:, D_STATE * g:D_STATE * (g + 1)].T.astype(BF16)
                st = jnp.dot(bg_t, xw[:, half * g:half * (g + 1)].astype(BF16), preferred_element_type=F32)
                ht[n, g] = htg * dec[:, half * g:half * (g + 1)] + st
            ys = y_diag + jnp.concatenate(y_off_parts, axis=1) * ecol
            y_ref[n, rows, LRU_WIDTH:MIX_WIDTH] = _ssd_gate_norm(
                ys, xs, zact_ref[n, rows, :], dskip_ref[...], gssd_ref[...])

    @pl.when(t == nt - 1)
    def _final():
        for n in range(PROMPT_NB):
            olh_ref[n] = hcar[n, 0:1, :]
            for g in range(N_SSD_GROUPS):
                osh_ref[n, half * g:half * (g + 1), :] = ht[n, g].T


def _param_specs(const):
    return [
        pl.BlockSpec((CONV_WIDTH, LRU_WIDTH), const),
        pl.BlockSpec((1, LRU_WIDTH), const),
        pl.BlockSpec((LRU_WIDTH // MXU_DIM, MXU_DIM, 2 * MXU_DIM), lambda *_: (0, 0, 0)),
        pl.BlockSpec((1, LRU_WIDTH), const),
        pl.BlockSpec((1, LRU_WIDTH), const),
        pl.BlockSpec((1, LRU_WIDTH), const),
        pl.BlockSpec((1, LRU_WIDTH), const),
        pl.BlockSpec((CONV_WIDTH, SSD_CONV_DIM), const),
        pl.BlockSpec((1, SSD_CONV_DIM), const),
        pl.BlockSpec((1, DT_PAD), const),
        pl.BlockSpec((1, DT_PAD), const),
        pl.BlockSpec((1, SSD_WIDTH), const),
        pl.BlockSpec((1, SSD_WIDTH), const),
    ]


def _head_selectors():
    k = jnp.arange(2 * LANES)[:, None] % LANES
    sel_t = (k == jnp.arange(N_SSD_HEADS * LANES)[None, :] // LANES).astype(BF16)
    sel_p = (k == jnp.arange(SSD_WIDTH)[None, :] // SSD_HEAD_DIM).astype(BF16)
    return sel_t, sel_p


def _mixer_prompt(act, wg, b_a, b_x, lam, g_lru, a_log, d_skip, g_ssd, sel_t, sel_p):
    bsz, seq, _ = act.shape
    tc = PROMPT_TC
    nb = PROMPT_NB
    const = lambda b, t: (0, 0)
    in_specs = [
        pl.BlockSpec((nb, tc, LRU_WIDTH), lambda b, t: (b, t, 0)),
        pl.BlockSpec((nb, tc, LRU_WIDTH), lambda b, t: (b, t, 1)),
        pl.BlockSpec((nb, tc, SSD_WIDTH), lambda b, t: (b, t, 2)),
        pl.BlockSpec((nb, tc, SSD_CONV_DIM), lambda b, t: (b, t, 2)),
        pl.BlockSpec((nb, tc, DT_PAD), lambda b, t: (b, t, PROJ_MAIN // DT_PAD)),
        pl.BlockSpec((LRU_WIDTH // MXU_DIM, MXU_DIM, 2 * MXU_DIM), lambda b, t: (0, 0, 0)),
        pl.BlockSpec((1, LRU_WIDTH), const),
        pl.BlockSpec((1, LRU_WIDTH), const),
        pl.BlockSpec((1, LRU_WIDTH), const),
        pl.BlockSpec((1, LRU_WIDTH), const),
        pl.BlockSpec((1, DT_PAD), const),
        pl.BlockSpec((1, SSD_WIDTH), const),
        pl.BlockSpec((1, SSD_WIDTH), const),
        pl.BlockSpec((2 * LANES, N_SSD_HEADS * LANES), const),
        pl.BlockSpec((2 * LANES, SSD_WIDTH), const),
    ]
    out_shape = (
        jax.ShapeDtypeStruct((bsz, seq, MIX_WIDTH), F32),
        jax.ShapeDtypeStruct((bsz, 1, LRU_WIDTH), F32),
        jax.ShapeDtypeStruct((bsz, SSD_WIDTH, D_STATE), F32),
    )
    out_specs = (
        pl.BlockSpec((nb, tc, MIX_WIDTH), lambda b, t: (b, t, 0)),
        pl.BlockSpec((nb, 1, LRU_WIDTH), lambda b, t: (b, 0, 0)),
        pl.BlockSpec((nb, SSD_WIDTH, D_STATE), lambda b, t: (b, 0, 0)),
    )
    scratch = [
        pltpu.VMEM((nb, LRU_WIDTH // LANES, SUBLANES * SCAN_PITCH, LANES), F32),
        pltpu.VMEM((nb, LRU_WIDTH // LANES, SUBLANES * SCAN_PITCH, LANES), F32),
        pltpu.VMEM((nb, LRU_WIDTH // LANES, SUBLANES * SCAN_PITCH, LANES), F32),
        pltpu.VMEM((nb, SUBLANES, LRU_WIDTH), F32),
        pltpu.VMEM((nb, N_SSD_GROUPS, D_STATE, SSD_WIDTH // N_SSD_GROUPS), F32),
    ]
    return pl.pallas_call(
        _mixer_prompt_kernel,
        out_shape=out_shape,
        grid=(bsz // nb, seq // tc),
        in_specs=in_specs,
        out_specs=out_specs,
        scratch_shapes=scratch,
        compiler_params=pltpu.CompilerParams(
            dimension_semantics=("parallel", "arbitrary"), vmem_limit_bytes=VMEM_LIMIT),
        name="mixer_prompt",
    )(act, act, act, act, act, wg, b_a, b_x, lam, g_lru, a_log, d_skip, g_ssd, sel_t, sel_p)


def _mixer_sample_kernel(lx_ref, gate_ref, z_ref, xbc_ref, dt_ref,
                         slc_ref, slh_ref, ssc_ref, ssh_ref,
                         lcw_ref, lcb_ref, wg_ref, ba_ref, bx_ref, lam_ref, glru_ref,
                         scw_ref, scb_ref, dtb_ref, alog_ref, dskip_ref, gssd_ref, selt_ref, selp_ref,
                         y_ref, olc_ref, olh_ref, osc_ref, osh_ref,
                         ext_l, ext_s, pad_scr, yoff_scr):
    S = SAMPLE_SEQS
    P = SUBLANES
    T = lx_ref.shape[1]
    K1 = CONV_WIDTH - 1
    R = S * P
    row_i = lax.broadcasted_iota(jnp.int32, (R, 1), 0) & (P - 1)
    valid = row_i < T

    def pad_rows(ref):
        width = ref.shape[-1]
        pad_scr[:, :, 0:width] = jnp.zeros((S, P, width), F32)
        pad_scr[:, 0:T, 0:width] = ref[...]
        return pad_scr[:, :, 0:width].reshape(R, width)

    ext_l[...] = jnp.zeros_like(ext_l)
    ext_s[...] = jnp.zeros_like(ext_s)
    ext_l[:, 0:K1, :] = slc_ref[...]
    ext_l[:, K1:K1 + T, :] = lx_ref[...]
    ext_s[:, 0:K1, :] = ssc_ref[...]
    ext_s[:, K1:K1 + T, :] = xbc_ref[...]
    olc_ref[...] = ext_l[:, T:T + K1, :]
    osc_ref[...] = ext_s[:, T:T + K1, :]

    el = ext_l[...].reshape(R, LRU_WIDTH)
    es = ext_s[...].reshape(R, SSD_CONV_DIM)

    def conv(e, w_ref, b_ref):
        out = b_ref[...] + e * w_ref[0:1, :]
        for k in range(1, CONV_WIDTH):
            out = out + pltpu.roll(e, R - k, axis=0) * w_ref[k:k + 1, :]
        return out

    u = conv(el, lcw_ref, lcb_ref)
    sp_lam = _softplus(-lam_ref[...])
    a, b = _lru_coeffs(u, wg_ref, ba_ref[...], bx_ref[...], sp_lam)
    a, b = _scan_within_8(a, b)
    h0 = jnp.broadcast_to(slh_ref[...], (S, P, LRU_WIDTH)).reshape(R, LRU_WIDTH)
    hseq = a * h0 + b
    olh_ref[...] = hseq.reshape(S, P, LRU_WIDTH)[:, T - 1:T, :]
    gate = pad_rows(gate_ref)
    y_lru = _rms(hseq * _gelu_tanh(gate), glru_ref[...])

    xbc = conv(es, scw_ref, scb_ref)
    xbc = xbc * _sigmoid(xbc)
    xs = xbc[:, 0:SSD_WIDTH]
    bm = xbc[:, SSD_WIDTH:SSD_WIDTH + N_SSD_GROUPS * D_STATE]
    cm = xbc[:, SSD_WIDTH + N_SSD_GROUPS * D_STATE:]
    dt_raw = pad_rows(dt_ref)
    dt = jnp.where(valid, _softplus(dt_raw + dtb_ref[...]), 0.0)
    lane1 = lax.broadcasted_iota(jnp.int32, (1, LANES), 1)
    a2_row = jnp.where(lane1 < N_SSD_HEADS, -LOG2E * jnp.exp(alog_ref[...]), 0.0)

    rr = lax.broadcasted_iota(jnp.int32, (R, R), 0)
    cc = lax.broadcasted_iota(jnp.int32, (R, R), 1)
    allowed = (cc <= rr) & ((rr - cc) <= (rr & (P - 1)))
    tri = jnp.where(allowed, 1.0, 0.0).astype(F32)
    mask_add = jnp.where(allowed, 0.0, NEG_BIG).astype(F32)

    y_diag, cum2, cum2_t = _ssd_intra(xs, bm, cm, dt, a2_row, tri, mask_add, selt_ref)
    ecol = _spread(jnp.exp2(cum2), selp_ref)
    end2 = jnp.broadcast_to(cum2.reshape(S, P, LANES)[:, P - 1:P, :], (S, P, LANES)).reshape(R, LANES)
    xw = xs * _spread(jnp.exp2(end2 - cum2) * dt, selp_ref)
    ecum_t = jnp.exp2(cum2_t)

    half = SSD_WIDTH // N_SSD_GROUPS
    for q in range(S):
        r0 = P * q
        vq = jnp.broadcast_to(ecum_t[:, r0 + P - 1:r0 + P], (N_SSD_HEADS, LANES))
        for g in range(N_SSD_GROUPS):
            hqg = ssh_ref[q, half * g:half * (g + 1), :]
            cq = cm[r0:r0 + P, D_STATE * g:D_STATE * (g + 1)].astype(BF16)
            yoff_scr[r0:r0 + P, half * g:half * (g + 1)] = lax.dot_general(
                cq, hqg.astype(BF16), (((1,), (1,)), ((), ())), preferred_element_type=F32)
            bq = bm[r0:r0 + P, D_STATE * g:D_STATE * (g + 1)].astype(BF16)
            xq = xw[r0:r0 + P, half * g:half * (g + 1)].astype(BF16)
            st = lax.dot_general(xq, bq, (((0,), (0,)), ((), ())), preferred_element_type=F32)
            for e in range(N_SSD_HEADS // N_SSD_GROUPS):
                h = (N_SSD_HEADS // N_SSD_GROUPS) * g + e
                lo_r = SSD_HEAD_DIM * e
                osh_ref[q, SSD_HEAD_DIM * h:SSD_HEAD_DIM * (h + 1), :] = (
                    vq[h:h + 1, :] * hqg[lo_r:lo_r + SSD_HEAD_DIM, :] + st[lo_r:lo_r + SSD_HEAD_DIM, :])

    ys = y_diag + yoff_scr[...] * ecol
    z = pad_rows(z_ref)
    y_ssd = _ssd_gate_norm(ys, xs, z * _sigmoid(z), dskip_ref[...], gssd_ref[...])
    y_ref[:, :, 0:LRU_WIDTH] = y_lru.reshape(S, P, LRU_WIDTH)[:, 0:T, :]
    y_ref[:, :, LRU_WIDTH:MIX_WIDTH] = y_ssd.reshape(S, P, SSD_WIDTH)[:, 0:T, :]


def _mixer_sample(proj, st_lc, st_lh, st_sc, st_sh, params, sel_t, sel_p):
    nseq, T, _ = proj.shape
    S = SAMPLE_SEQS
    const = lambda i: (0, 0)
    in_specs = [
        pl.BlockSpec((S, T, LRU_WIDTH), lambda i: (i, 0, 0)),
        pl.BlockSpec((S, T, LRU_WIDTH), lambda i: (i, 0, 1)),
        pl.BlockSpec((S, T, SSD_WIDTH), lambda i: (i, 0, 2)),
        pl.BlockSpec((S, T, SSD_CONV_DIM), lambda i: (i, 0, 2)),
        pl.BlockSpec((S, T, DT_PAD), lambda i: (i, 0, PROJ_MAIN // DT_PAD)),
        pl.BlockSpec((S, CONV_WIDTH - 1, LRU_WIDTH), lambda i: (i, 0, 0)),
        pl.BlockSpec((S, 1, LRU_WIDTH), lambda i: (i, 0, 0)),
        pl.BlockSpec((S, CONV_WIDTH - 1, SSD_CONV_DIM), lambda i: (i, 0, 0)),
        pl.BlockSpec((S, SSD_WIDTH, D_STATE), lambda i: (i, 0, 0)),
    ] + _param_specs(const) + [
        pl.BlockSpec((2 * LANES, N_SSD_HEADS * LANES), const),
        pl.BlockSpec((2 * LANES, SSD_WIDTH), const),
    ]
    out_shape = (
        jax.ShapeDtypeStruct((nseq, T, MIX_WIDTH), F32),
        jax.ShapeDtypeStruct((nseq, CONV_WIDTH - 1, LRU_WIDTH), F32),
        jax.ShapeDtypeStruct((nseq, 1, LRU_WIDTH), F32),
        jax.ShapeDtypeStruct((nseq, CONV_WIDTH - 1, SSD_CONV_DIM), F32),
        jax.ShapeDtypeStruct((nseq, SSD_WIDTH, D_STATE), F32),
    )
    out_specs = (
        pl.BlockSpec((S, T, MIX_WIDTH), lambda i: (i, 0, 0)),
        pl.BlockSpec((S, CONV_WIDTH - 1, LRU_WIDTH), lambda i: (i, 0, 0)),
        pl.BlockSpec((S, 1, LRU_WIDTH), lambda i: (i, 0, 0)),
        pl.BlockSpec((S, CONV_WIDTH - 1, SSD_CONV_DIM), lambda i: (i, 0, 0)),
        pl.BlockSpec((S, SSD_WIDTH, D_STATE), lambda i: (i, 0, 0)),
    )
    scratch = [
        pltpu.VMEM((S, SUBLANES, LRU_WIDTH), F32),
        pltpu.VMEM((S, SUBLANES, SSD_CONV_DIM), F32),
        pltpu.VMEM((S, SUBLANES, LRU_WIDTH), F32),
        pltpu.VMEM((S * SUBLANES, SSD_WIDTH), F32),
    ]
    return pl.pallas_call(
        _mixer_sample_kernel,
        out_shape=out_shape,
        grid=(nseq // S,),
        in_specs=in_specs,
        out_specs=out_specs,
        scratch_shapes=scratch,
        compiler_params=pltpu.CompilerParams(
            dimension_semantics=("parallel",), vmem_limit_bytes=VMEM_LIMIT),
        name="mixer_sample",
    )(proj, proj, proj, proj, proj, st_lc, st_lh, st_sc, st_sh, *params, sel_t, sel_p)


def _gate_weights(w_a, w_x):
    def tiles(w):
        per = MXU_DIM // LRU_BLOCK
        w4 = w.reshape(N_LRU_HEADS // per, per, LRU_BLOCK, LRU_BLOCK)
        eye = jnp.eye(per, dtype=w.dtype)
        t = jnp.einsum('jaik,ab->jaibk', w4, eye)
        return t.reshape(N_LRU_HEADS // per, MXU_DIM, MXU_DIM)
    return jnp.concatenate([tiles(w_a), tiles(w_x)], axis=2).astype(BF16)


def kernel(x_prompt, x_sample, state_lru_conv, state_lru_h, state_ssd_conv, state_ssd_h, g_mix, w_in,
           lru_conv_w, lru_conv_b, w_a, b_a, w_x, b_x, lam, g_lru_out, ssd_conv_w, ssd_conv_b, dt_bias,
           a_log, d_skip, g_ssd_out, w_out, g_mlp, w_up, w_down, g_final):
    depth = w_in.shape[0]
    assert depth == 1
    bp, seq, _ = x_prompt.shape
    bs, dseq, _ = x_sample.shape
    l = 0
    row = lambda v: v.reshape(1, -1)
    w_main = w_in[l, :, 0:PROJ_MAIN].astype(BF16)
    w_dt = jnp.pad(w_in[l, :, PROJ_MAIN:], ((0, 0), (0, DT_PAD - N_SSD_HEADS))).astype(BF16)
    params = (
        lru_conv_w[l], row(lru_conv_b[l]), _gate_weights(w_a[l], w_x[l]),
        row(b_a[l]), row(b_x[l]), row(lam[l]), row(g_lru_out[l]),
        ssd_conv_w[l], row(ssd_conv_b[l]),
        jnp.pad(row(dt_bias[l]), ((0, 0), (0, DT_PAD - N_SSD_HEADS))),
        jnp.pad(row(a_log[l]), ((0, 0), (0, DT_PAD - N_SSD_HEADS))),
        row(jnp.repeat(d_skip[l], SSD_HEAD_DIM)), row(g_ssd_out[l]),
    )
    w_out_b = w_out[l].astype(BF16)
    w_up_b = w_up[l].astype(BF16)
    w_down_b = w_down[l].astype(BF16)
    gmix = row(g_mix[l])
    gmlp = row(g_mlp[l])
    gfin = row(g_final)

    xp2 = x_prompt.reshape(bp * seq, D_MODEL)
    (lcw, lcb, wg, ba, bx, lam_r, glru, scw, scb, dtb, alog, dskip, gssd) = params
    act_p, p_lc, p_sc = _in_proj_prompt(xp2, bp, gmix, w_main, w_dt, lcw, lcb, scw, scb, dtb)
    sel_t, sel_p = _head_selectors()
    ymix_p, p_lh, p_sh = _mixer_prompt(
        act_p.reshape(bp, seq, PROJ_PAD), wg, ba, bx, lam_r, glru, alog, dskip, gssd, sel_t, sel_p)
    y_prompt = _out_mlp(xp2, ymix_p.reshape(bp * seq, MIX_WIDTH), w_out_b, gmlp, w_up_b, w_down_b, gfin)

    xs2 = x_sample.reshape(bs * dseq, D_MODEL)
    proj_s = _in_proj(xs2, gmix, w_main, w_dt).reshape(bs, dseq, PROJ_PAD)
    ymix_s, s_lc, s_lh, s_sc, s_sh = _mixer_sample(
        proj_s, state_lru_conv[l], state_lru_h[l].reshape(bs, 1, LRU_WIDTH), state_ssd_conv[l],
        state_ssd_h[l].reshape(bs, SSD_WIDTH, D_STATE), params, sel_t, sel_p)
    y_sample = _out_mlp(xs2, ymix_s.reshape(bs * dseq, MIX_WIDTH), w_out_b, gmlp, w_up_b, w_down_b, gfin)

    hshape = (N_SSD_HEADS, SSD_HEAD_DIM, D_STATE)
    return (
        y_prompt.reshape(bp, seq, D_MODEL), y_sample.reshape(bs, dseq, D_MODEL),
        p_lc[None], p_lh.reshape(1, bp, LRU_WIDTH), p_sc[None], p_sh.reshape(1, bp, *hshape),
        s_lc[None], s_lh.reshape(1, bs, LRU_WIDTH), s_sc[None], s_sh.reshape(1, bs, *hshape),
    )
```

```python
import functools
import math

import jax
import jax.numpy as jnp
from jax import lax
from jax.experimental import pallas as pl
from jax.experimental.pallas import tpu as pltpu

F32 = jnp.float32
BF16 = jnp.bfloat16

D_MODEL = 1024
LRU_WIDTH = 1024
N_LRU_HEADS = 16
LRU_BLOCK = 64
LRU_C = 8.0
SSD_WIDTH = 1024
SSD_HEAD_DIM = 64
N_SSD_HEADS = 16
N_SSD_GROUPS = 2
D_STATE = 128
CONV_WIDTH = 4
SSD_CONV_DIM = SSD_WIDTH + 2 * N_SSD_GROUPS * D_STATE
D_FF = 4 * D_MODEL
EPS = 1e-6

LANES = 128
SUBLANES = 8
MXU_DIM = 256
DT_PAD = LANES
PROJ_MAIN = 2 * LRU_WIDTH + SSD_WIDTH + SSD_CONV_DIM
PROJ_PAD = PROJ_MAIN + DT_PAD
MIX_WIDTH = LRU_WIDTH + SSD_WIDTH
SSD_CHUNK = 128
PROMPT_TC = 256
PROMPT_NB = 2
ROW_TILE = 512
SAMPLE_SEQS = SSD_CHUNK // SUBLANES
SCAN_RUN = PROMPT_TC // SUBLANES
SCAN_PITCH = SCAN_RUN + 4
NEG_BIG = -1e30
LOG2E = 1.4426950408889634
VMEM_LIMIT = 56 * 1024 * 1024
HI = lax.Precision.HIGHEST


def _rms(x, g):
    ms = jnp.mean(x * x, axis=-1, keepdims=True)
    return x * lax.rsqrt(ms + EPS) * g


def _sigmoid(x):
    return 1.0 / (1.0 + jnp.exp(-x))


def _softplus(x):
    return jnp.maximum(x, 0.0) + jnp.log1p(jnp.exp(-jnp.abs(x)))


def _gelu_tanh(x):
    c = math.sqrt(2.0 / math.pi)
    return 0.5 * x * (1.0 + jnp.tanh(c * (x + 0.044715 * (x * x * x))))


def _lru_coeffs(u, wg_ref, b_a, b_x, sp_lam):
    ub = u.astype(BF16)
    r_parts, i_parts = [], []
    for j in range(LRU_WIDTH // MXU_DIM):
        g = jnp.dot(ub[:, MXU_DIM * j:MXU_DIM * (j + 1)], wg_ref[j], preferred_element_type=F32)
        r_parts.append(g[:, :MXU_DIM])
        i_parts.append(g[:, MXU_DIM:])
    r = _sigmoid(jnp.concatenate(r_parts, axis=1) + b_a)
    i = _sigmoid(jnp.concatenate(i_parts, axis=1) + b_x)
    log_a = (-LRU_C) * r * sp_lam
    a = jnp.exp(log_a)
    th = jnp.tanh(-log_a)
    v = 2.0 * th / (1.0 + th)
    mult = jnp.where(v > 0.0, v * lax.rsqrt(v), 0.0)
    return a, mult * (i * u)


def _scan_within_8(a, b):
    ridx = lax.broadcasted_iota(jnp.int32, a.shape, 0) & (SUBLANES - 1)
    for k in (1, 2, 4):
        a_s = pltpu.roll(a, k, axis=0)
        b_s = pltpu.roll(b, k, axis=0)
        m = ridx >= k
        b = jnp.where(m, a * b_s + b, b)
        a = jnp.where(m, a * a_s, a)
    return a, b


def _conv_slabs(ext, w_ref, b_ref, rows, first):
    parts = []
    for s in range(ext.shape[0]):
        cols = slice(LANES * s, LANES * (s + 1))
        acc = b_ref[:, cols] + ext[s, pl.ds(first, rows), :] * w_ref[0:1, cols]
        for k in range(1, CONV_WIDTH):
            acc = acc + ext[s, pl.ds(first + k, rows), :] * w_ref[k:k + 1, cols]
        parts.append(acc)
    return jnp.concatenate(parts, axis=1)


def _lru_scan_strided(a, b, hcar, a_pad, b_pad, h_pad):
    rows = a.shape[0]
    S = rows // SUBLANES
    nslab = LRU_WIDTH // LANES
    ridx = lax.broadcasted_iota(jnp.int32, (SUBLANES, LANES), 0)
    step = lambda ref, s, i: ref[s, pl.ds(i, SUBLANES, stride=SCAN_PITCH), :]
    for s in range(nslab):
        cols = slice(LANES * s, LANES * (s + 1))
        for j in range(SUBLANES):
            a_pad[s, SCAN_PITCH * j:SCAN_PITCH * j + S, :] = a[S * j:S * (j + 1), cols]
            b_pad[s, SCAN_PITCH * j:SCAN_PITCH * j + S, :] = b[S * j:S * (j + 1), cols]
    h = [jnp.zeros((SUBLANES, LANES), F32)] * nslab
    prod = [jnp.ones((SUBLANES, LANES), F32)] * nslab
    for i in range(S):
        for s in range(nslab):
            av = step(a_pad, s, i)
            h[s] = av * h[s] + step(b_pad, s, i)
            prod[s] = av * prod[s]
    for s in range(nslab):
        cols = slice(LANES * s, LANES * (s + 1))
        pcum, hcum = _scan_within_8(prod[s], h[s])
        cin = hcar[:, cols]
        ends = hcum + pcum * cin
        h[s] = jnp.where(ridx == 0, cin, pltpu.roll(ends, 1, axis=0))
        hcar[:, cols] = jnp.broadcast_to(ends[SUBLANES - 1:SUBLANES, :], (SUBLANES, LANES))
    for i in range(S):
        for s in range(nslab):
            h[s] = step(a_pad, s, i) * h[s] + step(b_pad, s, i)
            h_pad[s, pl.ds(i, SUBLANES, stride=SCAN_PITCH), :] = h[s]
    return jnp.concatenate(
        [jnp.concatenate([h_pad[s, SCAN_PITCH * j:SCAN_PITCH * j + S, :] for j in range(SUBLANES)], axis=0)
         for s in range(nslab)], axis=1)


def _ssd_intra(xs, bm, cm, dt, a2_row, tri, mask_add, selt_ref):
    L = xs.shape[0]
    cum2 = jnp.dot(tri, dt * a2_row, precision=HI, preferred_element_type=F32)
    cum2_t = cum2.T[0:N_SSD_HEADS, :]
    c2_t = cum2_t - jnp.log2(dt.T[0:N_SSD_HEADS, :])
    cols = _spread(cum2, selt_ref)
    lane = lax.broadcasted_iota(jnp.int32, (L, LANES), 1)
    lo = lane < SSD_HEAD_DIM
    y_parts = []
    for g in range(N_SSD_GROUPS):
        bg = bm[:, D_STATE * g:D_STATE * (g + 1)].astype(BF16)
        cg = cm[:, D_STATE * g:D_STATE * (g + 1)].astype(BF16)
        cb = lax.dot_general(cg, bg, (((1,), (1,)), ((), ())), preferred_element_type=F32)
        for jj in range(N_SSD_HEADS // N_SSD_GROUPS // 2):
            j = (N_SSD_HEADS // N_SSD_GROUPS // 2) * g + jj
            h0, h1 = 2 * j, 2 * j + 1
            col0 = cols[:, LANES * h0:LANES * (h0 + 1)]
            col1 = cols[:, LANES * h1:LANES * (h1 + 1)]
            m0 = cb * jnp.exp2(col0 - c2_t[h0:h0 + 1, :] + mask_add)
            m1 = cb * jnp.exp2(col1 - c2_t[h1:h1 + 1, :] + mask_add)
            lhs = jnp.concatenate([m0, m1], axis=1).astype(BF16)
            xp = xs[:, LANES * j:LANES * (j + 1)]
            rhs = jnp.concatenate([jnp.where(lo, xp, 0.0), jnp.where(lo, 0.0, xp)], axis=0).astype(BF16)
            y_parts.append(jnp.dot(lhs, rhs, preferred_element_type=F32))
    return jnp.concatenate(y_parts, axis=1), cum2, cum2_t


def _spread(v, sel_ref):
    p0 = v.astype(BF16)
    p1 = (v - p0.astype(F32)).astype(BF16)
    return jnp.dot(jnp.concatenate([p0, p1], axis=1), sel_ref[...], preferred_element_type=F32)


def _ssd_gate_norm(ys, xs, z_act, dskip, g_ssd):
    ys = ys + dskip * xs
    gated = ys * z_act
    half = SSD_WIDTH // N_SSD_GROUPS
    outs = []
    for g in range(N_SSD_GROUPS):
        outs.append(_rms(gated[:, half * g:half * (g + 1)], g_ssd[:, half * g:half * (g + 1)]))
    return jnp.concatenate(outs, axis=1)


def _split_w_in_kernel(w_ref, main_ref, dt_ref):
    main_ref[...] = w_ref[:, 0:PROJ_MAIN].astype(BF16)
    dt_ref[...] = jnp.zeros_like(dt_ref)
    dt_ref[:, 0:N_SSD_HEADS] = w_ref[:, PROJ_MAIN:PROJ_MAIN + N_SSD_HEADS].astype(BF16)


def _split_w_in(w):
    rows, cols = w.shape
    blk = MXU_DIM
    return pl.pallas_call(
        _split_w_in_kernel,
        out_shape=(jax.ShapeDtypeStruct((rows, PROJ_MAIN), BF16), jax.ShapeDtypeStruct((rows, DT_PAD), BF16)),
        grid=(rows // blk,),
        in_specs=[pl.BlockSpec((blk, cols), lambda i: (i, 0))],
        out_specs=(pl.BlockSpec((blk, PROJ_MAIN), lambda i: (i, 0)), pl.BlockSpec((blk, DT_PAD), lambda i: (i, 0))),
        compiler_params=pltpu.CompilerParams(dimension_semantics=("parallel",)),
        name="split_w_in",
    )(w)


def _inproj_kernel(x_ref, g_ref, w_ref, wdt_ref, o_ref):
    hn = _rms(x_ref[...], g_ref[...]).astype(BF16)
    o_ref[:, 0:PROJ_MAIN] = jnp.dot(hn, w_ref[...], preferred_element_type=F32)
    o_ref[:, PROJ_MAIN:PROJ_PAD] = jnp.dot(hn, wdt_ref[...], preferred_element_type=F32)


def _in_proj(x2d, g_mix, w_main, w_dt):
    n = x2d.shape[0]
    return pl.pallas_call(
        _inproj_kernel,
        out_shape=jax.ShapeDtypeStruct((n, PROJ_PAD), F32),
        grid=(n // ROW_TILE,),
        in_specs=[
            pl.BlockSpec((ROW_TILE, D_MODEL), lambda i: (i, 0)),
            pl.BlockSpec((1, D_MODEL), lambda i: (0, 0)),
            pl.BlockSpec((D_MODEL, PROJ_MAIN), lambda i: (0, 0), pipeline_mode=pl.Buffered(1)),
            pl.BlockSpec((D_MODEL, DT_PAD), lambda i: (0, 0), pipeline_mode=pl.Buffered(1)),
        ],
        out_specs=pl.BlockSpec((ROW_TILE, PROJ_PAD), lambda i: (i, 0)),
        compiler_params=pltpu.CompilerParams(
            dimension_semantics=("parallel",), vmem_limit_bytes=VMEM_LIMIT),
        name="in_proj",
    )(x2d, g_mix, w_main, w_dt)


def _inproj_prompt_kernel(x_ref, g_ref, w_ref, wdt_ref, lcw_ref, lcb_ref, scw_ref, scb_ref, dtb_ref,
                          o_ref, olc_ref, osc_ref, ext_l, ext_s, *, steps_per_seq):
    t = lax.rem(pl.program_id(0), steps_per_seq)
    rows = ROW_TILE
    hist = SUBLANES
    o1, o2, o3 = LRU_WIDTH, 2 * LRU_WIDTH, 2 * LRU_WIDTH + SSD_WIDTH

    @pl.when(t == 0)
    def _init():
        ext_l[:, 0:hist, :] = jnp.zeros((ext_l.shape[0], hist, LANES), F32)
        ext_s[:, 0:hist, :] = jnp.zeros((ext_s.shape[0], hist, LANES), F32)

    hn = _rms(x_ref[...], g_ref[...]).astype(BF16)
    lx = jnp.dot(hn, w_ref[:, 0:o1], preferred_element_type=F32)
    for s in range(ext_l.shape[0]):
        ext_l[s, hist:hist + rows, :] = lx[:, LANES * s:LANES * (s + 1)]
    xbc_in = jnp.dot(hn, w_ref[:, o3:PROJ_MAIN], preferred_element_type=F32)
    for s in range(ext_s.shape[0]):
        ext_s[s, hist:hist + rows, :] = xbc_in[:, LANES * s:LANES * (s + 1)]
    o_ref[:, o1:o2] = _gelu_tanh(jnp.dot(hn, w_ref[:, o1:o2], preferred_element_type=F32))
    z = jnp.dot(hn, w_ref[:, o2:o3], preferred_element_type=F32)
    o_ref[:, o2:o3] = z * _sigmoid(z)
    o_ref[:, PROJ_MAIN:PROJ_PAD] = _softplus(
        jnp.dot(hn, wdt_ref[...], preferred_element_type=F32) + dtb_ref[...])
    o_ref[:, 0:o1] = _conv_slabs(ext_l, lcw_ref, lcb_ref, rows, hist - (CONV_WIDTH - 1))
    xbc = _conv_slabs(ext_s, scw_ref, scb_ref, rows, hist - (CONV_WIDTH - 1))
    o_ref[:, o3:PROJ_MAIN] = xbc * _sigmoid(xbc)

    @pl.when(t == steps_per_seq - 1)
    def _final():
        last = slice(hist + rows - (CONV_WIDTH - 1), hist + rows)
        for s in range(ext_l.shape[0]):
            olc_ref[:, LANES * s:LANES * (s + 1)] = ext_l[s, last, :]
        for s in range(ext_s.shape[0]):
            osc_ref[:, LANES * s:LANES * (s + 1)] = ext_s[s, last, :]

    tail_l = ext_l[:, rows:rows + hist, :]
    tail_s = ext_s[:, rows:rows + hist, :]
    ext_l[:, 0:hist, :] = tail_l
    ext_s[:, 0:hist, :] = tail_s


def _in_proj_prompt(x2d, bsz, g_mix, w_main, w_dt, lcw, lcb, scw, scb, dtb):
    n = x2d.shape[0]
    steps_per_seq = n // bsz // ROW_TILE
    const = lambda i: (0, 0)
    return pl.pallas_call(
        functools.partial(_inproj_prompt_kernel, steps_per_seq=steps_per_seq),
        out_shape=(
            jax.ShapeDtypeStruct((n, PROJ_PAD), F32),
            jax.ShapeDtypeStruct((bsz, CONV_WIDTH - 1, LRU_WIDTH), F32),
            jax.ShapeDtypeStruct((bsz, CONV_WIDTH - 1, SSD_CONV_DIM), F32),
        ),
        grid=(n // ROW_TILE,),
        in_specs=[
            pl.BlockSpec((ROW_TILE, D_MODEL), lambda i: (i, 0)),
            pl.BlockSpec((1, D_MODEL), const),
            pl.BlockSpec((D_MODEL, PROJ_MAIN), const, pipeline_mode=pl.Buffered(1)),
            pl.BlockSpec((D_MODEL, DT_PAD), const, pipeline_mode=pl.Buffered(1)),
            pl.BlockSpec((CONV_WIDTH, LRU_WIDTH), const),
            pl.BlockSpec((1, LRU_WIDTH), const),
            pl.BlockSpec((CONV_WIDTH, SSD_CONV_DIM), const),
            pl.BlockSpec((1, SSD_CONV_DIM), const),
            pl.BlockSpec((1, DT_PAD), const),
        ],
        out_specs=(
            pl.BlockSpec((ROW_TILE, PROJ_PAD), lambda i: (i, 0)),
            pl.BlockSpec((None, CONV_WIDTH - 1, LRU_WIDTH), lambda i: (i // steps_per_seq, 0, 0)),
            pl.BlockSpec((None, CONV_WIDTH - 1, SSD_CONV_DIM), lambda i: (i // steps_per_seq, 0, 0)),
        ),
        scratch_shapes=[
            pltpu.VMEM((LRU_WIDTH // LANES, SUBLANES + ROW_TILE, LANES), F32),
            pltpu.VMEM((SSD_CONV_DIM // LANES, SUBLANES + ROW_TILE, LANES), F32),
        ],
        compiler_params=pltpu.CompilerParams(
            dimension_semantics=("arbitrary",), vmem_limit_bytes=VMEM_LIMIT),
        name="in_proj_prompt",
    )(x2d, g_mix, w_main, w_dt, lcw, lcb, scw, scb, dtb)


def _outmlp_kernel(x_ref, y_ref, wo_ref, gm_ref, wu_ref, wd_ref, gf_ref, o_ref):
    x1 = x_ref[...] + jnp.dot(y_ref[...].astype(BF16), wo_ref[...], preferred_element_type=F32)
    m = _rms(x1, gm_ref[...]).astype(BF16)
    u = jnp.dot(m, wu_ref[...], preferred_element_type=F32)
    u = jnp.square(jnp.maximum(u, 0.0)).astype(BF16)
    x2 = x1 + jnp.dot(u, wd_ref[...], preferred_element_type=F32)
    o_ref[...] = _rms(x2, gf_ref[...])


def _out_mlp(x2d, ymix2d, w_out_b, g_mlp, w_up_b, w_down_b, g_final):
    n = x2d.shape[0]
    const = lambda i: (0, 0)
    return pl.pallas_call(
        _outmlp_kernel,
        out_shape=jax.ShapeDtypeStruct((n, D_MODEL), F32),
        grid=(n // ROW_TILE,),
        in_specs=[
            pl.BlockSpec((ROW_TILE, D_MODEL), lambda i: (i, 0)),
            pl.BlockSpec((ROW_TILE, MIX_WIDTH), lambda i: (i, 0)),
            pl.BlockSpec((MIX_WIDTH, D_MODEL), const, pipeline_mode=pl.Buffered(1)),
            pl.BlockSpec((1, D_MODEL), const),
            pl.BlockSpec((D_MODEL, D_FF), const, pipeline_mode=pl.Buffered(1)),
            pl.BlockSpec((D_FF, D_MODEL), const, pipeline_mode=pl.Buffered(1)),
            pl.BlockSpec((1, D_MODEL), const),
        ],
        out_specs=pl.BlockSpec((ROW_TILE, D_MODEL), lambda i: (i, 0)),
        compiler_params=pltpu.CompilerParams(
            dimension_semantics=("parallel",), vmem_limit_bytes=VMEM_LIMIT),
        name="out_mlp",
    )(x2d, ymix2d, w_out_b, g_mlp, w_up_b, w_down_b, g_final)


def _mixer_prompt_kernel(u_ref, gl_ref, zact_ref, xbc_ref, dt_ref,
                         wg_ref, ba_ref, bx_ref, lam_ref, glru_ref, alog_ref, dskip_ref, gssd_ref,
                         selt_ref, selp_ref,
                         y_ref, olh_ref, osh_ref,
                         a_pad, b_pad, h_pad, hcar, ht):
    t = pl.program_id(1)
    nt = pl.num_programs(1)
    tc = PROMPT_TC

    @pl.when(t == 0)
    def _init():
        hcar[...] = jnp.zeros_like(hcar)
        ht[...] = jnp.zeros_like(ht)

    sp_lam = _softplus(-lam_ref[...])
    lane1 = lax.broadcasted_iota(jnp.int32, (1, LANES), 1)
    a2_row = jnp.where(lane1 < N_SSD_HEADS, -LOG2E * jnp.exp(alog_ref[...]), 0.0)
    L = SSD_CHUNK
    rr = lax.broadcasted_iota(jnp.int32, (L, L), 0)
    cc = lax.broadcasted_iota(jnp.int32, (L, L), 1)
    causal = cc <= rr
    tri = jnp.where(causal, 1.0, 0.0).astype(F32)
    mask_add = jnp.where(causal, 0.0, NEG_BIG).astype(F32)
    half = SSD_WIDTH // N_SSD_GROUPS

    for n in range(PROMPT_NB):
        a, b = _lru_coeffs(u_ref[n], wg_ref, ba_ref[...], bx_ref[...], sp_lam)
        hseq = _lru_scan_strided(a, b, hcar.at[n], a_pad.at[n], b_pad.at[n], h_pad.at[n])
        y_ref[n, :, 0:LRU_WIDTH] = _rms(hseq * gl_ref[n], glru_ref[...])

        for c in range(tc // L):
            rows = slice(L * c, L * (c + 1))
            xs = xbc_ref[n, rows, 0:SSD_WIDTH]
            bm = xbc_ref[n, rows, SSD_WIDTH:SSD_WIDTH + N_SSD_GROUPS * D_STATE]
            cm = xbc_ref[n, rows, SSD_WIDTH + N_SSD_GROUPS * D_STATE:SSD_CONV_DIM]
            dt = dt_ref[n, rows, :]
            y_diag, cum2, _ = _ssd_intra(xs, bm, cm, dt, a2_row, tri, mask_add, selt_ref)
            ecol = _spread(jnp.exp2(cum2), selp_ref)
            xw = xs * _spread(jnp.exp2(cum2[L - 1:L, :] - cum2) * dt, selp_ref)
            dec = ecol[L - 1:L, :]
            y_off_parts = []
            for g in range(N_SSD_GROUPS):
                htg = ht[n, g]
                cg = cm[:, D_STATE * g:D_STATE * (g + 1)].astype(BF16)
                y_off_parts.append(jnp.dot(cg, htg.astype(BF16), preferred_element_type=F32))
                bg_t = bm[:, D_STATE * g:D_STATE * (g + 1)].T.astype(BF16)
                st = jnp.dot(bg_t, xw[:, half * g:half * (g + 1)].astype(BF16), preferred_element_type=F32)
                ht[n, g] = htg * dec[:, half * g:half * (g + 1)] + st
            ys = y_diag + jnp.concatenate(y_off_parts, axis=1) * ecol
            y_ref[n, rows, LRU_WIDTH:MIX_WIDTH] = _ssd_gate_norm(
                ys, xs, zact_ref[n, rows, :], dskip_ref[...], gssd_ref[...])

    @pl.when(t == nt - 1)
    def _final():
        for n in range(PROMPT_NB):
            olh_ref[n] = hcar[n, 0:1, :]
            for g in range(N_SSD_GROUPS):
                osh_ref[n, half * g:half * (g + 1), :] = ht[n, g].T


def _param_specs(const):
    return [
        pl.BlockSpec((CONV_WIDTH, LRU_WIDTH), const),
        pl.BlockSpec((1, LRU_WIDTH), const),
        pl.BlockSpec((LRU_WIDTH // MXU_DIM, MXU_DIM, 2 * MXU_DIM), lambda *_: (0, 0, 0)),
        pl.BlockSpec((1, LRU_WIDTH), const),
        pl.BlockSpec((1, LRU_WIDTH), const),
        pl.BlockSpec((1, LRU_WIDTH), const),
        pl.BlockSpec((1, LRU_WIDTH), const),
        pl.BlockSpec((CONV_WIDTH, SSD_CONV_DIM), const),
        pl.BlockSpec((1, SSD_CONV_DIM), const),
        pl.BlockSpec((1, DT_PAD), const),
        pl.BlockSpec((1, DT_PAD), const),
        pl.BlockSpec((1, SSD_WIDTH), const),
        pl.BlockSpec((1, SSD_WIDTH), const),
    ]


def _head_selectors():
    k = jnp.arange(2 * LANES)[:, None] % LANES
    sel_t = (k == jnp.arange(N_SSD_HEADS * LANES)[None, :] // LANES).astype(BF16)
    sel_p = (k == jnp.arange(SSD_WIDTH)[None, :] // SSD_HEAD_DIM).astype(BF16)
    return sel_t, sel_p


def _mixer_prompt(act, wg, b_a, b_x, lam, g_lru, a_log, d_skip, g_ssd, sel_t, sel_p):
    bsz, seq, _ = act.shape
    tc = PROMPT_TC
    nb = PROMPT_NB
    const = lambda b, t: (0, 0)
    in_specs = [
        pl.BlockSpec((nb, tc, LRU_WIDTH), lambda b, t: (b, t, 0)),
        pl.BlockSpec((nb, tc, LRU_WIDTH), lambda b, t: (b, t, 1)),
        pl.BlockSpec((nb, tc, SSD_WIDTH), lambda b, t: (b, t, 2)),
        pl.BlockSpec((nb, tc, SSD_CONV_DIM), lambda b, t: (b, t, 2)),
        pl.BlockSpec((nb, tc, DT_PAD), lambda b, t: (b, t, PROJ_MAIN // DT_PAD)),
        pl.BlockSpec((LRU_WIDTH // MXU_DIM, MXU_DIM, 2 * MXU_DIM), lambda b, t: (0, 0, 0)),
        pl.BlockSpec((1, LRU_WIDTH), const),
        pl.BlockSpec((1, LRU_WIDTH), const),
        pl.BlockSpec((1, LRU_WIDTH), const),
        pl.BlockSpec((1, LRU_WIDTH), const),
        pl.BlockSpec((1, DT_PAD), const),
        pl.BlockSpec((1, SSD_WIDTH), const),
        pl.BlockSpec((1, SSD_WIDTH), const),
        pl.BlockSpec((2 * LANES, N_SSD_HEADS * LANES), const),
        pl.BlockSpec((2 * LANES, SSD_WIDTH), const),
    ]
    out_shape = (
        jax.ShapeDtypeStruct((bsz, seq, MIX_WIDTH), F32),
        jax.ShapeDtypeStruct((bsz, 1, LRU_WIDTH), F32),
        jax.ShapeDtypeStruct((bsz, SSD_WIDTH, D_STATE), F32),
    )
    out_specs = (
        pl.BlockSpec((nb, tc, MIX_WIDTH), lambda b, t: (b, t, 0)),
        pl.BlockSpec((nb, 1, LRU_WIDTH), lambda b, t: (b, 0, 0)),
        pl.BlockSpec((nb, SSD_WIDTH, D_STATE), lambda b, t: (b, 0, 0)),
    )
    scratch = [
        pltpu.VMEM((nb, LRU_WIDTH // LANES, SUBLANES * SCAN_PITCH, LANES), F32),
        pltpu.VMEM((nb, LRU_WIDTH // LANES, SUBLANES * SCAN_PITCH, LANES), F32),
        pltpu.VMEM((nb, LRU_WIDTH // LANES, SUBLANES * SCAN_PITCH, LANES), F32),
        pltpu.VMEM((nb, SUBLANES, LRU_WIDTH), F32),
        pltpu.VMEM((nb, N_SSD_GROUPS, D_STATE, SSD_WIDTH // N_SSD_GROUPS), F32),
    ]
    return pl.pallas_call(
        _mixer_prompt_kernel,
        out_shape=out_shape,
        grid=(bsz // nb, seq // tc),
        in_specs=in_specs,
        out_specs=out_specs,
        scratch_shapes=scratch,
        compiler_params=pltpu.CompilerParams(
            dimension_semantics=("parallel", "arbitrary"), vmem_limit_bytes=VMEM_LIMIT),
        name="mixer_prompt",
    )(act, act, act, act, act, wg, b_a, b_x, lam, g_lru, a_log, d_skip, g_ssd, sel_t, sel_p)


def _mixer_sample_kernel(lx_ref, gate_ref, z_ref, xbc_ref, dt_ref,
                         slc_ref, slh_ref, ssc_ref, ssh_ref,
                         lcw_ref, lcb_ref, wg_ref, ba_ref, bx_ref, lam_ref, glru_ref,
                         scw_ref, scb_ref, dtb_ref, alog_ref, dskip_ref, gssd_ref, selt_ref, selp_ref,
                         y_ref, olc_ref, olh_ref, osc_ref, osh_ref,
                         ext_l, ext_s, pad_scr, yoff_scr, *, T):
    S = SAMPLE_SEQS
    P = SUBLANES
    K1 = CONV_WIDTH - 1
    R = S * P
    row_i = lax.broadcasted_iota(jnp.int32, (R, 1), 0) & (P - 1)
    valid = row_i < T

    def pad_rows(ref):
        width = ref.shape[-1]
        pad_scr[:, :, 0:width] = jnp.zeros((S, P, width), F32)
        pad_scr[:, 0:T, 0:width] = ref[...].reshape(S, T, width)
        return pad_scr[:, :, 0:width].reshape(R, width)

    ext_l[...] = jnp.zeros_like(ext_l)
    ext_s[...] = jnp.zeros_like(ext_s)
    ext_l[:, 0:K1, :] = slc_ref[...]
    ext_l[:, K1:K1 + T, :] = lx_ref[...].reshape(S, T, LRU_WIDTH)
    ext_s[:, 0:K1, :] = ssc_ref[...]
    ext_s[:, K1:K1 + T, :] = xbc_ref[...].reshape(S, T, SSD_CONV_DIM)
    olc_ref[...] = ext_l[:, T:T + K1, :]
    osc_ref[...] = ext_s[:, T:T + K1, :]

    el = ext_l[...].reshape(R, LRU_WIDTH)
    es = ext_s[...].reshape(R, SSD_CONV_DIM)

    def conv(e, w_ref, b_ref):
        out = b_ref[...] + e * w_ref[0:1, :]
        for k in range(1, CONV_WIDTH):
            out = out + pltpu.roll(e, R - k, axis=0) * w_ref[k:k + 1, :]
        return out

    u = conv(el, lcw_ref, lcb_ref)
    sp_lam = _softplus(-lam_ref[...])
    a, b = _lru_coeffs(u, wg_ref, ba_ref[...], bx_ref[...], sp_lam)
    a, b = _scan_within_8(a, b)
    h0 = jnp.broadcast_to(slh_ref[...], (S, P, LRU_WIDTH)).reshape(R, LRU_WIDTH)
    hseq = a * h0 + b
    olh_ref[...] = hseq.reshape(S, P, LRU_WIDTH)[:, T - 1:T, :]
    gate = pad_rows(gate_ref)
    y_lru = _rms(hseq * _gelu_tanh(gate), glru_ref[...])

    xbc = conv(es, scw_ref, scb_ref)
    xbc = xbc * _sigmoid(xbc)
    xs = xbc[:, 0:SSD_WIDTH]
    bm = xbc[:, SSD_WIDTH:SSD_WIDTH + N_SSD_GROUPS * D_STATE]
    cm = xbc[:, SSD_WIDTH + N_SSD_GROUPS * D_STATE:]
    dt_raw = pad_rows(dt_ref)
    dt = jnp.where(valid, _softplus(dt_raw + dtb_ref[...]), 0.0)
    lane1 = lax.broadcasted_iota(jnp.int32, (1, LANES), 1)
    a2_row = jnp.where(lane1 < N_SSD_HEADS, -LOG2E * jnp.exp(alog_ref[...]), 0.0)

    rr = lax.broadcasted_iota(jnp.int32, (R, R), 0)
    cc = lax.broadcasted_iota(jnp.int32, (R, R), 1)
    allowed = (cc <= rr) & ((rr - cc) <= (rr & (P - 1)))
    tri = jnp.where(allowed, 1.0, 0.0).astype(F32)
    mask_add = jnp.where(allowed, 0.0, NEG_BIG).astype(F32)

    y_diag, cum2, cum2_t = _ssd_intra(xs, bm, cm, dt, a2_row, tri, mask_add, selt_ref)
    ecol = _spread(jnp.exp2(cum2), selp_ref)
    end2 = jnp.broadcast_to(cum2.reshape(S, P, LANES)[:, P - 1:P, :], (S, P, LANES)).reshape(R, LANES)
    xw = xs * _spread(jnp.exp2(end2 - cum2) * dt, selp_ref)
    ecum_t = jnp.exp2(cum2_t)

    half = SSD_WIDTH // N_SSD_GROUPS
    for q in range(S):
        r0 = P * q
        vq = jnp.broadcast_to(ecum_t[:, r0 + P - 1:r0 + P], (N_SSD_HEADS, LANES))
        for g in range(N_SSD_GROUPS):
            hqg = ssh_ref[q, half * g:half * (g + 1), :]
            cq = cm[r0:r0 + P, D_STATE * g:D_STATE * (g + 1)].astype(BF16)
            yoff_scr[r0:r0 + P, half * g:half * (g + 1)] = lax.dot_general(
                cq, hqg.astype(BF16), (((1,), (1,)), ((), ())), preferred_element_type=F32)
            bq = bm[r0:r0 + P, D_STATE * g:D_STATE * (g + 1)].astype(BF16)
            xq = xw[r0:r0 + P, half * g:half * (g + 1)].astype(BF16)
            st = lax.dot_general(xq, bq, (((0,), (0,)), ((), ())), preferred_element_type=F32)
            for e in range(N_SSD_HEADS // N_SSD_GROUPS):
                h = (N_SSD_HEADS // N_SSD_GROUPS) * g + e
                lo_r = SSD_HEAD_DIM * e
                osh_ref[q, SSD_HEAD_DIM * h:SSD_HEAD_DIM * (h + 1), :] = (
                    vq[h:h + 1, :] * hqg[lo_r:lo_r + SSD_HEAD_DIM, :] + st[lo_r:lo_r + SSD_HEAD_DIM, :])

    ys = y_diag + yoff_scr[...] * ecol
    z = pad_rows(z_ref)
    y_ssd = _ssd_gate_norm(ys, xs, z * _sigmoid(z), dskip_ref[...], gssd_ref[...])
    y_ref[:, 0:LRU_WIDTH] = y_lru.reshape(S, P, LRU_WIDTH)[:, 0:T, :].reshape(S * T, LRU_WIDTH)
    y_ref[:, LRU_WIDTH:MIX_WIDTH] = y_ssd.reshape(S, P, SSD_WIDTH)[:, 0:T, :].reshape(S * T, SSD_WIDTH)


def _mixer_sample(proj, T, st_lc, st_lh, st_sc, st_sh, params, sel_t, sel_p):
    nseq = proj.shape[0] // T
    S = SAMPLE_SEQS
    const = lambda i: (0, 0)
    in_specs = [
        pl.BlockSpec((S * T, LRU_WIDTH), lambda i: (i, 0)),
        pl.BlockSpec((S * T, LRU_WIDTH), lambda i: (i, 1)),
        pl.BlockSpec((S * T, SSD_WIDTH), lambda i: (i, 2)),
        pl.BlockSpec((S * T, SSD_CONV_DIM), lambda i: (i, 2)),
        pl.BlockSpec((S * T, DT_PAD), lambda i: (i, PROJ_MAIN // DT_PAD)),
        pl.BlockSpec((S, CONV_WIDTH - 1, LRU_WIDTH), lambda i: (i, 0, 0)),
        pl.BlockSpec((S, 1, LRU_WIDTH), lambda i: (i, 0, 0)),
        pl.BlockSpec((S, CONV_WIDTH - 1, SSD_CONV_DIM), lambda i: (i, 0, 0)),
        pl.BlockSpec((S, SSD_WIDTH, D_STATE), lambda i: (i, 0, 0)),
    ] + _param_specs(const) + [
        pl.BlockSpec((2 * LANES, N_SSD_HEADS * LANES), const),
        pl.BlockSpec((2 * LANES, SSD_WIDTH), const),
    ]
    out_shape = (
        jax.ShapeDtypeStruct((nseq * T, MIX_WIDTH), F32),
        jax.ShapeDtypeStruct((nseq, CONV_WIDTH - 1, LRU_WIDTH), F32),
        jax.ShapeDtypeStruct((nseq, 1, LRU_WIDTH), F32),
        jax.ShapeDtypeStruct((nseq, CONV_WIDTH - 1, SSD_CONV_DIM), F32),
        jax.ShapeDtypeStruct((nseq, SSD_WIDTH, D_STATE), F32),
    )
    out_specs = (
        pl.BlockSpec((S * T, MIX_WIDTH), lambda i: (i, 0)),
        pl.BlockSpec((S, CONV_WIDTH - 1, LRU_WIDTH), lambda i: (i, 0, 0)),
        pl.BlockSpec((S, 1, LRU_WIDTH), lambda i: (i, 0, 0)),
        pl.BlockSpec((S, CONV_WIDTH - 1, SSD_CONV_DIM), lambda i: (i, 0, 0)),
        pl.BlockSpec((S, SSD_WIDTH, D_STATE), lambda i: (i, 0, 0)),
    )
    scratch = [
        pltpu.VMEM((S, SUBLANES, LRU_WIDTH), F32),
        pltpu.VMEM((S, SUBLANES, SSD_CONV_DIM), F32),
        pltpu.VMEM((S, SUBLANES, LRU_WIDTH), F32),
        pltpu.VMEM((S * SUBLANES, SSD_WIDTH), F32),
    ]
    return pl.pallas_call(
        functools.partial(_mixer_sample_kernel, T=T),
        out_shape=out_shape,
        grid=(nseq // S,),
        in_specs=in_specs,
        out_specs=out_specs,
        scratch_shapes=scratch,
        compiler_params=pltpu.CompilerParams(
            dimension_semantics=("parallel",), vmem_limit_bytes=VMEM_LIMIT),
        name="mixer_sample",
    )(proj, proj, proj, proj, proj, st_lc, st_lh, st_sc, st_sh, *params, sel_t, sel_p)


def _gate_weights(w_a, w_x):
    def tiles(w):
        per = MXU_DIM // LRU_BLOCK
        w4 = w.reshape(N_LRU_HEADS // per, per, LRU_BLOCK, LRU_BLOCK)
        eye = jnp.eye(per, dtype=w.dtype)
        t = jnp.einsum('jaik,ab->jaibk', w4, eye)
        return t.reshape(N_LRU_HEADS // per, MXU_DIM, MXU_DIM)
    return jnp.concatenate([tiles(w_a), tiles(w_x)], axis=2).astype(BF16)


def kernel(x_prompt, x_sample, state_lru_conv, state_lru_h, state_ssd_conv, state_ssd_h, g_mix, w_in,
           lru_conv_w, lru_conv_b, w_a, b_a, w_x, b_x, lam, g_lru_out, ssd_conv_w, ssd_conv_b, dt_bias,
           a_log, d_skip, g_ssd_out, w_out, g_mlp, w_up, w_down, g_final):
    depth = w_in.shape[0]
    assert depth == 1
    bp, seq, _ = x_prompt.shape
    bs, dseq, _ = x_sample.shape
    l = 0
    row = lambda v: v.reshape(1, -1)
    w_main, w_dt = _split_w_in(w_in[l])
    params = (
        lru_conv_w[l], row(lru_conv_b[l]), _gate_weights(w_a[l], w_x[l]),
        row(b_a[l]), row(b_x[l]), row(lam[l]), row(g_lru_out[l]),
        ssd_conv_w[l], row(ssd_conv_b[l]),
        jnp.pad(row(dt_bias[l]), ((0, 0), (0, DT_PAD - N_SSD_HEADS))),
        jnp.pad(row(a_log[l]), ((0, 0), (0, DT_PAD - N_SSD_HEADS))),
        row(jnp.repeat(d_skip[l], SSD_HEAD_DIM)), row(g_ssd_out[l]),
    )
    w_out_b = w_out[l].astype(BF16)
    w_up_b = w_up[l].astype(BF16)
    w_down_b = w_down[l].astype(BF16)
    gmix = row(g_mix[l])
    gmlp = row(g_mlp[l])
    gfin = row(g_final)

    xp2 = x_prompt.reshape(bp * seq, D_MODEL)
    (lcw, lcb, wg, ba, bx, lam_r, glru, scw, scb, dtb, alog, dskip, gssd) = params
    act_p, p_lc, p_sc = _in_proj_prompt(xp2, bp, gmix, w_main, w_dt, lcw, lcb, scw, scb, dtb)
    sel_t, sel_p = _head_selectors()
    ymix_p, p_lh, p_sh = _mixer_prompt(
        act_p.reshape(bp, seq, PROJ_PAD), wg, ba, bx, lam_r, glru, alog, dskip, gssd, sel_t, sel_p)
    y_prompt = _out_mlp(xp2, ymix_p.reshape(bp * seq, MIX_WIDTH), w_out_b, gmlp, w_up_b, w_down_b, gfin)

    xs2 = x_sample.reshape(bs * dseq, D_MODEL)
    proj_s = _in_proj(xs2, gmix, w_main, w_dt)
    ymix_s, s_lc, s_lh, s_sc, s_sh = _mixer_sample(
        proj_s, dseq, state_lru_conv[l], state_lru_h[l].reshape(bs, 1, LRU_WIDTH), state_ssd_conv[l],
        state_ssd_h[l].reshape(bs, SSD_WIDTH, D_STATE), params, sel_t, sel_p)
    y_sample = _out_mlp(xs2, ymix_s, w_out_b, gmlp, w_up_b, w_down_b, gfin)

    hshape = (N_SSD_HEADS, SSD_HEAD_DIM, D_STATE)
    return (
        y_prompt.reshape(bp, seq, D_MODEL), y_sample.reshape(bs, dseq, D_MODEL),
        p_lc[None], p_lh.reshape(1, bp, LRU_WIDTH), p_sc[None], p_sh.reshape(1, bp, *hshape),
        s_lc[None], s_lh.reshape(1, bs, LRU_WIDTH), s_sc[None], s_sh.reshape(1, bs, *hshape),
    )
```

```python
import functools
import math

import jax
import jax.numpy as jnp
from jax import lax
from jax.experimental import pallas as pl
from jax.experimental.pallas import tpu as pltpu

F32 = jnp.float32
BF16 = jnp.bfloat16

D_MODEL = 1024
LRU_WIDTH = 1024
N_LRU_HEADS = 16
LRU_BLOCK = 64
LRU_C = 8.0
SSD_WIDTH = 1024
SSD_HEAD_DIM = 64
N_SSD_HEADS = 16
N_SSD_GROUPS = 2
D_STATE = 128
CONV_WIDTH = 4
SSD_CONV_DIM = SSD_WIDTH + 2 * N_SSD_GROUPS * D_STATE
D_FF = 4 * D_MODEL
EPS = 1e-6

LANES = 128
SUBLANES = 8
MXU_DIM = 256
DT_PAD = LANES
PROJ_MAIN = 2 * LRU_WIDTH + SSD_WIDTH + SSD_CONV_DIM
PROJ_PAD = PROJ_MAIN + DT_PAD
MIX_WIDTH = LRU_WIDTH + SSD_WIDTH
SSD_CHUNK = 128
PROMPT_TC = 256
PROMPT_NB = 2
ROW_TILE = 512
SAMPLE_SEQS = SSD_CHUNK // SUBLANES
SCAN_RUN = PROMPT_TC // SUBLANES
SCAN_PITCH = SCAN_RUN + 4
NEG_BIG = -1e30
LOG2E = 1.4426950408889634
VMEM_LIMIT = 56 * 1024 * 1024
HI = lax.Precision.HIGHEST


def _rms(x, g):
    ms = jnp.mean(x * x, axis=-1, keepdims=True)
    return x * lax.rsqrt(ms + EPS) * g


def _sigmoid(x):
    return 1.0 / (1.0 + jnp.exp(-x))


def _softplus(x):
    return jnp.maximum(x, 0.0) + jnp.log1p(jnp.exp(-jnp.abs(x)))


def _gelu_tanh(x):
    c = math.sqrt(2.0 / math.pi)
    return 0.5 * x * (1.0 + jnp.tanh(c * (x + 0.044715 * (x * x * x))))


def _lru_coeffs(u, wg_ref, b_a, b_x, neg_c_sp):
    ub = u.astype(BF16)
    r_parts, i_parts = [], []
    for j in range(LRU_WIDTH // MXU_DIM):
        g = jnp.dot(ub[:, MXU_DIM * j:MXU_DIM * (j + 1)], wg_ref[j], preferred_element_type=F32)
        r_parts.append(g[:, :MXU_DIM])
        i_parts.append(g[:, MXU_DIM:])
    r = _sigmoid(jnp.concatenate(r_parts, axis=1) + b_a)
    i = _sigmoid(jnp.concatenate(i_parts, axis=1) + b_x)
    log_a = r * neg_c_sp
    a = jnp.exp(log_a)
    th = jnp.tanh(log_a)
    v = (th + th) / (th - 1.0)
    mult = jnp.where(v > 0.0, v * lax.rsqrt(v), 0.0)
    return a, mult * (i * u)


def _scan_within_8(a, b):
    ridx = lax.broadcasted_iota(jnp.int32, a.shape, 0) & (SUBLANES - 1)
    for k in (1, 2, 4):
        a_s = pltpu.roll(a, k, axis=0)
        b_s = pltpu.roll(b, k, axis=0)
        m = ridx >= k
        b = jnp.where(m, a * b_s + b, b)
        a = jnp.where(m, a * a_s, a)
    return a, b


def _conv_slabs(ext, w_ref, b_ref, rows, first):
    parts = []
    for s in range(ext.shape[0]):
        cols = slice(LANES * s, LANES * (s + 1))
        acc = b_ref[:, cols] + ext[s, pl.ds(first, rows), :] * w_ref[0:1, cols]
        for k in range(1, CONV_WIDTH):
            acc = acc + ext[s, pl.ds(first + k, rows), :] * w_ref[k:k + 1, cols]
        parts.append(acc)
    return jnp.concatenate(parts, axis=1)


def _lru_scan_strided(a, b, hcar, a_pad, b_pad, h_pad):
    rows = a.shape[0]
    S = rows // SUBLANES
    nslab = LRU_WIDTH // LANES
    ridx = lax.broadcasted_iota(jnp.int32, (SUBLANES, LANES), 0)
    step = lambda ref, s, i: ref[s, pl.ds(i, SUBLANES, stride=SCAN_PITCH), :]
    for s in range(nslab):
        cols = slice(LANES * s, LANES * (s + 1))
        for j in range(SUBLANES):
            a_pad[s, SCAN_PITCH * j:SCAN_PITCH * j + S, :] = a[S * j:S * (j + 1), cols]
            b_pad[s, SCAN_PITCH * j:SCAN_PITCH * j + S, :] = b[S * j:S * (j + 1), cols]
    h = [jnp.zeros((SUBLANES, LANES), F32)] * nslab
    prod = [jnp.ones((SUBLANES, LANES), F32)] * nslab
    for i in range(S):
        for s in range(nslab):
            av = step(a_pad, s, i)
            h[s] = av * h[s] + step(b_pad, s, i)
            prod[s] = av * prod[s]
    for s in range(nslab):
        cols = slice(LANES * s, LANES * (s + 1))
        pcum, hcum = _scan_within_8(prod[s], h[s])
        cin = hcar[:, cols]
        ends = hcum + pcum * cin
        h[s] = jnp.where(ridx == 0, cin, pltpu.roll(ends, 1, axis=0))
        hcar[:, cols] = jnp.broadcast_to(ends[SUBLANES - 1:SUBLANES, :], (SUBLANES, LANES))
    for i in range(S):
        for s in range(nslab):
            h[s] = step(a_pad, s, i) * h[s] + step(b_pad, s, i)
            h_pad[s, pl.ds(i, SUBLANES, stride=SCAN_PITCH), :] = h[s]
    return jnp.concatenate(
        [jnp.concatenate([h_pad[s, SCAN_PITCH * j:SCAN_PITCH * j + S, :] for j in range(SUBLANES)], axis=0)
         for s in range(nslab)], axis=1)


def _ssd_intra(xs, bm, cm, dt, a2_row, tri, mask_add, selt_ref):
    L = xs.shape[0]
    cum2 = jnp.dot(tri, dt * a2_row, precision=HI, preferred_element_type=F32)
    cum2_t = cum2.T[0:N_SSD_HEADS, :]
    c2_t = cum2_t - jnp.log2(dt.T[0:N_SSD_HEADS, :])
    cols = _spread(cum2, selt_ref)
    lane = lax.broadcasted_iota(jnp.int32, (L, LANES), 1)
    lo = lane < SSD_HEAD_DIM
    y_parts = []
    for g in range(N_SSD_GROUPS):
        bg = bm[:, D_STATE * g:D_STATE * (g + 1)].astype(BF16)
        cg = cm[:, D_STATE * g:D_STATE * (g + 1)].astype(BF16)
        cb = lax.dot_general(cg, bg, (((1,), (1,)), ((), ())), preferred_element_type=F32)
        for jj in range(N_SSD_HEADS // N_SSD_GROUPS // 2):
            j = (N_SSD_HEADS // N_SSD_GROUPS // 2) * g + jj
            h0, h1 = 2 * j, 2 * j + 1
            col0 = cols[:, LANES * h0:LANES * (h0 + 1)]
            col1 = cols[:, LANES * h1:LANES * (h1 + 1)]
            m0 = cb * jnp.exp2(col0 - c2_t[h0:h0 + 1, :] + mask_add)
            m1 = cb * jnp.exp2(col1 - c2_t[h1:h1 + 1, :] + mask_add)
            lhs = jnp.concatenate([m0, m1], axis=1).astype(BF16)
            xp = xs[:, LANES * j:LANES * (j + 1)]
            rhs = jnp.concatenate([jnp.where(lo, xp, 0.0), jnp.where(lo, 0.0, xp)], axis=0).astype(BF16)
            y_parts.append(jnp.dot(lhs, rhs, preferred_element_type=F32))
    return jnp.concatenate(y_parts, axis=1), cum2, cum2_t


def _spread(v, sel_ref):
    p0 = v.astype(BF16)
    p1 = (v - p0.astype(F32)).astype(BF16)
    return jnp.dot(jnp.concatenate([p0, p1], axis=1), sel_ref[...], preferred_element_type=F32)


def _ssd_gate_norm(ys, xs, z_act, dskip, g_ssd):
    ys = ys + dskip * xs
    gated = ys * z_act
    half = SSD_WIDTH // N_SSD_GROUPS
    outs = []
    for g in range(N_SSD_GROUPS):
        outs.append(_rms(gated[:, half * g:half * (g + 1)], g_ssd[:, half * g:half * (g + 1)]))
    return jnp.concatenate(outs, axis=1)


def _split_w_in_kernel(w_ref, main_ref, dt_ref):
    main_ref[...] = w_ref[:, 0:PROJ_MAIN].astype(BF16)
    dt_ref[...] = jnp.zeros_like(dt_ref)
    dt_ref[:, 0:N_SSD_HEADS] = w_ref[:, PROJ_MAIN:PROJ_MAIN + N_SSD_HEADS].astype(BF16)


def _split_w_in(w):
    rows, cols = w.shape
    blk = MXU_DIM
    return pl.pallas_call(
        _split_w_in_kernel,
        out_shape=(jax.ShapeDtypeStruct((rows, PROJ_MAIN), BF16), jax.ShapeDtypeStruct((rows, DT_PAD), BF16)),
        grid=(rows // blk,),
        in_specs=[pl.BlockSpec((blk, cols), lambda i: (i, 0))],
        out_specs=(pl.BlockSpec((blk, PROJ_MAIN), lambda i: (i, 0)), pl.BlockSpec((blk, DT_PAD), lambda i: (i, 0))),
        compiler_params=pltpu.CompilerParams(dimension_semantics=("parallel",)),
        name="split_w_in",
    )(w)


def _inproj_kernel(x_ref, g_ref, w_ref, wdt_ref, o_ref):
    hn = _rms(x_ref[...], g_ref[...]).astype(BF16)
    o_ref[:, 0:PROJ_MAIN] = jnp.dot(hn, w_ref[...], preferred_element_type=F32)
    o_ref[:, PROJ_MAIN:PROJ_PAD] = jnp.dot(hn, wdt_ref[...], preferred_element_type=F32)


def _in_proj(x2d, g_mix, w_main, w_dt):
    n = x2d.shape[0]
    return pl.pallas_call(
        _inproj_kernel,
        out_shape=jax.ShapeDtypeStruct((n, PROJ_PAD), F32),
        grid=(n // ROW_TILE,),
        in_specs=[
            pl.BlockSpec((ROW_TILE, D_MODEL), lambda i: (i, 0)),
            pl.BlockSpec((1, D_MODEL), lambda i: (0, 0)),
            pl.BlockSpec((D_MODEL, PROJ_MAIN), lambda i: (0, 0), pipeline_mode=pl.Buffered(1)),
            pl.BlockSpec((D_MODEL, DT_PAD), lambda i: (0, 0), pipeline_mode=pl.Buffered(1)),
        ],
        out_specs=pl.BlockSpec((ROW_TILE, PROJ_PAD), lambda i: (i, 0)),
        compiler_params=pltpu.CompilerParams(
            dimension_semantics=("parallel",), vmem_limit_bytes=VMEM_LIMIT),
        name="in_proj",
    )(x2d, g_mix, w_main, w_dt)


def _inproj_prompt_kernel(x_ref, g_ref, w_ref, wdt_ref, lcw_ref, lcb_ref, scw_ref, scb_ref, dtb_ref,
                          o_ref, olc_ref, osc_ref, ext_l, ext_s, *, steps_per_seq):
    t = lax.rem(pl.program_id(0), steps_per_seq)
    rows = ROW_TILE
    hist = SUBLANES
    o1, o2, o3 = LRU_WIDTH, 2 * LRU_WIDTH, 2 * LRU_WIDTH + SSD_WIDTH

    @pl.when(t == 0)
    def _init():
        ext_l[:, 0:hist, :] = jnp.zeros((ext_l.shape[0], hist, LANES), F32)
        ext_s[:, 0:hist, :] = jnp.zeros((ext_s.shape[0], hist, LANES), F32)

    hn = _rms(x_ref[...], g_ref[...]).astype(BF16)
    lx = jnp.dot(hn, w_ref[:, 0:o1], preferred_element_type=F32)
    for s in range(ext_l.shape[0]):
        ext_l[s, hist:hist + rows, :] = lx[:, LANES * s:LANES * (s + 1)]
    xbc_in = jnp.dot(hn, w_ref[:, o3:PROJ_MAIN], preferred_element_type=F32)
    for s in range(ext_s.shape[0]):
        ext_s[s, hist:hist + rows, :] = xbc_in[:, LANES * s:LANES * (s + 1)]
    o_ref[:, o1:o2] = _gelu_tanh(jnp.dot(hn, w_ref[:, o1:o2], preferred_element_type=F32))
    z = jnp.dot(hn, w_ref[:, o2:o3], preferred_element_type=F32)
    o_ref[:, o2:o3] = z * _sigmoid(z)
    o_ref[:, PROJ_MAIN:PROJ_PAD] = _softplus(
        jnp.dot(hn, wdt_ref[...], preferred_element_type=F32) + dtb_ref[...])
    o_ref[:, 0:o1] = _conv_slabs(ext_l, lcw_ref, lcb_ref, rows, hist - (CONV_WIDTH - 1))
    xbc = _conv_slabs(ext_s, scw_ref, scb_ref, rows, hist - (CONV_WIDTH - 1))
    o_ref[:, o3:PROJ_MAIN] = xbc * _sigmoid(xbc)

    @pl.when(t == steps_per_seq - 1)
    def _final():
        last = slice(hist + rows - (CONV_WIDTH - 1), hist + rows)
        for s in range(ext_l.shape[0]):
            olc_ref[:, LANES * s:LANES * (s + 1)] = ext_l[s, last, :]
        for s in range(ext_s.shape[0]):
            osc_ref[:, LANES * s:LANES * (s + 1)] = ext_s[s, last, :]

    tail_l = ext_l[:, rows:rows + hist, :]
    tail_s = ext_s[:, rows:rows + hist, :]
    ext_l[:, 0:hist, :] = tail_l
    ext_s[:, 0:hist, :] = tail_s


def _in_proj_prompt(x2d, bsz, g_mix, w_main, w_dt, lcw, lcb, scw, scb, dtb):
    n = x2d.shape[0]
    steps_per_seq = n // bsz // ROW_TILE
    const = lambda i: (0, 0)
    return pl.pallas_call(
        functools.partial(_inproj_prompt_kernel, steps_per_seq=steps_per_seq),
        out_shape=(
            jax.ShapeDtypeStruct((n, PROJ_PAD), F32),
            jax.ShapeDtypeStruct((bsz, CONV_WIDTH - 1, LRU_WIDTH), F32),
            jax.ShapeDtypeStruct((bsz, CONV_WIDTH - 1, SSD_CONV_DIM), F32),
        ),
        grid=(n // ROW_TILE,),
        in_specs=[
            pl.BlockSpec((ROW_TILE, D_MODEL), lambda i: (i, 0)),
            pl.BlockSpec((1, D_MODEL), const),
            pl.BlockSpec((D_MODEL, PROJ_MAIN), const, pipeline_mode=pl.Buffered(1)),
            pl.BlockSpec((D_MODEL, DT_PAD), const, pipeline_mode=pl.Buffered(1)),
            pl.BlockSpec((CONV_WIDTH, LRU_WIDTH), const),
            pl.BlockSpec((1, LRU_WIDTH), const),
            pl.BlockSpec((CONV_WIDTH, SSD_CONV_DIM), const),
            pl.BlockSpec((1, SSD_CONV_DIM), const),
            pl.BlockSpec((1, DT_PAD), const),
        ],
        out_specs=(
            pl.BlockSpec((ROW_TILE, PROJ_PAD), lambda i: (i, 0)),
            pl.BlockSpec((None, CONV_WIDTH - 1, LRU_WIDTH), lambda i: (i // steps_per_seq, 0, 0)),
            pl.BlockSpec((None, CONV_WIDTH - 1, SSD_CONV_DIM), lambda i: (i // steps_per_seq, 0, 0)),
        ),
        scratch_shapes=[
            pltpu.VMEM((LRU_WIDTH // LANES, SUBLANES + ROW_TILE, LANES), F32),
            pltpu.VMEM((SSD_CONV_DIM // LANES, SUBLANES + ROW_TILE, LANES), F32),
        ],
        compiler_params=pltpu.CompilerParams(
            dimension_semantics=("arbitrary",), vmem_limit_bytes=VMEM_LIMIT),
        name="in_proj_prompt",
    )(x2d, g_mix, w_main, w_dt, lcw, lcb, scw, scb, dtb)


def _outmlp_kernel(x_ref, y_ref, wo_ref, gm_ref, wu_ref, wd_ref, gf_ref, o_ref):
    x1 = x_ref[...] + jnp.dot(y_ref[...].astype(BF16), wo_ref[...], preferred_element_type=F32)
    m = _rms(x1, gm_ref[...]).astype(BF16)
    u = jnp.dot(m, wu_ref[...], preferred_element_type=F32)
    u = jnp.square(jnp.maximum(u, 0.0)).astype(BF16)
    x2 = x1 + jnp.dot(u, wd_ref[...], preferred_element_type=F32)
    o_ref[...] = _rms(x2, gf_ref[...])


def _out_mlp(x2d, ymix2d, w_out_b, g_mlp, w_up_b, w_down_b, g_final):
    n = x2d.shape[0]
    const = lambda i: (0, 0)
    return pl.pallas_call(
        _outmlp_kernel,
        out_shape=jax.ShapeDtypeStruct((n, D_MODEL), F32),
        grid=(n // ROW_TILE,),
        in_specs=[
            pl.BlockSpec((ROW_TILE, D_MODEL), lambda i: (i, 0)),
            pl.BlockSpec((ROW_TILE, MIX_WIDTH), lambda i: (i, 0)),
            pl.BlockSpec((MIX_WIDTH, D_MODEL), const, pipeline_mode=pl.Buffered(1)),
            pl.BlockSpec((1, D_MODEL), const),
            pl.BlockSpec((D_MODEL, D_FF), const, pipeline_mode=pl.Buffered(1)),
            pl.BlockSpec((D_FF, D_MODEL), const, pipeline_mode=pl.Buffered(1)),
            pl.BlockSpec((1, D_MODEL), const),
        ],
        out_specs=pl.BlockSpec((ROW_TILE, D_MODEL), lambda i: (i, 0)),
        compiler_params=pltpu.CompilerParams(
            dimension_semantics=("parallel",), vmem_limit_bytes=VMEM_LIMIT),
        name="out_mlp",
    )(x2d, ymix2d, w_out_b, g_mlp, w_up_b, w_down_b, g_final)


def _mixer_prompt_kernel(u_ref, gl_ref, zact_ref, xbc_ref, dt_ref,
                         wg_ref, ba_ref, bx_ref, lam_ref, glru_ref, alog_ref, dskip_ref, gssd_ref,
                         selt_ref, selp_ref,
                         y_ref, olh_ref, osh_ref,
                         a_pad, b_pad, h_pad, hcar, ht):
    t = pl.program_id(1)
    nt = pl.num_programs(1)
    tc = PROMPT_TC

    @pl.when(t == 0)
    def _init():
        hcar[...] = jnp.zeros_like(hcar)
        ht[...] = jnp.zeros_like(ht)

    neg_c_sp = (-LRU_C) * _softplus(-lam_ref[...])
    lane1 = lax.broadcasted_iota(jnp.int32, (1, LANES), 1)
    a2_row = jnp.where(lane1 < N_SSD_HEADS, -LOG2E * jnp.exp(alog_ref[...]), 0.0)
    L = SSD_CHUNK
    rr = lax.broadcasted_iota(jnp.int32, (L, L), 0)
    cc = lax.broadcasted_iota(jnp.int32, (L, L), 1)
    causal = cc <= rr
    tri = jnp.where(causal, 1.0, 0.0).astype(F32)
    mask_add = jnp.where(causal, 0.0, NEG_BIG).astype(F32)
    half = SSD_WIDTH // N_SSD_GROUPS

    for n in range(PROMPT_NB):
        a, b = _lru_coeffs(u_ref[n], wg_ref, ba_ref[...], bx_ref[...], neg_c_sp)
        hseq = _lru_scan_strided(a, b, hcar.at[n], a_pad.at[n], b_pad.at[n], h_pad.at[n])
        y_ref[n, :, 0:LRU_WIDTH] = _rms(hseq * gl_ref[n], glru_ref[...])

        for c in range(tc // L):
            rows = slice(L * c, L * (c + 1))
            xs = xbc_ref[n, rows, 0:SSD_WIDTH]
            bm = xbc_ref[n, rows, SSD_WIDTH:SSD_WIDTH + N_SSD_GROUPS * D_STATE]
            cm = xbc_ref[n, rows, SSD_WIDTH + N_SSD_GROUPS * D_STATE:SSD_CONV_DIM]
            dt = dt_ref[n, rows, :]
            y_diag, cum2, _ = _ssd_intra(xs, bm, cm, dt, a2_row, tri, mask_add, selt_ref)
            ecol = _spread(jnp.exp2(cum2), selp_ref)
            xw = xs * _spread(jnp.exp2(cum2[L - 1:L, :] - cum2) * dt, selp_ref)
            dec = ecol[L - 1:L, :]
            y_off_parts = []
            for g in range(N_SSD_GROUPS):
                htg = ht[n, g]
                cg = cm[:, D_STATE * g:D_STATE * (g + 1)].astype(BF16)
                y_off_parts.append(jnp.dot(cg, htg.astype(BF16), preferred_element_type=F32))
                bg_t = bm[:, D_STATE * g:D_STATE * (g + 1)].T.astype(BF16)
                st = jnp.dot(bg_t, xw[:, half * g:half * (g + 1)].astype(BF16), preferred_element_type=F32)
                ht[n, g] = htg * dec[:, half * g:half * (g + 1)] + st
            ys = y_diag + jnp.concatenate(y_off_parts, axis=1) * ecol
            y_ref[n, rows, LRU_WIDTH:MIX_WIDTH] = _ssd_gate_norm(
                ys, xs, zact_ref[n, rows, :], dskip_ref[...], gssd_ref[...])

    @pl.when(t == nt - 1)
    def _final():
        for n in range(PROMPT_NB):
            olh_ref[n] = hcar[n, 0:1, :]
            for g in range(N_SSD_GROUPS):
                osh_ref[n, half * g:half * (g + 1), :] = ht[n, g].T


def _param_specs(const):
    return [
        pl.BlockSpec((CONV_WIDTH, LRU_WIDTH), const),
        pl.BlockSpec((1, LRU_WIDTH), const),
        pl.BlockSpec((LRU_WIDTH // MXU_DIM, MXU_DIM, 2 * MXU_DIM), lambda *_: (0, 0, 0)),
        pl.BlockSpec((1, LRU_WIDTH), const),
        pl.BlockSpec((1, LRU_WIDTH), const),
        pl.BlockSpec((1, LRU_WIDTH), const),
        pl.BlockSpec((1, LRU_WIDTH), const),
        pl.BlockSpec((CONV_WIDTH, SSD_CONV_DIM), const),
        pl.BlockSpec((1, SSD_CONV_DIM), const),
        pl.BlockSpec((1, DT_PAD), const),
        pl.BlockSpec((1, DT_PAD), const),
        pl.BlockSpec((1, SSD_WIDTH), const),
        pl.BlockSpec((1, SSD_WIDTH), const),
    ]


def _head_selectors():
    k = jnp.arange(2 * LANES)[:, None] % LANES
    sel_t = (k == jnp.arange(N_SSD_HEADS * LANES)[None, :] // LANES).astype(BF16)
    sel_p = (k == jnp.arange(SSD_WIDTH)[None, :] // SSD_HEAD_DIM).astype(BF16)
    return sel_t, sel_p


def _mixer_prompt(act, wg, b_a, b_x, lam, g_lru, a_log, d_skip, g_ssd, sel_t, sel_p):
    bsz, seq, _ = act.shape
    tc = PROMPT_TC
    nb = PROMPT_NB
    const = lambda b, t: (0, 0)
    in_specs = [
        pl.BlockSpec((nb, tc, LRU_WIDTH), lambda b, t: (b, t, 0)),
        pl.BlockSpec((nb, tc, LRU_WIDTH), lambda b, t: (b, t, 1)),
        pl.BlockSpec((nb, tc, SSD_WIDTH), lambda b, t: (b, t, 2)),
        pl.BlockSpec((nb, tc, SSD_CONV_DIM), lambda b, t: (b, t, 2)),
        pl.BlockSpec((nb, tc, DT_PAD), lambda b, t: (b, t, PROJ_MAIN // DT_PAD)),
        pl.BlockSpec((LRU_WIDTH // MXU_DIM, MXU_DIM, 2 * MXU_DIM), lambda b, t: (0, 0, 0)),
        pl.BlockSpec((1, LRU_WIDTH), const),
        pl.BlockSpec((1, LRU_WIDTH), const),
        pl.BlockSpec((1, LRU_WIDTH), const),
        pl.BlockSpec((1, LRU_WIDTH), const),
        pl.BlockSpec((1, DT_PAD), const),
        pl.BlockSpec((1, SSD_WIDTH), const),
        pl.BlockSpec((1, SSD_WIDTH), const),
        pl.BlockSpec((2 * LANES, N_SSD_HEADS * LANES), const),
        pl.BlockSpec((2 * LANES, SSD_WIDTH), const),
    ]
    out_shape = (
        jax.ShapeDtypeStruct((bsz, seq, MIX_WIDTH), F32),
        jax.ShapeDtypeStruct((bsz, 1, LRU_WIDTH), F32),
        jax.ShapeDtypeStruct((bsz, SSD_WIDTH, D_STATE), F32),
    )
    out_specs = (
        pl.BlockSpec((nb, tc, MIX_WIDTH), lambda b, t: (b, t, 0)),
        pl.BlockSpec((nb, 1, LRU_WIDTH), lambda b, t: (b, 0, 0)),
        pl.BlockSpec((nb, SSD_WIDTH, D_STATE), lambda b, t: (b, 0, 0)),
    )
    scratch = [
        pltpu.VMEM((nb, LRU_WIDTH // LANES, SUBLANES * SCAN_PITCH, LANES), F32),
        pltpu.VMEM((nb, LRU_WIDTH // LANES, SUBLANES * SCAN_PITCH, LANES), F32),
        pltpu.VMEM((nb, LRU_WIDTH // LANES, SUBLANES * SCAN_PITCH, LANES), F32),
        pltpu.VMEM((nb, SUBLANES, LRU_WIDTH), F32),
        pltpu.VMEM((nb, N_SSD_GROUPS, D_STATE, SSD_WIDTH // N_SSD_GROUPS), F32),
    ]
    return pl.pallas_call(
        _mixer_prompt_kernel,
        out_shape=out_shape,
        grid=(bsz // nb, seq // tc),
        in_specs=in_specs,
        out_specs=out_specs,
        scratch_shapes=scratch,
        compiler_params=pltpu.CompilerParams(
            dimension_semantics=("parallel", "arbitrary"), vmem_limit_bytes=VMEM_LIMIT),
        name="mixer_prompt",
    )(act, act, act, act, act, wg, b_a, b_x, lam, g_lru, a_log, d_skip, g_ssd, sel_t, sel_p)


def _mixer_sample_kernel(lx_ref, gate_ref, z_ref, xbc_ref, dt_ref,
                         slc_ref, slh_ref, ssc_ref, ssh_ref,
                         lcw_ref, lcb_ref, wg_ref, ba_ref, bx_ref, lam_ref, glru_ref,
                         scw_ref, scb_ref, dtb_ref, alog_ref, dskip_ref, gssd_ref, selt_ref, selp_ref,
                         y_ref, olc_ref, olh_ref, osc_ref, osh_ref,
                         ext_l, ext_s, pad_scr, yoff_scr, *, T):
    S = SAMPLE_SEQS
    P = SUBLANES
    K1 = CONV_WIDTH - 1
    R = S * P
    row_i = lax.broadcasted_iota(jnp.int32, (R, 1), 0) & (P - 1)
    valid = row_i < T

    def pad_rows(ref):
        width = ref.shape[-1]
        pad_scr[:, :, 0:width] = jnp.zeros((S, P, width), F32)
        pad_scr[:, 0:T, 0:width] = ref[...].reshape(S, T, width)
        return pad_scr[:, :, 0:width].reshape(R, width)

    ext_l[...] = jnp.zeros_like(ext_l)
    ext_s[...] = jnp.zeros_like(ext_s)
    ext_l[:, 0:K1, :] = slc_ref[...]
    ext_l[:, K1:K1 + T, :] = lx_ref[...].reshape(S, T, LRU_WIDTH)
    ext_s[:, 0:K1, :] = ssc_ref[...]
    ext_s[:, K1:K1 + T, :] = xbc_ref[...].reshape(S, T, SSD_CONV_DIM)
    olc_ref[...] = ext_l[:, T:T + K1, :]
    osc_ref[...] = ext_s[:, T:T + K1, :]

    el = ext_l[...].reshape(R, LRU_WIDTH)
    es = ext_s[...].reshape(R, SSD_CONV_DIM)

    def conv(e, w_ref, b_ref):
        out = b_ref[...] + e * w_ref[0:1, :]
        for k in range(1, CONV_WIDTH):
            out = out + pltpu.roll(e, R - k, axis=0) * w_ref[k:k + 1, :]
        return out

    u = conv(el, lcw_ref, lcb_ref)
    neg_c_sp = (-LRU_C) * _softplus(-lam_ref[...])
    a, b = _lru_coeffs(u, wg_ref, ba_ref[...], bx_ref[...], neg_c_sp)
    a, b = _scan_within_8(a, b)
    h0 = jnp.broadcast_to(slh_ref[...], (S, P, LRU_WIDTH)).reshape(R, LRU_WIDTH)
    hseq = a * h0 + b
    olh_ref[...] = hseq.reshape(S, P, LRU_WIDTH)[:, T - 1:T, :]
    gate = pad_rows(gate_ref)
    y_lru = _rms(hseq * _gelu_tanh(gate), glru_ref[...])

    xbc = conv(es, scw_ref, scb_ref)
    xbc = xbc * _sigmoid(xbc)
    xs = xbc[:, 0:SSD_WIDTH]
    bm = xbc[:, SSD_WIDTH:SSD_WIDTH + N_SSD_GROUPS * D_STATE]
    cm = xbc[:, SSD_WIDTH + N_SSD_GROUPS * D_STATE:]
    dt_raw = pad_rows(dt_ref)
    dt = jnp.where(valid, _softplus(dt_raw + dtb_ref[...]), 0.0)
    lane1 = lax.broadcasted_iota(jnp.int32, (1, LANES), 1)
    a2_row = jnp.where(lane1 < N_SSD_HEADS, -LOG2E * jnp.exp(alog_ref[...]), 0.0)

    rr = lax.broadcasted_iota(jnp.int32, (R, R), 0)
    cc = lax.broadcasted_iota(jnp.int32, (R, R), 1)
    allowed = (cc <= rr) & ((rr - cc) <= (rr & (P - 1)))
    tri = jnp.where(allowed, 1.0, 0.0).astype(F32)
    mask_add = jnp.where(allowed, 0.0, NEG_BIG).astype(F32)

    y_diag, cum2, cum2_t = _ssd_intra(xs, bm, cm, dt, a2_row, tri, mask_add, selt_ref)
    ecol = _spread(jnp.exp2(cum2), selp_ref)
    end2 = jnp.broadcast_to(cum2.reshape(S, P, LANES)[:, P - 1:P, :], (S, P, LANES)).reshape(R, LANES)
    xw = xs * _spread(jnp.exp2(end2 - cum2) * dt, selp_ref)
    ecum_t = jnp.exp2(cum2_t)

    half = SSD_WIDTH // N_SSD_GROUPS
    for q in range(S):
        r0 = P * q
        vq = jnp.broadcast_to(ecum_t[:, r0 + P - 1:r0 + P], (N_SSD_HEADS, LANES))
        for g in range(N_SSD_GROUPS):
            hqg = ssh_ref[q, half * g:half * (g + 1), :]
            cq = cm[r0:r0 + P, D_STATE * g:D_STATE * (g + 1)].astype(BF16)
            yoff_scr[r0:r0 + P, half * g:half * (g + 1)] = lax.dot_general(
                cq, hqg.astype(BF16), (((1,), (1,)), ((), ())), preferred_element_type=F32)
            bq = bm[r0:r0 + P, D_STATE * g:D_STATE * (g + 1)].astype(BF16)
            xq = xw[r0:r0 + P, half * g:half * (g + 1)].astype(BF16)
            st = lax.dot_general(xq, bq, (((0,), (0,)), ((), ())), preferred_element_type=F32)
            for e in range(N_SSD_HEADS // N_SSD_GROUPS):
                h = (N_SSD_HEADS // N_SSD_GROUPS) * g + e
                lo_r = SSD_HEAD_DIM * e
                osh_ref[q, SSD_HEAD_DIM * h:SSD_HEAD_DIM * (h + 1), :] = (
                    vq[h:h + 1, :] * hqg[lo_r:lo_r + SSD_HEAD_DIM, :] + st[lo_r:lo_r + SSD_HEAD_DIM, :])

    ys = y_diag + yoff_scr[...] * ecol
    z = pad_rows(z_ref)
    y_ssd = _ssd_gate_norm(ys, xs, z * _sigmoid(z), dskip_ref[...], gssd_ref[...])
    y_ref[:, 0:LRU_WIDTH] = y_lru.reshape(S, P, LRU_WIDTH)[:, 0:T, :].reshape(S * T, LRU_WIDTH)
    y_ref[:, LRU_WIDTH:MIX_WIDTH] = y_ssd.reshape(S, P, SSD_WIDTH)[:, 0:T, :].reshape(S * T, SSD_WIDTH)


def _mixer_sample(proj, T, st_lc, st_lh, st_sc, st_sh, params, sel_t, sel_p):
    nseq = proj.shape[0] // T
    S = SAMPLE_SEQS
    const = lambda i: (0, 0)
    in_specs = [
        pl.BlockSpec((S * T, LRU_WIDTH), lambda i: (i, 0)),
        pl.BlockSpec((S * T, LRU_WIDTH), lambda i: (i, 1)),
        pl.BlockSpec((S * T, SSD_WIDTH), lambda i: (i, 2)),
        pl.BlockSpec((S * T, SSD_CONV_DIM), lambda i: (i, 2)),
        pl.BlockSpec((S * T, DT_PAD), lambda i: (i, PROJ_MAIN // DT_PAD)),
        pl.BlockSpec((S, CONV_WIDTH - 1, LRU_WIDTH), lambda i: (i, 0, 0)),
        pl.BlockSpec((S, 1, LRU_WIDTH), lambda i: (i, 0, 0)),
        pl.BlockSpec((S, CONV_WIDTH - 1, SSD_CONV_DIM), lambda i: (i, 0, 0)),
        pl.BlockSpec((S, SSD_WIDTH, D_STATE), lambda i: (i, 0, 0)),
    ] + _param_specs(const) + [
        pl.BlockSpec((2 * LANES, N_SSD_HEADS * LANES), const),
        pl.BlockSpec((2 * LANES, SSD_WIDTH), const),
    ]
    out_shape = (
        jax.ShapeDtypeStruct((nseq * T, MIX_WIDTH), F32),
        jax.ShapeDtypeStruct((nseq, CONV_WIDTH - 1, LRU_WIDTH), F32),
        jax.ShapeDtypeStruct((nseq, 1, LRU_WIDTH), F32),
        jax.ShapeDtypeStruct((nseq, CONV_WIDTH - 1, SSD_CONV_DIM), F32),
        jax.ShapeDtypeStruct((nseq, SSD_WIDTH, D_STATE), F32),
    )
    out_specs = (
        pl.BlockSpec((S * T, MIX_WIDTH), lambda i: (i, 0)),
        pl.BlockSpec((S, CONV_WIDTH - 1, LRU_WIDTH), lambda i: (i, 0, 0)),
        pl.BlockSpec((S, 1, LRU_WIDTH), lambda i: (i, 0, 0)),
        pl.BlockSpec((S, CONV_WIDTH - 1, SSD_CONV_DIM), lambda i: (i, 0, 0)),
        pl.BlockSpec((S, SSD_WIDTH, D_STATE), lambda i: (i, 0, 0)),
    )
    scratch = [
        pltpu.VMEM((S, SUBLANES, LRU_WIDTH), F32),
        pltpu.VMEM((S, SUBLANES, SSD_CONV_DIM), F32),
        pltpu.VMEM((S, SUBLANES, LRU_WIDTH), F32),
        pltpu.VMEM((S * SUBLANES, SSD_WIDTH), F32),
    ]
    return pl.pallas_call(
        functools.partial(_mixer_sample_kernel, T=T),
        out_shape=out_shape,
        grid=(nseq // S,),
        in_specs=in_specs,
        out_specs=out_specs,
        scratch_shapes=scratch,
        compiler_params=pltpu.CompilerParams(
            dimension_semantics=("parallel",), vmem_limit_bytes=VMEM_LIMIT),
        name="mixer_sample",
    )(proj, proj, proj, proj, proj, st_lc, st_lh, st_sc, st_sh, *params, sel_t, sel_p)


def _gate_weights(w_a, w_x):
    def tiles(w):
        per = MXU_DIM // LRU_BLOCK
        w4 = w.reshape(N_LRU_HEADS // per, per, LRU_BLOCK, LRU_BLOCK)
        eye = jnp.eye(per, dtype=w.dtype)
        t = jnp.einsum('jaik,ab->jaibk', w4, eye)
        return t.reshape(N_LRU_HEADS // per, MXU_DIM, MXU_DIM)
    return jnp.concatenate([tiles(w_a), tiles(w_x)], axis=2).astype(BF16)


def kernel(x_prompt, x_sample, state_lru_conv, state_lru_h, state_ssd_conv, state_ssd_h, g_mix, w_in,
           lru_conv_w, lru_conv_b, w_a, b_a, w_x, b_x, lam, g_lru_out, ssd_conv_w, ssd_conv_b, dt_bias,
           a_log, d_skip, g_ssd_out, w_out, g_mlp, w_up, w_down, g_final):
    depth = w_in.shape[0]
    assert depth == 1
    bp, seq, _ = x_prompt.shape
    bs, dseq, _ = x_sample.shape
    l = 0
    row = lambda v: v.reshape(1, -1)
    w_main, w_dt = _split_w_in(w_in[l])
    params = (
        lru_conv_w[l], row(lru_conv_b[l]), _gate_weights(w_a[l], w_x[l]),
        row(b_a[l]), row(b_x[l]), row(lam[l]), row(g_lru_out[l]),
        ssd_conv_w[l], row(ssd_conv_b[l]),
        jnp.pad(row(dt_bias[l]), ((0, 0), (0, DT_PAD - N_SSD_HEADS))),
        jnp.pad(row(a_log[l]), ((0, 0), (0, DT_PAD - N_SSD_HEADS))),
        row(jnp.repeat(d_skip[l], SSD_HEAD_DIM)), row(g_ssd_out[l]),
    )
    w_out_b = w_out[l].astype(BF16)
    w_up_b = w_up[l].astype(BF16)
    w_down_b = w_down[l].astype(BF16)
    gmix = row(g_mix[l])
    gmlp = row(g_mlp[l])
    gfin = row(g_final)

    xp2 = x_prompt.reshape(bp * seq, D_MODEL)
    (lcw, lcb, wg, ba, bx, lam_r, glru, scw, scb, dtb, alog, dskip, gssd) = params
    act_p, p_lc, p_sc = _in_proj_prompt(xp2, bp, gmix, w_main, w_dt, lcw, lcb, scw, scb, dtb)
    sel_t, sel_p = _head_selectors()
    ymix_p, p_lh, p_sh = _mixer_prompt(
        act_p.reshape(bp, seq, PROJ_PAD), wg, ba, bx, lam_r, glru, alog, dskip, gssd, sel_t, sel_p)
    y_prompt = _out_mlp(xp2, ymix_p.reshape(bp * seq, MIX_WIDTH), w_out_b, gmlp, w_up_b, w_down_b, gfin)

    xs2 = x_sample.reshape(bs * dseq, D_MODEL)
    proj_s = _in_proj(xs2, gmix, w_main, w_dt)
    ymix_s, s_lc, s_lh, s_sc, s_sh = _mixer_sample(
        proj_s, dseq, state_lru_conv[l], state_lru_h[l].reshape(bs, 1, LRU_WIDTH), state_ssd_conv[l],
        state_ssd_h[l].reshape(bs, SSD_WIDTH, D_STATE), params, sel_t, sel_p)
    y_sample = _out_mlp(xs2, ymix_s, w_out_b, gmlp, w_up_b, w_down_b, gfin)

    hshape = (N_SSD_HEADS, SSD_HEAD_DIM, D_STATE)
    return (
        y_prompt.reshape(bp, seq, D_MODEL), y_sample.reshape(bs, dseq, D_MODEL),
        p_lc[None], p_lh.reshape(1, bp, LRU_WIDTH), p_sc[None], p_sh.reshape(1, bp, *hshape),
        s_lc[None], s_lh.reshape(1, bs, LRU_WIDTH), s_sc[None], s_sh.reshape(1, bs, *hshape),
    )
```

```python
import functools
import math

import jax
import jax.numpy as jnp
from jax import lax
from jax.experimental import pallas as pl
from jax.experimental.pallas import tpu as pltpu

F32 = jnp.float32
BF16 = jnp.bfloat16

D_MODEL = 1024
LRU_WIDTH = 1024
N_LRU_HEADS = 16
LRU_BLOCK = 64
LRU_C = 8.0
SSD_WIDTH = 1024
SSD_HEAD_DIM = 64
N_SSD_HEADS = 16
N_SSD_GROUPS = 2
D_STATE = 128
CONV_WIDTH = 4
SSD_CONV_DIM = SSD_WIDTH + 2 * N_SSD_GROUPS * D_STATE
D_FF = 4 * D_MODEL
EPS = 1e-6

LANES = 128
SUBLANES = 8
MXU_DIM = 256
DT_PAD = LANES
PROJ_MAIN = 2 * LRU_WIDTH + SSD_WIDTH + SSD_CONV_DIM
PROJ_PAD = PROJ_MAIN + DT_PAD
MIX_WIDTH = LRU_WIDTH + SSD_WIDTH
SSD_CHUNK = 128
PROMPT_TC = 256
PROMPT_NB = 2
ROW_TILE = 512
SAMPLE_SEQS = SSD_CHUNK // SUBLANES
SCAN_RUN = PROMPT_TC // SUBLANES
SCAN_PITCH = SCAN_RUN + 4
NEG_BIG = -1e30
LOG2E = 1.4426950408889634
VMEM_LIMIT = 56 * 1024 * 1024
HI = lax.Precision.HIGHEST


def _rms(x, g):
    ms = jnp.mean(x * x, axis=-1, keepdims=True)
    return x * lax.rsqrt(ms + EPS) * g


def _sigmoid(x):
    return 1.0 / (1.0 + jnp.exp(-x))


def _softplus(x):
    return jnp.maximum(x, 0.0) + jnp.log1p(jnp.exp(-jnp.abs(x)))


def _gelu_tanh(x):
    c = math.sqrt(2.0 / math.pi)
    return 0.5 * x * (1.0 + jnp.tanh(c * (x + 0.044715 * (x * x * x))))


def _lru_coeffs(u, wg_ref, b_a, b_x, neg_c_sp):
    ub = u.astype(BF16)
    r_parts, i_parts = [], []
    for j in range(LRU_WIDTH // MXU_DIM):
        g = jnp.dot(ub[:, MXU_DIM * j:MXU_DIM * (j + 1)], wg_ref[j], preferred_element_type=F32)
        r_parts.append(g[:, :MXU_DIM])
        i_parts.append(g[:, MXU_DIM:])
    r = _sigmoid(jnp.concatenate(r_parts, axis=1) + b_a)
    i = _sigmoid(jnp.concatenate(i_parts, axis=1) + b_x)
    log_a = r * neg_c_sp
    a = jnp.exp(log_a)
    th = jnp.tanh(log_a)
    v = (th + th) / (th - 1.0)
    mult = jnp.where(v > 0.0, v * lax.rsqrt(v), 0.0)
    return a, mult * (i * u)


def _scan_within_8(a, b):
    ridx = lax.broadcasted_iota(jnp.int32, a.shape, 0) & (SUBLANES - 1)
    for k in (1, 2, 4):
        a_s = pltpu.roll(a, k, axis=0)
        b_s = pltpu.roll(b, k, axis=0)
        m = ridx >= k
        b = jnp.where(m, a * b_s + b, b)
        a = jnp.where(m, a * a_s, a)
    return a, b


def _conv_slabs(ext, w_ref, b_ref, rows, first):
    parts = []
    for s in range(ext.shape[0]):
        cols = slice(LANES * s, LANES * (s + 1))
        acc = b_ref[:, cols] + ext[s, pl.ds(first, rows), :] * w_ref[0:1, cols]
        for k in range(1, CONV_WIDTH):
            acc = acc + ext[s, pl.ds(first + k, rows), :] * w_ref[k:k + 1, cols]
        parts.append(acc)
    return jnp.concatenate(parts, axis=1)


def _lru_scan_strided(a, b, hcar, a_pad, b_pad, h_pad):
    rows = a.shape[0]
    S = rows // SUBLANES
    nslab = LRU_WIDTH // LANES
    ridx = lax.broadcasted_iota(jnp.int32, (SUBLANES, LANES), 0)
    step = lambda ref, s, i: ref[s, pl.ds(i, SUBLANES, stride=SCAN_PITCH), :]
    for s in range(nslab):
        cols = slice(LANES * s, LANES * (s + 1))
        for j in range(SUBLANES):
            a_pad[s, SCAN_PITCH * j:SCAN_PITCH * j + S, :] = a[S * j:S * (j + 1), cols]
            b_pad[s, SCAN_PITCH * j:SCAN_PITCH * j + S, :] = b[S * j:S * (j + 1), cols]
    h = [jnp.zeros((SUBLANES, LANES), F32)] * nslab
    prod = [jnp.ones((SUBLANES, LANES), F32)] * nslab
    for i in range(S):
        for s in range(nslab):
            av = step(a_pad, s, i)
            h[s] = av * h[s] + step(b_pad, s, i)
            prod[s] = av * prod[s]
    for s in range(nslab):
        cols = slice(LANES * s, LANES * (s + 1))
        pcum, hcum = _scan_within_8(prod[s], h[s])
        cin = hcar[:, cols]
        ends = hcum + pcum * cin
        h[s] = jnp.where(ridx == 0, cin, pltpu.roll(ends, 1, axis=0))
        hcar[:, cols] = jnp.broadcast_to(ends[SUBLANES - 1:SUBLANES, :], (SUBLANES, LANES))
    for i in range(S):
        for s in range(nslab):
            h[s] = step(a_pad, s, i) * h[s] + step(b_pad, s, i)
            h_pad[s, pl.ds(i, SUBLANES, stride=SCAN_PITCH), :] = h[s]
    return jnp.concatenate(
        [jnp.concatenate([h_pad[s, SCAN_PITCH * j:SCAN_PITCH * j + S, :] for j in range(SUBLANES)], axis=0)
         for s in range(nslab)], axis=1)


def _ssd_cumdecay(dt, a2_row, tri):
    cum2 = jnp.dot(tri, dt * a2_row, precision=HI, preferred_element_type=F32)
    return cum2, cum2.T[0:N_SSD_HEADS, :]


def _ssd_diag(xs, bm, cm, dt, cols, cum2_t, mask_add):
    L = xs.shape[0]
    c2_t = cum2_t - jnp.log2(dt.T[0:N_SSD_HEADS, :])
    lane = lax.broadcasted_iota(jnp.int32, (L, LANES), 1)
    lo = lane < SSD_HEAD_DIM
    y_parts = []
    for g in range(N_SSD_GROUPS):
        bg = bm[:, D_STATE * g:D_STATE * (g + 1)].astype(BF16)
        cg = cm[:, D_STATE * g:D_STATE * (g + 1)].astype(BF16)
        cb = lax.dot_general(cg, bg, (((1,), (1,)), ((), ())), preferred_element_type=F32)
        for jj in range(N_SSD_HEADS // N_SSD_GROUPS // 2):
            j = (N_SSD_HEADS // N_SSD_GROUPS // 2) * g + jj
            h0, h1 = 2 * j, 2 * j + 1
            col0 = cols[:, LANES * h0:LANES * (h0 + 1)]
            col1 = cols[:, LANES * h1:LANES * (h1 + 1)]
            m0 = cb * jnp.exp2(col0 - c2_t[h0:h0 + 1, :] + mask_add)
            m1 = cb * jnp.exp2(col1 - c2_t[h1:h1 + 1, :] + mask_add)
            lhs = jnp.concatenate([m0, m1], axis=1).astype(BF16)
            xp = xs[:, LANES * j:LANES * (j + 1)]
            rhs = jnp.concatenate([jnp.where(lo, xp, 0.0), jnp.where(lo, 0.0, xp)], axis=0).astype(BF16)
            y_parts.append(jnp.dot(lhs, rhs, preferred_element_type=F32))
    return jnp.concatenate(y_parts, axis=1)


def _spread(v, sel_ref):
    p0 = v.astype(BF16)
    p1 = (v - p0.astype(F32)).astype(BF16)
    return jnp.dot(jnp.concatenate([p0, p1], axis=1), sel_ref[...], preferred_element_type=F32)


def _ssd_gate_norm(ys, xs, z_act, dskip, g_ssd):
    ys = ys + dskip * xs
    gated = ys * z_act
    half = SSD_WIDTH // N_SSD_GROUPS
    outs = []
    for g in range(N_SSD_GROUPS):
        outs.append(_rms(gated[:, half * g:half * (g + 1)], g_ssd[:, half * g:half * (g + 1)]))
    return jnp.concatenate(outs, axis=1)


def _split_w_in_kernel(w_ref, main_ref, dt_ref):
    main_ref[...] = w_ref[:, 0:PROJ_MAIN].astype(BF16)
    dt_ref[...] = jnp.zeros_like(dt_ref)
    dt_ref[:, 0:N_SSD_HEADS] = w_ref[:, PROJ_MAIN:PROJ_MAIN + N_SSD_HEADS].astype(BF16)


def _split_w_in(w):
    rows, cols = w.shape
    blk = MXU_DIM
    return pl.pallas_call(
        _split_w_in_kernel,
        out_shape=(jax.ShapeDtypeStruct((rows, PROJ_MAIN), BF16), jax.ShapeDtypeStruct((rows, DT_PAD), BF16)),
        grid=(rows // blk,),
        in_specs=[pl.BlockSpec((blk, cols), lambda i: (i, 0))],
        out_specs=(pl.BlockSpec((blk, PROJ_MAIN), lambda i: (i, 0)), pl.BlockSpec((blk, DT_PAD), lambda i: (i, 0))),
        compiler_params=pltpu.CompilerParams(dimension_semantics=("parallel",)),
        name="split_w_in",
    )(w)


def _inproj_kernel(x_ref, g_ref, w_ref, wdt_ref, o_ref):
    hn = _rms(x_ref[...], g_ref[...]).astype(BF16)
    o_ref[:, 0:PROJ_MAIN] = jnp.dot(hn, w_ref[...], preferred_element_type=F32)
    o_ref[:, PROJ_MAIN:PROJ_PAD] = jnp.dot(hn, wdt_ref[...], preferred_element_type=F32)


def _in_proj(x2d, g_mix, w_main, w_dt):
    n = x2d.shape[0]
    return pl.pallas_call(
        _inproj_kernel,
        out_shape=jax.ShapeDtypeStruct((n, PROJ_PAD), F32),
        grid=(n // ROW_TILE,),
        in_specs=[
            pl.BlockSpec((ROW_TILE, D_MODEL), lambda i: (i, 0)),
            pl.BlockSpec((1, D_MODEL), lambda i: (0, 0)),
            pl.BlockSpec((D_MODEL, PROJ_MAIN), lambda i: (0, 0), pipeline_mode=pl.Buffered(1)),
            pl.BlockSpec((D_MODEL, DT_PAD), lambda i: (0, 0), pipeline_mode=pl.Buffered(1)),
        ],
        out_specs=pl.BlockSpec((ROW_TILE, PROJ_PAD), lambda i: (i, 0)),
        compiler_params=pltpu.CompilerParams(
            dimension_semantics=("parallel",), vmem_limit_bytes=VMEM_LIMIT),
        name="in_proj",
    )(x2d, g_mix, w_main, w_dt)


def _inproj_prompt_kernel(x_ref, g_ref, w_ref, wdt_ref, lcw_ref, lcb_ref, scw_ref, scb_ref, dtb_ref,
                          o_ref, olc_ref, osc_ref, ext_l, ext_s, *, steps_per_seq):
    t = lax.rem(pl.program_id(0), steps_per_seq)
    rows = ROW_TILE
    hist = SUBLANES
    o1, o2, o3 = LRU_WIDTH, 2 * LRU_WIDTH, 2 * LRU_WIDTH + SSD_WIDTH

    @pl.when(t == 0)
    def _init():
        ext_l[:, 0:hist, :] = jnp.zeros((ext_l.shape[0], hist, LANES), F32)
        ext_s[:, 0:hist, :] = jnp.zeros((ext_s.shape[0], hist, LANES), F32)

    hn = _rms(x_ref[...], g_ref[...]).astype(BF16)
    lx = jnp.dot(hn, w_ref[:, 0:o1], preferred_element_type=F32)
    for s in range(ext_l.shape[0]):
        ext_l[s, hist:hist + rows, :] = lx[:, LANES * s:LANES * (s + 1)]
    xbc_in = jnp.dot(hn, w_ref[:, o3:PROJ_MAIN], preferred_element_type=F32)
    for s in range(ext_s.shape[0]):
        ext_s[s, hist:hist + rows, :] = xbc_in[:, LANES * s:LANES * (s + 1)]
    o_ref[:, o1:o2] = _gelu_tanh(jnp.dot(hn, w_ref[:, o1:o2], preferred_element_type=F32))
    z = jnp.dot(hn, w_ref[:, o2:o3], preferred_element_type=F32)
    o_ref[:, o2:o3] = z * _sigmoid(z)
    o_ref[:, PROJ_MAIN:PROJ_PAD] = _softplus(
        jnp.dot(hn, wdt_ref[...], preferred_element_type=F32) + dtb_ref[...])
    o_ref[:, 0:o1] = _conv_slabs(ext_l, lcw_ref, lcb_ref, rows, hist - (CONV_WIDTH - 1))
    xbc = _conv_slabs(ext_s, scw_ref, scb_ref, rows, hist - (CONV_WIDTH - 1))
    o_ref[:, o3:PROJ_MAIN] = xbc * _sigmoid(xbc)

    @pl.when(t == steps_per_seq - 1)
    def _final():
        last = slice(hist + rows - (CONV_WIDTH - 1), hist + rows)
        for s in range(ext_l.shape[0]):
            olc_ref[:, LANES * s:LANES * (s + 1)] = ext_l[s, last, :]
        for s in range(ext_s.shape[0]):
            osc_ref[:, LANES * s:LANES * (s + 1)] = ext_s[s, last, :]

    tail_l = ext_l[:, rows:rows + hist, :]
    tail_s = ext_s[:, rows:rows + hist, :]
    ext_l[:, 0:hist, :] = tail_l
    ext_s[:, 0:hist, :] = tail_s


def _in_proj_prompt(x2d, bsz, g_mix, w_main, w_dt, lcw, lcb, scw, scb, dtb):
    n = x2d.shape[0]
    steps_per_seq = n // bsz // ROW_TILE
    const = lambda i: (0, 0)
    return pl.pallas_call(
        functools.partial(_inproj_prompt_kernel, steps_per_seq=steps_per_seq),
        out_shape=(
            jax.ShapeDtypeStruct((n, PROJ_PAD), F32),
            jax.ShapeDtypeStruct((bsz, CONV_WIDTH - 1, LRU_WIDTH), F32),
            jax.ShapeDtypeStruct((bsz, CONV_WIDTH - 1, SSD_CONV_DIM), F32),
        ),
        grid=(n // ROW_TILE,),
        in_specs=[
            pl.BlockSpec((ROW_TILE, D_MODEL), lambda i: (i, 0)),
            pl.BlockSpec((1, D_MODEL), const),
            pl.BlockSpec((D_MODEL, PROJ_MAIN), const, pipeline_mode=pl.Buffered(1)),
            pl.BlockSpec((D_MODEL, DT_PAD), const, pipeline_mode=pl.Buffered(1)),
            pl.BlockSpec((CONV_WIDTH, LRU_WIDTH), const),
            pl.BlockSpec((1, LRU_WIDTH), const),
            pl.BlockSpec((CONV_WIDTH, SSD_CONV_DIM), const),
            pl.BlockSpec((1, SSD_CONV_DIM), const),
            pl.BlockSpec((1, DT_PAD), const),
        ],
        out_specs=(
            pl.BlockSpec((ROW_TILE, PROJ_PAD), lambda i: (i, 0)),
            pl.BlockSpec((None, CONV_WIDTH - 1, LRU_WIDTH), lambda i: (i // steps_per_seq, 0, 0)),
            pl.BlockSpec((None, CONV_WIDTH - 1, SSD_CONV_DIM), lambda i: (i // steps_per_seq, 0, 0)),
        ),
        scratch_shapes=[
            pltpu.VMEM((LRU_WIDTH // LANES, SUBLANES + ROW_TILE, LANES), F32),
            pltpu.VMEM((SSD_CONV_DIM // LANES, SUBLANES + ROW_TILE, LANES), F32),
        ],
        compiler_params=pltpu.CompilerParams(
            dimension_semantics=("arbitrary",), vmem_limit_bytes=VMEM_LIMIT),
        name="in_proj_prompt",
    )(x2d, g_mix, w_main, w_dt, lcw, lcb, scw, scb, dtb)


def _outmlp_kernel(x_ref, y_ref, wo_ref, gm_ref, wu_ref, wd_ref, gf_ref, o_ref):
    x1 = x_ref[...] + jnp.dot(y_ref[...].astype(BF16), wo_ref[...], preferred_element_type=F32)
    m = _rms(x1, gm_ref[...]).astype(BF16)
    u = jnp.dot(m, wu_ref[...], preferred_element_type=F32)
    u = jnp.square(jnp.maximum(u, 0.0)).astype(BF16)
    x2 = x1 + jnp.dot(u, wd_ref[...], preferred_element_type=F32)
    o_ref[...] = _rms(x2, gf_ref[...])


def _out_mlp(x2d, ymix2d, w_out_b, g_mlp, w_up_b, w_down_b, g_final):
    n = x2d.shape[0]
    const = lambda i: (0, 0)
    return pl.pallas_call(
        _outmlp_kernel,
        out_shape=jax.ShapeDtypeStruct((n, D_MODEL), F32),
        grid=(n // ROW_TILE,),
        in_specs=[
            pl.BlockSpec((ROW_TILE, D_MODEL), lambda i: (i, 0)),
            pl.BlockSpec((ROW_TILE, MIX_WIDTH), lambda i: (i, 0)),
            pl.BlockSpec((MIX_WIDTH, D_MODEL), const, pipeline_mode=pl.Buffered(1)),
            pl.BlockSpec((1, D_MODEL), const),
            pl.BlockSpec((D_MODEL, D_FF), const, pipeline_mode=pl.Buffered(1)),
            pl.BlockSpec((D_FF, D_MODEL), const, pipeline_mode=pl.Buffered(1)),
            pl.BlockSpec((1, D_MODEL), const),
        ],
        out_specs=pl.BlockSpec((ROW_TILE, D_MODEL), lambda i: (i, 0)),
        compiler_params=pltpu.CompilerParams(
            dimension_semantics=("parallel",), vmem_limit_bytes=VMEM_LIMIT),
        name="out_mlp",
    )(x2d, ymix2d, w_out_b, g_mlp, w_up_b, w_down_b, g_final)


def _mixer_prompt_kernel(u_ref, gl_ref, zact_ref, xbc_ref, dt_ref,
                         wg_ref, ba_ref, bx_ref, lam_ref, glru_ref, alog_ref, dskip_ref, gssd_ref,
                         selt_ref, selp_ref,
                         y_ref, olh_ref, osh_ref,
                         a_pad, b_pad, h_pad, hcar, ht):
    t = pl.program_id(1)
    nt = pl.num_programs(1)
    tc = PROMPT_TC

    @pl.when(t == 0)
    def _init():
        hcar[...] = jnp.zeros_like(hcar)
        ht[...] = jnp.zeros_like(ht)

    neg_c_sp = (-LRU_C) * _softplus(-lam_ref[...])
    lane1 = lax.broadcasted_iota(jnp.int32, (1, LANES), 1)
    a2_row = jnp.where(lane1 < N_SSD_HEADS, -LOG2E * jnp.exp(alog_ref[...]), 0.0)
    L = SSD_CHUNK
    rr = lax.broadcasted_iota(jnp.int32, (L, L), 0)
    cc = lax.broadcasted_iota(jnp.int32, (L, L), 1)
    causal = cc <= rr
    tri = jnp.where(causal, 1.0, 0.0).astype(F32)
    mask_add = jnp.where(causal, 0.0, NEG_BIG).astype(F32)
    half = SSD_WIDTH // N_SSD_GROUPS

    chunks = [(n, c) for n in range(PROMPT_NB) for c in range(tc // L)]
    cums = [_ssd_cumdecay(dt_ref[n, L * c:L * (c + 1), :], a2_row, tri) for n, c in chunks]
    dts = [dt_ref[n, L * c:L * (c + 1), :] for n, c in chunks]
    cum_all = jnp.concatenate([cum2 for cum2, _ in cums], axis=0)
    cols_all = _spread(cum_all, selt_ref)
    ecol_all = _spread(jnp.exp2(cum_all), selp_ref)
    sdt_all = _spread(jnp.concatenate(
        [jnp.exp2(cum2[L - 1:L, :] - cum2) * dt for (cum2, _), dt in zip(cums, dts)], axis=0), selp_ref)

    for n in range(PROMPT_NB):
        a, b = _lru_coeffs(u_ref[n], wg_ref, ba_ref[...], bx_ref[...], neg_c_sp)
        hseq = _lru_scan_strided(a, b, hcar.at[n], a_pad.at[n], b_pad.at[n], h_pad.at[n])
        y_ref[n, :, 0:LRU_WIDTH] = _rms(hseq * gl_ref[n], glru_ref[...])

        for c in range(tc // L):
            k = chunks.index((n, c))
            rows = slice(L * c, L * (c + 1))
            krows = slice(L * k, L * (k + 1))
            xs = xbc_ref[n, rows, 0:SSD_WIDTH]
            bm = xbc_ref[n, rows, SSD_WIDTH:SSD_WIDTH + N_SSD_GROUPS * D_STATE]
            cm = xbc_ref[n, rows, SSD_WIDTH + N_SSD_GROUPS * D_STATE:SSD_CONV_DIM]
            y_diag = _ssd_diag(xs, bm, cm, dts[k], cols_all[krows, :], cums[k][1], mask_add)
            ecol = ecol_all[krows, :]
            xw = xs * sdt_all[krows, :]
            dec = ecol[L - 1:L, :]
            y_off_parts = []
            for g in range(N_SSD_GROUPS):
                htg = ht[n, g]
                cg = cm[:, D_STATE * g:D_STATE * (g + 1)].astype(BF16)
                y_off_parts.append(jnp.dot(cg, htg.astype(BF16), preferred_element_type=F32))
                bg_t = bm[:, D_STATE * g:D_STATE * (g + 1)].T.astype(BF16)
                st = jnp.dot(bg_t, xw[:, half * g:half * (g + 1)].astype(BF16), preferred_element_type=F32)
                ht[n, g] = htg * dec[:, half * g:half * (g + 1)] + st
            ys = y_diag + jnp.concatenate(y_off_parts, axis=1) * ecol
            y_ref[n, rows, LRU_WIDTH:MIX_WIDTH] = _ssd_gate_norm(
                ys, xs, zact_ref[n, rows, :], dskip_ref[...], gssd_ref[...])

    @pl.when(t == nt - 1)
    def _final():
        for n in range(PROMPT_NB):
            olh_ref[n] = hcar[n, 0:1, :]
            for g in range(N_SSD_GROUPS):
                osh_ref[n, half * g:half * (g + 1), :] = ht[n, g].T


def _param_specs(const):
    return [
        pl.BlockSpec((CONV_WIDTH, LRU_WIDTH), const),
        pl.BlockSpec((1, LRU_WIDTH), const),
        pl.BlockSpec((LRU_WIDTH // MXU_DIM, MXU_DIM, 2 * MXU_DIM), lambda *_: (0, 0, 0)),
        pl.BlockSpec((1, LRU_WIDTH), const),
        pl.BlockSpec((1, LRU_WIDTH), const),
        pl.BlockSpec((1, LRU_WIDTH), const),
        pl.BlockSpec((1, LRU_WIDTH), const),
        pl.BlockSpec((CONV_WIDTH, SSD_CONV_DIM), const),
        pl.BlockSpec((1, SSD_CONV_DIM), const),
        pl.BlockSpec((1, DT_PAD), const),
        pl.BlockSpec((1, DT_PAD), const),
        pl.BlockSpec((1, SSD_WIDTH), const),
        pl.BlockSpec((1, SSD_WIDTH), const),
    ]


def _head_selectors():
    k = jnp.arange(2 * LANES)[:, None] % LANES
    sel_t = (k == jnp.arange(N_SSD_HEADS * LANES)[None, :] // LANES).astype(BF16)
    sel_p = (k == jnp.arange(SSD_WIDTH)[None, :] // SSD_HEAD_DIM).astype(BF16)
    return sel_t, sel_p


def _mixer_prompt(act, wg, b_a, b_x, lam, g_lru, a_log, d_skip, g_ssd, sel_t, sel_p):
    bsz, seq, _ = act.shape
    tc = PROMPT_TC
    nb = PROMPT_NB
    const = lambda b, t: (0, 0)
    in_specs = [
        pl.BlockSpec((nb, tc, LRU_WIDTH), lambda b, t: (b, t, 0)),
        pl.BlockSpec((nb, tc, LRU_WIDTH), lambda b, t: (b, t, 1)),
        pl.BlockSpec((nb, tc, SSD_WIDTH), lambda b, t: (b, t, 2)),
        pl.BlockSpec((nb, tc, SSD_CONV_DIM), lambda b, t: (b, t, 2)),
        pl.BlockSpec((nb, tc, DT_PAD), lambda b, t: (b, t, PROJ_MAIN // DT_PAD)),
        pl.BlockSpec((LRU_WIDTH // MXU_DIM, MXU_DIM, 2 * MXU_DIM), lambda b, t: (0, 0, 0)),
        pl.BlockSpec((1, LRU_WIDTH), const),
        pl.BlockSpec((1, LRU_WIDTH), const),
        pl.BlockSpec((1, LRU_WIDTH), const),
        pl.BlockSpec((1, LRU_WIDTH), const),
        pl.BlockSpec((1, DT_PAD), const),
        pl.BlockSpec((1, SSD_WIDTH), const),
        pl.BlockSpec((1, SSD_WIDTH), const),
        pl.BlockSpec((2 * LANES, N_SSD_HEADS * LANES), const),
        pl.BlockSpec((2 * LANES, SSD_WIDTH), const),
    ]
    out_shape = (
        jax.ShapeDtypeStruct((bsz, seq, MIX_WIDTH), F32),
        jax.ShapeDtypeStruct((bsz, 1, LRU_WIDTH), F32),
        jax.ShapeDtypeStruct((bsz, SSD_WIDTH, D_STATE), F32),
    )
    out_specs = (
        pl.BlockSpec((nb, tc, MIX_WIDTH), lambda b, t: (b, t, 0)),
        pl.BlockSpec((nb, 1, LRU_WIDTH), lambda b, t: (b, 0, 0)),
        pl.BlockSpec((nb, SSD_WIDTH, D_STATE), lambda b, t: (b, 0, 0)),
    )
    scratch = [
        pltpu.VMEM((nb, LRU_WIDTH // LANES, SUBLANES * SCAN_PITCH, LANES), F32),
        pltpu.VMEM((nb, LRU_WIDTH // LANES, SUBLANES * SCAN_PITCH, LANES), F32),
        pltpu.VMEM((nb, LRU_WIDTH // LANES, SUBLANES * SCAN_PITCH, LANES), F32),
        pltpu.VMEM((nb, SUBLANES, LRU_WIDTH), F32),
        pltpu.VMEM((nb, N_SSD_GROUPS, D_STATE, SSD_WIDTH // N_SSD_GROUPS), F32),
    ]
    return pl.pallas_call(
        _mixer_prompt_kernel,
        out_shape=out_shape,
        grid=(bsz // nb, seq // tc),
        in_specs=in_specs,
        out_specs=out_specs,
        scratch_shapes=scratch,
        compiler_params=pltpu.CompilerParams(
            dimension_semantics=("parallel", "arbitrary"), vmem_limit_bytes=VMEM_LIMIT),
        name="mixer_prompt",
    )(act, act, act, act, act, wg, b_a, b_x, lam, g_lru, a_log, d_skip, g_ssd, sel_t, sel_p)


def _mixer_sample_kernel(lx_ref, gate_ref, z_ref, xbc_ref, dt_ref,
                         slc_ref, slh_ref, ssc_ref, ssh_ref,
                         lcw_ref, lcb_ref, wg_ref, ba_ref, bx_ref, lam_ref, glru_ref,
                         scw_ref, scb_ref, dtb_ref, alog_ref, dskip_ref, gssd_ref, selt_ref, selp_ref,
                         y_ref, olc_ref, olh_ref, osc_ref, osh_ref,
                         ext_l, ext_s, pad_scr, yoff_scr, *, T):
    S = SAMPLE_SEQS
    P = SUBLANES
    K1 = CONV_WIDTH - 1
    R = S * P
    row_i = lax.broadcasted_iota(jnp.int32, (R, 1), 0) & (P - 1)
    valid = row_i < T

    def pad_rows(ref):
        width = ref.shape[-1]
        pad_scr[:, :, 0:width] = jnp.zeros((S, P, width), F32)
        pad_scr[:, 0:T, 0:width] = ref[...].reshape(S, T, width)
        return pad_scr[:, :, 0:width].reshape(R, width)

    ext_l[...] = jnp.zeros_like(ext_l)
    ext_s[...] = jnp.zeros_like(ext_s)
    ext_l[:, 0:K1, :] = slc_ref[...]
    ext_l[:, K1:K1 + T, :] = lx_ref[...].reshape(S, T, LRU_WIDTH)
    ext_s[:, 0:K1, :] = ssc_ref[...]
    ext_s[:, K1:K1 + T, :] = xbc_ref[...].reshape(S, T, SSD_CONV_DIM)
    olc_ref[...] = ext_l[:, T:T + K1, :]
    osc_ref[...] = ext_s[:, T:T + K1, :]

    el = ext_l[...].reshape(R, LRU_WIDTH)
    es = ext_s[...].reshape(R, SSD_CONV_DIM)

    def conv(e, w_ref, b_ref):
        out = b_ref[...] + e * w_ref[0:1, :]
        for k in range(1, CONV_WIDTH):
            out = out + pltpu.roll(e, R - k, axis=0) * w_ref[k:k + 1, :]
        return out

    u = conv(el, lcw_ref, lcb_ref)
    neg_c_sp = (-LRU_C) * _softplus(-lam_ref[...])
    a, b = _lru_coeffs(u, wg_ref, ba_ref[...], bx_ref[...], neg_c_sp)
    a, b = _scan_within_8(a, b)
    h0 = jnp.broadcast_to(slh_ref[...], (S, P, LRU_WIDTH)).reshape(R, LRU_WIDTH)
    hseq = a * h0 + b
    olh_ref[...] = hseq.reshape(S, P, LRU_WIDTH)[:, T - 1:T, :]
    gate = pad_rows(gate_ref)
    y_lru = _rms(hseq * _gelu_tanh(gate), glru_ref[...])

    xbc = conv(es, scw_ref, scb_ref)
    xbc = xbc * _sigmoid(xbc)
    xs = xbc[:, 0:SSD_WIDTH]
    bm = xbc[:, SSD_WIDTH:SSD_WIDTH + N_SSD_GROUPS * D_STATE]
    cm = xbc[:, SSD_WIDTH + N_SSD_GROUPS * D_STATE:]
    dt_raw = pad_rows(dt_ref)
    dt = jnp.where(valid, _softplus(dt_raw + dtb_ref[...]), 0.0)
    lane1 = lax.broadcasted_iota(jnp.int32, (1, LANES), 1)
    a2_row = jnp.where(lane1 < N_SSD_HEADS, -LOG2E * jnp.exp(alog_ref[...]), 0.0)

    rr = lax.broadcasted_iota(jnp.int32, (R, R), 0)
    cc = lax.broadcasted_iota(jnp.int32, (R, R), 1)
    allowed = (cc <= rr) & ((rr - cc) <= (rr & (P - 1)))
    tri = jnp.where(allowed, 1.0, 0.0).astype(F32)
    mask_add = jnp.where(allowed, 0.0, NEG_BIG).astype(F32)

    cum2, cum2_t = _ssd_cumdecay(dt, a2_row, tri)
    y_diag = _ssd_diag(xs, bm, cm, dt, _spread(cum2, selt_ref), cum2_t, mask_add)
    ecol = _spread(jnp.exp2(cum2), selp_ref)
    end2 = jnp.broadcast_to(cum2.reshape(S, P, LANES)[:, P - 1:P, :], (S, P, LANES)).reshape(R, LANES)
    xw = xs * _spread(jnp.exp2(end2 - cum2) * dt, selp_ref)
    ecum_t = jnp.exp2(cum2_t)

    half = SSD_WIDTH // N_SSD_GROUPS
    for q in range(S):
        r0 = P * q
        vq = jnp.broadcast_to(ecum_t[:, r0 + P - 1:r0 + P], (N_SSD_HEADS, LANES))
        for g in range(N_SSD_GROUPS):
            hqg = ssh_ref[q, half * g:half * (g + 1), :]
            cq = cm[r0:r0 + P, D_STATE * g:D_STATE * (g + 1)].astype(BF16)
            yoff_scr[r0:r0 + P, half * g:half * (g + 1)] = lax.dot_general(
                cq, hqg.astype(BF16), (((1,), (1,)), ((), ())), preferred_element_type=F32)
            bq = bm[r0:r0 + P, D_STATE * g:D_STATE * (g + 1)].astype(BF16)
            xq = xw[r0:r0 + P, half * g:half * (g + 1)].astype(BF16)
            st = lax.dot_general(xq, bq, (((0,), (0,)), ((), ())), preferred_element_type=F32)
            for e in range(N_SSD_HEADS // N_SSD_GROUPS):
                h = (N_SSD_HEADS // N_SSD_GROUPS) * g + e
                lo_r = SSD_HEAD_DIM * e
                osh_ref[q, SSD_HEAD_DIM * h:SSD_HEAD_DIM * (h + 1), :] = (
                    vq[h:h + 1, :] * hqg[lo_r:lo_r + SSD_HEAD_DIM, :] + st[lo_r:lo_r + SSD_HEAD_DIM, :])

    ys = y_diag + yoff_scr[...] * ecol
    z = pad_rows(z_ref)
    y_ssd = _ssd_gate_norm(ys, xs, z * _sigmoid(z), dskip_ref[...], gssd_ref[...])
    y_ref[:, 0:LRU_WIDTH] = y_lru.reshape(S, P, LRU_WIDTH)[:, 0:T, :].reshape(S * T, LRU_WIDTH)
    y_ref[:, LRU_WIDTH:MIX_WIDTH] = y_ssd.reshape(S, P, SSD_WIDTH)[:, 0:T, :].reshape(S * T, SSD_WIDTH)


def _mixer_sample(proj, T, st_lc, st_lh, st_sc, st_sh, params, sel_t, sel_p):
    nseq = proj.shape[0] // T
    S = SAMPLE_SEQS
    const = lambda i: (0, 0)
    in_specs = [
        pl.BlockSpec((S * T, LRU_WIDTH), lambda i: (i, 0)),
        pl.BlockSpec((S * T, LRU_WIDTH), lambda i: (i, 1)),
        pl.BlockSpec((S * T, SSD_WIDTH), lambda i: (i, 2)),
        pl.BlockSpec((S * T, SSD_CONV_DIM), lambda i: (i, 2)),
        pl.BlockSpec((S * T, DT_PAD), lambda i: (i, PROJ_MAIN // DT_PAD)),
        pl.BlockSpec((S, CONV_WIDTH - 1, LRU_WIDTH), lambda i: (i, 0, 0)),
        pl.BlockSpec((S, 1, LRU_WIDTH), lambda i: (i, 0, 0)),
        pl.BlockSpec((S, CONV_WIDTH - 1, SSD_CONV_DIM), lambda i: (i, 0, 0)),
        pl.BlockSpec((S, SSD_WIDTH, D_STATE), lambda i: (i, 0, 0)),
    ] + _param_specs(const) + [
        pl.BlockSpec((2 * LANES, N_SSD_HEADS * LANES), const),
        pl.BlockSpec((2 * LANES, SSD_WIDTH), const),
    ]
    out_shape = (
        jax.ShapeDtypeStruct((nseq * T, MIX_WIDTH), F32),
        jax.ShapeDtypeStruct((nseq, CONV_WIDTH - 1, LRU_WIDTH), F32),
        jax.ShapeDtypeStruct((nseq, 1, LRU_WIDTH), F32),
        jax.ShapeDtypeStruct((nseq, CONV_WIDTH - 1, SSD_CONV_DIM), F32),
        jax.ShapeDtypeStruct((nseq, SSD_WIDTH, D_STATE), F32),
    )
    out_specs = (
        pl.BlockSpec((S * T, MIX_WIDTH), lambda i: (i, 0)),
        pl.BlockSpec((S, CONV_WIDTH - 1, LRU_WIDTH), lambda i: (i, 0, 0)),
        pl.BlockSpec((S, 1, LRU_WIDTH), lambda i: (i, 0, 0)),
        pl.BlockSpec((S, CONV_WIDTH - 1, SSD_CONV_DIM), lambda i: (i, 0, 0)),
        pl.BlockSpec((S, SSD_WIDTH, D_STATE), lambda i: (i, 0, 0)),
    )
    scratch = [
        pltpu.VMEM((S, SUBLANES, LRU_WIDTH), F32),
        pltpu.VMEM((S, SUBLANES, SSD_CONV_DIM), F32),
        pltpu.VMEM((S, SUBLANES, LRU_WIDTH), F32),
        pltpu.VMEM((S * SUBLANES, SSD_WIDTH), F32),
    ]
    return pl.pallas_call(
        functools.partial(_mixer_sample_kernel, T=T),
        out_shape=out_shape,
        grid=(nseq // S,),
        in_specs=in_specs,
        out_specs=out_specs,
        scratch_shapes=scratch,
        compiler_params=pltpu.CompilerParams(
            dimension_semantics=("parallel",), vmem_limit_bytes=VMEM_LIMIT),
        name="mixer_sample",
    )(proj, proj, proj, proj, proj, st_lc, st_lh, st_sc, st_sh, *params, sel_t, sel_p)


def _gate_weights(w_a, w_x):
    def tiles(w):
        per = MXU_DIM // LRU_BLOCK
        w4 = w.reshape(N_LRU_HEADS // per, per, LRU_BLOCK, LRU_BLOCK)
        eye = jnp.eye(per, dtype=w.dtype)
        t = jnp.einsum('jaik,ab->jaibk', w4, eye)
        return t.reshape(N_LRU_HEADS // per, MXU_DIM, MXU_DIM)
    return jnp.concatenate([tiles(w_a), tiles(w_x)], axis=2).astype(BF16)


def kernel(x_prompt, x_sample, state_lru_conv, state_lru_h, state_ssd_conv, state_ssd_h, g_mix, w_in,
           lru_conv_w, lru_conv_b, w_a, b_a, w_x, b_x, lam, g_lru_out, ssd_conv_w, ssd_conv_b, dt_bias,
           a_log, d_skip, g_ssd_out, w_out, g_mlp, w_up, w_down, g_final):
    depth = w_in.shape[0]
    assert depth == 1
    bp, seq, _ = x_prompt.shape
    bs, dseq, _ = x_sample.shape
    l = 0
    row = lambda v: v.reshape(1, -1)
    w_main, w_dt = _split_w_in(w_in[l])
    params = (
        lru_conv_w[l], row(lru_conv_b[l]), _gate_weights(w_a[l], w_x[l]),
        row(b_a[l]), row(b_x[l]), row(lam[l]), row(g_lru_out[l]),
        ssd_conv_w[l], row(ssd_conv_b[l]),
        jnp.pad(row(dt_bias[l]), ((0, 0), (0, DT_PAD - N_SSD_HEADS))),
        jnp.pad(row(a_log[l]), ((0, 0), (0, DT_PAD - N_SSD_HEADS))),
        row(jnp.repeat(d_skip[l], SSD_HEAD_DIM)), row(g_ssd_out[l]),
    )
    w_out_b = w_out[l].astype(BF16)
    w_up_b = w_up[l].astype(BF16)
    w_down_b = w_down[l].astype(BF16)
    gmix = row(g_mix[l])
    gmlp = row(g_mlp[l])
    gfin = row(g_final)

    xp2 = x_prompt.reshape(bp * seq, D_MODEL)
    (lcw, lcb, wg, ba, bx, lam_r, glru, scw, scb, dtb, alog, dskip, gssd) = params
    act_p, p_lc, p_sc = _in_proj_prompt(xp2, bp, gmix, w_main, w_dt, lcw, lcb, scw, scb, dtb)
    sel_t, sel_p = _head_selectors()
    ymix_p, p_lh, p_sh = _mixer_prompt(
        act_p.reshape(bp, seq, PROJ_PAD), wg, ba, bx, lam_r, glru, alog, dskip, gssd, sel_t, sel_p)
    y_prompt = _out_mlp(xp2, ymix_p.reshape(bp * seq, MIX_WIDTH), w_out_b, gmlp, w_up_b, w_down_b, gfin)

    xs2 = x_sample.reshape(bs * dseq, D_MODEL)
    proj_s = _in_proj(xs2, gmix, w_main, w_dt)
    ymix_s, s_lc, s_lh, s_sc, s_sh = _mixer_sample(
        proj_s, dseq, state_lru_conv[l], state_lru_h[l].reshape(bs, 1, LRU_WIDTH), state_ssd_conv[l],
        state_ssd_h[l].reshape(bs, SSD_WIDTH, D_STATE), params, sel_t, sel_p)
    y_sample = _out_mlp(xs2, ymix_s, w_out_b, gmlp, w_up_b, w_down_b, gfin)

    hshape = (N_SSD_HEADS, SSD_HEAD_DIM, D_STATE)
    return (
        y_prompt.reshape(bp, seq, D_MODEL), y_sample.reshape(bs, dseq, D_MODEL),
        p_lc[None], p_lh.reshape(1, bp, LRU_WIDTH), p_sc[None], p_sh.reshape(1, bp, *hshape),
        s_lc[None], s_lh.reshape(1, bs, LRU_WIDTH), s_sc[None], s_sh.reshape(1, bs, *hshape),
    )
```

```python
import functools
import math

import jax
import jax.numpy as jnp
from jax import lax
from jax.experimental import pallas as pl
from jax.experimental.pallas import tpu as pltpu

F32 = jnp.float32
BF16 = jnp.bfloat16

D_MODEL = 1024
LRU_WIDTH = 1024
N_LRU_HEADS = 16
LRU_BLOCK = 64
LRU_C = 8.0
SSD_WIDTH = 1024
SSD_HEAD_DIM = 64
N_SSD_HEADS = 16
N_SSD_GROUPS = 2
D_STATE = 128
CONV_WIDTH = 4
SSD_CONV_DIM = SSD_WIDTH + 2 * N_SSD_GROUPS * D_STATE
D_FF = 4 * D_MODEL
EPS = 1e-6

LANES = 128
SUBLANES = 8
MXU_DIM = 256
DT_PAD = LANES
PROJ_MAIN = 2 * LRU_WIDTH + SSD_WIDTH + SSD_CONV_DIM
PROJ_PAD = PROJ_MAIN + DT_PAD
MIX_WIDTH = LRU_WIDTH + SSD_WIDTH
SSD_CHUNK = 128
PROMPT_TC = 256
PROMPT_NB = 2
ROW_TILE = 512
SAMPLE_SEQS = SSD_CHUNK // SUBLANES
SCAN_RUN = PROMPT_TC // SUBLANES
SCAN_PITCH = SCAN_RUN + 4
NEG_BIG = -1e30
LOG2E = 1.4426950408889634
VMEM_LIMIT = 56 * 1024 * 1024


def _rms(x, g):
    ms = jnp.mean(x * x, axis=-1, keepdims=True)
    return x * lax.rsqrt(ms + EPS) * g


def _sigmoid(x):
    return 1.0 / (1.0 + jnp.exp(-x))


def _softplus(x):
    return jnp.maximum(x, 0.0) + jnp.log1p(jnp.exp(-jnp.abs(x)))


def _gelu_tanh(x):
    c = math.sqrt(2.0 / math.pi)
    return 0.5 * x * (1.0 + jnp.tanh(c * (x + 0.044715 * (x * x * x))))


def _lru_coeffs(u, wg_ref, b_a, b_x, neg_c_sp):
    ub = u.astype(BF16)
    r_parts, i_parts = [], []
    for j in range(LRU_WIDTH // MXU_DIM):
        g = jnp.dot(ub[:, MXU_DIM * j:MXU_DIM * (j + 1)], wg_ref[j], preferred_element_type=F32)
        r_parts.append(g[:, :MXU_DIM])
        i_parts.append(g[:, MXU_DIM:])
    r = _sigmoid(jnp.concatenate(r_parts, axis=1) + b_a)
    i = _sigmoid(jnp.concatenate(i_parts, axis=1) + b_x)
    log_a = r * neg_c_sp
    a = jnp.exp(log_a)
    th = jnp.tanh(log_a)
    v = (th + th) / (th - 1.0)
    mult = jnp.where(v > 0.0, v * lax.rsqrt(v), 0.0)
    return a, mult * (i * u)


def _scan_within_8(a, b):
    ridx = lax.broadcasted_iota(jnp.int32, a.shape, 0) & (SUBLANES - 1)
    for k in (1, 2, 4):
        a_s = pltpu.roll(a, k, axis=0)
        b_s = pltpu.roll(b, k, axis=0)
        m = ridx >= k
        b = jnp.where(m, a * b_s + b, b)
        a = jnp.where(m, a * a_s, a)
    return a, b


def _conv_slabs(ext, w_ref, b_ref, rows, first):
    parts = []
    for s in range(ext.shape[0]):
        cols = slice(LANES * s, LANES * (s + 1))
        acc = b_ref[:, cols] + ext[s, pl.ds(first, rows), :] * w_ref[0:1, cols]
        for k in range(1, CONV_WIDTH):
            acc = acc + ext[s, pl.ds(first + k, rows), :] * w_ref[k:k + 1, cols]
        parts.append(acc)
    return jnp.concatenate(parts, axis=1)


def _lru_scan_strided(a, b, hcar, a_pad, b_pad, h_pad):
    rows = a.shape[0]
    S = rows // SUBLANES
    nslab = LRU_WIDTH // LANES
    ridx = lax.broadcasted_iota(jnp.int32, (SUBLANES, LANES), 0)
    step = lambda ref, s, i: ref[s, pl.ds(i, SUBLANES, stride=SCAN_PITCH), :]
    for s in range(nslab):
        cols = slice(LANES * s, LANES * (s + 1))
        for j in range(SUBLANES):
            a_pad[s, SCAN_PITCH * j:SCAN_PITCH * j + S, :] = a[S * j:S * (j + 1), cols]
            b_pad[s, SCAN_PITCH * j:SCAN_PITCH * j + S, :] = b[S * j:S * (j + 1), cols]
    h = [jnp.zeros((SUBLANES, LANES), F32)] * nslab
    prod = [jnp.ones((SUBLANES, LANES), F32)] * nslab
    for i in range(S):
        for s in range(nslab):
            av = step(a_pad, s, i)
            h[s] = av * h[s] + step(b_pad, s, i)
            prod[s] = av * prod[s]
    for s in range(nslab):
        cols = slice(LANES * s, LANES * (s + 1))
        pcum, hcum = _scan_within_8(prod[s], h[s])
        cin = hcar[:, cols]
        ends = hcum + pcum * cin
        h[s] = jnp.where(ridx == 0, cin, pltpu.roll(ends, 1, axis=0))
        hcar[:, cols] = jnp.broadcast_to(ends[SUBLANES - 1:SUBLANES, :], (SUBLANES, LANES))
    for i in range(S):
        for s in range(nslab):
            h[s] = step(a_pad, s, i) * h[s] + step(b_pad, s, i)
            h_pad[s, pl.ds(i, SUBLANES, stride=SCAN_PITCH), :] = h[s]
    return jnp.concatenate(
        [jnp.concatenate([h_pad[s, SCAN_PITCH * j:SCAN_PITCH * j + S, :] for j in range(SUBLANES)], axis=0)
         for s in range(nslab)], axis=1)


def _ssd_cumdecay(dt, a2_row, tri):
    da = dt * a2_row
    p0 = da.astype(BF16)
    p1 = (da - p0.astype(F32)).astype(BF16)
    cum2 = jnp.dot(jnp.concatenate([tri, tri], axis=1), jnp.concatenate([p0, p1], axis=0),
                   preferred_element_type=F32)
    return cum2, cum2.T[0:N_SSD_HEADS, :]


def _ssd_diag(xs, bm, cm, dt, cols, cum2_t, mask_add):
    L = xs.shape[0]
    c2_t = cum2_t - jnp.log2(dt.T[0:N_SSD_HEADS, :])
    lane = lax.broadcasted_iota(jnp.int32, (L, LANES), 1)
    lo = lane < SSD_HEAD_DIM
    y_parts = []
    for g in range(N_SSD_GROUPS):
        bg = bm[:, D_STATE * g:D_STATE * (g + 1)].astype(BF16)
        cg = cm[:, D_STATE * g:D_STATE * (g + 1)].astype(BF16)
        cb = lax.dot_general(cg, bg, (((1,), (1,)), ((), ())), preferred_element_type=F32)
        for jj in range(N_SSD_HEADS // N_SSD_GROUPS // 2):
            j = (N_SSD_HEADS // N_SSD_GROUPS // 2) * g + jj
            h0, h1 = 2 * j, 2 * j + 1
            col0 = cols[:, LANES * h0:LANES * (h0 + 1)]
            col1 = cols[:, LANES * h1:LANES * (h1 + 1)]
            m0 = cb * jnp.exp2(col0 - c2_t[h0:h0 + 1, :] + mask_add)
            m1 = cb * jnp.exp2(col1 - c2_t[h1:h1 + 1, :] + mask_add)
            lhs = jnp.concatenate([m0, m1], axis=1).astype(BF16)
            xp = xs[:, LANES * j:LANES * (j + 1)]
            rhs = jnp.concatenate([jnp.where(lo, xp, 0.0), jnp.where(lo, 0.0, xp)], axis=0).astype(BF16)
            y_parts.append(jnp.dot(lhs, rhs, preferred_element_type=F32))
    return jnp.concatenate(y_parts, axis=1)


def _spread(v, sel_ref):
    p0 = v.astype(BF16)
    p1 = (v - p0.astype(F32)).astype(BF16)
    return jnp.dot(jnp.concatenate([p0, p1], axis=1), sel_ref[...], preferred_element_type=F32)


def _ssd_gate_norm(ys, xs, z_act, dskip, g_ssd):
    ys = ys + dskip * xs
    gated = ys * z_act
    half = SSD_WIDTH // N_SSD_GROUPS
    outs = []
    for g in range(N_SSD_GROUPS):
        outs.append(_rms(gated[:, half * g:half * (g + 1)], g_ssd[:, half * g:half * (g + 1)]))
    return jnp.concatenate(outs, axis=1)


def _split_w_in_kernel(w_ref, main_ref, dt_ref):
    main_ref[...] = w_ref[:, 0:PROJ_MAIN].astype(BF16)
    dt_ref[...] = jnp.zeros_like(dt_ref)
    dt_ref[:, 0:N_SSD_HEADS] = w_ref[:, PROJ_MAIN:PROJ_MAIN + N_SSD_HEADS].astype(BF16)


def _split_w_in(w):
    rows, cols = w.shape
    blk = MXU_DIM
    return pl.pallas_call(
        _split_w_in_kernel,
        out_shape=(jax.ShapeDtypeStruct((rows, PROJ_MAIN), BF16), jax.ShapeDtypeStruct((rows, DT_PAD), BF16)),
        grid=(rows // blk,),
        in_specs=[pl.BlockSpec((blk, cols), lambda i: (i, 0))],
        out_specs=(pl.BlockSpec((blk, PROJ_MAIN), lambda i: (i, 0)), pl.BlockSpec((blk, DT_PAD), lambda i: (i, 0))),
        compiler_params=pltpu.CompilerParams(dimension_semantics=("parallel",)),
        name="split_w_in",
    )(w)


def _inproj_kernel(x_ref, g_ref, w_ref, wdt_ref, o_ref):
    hn = _rms(x_ref[...], g_ref[...]).astype(BF16)
    o_ref[:, 0:PROJ_MAIN] = jnp.dot(hn, w_ref[...], preferred_element_type=F32)
    o_ref[:, PROJ_MAIN:PROJ_PAD] = jnp.dot(hn, wdt_ref[...], preferred_element_type=F32)


def _in_proj(x2d, g_mix, w_main, w_dt):
    n = x2d.shape[0]
    return pl.pallas_call(
        _inproj_kernel,
        out_shape=jax.ShapeDtypeStruct((n, PROJ_PAD), F32),
        grid=(n // ROW_TILE,),
        in_specs=[
            pl.BlockSpec((ROW_TILE, D_MODEL), lambda i: (i, 0)),
            pl.BlockSpec((1, D_MODEL), lambda i: (0, 0)),
            pl.BlockSpec((D_MODEL, PROJ_MAIN), lambda i: (0, 0), pipeline_mode=pl.Buffered(1)),
            pl.BlockSpec((D_MODEL, DT_PAD), lambda i: (0, 0), pipeline_mode=pl.Buffered(1)),
        ],
        out_specs=pl.BlockSpec((ROW_TILE, PROJ_PAD), lambda i: (i, 0)),
        compiler_params=pltpu.CompilerParams(
            dimension_semantics=("parallel",), vmem_limit_bytes=VMEM_LIMIT),
        name="in_proj",
    )(x2d, g_mix, w_main, w_dt)


def _inproj_prompt_kernel(x_ref, g_ref, w_ref, wdt_ref, lcw_ref, lcb_ref, scw_ref, scb_ref, dtb_ref,
                          o_ref, olc_ref, osc_ref, ext_l, ext_s, *, steps_per_seq):
    t = lax.rem(pl.program_id(0), steps_per_seq)
    rows = ROW_TILE
    hist = SUBLANES
    o1, o2, o3 = LRU_WIDTH, 2 * LRU_WIDTH, 2 * LRU_WIDTH + SSD_WIDTH

    @pl.when(t == 0)
    def _init():
        ext_l[:, 0:hist, :] = jnp.zeros((ext_l.shape[0], hist, LANES), F32)
        ext_s[:, 0:hist, :] = jnp.zeros((ext_s.shape[0], hist, LANES), F32)

    hn = _rms(x_ref[...], g_ref[...]).astype(BF16)
    lx = jnp.dot(hn, w_ref[:, 0:o1], preferred_element_type=F32)
    for s in range(ext_l.shape[0]):
        ext_l[s, hist:hist + rows, :] = lx[:, LANES * s:LANES * (s + 1)]
    xbc_in = jnp.dot(hn, w_ref[:, o3:PROJ_MAIN], preferred_element_type=F32)
    for s in range(ext_s.shape[0]):
        ext_s[s, hist:hist + rows, :] = xbc_in[:, LANES * s:LANES * (s + 1)]
    o_ref[:, o1:o2] = _gelu_tanh(jnp.dot(hn, w_ref[:, o1:o2], preferred_element_type=F32))
    z = jnp.dot(hn, w_ref[:, o2:o3], preferred_element_type=F32)
    o_ref[:, o2:o3] = z * _sigmoid(z)
    o_ref[:, PROJ_MAIN:PROJ_PAD] = _softplus(
        jnp.dot(hn, wdt_ref[...], preferred_element_type=F32) + dtb_ref[...])
    o_ref[:, 0:o1] = _conv_slabs(ext_l, lcw_ref, lcb_ref, rows, hist - (CONV_WIDTH - 1))
    xbc = _conv_slabs(ext_s, scw_ref, scb_ref, rows, hist - (CONV_WIDTH - 1))
    o_ref[:, o3:PROJ_MAIN] = xbc * _sigmoid(xbc)

    @pl.when(t == steps_per_seq - 1)
    def _final():
        last = slice(hist + rows - (CONV_WIDTH - 1), hist + rows)
        for s in range(ext_l.shape[0]):
            olc_ref[:, LANES * s:LANES * (s + 1)] = ext_l[s, last, :]
        for s in range(ext_s.shape[0]):
            osc_ref[:, LANES * s:LANES * (s + 1)] = ext_s[s, last, :]

    tail_l = ext_l[:, rows:rows + hist, :]
    tail_s = ext_s[:, rows:rows + hist, :]
    ext_l[:, 0:hist, :] = tail_l
    ext_s[:, 0:hist, :] = tail_s


def _in_proj_prompt(x2d, bsz, g_mix, w_main, w_dt, lcw, lcb, scw, scb, dtb):
    n = x2d.shape[0]
    steps_per_seq = n // bsz // ROW_TILE
    const = lambda i: (0, 0)
    return pl.pallas_call(
        functools.partial(_inproj_prompt_kernel, steps_per_seq=steps_per_seq),
        out_shape=(
            jax.ShapeDtypeStruct((n, PROJ_PAD), F32),
            jax.ShapeDtypeStruct((bsz, CONV_WIDTH - 1, LRU_WIDTH), F32),
            jax.ShapeDtypeStruct((bsz, CONV_WIDTH - 1, SSD_CONV_DIM), F32),
        ),
        grid=(n // ROW_TILE,),
        in_specs=[
            pl.BlockSpec((ROW_TILE, D_MODEL), lambda i: (i, 0)),
            pl.BlockSpec((1, D_MODEL), const),
            pl.BlockSpec((D_MODEL, PROJ_MAIN), const, pipeline_mode=pl.Buffered(1)),
            pl.BlockSpec((D_MODEL, DT_PAD), const, pipeline_mode=pl.Buffered(1)),
            pl.BlockSpec((CONV_WIDTH, LRU_WIDTH), const),
            pl.BlockSpec((1, LRU_WIDTH), const),
            pl.BlockSpec((CONV_WIDTH, SSD_CONV_DIM), const),
            pl.BlockSpec((1, SSD_CONV_DIM), const),
            pl.BlockSpec((1, DT_PAD), const),
        ],
        out_specs=(
            pl.BlockSpec((ROW_TILE, PROJ_PAD), lambda i: (i, 0)),
            pl.BlockSpec((None, CONV_WIDTH - 1, LRU_WIDTH), lambda i: (i // steps_per_seq, 0, 0)),
            pl.BlockSpec((None, CONV_WIDTH - 1, SSD_CONV_DIM), lambda i: (i // steps_per_seq, 0, 0)),
        ),
        scratch_shapes=[
            pltpu.VMEM((LRU_WIDTH // LANES, SUBLANES + ROW_TILE, LANES), F32),
            pltpu.VMEM((SSD_CONV_DIM // LANES, SUBLANES + ROW_TILE, LANES), F32),
        ],
        compiler_params=pltpu.CompilerParams(
            dimension_semantics=("arbitrary",), vmem_limit_bytes=VMEM_LIMIT),
        name="in_proj_prompt",
    )(x2d, g_mix, w_main, w_dt, lcw, lcb, scw, scb, dtb)


def _outmlp_kernel(x_ref, y_ref, wo_ref, gm_ref, wu_ref, wd_ref, gf_ref, o_ref):
    x1 = x_ref[...] + jnp.dot(y_ref[...].astype(BF16), wo_ref[...], preferred_element_type=F32)
    m = _rms(x1, gm_ref[...]).astype(BF16)
    u = jnp.dot(m, wu_ref[...], preferred_element_type=F32)
    u = jnp.square(jnp.maximum(u, 0.0)).astype(BF16)
    x2 = x1 + jnp.dot(u, wd_ref[...], preferred_element_type=F32)
    o_ref[...] = _rms(x2, gf_ref[...])


def _out_mlp(x2d, ymix2d, w_out_b, g_mlp, w_up_b, w_down_b, g_final):
    n = x2d.shape[0]
    const = lambda i: (0, 0)
    return pl.pallas_call(
        _outmlp_kernel,
        out_shape=jax.ShapeDtypeStruct((n, D_MODEL), F32),
        grid=(n // ROW_TILE,),
        in_specs=[
            pl.BlockSpec((ROW_TILE, D_MODEL), lambda i: (i, 0)),
            pl.BlockSpec((ROW_TILE, MIX_WIDTH), lambda i: (i, 0)),
            pl.BlockSpec((MIX_WIDTH, D_MODEL), const, pipeline_mode=pl.Buffered(1)),
            pl.BlockSpec((1, D_MODEL), const),
            pl.BlockSpec((D_MODEL, D_FF), const, pipeline_mode=pl.Buffered(1)),
            pl.BlockSpec((D_FF, D_MODEL), const, pipeline_mode=pl.Buffered(1)),
            pl.BlockSpec((1, D_MODEL), const),
        ],
        out_specs=pl.BlockSpec((ROW_TILE, D_MODEL), lambda i: (i, 0)),
        compiler_params=pltpu.CompilerParams(
            dimension_semantics=("parallel",), vmem_limit_bytes=VMEM_LIMIT),
        name="out_mlp",
    )(x2d, ymix2d, w_out_b, g_mlp, w_up_b, w_down_b, g_final)


def _mixer_prompt_kernel(u_ref, gl_ref, zact_ref, xbc_ref, dt_ref,
                         wg_ref, ba_ref, bx_ref, lam_ref, glru_ref, alog_ref, dskip_ref, gssd_ref,
                         selt_ref, selp_ref,
                         y_ref, olh_ref, osh_ref,
                         a_pad, b_pad, h_pad, hcar, ht):
    t = pl.program_id(1)
    nt = pl.num_programs(1)
    tc = PROMPT_TC

    @pl.when(t == 0)
    def _init():
        hcar[...] = jnp.zeros_like(hcar)
        ht[...] = jnp.zeros_like(ht)

    neg_c_sp = (-LRU_C) * _softplus(-lam_ref[...])
    lane1 = lax.broadcasted_iota(jnp.int32, (1, LANES), 1)
    a2_row = jnp.where(lane1 < N_SSD_HEADS, -LOG2E * jnp.exp(alog_ref[...]), 0.0)
    L = SSD_CHUNK
    rr = lax.broadcasted_iota(jnp.int32, (L, L), 0)
    cc = lax.broadcasted_iota(jnp.int32, (L, L), 1)
    causal = cc <= rr
    tri = jnp.where(causal, 1.0, 0.0).astype(BF16)
    mask_add = jnp.where(causal, 0.0, NEG_BIG).astype(F32)
    half = SSD_WIDTH // N_SSD_GROUPS

    chunks = [(n, c) for n in range(PROMPT_NB) for c in range(tc // L)]
    cums = [_ssd_cumdecay(dt_ref[n, L * c:L * (c + 1), :], a2_row, tri) for n, c in chunks]
    dts = [dt_ref[n, L * c:L * (c + 1), :] for n, c in chunks]
    cum_all = jnp.concatenate([cum2 for cum2, _ in cums], axis=0)
    cols_all = _spread(cum_all, selt_ref)
    ecol_all = _spread(jnp.exp2(cum_all), selp_ref)
    sdt_all = _spread(jnp.concatenate(
        [jnp.exp2(cum2[L - 1:L, :] - cum2) * dt for (cum2, _), dt in zip(cums, dts)], axis=0), selp_ref)

    for n in range(PROMPT_NB):
        a, b = _lru_coeffs(u_ref[n], wg_ref, ba_ref[...], bx_ref[...], neg_c_sp)
        hseq = _lru_scan_strided(a, b, hcar.at[n], a_pad.at[n], b_pad.at[n], h_pad.at[n])
        y_ref[n, :, 0:LRU_WIDTH] = _rms(hseq * gl_ref[n], glru_ref[...])

        for c in range(tc // L):
            k = chunks.index((n, c))
            rows = slice(L * c, L * (c + 1))
            krows = slice(L * k, L * (k + 1))
            xs = xbc_ref[n, rows, 0:SSD_WIDTH]
            bm = xbc_ref[n, rows, SSD_WIDTH:SSD_WIDTH + N_SSD_GROUPS * D_STATE]
            cm = xbc_ref[n, rows, SSD_WIDTH + N_SSD_GROUPS * D_STATE:SSD_CONV_DIM]
            y_diag = _ssd_diag(xs, bm, cm, dts[k], cols_all[krows, :], cums[k][1], mask_add)
            ecol = ecol_all[krows, :]
            xw = xs * sdt_all[krows, :]
            dec = ecol[L - 1:L, :]
            y_off_parts = []
            for g in range(N_SSD_GROUPS):
                htg = ht[n, g]
                cg = cm[:, D_STATE * g:D_STATE * (g + 1)].astype(BF16)
                y_off_parts.append(jnp.dot(cg, htg.astype(BF16), preferred_element_type=F32))
                bg_t = bm[:, D_STATE * g:D_STATE * (g + 1)].T.astype(BF16)
                st = jnp.dot(bg_t, xw[:, half * g:half * (g + 1)].astype(BF16), preferred_element_type=F32)
                ht[n, g] = htg * dec[:, half * g:half * (g + 1)] + st
            ys = y_diag + jnp.concatenate(y_off_parts, axis=1) * ecol
            y_ref[n, rows, LRU_WIDTH:MIX_WIDTH] = _ssd_gate_norm(
                ys, xs, zact_ref[n, rows, :], dskip_ref[...], gssd_ref[...])

    @pl.when(t == nt - 1)
    def _final():
        for n in range(PROMPT_NB):
            olh_ref[n] = hcar[n, 0:1, :]
            for g in range(N_SSD_GROUPS):
                osh_ref[n, half * g:half * (g + 1), :] = ht[n, g].T


def _param_specs(const):
    return [
        pl.BlockSpec((CONV_WIDTH, LRU_WIDTH), const),
        pl.BlockSpec((1, LRU_WIDTH), const),
        pl.BlockSpec((LRU_WIDTH // MXU_DIM, MXU_DIM, 2 * MXU_DIM), lambda *_: (0, 0, 0)),
        pl.BlockSpec((1, LRU_WIDTH), const),
        pl.BlockSpec((1, LRU_WIDTH), const),
        pl.BlockSpec((1, LRU_WIDTH), const),
        pl.BlockSpec((1, LRU_WIDTH), const),
        pl.BlockSpec((CONV_WIDTH, SSD_CONV_DIM), const),
        pl.BlockSpec((1, SSD_CONV_DIM), const),
        pl.BlockSpec((1, DT_PAD), const),
        pl.BlockSpec((1, DT_PAD), const),
        pl.BlockSpec((1, SSD_WIDTH), const),
        pl.BlockSpec((1, SSD_WIDTH), const),
    ]


def _head_selectors():
    k = jnp.arange(2 * LANES)[:, None] % LANES
    sel_t = (k == jnp.arange(N_SSD_HEADS * LANES)[None, :] // LANES).astype(BF16)
    sel_p = (k == jnp.arange(SSD_WIDTH)[None, :] // SSD_HEAD_DIM).astype(BF16)
    return sel_t, sel_p


def _mixer_prompt(act, wg, b_a, b_x, lam, g_lru, a_log, d_skip, g_ssd, sel_t, sel_p):
    bsz, seq, _ = act.shape
    tc = PROMPT_TC
    nb = PROMPT_NB
    const = lambda b, t: (0, 0)
    in_specs = [
        pl.BlockSpec((nb, tc, LRU_WIDTH), lambda b, t: (b, t, 0)),
        pl.BlockSpec((nb, tc, LRU_WIDTH), lambda b, t: (b, t, 1)),
        pl.BlockSpec((nb, tc, SSD_WIDTH), lambda b, t: (b, t, 2)),
        pl.BlockSpec((nb, tc, SSD_CONV_DIM), lambda b, t: (b, t, 2)),
        pl.BlockSpec((nb, tc, DT_PAD), lambda b, t: (b, t, PROJ_MAIN // DT_PAD)),
        pl.BlockSpec((LRU_WIDTH // MXU_DIM, MXU_DIM, 2 * MXU_DIM), lambda b, t: (0, 0, 0)),
        pl.BlockSpec((1, LRU_WIDTH), const),
        pl.BlockSpec((1, LRU_WIDTH), const),
        pl.BlockSpec((1, LRU_WIDTH), const),
        pl.BlockSpec((1, LRU_WIDTH), const),
        pl.BlockSpec((1, DT_PAD), const),
        pl.BlockSpec((1, SSD_WIDTH), const),
        pl.BlockSpec((1, SSD_WIDTH), const),
        pl.BlockSpec((2 * LANES, N_SSD_HEADS * LANES), const),
        pl.BlockSpec((2 * LANES, SSD_WIDTH), const),
    ]
    out_shape = (
        jax.ShapeDtypeStruct((bsz, seq, MIX_WIDTH), F32),
        jax.ShapeDtypeStruct((bsz, 1, LRU_WIDTH), F32),
        jax.ShapeDtypeStruct((bsz, SSD_WIDTH, D_STATE), F32),
    )
    out_specs = (
        pl.BlockSpec((nb, tc, MIX_WIDTH), lambda b, t: (b, t, 0)),
        pl.BlockSpec((nb, 1, LRU_WIDTH), lambda b, t: (b, 0, 0)),
        pl.BlockSpec((nb, SSD_WIDTH, D_STATE), lambda b, t: (b, 0, 0)),
    )
    scratch = [
        pltpu.VMEM((nb, LRU_WIDTH // LANES, SUBLANES * SCAN_PITCH, LANES), F32),
        pltpu.VMEM((nb, LRU_WIDTH // LANES, SUBLANES * SCAN_PITCH, LANES), F32),
        pltpu.VMEM((nb, LRU_WIDTH // LANES, SUBLANES * SCAN_PITCH, LANES), F32),
        pltpu.VMEM((nb, SUBLANES, LRU_WIDTH), F32),
        pltpu.VMEM((nb, N_SSD_GROUPS, D_STATE, SSD_WIDTH // N_SSD_GROUPS), F32),
    ]
    return pl.pallas_call(
        _mixer_prompt_kernel,
        out_shape=out_shape,
        grid=(bsz // nb, seq // tc),
        in_specs=in_specs,
        out_specs=out_specs,
        scratch_shapes=scratch,
        compiler_params=pltpu.CompilerParams(
            dimension_semantics=("parallel", "arbitrary"), vmem_limit_bytes=VMEM_LIMIT),
        name="mixer_prompt",
    )(act, act, act, act, act, wg, b_a, b_x, lam, g_lru, a_log, d_skip, g_ssd, sel_t, sel_p)


def _mixer_sample_kernel(lx_ref, gate_ref, z_ref, xbc_ref, dt_ref,
                         slc_ref, slh_ref, ssc_ref, ssh_ref,
                         lcw_ref, lcb_ref, wg_ref, ba_ref, bx_ref, lam_ref, glru_ref,
                         scw_ref, scb_ref, dtb_ref, alog_ref, dskip_ref, gssd_ref, selt_ref, selp_ref,
                         y_ref, olc_ref, olh_ref, osc_ref, osh_ref,
                         ext_l, ext_s, pad_scr, yoff_scr, *, T):
    S = SAMPLE_SEQS
    P = SUBLANES
    K1 = CONV_WIDTH - 1
    R = S * P
    row_i = lax.broadcasted_iota(jnp.int32, (R, 1), 0) & (P - 1)
    valid = row_i < T

    def pad_rows(ref):
        width = ref.shape[-1]
        pad_scr[:, :, 0:width] = jnp.zeros((S, P, width), F32)
        pad_scr[:, 0:T, 0:width] = ref[...].reshape(S, T, width)
        return pad_scr[:, :, 0:width].reshape(R, width)

    ext_l[...] = jnp.zeros_like(ext_l)
    ext_s[...] = jnp.zeros_like(ext_s)
    ext_l[:, 0:K1, :] = slc_ref[...]
    ext_l[:, K1:K1 + T, :] = lx_ref[...].reshape(S, T, LRU_WIDTH)
    ext_s[:, 0:K1, :] = ssc_ref[...]
    ext_s[:, K1:K1 + T, :] = xbc_ref[...].reshape(S, T, SSD_CONV_DIM)
    olc_ref[...] = ext_l[:, T:T + K1, :]
    osc_ref[...] = ext_s[:, T:T + K1, :]

    el = ext_l[...].reshape(R, LRU_WIDTH)
    es = ext_s[...].reshape(R, SSD_CONV_DIM)

    def conv(e, w_ref, b_ref):
        out = b_ref[...] + e * w_ref[0:1, :]
        for k in range(1, CONV_WIDTH):
            out = out + pltpu.roll(e, R - k, axis=0) * w_ref[k:k + 1, :]
        return out

    u = conv(el, lcw_ref, lcb_ref)
    neg_c_sp = (-LRU_C) * _softplus(-lam_ref[...])
    a, b = _lru_coeffs(u, wg_ref, ba_ref[...], bx_ref[...], neg_c_sp)
    a, b = _scan_within_8(a, b)
    h0 = jnp.broadcast_to(slh_ref[...], (S, P, LRU_WIDTH)).reshape(R, LRU_WIDTH)
    hseq = a * h0 + b
    olh_ref[...] = hseq.reshape(S, P, LRU_WIDTH)[:, T - 1:T, :]
    gate = pad_rows(gate_ref)
    y_lru = _rms(hseq * _gelu_tanh(gate), glru_ref[...])

    xbc = conv(es, scw_ref, scb_ref)
    xbc = xbc * _sigmoid(xbc)
    xs = xbc[:, 0:SSD_WIDTH]
    bm = xbc[:, SSD_WIDTH:SSD_WIDTH + N_SSD_GROUPS * D_STATE]
    cm = xbc[:, SSD_WIDTH + N_SSD_GROUPS * D_STATE:]
    dt_raw = pad_rows(dt_ref)
    dt = jnp.where(valid, _softplus(dt_raw + dtb_ref[...]), 0.0)
    lane1 = lax.broadcasted_iota(jnp.int32, (1, LANES), 1)
    a2_row = jnp.where(lane1 < N_SSD_HEADS, -LOG2E * jnp.exp(alog_ref[...]), 0.0)

    rr = lax.broadcasted_iota(jnp.int32, (R, R), 0)
    cc = lax.broadcasted_iota(jnp.int32, (R, R), 1)
    allowed = (cc <= rr) & ((rr - cc) <= (rr & (P - 1)))
    tri = jnp.where(allowed, 1.0, 0.0).astype(BF16)
    mask_add = jnp.where(allowed, 0.0, NEG_BIG).astype(F32)

    cum2, cum2_t = _ssd_cumdecay(dt, a2_row, tri)
    y_diag = _ssd_diag(xs, bm, cm, dt, _spread(cum2, selt_ref), cum2_t, mask_add)
    ecol = _spread(jnp.exp2(cum2), selp_ref)
    end2 = jnp.broadcast_to(cum2.reshape(S, P, LANES)[:, P - 1:P, :], (S, P, LANES)).reshape(R, LANES)
    xw = xs * _spread(jnp.exp2(end2 - cum2) * dt, selp_ref)
    ecum_t = jnp.exp2(cum2_t)

    half = SSD_WIDTH // N_SSD_GROUPS
    for q in range(S):
        r0 = P * q
        vq = jnp.broadcast_to(ecum_t[:, r0 + P - 1:r0 + P], (N_SSD_HEADS, LANES))
        for g in range(N_SSD_GROUPS):
            hqg = ssh_ref[q, half * g:half * (g + 1), :]
            cq = cm[r0:r0 + P, D_STATE * g:D_STATE * (g + 1)].astype(BF16)
            yoff_scr[r0:r0 + P, half * g:half * (g + 1)] = lax.dot_general(
                cq, hqg.astype(BF16), (((1,), (1,)), ((), ())), preferred_element_type=F32)
            bq = bm[r0:r0 + P, D_STATE * g:D_STATE * (g + 1)].astype(BF16)
            xq = xw[r0:r0 + P, half * g:half * (g + 1)].astype(BF16)
            st = lax.dot_general(xq, bq, (((0,), (0,)), ((), ())), preferred_element_type=F32)
            for e in range(N_SSD_HEADS // N_SSD_GROUPS):
                h = (N_SSD_HEADS // N_SSD_GROUPS) * g + e
                lo_r = SSD_HEAD_DIM * e
                osh_ref[q, SSD_HEAD_DIM * h:SSD_HEAD_DIM * (h + 1), :] = (
                    vq[h:h + 1, :] * hqg[lo_r:lo_r + SSD_HEAD_DIM, :] + st[lo_r:lo_r + SSD_HEAD_DIM, :])

    ys = y_diag + yoff_scr[...] * ecol
    z = pad_rows(z_ref)
    y_ssd = _ssd_gate_norm(ys, xs, z * _sigmoid(z), dskip_ref[...], gssd_ref[...])
    y_ref[:, 0:LRU_WIDTH] = y_lru.reshape(S, P, LRU_WIDTH)[:, 0:T, :].reshape(S * T, LRU_WIDTH)
    y_ref[:, LRU_WIDTH:MIX_WIDTH] = y_ssd.reshape(S, P, SSD_WIDTH)[:, 0:T, :].reshape(S * T, SSD_WIDTH)


def _mixer_sample(proj, T, st_lc, st_lh, st_sc, st_sh, params, sel_t, sel_p):
    nseq = proj.shape[0] // T
    S = SAMPLE_SEQS
    const = lambda i: (0, 0)
    in_specs = [
        pl.BlockSpec((S * T, LRU_WIDTH), lambda i: (i, 0)),
        pl.BlockSpec((S * T, LRU_WIDTH), lambda i: (i, 1)),
        pl.BlockSpec((S * T, SSD_WIDTH), lambda i: (i, 2)),
        pl.BlockSpec((S * T, SSD_CONV_DIM), lambda i: (i, 2)),
        pl.BlockSpec((S * T, DT_PAD), lambda i: (i, PROJ_MAIN // DT_PAD)),
        pl.BlockSpec((S, CONV_WIDTH - 1, LRU_WIDTH), lambda i: (i, 0, 0)),
        pl.BlockSpec((S, 1, LRU_WIDTH), lambda i: (i, 0, 0)),
        pl.BlockSpec((S, CONV_WIDTH - 1, SSD_CONV_DIM), lambda i: (i, 0, 0)),
        pl.BlockSpec((S, SSD_WIDTH, D_STATE), lambda i: (i, 0, 0)),
    ] + _param_specs(const) + [
        pl.BlockSpec((2 * LANES, N_SSD_HEADS * LANES), const),
        pl.BlockSpec((2 * LANES, SSD_WIDTH), const),
    ]
    out_shape = (
        jax.ShapeDtypeStruct((nseq * T, MIX_WIDTH), F32),
        jax.ShapeDtypeStruct((nseq, CONV_WIDTH - 1, LRU_WIDTH), F32),
        jax.ShapeDtypeStruct((nseq, 1, LRU_WIDTH), F32),
        jax.ShapeDtypeStruct((nseq, CONV_WIDTH - 1, SSD_CONV_DIM), F32),
        jax.ShapeDtypeStruct((nseq, SSD_WIDTH, D_STATE), F32),
    )
    out_specs = (
        pl.BlockSpec((S * T, MIX_WIDTH), lambda i: (i, 0)),
        pl.BlockSpec((S, CONV_WIDTH - 1, LRU_WIDTH), lambda i: (i, 0, 0)),
        pl.BlockSpec((S, 1, LRU_WIDTH), lambda i: (i, 0, 0)),
        pl.BlockSpec((S, CONV_WIDTH - 1, SSD_CONV_DIM), lambda i: (i, 0, 0)),
        pl.BlockSpec((S, SSD_WIDTH, D_STATE), lambda i: (i, 0, 0)),
    )
    scratch = [
        pltpu.VMEM((S, SUBLANES, LRU_WIDTH), F32),
        pltpu.VMEM((S, SUBLANES, SSD_CONV_DIM), F32),
        pltpu.VMEM((S, SUBLANES, LRU_WIDTH), F32),
        pltpu.VMEM((S * SUBLANES, SSD_WIDTH), F32),
    ]
    return pl.pallas_call(
        functools.partial(_mixer_sample_kernel, T=T),
        out_shape=out_shape,
        grid=(nseq // S,),
        in_specs=in_specs,
        out_specs=out_specs,
        scratch_shapes=scratch,
        compiler_params=pltpu.CompilerParams(
            dimension_semantics=("parallel",), vmem_limit_bytes=VMEM_LIMIT),
        name="mixer_sample",
    )(proj, proj, proj, proj, proj, st_lc, st_lh, st_sc, st_sh, *params, sel_t, sel_p)


def _gate_weights(w_a, w_x):
    def tiles(w):
        per = MXU_DIM // LRU_BLOCK
        w4 = w.reshape(N_LRU_HEADS // per, per, LRU_BLOCK, LRU_BLOCK)
        eye = jnp.eye(per, dtype=w.dtype)
        t = jnp.einsum('jaik,ab->jaibk', w4, eye)
        return t.reshape(N_LRU_HEADS // per, MXU_DIM, MXU_DIM)
    return jnp.concatenate([tiles(w_a), tiles(w_x)], axis=2).astype(BF16)


def kernel(x_prompt, x_sample, state_lru_conv, state_lru_h, state_ssd_conv, state_ssd_h, g_mix, w_in,
           lru_conv_w, lru_conv_b, w_a, b_a, w_x, b_x, lam, g_lru_out, ssd_conv_w, ssd_conv_b, dt_bias,
           a_log, d_skip, g_ssd_out, w_out, g_mlp, w_up, w_down, g_final):
    depth = w_in.shape[0]
    assert depth == 1
    bp, seq, _ = x_prompt.shape
    bs, dseq, _ = x_sample.shape
    l = 0
    row = lambda v: v.reshape(1, -1)
    w_main, w_dt = _split_w_in(w_in[l])
    params = (
        lru_conv_w[l], row(lru_conv_b[l]), _gate_weights(w_a[l], w_x[l]),
        row(b_a[l]), row(b_x[l]), row(lam[l]), row(g_lru_out[l]),
        ssd_conv_w[l], row(ssd_conv_b[l]),
        jnp.pad(row(dt_bias[l]), ((0, 0), (0, DT_PAD - N_SSD_HEADS))),
        jnp.pad(row(a_log[l]), ((0, 0), (0, DT_PAD - N_SSD_HEADS))),
        row(jnp.repeat(d_skip[l], SSD_HEAD_DIM)), row(g_ssd_out[l]),
    )
    w_out_b = w_out[l].astype(BF16)
    w_up_b = w_up[l].astype(BF16)
    w_down_b = w_down[l].astype(BF16)
    gmix = row(g_mix[l])
    gmlp = row(g_mlp[l])
    gfin = row(g_final)

    xp2 = x_prompt.reshape(bp * seq, D_MODEL)
    (lcw, lcb, wg, ba, bx, lam_r, glru, scw, scb, dtb, alog, dskip, gssd) = params
    act_p, p_lc, p_sc = _in_proj_prompt(xp2, bp, gmix, w_main, w_dt, lcw, lcb, scw, scb, dtb)
    sel_t, sel_p = _head_selectors()
    ymix_p, p_lh, p_sh = _mixer_prompt(
        act_p.reshape(bp, seq, PROJ_PAD), wg, ba, bx, lam_r, glru, alog, dskip, gssd, sel_t, sel_p)
    y_prompt = _out_mlp(xp2, ymix_p.reshape(bp * seq, MIX_WIDTH), w_out_b, gmlp, w_up_b, w_down_b, gfin)

    xs2 = x_sample.reshape(bs * dseq, D_MODEL)
    proj_s = _in_proj(xs2, gmix, w_main, w_dt)
    ymix_s, s_lc, s_lh, s_sc, s_sh = _mixer_sample(
        proj_s, dseq, state_lru_conv[l], state_lru_h[l].reshape(bs, 1, LRU_WIDTH), state_ssd_conv[l],
        state_ssd_h[l].reshape(bs, SSD_WIDTH, D_STATE), params, sel_t, sel_p)
    y_sample = _out_mlp(xs2, ymix_s, w_out_b, gmlp, w_up_b, w_down_b, gfin)

    hshape = (N_SSD_HEADS, SSD_HEAD_DIM, D_STATE)
    return (
        y_prompt.reshape(bp, seq, D_MODEL), y_sample.reshape(bs, dseq, D_MODEL),
        p_lc[None], p_lh.reshape(1, bp, LRU_WIDTH), p_sc[None], p_sh.reshape(1, bp, *hshape),
        s_lc[None], s_lh.reshape(1, bs, LRU_WIDTH), s_sc[None], s_sh.reshape(1, bs, *hshape),
    )
```

```python
import functools
import math

import jax
import jax.numpy as jnp
from jax import lax
from jax.experimental import pallas as pl
from jax.experimental.pallas import tpu as pltpu

F32 = jnp.float32
BF16 = jnp.bfloat16

D_MODEL = 1024
LRU_WIDTH = 1024
N_LRU_HEADS = 16
LRU_BLOCK = 64
LRU_C = 8.0
SSD_WIDTH = 1024
SSD_HEAD_DIM = 64
N_SSD_HEADS = 16
N_SSD_GROUPS = 2
D_STATE = 128
CONV_WIDTH = 4
SSD_CONV_DIM = SSD_WIDTH + 2 * N_SSD_GROUPS * D_STATE
D_FF = 4 * D_MODEL
EPS = 1e-6

LANES = 128
SUBLANES = 8
MXU_DIM = 256
DT_PAD = LANES
PROJ_MAIN = 2 * LRU_WIDTH + SSD_WIDTH + SSD_CONV_DIM
PROJ_PAD = PROJ_MAIN + DT_PAD
MIX_WIDTH = LRU_WIDTH + SSD_WIDTH
SSD_CHUNK = 128
PROMPT_TC = 256
PROMPT_NB = 2
ROW_TILE = 512
SAMPLE_SEQS = SSD_CHUNK // SUBLANES
SCAN_RUN = PROMPT_TC // SUBLANES
SCAN_PITCH = SCAN_RUN + 4
NEG_BIG = -1e30
LOG2E = 1.4426950408889634
VMEM_LIMIT = 56 * 1024 * 1024
HI = lax.Precision.HIGHEST


def _rms(x, g):
    ms = jnp.mean(x * x, axis=-1, keepdims=True)
    return x * lax.rsqrt(ms + EPS) * g


def _sigmoid(x):
    return 1.0 / (1.0 + jnp.exp(-x))


def _softplus(x):
    return jnp.maximum(x, 0.0) + jnp.log1p(jnp.exp(-jnp.abs(x)))


def _gelu_tanh(x):
    c = math.sqrt(2.0 / math.pi)
    return 0.5 * x * (1.0 + jnp.tanh(c * (x + 0.044715 * (x * x * x))))


def _lru_coeffs(u, wg_ref, b_a, b_x, neg_c_sp):
    ub = u.astype(BF16)
    r_parts, i_parts = [], []
    for j in range(LRU_WIDTH // MXU_DIM):
        g = jnp.dot(ub[:, MXU_DIM * j:MXU_DIM * (j + 1)], wg_ref[j], preferred_element_type=F32)
        r_parts.append(g[:, :MXU_DIM])
        i_parts.append(g[:, MXU_DIM:])
    r = _sigmoid(jnp.concatenate(r_parts, axis=1) + b_a)
    i = _sigmoid(jnp.concatenate(i_parts, axis=1) + b_x)
    log_a = r * neg_c_sp
    a = jnp.exp(log_a)
    th = jnp.tanh(log_a)
    v = (th + th) / (th - 1.0)
    mult = jnp.where(v > 0.0, v * lax.rsqrt(v), 0.0)
    return a, mult * (i * u)


def _scan_within_8(a, b):
    ridx = lax.broadcasted_iota(jnp.int32, a.shape, 0) & (SUBLANES - 1)
    for k in (1, 2, 4):
        a_s = pltpu.roll(a, k, axis=0)
        b_s = pltpu.roll(b, k, axis=0)
        m = ridx >= k
        b = jnp.where(m, a * b_s + b, b)
        a = jnp.where(m, a * a_s, a)
    return a, b


def _conv_slabs(ext, w_ref, b_ref, rows, first):
    parts = []
    for s in range(ext.shape[0]):
        cols = slice(LANES * s, LANES * (s + 1))
        acc = b_ref[:, cols] + ext[s, pl.ds(first, rows), :] * w_ref[0:1, cols]
        for k in range(1, CONV_WIDTH):
            acc = acc + ext[s, pl.ds(first + k, rows), :] * w_ref[k:k + 1, cols]
        parts.append(acc)
    return jnp.concatenate(parts, axis=1)


def _lru_scan_strided(a, b, hcar, a_pad, b_pad, h_pad):
    rows = a.shape[0]
    S = rows // SUBLANES
    nslab = LRU_WIDTH // LANES
    ridx = lax.broadcasted_iota(jnp.int32, (SUBLANES, LANES), 0)
    step = lambda ref, s, i: ref[s, pl.ds(i, SUBLANES, stride=SCAN_PITCH), :]
    for s in range(nslab):
        cols = slice(LANES * s, LANES * (s + 1))
        for j in range(SUBLANES):
            a_pad[s, SCAN_PITCH * j:SCAN_PITCH * j + S, :] = a[S * j:S * (j + 1), cols]
            b_pad[s, SCAN_PITCH * j:SCAN_PITCH * j + S, :] = b[S * j:S * (j + 1), cols]
    h = [jnp.zeros((SUBLANES, LANES), F32)] * nslab
    prod = [jnp.ones((SUBLANES, LANES), F32)] * nslab
    for i in range(S):
        for s in range(nslab):
            av = step(a_pad, s, i)
            h[s] = av * h[s] + step(b_pad, s, i)
            prod[s] = av * prod[s]
    for s in range(nslab):
        cols = slice(LANES * s, LANES * (s + 1))
        pcum, hcum = _scan_within_8(prod[s], h[s])
        cin = hcar[:, cols]
        ends = hcum + pcum * cin
        h[s] = jnp.where(ridx == 0, cin, pltpu.roll(ends, 1, axis=0))
        hcar[:, cols] = jnp.broadcast_to(ends[SUBLANES - 1:SUBLANES, :], (SUBLANES, LANES))
    for i in range(S):
        for s in range(nslab):
            h[s] = step(a_pad, s, i) * h[s] + step(b_pad, s, i)
            h_pad[s, pl.ds(i, SUBLANES, stride=SCAN_PITCH), :] = h[s]
    return jnp.concatenate(
        [jnp.concatenate([h_pad[s, SCAN_PITCH * j:SCAN_PITCH * j + S, :] for j in range(SUBLANES)], axis=0)
         for s in range(nslab)], axis=1)


def _ssd_cumdecay(dt, a2_row, tri):
    cum2 = jnp.dot(tri, dt * a2_row, precision=HI, preferred_element_type=F32)
    return cum2, cum2.T[0:N_SSD_HEADS, :]


def _ssd_diag(xs, bm, cm, dt, cols, cum2_t, mask_add):
    L = xs.shape[0]
    c2_t = cum2_t - jnp.log2(dt.T[0:N_SSD_HEADS, :])
    lane = lax.broadcasted_iota(jnp.int32, (L, LANES), 1)
    lo = lane < SSD_HEAD_DIM
    y_parts = []
    for g in range(N_SSD_GROUPS):
        bg = bm[:, D_STATE * g:D_STATE * (g + 1)].astype(BF16)
        cg = cm[:, D_STATE * g:D_STATE * (g + 1)].astype(BF16)
        cb = lax.dot_general(cg, bg, (((1,), (1,)), ((), ())), preferred_element_type=F32)
        for jj in range(N_SSD_HEADS // N_SSD_GROUPS // 2):
            j = (N_SSD_HEADS // N_SSD_GROUPS // 2) * g + jj
            h0, h1 = 2 * j, 2 * j + 1
            col0 = cols[:, LANES * h0:LANES * (h0 + 1)]
            col1 = cols[:, LANES * h1:LANES * (h1 + 1)]
            m0 = cb * jnp.exp2(col0 - c2_t[h0:h0 + 1, :] + mask_add)
            m1 = cb * jnp.exp2(col1 - c2_t[h1:h1 + 1, :] + mask_add)
            lhs = jnp.concatenate([m0, m1], axis=1).astype(BF16)
            xp = xs[:, LANES * j:LANES * (j + 1)]
            rhs = jnp.concatenate([jnp.where(lo, xp, 0.0), jnp.where(lo, 0.0, xp)], axis=0).astype(BF16)
            y_parts.append(jnp.dot(lhs, rhs, preferred_element_type=F32))
    return jnp.concatenate(y_parts, axis=1)


def _spread(v, sel_ref):
    p0 = v.astype(BF16)
    p1 = (v - p0.astype(F32)).astype(BF16)
    return jnp.dot(jnp.concatenate([p0, p1], axis=1), sel_ref[...], preferred_element_type=F32)


def _ssd_gate_norm(ys, xs, z_act, dskip, g_ssd):
    ys = ys + dskip * xs
    gated = ys * z_act
    half = SSD_WIDTH // N_SSD_GROUPS
    outs = []
    for g in range(N_SSD_GROUPS):
        outs.append(_rms(gated[:, half * g:half * (g + 1)], g_ssd[:, half * g:half * (g + 1)]))
    return jnp.concatenate(outs, axis=1)


def _split_w_in_kernel(wt_ref, wdt_t_ref, main_ref, dt_ref):
    main_ref[...] = wt_ref[...].T.astype(BF16)

    @pl.when(pl.program_id(0) == 0)
    def _dt():
        dt_ref[...] = jnp.zeros_like(dt_ref)
        dt_ref[:, 0:N_SSD_HEADS] = wdt_t_ref[...].T.astype(BF16)


def _split_w_in(w_t):
    cols, rows = w_t.shape
    blk = MXU_DIM
    return pl.pallas_call(
        _split_w_in_kernel,
        out_shape=(jax.ShapeDtypeStruct((rows, PROJ_MAIN), BF16), jax.ShapeDtypeStruct((rows, DT_PAD), BF16)),
        grid=(PROJ_MAIN // blk,),
        in_specs=[pl.BlockSpec((blk, rows), lambda j: (j, 0)),
                  pl.BlockSpec((N_SSD_HEADS, rows), lambda j: (0, 0))],
        out_specs=(pl.BlockSpec((rows, blk), lambda j: (0, j)), pl.BlockSpec((rows, DT_PAD), lambda j: (0, 0))),
        compiler_params=pltpu.CompilerParams(dimension_semantics=("arbitrary",)),
        name="split_w_in",
    )(w_t, w_t[PROJ_MAIN:, :])


def _inproj_kernel(x_ref, g_ref, w_ref, wdt_ref, o_ref):
    hn = _rms(x_ref[...], g_ref[...]).astype(BF16)
    o_ref[:, 0:PROJ_MAIN] = jnp.dot(hn, w_ref[...], preferred_element_type=F32)
    o_ref[:, PROJ_MAIN:PROJ_PAD] = jnp.dot(hn, wdt_ref[...], preferred_element_type=F32)


def _in_proj(x2d, g_mix, w_main, w_dt):
    n = x2d.shape[0]
    return pl.pallas_call(
        _inproj_kernel,
        out_shape=jax.ShapeDtypeStruct((n, PROJ_PAD), F32),
        grid=(n // ROW_TILE,),
        in_specs=[
            pl.BlockSpec((ROW_TILE, D_MODEL), lambda i: (i, 0)),
            pl.BlockSpec((1, D_MODEL), lambda i: (0, 0)),
            pl.BlockSpec((D_MODEL, PROJ_MAIN), lambda i: (0, 0), pipeline_mode=pl.Buffered(1)),
            pl.BlockSpec((D_MODEL, DT_PAD), lambda i: (0, 0), pipeline_mode=pl.Buffered(1)),
        ],
        out_specs=pl.BlockSpec((ROW_TILE, PROJ_PAD), lambda i: (i, 0)),
        compiler_params=pltpu.CompilerParams(
            dimension_semantics=("parallel",), vmem_limit_bytes=VMEM_LIMIT),
        name="in_proj",
    )(x2d, g_mix, w_main, w_dt)


def _inproj_prompt_kernel(x_ref, g_ref, w_ref, wdt_ref, lcw_ref, lcb_ref, scw_ref, scb_ref, dtb_ref,
                          o_ref, olc_ref, osc_ref, ext_l, ext_s, *, steps_per_seq):
    t = lax.rem(pl.program_id(0), steps_per_seq)
    rows = ROW_TILE
    hist = SUBLANES
    o1, o2, o3 = LRU_WIDTH, 2 * LRU_WIDTH, 2 * LRU_WIDTH + SSD_WIDTH

    @pl.when(t == 0)
    def _init():
        ext_l[:, 0:hist, :] = jnp.zeros((ext_l.shape[0], hist, LANES), F32)
        ext_s[:, 0:hist, :] = jnp.zeros((ext_s.shape[0], hist, LANES), F32)

    hn = _rms(x_ref[...], g_ref[...]).astype(BF16)
    lx = jnp.dot(hn, w_ref[:, 0:o1], preferred_element_type=F32)
    for s in range(ext_l.shape[0]):
        ext_l[s, hist:hist + rows, :] = lx[:, LANES * s:LANES * (s + 1)]
    xbc_in = jnp.dot(hn, w_ref[:, o3:PROJ_MAIN], preferred_element_type=F32)
    for s in range(ext_s.shape[0]):
        ext_s[s, hist:hist + rows, :] = xbc_in[:, LANES * s:LANES * (s + 1)]
    o_ref[:, o1:o2] = _gelu_tanh(jnp.dot(hn, w_ref[:, o1:o2], preferred_element_type=F32))
    z = jnp.dot(hn, w_ref[:, o2:o3], preferred_element_type=F32)
    o_ref[:, o2:o3] = z * _sigmoid(z)
    o_ref[:, PROJ_MAIN:PROJ_PAD] = _softplus(
        jnp.dot(hn, wdt_ref[...], preferred_element_type=F32) + dtb_ref[...])
    o_ref[:, 0:o1] = _conv_slabs(ext_l, lcw_ref, lcb_ref, rows, hist - (CONV_WIDTH - 1))
    xbc = _conv_slabs(ext_s, scw_ref, scb_ref, rows, hist - (CONV_WIDTH - 1))
    o_ref[:, o3:PROJ_MAIN] = xbc * _sigmoid(xbc)

    @pl.when(t == steps_per_seq - 1)
    def _final():
        last = slice(hist + rows - (CONV_WIDTH - 1), hist + rows)
        for s in range(ext_l.shape[0]):
            olc_ref[:, LANES * s:LANES * (s + 1)] = ext_l[s, last, :]
        for s in range(ext_s.shape[0]):
            osc_ref[:, LANES * s:LANES * (s + 1)] = ext_s[s, last, :]

    tail_l = ext_l[:, rows:rows + hist, :]
    tail_s = ext_s[:, rows:rows + hist, :]
    ext_l[:, 0:hist, :] = tail_l
    ext_s[:, 0:hist, :] = tail_s


def _in_proj_prompt(x2d, bsz, g_mix, w_main, w_dt, lcw, lcb, scw, scb, dtb):
    n = x2d.shape[0]
    steps_per_seq = n // bsz // ROW_TILE
    const = lambda i: (0, 0)
    return pl.pallas_call(
        functools.partial(_inproj_prompt_kernel, steps_per_seq=steps_per_seq),
        out_shape=(
            jax.ShapeDtypeStruct((n, PROJ_PAD), F32),
            jax.ShapeDtypeStruct((bsz, CONV_WIDTH - 1, LRU_WIDTH), F32),
            jax.ShapeDtypeStruct((bsz, CONV_WIDTH - 1, SSD_CONV_DIM), F32),
        ),
        grid=(n // ROW_TILE,),
        in_specs=[
            pl.BlockSpec((ROW_TILE, D_MODEL), lambda i: (i, 0)),
            pl.BlockSpec((1, D_MODEL), const),
            pl.BlockSpec((D_MODEL, PROJ_MAIN), const, pipeline_mode=pl.Buffered(1)),
            pl.BlockSpec((D_MODEL, DT_PAD), const, pipeline_mode=pl.Buffered(1)),
            pl.BlockSpec((CONV_WIDTH, LRU_WIDTH), const),
            pl.BlockSpec((1, LRU_WIDTH), const),
            pl.BlockSpec((CONV_WIDTH, SSD_CONV_DIM), const),
            pl.BlockSpec((1, SSD_CONV_DIM), const),
            pl.BlockSpec((1, DT_PAD), const),
        ],
        out_specs=(
            pl.BlockSpec((ROW_TILE, PROJ_PAD), lambda i: (i, 0)),
            pl.BlockSpec((None, CONV_WIDTH - 1, LRU_WIDTH), lambda i: (i // steps_per_seq, 0, 0)),
            pl.BlockSpec((None, CONV_WIDTH - 1, SSD_CONV_DIM), lambda i: (i // steps_per_seq, 0, 0)),
        ),
        scratch_shapes=[
            pltpu.VMEM((LRU_WIDTH // LANES, SUBLANES + ROW_TILE, LANES), F32),
            pltpu.VMEM((SSD_CONV_DIM // LANES, SUBLANES + ROW_TILE, LANES), F32),
        ],
        compiler_params=pltpu.CompilerParams(
            dimension_semantics=("arbitrary",), vmem_limit_bytes=VMEM_LIMIT),
        name="in_proj_prompt",
    )(x2d, g_mix, w_main, w_dt, lcw, lcb, scw, scb, dtb)


def _outmlp_kernel(x_ref, y_ref, wo_ref, gm_ref, wu_ref, wd_ref, gf_ref, o_ref):
    x1 = x_ref[...] + jnp.dot(y_ref[...].astype(BF16), wo_ref[...], preferred_element_type=F32)
    m = _rms(x1, gm_ref[...]).astype(BF16)
    u = jnp.dot(m, wu_ref[...], preferred_element_type=F32)
    u = jnp.square(jnp.maximum(u, 0.0)).astype(BF16)
    x2 = x1 + jnp.dot(u, wd_ref[...], preferred_element_type=F32)
    o_ref[...] = _rms(x2, gf_ref[...])


def _out_mlp(x2d, ymix2d, w_out_b, g_mlp, w_up_b, w_down_b, g_final):
    n = x2d.shape[0]
    const = lambda i: (0, 0)
    return pl.pallas_call(
        _outmlp_kernel,
        out_shape=jax.ShapeDtypeStruct((n, D_MODEL), F32),
        grid=(n // ROW_TILE,),
        in_specs=[
            pl.BlockSpec((ROW_TILE, D_MODEL), lambda i: (i, 0)),
            pl.BlockSpec((ROW_TILE, MIX_WIDTH), lambda i: (i, 0)),
            pl.BlockSpec((MIX_WIDTH, D_MODEL), const, pipeline_mode=pl.Buffered(1)),
            pl.BlockSpec((1, D_MODEL), const),
            pl.BlockSpec((D_MODEL, D_FF), const, pipeline_mode=pl.Buffered(1)),
            pl.BlockSpec((D_FF, D_MODEL), const, pipeline_mode=pl.Buffered(1)),
            pl.BlockSpec((1, D_MODEL), const),
        ],
        out_specs=pl.BlockSpec((ROW_TILE, D_MODEL), lambda i: (i, 0)),
        compiler_params=pltpu.CompilerParams(
            dimension_semantics=("parallel",), vmem_limit_bytes=VMEM_LIMIT),
        name="out_mlp",
    )(x2d, ymix2d, w_out_b, g_mlp, w_up_b, w_down_b, g_final)


def _mixer_prompt_kernel(u_ref, gl_ref, zact_ref, xbc_ref, dt_ref,
                         wg_ref, ba_ref, bx_ref, lam_ref, glru_ref, alog_ref, dskip_ref, gssd_ref,
                         selt_ref, selp_ref,
                         y_ref, olh_ref, osh_ref,
                         a_pad, b_pad, h_pad, hcar, ht):
    t = pl.program_id(1)
    nt = pl.num_programs(1)
    tc = PROMPT_TC

    @pl.when(t == 0)
    def _init():
        hcar[...] = jnp.zeros_like(hcar)
        ht[...] = jnp.zeros_like(ht)

    neg_c_sp = (-LRU_C) * _softplus(-lam_ref[...])
    lane1 = lax.broadcasted_iota(jnp.int32, (1, LANES), 1)
    a2_row = jnp.where(lane1 < N_SSD_HEADS, -LOG2E * jnp.exp(alog_ref[...]), 0.0)
    L = SSD_CHUNK
    rr = lax.broadcasted_iota(jnp.int32, (L, L), 0)
    cc = lax.broadcasted_iota(jnp.int32, (L, L), 1)
    causal = cc <= rr
    tri = jnp.where(causal, 1.0, 0.0).astype(F32)
    mask_add = jnp.where(causal, 0.0, NEG_BIG).astype(F32)
    half = SSD_WIDTH // N_SSD_GROUPS

    chunks = [(n, c) for n in range(PROMPT_NB) for c in range(tc // L)]
    cums = [_ssd_cumdecay(dt_ref[n, L * c:L * (c + 1), :], a2_row, tri) for n, c in chunks]
    dts = [dt_ref[n, L * c:L * (c + 1), :] for n, c in chunks]
    cum_all = jnp.concatenate([cum2 for cum2, _ in cums], axis=0)
    cols_all = _spread(cum_all, selt_ref)
    ecol_all = _spread(jnp.exp2(cum_all), selp_ref)
    sdt_all = _spread(jnp.concatenate(
        [jnp.exp2(cum2[L - 1:L, :] - cum2) * dt for (cum2, _), dt in zip(cums, dts)], axis=0), selp_ref)

    for n in range(PROMPT_NB):
        a, b = _lru_coeffs(u_ref[n], wg_ref, ba_ref[...], bx_ref[...], neg_c_sp)
        hseq = _lru_scan_strided(a, b, hcar.at[n], a_pad.at[n], b_pad.at[n], h_pad.at[n])
        y_ref[n, :, 0:LRU_WIDTH] = _rms(hseq * gl_ref[n], glru_ref[...])

        for c in range(tc // L):
            k = chunks.index((n, c))
            rows = slice(L * c, L * (c + 1))
            krows = slice(L * k, L * (k + 1))
            xs = xbc_ref[n, rows, 0:SSD_WIDTH]
            bm = xbc_ref[n, rows, SSD_WIDTH:SSD_WIDTH + N_SSD_GROUPS * D_STATE]
            cm = xbc_ref[n, rows, SSD_WIDTH + N_SSD_GROUPS * D_STATE:SSD_CONV_DIM]
            y_diag = _ssd_diag(xs, bm, cm, dts[k], cols_all[krows, :], cums[k][1], mask_add)
            ecol = ecol_all[krows, :]
            xw = xs * sdt_all[krows, :]
            dec = ecol[L - 1:L, :]
            y_off_parts = []
            for g in range(N_SSD_GROUPS):
                htg = ht[n, g]
                cg = cm[:, D_STATE * g:D_STATE * (g + 1)].astype(BF16)
                y_off_parts.append(jnp.dot(cg, htg.astype(BF16), preferred_element_type=F32))
                bg_t = bm[:, D_STATE * g:D_STATE * (g + 1)].T.astype(BF16)
                st = jnp.dot(bg_t, xw[:, half * g:half * (g + 1)].astype(BF16), preferred_element_type=F32)
                ht[n, g] = htg * dec[:, half * g:half * (g + 1)] + st
            ys = y_diag + jnp.concatenate(y_off_parts, axis=1) * ecol
            y_ref[n, rows, LRU_WIDTH:MIX_WIDTH] = _ssd_gate_norm(
                ys, xs, zact_ref[n, rows, :], dskip_ref[...], gssd_ref[...])

    @pl.when(t == nt - 1)
    def _final():
        for n in range(PROMPT_NB):
            olh_ref[n] = hcar[n, 0:1, :]
            for g in range(N_SSD_GROUPS):
                osh_ref[n, half * g:half * (g + 1), :] = ht[n, g].T


def _param_specs(const):
    return [
        pl.BlockSpec((CONV_WIDTH, LRU_WIDTH), const),
        pl.BlockSpec((1, LRU_WIDTH), const),
        pl.BlockSpec((LRU_WIDTH // MXU_DIM, MXU_DIM, 2 * MXU_DIM), lambda *_: (0, 0, 0)),
        pl.BlockSpec((1, LRU_WIDTH), const),
        pl.BlockSpec((1, LRU_WIDTH), const),
        pl.BlockSpec((1, LRU_WIDTH), const),
        pl.BlockSpec((1, LRU_WIDTH), const),
        pl.BlockSpec((CONV_WIDTH, SSD_CONV_DIM), const),
        pl.BlockSpec((1, SSD_CONV_DIM), const),
        pl.BlockSpec((1, DT_PAD), const),
        pl.BlockSpec((1, DT_PAD), const),
        pl.BlockSpec((1, SSD_WIDTH), const),
        pl.BlockSpec((1, SSD_WIDTH), const),
    ]


def _head_selectors():
    k = jnp.arange(2 * LANES)[:, None] % LANES
    sel_t = (k == jnp.arange(N_SSD_HEADS * LANES)[None, :] // LANES).astype(BF16)
    sel_p = (k == jnp.arange(SSD_WIDTH)[None, :] // SSD_HEAD_DIM).astype(BF16)
    return sel_t, sel_p


def _mixer_prompt(act, wg, b_a, b_x, lam, g_lru, a_log, d_skip, g_ssd, sel_t, sel_p):
    bsz, seq, _ = act.shape
    tc = PROMPT_TC
    nb = PROMPT_NB
    const = lambda b, t: (0, 0)
    in_specs = [
        pl.BlockSpec((nb, tc, LRU_WIDTH), lambda b, t: (b, t, 0)),
        pl.BlockSpec((nb, tc, LRU_WIDTH), lambda b, t: (b, t, 1)),
        pl.BlockSpec((nb, tc, SSD_WIDTH), lambda b, t: (b, t, 2)),
        pl.BlockSpec((nb, tc, SSD_CONV_DIM), lambda b, t: (b, t, 2)),
        pl.BlockSpec((nb, tc, DT_PAD), lambda b, t: (b, t, PROJ_MAIN // DT_PAD)),
        pl.BlockSpec((LRU_WIDTH // MXU_DIM, MXU_DIM, 2 * MXU_DIM), lambda b, t: (0, 0, 0)),
        pl.BlockSpec((1, LRU_WIDTH), const),
        pl.BlockSpec((1, LRU_WIDTH), const),
        pl.BlockSpec((1, LRU_WIDTH), const),
        pl.BlockSpec((1, LRU_WIDTH), const),
        pl.BlockSpec((1, DT_PAD), const),
        pl.BlockSpec((1, SSD_WIDTH), const),
        pl.BlockSpec((1, SSD_WIDTH), const),
        pl.BlockSpec((2 * LANES, N_SSD_HEADS * LANES), const),
        pl.BlockSpec((2 * LANES, SSD_WIDTH), const),
    ]
    out_shape = (
        jax.ShapeDtypeStruct((bsz, seq, MIX_WIDTH), F32),
        jax.ShapeDtypeStruct((bsz, 1, LRU_WIDTH), F32),
        jax.ShapeDtypeStruct((bsz, SSD_WIDTH, D_STATE), F32),
    )
    out_specs = (
        pl.BlockSpec((nb, tc, MIX_WIDTH), lambda b, t: (b, t, 0)),
        pl.BlockSpec((nb, 1, LRU_WIDTH), lambda b, t: (b, 0, 0)),
        pl.BlockSpec((nb, SSD_WIDTH, D_STATE), lambda b, t: (b, 0, 0)),
    )
    scratch = [
        pltpu.VMEM((nb, LRU_WIDTH // LANES, SUBLANES * SCAN_PITCH, LANES), F32),
        pltpu.VMEM((nb, LRU_WIDTH // LANES, SUBLANES * SCAN_PITCH, LANES), F32),
        pltpu.VMEM((nb, LRU_WIDTH // LANES, SUBLANES * SCAN_PITCH, LANES), F32),
        pltpu.VMEM((nb, SUBLANES, LRU_WIDTH), F32),
        pltpu.VMEM((nb, N_SSD_GROUPS, D_STATE, SSD_WIDTH // N_SSD_GROUPS), F32),
    ]
    return pl.pallas_call(
        _mixer_prompt_kernel,
        out_shape=out_shape,
        grid=(bsz // nb, seq // tc),
        in_specs=in_specs,
        out_specs=out_specs,
        scratch_shapes=scratch,
        compiler_params=pltpu.CompilerParams(
            dimension_semantics=("parallel", "arbitrary"), vmem_limit_bytes=VMEM_LIMIT),
        name="mixer_prompt",
    )(act, act, act, act, act, wg, b_a, b_x, lam, g_lru, a_log, d_skip, g_ssd, sel_t, sel_p)


def _mixer_sample_kernel(lx_ref, gate_ref, z_ref, xbc_ref, dt_ref,
                         slc_ref, slh_ref, ssc_ref, ssh_ref,
                         lcw_ref, lcb_ref, wg_ref, ba_ref, bx_ref, lam_ref, glru_ref,
                         scw_ref, scb_ref, dtb_ref, alog_ref, dskip_ref, gssd_ref, selt_ref, selp_ref,
                         y_ref, olc_ref, olh_ref, osc_ref, osh_ref,
                         ext_l, ext_s, pad_scr, yoff_scr, *, T):
    S = SAMPLE_SEQS
    P = SUBLANES
    K1 = CONV_WIDTH - 1
    R = S * P
    row_i = lax.broadcasted_iota(jnp.int32, (R, 1), 0) & (P - 1)
    valid = row_i < T

    def pad_rows(ref):
        width = ref.shape[-1]
        pad_scr[:, :, 0:width] = jnp.zeros((S, P, width), F32)
        pad_scr[:, 0:T, 0:width] = ref[...].reshape(S, T, width)
        return pad_scr[:, :, 0:width].reshape(R, width)

    ext_l[...] = jnp.zeros_like(ext_l)
    ext_s[...] = jnp.zeros_like(ext_s)
    ext_l[:, 0:K1, :] = slc_ref[...]
    ext_l[:, K1:K1 + T, :] = lx_ref[...].reshape(S, T, LRU_WIDTH)
    ext_s[:, 0:K1, :] = ssc_ref[...]
    ext_s[:, K1:K1 + T, :] = xbc_ref[...].reshape(S, T, SSD_CONV_DIM)
    olc_ref[...] = ext_l[:, T:T + K1, :]
    osc_ref[...] = ext_s[:, T:T + K1, :]

    el = ext_l[...].reshape(R, LRU_WIDTH)
    es = ext_s[...].reshape(R, SSD_CONV_DIM)

    def conv(e, w_ref, b_ref):
        out = b_ref[...] + e * w_ref[0:1, :]
        for k in range(1, CONV_WIDTH):
            out = out + pltpu.roll(e, R - k, axis=0) * w_ref[k:k + 1, :]
        return out

    u = conv(el, lcw_ref, lcb_ref)
    neg_c_sp = (-LRU_C) * _softplus(-lam_ref[...])
    a, b = _lru_coeffs(u, wg_ref, ba_ref[...], bx_ref[...], neg_c_sp)
    a, b = _scan_within_8(a, b)
    h0 = jnp.broadcast_to(slh_ref[...], (S, P, LRU_WIDTH)).reshape(R, LRU_WIDTH)
    hseq = a * h0 + b
    olh_ref[...] = hseq.reshape(S, P, LRU_WIDTH)[:, T - 1:T, :]
    gate = pad_rows(gate_ref)
    y_lru = _rms(hseq * _gelu_tanh(gate), glru_ref[...])

    xbc = conv(es, scw_ref, scb_ref)
    xbc = xbc * _sigmoid(xbc)
    xs = xbc[:, 0:SSD_WIDTH]
    bm = xbc[:, SSD_WIDTH:SSD_WIDTH + N_SSD_GROUPS * D_STATE]
    cm = xbc[:, SSD_WIDTH + N_SSD_GROUPS * D_STATE:]
    dt_raw = pad_rows(dt_ref)
    dt = jnp.where(valid, _softplus(dt_raw + dtb_ref[...]), 0.0)
    lane1 = lax.broadcasted_iota(jnp.int32, (1, LANES), 1)
    a2_row = jnp.where(lane1 < N_SSD_HEADS, -LOG2E * jnp.exp(alog_ref[...]), 0.0)

    rr = lax.broadcasted_iota(jnp.int32, (R, R), 0)
    cc = lax.broadcasted_iota(jnp.int32, (R, R), 1)
    allowed = (cc <= rr) & ((rr - cc) <= (rr & (P - 1)))
    tri = jnp.where(allowed, 1.0, 0.0).astype(F32)
    mask_add = jnp.where(allowed, 0.0, NEG_BIG).astype(F32)

    cum2, cum2_t = _ssd_cumdecay(dt, a2_row, tri)
    y_diag = _ssd_diag(xs, bm, cm, dt, _spread(cum2, selt_ref), cum2_t, mask_add)
    ecol = _spread(jnp.exp2(cum2), selp_ref)
    end2 = jnp.broadcast_to(cum2.reshape(S, P, LANES)[:, P - 1:P, :], (S, P, LANES)).reshape(R, LANES)
    xw = xs * _spread(jnp.exp2(end2 - cum2) * dt, selp_ref)
    ecum_t = jnp.exp2(cum2_t)

    half = SSD_WIDTH // N_SSD_GROUPS
    for q in range(S):
        r0 = P * q
        vq = jnp.broadcast_to(ecum_t[:, r0 + P - 1:r0 + P], (N_SSD_HEADS, LANES))
        for g in range(N_SSD_GROUPS):
            hqg = ssh_ref[q, half * g:half * (g + 1), :]
            cq = cm[r0:r0 + P, D_STATE * g:D_STATE * (g + 1)].astype(BF16)
            yoff_scr[r0:r0 + P, half * g:half * (g + 1)] = lax.dot_general(
                cq, hqg.astype(BF16), (((1,), (1,)), ((), ())), preferred_element_type=F32)
            bq = bm[r0:r0 + P, D_STATE * g:D_STATE * (g + 1)].astype(BF16)
            xq = xw[r0:r0 + P, half * g:half * (g + 1)].astype(BF16)
            st = lax.dot_general(xq, bq, (((0,), (0,)), ((), ())), preferred_element_type=F32)
            for e in range(N_SSD_HEADS // N_SSD_GROUPS):
                h = (N_SSD_HEADS // N_SSD_GROUPS) * g + e
                lo_r = SSD_HEAD_DIM * e
                osh_ref[q, SSD_HEAD_DIM * h:SSD_HEAD_DIM * (h + 1), :] = (
                    vq[h:h + 1, :] * hqg[lo_r:lo_r + SSD_HEAD_DIM, :] + st[lo_r:lo_r + SSD_HEAD_DIM, :])

    ys = y_diag + yoff_scr[...] * ecol
    z = pad_rows(z_ref)
    y_ssd = _ssd_gate_norm(ys, xs, z * _sigmoid(z), dskip_ref[...], gssd_ref[...])
    y_ref[:, 0:LRU_WIDTH] = y_lru.reshape(S, P, LRU_WIDTH)[:, 0:T, :].reshape(S * T, LRU_WIDTH)
    y_ref[:, LRU_WIDTH:MIX_WIDTH] = y_ssd.reshape(S, P, SSD_WIDTH)[:, 0:T, :].reshape(S * T, SSD_WIDTH)


def _mixer_sample(proj, T, st_lc, st_lh, st_sc, st_sh, params, sel_t, sel_p):
    nseq = proj.shape[0] // T
    S = SAMPLE_SEQS
    const = lambda i: (0, 0)
    in_specs = [
        pl.BlockSpec((S * T, LRU_WIDTH), lambda i: (i, 0)),
        pl.BlockSpec((S * T, LRU_WIDTH), lambda i: (i, 1)),
        pl.BlockSpec((S * T, SSD_WIDTH), lambda i: (i, 2)),
        pl.BlockSpec((S * T, SSD_CONV_DIM), lambda i: (i, 2)),
        pl.BlockSpec((S * T, DT_PAD), lambda i: (i, PROJ_MAIN // DT_PAD)),
        pl.BlockSpec((S, CONV_WIDTH - 1, LRU_WIDTH), lambda i: (i, 0, 0)),
        pl.BlockSpec((S, 1, LRU_WIDTH), lambda i: (i, 0, 0)),
        pl.BlockSpec((S, CONV_WIDTH - 1, SSD_CONV_DIM), lambda i: (i, 0, 0)),
        pl.BlockSpec((S, SSD_WIDTH, D_STATE), lambda i: (i, 0, 0)),
    ] + _param_specs(const) + [
        pl.BlockSpec((2 * LANES, N_SSD_HEADS * LANES), const),
        pl.BlockSpec((2 * LANES, SSD_WIDTH), const),
    ]
    out_shape = (
        jax.ShapeDtypeStruct((nseq * T, MIX_WIDTH), F32),
        jax.ShapeDtypeStruct((nseq, CONV_WIDTH - 1, LRU_WIDTH), F32),
        jax.ShapeDtypeStruct((nseq, 1, LRU_WIDTH), F32),
        jax.ShapeDtypeStruct((nseq, CONV_WIDTH - 1, SSD_CONV_DIM), F32),
        jax.ShapeDtypeStruct((nseq, SSD_WIDTH, D_STATE), F32),
    )
    out_specs = (
        pl.BlockSpec((S * T, MIX_WIDTH), lambda i: (i, 0)),
        pl.BlockSpec((S, CONV_WIDTH - 1, LRU_WIDTH), lambda i: (i, 0, 0)),
        pl.BlockSpec((S, 1, LRU_WIDTH), lambda i: (i, 0, 0)),
        pl.BlockSpec((S, CONV_WIDTH - 1, SSD_CONV_DIM), lambda i: (i, 0, 0)),
        pl.BlockSpec((S, SSD_WIDTH, D_STATE), lambda i: (i, 0, 0)),
    )
    scratch = [
        pltpu.VMEM((S, SUBLANES, LRU_WIDTH), F32),
        pltpu.VMEM((S, SUBLANES, SSD_CONV_DIM), F32),
        pltpu.VMEM((S, SUBLANES, LRU_WIDTH), F32),
        pltpu.VMEM((S * SUBLANES, SSD_WIDTH), F32),
    ]
    return pl.pallas_call(
        functools.partial(_mixer_sample_kernel, T=T),
        out_shape=out_shape,
        grid=(nseq // S,),
        in_specs=in_specs,
        out_specs=out_specs,
        scratch_shapes=scratch,
        compiler_params=pltpu.CompilerParams(
            dimension_semantics=("parallel",), vmem_limit_bytes=VMEM_LIMIT),
        name="mixer_sample",
    )(proj, proj, proj, proj, proj, st_lc, st_lh, st_sc, st_sh, *params, sel_t, sel_p)


def _gate_weights(w_a, w_x):
    def tiles(w):
        per = MXU_DIM // LRU_BLOCK
        w4 = w.reshape(N_LRU_HEADS // per, per, LRU_BLOCK, LRU_BLOCK)
        eye = jnp.eye(per, dtype=w.dtype)
        t = jnp.einsum('jaik,ab->jaibk', w4, eye)
        return t.reshape(N_LRU_HEADS // per, MXU_DIM, MXU_DIM)
    return jnp.concatenate([tiles(w_a), tiles(w_x)], axis=2).astype(BF16)


def kernel(x_prompt, x_sample, state_lru_conv, state_lru_h, state_ssd_conv, state_ssd_h, g_mix, w_in,
           lru_conv_w, lru_conv_b, w_a, b_a, w_x, b_x, lam, g_lru_out, ssd_conv_w, ssd_conv_b, dt_bias,
           a_log, d_skip, g_ssd_out, w_out, g_mlp, w_up, w_down, g_final):
    depth = w_in.shape[0]
    assert depth == 1
    bp, seq, _ = x_prompt.shape
    bs, dseq, _ = x_sample.shape
    l = 0
    row = lambda v: v.reshape(1, -1)
    w_main, w_dt = _split_w_in(jnp.swapaxes(w_in, 1, 2)[l])
    params = (
        lru_conv_w[l], row(lru_conv_b[l]), _gate_weights(w_a[l], w_x[l]),
        row(b_a[l]), row(b_x[l]), row(lam[l]), row(g_lru_out[l]),
        ssd_conv_w[l], row(ssd_conv_b[l]),
        jnp.pad(row(dt_bias[l]), ((0, 0), (0, DT_PAD - N_SSD_HEADS))),
        jnp.pad(row(a_log[l]), ((0, 0), (0, DT_PAD - N_SSD_HEADS))),
        row(jnp.repeat(d_skip[l], SSD_HEAD_DIM)), row(g_ssd_out[l]),
    )
    w_out_b = w_out[l].astype(BF16)
    w_up_b = w_up[l].astype(BF16)
    w_down_b = w_down[l].astype(BF16)
    gmix = row(g_mix[l])
    gmlp = row(g_mlp[l])
    gfin = row(g_final)

    xp2 = x_prompt.reshape(bp * seq, D_MODEL)
    (lcw, lcb, wg, ba, bx, lam_r, glru, scw, scb, dtb, alog, dskip, gssd) = params
    act_p, p_lc, p_sc = _in_proj_prompt(xp2, bp, gmix, w_main, w_dt, lcw, lcb, scw, scb, dtb)
    sel_t, sel_p = _head_selectors()
    ymix_p, p_lh, p_sh = _mixer_prompt(
        act_p.reshape(bp, seq, PROJ_PAD), wg, ba, bx, lam_r, glru, alog, dskip, gssd, sel_t, sel_p)
    y_prompt = _out_mlp(xp2, ymix_p.reshape(bp * seq, MIX_WIDTH), w_out_b, gmlp, w_up_b, w_down_b, gfin)

    xs2 = x_sample.reshape(bs * dseq, D_MODEL)
    proj_s = _in_proj(xs2, gmix, w_main, w_dt)
    ymix_s, s_lc, s_lh, s_sc, s_sh = _mixer_sample(
        proj_s, dseq, state_lru_conv[l], state_lru_h[l].reshape(bs, 1, LRU_WIDTH), state_ssd_conv[l],
        state_ssd_h[l].reshape(bs, SSD_WIDTH, D_STATE), params, sel_t, sel_p)
    y_sample = _out_mlp(xs2, ymix_s, w_out_b, gmlp, w_up_b, w_down_b, gfin)

    hshape = (N_SSD_HEADS, SSD_HEAD_DIM, D_STATE)
    return (
        y_prompt.reshape(bp, seq, D_MODEL), y_sample.reshape(bs, dseq, D_MODEL),
        p_lc[None], p_lh.reshape(1, bp, LRU_WIDTH), p_sc[None], p_sh.reshape(1, bp, *hshape),
        s_lc[None], s_lh.reshape(1, bs, LRU_WIDTH), s_sc[None], s_sh.reshape(1, bs, *hshape),
    )
```

```python
import functools
import math

import jax
import jax.numpy as jnp
from jax import lax
from jax.experimental import pallas as pl
from jax.experimental.pallas import tpu as pltpu

F32 = jnp.float32
BF16 = jnp.bfloat16

D_MODEL = 1024
LRU_WIDTH = 1024
N_LRU_HEADS = 16
LRU_BLOCK = 64
LRU_C = 8.0
SSD_WIDTH = 1024
SSD_HEAD_DIM = 64
N_SSD_HEADS = 16
N_SSD_GROUPS = 2
D_STATE = 128
CONV_WIDTH = 4
SSD_CONV_DIM = SSD_WIDTH + 2 * N_SSD_GROUPS * D_STATE
D_FF = 4 * D_MODEL
EPS = 1e-6

LANES = 128
SUBLANES = 8
MXU_DIM = 256
DT_PAD = LANES
PROJ_MAIN = 2 * LRU_WIDTH + SSD_WIDTH + SSD_CONV_DIM
PROJ_PAD = PROJ_MAIN + DT_PAD
MIX_WIDTH = LRU_WIDTH + SSD_WIDTH
SSD_CHUNK = 128
PROMPT_TC = 256
PROMPT_NB = 2
ROW_TILE = 512
SPLIT_COLS = 1536
SAMPLE_SEQS = SSD_CHUNK // SUBLANES
SCAN_RUN = PROMPT_TC // SUBLANES
SCAN_PITCH = SCAN_RUN + 4
NEG_BIG = -1e30
LOG2E = 1.4426950408889634
VMEM_LIMIT = 56 * 1024 * 1024
HI = lax.Precision.HIGHEST


def _rms(x, g):
    ms = jnp.mean(x * x, axis=-1, keepdims=True)
    return x * lax.rsqrt(ms + EPS) * g


def _sigmoid(x):
    return 1.0 / (1.0 + jnp.exp(-x))


def _softplus(x):
    return jnp.maximum(x, 0.0) + jnp.log1p(jnp.exp(-jnp.abs(x)))


def _gelu_tanh(x):
    c = math.sqrt(2.0 / math.pi)
    return 0.5 * x * (1.0 + jnp.tanh(c * (x + 0.044715 * (x * x * x))))


def _lru_coeffs(u, wg_ref, b_a, b_x, neg_c_sp):
    ub = u.astype(BF16)
    r_parts, i_parts = [], []
    for j in range(LRU_WIDTH // MXU_DIM):
        g = jnp.dot(ub[:, MXU_DIM * j:MXU_DIM * (j + 1)], wg_ref[j], preferred_element_type=F32)
        r_parts.append(g[:, :MXU_DIM])
        i_parts.append(g[:, MXU_DIM:])
    r = _sigmoid(jnp.concatenate(r_parts, axis=1) + b_a)
    i = _sigmoid(jnp.concatenate(i_parts, axis=1) + b_x)
    log_a = r * neg_c_sp
    a = jnp.exp(log_a)
    th = jnp.tanh(log_a)
    v = (th + th) / (th - 1.0)
    mult = jnp.where(v > 0.0, v * lax.rsqrt(v), 0.0)
    return a, mult * (i * u)


def _scan_within_8(a, b):
    ridx = lax.broadcasted_iota(jnp.int32, a.shape, 0) & (SUBLANES - 1)
    for k in (1, 2, 4):
        a_s = pltpu.roll(a, k, axis=0)
        b_s = pltpu.roll(b, k, axis=0)
        m = ridx >= k
        b = jnp.where(m, a * b_s + b, b)
        a = jnp.where(m, a * a_s, a)
    return a, b


def _conv_slabs(ext, w_ref, b_ref, rows, first):
    parts = []
    for s in range(ext.shape[0]):
        cols = slice(LANES * s, LANES * (s + 1))
        acc = b_ref[:, cols] + ext[s, pl.ds(first, rows), :] * w_ref[0:1, cols]
        for k in range(1, CONV_WIDTH):
            acc = acc + ext[s, pl.ds(first + k, rows), :] * w_ref[k:k + 1, cols]
        parts.append(acc)
    return jnp.concatenate(parts, axis=1)


def _lru_scan_strided(a, b, hcar, a_pad, b_pad, h_pad):
    rows = a.shape[0]
    S = rows // SUBLANES
    nslab = LRU_WIDTH // LANES
    ridx = lax.broadcasted_iota(jnp.int32, (SUBLANES, LANES), 0)
    step = lambda ref, s, i: ref[s, pl.ds(i, SUBLANES, stride=SCAN_PITCH), :]
    for s in range(nslab):
        cols = slice(LANES * s, LANES * (s + 1))
        for j in range(SUBLANES):
            a_pad[s, SCAN_PITCH * j:SCAN_PITCH * j + S, :] = a[S * j:S * (j + 1), cols]
            b_pad[s, SCAN_PITCH * j:SCAN_PITCH * j + S, :] = b[S * j:S * (j + 1), cols]
    h = [jnp.zeros((SUBLANES, LANES), F32)] * nslab
    prod = [jnp.ones((SUBLANES, LANES), F32)] * nslab
    for i in range(S):
        for s in range(nslab):
            av = step(a_pad, s, i)
            h[s] = av * h[s] + step(b_pad, s, i)
            prod[s] = av * prod[s]
    for s in range(nslab):
        cols = slice(LANES * s, LANES * (s + 1))
        pcum, hcum = _scan_within_8(prod[s], h[s])
        cin = hcar[:, cols]
        ends = hcum + pcum * cin
        h[s] = jnp.where(ridx == 0, cin, pltpu.roll(ends, 1, axis=0))
        hcar[:, cols] = jnp.broadcast_to(ends[SUBLANES - 1:SUBLANES, :], (SUBLANES, LANES))
    for i in range(S):
        for s in range(nslab):
            h[s] = step(a_pad, s, i) * h[s] + step(b_pad, s, i)
            h_pad[s, pl.ds(i, SUBLANES, stride=SCAN_PITCH), :] = h[s]
    return jnp.concatenate(
        [jnp.concatenate([h_pad[s, SCAN_PITCH * j:SCAN_PITCH * j + S, :] for j in range(SUBLANES)], axis=0)
         for s in range(nslab)], axis=1)


def _ssd_cumdecay(dt, a2_row, tri):
    cum2 = jnp.dot(tri, dt * a2_row, precision=HI, preferred_element_type=F32)
    return cum2, cum2.T[0:N_SSD_HEADS, :]


def _ssd_diag(xs, bm, cm, dt, cols, cum2_t, mask_add):
    L = xs.shape[0]
    c2_t = cum2_t - jnp.log2(dt.T[0:N_SSD_HEADS, :])
    lane = lax.broadcasted_iota(jnp.int32, (L, LANES), 1)
    lo = lane < SSD_HEAD_DIM
    y_parts = []
    for g in range(N_SSD_GROUPS):
        bg = bm[:, D_STATE * g:D_STATE * (g + 1)].astype(BF16)
        cg = cm[:, D_STATE * g:D_STATE * (g + 1)].astype(BF16)
        cb = lax.dot_general(cg, bg, (((1,), (1,)), ((), ())), preferred_element_type=F32)
        for jj in range(N_SSD_HEADS // N_SSD_GROUPS // 2):
            j = (N_SSD_HEADS // N_SSD_GROUPS // 2) * g + jj
            h0, h1 = 2 * j, 2 * j + 1
            col0 = cols[:, LANES * h0:LANES * (h0 + 1)]
            col1 = cols[:, LANES * h1:LANES * (h1 + 1)]
            m0 = cb * jnp.exp2(col0 - c2_t[h0:h0 + 1, :] + mask_add)
            m1 = cb * jnp.exp2(col1 - c2_t[h1:h1 + 1, :] + mask_add)
            lhs = jnp.concatenate([m0, m1], axis=1).astype(BF16)
            xp = xs[:, LANES * j:LANES * (j + 1)]
            rhs = jnp.concatenate([jnp.where(lo, xp, 0.0), jnp.where(lo, 0.0, xp)], axis=0).astype(BF16)
            y_parts.append(jnp.dot(lhs, rhs, preferred_element_type=F32))
    return jnp.concatenate(y_parts, axis=1)


def _spread(v, sel_ref):
    p0 = v.astype(BF16)
    p1 = (v - p0.astype(F32)).astype(BF16)
    return jnp.dot(jnp.concatenate([p0, p1], axis=1), sel_ref[...], preferred_element_type=F32)


def _ssd_gate_norm(ys, xs, z_act, dskip, g_ssd):
    ys = ys + dskip * xs
    gated = ys * z_act
    half = SSD_WIDTH // N_SSD_GROUPS
    outs = []
    for g in range(N_SSD_GROUPS):
        outs.append(_rms(gated[:, half * g:half * (g + 1)], g_ssd[:, half * g:half * (g + 1)]))
    return jnp.concatenate(outs, axis=1)


def _split_w_in_kernel(wt_ref, wdt_t_ref, main_ref, dt_ref):
    main_ref[...] = wt_ref[...].T.astype(BF16)

    @pl.when(pl.program_id(0) == 0)
    def _dt():
        dt_ref[...] = jnp.zeros_like(dt_ref)
        dt_ref[:, 0:N_SSD_HEADS] = wdt_t_ref[...].T.astype(BF16)


def _split_w_in(w_t):
    cols, rows = w_t.shape
    blk = SPLIT_COLS
    return pl.pallas_call(
        _split_w_in_kernel,
        out_shape=(jax.ShapeDtypeStruct((rows, PROJ_MAIN), BF16), jax.ShapeDtypeStruct((rows, DT_PAD), BF16)),
        grid=(PROJ_MAIN // blk,),
        in_specs=[pl.BlockSpec((blk, rows), lambda j: (j, 0)),
                  pl.BlockSpec((N_SSD_HEADS, rows), lambda j: (0, 0))],
        out_specs=(pl.BlockSpec((rows, blk), lambda j: (0, j)), pl.BlockSpec((rows, DT_PAD), lambda j: (0, 0))),
        compiler_params=pltpu.CompilerParams(dimension_semantics=("arbitrary",)),
        name="split_w_in",
    )(w_t, w_t[PROJ_MAIN:, :])


def _inproj_kernel(x_ref, g_ref, w_ref, wdt_ref, o_ref):
    hn = _rms(x_ref[...], g_ref[...]).astype(BF16)
    o_ref[:, 0:PROJ_MAIN] = jnp.dot(hn, w_ref[...], preferred_element_type=F32)
    o_ref[:, PROJ_MAIN:PROJ_PAD] = jnp.dot(hn, wdt_ref[...], preferred_element_type=F32)


def _in_proj(x2d, g_mix, w_main, w_dt):
    n = x2d.shape[0]
    return pl.pallas_call(
        _inproj_kernel,
        out_shape=jax.ShapeDtypeStruct((n, PROJ_PAD), F32),
        grid=(n // ROW_TILE,),
        in_specs=[
            pl.BlockSpec((ROW_TILE, D_MODEL), lambda i: (i, 0)),
            pl.BlockSpec((1, D_MODEL), lambda i: (0, 0)),
            pl.BlockSpec((D_MODEL, PROJ_MAIN), lambda i: (0, 0), pipeline_mode=pl.Buffered(1)),
            pl.BlockSpec((D_MODEL, DT_PAD), lambda i: (0, 0), pipeline_mode=pl.Buffered(1)),
        ],
        out_specs=pl.BlockSpec((ROW_TILE, PROJ_PAD), lambda i: (i, 0)),
        compiler_params=pltpu.CompilerParams(
            dimension_semantics=("parallel",), vmem_limit_bytes=VMEM_LIMIT),
        name="in_proj",
    )(x2d, g_mix, w_main, w_dt)


def _inproj_prompt_kernel(x_ref, g_ref, w_ref, wdt_ref, lcw_ref, lcb_ref, scw_ref, scb_ref, dtb_ref,
                          o_ref, olc_ref, osc_ref, ext_l, ext_s, *, steps_per_seq):
    t = lax.rem(pl.program_id(0), steps_per_seq)
    rows = ROW_TILE
    hist = SUBLANES
    o1, o2, o3 = LRU_WIDTH, 2 * LRU_WIDTH, 2 * LRU_WIDTH + SSD_WIDTH

    @pl.when(t == 0)
    def _init():
        ext_l[:, 0:hist, :] = jnp.zeros((ext_l.shape[0], hist, LANES), F32)
        ext_s[:, 0:hist, :] = jnp.zeros((ext_s.shape[0], hist, LANES), F32)

    hn = _rms(x_ref[...], g_ref[...]).astype(BF16)
    lx = jnp.dot(hn, w_ref[:, 0:o1], preferred_element_type=F32)
    for s in range(ext_l.shape[0]):
        ext_l[s, hist:hist + rows, :] = lx[:, LANES * s:LANES * (s + 1)]
    xbc_in = jnp.dot(hn, w_ref[:, o3:PROJ_MAIN], preferred_element_type=F32)
    for s in range(ext_s.shape[0]):
        ext_s[s, hist:hist + rows, :] = xbc_in[:, LANES * s:LANES * (s + 1)]
    o_ref[:, o1:o2] = _gelu_tanh(jnp.dot(hn, w_ref[:, o1:o2], preferred_element_type=F32))
    z = jnp.dot(hn, w_ref[:, o2:o3], preferred_element_type=F32)
    o_ref[:, o2:o3] = z * _sigmoid(z)
    o_ref[:, PROJ_MAIN:PROJ_PAD] = _softplus(
        jnp.dot(hn, wdt_ref[...], preferred_element_type=F32) + dtb_ref[...])
    o_ref[:, 0:o1] = _conv_slabs(ext_l, lcw_ref, lcb_ref, rows, hist - (CONV_WIDTH - 1))
    xbc = _conv_slabs(ext_s, scw_ref, scb_ref, rows, hist - (CONV_WIDTH - 1))
    o_ref[:, o3:PROJ_MAIN] = xbc * _sigmoid(xbc)

    @pl.when(t == steps_per_seq - 1)
    def _final():
        last = slice(hist + rows - (CONV_WIDTH - 1), hist + rows)
        for s in range(ext_l.shape[0]):
            olc_ref[:, LANES * s:LANES * (s + 1)] = ext_l[s, last, :]
        for s in range(ext_s.shape[0]):
            osc_ref[:, LANES * s:LANES * (s + 1)] = ext_s[s, last, :]

    tail_l = ext_l[:, rows:rows + hist, :]
    tail_s = ext_s[:, rows:rows + hist, :]
    ext_l[:, 0:hist, :] = tail_l
    ext_s[:, 0:hist, :] = tail_s


def _in_proj_prompt(x2d, bsz, g_mix, w_main, w_dt, lcw, lcb, scw, scb, dtb):
    n = x2d.shape[0]
    steps_per_seq = n // bsz // ROW_TILE
    const = lambda i: (0, 0)
    return pl.pallas_call(
        functools.partial(_inproj_prompt_kernel, steps_per_seq=steps_per_seq),
        out_shape=(
            jax.ShapeDtypeStruct((n, PROJ_PAD), F32),
            jax.ShapeDtypeStruct((bsz, CONV_WIDTH - 1, LRU_WIDTH), F32),
            jax.ShapeDtypeStruct((bsz, CONV_WIDTH - 1, SSD_CONV_DIM), F32),
        ),
        grid=(n // ROW_TILE,),
        in_specs=[
            pl.BlockSpec((ROW_TILE, D_MODEL), lambda i: (i, 0)),
            pl.BlockSpec((1, D_MODEL), const),
            pl.BlockSpec((D_MODEL, PROJ_MAIN), const, pipeline_mode=pl.Buffered(1)),
            pl.BlockSpec((D_MODEL, DT_PAD), const, pipeline_mode=pl.Buffered(1)),
            pl.BlockSpec((CONV_WIDTH, LRU_WIDTH), const),
            pl.BlockSpec((1, LRU_WIDTH), const),
            pl.BlockSpec((CONV_WIDTH, SSD_CONV_DIM), const),
            pl.BlockSpec((1, SSD_CONV_DIM), const),
            pl.BlockSpec((1, DT_PAD), const),
        ],
        out_specs=(
            pl.BlockSpec((ROW_TILE, PROJ_PAD), lambda i: (i, 0)),
            pl.BlockSpec((None, CONV_WIDTH - 1, LRU_WIDTH), lambda i: (i // steps_per_seq, 0, 0)),
            pl.BlockSpec((None, CONV_WIDTH - 1, SSD_CONV_DIM), lambda i: (i // steps_per_seq, 0, 0)),
        ),
        scratch_shapes=[
            pltpu.VMEM((LRU_WIDTH // LANES, SUBLANES + ROW_TILE, LANES), F32),
            pltpu.VMEM((SSD_CONV_DIM // LANES, SUBLANES + ROW_TILE, LANES), F32),
        ],
        compiler_params=pltpu.CompilerParams(
            dimension_semantics=("arbitrary",), vmem_limit_bytes=VMEM_LIMIT),
        name="in_proj_prompt",
    )(x2d, g_mix, w_main, w_dt, lcw, lcb, scw, scb, dtb)


def _outmlp_kernel(x_ref, y_ref, wo_ref, gm_ref, wu_ref, wd_ref, gf_ref, o_ref):
    x1 = x_ref[...] + jnp.dot(y_ref[...].astype(BF16), wo_ref[...], preferred_element_type=F32)
    m = _rms(x1, gm_ref[...]).astype(BF16)
    u = jnp.dot(m, wu_ref[...], preferred_element_type=F32)
    u = jnp.square(jnp.maximum(u, 0.0)).astype(BF16)
    x2 = x1 + jnp.dot(u, wd_ref[...], preferred_element_type=F32)
    o_ref[...] = _rms(x2, gf_ref[...])


def _out_mlp(x2d, ymix2d, w_out_b, g_mlp, w_up_b, w_down_b, g_final):
    n = x2d.shape[0]
    const = lambda i: (0, 0)
    return pl.pallas_call(
        _outmlp_kernel,
        out_shape=jax.ShapeDtypeStruct((n, D_MODEL), F32),
        grid=(n // ROW_TILE,),
        in_specs=[
            pl.BlockSpec((ROW_TILE, D_MODEL), lambda i: (i, 0)),
            pl.BlockSpec((ROW_TILE, MIX_WIDTH), lambda i: (i, 0)),
            pl.BlockSpec((MIX_WIDTH, D_MODEL), const, pipeline_mode=pl.Buffered(1)),
            pl.BlockSpec((1, D_MODEL), const),
            pl.BlockSpec((D_MODEL, D_FF), const, pipeline_mode=pl.Buffered(1)),
            pl.BlockSpec((D_FF, D_MODEL), const, pipeline_mode=pl.Buffered(1)),
            pl.BlockSpec((1, D_MODEL), const),
        ],
        out_specs=pl.BlockSpec((ROW_TILE, D_MODEL), lambda i: (i, 0)),
        compiler_params=pltpu.CompilerParams(
            dimension_semantics=("parallel",), vmem_limit_bytes=VMEM_LIMIT),
        name="out_mlp",
    )(x2d, ymix2d, w_out_b, g_mlp, w_up_b, w_down_b, g_final)


def _mixer_prompt_kernel(u_ref, gl_ref, zact_ref, xbc_ref, dt_ref,
                         wg_ref, ba_ref, bx_ref, lam_ref, glru_ref, alog_ref, dskip_ref, gssd_ref,
                         selt_ref, selp_ref,
                         y_ref, olh_ref, osh_ref,
                         a_pad, b_pad, h_pad, hcar, ht):
    t = pl.program_id(1)
    nt = pl.num_programs(1)
    tc = PROMPT_TC

    @pl.when(t == 0)
    def _init():
        hcar[...] = jnp.zeros_like(hcar)
        ht[...] = jnp.zeros_like(ht)

    neg_c_sp = (-LRU_C) * _softplus(-lam_ref[...])
    lane1 = lax.broadcasted_iota(jnp.int32, (1, LANES), 1)
    a2_row = jnp.where(lane1 < N_SSD_HEADS, -LOG2E * jnp.exp(alog_ref[...]), 0.0)
    L = SSD_CHUNK
    rr = lax.broadcasted_iota(jnp.int32, (L, L), 0)
    cc = lax.broadcasted_iota(jnp.int32, (L, L), 1)
    causal = cc <= rr
    tri = jnp.where(causal, 1.0, 0.0).astype(F32)
    mask_add = jnp.where(causal, 0.0, NEG_BIG).astype(F32)
    half = SSD_WIDTH // N_SSD_GROUPS

    chunks = [(n, c) for n in range(PROMPT_NB) for c in range(tc // L)]
    cums = [_ssd_cumdecay(dt_ref[n, L * c:L * (c + 1), :], a2_row, tri) for n, c in chunks]
    dts = [dt_ref[n, L * c:L * (c + 1), :] for n, c in chunks]
    cum_all = jnp.concatenate([cum2 for cum2, _ in cums], axis=0)
    cols_all = _spread(cum_all, selt_ref)
    ecol_all = _spread(jnp.exp2(cum_all), selp_ref)
    sdt_all = _spread(jnp.concatenate(
        [jnp.exp2(cum2[L - 1:L, :] - cum2) * dt for (cum2, _), dt in zip(cums, dts)], axis=0), selp_ref)

    for n in range(PROMPT_NB):
        a, b = _lru_coeffs(u_ref[n], wg_ref, ba_ref[...], bx_ref[...], neg_c_sp)
        hseq = _lru_scan_strided(a, b, hcar.at[n], a_pad.at[n], b_pad.at[n], h_pad.at[n])
        y_ref[n, :, 0:LRU_WIDTH] = _rms(hseq * gl_ref[n], glru_ref[...])

        for c in range(tc // L):
            k = chunks.index((n, c))
            rows = slice(L * c, L * (c + 1))
            krows = slice(L * k, L * (k + 1))
            xs = xbc_ref[n, rows, 0:SSD_WIDTH]
            bm = xbc_ref[n, rows, SSD_WIDTH:SSD_WIDTH + N_SSD_GROUPS * D_STATE]
            cm = xbc_ref[n, rows, SSD_WIDTH + N_SSD_GROUPS * D_STATE:SSD_CONV_DIM]
            y_diag = _ssd_diag(xs, bm, cm, dts[k], cols_all[krows, :], cums[k][1], mask_add)
            ecol = ecol_all[krows, :]
            xw = xs * sdt_all[krows, :]
            dec = ecol[L - 1:L, :]
            y_off_parts = []
            for g in range(N_SSD_GROUPS):
                htg = ht[n, g]
                cg = cm[:, D_STATE * g:D_STATE * (g + 1)].astype(BF16)
                y_off_parts.append(jnp.dot(cg, htg.astype(BF16), preferred_element_type=F32))
                bg_t = bm[:, D_STATE * g:D_STATE * (g + 1)].T.astype(BF16)
                st = jnp.dot(bg_t, xw[:, half * g:half * (g + 1)].astype(BF16), preferred_element_type=F32)
                ht[n, g] = htg * dec[:, half * g:half * (g + 1)] + st
            ys = y_diag + jnp.concatenate(y_off_parts, axis=1) * ecol
            y_ref[n, rows, LRU_WIDTH:MIX_WIDTH] = _ssd_gate_norm(
                ys, xs, zact_ref[n, rows, :], dskip_ref[...], gssd_ref[...])

    @pl.when(t == nt - 1)
    def _final():
        for n in range(PROMPT_NB):
            olh_ref[n] = hcar[n, 0:1, :]
            for g in range(N_SSD_GROUPS):
                osh_ref[n, half * g:half * (g + 1), :] = ht[n, g].T


def _param_specs(const):
    return [
        pl.BlockSpec((CONV_WIDTH, LRU_WIDTH), const),
        pl.BlockSpec((1, LRU_WIDTH), const),
        pl.BlockSpec((LRU_WIDTH // MXU_DIM, MXU_DIM, 2 * MXU_DIM), lambda *_: (0, 0, 0)),
        pl.BlockSpec((1, LRU_WIDTH), const),
        pl.BlockSpec((1, LRU_WIDTH), const),
        pl.BlockSpec((1, LRU_WIDTH), const),
        pl.BlockSpec((1, LRU_WIDTH), const),
        pl.BlockSpec((CONV_WIDTH, SSD_CONV_DIM), const),
        pl.BlockSpec((1, SSD_CONV_DIM), const),
        pl.BlockSpec((1, DT_PAD), const),
        pl.BlockSpec((1, DT_PAD), const),
        pl.BlockSpec((1, SSD_WIDTH), const),
        pl.BlockSpec((1, SSD_WIDTH), const),
    ]


def _head_selectors():
    k = jnp.arange(2 * LANES)[:, None] % LANES
    sel_t = (k == jnp.arange(N_SSD_HEADS * LANES)[None, :] // LANES).astype(BF16)
    sel_p = (k == jnp.arange(SSD_WIDTH)[None, :] // SSD_HEAD_DIM).astype(BF16)
    return sel_t, sel_p


def _mixer_prompt(act, wg, b_a, b_x, lam, g_lru, a_log, d_skip, g_ssd, sel_t, sel_p):
    bsz, seq, _ = act.shape
    tc = PROMPT_TC
    nb = PROMPT_NB
    const = lambda b, t: (0, 0)
    in_specs = [
        pl.BlockSpec((nb, tc, LRU_WIDTH), lambda b, t: (b, t, 0)),
        pl.BlockSpec((nb, tc, LRU_WIDTH), lambda b, t: (b, t, 1)),
        pl.BlockSpec((nb, tc, SSD_WIDTH), lambda b, t: (b, t, 2)),
        pl.BlockSpec((nb, tc, SSD_CONV_DIM), lambda b, t: (b, t, 2)),
        pl.BlockSpec((nb, tc, DT_PAD), lambda b, t: (b, t, PROJ_MAIN // DT_PAD)),
        pl.BlockSpec((LRU_WIDTH // MXU_DIM, MXU_DIM, 2 * MXU_DIM), lambda b, t: (0, 0, 0)),
        pl.BlockSpec((1, LRU_WIDTH), const),
        pl.BlockSpec((1, LRU_WIDTH), const),
        pl.BlockSpec((1, LRU_WIDTH), const),
        pl.BlockSpec((1, LRU_WIDTH), const),
        pl.BlockSpec((1, DT_PAD), const),
        pl.BlockSpec((1, SSD_WIDTH), const),
        pl.BlockSpec((1, SSD_WIDTH), const),
        pl.BlockSpec((2 * LANES, N_SSD_HEADS * LANES), const),
        pl.BlockSpec((2 * LANES, SSD_WIDTH), const),
    ]
    out_shape = (
        jax.ShapeDtypeStruct((bsz, seq, MIX_WIDTH), F32),
        jax.ShapeDtypeStruct((bsz, 1, LRU_WIDTH), F32),
        jax.ShapeDtypeStruct((bsz, SSD_WIDTH, D_STATE), F32),
    )
    out_specs = (
        pl.BlockSpec((nb, tc, MIX_WIDTH), lambda b, t: (b, t, 0)),
        pl.BlockSpec((nb, 1, LRU_WIDTH), lambda b, t: (b, 0, 0)),
        pl.BlockSpec((nb, SSD_WIDTH, D_STATE), lambda b, t: (b, 0, 0)),
    )
    scratch = [
        pltpu.VMEM((nb, LRU_WIDTH // LANES, SUBLANES * SCAN_PITCH, LANES), F32),
        pltpu.VMEM((nb, LRU_WIDTH // LANES, SUBLANES * SCAN_PITCH, LANES), F32),
        pltpu.VMEM((nb, LRU_WIDTH // LANES, SUBLANES * SCAN_PITCH, LANES), F32),
        pltpu.VMEM((nb, SUBLANES, LRU_WIDTH), F32),
        pltpu.VMEM((nb, N_SSD_GROUPS, D_STATE, SSD_WIDTH // N_SSD_GROUPS), F32),
    ]
    return pl.pallas_call(
        _mixer_prompt_kernel,
        out_shape=out_shape,
        grid=(bsz // nb, seq // tc),
        in_specs=in_specs,
        out_specs=out_specs,
        scratch_shapes=scratch,
        compiler_params=pltpu.CompilerParams(
            dimension_semantics=("parallel", "arbitrary"), vmem_limit_bytes=VMEM_LIMIT),
        name="mixer_prompt",
    )(act, act, act, act, act, wg, b_a, b_x, lam, g_lru, a_log, d_skip, g_ssd, sel_t, sel_p)


def _mixer_sample_kernel(lx_ref, gate_ref, z_ref, xbc_ref, dt_ref,
                         slc_ref, slh_ref, ssc_ref, ssh_ref,
                         lcw_ref, lcb_ref, wg_ref, ba_ref, bx_ref, lam_ref, glru_ref,
                         scw_ref, scb_ref, dtb_ref, alog_ref, dskip_ref, gssd_ref, selt_ref, selp_ref,
                         y_ref, olc_ref, olh_ref, osc_ref, osh_ref,
                         ext_l, ext_s, pad_scr, yoff_scr, *, T):
    S = SAMPLE_SEQS
    P = SUBLANES
    K1 = CONV_WIDTH - 1
    R = S * P
    row_i = lax.broadcasted_iota(jnp.int32, (R, 1), 0) & (P - 1)
    valid = row_i < T

    def pad_rows(ref):
        width = ref.shape[-1]
        pad_scr[:, :, 0:width] = jnp.zeros((S, P, width), F32)
        pad_scr[:, 0:T, 0:width] = ref[...].reshape(S, T, width)
        return pad_scr[:, :, 0:width].reshape(R, width)

    ext_l[...] = jnp.zeros_like(ext_l)
    ext_s[...] = jnp.zeros_like(ext_s)
    ext_l[:, 0:K1, :] = slc_ref[...]
    ext_l[:, K1:K1 + T, :] = lx_ref[...].reshape(S, T, LRU_WIDTH)
    ext_s[:, 0:K1, :] = ssc_ref[...]
    ext_s[:, K1:K1 + T, :] = xbc_ref[...].reshape(S, T, SSD_CONV_DIM)
    olc_ref[...] = ext_l[:, T:T + K1, :]
    osc_ref[...] = ext_s[:, T:T + K1, :]

    el = ext_l[...].reshape(R, LRU_WIDTH)
    es = ext_s[...].reshape(R, SSD_CONV_DIM)

    def conv(e, w_ref, b_ref):
        out = b_ref[...] + e * w_ref[0:1, :]
        for k in range(1, CONV_WIDTH):
            out = out + pltpu.roll(e, R - k, axis=0) * w_ref[k:k + 1, :]
        return out

    u = conv(el, lcw_ref, lcb_ref)
    neg_c_sp = (-LRU_C) * _softplus(-lam_ref[...])
    a, b = _lru_coeffs(u, wg_ref, ba_ref[...], bx_ref[...], neg_c_sp)
    a, b = _scan_within_8(a, b)
    h0 = jnp.broadcast_to(slh_ref[...], (S, P, LRU_WIDTH)).reshape(R, LRU_WIDTH)
    hseq = a * h0 + b
    olh_ref[...] = hseq.reshape(S, P, LRU_WIDTH)[:, T - 1:T, :]
    gate = pad_rows(gate_ref)
    y_lru = _rms(hseq * _gelu_tanh(gate), glru_ref[...])

    xbc = conv(es, scw_ref, scb_ref)
    xbc = xbc * _sigmoid(xbc)
    xs = xbc[:, 0:SSD_WIDTH]
    bm = xbc[:, SSD_WIDTH:SSD_WIDTH + N_SSD_GROUPS * D_STATE]
    cm = xbc[:, SSD_WIDTH + N_SSD_GROUPS * D_STATE:]
    dt_raw = pad_rows(dt_ref)
    dt = jnp.where(valid, _softplus(dt_raw + dtb_ref[...]), 0.0)
    lane1 = lax.broadcasted_iota(jnp.int32, (1, LANES), 1)
    a2_row = jnp.where(lane1 < N_SSD_HEADS, -LOG2E * jnp.exp(alog_ref[...]), 0.0)

    rr = lax.broadcasted_iota(jnp.int32, (R, R), 0)
    cc = lax.broadcasted_iota(jnp.int32, (R, R), 1)
    allowed = (cc <= rr) & ((rr - cc) <= (rr & (P - 1)))
    tri = jnp.where(allowed, 1.0, 0.0).astype(F32)
    mask_add = jnp.where(allowed, 0.0, NEG_BIG).astype(F32)

    cum2, cum2_t = _ssd_cumdecay(dt, a2_row, tri)
    y_diag = _ssd_diag(xs, bm, cm, dt, _spread(cum2, selt_ref), cum2_t, mask_add)
    ecol = _spread(jnp.exp2(cum2), selp_ref)
    end2 = jnp.broadcast_to(cum2.reshape(S, P, LANES)[:, P - 1:P, :], (S, P, LANES)).reshape(R, LANES)
    xw = xs * _spread(jnp.exp2(end2 - cum2) * dt, selp_ref)
    ecum_t = jnp.exp2(cum2_t)

    half = SSD_WIDTH // N_SSD_GROUPS
    for q in range(S):
        r0 = P * q
        vq = jnp.broadcast_to(ecum_t[:, r0 + P - 1:r0 + P], (N_SSD_HEADS, LANES))
        for g in range(N_SSD_GROUPS):
            hqg = ssh_ref[q, half * g:half * (g + 1), :]
            cq = cm[r0:r0 + P, D_STATE * g:D_STATE * (g + 1)].astype(BF16)
            yoff_scr[r0:r0 + P, half * g:half * (g + 1)] = lax.dot_general(
                cq, hqg.astype(BF16), (((1,), (1,)), ((), ())), preferred_element_type=F32)
            bq = bm[r0:r0 + P, D_STATE * g:D_STATE * (g + 1)].astype(BF16)
            xq = xw[r0:r0 + P, half * g:half * (g + 1)].astype(BF16)
            st = lax.dot_general(xq, bq, (((0,), (0,)), ((), ())), preferred_element_type=F32)
            for e in range(N_SSD_HEADS // N_SSD_GROUPS):
                h = (N_SSD_HEADS // N_SSD_GROUPS) * g + e
                lo_r = SSD_HEAD_DIM * e
                osh_ref[q, SSD_HEAD_DIM * h:SSD_HEAD_DIM * (h + 1), :] = (
                    vq[h:h + 1, :] * hqg[lo_r:lo_r + SSD_HEAD_DIM, :] + st[lo_r:lo_r + SSD_HEAD_DIM, :])

    ys = y_diag + yoff_scr[...] * ecol
    z = pad_rows(z_ref)
    y_ssd = _ssd_gate_norm(ys, xs, z * _sigmoid(z), dskip_ref[...], gssd_ref[...])
    y_ref[:, 0:LRU_WIDTH] = y_lru.reshape(S, P, LRU_WIDTH)[:, 0:T, :].reshape(S * T, LRU_WIDTH)
    y_ref[:, LRU_WIDTH:MIX_WIDTH] = y_ssd.reshape(S, P, SSD_WIDTH)[:, 0:T, :].reshape(S * T, SSD_WIDTH)


def _mixer_sample(proj, T, st_lc, st_lh, st_sc, st_sh, params, sel_t, sel_p):
    nseq = proj.shape[0] // T
    S = SAMPLE_SEQS
    const = lambda i: (0, 0)
    in_specs = [
        pl.BlockSpec((S * T, LRU_WIDTH), lambda i: (i, 0)),
        pl.BlockSpec((S * T, LRU_WIDTH), lambda i: (i, 1)),
        pl.BlockSpec((S * T, SSD_WIDTH), lambda i: (i, 2)),
        pl.BlockSpec((S * T, SSD_CONV_DIM), lambda i: (i, 2)),
        pl.BlockSpec((S * T, DT_PAD), lambda i: (i, PROJ_MAIN // DT_PAD)),
        pl.BlockSpec((S, CONV_WIDTH - 1, LRU_WIDTH), lambda i: (i, 0, 0)),
        pl.BlockSpec((S, 1, LRU_WIDTH), lambda i: (i, 0, 0)),
        pl.BlockSpec((S, CONV_WIDTH - 1, SSD_CONV_DIM), lambda i: (i, 0, 0)),
        pl.BlockSpec((S, SSD_WIDTH, D_STATE), lambda i: (i, 0, 0)),
    ] + _param_specs(const) + [
        pl.BlockSpec((2 * LANES, N_SSD_HEADS * LANES), const),
        pl.BlockSpec((2 * LANES, SSD_WIDTH), const),
    ]
    out_shape = (
        jax.ShapeDtypeStruct((nseq * T, MIX_WIDTH), F32),
        jax.ShapeDtypeStruct((nseq, CONV_WIDTH - 1, LRU_WIDTH), F32),
        jax.ShapeDtypeStruct((nseq, 1, LRU_WIDTH), F32),
        jax.ShapeDtypeStruct((nseq, CONV_WIDTH - 1, SSD_CONV_DIM), F32),
        jax.ShapeDtypeStruct((nseq, SSD_WIDTH, D_STATE), F32),
    )
    out_specs = (
        pl.BlockSpec((S * T, MIX_WIDTH), lambda i: (i, 0)),
        pl.BlockSpec((S, CONV_WIDTH - 1, LRU_WIDTH), lambda i: (i, 0, 0)),
        pl.BlockSpec((S, 1, LRU_WIDTH), lambda i: (i, 0, 0)),
        pl.BlockSpec((S, CONV_WIDTH - 1, SSD_CONV_DIM), lambda i: (i, 0, 0)),
        pl.BlockSpec((S, SSD_WIDTH, D_STATE), lambda i: (i, 0, 0)),
    )
    scratch = [
        pltpu.VMEM((S, SUBLANES, LRU_WIDTH), F32),
        pltpu.VMEM((S, SUBLANES, SSD_CONV_DIM), F32),
        pltpu.VMEM((S, SUBLANES, LRU_WIDTH), F32),
        pltpu.VMEM((S * SUBLANES, SSD_WIDTH), F32),
    ]
    return pl.pallas_call(
        functools.partial(_mixer_sample_kernel, T=T),
        out_shape=out_shape,
        grid=(nseq // S,),
        in_specs=in_specs,
        out_specs=out_specs,
        scratch_shapes=scratch,
        compiler_params=pltpu.CompilerParams(
            dimension_semantics=("parallel",), vmem_limit_bytes=VMEM_LIMIT),
        name="mixer_sample",
    )(proj, proj, proj, proj, proj, st_lc, st_lh, st_sc, st_sh, *params, sel_t, sel_p)


def _gate_weights(w_a, w_x):
    def tiles(w):
        per = MXU_DIM // LRU_BLOCK
        w4 = w.reshape(N_LRU_HEADS // per, per, LRU_BLOCK, LRU_BLOCK)
        eye = jnp.eye(per, dtype=w.dtype)
        t = jnp.einsum('jaik,ab->jaibk', w4, eye)
        return t.reshape(N_LRU_HEADS // per, MXU_DIM, MXU_DIM)
    return jnp.concatenate([tiles(w_a), tiles(w_x)], axis=2).astype(BF16)


def kernel(x_prompt, x_sample, state_lru_conv, state_lru_h, state_ssd_conv, state_ssd_h, g_mix, w_in,
           lru_conv_w, lru_conv_b, w_a, b_a, w_x, b_x, lam, g_lru_out, ssd_conv_w, ssd_conv_b, dt_bias,
           a_log, d_skip, g_ssd_out, w_out, g_mlp, w_up, w_down, g_final):
    depth = w_in.shape[0]
    assert depth == 1
    bp, seq, _ = x_prompt.shape
    bs, dseq, _ = x_sample.shape
    l = 0
    row = lambda v: v.reshape(1, -1)
    w_main, w_dt = _split_w_in(jnp.swapaxes(w_in, 1, 2)[l])
    params = (
        lru_conv_w[l], row(lru_conv_b[l]), _gate_weights(w_a[l], w_x[l]),
        row(b_a[l]), row(b_x[l]), row(lam[l]), row(g_lru_out[l]),
        ssd_conv_w[l], row(ssd_conv_b[l]),
        jnp.pad(row(dt_bias[l]), ((0, 0), (0, DT_PAD - N_SSD_HEADS))),
        jnp.pad(row(a_log[l]), ((0, 0), (0, DT_PAD - N_SSD_HEADS))),
        row(jnp.repeat(d_skip[l], SSD_HEAD_DIM)), row(g_ssd_out[l]),
    )
    w_out_b = w_out[l].astype(BF16)
    w_up_b = w_up[l].astype(BF16)
    w_down_b = w_down[l].astype(BF16)
    gmix = row(g_mix[l])
    gmlp = row(g_mlp[l])
    gfin = row(g_final)

    xp2 = x_prompt.reshape(bp * seq, D_MODEL)
    (lcw, lcb, wg, ba, bx, lam_r, glru, scw, scb, dtb, alog, dskip, gssd) = params
    act_p, p_lc, p_sc = _in_proj_prompt(xp2, bp, gmix, w_main, w_dt, lcw, lcb, scw, scb, dtb)
    sel_t, sel_p = _head_selectors()
    ymix_p, p_lh, p_sh = _mixer_prompt(
        act_p.reshape(bp, seq, PROJ_PAD), wg, ba, bx, lam_r, glru, alog, dskip, gssd, sel_t, sel_p)
    y_prompt = _out_mlp(xp2, ymix_p.reshape(bp * seq, MIX_WIDTH), w_out_b, gmlp, w_up_b, w_down_b, gfin)

    xs2 = x_sample.reshape(bs * dseq, D_MODEL)
    proj_s = _in_proj(xs2, gmix, w_main, w_dt)
    ymix_s, s_lc, s_lh, s_sc, s_sh = _mixer_sample(
        proj_s, dseq, state_lru_conv[l], state_lru_h[l].reshape(bs, 1, LRU_WIDTH), state_ssd_conv[l],
        state_ssd_h[l].reshape(bs, SSD_WIDTH, D_STATE), params, sel_t, sel_p)
    y_sample = _out_mlp(xs2, ymix_s, w_out_b, gmlp, w_up_b, w_down_b, gfin)

    hshape = (N_SSD_HEADS, SSD_HEAD_DIM, D_STATE)
    return (
        y_prompt.reshape(bp, seq, D_MODEL), y_sample.reshape(bs, dseq, D_MODEL),
        p_lc[None], p_lh.reshape(1, bp, LRU_WIDTH), p_sc[None], p_sh.reshape(1, bp, *hshape),
        s_lc[None], s_lh.reshape(1, bs, LRU_WIDTH), s_sc[None], s_sh.reshape(1, bs, *hshape),
    )
```

```python
import functools
import math

import jax
import jax.numpy as jnp
from jax import lax
from jax.experimental import pallas as pl
from jax.experimental.pallas import tpu as pltpu

F32 = jnp.float32
BF16 = jnp.bfloat16

D_MODEL = 1024
LRU_WIDTH = 1024
N_LRU_HEADS = 16
LRU_BLOCK = 64
LRU_C = 8.0
SSD_WIDTH = 1024
SSD_HEAD_DIM = 64
N_SSD_HEADS = 16
N_SSD_GROUPS = 2
D_STATE = 128
CONV_WIDTH = 4
SSD_CONV_DIM = SSD_WIDTH + 2 * N_SSD_GROUPS * D_STATE
D_FF = 4 * D_MODEL
EPS = 1e-6

LANES = 128
SUBLANES = 8
MXU_DIM = 256
DT_PAD = LANES
PROJ_MAIN = 2 * LRU_WIDTH + SSD_WIDTH + SSD_CONV_DIM
PROJ_PAD = PROJ_MAIN + DT_PAD
MIX_WIDTH = LRU_WIDTH + SSD_WIDTH
SSD_CHUNK = 128
PROMPT_TC = 256
PROMPT_NB = 2
ROW_TILE = 512
SPLIT_COLS = 1536
SAMPLE_SEQS = SSD_CHUNK // SUBLANES
SCAN_RUN = PROMPT_TC // SUBLANES
SCAN_PITCH = SCAN_RUN + 4
NEG_BIG = -1e30
LOG2E = 1.4426950408889634
VMEM_LIMIT = 56 * 1024 * 1024
HI = lax.Precision.HIGHEST


def _rms(x, g):
    ms = jnp.mean(x * x, axis=-1, keepdims=True)
    return x * lax.rsqrt(ms + EPS) * g


def _sigmoid(x):
    return 1.0 / (1.0 + jnp.exp(-x))


def _softplus(x):
    return jnp.maximum(x, 0.0) + jnp.log1p(jnp.exp(-jnp.abs(x)))


def _gelu_tanh(x):
    c = math.sqrt(2.0 / math.pi)
    return 0.5 * x * (1.0 + jnp.tanh(c * (x + 0.044715 * (x * x * x))))


def _lru_coeffs(u, wg_ref, b_a, b_x, neg_c_sp):
    ub = u.astype(BF16)
    r_parts, i_parts = [], []
    for j in range(LRU_WIDTH // MXU_DIM):
        g = jnp.dot(ub[:, MXU_DIM * j:MXU_DIM * (j + 1)], wg_ref[j], preferred_element_type=F32)
        r_parts.append(g[:, :MXU_DIM])
        i_parts.append(g[:, MXU_DIM:])
    r = _sigmoid(jnp.concatenate(r_parts, axis=1) + b_a)
    i = _sigmoid(jnp.concatenate(i_parts, axis=1) + b_x)
    log_a = r * neg_c_sp
    a = jnp.exp(log_a)
    th = jnp.tanh(log_a)
    v = (th + th) / (th - 1.0)
    mult = jnp.where(v > 0.0, v * lax.rsqrt(v), 0.0)
    return a, mult * (i * u)


def _scan_within_8(a, b):
    ridx = lax.broadcasted_iota(jnp.int32, a.shape, 0) & (SUBLANES - 1)
    for k in (1, 2, 4):
        a_s = pltpu.roll(a, k, axis=0)
        b_s = pltpu.roll(b, k, axis=0)
        m = ridx >= k
        b = jnp.where(m, a * b_s + b, b)
        a = jnp.where(m, a * a_s, a)
    return a, b


def _conv_slabs(ext, w_ref, b_ref, rows, first):
    parts = []
    for s in range(ext.shape[0]):
        cols = slice(LANES * s, LANES * (s + 1))
        acc = b_ref[:, cols] + ext[s, pl.ds(first, rows), :] * w_ref[0:1, cols]
        for k in range(1, CONV_WIDTH):
            acc = acc + ext[s, pl.ds(first + k, rows), :] * w_ref[k:k + 1, cols]
        parts.append(acc)
    return jnp.concatenate(parts, axis=1)


def _lru_scan_strided(a, b, hcar, a_pad, b_pad, h_pad):
    rows = a.shape[0]
    S = rows // SUBLANES
    nslab = LRU_WIDTH // LANES
    ridx = lax.broadcasted_iota(jnp.int32, (SUBLANES, LANES), 0)
    step = lambda ref, s, i: ref[s, pl.ds(i, SUBLANES, stride=SCAN_PITCH), :]
    for s in range(nslab):
        cols = slice(LANES * s, LANES * (s + 1))
        for j in range(SUBLANES):
            a_pad[s, SCAN_PITCH * j:SCAN_PITCH * j + S, :] = a[S * j:S * (j + 1), cols]
            b_pad[s, SCAN_PITCH * j:SCAN_PITCH * j + S, :] = b[S * j:S * (j + 1), cols]
    h = [jnp.zeros((SUBLANES, LANES), F32)] * nslab
    prod = [jnp.ones((SUBLANES, LANES), F32)] * nslab
    for i in range(S):
        for s in range(nslab):
            av = step(a_pad, s, i)
            h[s] = av * h[s] + step(b_pad, s, i)
            prod[s] = av * prod[s]
    for s in range(nslab):
        cols = slice(LANES * s, LANES * (s + 1))
        pcum, hcum = _scan_within_8(prod[s], h[s])
        cin = hcar[:, cols]
        ends = hcum + pcum * cin
        h[s] = jnp.where(ridx == 0, cin, pltpu.roll(ends, 1, axis=0))
        hcar[:, cols] = jnp.broadcast_to(ends[SUBLANES - 1:SUBLANES, :], (SUBLANES, LANES))
    for i in range(S):
        for s in range(nslab):
            h[s] = step(a_pad, s, i) * h[s] + step(b_pad, s, i)
            h_pad[s, pl.ds(i, SUBLANES, stride=SCAN_PITCH), :] = h[s]
    return jnp.concatenate(
        [jnp.concatenate([h_pad[s, SCAN_PITCH * j:SCAN_PITCH * j + S, :] for j in range(SUBLANES)], axis=0)
         for s in range(nslab)], axis=1)


def _ssd_cumdecay(dt, a2_row, tri):
    cum2 = jnp.dot(tri, dt * a2_row, precision=HI, preferred_element_type=F32)
    return cum2, cum2.T[0:N_SSD_HEADS, :]


def _ssd_diag(xs, bm, cm, dt, cols, cum2_t, mask_add):
    L = xs.shape[0]
    c2_t = cum2_t - jnp.log2(dt.T[0:N_SSD_HEADS, :])
    lane = lax.broadcasted_iota(jnp.int32, (L, LANES), 1)
    lo = lane < SSD_HEAD_DIM
    y_parts = []
    for g in range(N_SSD_GROUPS):
        bg = bm[:, D_STATE * g:D_STATE * (g + 1)].astype(BF16)
        cg = cm[:, D_STATE * g:D_STATE * (g + 1)].astype(BF16)
        cb = lax.dot_general(cg, bg, (((1,), (1,)), ((), ())), preferred_element_type=F32)
        for jj in range(N_SSD_HEADS // N_SSD_GROUPS // 2):
            j = (N_SSD_HEADS // N_SSD_GROUPS // 2) * g + jj
            h0, h1 = 2 * j, 2 * j + 1
            col0 = cols[:, LANES * h0:LANES * (h0 + 1)]
            col1 = cols[:, LANES * h1:LANES * (h1 + 1)]
            m0 = cb * jnp.exp2(col0 - c2_t[h0:h0 + 1, :] + mask_add)
            m1 = cb * jnp.exp2(col1 - c2_t[h1:h1 + 1, :] + mask_add)
            lhs = jnp.concatenate([m0, m1], axis=1).astype(BF16)
            xp = xs[:, LANES * j:LANES * (j + 1)]
            rhs = jnp.concatenate([jnp.where(lo, xp, 0.0), jnp.where(lo, 0.0, xp)], axis=0).astype(BF16)
            y_parts.append(jnp.dot(lhs, rhs, preferred_element_type=F32))
    return jnp.concatenate(y_parts, axis=1)


def _spread(v, sel_ref):
    p0 = v.astype(BF16)
    p1 = (v - p0.astype(F32)).astype(BF16)
    return jnp.dot(jnp.concatenate([p0, p1], axis=1), sel_ref[...], preferred_element_type=F32)


def _ssd_gate_norm(ys, xs, z_act, dskip, g_ssd):
    ys = ys + dskip * xs
    gated = ys * z_act
    half = SSD_WIDTH // N_SSD_GROUPS
    outs = []
    for g in range(N_SSD_GROUPS):
        outs.append(_rms(gated[:, half * g:half * (g + 1)], g_ssd[:, half * g:half * (g + 1)]))
    return jnp.concatenate(outs, axis=1)


def _split_w_in_kernel(wt_ref, wdt_t_ref, main_ref, dt_ref):
    main_ref[...] = wt_ref[...].T.astype(BF16)

    @pl.when(pl.program_id(0) == 0)
    def _dt():
        dt_ref[...] = jnp.zeros_like(dt_ref)
        dt_ref[:, 0:N_SSD_HEADS] = wdt_t_ref[...].T.astype(BF16)


def _split_w_in(w_t):
    cols, rows = w_t.shape
    blk = SPLIT_COLS
    return pl.pallas_call(
        _split_w_in_kernel,
        out_shape=(jax.ShapeDtypeStruct((rows, PROJ_MAIN), BF16), jax.ShapeDtypeStruct((rows, DT_PAD), BF16)),
        grid=(PROJ_MAIN // blk,),
        in_specs=[pl.BlockSpec((blk, rows), lambda j: (j, 0)),
                  pl.BlockSpec((N_SSD_HEADS, rows), lambda j: (0, 0))],
        out_specs=(pl.BlockSpec((rows, blk), lambda j: (0, j)), pl.BlockSpec((rows, DT_PAD), lambda j: (0, 0))),
        compiler_params=pltpu.CompilerParams(dimension_semantics=("arbitrary",)),
        name="split_w_in",
    )(w_t, w_t[PROJ_MAIN:, :])


def _inproj_kernel(x_ref, g_ref, w_ref, wdt_ref, o_ref):
    hn = _rms(x_ref[...].reshape(-1, D_MODEL), g_ref[...]).astype(BF16)
    o_ref[:, 0:PROJ_MAIN] = jnp.dot(hn, w_ref[...], preferred_element_type=F32)
    o_ref[:, PROJ_MAIN:PROJ_PAD] = jnp.dot(hn, wdt_ref[...], preferred_element_type=F32)


def _in_proj(x, g_mix, w_main, w_dt):
    nseq, ntok, _ = x.shape
    n = nseq * ntok
    return pl.pallas_call(
        _inproj_kernel,
        out_shape=jax.ShapeDtypeStruct((n, PROJ_PAD), F32),
        grid=(n // ROW_TILE,),
        in_specs=[
            pl.BlockSpec((ROW_TILE // ntok, ntok, D_MODEL), lambda i: (i, 0, 0)),
            pl.BlockSpec((1, D_MODEL), lambda i: (0, 0)),
            pl.BlockSpec((D_MODEL, PROJ_MAIN), lambda i: (0, 0), pipeline_mode=pl.Buffered(1)),
            pl.BlockSpec((D_MODEL, DT_PAD), lambda i: (0, 0), pipeline_mode=pl.Buffered(1)),
        ],
        out_specs=pl.BlockSpec((ROW_TILE, PROJ_PAD), lambda i: (i, 0)),
        compiler_params=pltpu.CompilerParams(
            dimension_semantics=("parallel",), vmem_limit_bytes=VMEM_LIMIT),
        name="in_proj",
    )(x, g_mix, w_main, w_dt)


def _inproj_prompt_kernel(x_ref, g_ref, w_ref, wdt_ref, lcw_ref, lcb_ref, scw_ref, scb_ref, dtb_ref,
                          o_ref, olc_ref, osc_ref, ext_l, ext_s, *, steps_per_seq):
    t = lax.rem(pl.program_id(0), steps_per_seq)
    rows = ROW_TILE
    hist = SUBLANES
    o1, o2, o3 = LRU_WIDTH, 2 * LRU_WIDTH, 2 * LRU_WIDTH + SSD_WIDTH

    @pl.when(t == 0)
    def _init():
        ext_l[:, 0:hist, :] = jnp.zeros((ext_l.shape[0], hist, LANES), F32)
        ext_s[:, 0:hist, :] = jnp.zeros((ext_s.shape[0], hist, LANES), F32)

    hn = _rms(x_ref[...], g_ref[...]).astype(BF16)
    lx = jnp.dot(hn, w_ref[:, 0:o1], preferred_element_type=F32)
    for s in range(ext_l.shape[0]):
        ext_l[s, hist:hist + rows, :] = lx[:, LANES * s:LANES * (s + 1)]
    xbc_in = jnp.dot(hn, w_ref[:, o3:PROJ_MAIN], preferred_element_type=F32)
    for s in range(ext_s.shape[0]):
        ext_s[s, hist:hist + rows, :] = xbc_in[:, LANES * s:LANES * (s + 1)]
    o_ref[:, o1:o2] = _gelu_tanh(jnp.dot(hn, w_ref[:, o1:o2], preferred_element_type=F32))
    z = jnp.dot(hn, w_ref[:, o2:o3], preferred_element_type=F32)
    o_ref[:, o2:o3] = z * _sigmoid(z)
    o_ref[:, PROJ_MAIN:PROJ_PAD] = _softplus(
        jnp.dot(hn, wdt_ref[...], preferred_element_type=F32) + dtb_ref[...])
    o_ref[:, 0:o1] = _conv_slabs(ext_l, lcw_ref, lcb_ref, rows, hist - (CONV_WIDTH - 1))
    xbc = _conv_slabs(ext_s, scw_ref, scb_ref, rows, hist - (CONV_WIDTH - 1))
    o_ref[:, o3:PROJ_MAIN] = xbc * _sigmoid(xbc)

    @pl.when(t == steps_per_seq - 1)
    def _final():
        last = slice(hist + rows - (CONV_WIDTH - 1), hist + rows)
        for s in range(ext_l.shape[0]):
            olc_ref[:, LANES * s:LANES * (s + 1)] = ext_l[s, last, :]
        for s in range(ext_s.shape[0]):
            osc_ref[:, LANES * s:LANES * (s + 1)] = ext_s[s, last, :]

    tail_l = ext_l[:, rows:rows + hist, :]
    tail_s = ext_s[:, rows:rows + hist, :]
    ext_l[:, 0:hist, :] = tail_l
    ext_s[:, 0:hist, :] = tail_s


def _in_proj_prompt(x2d, bsz, g_mix, w_main, w_dt, lcw, lcb, scw, scb, dtb):
    n = x2d.shape[0]
    steps_per_seq = n // bsz // ROW_TILE
    const = lambda i: (0, 0)
    return pl.pallas_call(
        functools.partial(_inproj_prompt_kernel, steps_per_seq=steps_per_seq),
        out_shape=(
            jax.ShapeDtypeStruct((n, PROJ_PAD), F32),
            jax.ShapeDtypeStruct((bsz, CONV_WIDTH - 1, LRU_WIDTH), F32),
            jax.ShapeDtypeStruct((bsz, CONV_WIDTH - 1, SSD_CONV_DIM), F32),
        ),
        grid=(n // ROW_TILE,),
        in_specs=[
            pl.BlockSpec((ROW_TILE, D_MODEL), lambda i: (i, 0)),
            pl.BlockSpec((1, D_MODEL), const),
            pl.BlockSpec((D_MODEL, PROJ_MAIN), const, pipeline_mode=pl.Buffered(1)),
            pl.BlockSpec((D_MODEL, DT_PAD), const, pipeline_mode=pl.Buffered(1)),
            pl.BlockSpec((CONV_WIDTH, LRU_WIDTH), const),
            pl.BlockSpec((1, LRU_WIDTH), const),
            pl.BlockSpec((CONV_WIDTH, SSD_CONV_DIM), const),
            pl.BlockSpec((1, SSD_CONV_DIM), const),
            pl.BlockSpec((1, DT_PAD), const),
        ],
        out_specs=(
            pl.BlockSpec((ROW_TILE, PROJ_PAD), lambda i: (i, 0)),
            pl.BlockSpec((None, CONV_WIDTH - 1, LRU_WIDTH), lambda i: (i // steps_per_seq, 0, 0)),
            pl.BlockSpec((None, CONV_WIDTH - 1, SSD_CONV_DIM), lambda i: (i // steps_per_seq, 0, 0)),
        ),
        scratch_shapes=[
            pltpu.VMEM((LRU_WIDTH // LANES, SUBLANES + ROW_TILE, LANES), F32),
            pltpu.VMEM((SSD_CONV_DIM // LANES, SUBLANES + ROW_TILE, LANES), F32),
        ],
        compiler_params=pltpu.CompilerParams(
            dimension_semantics=("arbitrary",), vmem_limit_bytes=VMEM_LIMIT),
        name="in_proj_prompt",
    )(x2d, g_mix, w_main, w_dt, lcw, lcb, scw, scb, dtb)


def _outmlp_kernel(x_ref, y_ref, wo_ref, gm_ref, wu_ref, wd_ref, gf_ref, o_ref):
    x = x_ref[...].reshape(-1, D_MODEL)
    x1 = x + jnp.dot(y_ref[...].astype(BF16), wo_ref[...], preferred_element_type=F32)
    m = _rms(x1, gm_ref[...]).astype(BF16)
    u = jnp.dot(m, wu_ref[...], preferred_element_type=F32)
    u = jnp.square(jnp.maximum(u, 0.0)).astype(BF16)
    x2 = x1 + jnp.dot(u, wd_ref[...], preferred_element_type=F32)
    o_ref[...] = _rms(x2, gf_ref[...]).reshape(o_ref.shape)


def _out_mlp(x, ymix2d, w_out_b, g_mlp, w_up_b, w_down_b, g_final):
    n = ymix2d.shape[0]
    const = lambda i: (0, 0)
    if x.ndim == 2:
        x_spec = pl.BlockSpec((ROW_TILE, D_MODEL), lambda i: (i, 0))
    else:
        x_spec = pl.BlockSpec((ROW_TILE // x.shape[1], x.shape[1], D_MODEL), lambda i: (i, 0, 0))
    return pl.pallas_call(
        _outmlp_kernel,
        out_shape=jax.ShapeDtypeStruct(x.shape, F32),
        grid=(n // ROW_TILE,),
        in_specs=[
            x_spec,
            pl.BlockSpec((ROW_TILE, MIX_WIDTH), lambda i: (i, 0)),
            pl.BlockSpec((MIX_WIDTH, D_MODEL), const, pipeline_mode=pl.Buffered(1)),
            pl.BlockSpec((1, D_MODEL), const),
            pl.BlockSpec((D_MODEL, D_FF), const, pipeline_mode=pl.Buffered(1)),
            pl.BlockSpec((D_FF, D_MODEL), const, pipeline_mode=pl.Buffered(1)),
            pl.BlockSpec((1, D_MODEL), const),
        ],
        out_specs=x_spec,
        compiler_params=pltpu.CompilerParams(
            dimension_semantics=("parallel",), vmem_limit_bytes=VMEM_LIMIT),
        name="out_mlp",
    )(x, ymix2d, w_out_b, g_mlp, w_up_b, w_down_b, g_final)


def _mixer_prompt_kernel(u_ref, gl_ref, zact_ref, xbc_ref, dt_ref,
                         wg_ref, ba_ref, bx_ref, lam_ref, glru_ref, alog_ref, dskip_ref, gssd_ref,
                         selt_ref, selp_ref,
                         y_ref, olh_ref, osh_ref,
                         a_pad, b_pad, h_pad, hcar, ht):
    t = pl.program_id(1)
    nt = pl.num_programs(1)
    tc = PROMPT_TC

    @pl.when(t == 0)
    def _init():
        hcar[...] = jnp.zeros_like(hcar)
        ht[...] = jnp.zeros_like(ht)

    neg_c_sp = (-LRU_C) * _softplus(-lam_ref[...])
    lane1 = lax.broadcasted_iota(jnp.int32, (1, LANES), 1)
    a2_row = jnp.where(lane1 < N_SSD_HEADS, -LOG2E * jnp.exp(alog_ref[...]), 0.0)
    L = SSD_CHUNK
    rr = lax.broadcasted_iota(jnp.int32, (L, L), 0)
    cc = lax.broadcasted_iota(jnp.int32, (L, L), 1)
    causal = cc <= rr
    tri = jnp.where(causal, 1.0, 0.0).astype(F32)
    mask_add = jnp.where(causal, 0.0, NEG_BIG).astype(F32)
    half = SSD_WIDTH // N_SSD_GROUPS

    chunks = [(n, c) for n in range(PROMPT_NB) for c in range(tc // L)]
    cums = [_ssd_cumdecay(dt_ref[n, L * c:L * (c + 1), :], a2_row, tri) for n, c in chunks]
    dts = [dt_ref[n, L * c:L * (c + 1), :] for n, c in chunks]
    cum_all = jnp.concatenate([cum2 for cum2, _ in cums], axis=0)
    cols_all = _spread(cum_all, selt_ref)
    ecol_all = _spread(jnp.exp2(cum_all), selp_ref)
    sdt_all = _spread(jnp.concatenate(
        [jnp.exp2(cum2[L - 1:L, :] - cum2) * dt for (cum2, _), dt in zip(cums, dts)], axis=0), selp_ref)

    for n in range(PROMPT_NB):
        a, b = _lru_coeffs(u_ref[n], wg_ref, ba_ref[...], bx_ref[...], neg_c_sp)
        hseq = _lru_scan_strided(a, b, hcar.at[n], a_pad.at[n], b_pad.at[n], h_pad.at[n])
        y_ref[n, :, 0:LRU_WIDTH] = _rms(hseq * gl_ref[n], glru_ref[...])

        for c in range(tc // L):
            k = chunks.index((n, c))
            rows = slice(L * c, L * (c + 1))
            krows = slice(L * k, L * (k + 1))
            xs = xbc_ref[n, rows, 0:SSD_WIDTH]
            bm = xbc_ref[n, rows, SSD_WIDTH:SSD_WIDTH + N_SSD_GROUPS * D_STATE]
            cm = xbc_ref[n, rows, SSD_WIDTH + N_SSD_GROUPS * D_STATE:SSD_CONV_DIM]
            y_diag = _ssd_diag(xs, bm, cm, dts[k], cols_all[krows, :], cums[k][1], mask_add)
            ecol = ecol_all[krows, :]
            xw = xs * sdt_all[krows, :]
            dec = ecol[L - 1:L, :]
            y_off_parts = []
            for g in range(N_SSD_GROUPS):
                htg = ht[n, g]
                cg = cm[:, D_STATE * g:D_STATE * (g + 1)].astype(BF16)
                y_off_parts.append(jnp.dot(cg, htg.astype(BF16), preferred_element_type=F32))
                bg_t = bm[:, D_STATE * g:D_STATE * (g + 1)].T.astype(BF16)
                st = jnp.dot(bg_t, xw[:, half * g:half * (g + 1)].astype(BF16), preferred_element_type=F32)
                ht[n, g] = htg * dec[:, half * g:half * (g + 1)] + st
            ys = y_diag + jnp.concatenate(y_off_parts, axis=1) * ecol
            y_ref[n, rows, LRU_WIDTH:MIX_WIDTH] = _ssd_gate_norm(
                ys, xs, zact_ref[n, rows, :], dskip_ref[...], gssd_ref[...])

    @pl.when(t == nt - 1)
    def _final():
        for n in range(PROMPT_NB):
            olh_ref[n] = hcar[n, 0:1, :]
            for g in range(N_SSD_GROUPS):
                osh_ref[n, half * g:half * (g + 1), :] = ht[n, g].T


def _param_specs(const):
    return [
        pl.BlockSpec((CONV_WIDTH, LRU_WIDTH), const),
        pl.BlockSpec((1, LRU_WIDTH), const),
        pl.BlockSpec((LRU_WIDTH // MXU_DIM, MXU_DIM, 2 * MXU_DIM), lambda *_: (0, 0, 0)),
        pl.BlockSpec((1, LRU_WIDTH), const),
        pl.BlockSpec((1, LRU_WIDTH), const),
        pl.BlockSpec((1, LRU_WIDTH), const),
        pl.BlockSpec((1, LRU_WIDTH), const),
        pl.BlockSpec((CONV_WIDTH, SSD_CONV_DIM), const),
        pl.BlockSpec((1, SSD_CONV_DIM), const),
        pl.BlockSpec((1, DT_PAD), const),
        pl.BlockSpec((1, DT_PAD), const),
        pl.BlockSpec((1, SSD_WIDTH), const),
        pl.BlockSpec((1, SSD_WIDTH), const),
    ]


def _head_selectors():
    k = jnp.arange(2 * LANES)[:, None] % LANES
    sel_t = (k == jnp.arange(N_SSD_HEADS * LANES)[None, :] // LANES).astype(BF16)
    sel_p = (k == jnp.arange(SSD_WIDTH)[None, :] // SSD_HEAD_DIM).astype(BF16)
    return sel_t, sel_p


def _mixer_prompt(act, wg, b_a, b_x, lam, g_lru, a_log, d_skip, g_ssd, sel_t, sel_p):
    bsz, seq, _ = act.shape
    tc = PROMPT_TC
    nb = PROMPT_NB
    const = lambda b, t: (0, 0)
    in_specs = [
        pl.BlockSpec((nb, tc, LRU_WIDTH), lambda b, t: (b, t, 0)),
        pl.BlockSpec((nb, tc, LRU_WIDTH), lambda b, t: (b, t, 1)),
        pl.BlockSpec((nb, tc, SSD_WIDTH), lambda b, t: (b, t, 2)),
        pl.BlockSpec((nb, tc, SSD_CONV_DIM), lambda b, t: (b, t, 2)),
        pl.BlockSpec((nb, tc, DT_PAD), lambda b, t: (b, t, PROJ_MAIN // DT_PAD)),
        pl.BlockSpec((LRU_WIDTH // MXU_DIM, MXU_DIM, 2 * MXU_DIM), lambda b, t: (0, 0, 0)),
        pl.BlockSpec((1, LRU_WIDTH), const),
        pl.BlockSpec((1, LRU_WIDTH), const),
        pl.BlockSpec((1, LRU_WIDTH), const),
        pl.BlockSpec((1, LRU_WIDTH), const),
        pl.BlockSpec((1, DT_PAD), const),
        pl.BlockSpec((1, SSD_WIDTH), const),
        pl.BlockSpec((1, SSD_WIDTH), const),
        pl.BlockSpec((2 * LANES, N_SSD_HEADS * LANES), const),
        pl.BlockSpec((2 * LANES, SSD_WIDTH), const),
    ]
    out_shape = (
        jax.ShapeDtypeStruct((bsz, seq, MIX_WIDTH), F32),
        jax.ShapeDtypeStruct((bsz, 1, LRU_WIDTH), F32),
        jax.ShapeDtypeStruct((bsz, SSD_WIDTH, D_STATE), F32),
    )
    out_specs = (
        pl.BlockSpec((nb, tc, MIX_WIDTH), lambda b, t: (b, t, 0)),
        pl.BlockSpec((nb, 1, LRU_WIDTH), lambda b, t: (b, 0, 0)),
        pl.BlockSpec((nb, SSD_WIDTH, D_STATE), lambda b, t: (b, 0, 0)),
    )
    scratch = [
        pltpu.VMEM((nb, LRU_WIDTH // LANES, SUBLANES * SCAN_PITCH, LANES), F32),
        pltpu.VMEM((nb, LRU_WIDTH // LANES, SUBLANES * SCAN_PITCH, LANES), F32),
        pltpu.VMEM((nb, LRU_WIDTH // LANES, SUBLANES * SCAN_PITCH, LANES), F32),
        pltpu.VMEM((nb, SUBLANES, LRU_WIDTH), F32),
        pltpu.VMEM((nb, N_SSD_GROUPS, D_STATE, SSD_WIDTH // N_SSD_GROUPS), F32),
    ]
    return pl.pallas_call(
        _mixer_prompt_kernel,
        out_shape=out_shape,
        grid=(bsz // nb, seq // tc),
        in_specs=in_specs,
        out_specs=out_specs,
        scratch_shapes=scratch,
        compiler_params=pltpu.CompilerParams(
            dimension_semantics=("parallel", "arbitrary"), vmem_limit_bytes=VMEM_LIMIT),
        name="mixer_prompt",
    )(act, act, act, act, act, wg, b_a, b_x, lam, g_lru, a_log, d_skip, g_ssd, sel_t, sel_p)


def _mixer_sample_kernel(lx_ref, gate_ref, z_ref, xbc_ref, dt_ref,
                         slc_ref, slh_ref, ssc_ref, ssh_ref,
                         lcw_ref, lcb_ref, wg_ref, ba_ref, bx_ref, lam_ref, glru_ref,
                         scw_ref, scb_ref, dtb_ref, alog_ref, dskip_ref, gssd_ref, selt_ref, selp_ref,
                         y_ref, olc_ref, olh_ref, osc_ref, osh_ref,
                         ext_l, ext_s, pad_scr, yoff_scr, *, T):
    S = SAMPLE_SEQS
    P = SUBLANES
    K1 = CONV_WIDTH - 1
    R = S * P
    row_i = lax.broadcasted_iota(jnp.int32, (R, 1), 0) & (P - 1)
    valid = row_i < T

    def pad_rows(ref):
        width = ref.shape[-1]
        pad_scr[:, :, 0:width] = jnp.zeros((S, P, width), F32)
        pad_scr[:, 0:T, 0:width] = ref[...].reshape(S, T, width)
        return pad_scr[:, :, 0:width].reshape(R, width)

    ext_l[...] = jnp.zeros_like(ext_l)
    ext_s[...] = jnp.zeros_like(ext_s)
    ext_l[:, 0:K1, :] = slc_ref[...]
    ext_l[:, K1:K1 + T, :] = lx_ref[...].reshape(S, T, LRU_WIDTH)
    ext_s[:, 0:K1, :] = ssc_ref[...]
    ext_s[:, K1:K1 + T, :] = xbc_ref[...].reshape(S, T, SSD_CONV_DIM)
    olc_ref[...] = ext_l[:, T:T + K1, :]
    osc_ref[...] = ext_s[:, T:T + K1, :]

    el = ext_l[...].reshape(R, LRU_WIDTH)
    es = ext_s[...].reshape(R, SSD_CONV_DIM)

    def conv(e, w_ref, b_ref):
        out = b_ref[...] + e * w_ref[0:1, :]
        for k in range(1, CONV_WIDTH):
            out = out + pltpu.roll(e, R - k, axis=0) * w_ref[k:k + 1, :]
        return out

    u = conv(el, lcw_ref, lcb_ref)
    neg_c_sp = (-LRU_C) * _softplus(-lam_ref[...])
    a, b = _lru_coeffs(u, wg_ref, ba_ref[...], bx_ref[...], neg_c_sp)
    a, b = _scan_within_8(a, b)
    h0 = jnp.broadcast_to(slh_ref[...], (S, P, LRU_WIDTH)).reshape(R, LRU_WIDTH)
    hseq = a * h0 + b
    olh_ref[...] = hseq.reshape(S, P, LRU_WIDTH)[:, T - 1:T, :]
    gate = pad_rows(gate_ref)
    y_lru = _rms(hseq * _gelu_tanh(gate), glru_ref[...])

    xbc = conv(es, scw_ref, scb_ref)
    xbc = xbc * _sigmoid(xbc)
    xs = xbc[:, 0:SSD_WIDTH]
    bm = xbc[:, SSD_WIDTH:SSD_WIDTH + N_SSD_GROUPS * D_STATE]
    cm = xbc[:, SSD_WIDTH + N_SSD_GROUPS * D_STATE:]
    dt_raw = pad_rows(dt_ref)
    dt = jnp.where(valid, _softplus(dt_raw + dtb_ref[...]), 0.0)
    lane1 = lax.broadcasted_iota(jnp.int32, (1, LANES), 1)
    a2_row = jnp.where(lane1 < N_SSD_HEADS, -LOG2E * jnp.exp(alog_ref[...]), 0.0)

    rr = lax.broadcasted_iota(jnp.int32, (R, R), 0)
    cc = lax.broadcasted_iota(jnp.int32, (R, R), 1)
    allowed = (cc <= rr) & ((rr - cc) <= (rr & (P - 1)))
    tri = jnp.where(allowed, 1.0, 0.0).astype(F32)
    mask_add = jnp.where(allowed, 0.0, NEG_BIG).astype(F32)

    cum2, cum2_t = _ssd_cumdecay(dt, a2_row, tri)
    y_diag = _ssd_diag(xs, bm, cm, dt, _spread(cum2, selt_ref), cum2_t, mask_add)
    ecol = _spread(jnp.exp2(cum2), selp_ref)
    end2 = jnp.broadcast_to(cum2.reshape(S, P, LANES)[:, P - 1:P, :], (S, P, LANES)).reshape(R, LANES)
    xw = xs * _spread(jnp.exp2(end2 - cum2) * dt, selp_ref)
    ecum_t = jnp.exp2(cum2_t)

    half = SSD_WIDTH // N_SSD_GROUPS
    for q in range(S):
        r0 = P * q
        vq = jnp.broadcast_to(ecum_t[:, r0 + P - 1:r0 + P], (N_SSD_HEADS, LANES))
        for g in range(N_SSD_GROUPS):
            hqg = ssh_ref[q, half * g:half * (g + 1), :]
            cq = cm[r0:r0 + P, D_STATE * g:D_STATE * (g + 1)].astype(BF16)
            yoff_scr[r0:r0 + P, half * g:half * (g + 1)] = lax.dot_general(
                cq, hqg.astype(BF16), (((1,), (1,)), ((), ())), preferred_element_type=F32)
            bq = bm[r0:r0 + P, D_STATE * g:D_STATE * (g + 1)].astype(BF16)
            xq = xw[r0:r0 + P, half * g:half * (g + 1)].astype(BF16)
            st = lax.dot_general(xq, bq, (((0,), (0,)), ((), ())), preferred_element_type=F32)
            for e in range(N_SSD_HEADS // N_SSD_GROUPS):
                h = (N_SSD_HEADS // N_SSD_GROUPS) * g + e
                lo_r = SSD_HEAD_DIM * e
                osh_ref[q, SSD_HEAD_DIM * h:SSD_HEAD_DIM * (h + 1), :] = (
                    vq[h:h + 1, :] * hqg[lo_r:lo_r + SSD_HEAD_DIM, :] + st[lo_r:lo_r + SSD_HEAD_DIM, :])

    ys = y_diag + yoff_scr[...] * ecol
    z = pad_rows(z_ref)
    y_ssd = _ssd_gate_norm(ys, xs, z * _sigmoid(z), dskip_ref[...], gssd_ref[...])
    y_ref[:, 0:LRU_WIDTH] = y_lru.reshape(S, P, LRU_WIDTH)[:, 0:T, :].reshape(S * T, LRU_WIDTH)
    y_ref[:, LRU_WIDTH:MIX_WIDTH] = y_ssd.reshape(S, P, SSD_WIDTH)[:, 0:T, :].reshape(S * T, SSD_WIDTH)


def _mixer_sample(proj, T, st_lc, st_lh, st_sc, st_sh, params, sel_t, sel_p):
    nseq = proj.shape[0] // T
    S = SAMPLE_SEQS
    const = lambda i: (0, 0)
    in_specs = [
        pl.BlockSpec((S * T, LRU_WIDTH), lambda i: (i, 0)),
        pl.BlockSpec((S * T, LRU_WIDTH), lambda i: (i, 1)),
        pl.BlockSpec((S * T, SSD_WIDTH), lambda i: (i, 2)),
        pl.BlockSpec((S * T, SSD_CONV_DIM), lambda i: (i, 2)),
        pl.BlockSpec((S * T, DT_PAD), lambda i: (i, PROJ_MAIN // DT_PAD)),
        pl.BlockSpec((S, CONV_WIDTH - 1, LRU_WIDTH), lambda i: (i, 0, 0)),
        pl.BlockSpec((S, 1, LRU_WIDTH), lambda i: (i, 0, 0)),
        pl.BlockSpec((S, CONV_WIDTH - 1, SSD_CONV_DIM), lambda i: (i, 0, 0)),
        pl.BlockSpec((S, SSD_WIDTH, D_STATE), lambda i: (i, 0, 0)),
    ] + _param_specs(const) + [
        pl.BlockSpec((2 * LANES, N_SSD_HEADS * LANES), const),
        pl.BlockSpec((2 * LANES, SSD_WIDTH), const),
    ]
    out_shape = (
        jax.ShapeDtypeStruct((nseq * T, MIX_WIDTH), F32),
        jax.ShapeDtypeStruct((nseq, CONV_WIDTH - 1, LRU_WIDTH), F32),
        jax.ShapeDtypeStruct((nseq, 1, LRU_WIDTH), F32),
        jax.ShapeDtypeStruct((nseq, CONV_WIDTH - 1, SSD_CONV_DIM), F32),
        jax.ShapeDtypeStruct((nseq, SSD_WIDTH, D_STATE), F32),
    )
    out_specs = (
        pl.BlockSpec((S * T, MIX_WIDTH), lambda i: (i, 0)),
        pl.BlockSpec((S, CONV_WIDTH - 1, LRU_WIDTH), lambda i: (i, 0, 0)),
        pl.BlockSpec((S, 1, LRU_WIDTH), lambda i: (i, 0, 0)),
        pl.BlockSpec((S, CONV_WIDTH - 1, SSD_CONV_DIM), lambda i: (i, 0, 0)),
        pl.BlockSpec((S, SSD_WIDTH, D_STATE), lambda i: (i, 0, 0)),
    )
    scratch = [
        pltpu.VMEM((S, SUBLANES, LRU_WIDTH), F32),
        pltpu.VMEM((S, SUBLANES, SSD_CONV_DIM), F32),
        pltpu.VMEM((S, SUBLANES, LRU_WIDTH), F32),
        pltpu.VMEM((S * SUBLANES, SSD_WIDTH), F32),
    ]
    return pl.pallas_call(
        functools.partial(_mixer_sample_kernel, T=T),
        out_shape=out_shape,
        grid=(nseq // S,),
        in_specs=in_specs,
        out_specs=out_specs,
        scratch_shapes=scratch,
        compiler_params=pltpu.CompilerParams(
            dimension_semantics=("parallel",), vmem_limit_bytes=VMEM_LIMIT),
        name="mixer_sample",
    )(proj, proj, proj, proj, proj, st_lc, st_lh, st_sc, st_sh, *params, sel_t, sel_p)


def _gate_weights(w_a, w_x):
    def tiles(w):
        per = MXU_DIM // LRU_BLOCK
        w4 = w.reshape(N_LRU_HEADS // per, per, LRU_BLOCK, LRU_BLOCK)
        eye = jnp.eye(per, dtype=w.dtype)
        t = jnp.einsum('jaik,ab->jaibk', w4, eye)
        return t.reshape(N_LRU_HEADS // per, MXU_DIM, MXU_DIM)
    return jnp.concatenate([tiles(w_a), tiles(w_x)], axis=2).astype(BF16)


def kernel(x_prompt, x_sample, state_lru_conv, state_lru_h, state_ssd_conv, state_ssd_h, g_mix, w_in,
           lru_conv_w, lru_conv_b, w_a, b_a, w_x, b_x, lam, g_lru_out, ssd_conv_w, ssd_conv_b, dt_bias,
           a_log, d_skip, g_ssd_out, w_out, g_mlp, w_up, w_down, g_final):
    depth = w_in.shape[0]
    assert depth == 1
    bp, seq, _ = x_prompt.shape
    bs, dseq, _ = x_sample.shape
    l = 0
    row = lambda v: v.reshape(1, -1)
    w_main, w_dt = _split_w_in(jnp.swapaxes(w_in, 1, 2)[l])
    params = (
        lru_conv_w[l], row(lru_conv_b[l]), _gate_weights(w_a[l], w_x[l]),
        row(b_a[l]), row(b_x[l]), row(lam[l]), row(g_lru_out[l]),
        ssd_conv_w[l], row(ssd_conv_b[l]),
        jnp.pad(row(dt_bias[l]), ((0, 0), (0, DT_PAD - N_SSD_HEADS))),
        jnp.pad(row(a_log[l]), ((0, 0), (0, DT_PAD - N_SSD_HEADS))),
        row(jnp.repeat(d_skip[l], SSD_HEAD_DIM)), row(g_ssd_out[l]),
    )
    w_out_b = w_out[l].astype(BF16)
    w_up_b = w_up[l].astype(BF16)
    w_down_b = w_down[l].astype(BF16)
    gmix = row(g_mix[l])
    gmlp = row(g_mlp[l])
    gfin = row(g_final)

    xp2 = x_prompt.reshape(bp * seq, D_MODEL)
    (lcw, lcb, wg, ba, bx, lam_r, glru, scw, scb, dtb, alog, dskip, gssd) = params
    act_p, p_lc, p_sc = _in_proj_prompt(xp2, bp, gmix, w_main, w_dt, lcw, lcb, scw, scb, dtb)
    sel_t, sel_p = _head_selectors()
    ymix_p, p_lh, p_sh = _mixer_prompt(
        act_p.reshape(bp, seq, PROJ_PAD), wg, ba, bx, lam_r, glru, alog, dskip, gssd, sel_t, sel_p)
    y_prompt = _out_mlp(xp2, ymix_p.reshape(bp * seq, MIX_WIDTH), w_out_b, gmlp, w_up_b, w_down_b, gfin)

    proj_s = _in_proj(x_sample, gmix, w_main, w_dt)
    ymix_s, s_lc, s_lh, s_sc, s_sh = _mixer_sample(
        proj_s, dseq, state_lru_conv[l], state_lru_h[l].reshape(bs, 1, LRU_WIDTH), state_ssd_conv[l],
        state_ssd_h[l].reshape(bs, SSD_WIDTH, D_STATE), params, sel_t, sel_p)
    y_sample = _out_mlp(x_sample, ymix_s, w_out_b, gmlp, w_up_b, w_down_b, gfin)

    hshape = (N_SSD_HEADS, SSD_HEAD_DIM, D_STATE)
    return (
        y_prompt.reshape(bp, seq, D_MODEL), y_sample,
        p_lc[None], p_lh.reshape(1, bp, LRU_WIDTH), p_sc[None], p_sh.reshape(1, bp, *hshape),
        s_lc[None], s_lh.reshape(1, bs, LRU_WIDTH), s_sc[None], s_sh.reshape(1, bs, *hshape),
    )
```

```python
import functools
import math

import jax
import jax.numpy as jnp
from jax import lax
from jax.experimental import pallas as pl
from jax.experimental.pallas import tpu as pltpu

F32 = jnp.float32
BF16 = jnp.bfloat16

D_MODEL = 1024
LRU_WIDTH = 1024
N_LRU_HEADS = 16
LRU_BLOCK = 64
LRU_C = 8.0
SSD_WIDTH = 1024
SSD_HEAD_DIM = 64
N_SSD_HEADS = 16
N_SSD_GROUPS = 2
D_STATE = 128
CONV_WIDTH = 4
SSD_CONV_DIM = SSD_WIDTH + 2 * N_SSD_GROUPS * D_STATE
D_FF = 4 * D_MODEL
EPS = 1e-6

LANES = 128
SUBLANES = 8
MXU_DIM = 256
DT_PAD = LANES
PROJ_MAIN = 2 * LRU_WIDTH + SSD_WIDTH + SSD_CONV_DIM
PROJ_PAD = PROJ_MAIN + DT_PAD
MIX_WIDTH = LRU_WIDTH + SSD_WIDTH
SSD_CHUNK = 128
PROMPT_TC = 256
PROMPT_NB = 2
ROW_TILE = 512
SPLIT_COLS = 1536
SAMPLE_SEQS = SSD_CHUNK // SUBLANES
SCAN_RUN = PROMPT_TC // SUBLANES
SCAN_PITCH = SCAN_RUN + 4
NEG_BIG = -1e30
LOG2E = 1.4426950408889634
VMEM_LIMIT = 56 * 1024 * 1024
HI = lax.Precision.HIGHEST


def _rms(x, g):
    ms = jnp.mean(x * x, axis=-1, keepdims=True)
    return x * lax.rsqrt(ms + EPS) * g


def _sigmoid(x):
    return 1.0 / (1.0 + jnp.exp(-x))


def _softplus(x):
    return jnp.maximum(x, 0.0) + jnp.log1p(jnp.exp(-jnp.abs(x)))


def _gelu_tanh(x):
    c = math.sqrt(2.0 / math.pi)
    return 0.5 * x * (1.0 + jnp.tanh(c * (x + 0.044715 * (x * x * x))))


def _lru_coeffs(u, wg_ref, b_a, b_x, neg_c_sp):
    ub = u.astype(BF16)
    r_parts, i_parts = [], []
    for j in range(LRU_WIDTH // MXU_DIM):
        g = jnp.dot(ub[:, MXU_DIM * j:MXU_DIM * (j + 1)], wg_ref[j], preferred_element_type=F32)
        r_parts.append(g[:, :MXU_DIM])
        i_parts.append(g[:, MXU_DIM:])
    r = _sigmoid(jnp.concatenate(r_parts, axis=1) + b_a)
    i = _sigmoid(jnp.concatenate(i_parts, axis=1) + b_x)
    log_a = r * neg_c_sp
    a = jnp.exp(log_a)
    th = jnp.tanh(log_a)
    v = (th + th) / (th - 1.0)
    mult = jnp.where(v > 0.0, v * lax.rsqrt(v), 0.0)
    return a, mult * (i * u)


def _scan_within_8(a, b):
    ridx = lax.broadcasted_iota(jnp.int32, a.shape, 0) & (SUBLANES - 1)
    for k in (1, 2, 4):
        a_s = pltpu.roll(a, k, axis=0)
        b_s = pltpu.roll(b, k, axis=0)
        m = ridx >= k
        b = jnp.where(m, a * b_s + b, b)
        a = jnp.where(m, a * a_s, a)
    return a, b


def _conv_slabs(ext, w_ref, b_ref, rows, first):
    parts = []
    for s in range(ext.shape[0]):
        cols = slice(LANES * s, LANES * (s + 1))
        acc = b_ref[:, cols] + ext[s, pl.ds(first, rows), :] * w_ref[0:1, cols]
        for k in range(1, CONV_WIDTH):
            acc = acc + ext[s, pl.ds(first + k, rows), :] * w_ref[k:k + 1, cols]
        parts.append(acc)
    return jnp.concatenate(parts, axis=1)


def _lru_scan_strided(a, b, hcar, a_pad, b_pad, h_pad):
    rows = a.shape[0]
    S = rows // SUBLANES
    nslab = LRU_WIDTH // LANES
    ridx = lax.broadcasted_iota(jnp.int32, (SUBLANES, LANES), 0)
    step = lambda ref, s, i: ref[s, pl.ds(i, SUBLANES, stride=SCAN_PITCH), :]
    for s in range(nslab):
        cols = slice(LANES * s, LANES * (s + 1))
        for j in range(SUBLANES):
            a_pad[s, SCAN_PITCH * j:SCAN_PITCH * j + S, :] = a[S * j:S * (j + 1), cols]
            b_pad[s, SCAN_PITCH * j:SCAN_PITCH * j + S, :] = b[S * j:S * (j + 1), cols]
    h = [jnp.zeros((SUBLANES, LANES), F32)] * nslab
    prod = [jnp.ones((SUBLANES, LANES), F32)] * nslab
    for i in range(S):
        for s in range(nslab):
            av = step(a_pad, s, i)
            h[s] = av * h[s] + step(b_pad, s, i)
            prod[s] = av * prod[s]
    for s in range(nslab):
        cols = slice(LANES * s, LANES * (s + 1))
        pcum, hcum = _scan_within_8(prod[s], h[s])
        cin = hcar[:, cols]
        ends = hcum + pcum * cin
        h[s] = jnp.where(ridx == 0, cin, pltpu.roll(ends, 1, axis=0))
        hcar[:, cols] = jnp.broadcast_to(ends[SUBLANES - 1:SUBLANES, :], (SUBLANES, LANES))
    for i in range(S):
        for s in range(nslab):
            h[s] = step(a_pad, s, i) * h[s] + step(b_pad, s, i)
            h_pad[s, pl.ds(i, SUBLANES, stride=SCAN_PITCH), :] = h[s]
    return jnp.concatenate(
        [jnp.concatenate([h_pad[s, SCAN_PITCH * j:SCAN_PITCH * j + S, :] for j in range(SUBLANES)], axis=0)
         for s in range(nslab)], axis=1)


def _ssd_cumdecay(dt, a2_row, tri):
    cum2 = jnp.dot(tri, dt * a2_row, precision=HI, preferred_element_type=F32)
    return cum2, cum2.T[0:N_SSD_HEADS, :]


def _ssd_diag(xs, bm, cm, dt, cols, cum2_t, mask_add):
    L = xs.shape[0]
    c2_t = cum2_t - jnp.log2(dt.T[0:N_SSD_HEADS, :])
    lane = lax.broadcasted_iota(jnp.int32, (L, LANES), 1)
    lo = lane < SSD_HEAD_DIM
    y_parts = []
    for g in range(N_SSD_GROUPS):
        bg = bm[:, D_STATE * g:D_STATE * (g + 1)].astype(BF16)
        cg = cm[:, D_STATE * g:D_STATE * (g + 1)].astype(BF16)
        cb = lax.dot_general(cg, bg, (((1,), (1,)), ((), ())), preferred_element_type=F32)
        for jj in range(N_SSD_HEADS // N_SSD_GROUPS // 2):
            j = (N_SSD_HEADS // N_SSD_GROUPS // 2) * g + jj
            h0, h1 = 2 * j, 2 * j + 1
            col0 = cols[:, LANES * h0:LANES * (h0 + 1)]
            col1 = cols[:, LANES * h1:LANES * (h1 + 1)]
            m0 = cb * jnp.exp2(col0 - c2_t[h0:h0 + 1, :] + mask_add)
            m1 = cb * jnp.exp2(col1 - c2_t[h1:h1 + 1, :] + mask_add)
            lhs = jnp.concatenate([m0, m1], axis=1).astype(BF16)
            xp = xs[:, LANES * j:LANES * (j + 1)]
            rhs = jnp.concatenate([jnp.where(lo, xp, 0.0), jnp.where(lo, 0.0, xp)], axis=0).astype(BF16)
            y_parts.append(jnp.dot(lhs, rhs, preferred_element_type=F32))
    return jnp.concatenate(y_parts, axis=1)


def _spread(v, sel_ref):
    p0 = v.astype(BF16)
    p1 = (v - p0.astype(F32)).astype(BF16)
    return jnp.dot(jnp.concatenate([p0, p1], axis=1), sel_ref[...], preferred_element_type=F32)


def _ssd_gate_norm(ys, xs, z_act, dskip, g_ssd):
    ys = ys + dskip * xs
    gated = ys * z_act
    half = SSD_WIDTH // N_SSD_GROUPS
    outs = []
    for g in range(N_SSD_GROUPS):
        outs.append(_rms(gated[:, half * g:half * (g + 1)], g_ssd[:, half * g:half * (g + 1)]))
    return jnp.concatenate(outs, axis=1)


def _split_w_in_kernel(wt_ref, wdt_t_ref, main_ref, dt_ref):
    main_ref[...] = wt_ref[...].T.astype(BF16)

    @pl.when(pl.program_id(0) == 0)
    def _dt():
        dt_ref[...] = jnp.zeros_like(dt_ref)
        dt_ref[:, 0:N_SSD_HEADS] = wdt_t_ref[...].T.astype(BF16)


def _split_w_in(w_t):
    cols, rows = w_t.shape
    blk = SPLIT_COLS
    return pl.pallas_call(
        _split_w_in_kernel,
        out_shape=(jax.ShapeDtypeStruct((rows, PROJ_MAIN), BF16), jax.ShapeDtypeStruct((rows, DT_PAD), BF16)),
        grid=(PROJ_MAIN // blk,),
        in_specs=[pl.BlockSpec((blk, rows), lambda j: (j, 0)),
                  pl.BlockSpec((N_SSD_HEADS, rows), lambda j: (0, 0))],
        out_specs=(pl.BlockSpec((rows, blk), lambda j: (0, j)), pl.BlockSpec((rows, DT_PAD), lambda j: (0, 0))),
        compiler_params=pltpu.CompilerParams(dimension_semantics=("arbitrary",)),
        name="split_w_in",
    )(w_t, w_t[PROJ_MAIN:, :])


def _inproj_kernel(x_ref, g_ref, w_ref, wdt_ref, o_ref):
    hn = _rms(x_ref[...].reshape(-1, D_MODEL), g_ref[...]).astype(BF16)
    o_ref[:, 0:PROJ_MAIN] = jnp.dot(hn, w_ref[...], preferred_element_type=F32)
    o_ref[:, PROJ_MAIN:PROJ_PAD] = jnp.dot(hn, wdt_ref[...], preferred_element_type=F32)


def _in_proj(x, g_mix, w_main, w_dt):
    nseq, ntok, _ = x.shape
    n = nseq * ntok
    return pl.pallas_call(
        _inproj_kernel,
        out_shape=jax.ShapeDtypeStruct((n, PROJ_PAD), F32),
        grid=(n // ROW_TILE,),
        in_specs=[
            pl.BlockSpec((ROW_TILE // ntok, ntok, D_MODEL), lambda i: (i, 0, 0)),
            pl.BlockSpec((1, D_MODEL), lambda i: (0, 0)),
            pl.BlockSpec((D_MODEL, PROJ_MAIN), lambda i: (0, 0), pipeline_mode=pl.Buffered(1)),
            pl.BlockSpec((D_MODEL, DT_PAD), lambda i: (0, 0), pipeline_mode=pl.Buffered(1)),
        ],
        out_specs=pl.BlockSpec((ROW_TILE, PROJ_PAD), lambda i: (i, 0)),
        compiler_params=pltpu.CompilerParams(
            dimension_semantics=("parallel",), vmem_limit_bytes=VMEM_LIMIT),
        name="in_proj",
    )(x, g_mix, w_main, w_dt)


def _inproj_prompt_kernel(x_ref, g_ref, w_ref, wdt_ref, lcw_ref, lcb_ref, scw_ref, scb_ref, dtb_ref,
                          o_ref, olc_ref, osc_ref, ext_l, ext_s, *, steps_per_seq):
    t = lax.rem(pl.program_id(0), steps_per_seq)
    rows = ROW_TILE
    hist = SUBLANES
    o1, o2, o3 = LRU_WIDTH, 2 * LRU_WIDTH, 2 * LRU_WIDTH + SSD_WIDTH

    @pl.when(t == 0)
    def _init():
        ext_l[:, 0:hist, :] = jnp.zeros((ext_l.shape[0], hist, LANES), F32)
        ext_s[:, 0:hist, :] = jnp.zeros((ext_s.shape[0], hist, LANES), F32)

    hn = _rms(x_ref[...], g_ref[...]).astype(BF16)
    lx = jnp.dot(hn, w_ref[:, 0:o1], preferred_element_type=F32)
    for s in range(ext_l.shape[0]):
        ext_l[s, hist:hist + rows, :] = lx[:, LANES * s:LANES * (s + 1)]
    xbc_in = jnp.dot(hn, w_ref[:, o3:PROJ_MAIN], preferred_element_type=F32)
    for s in range(ext_s.shape[0]):
        ext_s[s, hist:hist + rows, :] = xbc_in[:, LANES * s:LANES * (s + 1)]
    o_ref[:, o1:o2] = _gelu_tanh(jnp.dot(hn, w_ref[:, o1:o2], preferred_element_type=F32))
    z = jnp.dot(hn, w_ref[:, o2:o3], preferred_element_type=F32)
    o_ref[:, o2:o3] = z * _sigmoid(z)
    o_ref[:, PROJ_MAIN:PROJ_PAD] = _softplus(
        jnp.dot(hn, wdt_ref[...], preferred_element_type=F32) + dtb_ref[...])
    o_ref[:, 0:o1] = _conv_slabs(ext_l, lcw_ref, lcb_ref, rows, hist - (CONV_WIDTH - 1))
    xbc = _conv_slabs(ext_s, scw_ref, scb_ref, rows, hist - (CONV_WIDTH - 1))
    o_ref[:, o3:PROJ_MAIN] = xbc * _sigmoid(xbc)

    @pl.when(t == steps_per_seq - 1)
    def _final():
        last = slice(hist + rows - (CONV_WIDTH - 1), hist + rows)
        for s in range(ext_l.shape[0]):
            olc_ref[:, LANES * s:LANES * (s + 1)] = ext_l[s, last, :]
        for s in range(ext_s.shape[0]):
            osc_ref[:, LANES * s:LANES * (s + 1)] = ext_s[s, last, :]

    tail_l = ext_l[:, rows:rows + hist, :]
    tail_s = ext_s[:, rows:rows + hist, :]
    ext_l[:, 0:hist, :] = tail_l
    ext_s[:, 0:hist, :] = tail_s


def _in_proj_prompt(x2d, bsz, g_mix, w_main, w_dt, lcw, lcb, scw, scb, dtb):
    n = x2d.shape[0]
    steps_per_seq = n // bsz // ROW_TILE
    const = lambda i: (0, 0)
    return pl.pallas_call(
        functools.partial(_inproj_prompt_kernel, steps_per_seq=steps_per_seq),
        out_shape=(
            jax.ShapeDtypeStruct((n, PROJ_PAD), F32),
            jax.ShapeDtypeStruct((bsz, CONV_WIDTH - 1, LRU_WIDTH), F32),
            jax.ShapeDtypeStruct((bsz, CONV_WIDTH - 1, SSD_CONV_DIM), F32),
        ),
        grid=(n // ROW_TILE,),
        in_specs=[
            pl.BlockSpec((ROW_TILE, D_MODEL), lambda i: (i, 0)),
            pl.BlockSpec((1, D_MODEL), const),
            pl.BlockSpec((D_MODEL, PROJ_MAIN), const, pipeline_mode=pl.Buffered(1)),
            pl.BlockSpec((D_MODEL, DT_PAD), const, pipeline_mode=pl.Buffered(1)),
            pl.BlockSpec((CONV_WIDTH, LRU_WIDTH), const),
            pl.BlockSpec((1, LRU_WIDTH), const),
            pl.BlockSpec((CONV_WIDTH, SSD_CONV_DIM), const),
            pl.BlockSpec((1, SSD_CONV_DIM), const),
            pl.BlockSpec((1, DT_PAD), const),
        ],
        out_specs=(
            pl.BlockSpec((ROW_TILE, PROJ_PAD), lambda i: (i, 0)),
            pl.BlockSpec((None, CONV_WIDTH - 1, LRU_WIDTH), lambda i: (i // steps_per_seq, 0, 0)),
            pl.BlockSpec((None, CONV_WIDTH - 1, SSD_CONV_DIM), lambda i: (i // steps_per_seq, 0, 0)),
        ),
        scratch_shapes=[
            pltpu.VMEM((LRU_WIDTH // LANES, SUBLANES + ROW_TILE, LANES), F32),
            pltpu.VMEM((SSD_CONV_DIM // LANES, SUBLANES + ROW_TILE, LANES), F32),
        ],
        compiler_params=pltpu.CompilerParams(
            dimension_semantics=("arbitrary",), vmem_limit_bytes=VMEM_LIMIT),
        name="in_proj_prompt",
    )(x2d, g_mix, w_main, w_dt, lcw, lcb, scw, scb, dtb)


def _outmlp_kernel(x_ref, y_ref, wo_ref, gm_ref, wu_ref, wd_ref, gf_ref, o_ref):
    x = x_ref[...].reshape(-1, D_MODEL)
    x1 = x + jnp.dot(y_ref[...].astype(BF16), wo_ref[...], preferred_element_type=F32)
    m = _rms(x1, gm_ref[...]).astype(BF16)
    u = jnp.dot(m, wu_ref[...], preferred_element_type=F32)
    u = jnp.square(jnp.maximum(u, 0.0)).astype(BF16)
    x2 = x1 + jnp.dot(u, wd_ref[...], preferred_element_type=F32)
    o_ref[...] = _rms(x2, gf_ref[...]).reshape(o_ref.shape)


def _out_mlp(x, ymix2d, w_out_b, g_mlp, w_up_b, w_down_b, g_final):
    n = ymix2d.shape[0]
    const = lambda i: (0, 0)
    if x.ndim == 2:
        x_spec = pl.BlockSpec((ROW_TILE, D_MODEL), lambda i: (i, 0))
    else:
        x_spec = pl.BlockSpec((ROW_TILE // x.shape[1], x.shape[1], D_MODEL), lambda i: (i, 0, 0))
    return pl.pallas_call(
        _outmlp_kernel,
        out_shape=jax.ShapeDtypeStruct(x.shape, F32),
        grid=(n // ROW_TILE,),
        in_specs=[
            x_spec,
            pl.BlockSpec((ROW_TILE, MIX_WIDTH), lambda i: (i, 0)),
            pl.BlockSpec((MIX_WIDTH, D_MODEL), const, pipeline_mode=pl.Buffered(1)),
            pl.BlockSpec((1, D_MODEL), const),
            pl.BlockSpec((D_MODEL, D_FF), const, pipeline_mode=pl.Buffered(1)),
            pl.BlockSpec((D_FF, D_MODEL), const, pipeline_mode=pl.Buffered(1)),
            pl.BlockSpec((1, D_MODEL), const),
        ],
        out_specs=x_spec,
        compiler_params=pltpu.CompilerParams(
            dimension_semantics=("parallel",), vmem_limit_bytes=VMEM_LIMIT),
        name="out_mlp",
    )(x, ymix2d, w_out_b, g_mlp, w_up_b, w_down_b, g_final)


def _mixer_prompt_kernel(u_ref, gl_ref, zact_ref, xbc_ref, dt_ref,
                         wg_ref, ba_ref, bx_ref, lam_ref, glru_ref, alog_ref, dskip_ref, gssd_ref,
                         selt_ref, selp_ref,
                         y_ref, olh_ref, osh_ref,
                         a_pad, b_pad, h_pad, hcar, ht):
    t = pl.program_id(1)
    nt = pl.num_programs(1)
    tc = PROMPT_TC

    @pl.when(t == 0)
    def _init():
        hcar[...] = jnp.zeros_like(hcar)
        ht[...] = jnp.zeros_like(ht)

    neg_c_sp = (-LRU_C) * _softplus(-lam_ref[...])
    lane1 = lax.broadcasted_iota(jnp.int32, (1, LANES), 1)
    a2_row = jnp.where(lane1 < N_SSD_HEADS, -LOG2E * jnp.exp(alog_ref[...]), 0.0)
    L = SSD_CHUNK
    rr = lax.broadcasted_iota(jnp.int32, (L, L), 0)
    cc = lax.broadcasted_iota(jnp.int32, (L, L), 1)
    causal = cc <= rr
    tri = jnp.where(causal, 1.0, 0.0).astype(F32)
    mask_add = jnp.where(causal, 0.0, NEG_BIG).astype(F32)
    half = SSD_WIDTH // N_SSD_GROUPS

    chunks = [(n, c) for n in range(PROMPT_NB) for c in range(tc // L)]
    cums = [_ssd_cumdecay(dt_ref[n, L * c:L * (c + 1), :], a2_row, tri) for n, c in chunks]
    dts = [dt_ref[n, L * c:L * (c + 1), :] for n, c in chunks]
    cum_all = jnp.concatenate([cum2 for cum2, _ in cums], axis=0)
    cols_all = _spread(cum_all, selt_ref)
    ecol_all = _spread(jnp.exp2(cum_all), selp_ref)
    sdt_all = _spread(jnp.concatenate(
        [jnp.exp2(cum2[L - 1:L, :] - cum2) * dt for (cum2, _), dt in zip(cums, dts)], axis=0), selp_ref)

    for n in range(PROMPT_NB):
        a, b = _lru_coeffs(u_ref[n], wg_ref, ba_ref[...], bx_ref[...], neg_c_sp)
        hseq = _lru_scan_strided(a, b, hcar.at[n], a_pad.at[n], b_pad.at[n], h_pad.at[n])
        y_ref[n, :, 0:LRU_WIDTH] = _rms(hseq * gl_ref[n], glru_ref[...])

        for c in range(tc // L):
            k = chunks.index((n, c))
            rows = slice(L * c, L * (c + 1))
            krows = slice(L * k, L * (k + 1))
            xs = xbc_ref[n, rows, 0:SSD_WIDTH]
            bm = xbc_ref[n, rows, SSD_WIDTH:SSD_WIDTH + N_SSD_GROUPS * D_STATE]
            cm = xbc_ref[n, rows, SSD_WIDTH + N_SSD_GROUPS * D_STATE:SSD_CONV_DIM]
            y_diag = _ssd_diag(xs, bm, cm, dts[k], cols_all[krows, :], cums[k][1], mask_add)
            ecol = ecol_all[krows, :]
            xw = xs * sdt_all[krows, :]
            dec = ecol[L - 1:L, :]
            y_off_parts = []
            for g in range(N_SSD_GROUPS):
                htg = ht[n, g]
                cg = cm[:, D_STATE * g:D_STATE * (g + 1)].astype(BF16)
                y_off_parts.append(jnp.dot(cg, htg.astype(BF16), preferred_element_type=F32))
                bg_t = bm[:, D_STATE * g:D_STATE * (g + 1)].T.astype(BF16)
                st = jnp.dot(bg_t, xw[:, half * g:half * (g + 1)].astype(BF16), preferred_element_type=F32)
                ht[n, g] = htg * dec[:, half * g:half * (g + 1)] + st
            ys = y_diag + jnp.concatenate(y_off_parts, axis=1) * ecol
            y_ref[n, rows, LRU_WIDTH:MIX_WIDTH] = _ssd_gate_norm(
                ys, xs, zact_ref[n, rows, :], dskip_ref[...], gssd_ref[...])

    @pl.when(t == nt - 1)
    def _final():
        for n in range(PROMPT_NB):
            olh_ref[n] = hcar[n, 0:1, :]
            for g in range(N_SSD_GROUPS):
                osh_ref[n, half * g:half * (g + 1), :] = ht[n, g].T


def _param_specs(const):
    return [
        pl.BlockSpec((CONV_WIDTH, LRU_WIDTH), const),
        pl.BlockSpec((1, LRU_WIDTH), const),
        pl.BlockSpec((LRU_WIDTH // MXU_DIM, MXU_DIM, 2 * MXU_DIM), lambda *_: (0, 0, 0)),
        pl.BlockSpec((1, LRU_WIDTH), const),
        pl.BlockSpec((1, LRU_WIDTH), const),
        pl.BlockSpec((1, LRU_WIDTH), const),
        pl.BlockSpec((1, LRU_WIDTH), const),
        pl.BlockSpec((CONV_WIDTH, SSD_CONV_DIM), const),
        pl.BlockSpec((1, SSD_CONV_DIM), const),
        pl.BlockSpec((1, DT_PAD), const),
        pl.BlockSpec((1, DT_PAD), const),
        pl.BlockSpec((1, SSD_WIDTH), const),
        pl.BlockSpec((1, SSD_WIDTH), const),
    ]


def _head_selectors():
    k = jnp.arange(2 * LANES)[:, None] % LANES
    sel_t = (k == jnp.arange(N_SSD_HEADS * LANES)[None, :] // LANES).astype(BF16)
    sel_p = (k == jnp.arange(SSD_WIDTH)[None, :] // SSD_HEAD_DIM).astype(BF16)
    return sel_t, sel_p


def _mixer_prompt(act, wg, b_a, b_x, lam, g_lru, a_log, d_skip, g_ssd, sel_t, sel_p):
    bsz, seq, _ = act.shape
    tc = PROMPT_TC
    nb = PROMPT_NB
    const = lambda b, t: (0, 0)
    in_specs = [
        pl.BlockSpec((nb, tc, LRU_WIDTH), lambda b, t: (b, t, 0)),
        pl.BlockSpec((nb, tc, LRU_WIDTH), lambda b, t: (b, t, 1)),
        pl.BlockSpec((nb, tc, SSD_WIDTH), lambda b, t: (b, t, 2)),
        pl.BlockSpec((nb, tc, SSD_CONV_DIM), lambda b, t: (b, t, 2)),
        pl.BlockSpec((nb, tc, DT_PAD), lambda b, t: (b, t, PROJ_MAIN // DT_PAD)),
        pl.BlockSpec((LRU_WIDTH // MXU_DIM, MXU_DIM, 2 * MXU_DIM), lambda b, t: (0, 0, 0)),
        pl.BlockSpec((1, LRU_WIDTH), const),
        pl.BlockSpec((1, LRU_WIDTH), const),
        pl.BlockSpec((1, LRU_WIDTH), const),
        pl.BlockSpec((1, LRU_WIDTH), const),
        pl.BlockSpec((1, DT_PAD), const),
        pl.BlockSpec((1, SSD_WIDTH), const),
        pl.BlockSpec((1, SSD_WIDTH), const),
        pl.BlockSpec((2 * LANES, N_SSD_HEADS * LANES), const),
        pl.BlockSpec((2 * LANES, SSD_WIDTH), const),
    ]
    out_shape = (
        jax.ShapeDtypeStruct((bsz, seq, MIX_WIDTH), F32),
        jax.ShapeDtypeStruct((bsz, 1, LRU_WIDTH), F32),
        jax.ShapeDtypeStruct((bsz, SSD_WIDTH, D_STATE), F32),
    )
    out_specs = (
        pl.BlockSpec((nb, tc, MIX_WIDTH), lambda b, t: (b, t, 0)),
        pl.BlockSpec((nb, 1, LRU_WIDTH), lambda b, t: (b, 0, 0)),
        pl.BlockSpec((nb, SSD_WIDTH, D_STATE), lambda b, t: (b, 0, 0)),
    )
    scratch = [
        pltpu.VMEM((nb, LRU_WIDTH // LANES, SUBLANES * SCAN_PITCH, LANES), F32),
        pltpu.VMEM((nb, LRU_WIDTH // LANES, SUBLANES * SCAN_PITCH, LANES), F32),
        pltpu.VMEM((nb, LRU_WIDTH // LANES, SUBLANES * SCAN_PITCH, LANES), F32),
        pltpu.VMEM((nb, SUBLANES, LRU_WIDTH), F32),
        pltpu.VMEM((nb, N_SSD_GROUPS, D_STATE, SSD_WIDTH // N_SSD_GROUPS), F32),
    ]
    return pl.pallas_call(
        _mixer_prompt_kernel,
        out_shape=out_shape,
        grid=(bsz // nb, seq // tc),
        in_specs=in_specs,
        out_specs=out_specs,
        scratch_shapes=scratch,
        compiler_params=pltpu.CompilerParams(
            dimension_semantics=("parallel", "arbitrary"), vmem_limit_bytes=VMEM_LIMIT),
        name="mixer_prompt",
    )(act, act, act, act, act, wg, b_a, b_x, lam, g_lru, a_log, d_skip, g_ssd, sel_t, sel_p)


def _mixer_sample_kernel(lx_ref, gate_ref, z_ref, xbc_ref, dt_ref,
                         slc_ref, slh_ref, ssc_ref, ssh_ref,
                         lcw_ref, lcb_ref, wg_ref, ba_ref, bx_ref, lam_ref, glru_ref,
                         scw_ref, scb_ref, dtb_ref, alog_ref, dskip_ref, gssd_ref, selt_ref, selp_ref,
                         y_ref, olc_ref, olh_ref, osc_ref, osh_ref,
                         ext_l, ext_s, pad_scr, yoff_scr, *, T):
    S = SAMPLE_SEQS
    P = SUBLANES
    K1 = CONV_WIDTH - 1
    R = S * P
    row_i = lax.broadcasted_iota(jnp.int32, (R, 1), 0) & (P - 1)
    valid = row_i < T

    def pad_rows(ref):
        width = ref.shape[-1]
        pad_scr[:, :, 0:width] = jnp.zeros((S, P, width), F32)
        pad_scr[:, 0:T, 0:width] = ref[...].reshape(S, T, width)
        return pad_scr[:, :, 0:width].reshape(R, width)

    ext_l[...] = jnp.zeros_like(ext_l)
    ext_s[...] = jnp.zeros_like(ext_s)
    for k in range(K1):
        ext_l[:, k, :] = slc_ref[k]
    ext_l[:, K1:K1 + T, :] = lx_ref[...].reshape(S, T, LRU_WIDTH)
    for k in range(K1):
        ext_s[:, k, :] = ssc_ref[k]
    ext_s[:, K1:K1 + T, :] = xbc_ref[...].reshape(S, T, SSD_CONV_DIM)
    for k in range(K1):
        olc_ref[k] = ext_l[:, T + k, :]
        osc_ref[k] = ext_s[:, T + k, :]

    el = ext_l[...].reshape(R, LRU_WIDTH)
    es = ext_s[...].reshape(R, SSD_CONV_DIM)

    def conv(e, w_ref, b_ref):
        out = b_ref[...] + e * w_ref[0:1, :]
        for k in range(1, CONV_WIDTH):
            out = out + pltpu.roll(e, R - k, axis=0) * w_ref[k:k + 1, :]
        return out

    u = conv(el, lcw_ref, lcb_ref)
    neg_c_sp = (-LRU_C) * _softplus(-lam_ref[...])
    a, b = _lru_coeffs(u, wg_ref, ba_ref[...], bx_ref[...], neg_c_sp)
    a, b = _scan_within_8(a, b)
    h0 = jnp.broadcast_to(slh_ref[...], (S, P, LRU_WIDTH)).reshape(R, LRU_WIDTH)
    hseq = a * h0 + b
    olh_ref[...] = hseq.reshape(S, P, LRU_WIDTH)[:, T - 1:T, :]
    gate = pad_rows(gate_ref)
    y_lru = _rms(hseq * _gelu_tanh(gate), glru_ref[...])

    xbc = conv(es, scw_ref, scb_ref)
    xbc = xbc * _sigmoid(xbc)
    xs = xbc[:, 0:SSD_WIDTH]
    bm = xbc[:, SSD_WIDTH:SSD_WIDTH + N_SSD_GROUPS * D_STATE]
    cm = xbc[:, SSD_WIDTH + N_SSD_GROUPS * D_STATE:]
    dt_raw = pad_rows(dt_ref)
    dt = jnp.where(valid, _softplus(dt_raw + dtb_ref[...]), 0.0)
    lane1 = lax.broadcasted_iota(jnp.int32, (1, LANES), 1)
    a2_row = jnp.where(lane1 < N_SSD_HEADS, -LOG2E * jnp.exp(alog_ref[...]), 0.0)

    rr = lax.broadcasted_iota(jnp.int32, (R, R), 0)
    cc = lax.broadcasted_iota(jnp.int32, (R, R), 1)
    allowed = (cc <= rr) & ((rr - cc) <= (rr & (P - 1)))
    tri = jnp.where(allowed, 1.0, 0.0).astype(F32)
    mask_add = jnp.where(allowed, 0.0, NEG_BIG).astype(F32)

    cum2, cum2_t = _ssd_cumdecay(dt, a2_row, tri)
    y_diag = _ssd_diag(xs, bm, cm, dt, _spread(cum2, selt_ref), cum2_t, mask_add)
    ecol = _spread(jnp.exp2(cum2), selp_ref)
    end2 = jnp.broadcast_to(cum2.reshape(S, P, LANES)[:, P - 1:P, :], (S, P, LANES)).reshape(R, LANES)
    xw = xs * _spread(jnp.exp2(end2 - cum2) * dt, selp_ref)
    ecum_t = jnp.exp2(cum2_t)

    half = SSD_WIDTH // N_SSD_GROUPS
    for q in range(S):
        r0 = P * q
        vq = jnp.broadcast_to(ecum_t[:, r0 + P - 1:r0 + P], (N_SSD_HEADS, LANES))
        for g in range(N_SSD_GROUPS):
            hqg = ssh_ref[q, half * g:half * (g + 1), :]
            cq = cm[r0:r0 + P, D_STATE * g:D_STATE * (g + 1)].astype(BF16)
            yoff_scr[r0:r0 + P, half * g:half * (g + 1)] = lax.dot_general(
                cq, hqg.astype(BF16), (((1,), (1,)), ((), ())), preferred_element_type=F32)
            bq = bm[r0:r0 + P, D_STATE * g:D_STATE * (g + 1)].astype(BF16)
            xq = xw[r0:r0 + P, half * g:half * (g + 1)].astype(BF16)
            st = lax.dot_general(xq, bq, (((0,), (0,)), ((), ())), preferred_element_type=F32)
            for e in range(N_SSD_HEADS // N_SSD_GROUPS):
                h = (N_SSD_HEADS // N_SSD_GROUPS) * g + e
                lo_r = SSD_HEAD_DIM * e
                osh_ref[q, SSD_HEAD_DIM * h:SSD_HEAD_DIM * (h + 1), :] = (
                    vq[h:h + 1, :] * hqg[lo_r:lo_r + SSD_HEAD_DIM, :] + st[lo_r:lo_r + SSD_HEAD_DIM, :])

    ys = y_diag + yoff_scr[...] * ecol
    z = pad_rows(z_ref)
    y_ssd = _ssd_gate_norm(ys, xs, z * _sigmoid(z), dskip_ref[...], gssd_ref[...])
    y_ref[:, 0:LRU_WIDTH] = y_lru.reshape(S, P, LRU_WIDTH)[:, 0:T, :].reshape(S * T, LRU_WIDTH)
    y_ref[:, LRU_WIDTH:MIX_WIDTH] = y_ssd.reshape(S, P, SSD_WIDTH)[:, 0:T, :].reshape(S * T, SSD_WIDTH)


def _mixer_sample(proj, T, st_lc, st_lh, st_sc, st_sh, params, sel_t, sel_p):
    nseq = proj.shape[0] // T
    S = SAMPLE_SEQS
    const = lambda i: (0, 0)
    in_specs = [
        pl.BlockSpec((S * T, LRU_WIDTH), lambda i: (i, 0)),
        pl.BlockSpec((S * T, LRU_WIDTH), lambda i: (i, 1)),
        pl.BlockSpec((S * T, SSD_WIDTH), lambda i: (i, 2)),
        pl.BlockSpec((S * T, SSD_CONV_DIM), lambda i: (i, 2)),
        pl.BlockSpec((S * T, DT_PAD), lambda i: (i, PROJ_MAIN // DT_PAD)),
        pl.BlockSpec((CONV_WIDTH - 1, S, LRU_WIDTH), lambda i: (0, i, 0)),
        pl.BlockSpec((S, 1, LRU_WIDTH), lambda i: (i, 0, 0)),
        pl.BlockSpec((CONV_WIDTH - 1, S, SSD_CONV_DIM), lambda i: (0, i, 0)),
        pl.BlockSpec((S, SSD_WIDTH, D_STATE), lambda i: (i, 0, 0)),
    ] + _param_specs(const) + [
        pl.BlockSpec((2 * LANES, N_SSD_HEADS * LANES), const),
        pl.BlockSpec((2 * LANES, SSD_WIDTH), const),
    ]
    out_shape = (
        jax.ShapeDtypeStruct((nseq * T, MIX_WIDTH), F32),
        jax.ShapeDtypeStruct((CONV_WIDTH - 1, nseq, LRU_WIDTH), F32),
        jax.ShapeDtypeStruct((nseq, 1, LRU_WIDTH), F32),
        jax.ShapeDtypeStruct((CONV_WIDTH - 1, nseq, SSD_CONV_DIM), F32),
        jax.ShapeDtypeStruct((nseq, SSD_WIDTH, D_STATE), F32),
    )
    out_specs = (
        pl.BlockSpec((S * T, MIX_WIDTH), lambda i: (i, 0)),
        pl.BlockSpec((CONV_WIDTH - 1, S, LRU_WIDTH), lambda i: (0, i, 0)),
        pl.BlockSpec((S, 1, LRU_WIDTH), lambda i: (i, 0, 0)),
        pl.BlockSpec((CONV_WIDTH - 1, S, SSD_CONV_DIM), lambda i: (0, i, 0)),
        pl.BlockSpec((S, SSD_WIDTH, D_STATE), lambda i: (i, 0, 0)),
    )
    scratch = [
        pltpu.VMEM((S, SUBLANES, LRU_WIDTH), F32),
        pltpu.VMEM((S, SUBLANES, SSD_CONV_DIM), F32),
        pltpu.VMEM((S, SUBLANES, LRU_WIDTH), F32),
        pltpu.VMEM((S * SUBLANES, SSD_WIDTH), F32),
    ]
    return pl.pallas_call(
        functools.partial(_mixer_sample_kernel, T=T),
        out_shape=out_shape,
        grid=(nseq // S,),
        in_specs=in_specs,
        out_specs=out_specs,
        scratch_shapes=scratch,
        compiler_params=pltpu.CompilerParams(
            dimension_semantics=("parallel",), vmem_limit_bytes=VMEM_LIMIT),
        name="mixer_sample",
    )(proj, proj, proj, proj, proj, st_lc, st_lh, st_sc, st_sh, *params, sel_t, sel_p)


def _gate_weights(w_a, w_x):
    def tiles(w):
        per = MXU_DIM // LRU_BLOCK
        w4 = w.reshape(N_LRU_HEADS // per, per, LRU_BLOCK, LRU_BLOCK)
        eye = jnp.eye(per, dtype=w.dtype)
        t = jnp.einsum('jaik,ab->jaibk', w4, eye)
        return t.reshape(N_LRU_HEADS // per, MXU_DIM, MXU_DIM)
    return jnp.concatenate([tiles(w_a), tiles(w_x)], axis=2).astype(BF16)


def kernel(x_prompt, x_sample, state_lru_conv, state_lru_h, state_ssd_conv, state_ssd_h, g_mix, w_in,
           lru_conv_w, lru_conv_b, w_a, b_a, w_x, b_x, lam, g_lru_out, ssd_conv_w, ssd_conv_b, dt_bias,
           a_log, d_skip, g_ssd_out, w_out, g_mlp, w_up, w_down, g_final):
    depth = w_in.shape[0]
    assert depth == 1
    bp, seq, _ = x_prompt.shape
    bs, dseq, _ = x_sample.shape
    l = 0
    row = lambda v: v.reshape(1, -1)
    w_main, w_dt = _split_w_in(jnp.swapaxes(w_in, 1, 2)[l])
    params = (
        lru_conv_w[l], row(lru_conv_b[l]), _gate_weights(w_a[l], w_x[l]),
        row(b_a[l]), row(b_x[l]), row(lam[l]), row(g_lru_out[l]),
        ssd_conv_w[l], row(ssd_conv_b[l]),
        jnp.pad(row(dt_bias[l]), ((0, 0), (0, DT_PAD - N_SSD_HEADS))),
        jnp.pad(row(a_log[l]), ((0, 0), (0, DT_PAD - N_SSD_HEADS))),
        row(jnp.repeat(d_skip[l], SSD_HEAD_DIM)), row(g_ssd_out[l]),
    )
    w_out_b = w_out[l].astype(BF16)
    w_up_b = w_up[l].astype(BF16)
    w_down_b = w_down[l].astype(BF16)
    gmix = row(g_mix[l])
    gmlp = row(g_mlp[l])
    gfin = row(g_final)

    xp2 = x_prompt.reshape(bp * seq, D_MODEL)
    (lcw, lcb, wg, ba, bx, lam_r, glru, scw, scb, dtb, alog, dskip, gssd) = params
    act_p, p_lc, p_sc = _in_proj_prompt(xp2, bp, gmix, w_main, w_dt, lcw, lcb, scw, scb, dtb)
    sel_t, sel_p = _head_selectors()
    ymix_p, p_lh, p_sh = _mixer_prompt(
        act_p.reshape(bp, seq, PROJ_PAD), wg, ba, bx, lam_r, glru, alog, dskip, gssd, sel_t, sel_p)
    y_prompt = _out_mlp(xp2, ymix_p.reshape(bp * seq, MIX_WIDTH), w_out_b, gmlp, w_up_b, w_down_b, gfin)

    proj_s = _in_proj(x_sample, gmix, w_main, w_dt)
    ymix_s, s_lc, s_lh, s_sc, s_sh = _mixer_sample(
        proj_s, dseq, jnp.swapaxes(state_lru_conv[l], 0, 1), state_lru_h[l].reshape(bs, 1, LRU_WIDTH),
        jnp.swapaxes(state_ssd_conv[l], 0, 1),
        state_ssd_h[l].reshape(bs, SSD_WIDTH, D_STATE), params, sel_t, sel_p)
    y_sample = _out_mlp(x_sample, ymix_s, w_out_b, gmlp, w_up_b, w_down_b, gfin)

    hshape = (N_SSD_HEADS, SSD_HEAD_DIM, D_STATE)
    return (
        y_prompt.reshape(bp, seq, D_MODEL), y_sample,
        p_lc[None], p_lh.reshape(1, bp, LRU_WIDTH), p_sc[None], p_sh.reshape(1, bp, *hshape),
        jnp.swapaxes(s_lc, 0, 1)[None], s_lh.reshape(1, bs, LRU_WIDTH), jnp.swapaxes(s_sc, 0, 1)[None],
        s_sh.reshape(1, bs, *hshape),
    )
```

```python
import functools
import math

import jax
import jax.numpy as jnp
import numpy as np
from jax import lax
from jax.experimental import pallas as pl
from jax.experimental.pallas import tpu as pltpu

F32 = jnp.float32
BF16 = jnp.bfloat16

D_MODEL = 1024
LRU_WIDTH = 1024
N_LRU_HEADS = 16
LRU_BLOCK = 64
LRU_C = 8.0
SSD_WIDTH = 1024
SSD_HEAD_DIM = 64
N_SSD_HEADS = 16
N_SSD_GROUPS = 2
D_STATE = 128
CONV_WIDTH = 4
SSD_CONV_DIM = SSD_WIDTH + 2 * N_SSD_GROUPS * D_STATE
D_FF = 4 * D_MODEL
EPS = 1e-6

LANES = 128
SUBLANES = 8
MXU_DIM = 256
DT_PAD = LANES
PROJ_MAIN = 2 * LRU_WIDTH + SSD_WIDTH + SSD_CONV_DIM
PROJ_PAD = PROJ_MAIN + DT_PAD
MIX_WIDTH = LRU_WIDTH + SSD_WIDTH
SSD_CHUNK = 128
PROMPT_TC = 256
PROMPT_NB = 2
ROW_TILE = 512
SPLIT_COLS = 1536
SAMPLE_SEQS = SSD_CHUNK // SUBLANES
SCAN_RUN = PROMPT_TC // SUBLANES
SCAN_PITCH = SCAN_RUN + 4
NEG_BIG = -1e30
LOG2E = 1.4426950408889634
VMEM_LIMIT = 56 * 1024 * 1024
HI = lax.Precision.HIGHEST


def _rms(x, g):
    ms = jnp.mean(x * x, axis=-1, keepdims=True)
    return x * lax.rsqrt(ms + EPS) * g


def _sigmoid(x):
    return 1.0 / (1.0 + jnp.exp(-x))


def _softplus(x):
    return jnp.maximum(x, 0.0) + jnp.log1p(jnp.exp(-jnp.abs(x)))


def _gelu_tanh(x):
    c = math.sqrt(2.0 / math.pi)
    return 0.5 * x * (1.0 + jnp.tanh(c * (x + 0.044715 * (x * x * x))))


def _lru_coeffs(u, wg_ref, b_a, b_x, neg_c_sp):
    ub = u.astype(BF16)
    r_parts, i_parts = [], []
    for j in range(LRU_WIDTH // MXU_DIM):
        g = jnp.dot(ub[:, MXU_DIM * j:MXU_DIM * (j + 1)], wg_ref[j], preferred_element_type=F32)
        r_parts.append(g[:, :MXU_DIM])
        i_parts.append(g[:, MXU_DIM:])
    r = _sigmoid(jnp.concatenate(r_parts, axis=1) + b_a)
    i = _sigmoid(jnp.concatenate(i_parts, axis=1) + b_x)
    log_a = r * neg_c_sp
    a = jnp.exp(log_a)
    th = jnp.tanh(log_a)
    v = (th + th) / (th - 1.0)
    mult = jnp.where(v > 0.0, v * lax.rsqrt(v), 0.0)
    return a, mult * (i * u)


def _scan_within_8(a, b):
    ridx = lax.broadcasted_iota(jnp.int32, a.shape, 0) & (SUBLANES - 1)
    for k in (1, 2, 4):
        a_s = pltpu.roll(a, k, axis=0)
        b_s = pltpu.roll(b, k, axis=0)
        m = ridx >= k
        b = jnp.where(m, a * b_s + b, b)
        a = jnp.where(m, a * a_s, a)
    return a, b


def _conv_slabs(ext, w_ref, b_ref, rows, first):
    parts = []
    for s in range(ext.shape[0]):
        cols = slice(LANES * s, LANES * (s + 1))
        acc = b_ref[:, cols] + ext[s, pl.ds(first, rows), :] * w_ref[0:1, cols]
        for k in range(1, CONV_WIDTH):
            acc = acc + ext[s, pl.ds(first + k, rows), :] * w_ref[k:k + 1, cols]
        parts.append(acc)
    return jnp.concatenate(parts, axis=1)


def _lru_scan_strided(a, b, hcar, a_pad, b_pad, h_pad):
    rows = a.shape[0]
    S = rows // SUBLANES
    nslab = LRU_WIDTH // LANES
    ridx = lax.broadcasted_iota(jnp.int32, (SUBLANES, LANES), 0)
    step = lambda ref, s, i: ref[s, pl.ds(i, SUBLANES, stride=SCAN_PITCH), :]
    for s in range(nslab):
        cols = slice(LANES * s, LANES * (s + 1))
        for j in range(SUBLANES):
            a_pad[s, SCAN_PITCH * j:SCAN_PITCH * j + S, :] = a[S * j:S * (j + 1), cols]
            b_pad[s, SCAN_PITCH * j:SCAN_PITCH * j + S, :] = b[S * j:S * (j + 1), cols]
    h = [jnp.zeros((SUBLANES, LANES), F32)] * nslab
    prod = [jnp.ones((SUBLANES, LANES), F32)] * nslab
    for i in range(S):
        for s in range(nslab):
            av = step(a_pad, s, i)
            h[s] = av * h[s] + step(b_pad, s, i)
            prod[s] = av * prod[s]
    for s in range(nslab):
        cols = slice(LANES * s, LANES * (s + 1))
        pcum, hcum = _scan_within_8(prod[s], h[s])
        cin = hcar[:, cols]
        ends = hcum + pcum * cin
        h[s] = jnp.where(ridx == 0, cin, pltpu.roll(ends, 1, axis=0))
        hcar[:, cols] = jnp.broadcast_to(ends[SUBLANES - 1:SUBLANES, :], (SUBLANES, LANES))
    for i in range(S):
        for s in range(nslab):
            h[s] = step(a_pad, s, i) * h[s] + step(b_pad, s, i)
            h_pad[s, pl.ds(i, SUBLANES, stride=SCAN_PITCH), :] = h[s]
    return jnp.concatenate(
        [jnp.concatenate([h_pad[s, SCAN_PITCH * j:SCAN_PITCH * j + S, :] for j in range(SUBLANES)], axis=0)
         for s in range(nslab)], axis=1)


def _ssd_cumdecay(dt, a2_row, tri):
    cum2 = jnp.dot(tri, dt * a2_row, precision=HI, preferred_element_type=F32)
    return cum2, cum2.T[0:N_SSD_HEADS, :]


def _ssd_diag(xs, bm, cm, dt, cols, cum2_t, mask_add):
    L = xs.shape[0]
    c2_t = cum2_t - jnp.log2(dt.T[0:N_SSD_HEADS, :])
    lane = lax.broadcasted_iota(jnp.int32, (L, LANES), 1)
    lo = lane < SSD_HEAD_DIM
    y_parts = []
    for g in range(N_SSD_GROUPS):
        bg = bm[:, D_STATE * g:D_STATE * (g + 1)].astype(BF16)
        cg = cm[:, D_STATE * g:D_STATE * (g + 1)].astype(BF16)
        cb = lax.dot_general(cg, bg, (((1,), (1,)), ((), ())), preferred_element_type=F32)
        for jj in range(N_SSD_HEADS // N_SSD_GROUPS // 2):
            j = (N_SSD_HEADS // N_SSD_GROUPS // 2) * g + jj
            h0, h1 = 2 * j, 2 * j + 1
            col0 = cols[:, LANES * h0:LANES * (h0 + 1)]
            col1 = cols[:, LANES * h1:LANES * (h1 + 1)]
            m0 = cb * jnp.exp2(col0 - c2_t[h0:h0 + 1, :] + mask_add)
            m1 = cb * jnp.exp2(col1 - c2_t[h1:h1 + 1, :] + mask_add)
            lhs = jnp.concatenate([m0, m1], axis=1).astype(BF16)
            xp = xs[:, LANES * j:LANES * (j + 1)]
            rhs = jnp.concatenate([jnp.where(lo, xp, 0.0), jnp.where(lo, 0.0, xp)], axis=0).astype(BF16)
            y_parts.append(jnp.dot(lhs, rhs, preferred_element_type=F32))
    return jnp.concatenate(y_parts, axis=1)


def _spread(v, sel_ref):
    p0 = v.astype(BF16)
    p1 = (v - p0.astype(F32)).astype(BF16)
    return jnp.dot(jnp.concatenate([p0, p1], axis=1), sel_ref[...], preferred_element_type=F32)


def _ssd_gate_norm(ys, xs, z_act, dskip, g_ssd):
    ys = ys + dskip * xs
    gated = ys * z_act
    half = SSD_WIDTH // N_SSD_GROUPS
    outs = []
    for g in range(N_SSD_GROUPS):
        outs.append(_rms(gated[:, half * g:half * (g + 1)], g_ssd[:, half * g:half * (g + 1)]))
    return jnp.concatenate(outs, axis=1)


def _split_w_in_kernel(wt_ref, wdt_t_ref, main_ref, dt_ref):
    main_ref[...] = wt_ref[...].T.astype(BF16)

    @pl.when(pl.program_id(0) == 0)
    def _dt():
        dt_ref[...] = jnp.zeros_like(dt_ref)
        dt_ref[:, 0:N_SSD_HEADS] = wdt_t_ref[...].T.astype(BF16)


def _split_w_in(w_t):
    cols, rows = w_t.shape
    blk = SPLIT_COLS
    return pl.pallas_call(
        _split_w_in_kernel,
        out_shape=(jax.ShapeDtypeStruct((rows, PROJ_MAIN), BF16), jax.ShapeDtypeStruct((rows, DT_PAD), BF16)),
        grid=(PROJ_MAIN // blk,),
        in_specs=[pl.BlockSpec((blk, rows), lambda j: (j, 0)),
                  pl.BlockSpec((N_SSD_HEADS, rows), lambda j: (PROJ_MAIN // N_SSD_HEADS, 0))],
        out_specs=(pl.BlockSpec((rows, blk), lambda j: (0, j)), pl.BlockSpec((rows, DT_PAD), lambda j: (0, 0))),
        compiler_params=pltpu.CompilerParams(dimension_semantics=("arbitrary",)),
        name="split_w_in",
    )(w_t, w_t)


def _inproj_kernel(x_ref, g_ref, w_ref, wdt_ref, o_ref):
    hn = _rms(x_ref[...].reshape(-1, D_MODEL), g_ref[...]).astype(BF16)
    o_ref[:, 0:PROJ_MAIN] = jnp.dot(hn, w_ref[...], preferred_element_type=F32)
    o_ref[:, PROJ_MAIN:PROJ_PAD] = jnp.dot(hn, wdt_ref[...], preferred_element_type=F32)


def _in_proj(x, g_mix, w_main, w_dt):
    nseq, ntok, _ = x.shape
    n = nseq * ntok
    return pl.pallas_call(
        _inproj_kernel,
        out_shape=jax.ShapeDtypeStruct((n, PROJ_PAD), F32),
        grid=(n // ROW_TILE,),
        in_specs=[
            pl.BlockSpec((ROW_TILE // ntok, ntok, D_MODEL), lambda i: (i, 0, 0)),
            pl.BlockSpec((1, D_MODEL), lambda i: (0, 0)),
            pl.BlockSpec((D_MODEL, PROJ_MAIN), lambda i: (0, 0), pipeline_mode=pl.Buffered(1)),
            pl.BlockSpec((D_MODEL, DT_PAD), lambda i: (0, 0), pipeline_mode=pl.Buffered(1)),
        ],
        out_specs=pl.BlockSpec((ROW_TILE, PROJ_PAD), lambda i: (i, 0)),
        compiler_params=pltpu.CompilerParams(
            dimension_semantics=("parallel",), vmem_limit_bytes=VMEM_LIMIT),
        name="in_proj",
    )(x, g_mix, w_main, w_dt)


def _inproj_prompt_kernel(x_ref, g_ref, w_ref, wdt_ref, lcw_ref, lcb_ref, scw_ref, scb_ref, dtb_ref,
                          o_ref, olc_ref, osc_ref, ext_l, ext_s, *, steps_per_seq):
    t = lax.rem(pl.program_id(0), steps_per_seq)
    rows = ROW_TILE
    hist = SUBLANES
    o1, o2, o3 = LRU_WIDTH, 2 * LRU_WIDTH, 2 * LRU_WIDTH + SSD_WIDTH

    @pl.when(t == 0)
    def _init():
        ext_l[:, 0:hist, :] = jnp.zeros((ext_l.shape[0], hist, LANES), F32)
        ext_s[:, 0:hist, :] = jnp.zeros((ext_s.shape[0], hist, LANES), F32)

    hn = _rms(x_ref[...], g_ref[...]).astype(BF16)
    lx = jnp.dot(hn, w_ref[:, 0:o1], preferred_element_type=F32)
    for s in range(ext_l.shape[0]):
        ext_l[s, hist:hist + rows, :] = lx[:, LANES * s:LANES * (s + 1)]
    xbc_in = jnp.dot(hn, w_ref[:, o3:PROJ_MAIN], preferred_element_type=F32)
    for s in range(ext_s.shape[0]):
        ext_s[s, hist:hist + rows, :] = xbc_in[:, LANES * s:LANES * (s + 1)]
    o_ref[:, o1:o2] = _gelu_tanh(jnp.dot(hn, w_ref[:, o1:o2], preferred_element_type=F32))
    z = jnp.dot(hn, w_ref[:, o2:o3], preferred_element_type=F32)
    o_ref[:, o2:o3] = z * _sigmoid(z)
    o_ref[:, PROJ_MAIN:PROJ_PAD] = _softplus(
        jnp.dot(hn, wdt_ref[...], preferred_element_type=F32) + dtb_ref[...])
    o_ref[:, 0:o1] = _conv_slabs(ext_l, lcw_ref, lcb_ref, rows, hist - (CONV_WIDTH - 1))
    xbc = _conv_slabs(ext_s, scw_ref, scb_ref, rows, hist - (CONV_WIDTH - 1))
    o_ref[:, o3:PROJ_MAIN] = xbc * _sigmoid(xbc)

    @pl.when(t == steps_per_seq - 1)
    def _final():
        last = slice(hist + rows - (CONV_WIDTH - 1), hist + rows)
        for s in range(ext_l.shape[0]):
            olc_ref[:, LANES * s:LANES * (s + 1)] = ext_l[s, last, :]
        for s in range(ext_s.shape[0]):
            osc_ref[:, LANES * s:LANES * (s + 1)] = ext_s[s, last, :]

    tail_l = ext_l[:, rows:rows + hist, :]
    tail_s = ext_s[:, rows:rows + hist, :]
    ext_l[:, 0:hist, :] = tail_l
    ext_s[:, 0:hist, :] = tail_s


def _in_proj_prompt(x2d, bsz, g_mix, w_main, w_dt, lcw, lcb, scw, scb, dtb):
    n = x2d.shape[0]
    steps_per_seq = n // bsz // ROW_TILE
    const = lambda i: (0, 0)
    return pl.pallas_call(
        functools.partial(_inproj_prompt_kernel, steps_per_seq=steps_per_seq),
        out_shape=(
            jax.ShapeDtypeStruct((n, PROJ_PAD), F32),
            jax.ShapeDtypeStruct((bsz, CONV_WIDTH - 1, LRU_WIDTH), F32),
            jax.ShapeDtypeStruct((bsz, CONV_WIDTH - 1, SSD_CONV_DIM), F32),
        ),
        grid=(n // ROW_TILE,),
        in_specs=[
            pl.BlockSpec((ROW_TILE, D_MODEL), lambda i: (i, 0)),
            pl.BlockSpec((1, D_MODEL), const),
            pl.BlockSpec((D_MODEL, PROJ_MAIN), const, pipeline_mode=pl.Buffered(1)),
            pl.BlockSpec((D_MODEL, DT_PAD), const, pipeline_mode=pl.Buffered(1)),
            pl.BlockSpec((CONV_WIDTH, LRU_WIDTH), const),
            pl.BlockSpec((1, LRU_WIDTH), const),
            pl.BlockSpec((CONV_WIDTH, SSD_CONV_DIM), const),
            pl.BlockSpec((1, SSD_CONV_DIM), const),
            pl.BlockSpec((1, DT_PAD), const),
        ],
        out_specs=(
            pl.BlockSpec((ROW_TILE, PROJ_PAD), lambda i: (i, 0)),
            pl.BlockSpec((None, CONV_WIDTH - 1, LRU_WIDTH), lambda i: (i // steps_per_seq, 0, 0)),
            pl.BlockSpec((None, CONV_WIDTH - 1, SSD_CONV_DIM), lambda i: (i // steps_per_seq, 0, 0)),
        ),
        scratch_shapes=[
            pltpu.VMEM((LRU_WIDTH // LANES, SUBLANES + ROW_TILE, LANES), F32),
            pltpu.VMEM((SSD_CONV_DIM // LANES, SUBLANES + ROW_TILE, LANES), F32),
        ],
        compiler_params=pltpu.CompilerParams(
            dimension_semantics=("arbitrary",), vmem_limit_bytes=VMEM_LIMIT),
        name="in_proj_prompt",
    )(x2d, g_mix, w_main, w_dt, lcw, lcb, scw, scb, dtb)


def _outmlp_kernel(x_ref, y_ref, wo_ref, gm_ref, wu_ref, wd_ref, gf_ref, o_ref):
    x = x_ref[...].reshape(-1, D_MODEL)
    x1 = x + jnp.dot(y_ref[...].astype(BF16), wo_ref[...], preferred_element_type=F32)
    m = _rms(x1, gm_ref[...]).astype(BF16)
    u = jnp.dot(m, wu_ref[...], preferred_element_type=F32)
    u = jnp.square(jnp.maximum(u, 0.0)).astype(BF16)
    x2 = x1 + jnp.dot(u, wd_ref[...], preferred_element_type=F32)
    o_ref[...] = _rms(x2, gf_ref[...]).reshape(o_ref.shape)


def _out_mlp(x, ymix2d, w_out_b, g_mlp, w_up_b, w_down_b, g_final):
    n = ymix2d.shape[0]
    const = lambda i: (0, 0)
    if x.ndim == 2:
        x_spec = pl.BlockSpec((ROW_TILE, D_MODEL), lambda i: (i, 0))
    else:
        x_spec = pl.BlockSpec((ROW_TILE // x.shape[1], x.shape[1], D_MODEL), lambda i: (i, 0, 0))
    return pl.pallas_call(
        _outmlp_kernel,
        out_shape=jax.ShapeDtypeStruct(x.shape, F32),
        grid=(n // ROW_TILE,),
        in_specs=[
            x_spec,
            pl.BlockSpec((ROW_TILE, MIX_WIDTH), lambda i: (i, 0)),
            pl.BlockSpec((MIX_WIDTH, D_MODEL), const, pipeline_mode=pl.Buffered(1)),
            pl.BlockSpec((1, D_MODEL), const),
            pl.BlockSpec((D_MODEL, D_FF), const, pipeline_mode=pl.Buffered(1)),
            pl.BlockSpec((D_FF, D_MODEL), const, pipeline_mode=pl.Buffered(1)),
            pl.BlockSpec((1, D_MODEL), const),
        ],
        out_specs=x_spec,
        compiler_params=pltpu.CompilerParams(
            dimension_semantics=("parallel",), vmem_limit_bytes=VMEM_LIMIT),
        name="out_mlp",
    )(x, ymix2d, w_out_b, g_mlp, w_up_b, w_down_b, g_final)


def _mixer_prompt_kernel(u_ref, gl_ref, zact_ref, xbc_ref, dt_ref,
                         wg_ref, ba_ref, bx_ref, lam_ref, glru_ref, alog_ref, dskip_ref, gssd_ref,
                         selt_ref, selp_ref,
                         y_ref, olh_ref, osh_ref,
                         a_pad, b_pad, h_pad, hcar, ht):
    t = pl.program_id(1)
    nt = pl.num_programs(1)
    tc = PROMPT_TC

    @pl.when(t == 0)
    def _init():
        hcar[...] = jnp.zeros_like(hcar)
        ht[...] = jnp.zeros_like(ht)

    neg_c_sp = (-LRU_C) * _softplus(-lam_ref[...])
    lane1 = lax.broadcasted_iota(jnp.int32, (1, LANES), 1)
    a2_row = jnp.where(lane1 < N_SSD_HEADS, -LOG2E * jnp.exp(alog_ref[...]), 0.0)
    L = SSD_CHUNK
    rr = lax.broadcasted_iota(jnp.int32, (L, L), 0)
    cc = lax.broadcasted_iota(jnp.int32, (L, L), 1)
    causal = cc <= rr
    tri = jnp.where(causal, 1.0, 0.0).astype(F32)
    mask_add = jnp.where(causal, 0.0, NEG_BIG).astype(F32)
    half = SSD_WIDTH // N_SSD_GROUPS

    chunks = [(n, c) for n in range(PROMPT_NB) for c in range(tc // L)]
    cums = [_ssd_cumdecay(dt_ref[n, L * c:L * (c + 1), :], a2_row, tri) for n, c in chunks]
    dts = [dt_ref[n, L * c:L * (c + 1), :] for n, c in chunks]
    cum_all = jnp.concatenate([cum2 for cum2, _ in cums], axis=0)
    cols_all = _spread(cum_all, selt_ref)
    ecol_all = _spread(jnp.exp2(cum_all), selp_ref)
    sdt_all = _spread(jnp.concatenate(
        [jnp.exp2(cum2[L - 1:L, :] - cum2) * dt for (cum2, _), dt in zip(cums, dts)], axis=0), selp_ref)

    for n in range(PROMPT_NB):
        a, b = _lru_coeffs(u_ref[n], wg_ref, ba_ref[...], bx_ref[...], neg_c_sp)
        hseq = _lru_scan_strided(a, b, hcar.at[n], a_pad.at[n], b_pad.at[n], h_pad.at[n])
        y_ref[n, :, 0:LRU_WIDTH] = _rms(hseq * gl_ref[n], glru_ref[...])

        for c in range(tc // L):
            k = chunks.index((n, c))
            rows = slice(L * c, L * (c + 1))
            krows = slice(L * k, L * (k + 1))
            xs = xbc_ref[n, rows, 0:SSD_WIDTH]
            bm = xbc_ref[n, rows, SSD_WIDTH:SSD_WIDTH + N_SSD_GROUPS * D_STATE]
            cm = xbc_ref[n, rows, SSD_WIDTH + N_SSD_GROUPS * D_STATE:SSD_CONV_DIM]
            y_diag = _ssd_diag(xs, bm, cm, dts[k], cols_all[krows, :], cums[k][1], mask_add)
            ecol = ecol_all[krows, :]
            xw = xs * sdt_all[krows, :]
            dec = ecol[L - 1:L, :]
            y_off_parts = []
            for g in range(N_SSD_GROUPS):
                htg = ht[n, g]
                cg = cm[:, D_STATE * g:D_STATE * (g + 1)].astype(BF16)
                y_off_parts.append(jnp.dot(cg, htg.astype(BF16), preferred_element_type=F32))
                bg_t = bm[:, D_STATE * g:D_STATE * (g + 1)].T.astype(BF16)
                st = jnp.dot(bg_t, xw[:, half * g:half * (g + 1)].astype(BF16), preferred_element_type=F32)
                ht[n, g] = htg * dec[:, half * g:half * (g + 1)] + st
            ys = y_diag + jnp.concatenate(y_off_parts, axis=1) * ecol
            y_ref[n, rows, LRU_WIDTH:MIX_WIDTH] = _ssd_gate_norm(
                ys, xs, zact_ref[n, rows, :], dskip_ref[...], gssd_ref[...])

    @pl.when(t == nt - 1)
    def _final():
        for n in range(PROMPT_NB):
            olh_ref[n] = hcar[n, 0:1, :]
            for g in range(N_SSD_GROUPS):
                osh_ref[n, half * g:half * (g + 1), :] = ht[n, g].T


def _param_specs(const):
    return [
        pl.BlockSpec((CONV_WIDTH, LRU_WIDTH), const),
        pl.BlockSpec((1, LRU_WIDTH), const),
        pl.BlockSpec((LRU_WIDTH // MXU_DIM, MXU_DIM, 2 * MXU_DIM), lambda *_: (0, 0, 0)),
        pl.BlockSpec((1, LRU_WIDTH), const),
        pl.BlockSpec((1, LRU_WIDTH), const),
        pl.BlockSpec((1, LRU_WIDTH), const),
        pl.BlockSpec((1, LRU_WIDTH), const),
        pl.BlockSpec((CONV_WIDTH, SSD_CONV_DIM), const),
        pl.BlockSpec((1, SSD_CONV_DIM), const),
        pl.BlockSpec((1, DT_PAD), const),
        pl.BlockSpec((1, DT_PAD), const),
        pl.BlockSpec((1, SSD_WIDTH), const),
        pl.BlockSpec((1, SSD_WIDTH), const),
    ]


def _head_selectors():
    k = np.arange(2 * LANES)[:, None] % LANES
    sel_t = (k == np.arange(N_SSD_HEADS * LANES)[None, :] // LANES).astype(np.float32)
    sel_p = (k == np.arange(SSD_WIDTH)[None, :] // SSD_HEAD_DIM).astype(np.float32)
    return jnp.asarray(sel_t, BF16), jnp.asarray(sel_p, BF16)


def _mixer_prompt(act, wg, b_a, b_x, lam, g_lru, a_log, d_skip, g_ssd, sel_t, sel_p):
    bsz, seq, _ = act.shape
    tc = PROMPT_TC
    nb = PROMPT_NB
    const = lambda b, t: (0, 0)
    in_specs = [
        pl.BlockSpec((nb, tc, LRU_WIDTH), lambda b, t: (b, t, 0)),
        pl.BlockSpec((nb, tc, LRU_WIDTH), lambda b, t: (b, t, 1)),
        pl.BlockSpec((nb, tc, SSD_WIDTH), lambda b, t: (b, t, 2)),
        pl.BlockSpec((nb, tc, SSD_CONV_DIM), lambda b, t: (b, t, 2)),
        pl.BlockSpec((nb, tc, DT_PAD), lambda b, t: (b, t, PROJ_MAIN // DT_PAD)),
        pl.BlockSpec((LRU_WIDTH // MXU_DIM, MXU_DIM, 2 * MXU_DIM), lambda b, t: (0, 0, 0)),
        pl.BlockSpec((1, LRU_WIDTH), const),
        pl.BlockSpec((1, LRU_WIDTH), const),
        pl.BlockSpec((1, LRU_WIDTH), const),
        pl.BlockSpec((1, LRU_WIDTH), const),
        pl.BlockSpec((1, DT_PAD), const),
        pl.BlockSpec((1, SSD_WIDTH), const),
        pl.BlockSpec((1, SSD_WIDTH), const),
        pl.BlockSpec((2 * LANES, N_SSD_HEADS * LANES), const),
        pl.BlockSpec((2 * LANES, SSD_WIDTH), const),
    ]
    out_shape = (
        jax.ShapeDtypeStruct((bsz, seq, MIX_WIDTH), F32),
        jax.ShapeDtypeStruct((bsz, 1, LRU_WIDTH), F32),
        jax.ShapeDtypeStruct((bsz, SSD_WIDTH, D_STATE), F32),
    )
    out_specs = (
        pl.BlockSpec((nb, tc, MIX_WIDTH), lambda b, t: (b, t, 0)),
        pl.BlockSpec((nb, 1, LRU_WIDTH), lambda b, t: (b, 0, 0)),
        pl.BlockSpec((nb, SSD_WIDTH, D_STATE), lambda b, t: (b, 0, 0)),
    )
    scratch = [
        pltpu.VMEM((nb, LRU_WIDTH // LANES, SUBLANES * SCAN_PITCH, LANES), F32),
        pltpu.VMEM((nb, LRU_WIDTH // LANES, SUBLANES * SCAN_PITCH, LANES), F32),
        pltpu.VMEM((nb, LRU_WIDTH // LANES, SUBLANES * SCAN_PITCH, LANES), F32),
        pltpu.VMEM((nb, SUBLANES, LRU_WIDTH), F32),
        pltpu.VMEM((nb, N_SSD_GROUPS, D_STATE, SSD_WIDTH // N_SSD_GROUPS), F32),
    ]
    return pl.pallas_call(
        _mixer_prompt_kernel,
        out_shape=out_shape,
        grid=(bsz // nb, seq // tc),
        in_specs=in_specs,
        out_specs=out_specs,
        scratch_shapes=scratch,
        compiler_params=pltpu.CompilerParams(
            dimension_semantics=("parallel", "arbitrary"), vmem_limit_bytes=VMEM_LIMIT),
        name="mixer_prompt",
    )(act, act, act, act, act, wg, b_a, b_x, lam, g_lru, a_log, d_skip, g_ssd, sel_t, sel_p)


def _mixer_sample_kernel(lx_ref, gate_ref, z_ref, xbc_ref, dt_ref,
                         slc_ref, slh_ref, ssc_ref, ssh_ref,
                         lcw_ref, lcb_ref, wg_ref, ba_ref, bx_ref, lam_ref, glru_ref,
                         scw_ref, scb_ref, dtb_ref, alog_ref, dskip_ref, gssd_ref, selt_ref, selp_ref,
                         y_ref, olc_ref, olh_ref, osc_ref, osh_ref,
                         ext_l, ext_s, pad_scr, yoff_scr, *, T):
    S = SAMPLE_SEQS
    P = SUBLANES
    K1 = CONV_WIDTH - 1
    R = S * P
    row_i = lax.broadcasted_iota(jnp.int32, (R, 1), 0) & (P - 1)
    valid = row_i < T

    def pad_rows(ref):
        width = ref.shape[-1]
        pad_scr[:, :, 0:width] = jnp.zeros((S, P, width), F32)
        pad_scr[:, 0:T, 0:width] = ref[...].reshape(S, T, width)
        return pad_scr[:, :, 0:width].reshape(R, width)

    ext_l[...] = jnp.zeros_like(ext_l)
    ext_s[...] = jnp.zeros_like(ext_s)
    for k in range(K1):
        ext_l[:, k, :] = slc_ref[k]
    ext_l[:, K1:K1 + T, :] = lx_ref[...].reshape(S, T, LRU_WIDTH)
    for k in range(K1):
        ext_s[:, k, :] = ssc_ref[k]
    ext_s[:, K1:K1 + T, :] = xbc_ref[...].reshape(S, T, SSD_CONV_DIM)
    for k in range(K1):
        olc_ref[k] = ext_l[:, T + k, :]
        osc_ref[k] = ext_s[:, T + k, :]

    el = ext_l[...].reshape(R, LRU_WIDTH)
    es = ext_s[...].reshape(R, SSD_CONV_DIM)

    def conv(e, w_ref, b_ref):
        out = b_ref[...] + e * w_ref[0:1, :]
        for k in range(1, CONV_WIDTH):
            out = out + pltpu.roll(e, R - k, axis=0) * w_ref[k:k + 1, :]
        return out

    u = conv(el, lcw_ref, lcb_ref)
    neg_c_sp = (-LRU_C) * _softplus(-lam_ref[...])
    a, b = _lru_coeffs(u, wg_ref, ba_ref[...], bx_ref[...], neg_c_sp)
    a, b = _scan_within_8(a, b)
    h0 = jnp.broadcast_to(slh_ref[...], (S, P, LRU_WIDTH)).reshape(R, LRU_WIDTH)
    hseq = a * h0 + b
    olh_ref[...] = hseq.reshape(S, P, LRU_WIDTH)[:, T - 1:T, :]
    gate = pad_rows(gate_ref)
    y_lru = _rms(hseq * _gelu_tanh(gate), glru_ref[...])

    xbc = conv(es, scw_ref, scb_ref)
    xbc = xbc * _sigmoid(xbc)
    xs = xbc[:, 0:SSD_WIDTH]
    bm = xbc[:, SSD_WIDTH:SSD_WIDTH + N_SSD_GROUPS * D_STATE]
    cm = xbc[:, SSD_WIDTH + N_SSD_GROUPS * D_STATE:]
    dt_raw = pad_rows(dt_ref)
    dt = jnp.where(valid, _softplus(dt_raw + dtb_ref[...]), 0.0)
    lane1 = lax.broadcasted_iota(jnp.int32, (1, LANES), 1)
    a2_row = jnp.where(lane1 < N_SSD_HEADS, -LOG2E * jnp.exp(alog_ref[...]), 0.0)

    rr = lax.broadcasted_iota(jnp.int32, (R, R), 0)
    cc = lax.broadcasted_iota(jnp.int32, (R, R), 1)
    allowed = (cc <= rr) & ((rr - cc) <= (rr & (P - 1)))
    tri = jnp.where(allowed, 1.0, 0.0).astype(F32)
    mask_add = jnp.where(allowed, 0.0, NEG_BIG).astype(F32)

    cum2, cum2_t = _ssd_cumdecay(dt, a2_row, tri)
    y_diag = _ssd_diag(xs, bm, cm, dt, _spread(cum2, selt_ref), cum2_t, mask_add)
    ecol = _spread(jnp.exp2(cum2), selp_ref)
    end2 = jnp.broadcast_to(cum2.reshape(S, P, LANES)[:, P - 1:P, :], (S, P, LANES)).reshape(R, LANES)
    xw = xs * _spread(jnp.exp2(end2 - cum2) * dt, selp_ref)
    ecum_t = jnp.exp2(cum2_t)

    half = SSD_WIDTH // N_SSD_GROUPS
    for q in range(S):
        r0 = P * q
        vq = jnp.broadcast_to(ecum_t[:, r0 + P - 1:r0 + P], (N_SSD_HEADS, LANES))
        for g in range(N_SSD_GROUPS):
            hqg = ssh_ref[q, half * g:half * (g + 1), :]
            cq = cm[r0:r0 + P, D_STATE * g:D_STATE * (g + 1)].astype(BF16)
            yoff_scr[r0:r0 + P, half * g:half * (g + 1)] = lax.dot_general(
                cq, hqg.astype(BF16), (((1,), (1,)), ((), ())), preferred_element_type=F32)
            bq = bm[r0:r0 + P, D_STATE * g:D_STATE * (g + 1)].astype(BF16)
            xq = xw[r0:r0 + P, half * g:half * (g + 1)].astype(BF16)
            st = lax.dot_general(xq, bq, (((0,), (0,)), ((), ())), preferred_element_type=F32)
            for e in range(N_SSD_HEADS // N_SSD_GROUPS):
                h = (N_SSD_HEADS // N_SSD_GROUPS) * g + e
                lo_r = SSD_HEAD_DIM * e
                osh_ref[q, SSD_HEAD_DIM * h:SSD_HEAD_DIM * (h + 1), :] = (
                    vq[h:h + 1, :] * hqg[lo_r:lo_r + SSD_HEAD_DIM, :] + st[lo_r:lo_r + SSD_HEAD_DIM, :])

    ys = y_diag + yoff_scr[...] * ecol
    z = pad_rows(z_ref)
    y_ssd = _ssd_gate_norm(ys, xs, z * _sigmoid(z), dskip_ref[...], gssd_ref[...])
    y_ref[:, 0:LRU_WIDTH] = y_lru.reshape(S, P, LRU_WIDTH)[:, 0:T, :].reshape(S * T, LRU_WIDTH)
    y_ref[:, LRU_WIDTH:MIX_WIDTH] = y_ssd.reshape(S, P, SSD_WIDTH)[:, 0:T, :].reshape(S * T, SSD_WIDTH)


def _mixer_sample(proj, T, st_lc, st_lh, st_sc, st_sh, params, sel_t, sel_p):
    nseq = proj.shape[0] // T
    S = SAMPLE_SEQS
    const = lambda i: (0, 0)
    in_specs = [
        pl.BlockSpec((S * T, LRU_WIDTH), lambda i: (i, 0)),
        pl.BlockSpec((S * T, LRU_WIDTH), lambda i: (i, 1)),
        pl.BlockSpec((S * T, SSD_WIDTH), lambda i: (i, 2)),
        pl.BlockSpec((S * T, SSD_CONV_DIM), lambda i: (i, 2)),
        pl.BlockSpec((S * T, DT_PAD), lambda i: (i, PROJ_MAIN // DT_PAD)),
        pl.BlockSpec((CONV_WIDTH - 1, S, LRU_WIDTH), lambda i: (0, i, 0)),
        pl.BlockSpec((S, 1, LRU_WIDTH), lambda i: (i, 0, 0)),
        pl.BlockSpec((CONV_WIDTH - 1, S, SSD_CONV_DIM), lambda i: (0, i, 0)),
        pl.BlockSpec((S, SSD_WIDTH, D_STATE), lambda i: (i, 0, 0)),
    ] + _param_specs(const) + [
        pl.BlockSpec((2 * LANES, N_SSD_HEADS * LANES), const),
        pl.BlockSpec((2 * LANES, SSD_WIDTH), const),
    ]
    out_shape = (
        jax.ShapeDtypeStruct((nseq * T, MIX_WIDTH), F32),
        jax.ShapeDtypeStruct((CONV_WIDTH - 1, nseq, LRU_WIDTH), F32),
        jax.ShapeDtypeStruct((nseq, 1, LRU_WIDTH), F32),
        jax.ShapeDtypeStruct((CONV_WIDTH - 1, nseq, SSD_CONV_DIM), F32),
        jax.ShapeDtypeStruct((nseq, SSD_WIDTH, D_STATE), F32),
    )
    out_specs = (
        pl.BlockSpec((S * T, MIX_WIDTH), lambda i: (i, 0)),
        pl.BlockSpec((CONV_WIDTH - 1, S, LRU_WIDTH), lambda i: (0, i, 0)),
        pl.BlockSpec((S, 1, LRU_WIDTH), lambda i: (i, 0, 0)),
        pl.BlockSpec((CONV_WIDTH - 1, S, SSD_CONV_DIM), lambda i: (0, i, 0)),
        pl.BlockSpec((S, SSD_WIDTH, D_STATE), lambda i: (i, 0, 0)),
    )
    scratch = [
        pltpu.VMEM((S, SUBLANES, LRU_WIDTH), F32),
        pltpu.VMEM((S, SUBLANES, SSD_CONV_DIM), F32),
        pltpu.VMEM((S, SUBLANES, LRU_WIDTH), F32),
        pltpu.VMEM((S * SUBLANES, SSD_WIDTH), F32),
    ]
    return pl.pallas_call(
        functools.partial(_mixer_sample_kernel, T=T),
        out_shape=out_shape,
        grid=(nseq // S,),
        in_specs=in_specs,
        out_specs=out_specs,
        scratch_shapes=scratch,
        compiler_params=pltpu.CompilerParams(
            dimension_semantics=("parallel",), vmem_limit_bytes=VMEM_LIMIT),
        name="mixer_sample",
    )(proj, proj, proj, proj, proj, st_lc, st_lh, st_sc, st_sh, *params, sel_t, sel_p)


def _gate_weights(w_a, w_x):
    def tiles(w):
        per = MXU_DIM // LRU_BLOCK
        w4 = w.reshape(N_LRU_HEADS // per, per, LRU_BLOCK, LRU_BLOCK)
        eye = jnp.eye(per, dtype=w.dtype)
        t = jnp.einsum('jaik,ab->jaibk', w4, eye)
        return t.reshape(N_LRU_HEADS // per, MXU_DIM, MXU_DIM)
    return jnp.concatenate([tiles(w_a), tiles(w_x)], axis=2).astype(BF16)


def kernel(x_prompt, x_sample, state_lru_conv, state_lru_h, state_ssd_conv, state_ssd_h, g_mix, w_in,
           lru_conv_w, lru_conv_b, w_a, b_a, w_x, b_x, lam, g_lru_out, ssd_conv_w, ssd_conv_b, dt_bias,
           a_log, d_skip, g_ssd_out, w_out, g_mlp, w_up, w_down, g_final):
    depth = w_in.shape[0]
    assert depth == 1
    bp, seq, _ = x_prompt.shape
    bs, dseq, _ = x_sample.shape
    l = 0
    row = lambda v: v.reshape(1, -1)
    w_main, w_dt = _split_w_in(jnp.swapaxes(w_in, 1, 2)[l])
    params = (
        lru_conv_w[l], row(lru_conv_b[l]), _gate_weights(w_a[l], w_x[l]),
        row(b_a[l]), row(b_x[l]), row(lam[l]), row(g_lru_out[l]),
        ssd_conv_w[l], row(ssd_conv_b[l]),
        jnp.pad(row(dt_bias[l]), ((0, 0), (0, DT_PAD - N_SSD_HEADS))),
        jnp.pad(row(a_log[l]), ((0, 0), (0, DT_PAD - N_SSD_HEADS))),
        row(jnp.repeat(d_skip[l], SSD_HEAD_DIM)), row(g_ssd_out[l]),
    )
    w_out_b = w_out[l].astype(BF16)
    w_up_b = w_up[l].astype(BF16)
    w_down_b = w_down[l].astype(BF16)
    gmix = row(g_mix[l])
    gmlp = row(g_mlp[l])
    gfin = row(g_final)

    xp2 = x_prompt.reshape(bp * seq, D_MODEL)
    (lcw, lcb, wg, ba, bx, lam_r, glru, scw, scb, dtb, alog, dskip, gssd) = params
    act_p, p_lc, p_sc = _in_proj_prompt(xp2, bp, gmix, w_main, w_dt, lcw, lcb, scw, scb, dtb)
    sel_t, sel_p = _head_selectors()
    ymix_p, p_lh, p_sh = _mixer_prompt(
        act_p.reshape(bp, seq, PROJ_PAD), wg, ba, bx, lam_r, glru, alog, dskip, gssd, sel_t, sel_p)
    y_prompt = _out_mlp(xp2, ymix_p.reshape(bp * seq, MIX_WIDTH), w_out_b, gmlp, w_up_b, w_down_b, gfin)

    proj_s = _in_proj(x_sample, gmix, w_main, w_dt)
    ymix_s, s_lc, s_lh, s_sc, s_sh = _mixer_sample(
        proj_s, dseq, jnp.swapaxes(state_lru_conv[l], 0, 1), state_lru_h[l].reshape(bs, 1, LRU_WIDTH),
        jnp.swapaxes(state_ssd_conv[l], 0, 1),
        state_ssd_h[l].reshape(bs, SSD_WIDTH, D_STATE), params, sel_t, sel_p)
    y_sample = _out_mlp(x_sample, ymix_s, w_out_b, gmlp, w_up_b, w_down_b, gfin)

    hshape = (N_SSD_HEADS, SSD_HEAD_DIM, D_STATE)
    return (
        y_prompt.reshape(bp, seq, D_MODEL), y_sample,
        p_lc[None], p_lh.reshape(1, bp, LRU_WIDTH), p_sc[None], p_sh.reshape(1, bp, *hshape),
        jnp.swapaxes(s_lc, 0, 1)[None], s_lh.reshape(1, bs, LRU_WIDTH), jnp.swapaxes(s_sc, 0, 1)[None],
        s_sh.reshape(1, bs, *hshape),
    )
```

```python
import functools
import math

import jax
import jax.numpy as jnp
import numpy as np
from jax import lax
from jax.experimental import pallas as pl
from jax.experimental.pallas import tpu as pltpu

F32 = jnp.float32
BF16 = jnp.bfloat16

D_MODEL = 1024
LRU_WIDTH = 1024
N_LRU_HEADS = 16
LRU_BLOCK = 64
LRU_C = 8.0
SSD_WIDTH = 1024
SSD_HEAD_DIM = 64
N_SSD_HEADS = 16
N_SSD_GROUPS = 2
D_STATE = 128
CONV_WIDTH = 4
SSD_CONV_DIM = SSD_WIDTH + 2 * N_SSD_GROUPS * D_STATE
D_FF = 4 * D_MODEL
EPS = 1e-6

LANES = 128
SUBLANES = 8
MXU_DIM = 256
DT_PAD = LANES
PROJ_MAIN = 2 * LRU_WIDTH + SSD_WIDTH + SSD_CONV_DIM
PROJ_PAD = PROJ_MAIN + DT_PAD
MIX_WIDTH = LRU_WIDTH + SSD_WIDTH
SSD_CHUNK = 128
PROMPT_TC = 256
PROMPT_NB = 2
ROW_TILE = 512
SPLIT_COLS = 1536
SAMPLE_SEQS = SSD_CHUNK // SUBLANES
SCAN_RUN = PROMPT_TC // SUBLANES
SCAN_PITCH = SCAN_RUN + 4
NEG_BIG = -1e30
LOG2E = 1.4426950408889634
VMEM_LIMIT = 56 * 1024 * 1024
HI = lax.Precision.HIGHEST


def _rms(x, g):
    ms = jnp.mean(x * x, axis=-1, keepdims=True)
    return x * lax.rsqrt(ms + EPS) * g


def _sigmoid(x):
    return 1.0 / (1.0 + jnp.exp(-x))


def _softplus(x):
    return jnp.maximum(x, 0.0) + jnp.log1p(jnp.exp(-jnp.abs(x)))


def _gelu_tanh(x):
    c = math.sqrt(2.0 / math.pi)
    return 0.5 * x * (1.0 + jnp.tanh(c * (x + 0.044715 * (x * x * x))))


def _lru_coeffs(u, wg_ref, b_a, b_x, neg_c_sp):
    ub = u.astype(BF16)
    r_parts, i_parts = [], []
    for j in range(LRU_WIDTH // MXU_DIM):
        g = jnp.dot(ub[:, MXU_DIM * j:MXU_DIM * (j + 1)], wg_ref[j], preferred_element_type=F32)
        r_parts.append(g[:, :MXU_DIM])
        i_parts.append(g[:, MXU_DIM:])
    r = _sigmoid(jnp.concatenate(r_parts, axis=1) + b_a)
    i = _sigmoid(jnp.concatenate(i_parts, axis=1) + b_x)
    log_a = r * neg_c_sp
    a = jnp.exp(log_a)
    th = jnp.tanh(log_a)
    v = (th + th) / (th - 1.0)
    mult = jnp.where(v > 0.0, v * lax.rsqrt(v), 0.0)
    return a, mult * (i * u)


def _scan_within_8(a, b):
    ridx = lax.broadcasted_iota(jnp.int32, a.shape, 0) & (SUBLANES - 1)
    for k in (1, 2, 4):
        a_s = pltpu.roll(a, k, axis=0)
        b_s = pltpu.roll(b, k, axis=0)
        m = ridx >= k
        b = jnp.where(m, a * b_s + b, b)
        a = jnp.where(m, a * a_s, a)
    return a, b


def _conv_slabs(ext, w_ref, b_ref, rows, first):
    parts = []
    for s in range(ext.shape[0]):
        cols = slice(LANES * s, LANES * (s + 1))
        acc = b_ref[:, cols] + ext[s, pl.ds(first, rows), :] * w_ref[0:1, cols]
        for k in range(1, CONV_WIDTH):
            acc = acc + ext[s, pl.ds(first + k, rows), :] * w_ref[k:k + 1, cols]
        parts.append(acc)
    return jnp.concatenate(parts, axis=1)


def _lru_scan_strided(a, b, hcar, a_pad, b_pad, h_pad):
    rows = a.shape[0]
    S = rows // SUBLANES
    nslab = LRU_WIDTH // LANES
    ridx = lax.broadcasted_iota(jnp.int32, (SUBLANES, LANES), 0)
    step = lambda ref, s, i: ref[s, pl.ds(i, SUBLANES, stride=SCAN_PITCH), :]
    for s in range(nslab):
        cols = slice(LANES * s, LANES * (s + 1))
        for j in range(SUBLANES):
            a_pad[s, SCAN_PITCH * j:SCAN_PITCH * j + S, :] = a[S * j:S * (j + 1), cols]
            b_pad[s, SCAN_PITCH * j:SCAN_PITCH * j + S, :] = b[S * j:S * (j + 1), cols]
    h = [jnp.zeros((SUBLANES, LANES), F32)] * nslab
    prod = [jnp.ones((SUBLANES, LANES), F32)] * nslab
    for i in range(S):
        for s in range(nslab):
            av = step(a_pad, s, i)
            h[s] = av * h[s] + step(b_pad, s, i)
            prod[s] = av * prod[s]
    for s in range(nslab):
        cols = slice(LANES * s, LANES * (s + 1))
        pcum, hcum = _scan_within_8(prod[s], h[s])
        cin = hcar[:, cols]
        ends = hcum + pcum * cin
        h[s] = jnp.where(ridx == 0, cin, pltpu.roll(ends, 1, axis=0))
        hcar[:, cols] = jnp.broadcast_to(ends[SUBLANES - 1:SUBLANES, :], (SUBLANES, LANES))
    for i in range(S):
        for s in range(nslab):
            h[s] = step(a_pad, s, i) * h[s] + step(b_pad, s, i)
            h_pad[s, pl.ds(i, SUBLANES, stride=SCAN_PITCH), :] = h[s]
    return jnp.concatenate(
        [jnp.concatenate([h_pad[s, SCAN_PITCH * j:SCAN_PITCH * j + S, :] for j in range(SUBLANES)], axis=0)
         for s in range(nslab)], axis=1)


def _ssd_cumdecay(dt, a2_row, tri):
    cum2 = jnp.dot(tri, dt * a2_row, precision=HI, preferred_element_type=F32)
    return cum2, cum2.T[0:N_SSD_HEADS, :]


def _ssd_diag(xs, bm, cm, dt, cols, cum2_t, mask_add):
    L = xs.shape[0]
    c2_t = cum2_t - jnp.log2(dt.T[0:N_SSD_HEADS, :])
    lane = lax.broadcasted_iota(jnp.int32, (L, LANES), 1)
    lo = lane < SSD_HEAD_DIM
    y_parts = []
    for g in range(N_SSD_GROUPS):
        bg = bm[:, D_STATE * g:D_STATE * (g + 1)].astype(BF16)
        cg = cm[:, D_STATE * g:D_STATE * (g + 1)].astype(BF16)
        cb = lax.dot_general(cg, bg, (((1,), (1,)), ((), ())), preferred_element_type=F32)
        for jj in range(N_SSD_HEADS // N_SSD_GROUPS // 2):
            j = (N_SSD_HEADS // N_SSD_GROUPS // 2) * g + jj
            h0, h1 = 2 * j, 2 * j + 1
            col0 = cols[:, LANES * h0:LANES * (h0 + 1)]
            col1 = cols[:, LANES * h1:LANES * (h1 + 1)]
            m0 = cb * jnp.exp2(col0 - c2_t[h0:h0 + 1, :] + mask_add)
            m1 = cb * jnp.exp2(col1 - c2_t[h1:h1 + 1, :] + mask_add)
            lhs = jnp.concatenate([m0, m1], axis=1).astype(BF16)
            xp = xs[:, LANES * j:LANES * (j + 1)]
            rhs = jnp.concatenate([jnp.where(lo, xp, 0.0), jnp.where(lo, 0.0, xp)], axis=0).astype(BF16)
            y_parts.append(jnp.dot(lhs, rhs, preferred_element_type=F32))
    return jnp.concatenate(y_parts, axis=1)


def _spread(v, sel_ref):
    p0 = v.astype(BF16)
    p1 = (v - p0.astype(F32)).astype(BF16)
    return jnp.dot(jnp.concatenate([p0, p1], axis=1), sel_ref[...], preferred_element_type=F32)


def _ssd_gate_norm(ys, xs, z_act, dskip, g_ssd):
    ys = ys + dskip * xs
    gated = ys * z_act
    half = SSD_WIDTH // N_SSD_GROUPS
    outs = []
    for g in range(N_SSD_GROUPS):
        outs.append(_rms(gated[:, half * g:half * (g + 1)], g_ssd[:, half * g:half * (g + 1)]))
    return jnp.concatenate(outs, axis=1)


def _split_w_in_kernel(wt_ref, wdt_t_ref, main_ref, dt_ref):
    main_ref[...] = wt_ref[...].T.astype(BF16)

    @pl.when(pl.program_id(0) == 0)
    def _dt():
        dt_ref[...] = jnp.zeros_like(dt_ref)
        dt_ref[:, 0:N_SSD_HEADS] = wdt_t_ref[...].T.astype(BF16)


def _split_w_in(w_t):
    cols, rows = w_t.shape
    blk = SPLIT_COLS
    return pl.pallas_call(
        _split_w_in_kernel,
        out_shape=(jax.ShapeDtypeStruct((rows, PROJ_MAIN), BF16), jax.ShapeDtypeStruct((rows, DT_PAD), BF16)),
        grid=(PROJ_MAIN // blk,),
        in_specs=[pl.BlockSpec((blk, rows), lambda j: (j, 0)),
                  pl.BlockSpec((N_SSD_HEADS, rows), lambda j: (PROJ_MAIN // N_SSD_HEADS, 0))],
        out_specs=(pl.BlockSpec((rows, blk), lambda j: (0, j)), pl.BlockSpec((rows, DT_PAD), lambda j: (0, 0))),
        compiler_params=pltpu.CompilerParams(dimension_semantics=("arbitrary",)),
        name="split_w_in",
    )(w_t, w_t)


def _inproj_kernel(x_ref, g_ref, w_ref, wdt_ref, o_ref):
    hn = _rms(x_ref[...].reshape(-1, D_MODEL), g_ref[...]).astype(BF16)
    o_ref[:, 0:PROJ_MAIN] = jnp.dot(hn, w_ref[...], preferred_element_type=F32)
    o_ref[:, PROJ_MAIN:PROJ_PAD] = jnp.dot(hn, wdt_ref[...], preferred_element_type=F32)


def _in_proj(x, g_mix, w_main, w_dt):
    nseq, ntok, _ = x.shape
    n = nseq * ntok
    return pl.pallas_call(
        _inproj_kernel,
        out_shape=jax.ShapeDtypeStruct((n, PROJ_PAD), F32),
        grid=(n // ROW_TILE,),
        in_specs=[
            pl.BlockSpec((ROW_TILE // ntok, ntok, D_MODEL), lambda i: (i, 0, 0)),
            pl.BlockSpec((1, D_MODEL), lambda i: (0, 0)),
            pl.BlockSpec((D_MODEL, PROJ_MAIN), lambda i: (0, 0), pipeline_mode=pl.Buffered(1)),
            pl.BlockSpec((D_MODEL, DT_PAD), lambda i: (0, 0), pipeline_mode=pl.Buffered(1)),
        ],
        out_specs=pl.BlockSpec((ROW_TILE, PROJ_PAD), lambda i: (i, 0)),
        compiler_params=pltpu.CompilerParams(
            dimension_semantics=("parallel",), vmem_limit_bytes=VMEM_LIMIT),
        name="in_proj",
    )(x, g_mix, w_main, w_dt)


def _inproj_prompt_kernel(x_ref, g_ref, w_ref, wdt_ref, lcw_ref, lcb_ref, scw_ref, scb_ref, dtb_ref,
                          o_ref, olc_ref, osc_ref, ext_l, ext_s, *, steps_per_seq):
    t = lax.rem(pl.program_id(0), steps_per_seq)
    rows = ROW_TILE
    hist = SUBLANES
    o1, o2, o3 = LRU_WIDTH, 2 * LRU_WIDTH, 2 * LRU_WIDTH + SSD_WIDTH

    @pl.when(t == 0)
    def _init():
        ext_l[:, 0:hist, :] = jnp.zeros((ext_l.shape[0], hist, LANES), F32)
        ext_s[:, 0:hist, :] = jnp.zeros((ext_s.shape[0], hist, LANES), F32)

    hn = _rms(x_ref[...], g_ref[...]).astype(BF16)
    lx = jnp.dot(hn, w_ref[:, 0:o1], preferred_element_type=F32)
    for s in range(ext_l.shape[0]):
        ext_l[s, hist:hist + rows, :] = lx[:, LANES * s:LANES * (s + 1)]
    xbc_in = jnp.dot(hn, w_ref[:, o3:PROJ_MAIN], preferred_element_type=F32)
    for s in range(ext_s.shape[0]):
        ext_s[s, hist:hist + rows, :] = xbc_in[:, LANES * s:LANES * (s + 1)]
    o_ref[:, o1:o2] = _gelu_tanh(jnp.dot(hn, w_ref[:, o1:o2], preferred_element_type=F32))
    z = jnp.dot(hn, w_ref[:, o2:o3], preferred_element_type=F32)
    o_ref[:, o2:o3] = z * _sigmoid(z)
    o_ref[:, PROJ_MAIN:PROJ_PAD] = _softplus(
        jnp.dot(hn, wdt_ref[...], preferred_element_type=F32) + dtb_ref[...])
    o_ref[:, 0:o1] = _conv_slabs(ext_l, lcw_ref, lcb_ref, rows, hist - (CONV_WIDTH - 1))
    xbc = _conv_slabs(ext_s, scw_ref, scb_ref, rows, hist - (CONV_WIDTH - 1))
    o_ref[:, o3:PROJ_MAIN] = xbc * _sigmoid(xbc)

    @pl.when(t == steps_per_seq - 1)
    def _final():
        last = slice(hist + rows - (CONV_WIDTH - 1), hist + rows)
        for s in range(ext_l.shape[0]):
            olc_ref[:, LANES * s:LANES * (s + 1)] = ext_l[s, last, :]
        for s in range(ext_s.shape[0]):
            osc_ref[:, LANES * s:LANES * (s + 1)] = ext_s[s, last, :]

    tail_l = ext_l[:, rows:rows + hist, :]
    tail_s = ext_s[:, rows:rows + hist, :]
    ext_l[:, 0:hist, :] = tail_l
    ext_s[:, 0:hist, :] = tail_s


def _in_proj_prompt(x2d, bsz, g_mix, w_main, w_dt, lcw, lcb, scw, scb, dtb):
    n = x2d.shape[0]
    steps_per_seq = n // bsz // ROW_TILE
    const = lambda i: (0, 0)
    return pl.pallas_call(
        functools.partial(_inproj_prompt_kernel, steps_per_seq=steps_per_seq),
        out_shape=(
            jax.ShapeDtypeStruct((n, PROJ_PAD), F32),
            jax.ShapeDtypeStruct((bsz, CONV_WIDTH - 1, LRU_WIDTH), F32),
            jax.ShapeDtypeStruct((bsz, CONV_WIDTH - 1, SSD_CONV_DIM), F32),
        ),
        grid=(n // ROW_TILE,),
        in_specs=[
            pl.BlockSpec((ROW_TILE, D_MODEL), lambda i: (i, 0)),
            pl.BlockSpec((1, D_MODEL), const),
            pl.BlockSpec((D_MODEL, PROJ_MAIN), const, pipeline_mode=pl.Buffered(1)),
            pl.BlockSpec((D_MODEL, DT_PAD), const, pipeline_mode=pl.Buffered(1)),
            pl.BlockSpec((CONV_WIDTH, LRU_WIDTH), const),
            pl.BlockSpec((1, LRU_WIDTH), const),
            pl.BlockSpec((CONV_WIDTH, SSD_CONV_DIM), const),
            pl.BlockSpec((1, SSD_CONV_DIM), const),
            pl.BlockSpec((1, DT_PAD), const),
        ],
        out_specs=(
            pl.BlockSpec((ROW_TILE, PROJ_PAD), lambda i: (i, 0)),
            pl.BlockSpec((None, CONV_WIDTH - 1, LRU_WIDTH), lambda i: (i // steps_per_seq, 0, 0)),
            pl.BlockSpec((None, CONV_WIDTH - 1, SSD_CONV_DIM), lambda i: (i // steps_per_seq, 0, 0)),
        ),
        scratch_shapes=[
            pltpu.VMEM((LRU_WIDTH // LANES, SUBLANES + ROW_TILE, LANES), F32),
            pltpu.VMEM((SSD_CONV_DIM // LANES, SUBLANES + ROW_TILE, LANES), F32),
        ],
        compiler_params=pltpu.CompilerParams(
            dimension_semantics=("arbitrary",), vmem_limit_bytes=VMEM_LIMIT),
        name="in_proj_prompt",
    )(x2d, g_mix, w_main, w_dt, lcw, lcb, scw, scb, dtb)


def _outmlp_kernel(x_ref, y_ref, wo_ref, gm_ref, wu_ref, wd_ref, gf_ref, o_ref):
    x = x_ref[...].reshape(-1, D_MODEL)
    x1 = x + jnp.dot(y_ref[...].astype(BF16), wo_ref[...], preferred_element_type=F32)
    m = _rms(x1, gm_ref[...]).astype(BF16)
    u = jnp.dot(m, wu_ref[...], preferred_element_type=F32)
    u = jnp.square(jnp.maximum(u, 0.0)).astype(BF16)
    x2 = x1 + jnp.dot(u, wd_ref[...], preferred_element_type=F32)
    o_ref[...] = _rms(x2, gf_ref[...]).reshape(o_ref.shape)


def _out_mlp(x, ymix2d, w_out_b, g_mlp, w_up_b, w_down_b, g_final):
    n = ymix2d.shape[0]
    const = lambda i: (0, 0)
    if x.ndim == 2:
        x_spec = pl.BlockSpec((ROW_TILE, D_MODEL), lambda i: (i, 0))
    else:
        x_spec = pl.BlockSpec((ROW_TILE // x.shape[1], x.shape[1], D_MODEL), lambda i: (i, 0, 0))
    return pl.pallas_call(
        _outmlp_kernel,
        out_shape=jax.ShapeDtypeStruct(x.shape, F32),
        grid=(n // ROW_TILE,),
        in_specs=[
            x_spec,
            pl.BlockSpec((ROW_TILE, MIX_WIDTH), lambda i: (i, 0)),
            pl.BlockSpec((MIX_WIDTH, D_MODEL), const, pipeline_mode=pl.Buffered(1)),
            pl.BlockSpec((1, D_MODEL), const),
            pl.BlockSpec((D_MODEL, D_FF), const, pipeline_mode=pl.Buffered(1)),
            pl.BlockSpec((D_FF, D_MODEL), const, pipeline_mode=pl.Buffered(1)),
            pl.BlockSpec((1, D_MODEL), const),
        ],
        out_specs=x_spec,
        compiler_params=pltpu.CompilerParams(
            dimension_semantics=("parallel",), vmem_limit_bytes=VMEM_LIMIT),
        name="out_mlp",
    )(x, ymix2d, w_out_b, g_mlp, w_up_b, w_down_b, g_final)


def _mixer_prompt_kernel(u_ref, gl_ref, zact_ref, xbc_ref, dt_ref,
                         wg_ref, ba_ref, bx_ref, lam_ref, glru_ref, alog_ref, dskip_ref, gssd_ref,
                         selt_ref, selp_ref,
                         y_ref, olh_ref, osh_ref,
                         a_pad, b_pad, h_pad, hcar, ht):
    t = pl.program_id(1)
    nt = pl.num_programs(1)
    tc = PROMPT_TC

    @pl.when(t == 0)
    def _init():
        hcar[...] = jnp.zeros_like(hcar)
        ht[...] = jnp.zeros_like(ht)

    neg_c_sp = (-LRU_C) * _softplus(-lam_ref[...])
    lane1 = lax.broadcasted_iota(jnp.int32, (1, LANES), 1)
    a2_row = jnp.where(lane1 < N_SSD_HEADS, -LOG2E * jnp.exp(alog_ref[...]), 0.0)
    L = SSD_CHUNK
    rr = lax.broadcasted_iota(jnp.int32, (L, L), 0)
    cc = lax.broadcasted_iota(jnp.int32, (L, L), 1)
    causal = cc <= rr
    tri = jnp.where(causal, 1.0, 0.0).astype(F32)
    mask_add = jnp.where(causal, 0.0, NEG_BIG).astype(F32)
    half = SSD_WIDTH // N_SSD_GROUPS

    chunks = [(n, c) for n in range(PROMPT_NB) for c in range(tc // L)]
    cums = [_ssd_cumdecay(dt_ref[n, L * c:L * (c + 1), :], a2_row, tri) for n, c in chunks]
    dts = [dt_ref[n, L * c:L * (c + 1), :] for n, c in chunks]
    cum_all = jnp.concatenate([cum2 for cum2, _ in cums], axis=0)
    cols_all = _spread(cum_all, selt_ref)
    ecol_all = _spread(jnp.exp2(cum_all), selp_ref)
    sdt_all = _spread(jnp.concatenate(
        [jnp.exp2(cum2[L - 1:L, :] - cum2) * dt for (cum2, _), dt in zip(cums, dts)], axis=0), selp_ref)

    for n in range(PROMPT_NB):
        a, b = _lru_coeffs(u_ref[n], wg_ref, ba_ref[...], bx_ref[...], neg_c_sp)
        hseq = _lru_scan_strided(a, b, hcar.at[n], a_pad.at[n], b_pad.at[n], h_pad.at[n])
        y_ref[n, :, 0:LRU_WIDTH] = _rms(hseq * gl_ref[n], glru_ref[...])

        for c in range(tc // L):
            k = chunks.index((n, c))
            rows = slice(L * c, L * (c + 1))
            krows = slice(L * k, L * (k + 1))
            xs = xbc_ref[n, rows, 0:SSD_WIDTH]
            bm = xbc_ref[n, rows, SSD_WIDTH:SSD_WIDTH + N_SSD_GROUPS * D_STATE]
            cm = xbc_ref[n, rows, SSD_WIDTH + N_SSD_GROUPS * D_STATE:SSD_CONV_DIM]
            y_diag = _ssd_diag(xs, bm, cm, dts[k], cols_all[krows, :], cums[k][1], mask_add)
            ecol = ecol_all[krows, :]
            xw = xs * sdt_all[krows, :]
            dec = ecol[L - 1:L, :]
            y_off_parts = []
            for g in range(N_SSD_GROUPS):
                htg = ht[n, g]
                cg = cm[:, D_STATE * g:D_STATE * (g + 1)].astype(BF16)
                y_off_parts.append(jnp.dot(cg, htg.astype(BF16), preferred_element_type=F32))
                bg_t = bm[:, D_STATE * g:D_STATE * (g + 1)].T.astype(BF16)
                st = jnp.dot(bg_t, xw[:, half * g:half * (g + 1)].astype(BF16), preferred_element_type=F32)
                ht[n, g] = htg * dec[:, half * g:half * (g + 1)] + st
            ys = y_diag + jnp.concatenate(y_off_parts, axis=1) * ecol
            y_ref[n, rows, LRU_WIDTH:MIX_WIDTH] = _ssd_gate_norm(
                ys, xs, zact_ref[n, rows, :], dskip_ref[...], gssd_ref[...])

    @pl.when(t == nt - 1)
    def _final():
        for n in range(PROMPT_NB):
            olh_ref[n] = hcar[n, 0:1, :]
            for g in range(N_SSD_GROUPS):
                osh_ref[n, half * g:half * (g + 1), :] = ht[n, g].T


def _param_specs(const):
    return [
        pl.BlockSpec((CONV_WIDTH, LRU_WIDTH), const),
        pl.BlockSpec((1, LRU_WIDTH), const),
        pl.BlockSpec((LRU_WIDTH // MXU_DIM, MXU_DIM, 2 * MXU_DIM), lambda *_: (0, 0, 0)),
        pl.BlockSpec((1, LRU_WIDTH), const),
        pl.BlockSpec((1, LRU_WIDTH), const),
        pl.BlockSpec((1, LRU_WIDTH), const),
        pl.BlockSpec((1, LRU_WIDTH), const),
        pl.BlockSpec((CONV_WIDTH, SSD_CONV_DIM), const),
        pl.BlockSpec((1, SSD_CONV_DIM), const),
        pl.BlockSpec((1, DT_PAD), const),
        pl.BlockSpec((1, DT_PAD), const),
        pl.BlockSpec((1, SSD_WIDTH), const),
        pl.BlockSpec((1, SSD_WIDTH), const),
    ]


def _head_selectors():
    k = np.arange(2 * LANES)[:, None] % LANES
    sel_t = (k == np.arange(N_SSD_HEADS * LANES)[None, :] // LANES).astype(np.float32)
    sel_p = (k == np.arange(SSD_WIDTH)[None, :] // SSD_HEAD_DIM).astype(np.float32)
    return jnp.asarray(sel_t, BF16), jnp.asarray(sel_p, BF16)


def _mixer_prompt(act, wg, b_a, b_x, lam, g_lru, a_log, d_skip, g_ssd, sel_t, sel_p):
    bsz, seq, _ = act.shape
    tc = PROMPT_TC
    nb = PROMPT_NB
    const = lambda b, t: (0, 0)
    in_specs = [
        pl.BlockSpec((nb, tc, LRU_WIDTH), lambda b, t: (b, t, 0)),
        pl.BlockSpec((nb, tc, LRU_WIDTH), lambda b, t: (b, t, 1)),
        pl.BlockSpec((nb, tc, SSD_WIDTH), lambda b, t: (b, t, 2)),
        pl.BlockSpec((nb, tc, SSD_CONV_DIM), lambda b, t: (b, t, 2)),
        pl.BlockSpec((nb, tc, DT_PAD), lambda b, t: (b, t, PROJ_MAIN // DT_PAD)),
        pl.BlockSpec((LRU_WIDTH // MXU_DIM, MXU_DIM, 2 * MXU_DIM), lambda b, t: (0, 0, 0)),
        pl.BlockSpec((1, LRU_WIDTH), const),
        pl.BlockSpec((1, LRU_WIDTH), const),
        pl.BlockSpec((1, LRU_WIDTH), const),
        pl.BlockSpec((1, LRU_WIDTH), const),
        pl.BlockSpec((1, DT_PAD), const),
        pl.BlockSpec((1, SSD_WIDTH), const),
        pl.BlockSpec((1, SSD_WIDTH), const),
        pl.BlockSpec((2 * LANES, N_SSD_HEADS * LANES), const),
        pl.BlockSpec((2 * LANES, SSD_WIDTH), const),
    ]
    out_shape = (
        jax.ShapeDtypeStruct((bsz, seq, MIX_WIDTH), F32),
        jax.ShapeDtypeStruct((bsz, 1, LRU_WIDTH), F32),
        jax.ShapeDtypeStruct((bsz, SSD_WIDTH, D_STATE), F32),
    )
    out_specs = (
        pl.BlockSpec((nb, tc, MIX_WIDTH), lambda b, t: (b, t, 0)),
        pl.BlockSpec((nb, 1, LRU_WIDTH), lambda b, t: (b, 0, 0)),
        pl.BlockSpec((nb, SSD_WIDTH, D_STATE), lambda b, t: (b, 0, 0)),
    )
    scratch = [
        pltpu.VMEM((nb, LRU_WIDTH // LANES, SUBLANES * SCAN_PITCH, LANES), F32),
        pltpu.VMEM((nb, LRU_WIDTH // LANES, SUBLANES * SCAN_PITCH, LANES), F32),
        pltpu.VMEM((nb, LRU_WIDTH // LANES, SUBLANES * SCAN_PITCH, LANES), F32),
        pltpu.VMEM((nb, SUBLANES, LRU_WIDTH), F32),
        pltpu.VMEM((nb, N_SSD_GROUPS, D_STATE, SSD_WIDTH // N_SSD_GROUPS), F32),
    ]
    return pl.pallas_call(
        _mixer_prompt_kernel,
        out_shape=out_shape,
        grid=(bsz // nb, seq // tc),
        in_specs=in_specs,
        out_specs=out_specs,
        scratch_shapes=scratch,
        compiler_params=pltpu.CompilerParams(
            dimension_semantics=("parallel", "arbitrary"), vmem_limit_bytes=VMEM_LIMIT),
        name="mixer_prompt",
    )(act, act, act, act, act, wg, b_a, b_x, lam, g_lru, a_log, d_skip, g_ssd, sel_t, sel_p)


def _mixer_sample_kernel(lx_ref, gate_ref, z_ref, xbc_ref, dt_ref,
                         slc_ref, slh_ref, ssc_ref, ssh_ref,
                         lcw_ref, lcb_ref, wg_ref, ba_ref, bx_ref, lam_ref, glru_ref,
                         scw_ref, scb_ref, dtb_ref, alog_ref, dskip_ref, gssd_ref, selt_ref, selp_ref,
                         y_ref, olc_ref, olh_ref, osc_ref, osh_ref,
                         ext_l, ext_s, pad_scr, yoff_scr, *, T):
    S = SAMPLE_SEQS
    P = SUBLANES
    K1 = CONV_WIDTH - 1
    R = S * P
    row_i = lax.broadcasted_iota(jnp.int32, (R, 1), 0) & (P - 1)
    valid = row_i < T

    def pad_rows(ref):
        width = ref.shape[-1]
        pad_scr[:, :, 0:width] = jnp.zeros((S, P, width), F32)
        pad_scr[:, 0:T, 0:width] = ref[...].reshape(S, T, width)
        return pad_scr[:, :, 0:width].reshape(R, width)

    ext_l[...] = jnp.zeros_like(ext_l)
    ext_s[...] = jnp.zeros_like(ext_s)
    for k in range(K1):
        ext_l[:, k, :] = slc_ref[k]
    ext_l[:, K1:K1 + T, :] = lx_ref[...].reshape(S, T, LRU_WIDTH)
    for k in range(K1):
        ext_s[:, k, :] = ssc_ref[k]
    ext_s[:, K1:K1 + T, :] = xbc_ref[...].reshape(S, T, SSD_CONV_DIM)
    for k in range(K1):
        olc_ref[k] = ext_l[:, T + k, :]
        osc_ref[k] = ext_s[:, T + k, :]

    el = ext_l[...].reshape(R, LRU_WIDTH)
    es = ext_s[...].reshape(R, SSD_CONV_DIM)

    def conv(e, w_ref, b_ref):
        out = b_ref[...] + e * w_ref[0:1, :]
        for k in range(1, CONV_WIDTH):
            out = out + pltpu.roll(e, R - k, axis=0) * w_ref[k:k + 1, :]
        return out

    u = conv(el, lcw_ref, lcb_ref)
    neg_c_sp = (-LRU_C) * _softplus(-lam_ref[...])
    a, b = _lru_coeffs(u, wg_ref, ba_ref[...], bx_ref[...], neg_c_sp)
    a, b = _scan_within_8(a, b)
    h0 = jnp.broadcast_to(slh_ref[...][:, None, :], (S, P, LRU_WIDTH)).reshape(R, LRU_WIDTH)
    hseq = a * h0 + b
    olh_ref[...] = hseq.reshape(S, P, LRU_WIDTH)[:, T - 1, :]
    gate = pad_rows(gate_ref)
    y_lru = _rms(hseq * _gelu_tanh(gate), glru_ref[...])

    xbc = conv(es, scw_ref, scb_ref)
    xbc = xbc * _sigmoid(xbc)
    xs = xbc[:, 0:SSD_WIDTH]
    bm = xbc[:, SSD_WIDTH:SSD_WIDTH + N_SSD_GROUPS * D_STATE]
    cm = xbc[:, SSD_WIDTH + N_SSD_GROUPS * D_STATE:]
    dt_raw = pad_rows(dt_ref)
    dt = jnp.where(valid, _softplus(dt_raw + dtb_ref[...]), 0.0)
    lane1 = lax.broadcasted_iota(jnp.int32, (1, LANES), 1)
    a2_row = jnp.where(lane1 < N_SSD_HEADS, -LOG2E * jnp.exp(alog_ref[...]), 0.0)

    rr = lax.broadcasted_iota(jnp.int32, (R, R), 0)
    cc = lax.broadcasted_iota(jnp.int32, (R, R), 1)
    allowed = (cc <= rr) & ((rr - cc) <= (rr & (P - 1)))
    tri = jnp.where(allowed, 1.0, 0.0).astype(F32)
    mask_add = jnp.where(allowed, 0.0, NEG_BIG).astype(F32)

    cum2, cum2_t = _ssd_cumdecay(dt, a2_row, tri)
    y_diag = _ssd_diag(xs, bm, cm, dt, _spread(cum2, selt_ref), cum2_t, mask_add)
    ecol = _spread(jnp.exp2(cum2), selp_ref)
    end2 = jnp.broadcast_to(cum2.reshape(S, P, LANES)[:, P - 1:P, :], (S, P, LANES)).reshape(R, LANES)
    xw = xs * _spread(jnp.exp2(end2 - cum2) * dt, selp_ref)
    ecum_t = jnp.exp2(cum2_t)

    half = SSD_WIDTH // N_SSD_GROUPS
    for q in range(S):
        r0 = P * q
        vq = jnp.broadcast_to(ecum_t[:, r0 + P - 1:r0 + P], (N_SSD_HEADS, LANES))
        for g in range(N_SSD_GROUPS):
            hqg = ssh_ref[q, half * g:half * (g + 1), :]
            cq = cm[r0:r0 + P, D_STATE * g:D_STATE * (g + 1)].astype(BF16)
            yoff_scr[r0:r0 + P, half * g:half * (g + 1)] = lax.dot_general(
                cq, hqg.astype(BF16), (((1,), (1,)), ((), ())), preferred_element_type=F32)
            bq = bm[r0:r0 + P, D_STATE * g:D_STATE * (g + 1)].astype(BF16)
            xq = xw[r0:r0 + P, half * g:half * (g + 1)].astype(BF16)
            st = lax.dot_general(xq, bq, (((0,), (0,)), ((), ())), preferred_element_type=F32)
            for e in range(N_SSD_HEADS // N_SSD_GROUPS):
                h = (N_SSD_HEADS // N_SSD_GROUPS) * g + e
                lo_r = SSD_HEAD_DIM * e
                osh_ref[q, SSD_HEAD_DIM * h:SSD_HEAD_DIM * (h + 1), :] = (
                    vq[h:h + 1, :] * hqg[lo_r:lo_r + SSD_HEAD_DIM, :] + st[lo_r:lo_r + SSD_HEAD_DIM, :])

    ys = y_diag + yoff_scr[...] * ecol
    z = pad_rows(z_ref)
    y_ssd = _ssd_gate_norm(ys, xs, z * _sigmoid(z), dskip_ref[...], gssd_ref[...])
    y_ref[:, 0:LRU_WIDTH] = y_lru.reshape(S, P, LRU_WIDTH)[:, 0:T, :].reshape(S * T, LRU_WIDTH)
    y_ref[:, LRU_WIDTH:MIX_WIDTH] = y_ssd.reshape(S, P, SSD_WIDTH)[:, 0:T, :].reshape(S * T, SSD_WIDTH)


def _mixer_sample(proj, T, st_lc, st_lh, st_sc, st_sh, params, sel_t, sel_p):
    nseq = proj.shape[0] // T
    S = SAMPLE_SEQS
    const = lambda i: (0, 0)
    in_specs = [
        pl.BlockSpec((S * T, LRU_WIDTH), lambda i: (i, 0)),
        pl.BlockSpec((S * T, LRU_WIDTH), lambda i: (i, 1)),
        pl.BlockSpec((S * T, SSD_WIDTH), lambda i: (i, 2)),
        pl.BlockSpec((S * T, SSD_CONV_DIM), lambda i: (i, 2)),
        pl.BlockSpec((S * T, DT_PAD), lambda i: (i, PROJ_MAIN // DT_PAD)),
        pl.BlockSpec((CONV_WIDTH - 1, S, LRU_WIDTH), lambda i: (0, i, 0)),
        pl.BlockSpec((S, LRU_WIDTH), lambda i: (i, 0)),
        pl.BlockSpec((CONV_WIDTH - 1, S, SSD_CONV_DIM), lambda i: (0, i, 0)),
        pl.BlockSpec((S, SSD_WIDTH, D_STATE), lambda i: (i, 0, 0)),
    ] + _param_specs(const) + [
        pl.BlockSpec((2 * LANES, N_SSD_HEADS * LANES), const),
        pl.BlockSpec((2 * LANES, SSD_WIDTH), const),
    ]
    out_shape = (
        jax.ShapeDtypeStruct((nseq * T, MIX_WIDTH), F32),
        jax.ShapeDtypeStruct((CONV_WIDTH - 1, nseq, LRU_WIDTH), F32),
        jax.ShapeDtypeStruct((nseq, LRU_WIDTH), F32),
        jax.ShapeDtypeStruct((CONV_WIDTH - 1, nseq, SSD_CONV_DIM), F32),
        jax.ShapeDtypeStruct((nseq, SSD_WIDTH, D_STATE), F32),
    )
    out_specs = (
        pl.BlockSpec((S * T, MIX_WIDTH), lambda i: (i, 0)),
        pl.BlockSpec((CONV_WIDTH - 1, S, LRU_WIDTH), lambda i: (0, i, 0)),
        pl.BlockSpec((S, LRU_WIDTH), lambda i: (i, 0)),
        pl.BlockSpec((CONV_WIDTH - 1, S, SSD_CONV_DIM), lambda i: (0, i, 0)),
        pl.BlockSpec((S, SSD_WIDTH, D_STATE), lambda i: (i, 0, 0)),
    )
    scratch = [
        pltpu.VMEM((S, SUBLANES, LRU_WIDTH), F32),
        pltpu.VMEM((S, SUBLANES, SSD_CONV_DIM), F32),
        pltpu.VMEM((S, SUBLANES, LRU_WIDTH), F32),
        pltpu.VMEM((S * SUBLANES, SSD_WIDTH), F32),
    ]
    return pl.pallas_call(
        functools.partial(_mixer_sample_kernel, T=T),
        out_shape=out_shape,
        grid=(nseq // S,),
        in_specs=in_specs,
        out_specs=out_specs,
        scratch_shapes=scratch,
        compiler_params=pltpu.CompilerParams(
            dimension_semantics=("parallel",), vmem_limit_bytes=VMEM_LIMIT),
        name="mixer_sample",
    )(proj, proj, proj, proj, proj, st_lc, st_lh, st_sc, st_sh, *params, sel_t, sel_p)


def _gate_weights(w_a, w_x):
    def tiles(w):
        per = MXU_DIM // LRU_BLOCK
        w4 = w.reshape(N_LRU_HEADS // per, per, LRU_BLOCK, LRU_BLOCK)
        eye = jnp.eye(per, dtype=w.dtype)
        t = jnp.einsum('jaik,ab->jaibk', w4, eye)
        return t.reshape(N_LRU_HEADS // per, MXU_DIM, MXU_DIM)
    return jnp.concatenate([tiles(w_a), tiles(w_x)], axis=2).astype(BF16)


def kernel(x_prompt, x_sample, state_lru_conv, state_lru_h, state_ssd_conv, state_ssd_h, g_mix, w_in,
           lru_conv_w, lru_conv_b, w_a, b_a, w_x, b_x, lam, g_lru_out, ssd_conv_w, ssd_conv_b, dt_bias,
           a_log, d_skip, g_ssd_out, w_out, g_mlp, w_up, w_down, g_final):
    depth = w_in.shape[0]
    assert depth == 1
    bp, seq, _ = x_prompt.shape
    bs, dseq, _ = x_sample.shape
    l = 0
    row = lambda v: v.reshape(1, -1)
    w_main, w_dt = _split_w_in(jnp.swapaxes(w_in, 1, 2)[l])
    params = (
        lru_conv_w[l], row(lru_conv_b[l]), _gate_weights(w_a[l], w_x[l]),
        row(b_a[l]), row(b_x[l]), row(lam[l]), row(g_lru_out[l]),
        ssd_conv_w[l], row(ssd_conv_b[l]),
        jnp.pad(row(dt_bias[l]), ((0, 0), (0, DT_PAD - N_SSD_HEADS))),
        jnp.pad(row(a_log[l]), ((0, 0), (0, DT_PAD - N_SSD_HEADS))),
        row(jnp.repeat(d_skip[l], SSD_HEAD_DIM)), row(g_ssd_out[l]),
    )
    w_out_b = w_out[l].astype(BF16)
    w_up_b = w_up[l].astype(BF16)
    w_down_b = w_down[l].astype(BF16)
    gmix = row(g_mix[l])
    gmlp = row(g_mlp[l])
    gfin = row(g_final)

    xp2 = x_prompt.reshape(bp * seq, D_MODEL)
    (lcw, lcb, wg, ba, bx, lam_r, glru, scw, scb, dtb, alog, dskip, gssd) = params
    act_p, p_lc, p_sc = _in_proj_prompt(xp2, bp, gmix, w_main, w_dt, lcw, lcb, scw, scb, dtb)
    sel_t, sel_p = _head_selectors()
    ymix_p, p_lh, p_sh = _mixer_prompt(
        act_p.reshape(bp, seq, PROJ_PAD), wg, ba, bx, lam_r, glru, alog, dskip, gssd, sel_t, sel_p)
    y_prompt = _out_mlp(xp2, ymix_p.reshape(bp * seq, MIX_WIDTH), w_out_b, gmlp, w_up_b, w_down_b, gfin)

    proj_s = _in_proj(x_sample, gmix, w_main, w_dt)
    ymix_s, s_lc, s_lh, s_sc, s_sh = _mixer_sample(
        proj_s, dseq, jnp.swapaxes(state_lru_conv[l], 0, 1), state_lru_h[l],
        jnp.swapaxes(state_ssd_conv[l], 0, 1),
        state_ssd_h[l].reshape(bs, SSD_WIDTH, D_STATE), params, sel_t, sel_p)
    y_sample = _out_mlp(x_sample, ymix_s, w_out_b, gmlp, w_up_b, w_down_b, gfin)

    hshape = (N_SSD_HEADS, SSD_HEAD_DIM, D_STATE)
    return (
        y_prompt.reshape(bp, seq, D_MODEL), y_sample,
        p_lc[None], p_lh.reshape(1, bp, LRU_WIDTH), p_sc[None], p_sh.reshape(1, bp, *hshape),
        jnp.swapaxes(s_lc, 0, 1)[None], s_lh[None], jnp.swapaxes(s_sc, 0, 1)[None],
        s_sh.reshape(1, bs, *hshape),
    )
```

```python
import functools
import math

import jax
import jax.numpy as jnp
import numpy as np
from jax import lax
from jax.experimental import pallas as pl
from jax.experimental.pallas import tpu as pltpu

F32 = jnp.float32
BF16 = jnp.bfloat16

D_MODEL = 1024
LRU_WIDTH = 1024
N_LRU_HEADS = 16
LRU_BLOCK = 64
LRU_C = 8.0
SSD_WIDTH = 1024
SSD_HEAD_DIM = 64
N_SSD_HEADS = 16
N_SSD_GROUPS = 2
D_STATE = 128
CONV_WIDTH = 4
SSD_CONV_DIM = SSD_WIDTH + 2 * N_SSD_GROUPS * D_STATE
D_FF = 4 * D_MODEL
EPS = 1e-6

LANES = 128
SUBLANES = 8
MXU_DIM = 256
DT_PAD = LANES
PROJ_MAIN = 2 * LRU_WIDTH + SSD_WIDTH + SSD_CONV_DIM
PROJ_PAD = PROJ_MAIN + DT_PAD
MIX_WIDTH = LRU_WIDTH + SSD_WIDTH
SSD_CHUNK = 128
PROMPT_TC = 256
PROMPT_NB = 2
ROW_TILE = 512
SPLIT_COLS = 1536
SAMPLE_SEQS = SSD_CHUNK // SUBLANES
SCAN_RUN = PROMPT_TC // SUBLANES
SCAN_PITCH = SCAN_RUN + 4
NEG_BIG = -1e30
LOG2E = 1.4426950408889634
VMEM_LIMIT = 56 * 1024 * 1024
VMEM_LIMIT_OUT = 60 * 1024 * 1024
HI = lax.Precision.HIGHEST


def _rms(x, g):
    ms = jnp.mean(x * x, axis=-1, keepdims=True)
    return x * lax.rsqrt(ms + EPS) * g


def _sigmoid(x):
    return 1.0 / (1.0 + jnp.exp(-x))


def _softplus(x):
    return jnp.maximum(x, 0.0) + jnp.log1p(jnp.exp(-jnp.abs(x)))


def _gelu_tanh(x):
    c = math.sqrt(2.0 / math.pi)
    return 0.5 * x * (1.0 + jnp.tanh(c * (x + 0.044715 * (x * x * x))))


def _lru_coeffs(u, wg_ref, b_a, b_x, neg_c_sp):
    ub = u.astype(BF16)
    r_parts, i_parts = [], []
    for j in range(LRU_WIDTH // MXU_DIM):
        g = jnp.dot(ub[:, MXU_DIM * j:MXU_DIM * (j + 1)], wg_ref[j], preferred_element_type=F32)
        r_parts.append(g[:, :MXU_DIM])
        i_parts.append(g[:, MXU_DIM:])
    r = _sigmoid(jnp.concatenate(r_parts, axis=1) + b_a)
    i = _sigmoid(jnp.concatenate(i_parts, axis=1) + b_x)
    log_a = r * neg_c_sp
    a = jnp.exp(log_a)
    th = jnp.tanh(log_a)
    v = (th + th) / (th - 1.0)
    mult = jnp.where(v > 0.0, v * lax.rsqrt(v), 0.0)
    return a, mult * (i * u)


def _scan_within_8(a, b):
    ridx = lax.broadcasted_iota(jnp.int32, a.shape, 0) & (SUBLANES - 1)
    for k in (1, 2, 4):
        a_s = pltpu.roll(a, k, axis=0)
        b_s = pltpu.roll(b, k, axis=0)
        m = ridx >= k
        b = jnp.where(m, a * b_s + b, b)
        a = jnp.where(m, a * a_s, a)
    return a, b


def _conv_slabs(ext, w_ref, b_ref, rows, first):
    parts = []
    for s in range(ext.shape[0]):
        cols = slice(LANES * s, LANES * (s + 1))
        acc = b_ref[:, cols] + ext[s, pl.ds(first, rows), :] * w_ref[0:1, cols]
        for k in range(1, CONV_WIDTH):
            acc = acc + ext[s, pl.ds(first + k, rows), :] * w_ref[k:k + 1, cols]
        parts.append(acc)
    return jnp.concatenate(parts, axis=1)


def _lru_scan_strided(a, b, hcar, a_pad, b_pad, h_pad):
    rows = a.shape[0]
    S = rows // SUBLANES
    nslab = LRU_WIDTH // LANES
    ridx = lax.broadcasted_iota(jnp.int32, (SUBLANES, LANES), 0)
    step = lambda ref, s, i: ref[s, pl.ds(i, SUBLANES, stride=SCAN_PITCH), :]
    for s in range(nslab):
        cols = slice(LANES * s, LANES * (s + 1))
        for j in range(SUBLANES):
            a_pad[s, SCAN_PITCH * j:SCAN_PITCH * j + S, :] = a[S * j:S * (j + 1), cols]
            b_pad[s, SCAN_PITCH * j:SCAN_PITCH * j + S, :] = b[S * j:S * (j + 1), cols]
    h = [jnp.zeros((SUBLANES, LANES), F32)] * nslab
    prod = [jnp.ones((SUBLANES, LANES), F32)] * nslab
    for i in range(S):
        for s in range(nslab):
            av = step(a_pad, s, i)
            h[s] = av * h[s] + step(b_pad, s, i)
            prod[s] = av * prod[s]
    for s in range(nslab):
        cols = slice(LANES * s, LANES * (s + 1))
        pcum, hcum = _scan_within_8(prod[s], h[s])
        cin = hcar[:, cols]
        ends = hcum + pcum * cin
        h[s] = jnp.where(ridx == 0, cin, pltpu.roll(ends, 1, axis=0))
        hcar[:, cols] = jnp.broadcast_to(ends[SUBLANES - 1:SUBLANES, :], (SUBLANES, LANES))
    for i in range(S):
        for s in range(nslab):
            h[s] = step(a_pad, s, i) * h[s] + step(b_pad, s, i)
            h_pad[s, pl.ds(i, SUBLANES, stride=SCAN_PITCH), :] = h[s]
    return jnp.concatenate(
        [jnp.concatenate([h_pad[s, SCAN_PITCH * j:SCAN_PITCH * j + S, :] for j in range(SUBLANES)], axis=0)
         for s in range(nslab)], axis=1)


def _ssd_cumdecay(dt, a2_row, tri):
    cum2 = jnp.dot(tri, dt * a2_row, precision=HI, preferred_element_type=F32)
    return cum2, cum2.T[0:N_SSD_HEADS, :]


def _ssd_diag(xs, bm, cm, dt, cols, cum2_t, mask_add):
    L = xs.shape[0]
    c2_t = cum2_t - jnp.log2(dt.T[0:N_SSD_HEADS, :])
    lane = lax.broadcasted_iota(jnp.int32, (L, LANES), 1)
    lo = lane < SSD_HEAD_DIM
    y_parts = []
    for g in range(N_SSD_GROUPS):
        bg = bm[:, D_STATE * g:D_STATE * (g + 1)].astype(BF16)
        cg = cm[:, D_STATE * g:D_STATE * (g + 1)].astype(BF16)
        cb = lax.dot_general(cg, bg, (((1,), (1,)), ((), ())), preferred_element_type=F32)
        for jj in range(N_SSD_HEADS // N_SSD_GROUPS // 2):
            j = (N_SSD_HEADS // N_SSD_GROUPS // 2) * g + jj
            h0, h1 = 2 * j, 2 * j + 1
            col0 = cols[:, LANES * h0:LANES * (h0 + 1)]
            col1 = cols[:, LANES * h1:LANES * (h1 + 1)]
            m0 = cb * jnp.exp2(col0 - c2_t[h0:h0 + 1, :] + mask_add)
            m1 = cb * jnp.exp2(col1 - c2_t[h1:h1 + 1, :] + mask_add)
            lhs = jnp.concatenate([m0, m1], axis=1).astype(BF16)
            xp = xs[:, LANES * j:LANES * (j + 1)]
            rhs = jnp.concatenate([jnp.where(lo, xp, 0.0), jnp.where(lo, 0.0, xp)], axis=0).astype(BF16)
            y_parts.append(jnp.dot(lhs, rhs, preferred_element_type=F32))
    return jnp.concatenate(y_parts, axis=1)


def _spread(v, sel_ref):
    p0 = v.astype(BF16)
    p1 = (v - p0.astype(F32)).astype(BF16)
    return jnp.dot(jnp.concatenate([p0, p1], axis=1), sel_ref[...], preferred_element_type=F32)


def _ssd_gate_norm(ys, xs, z_act, dskip, g_ssd):
    ys = ys + dskip * xs
    gated = ys * z_act
    half = SSD_WIDTH // N_SSD_GROUPS
    outs = []
    for g in range(N_SSD_GROUPS):
        outs.append(_rms(gated[:, half * g:half * (g + 1)], g_ssd[:, half * g:half * (g + 1)]))
    return jnp.concatenate(outs, axis=1)


def _split_w_in_kernel(wt_ref, wdt_t_ref, main_ref, dt_ref):
    main_ref[...] = wt_ref[...].T.astype(BF16)

    @pl.when(pl.program_id(0) == 0)
    def _dt():
        dt_ref[...] = jnp.zeros_like(dt_ref)
        dt_ref[:, 0:N_SSD_HEADS] = wdt_t_ref[...].T.astype(BF16)


def _split_w_in(w_t):
    cols, rows = w_t.shape
    blk = SPLIT_COLS
    return pl.pallas_call(
        _split_w_in_kernel,
        out_shape=(jax.ShapeDtypeStruct((rows, PROJ_MAIN), BF16), jax.ShapeDtypeStruct((rows, DT_PAD), BF16)),
        grid=(PROJ_MAIN // blk,),
        in_specs=[pl.BlockSpec((blk, rows), lambda j: (j, 0)),
                  pl.BlockSpec((N_SSD_HEADS, rows), lambda j: (PROJ_MAIN // N_SSD_HEADS, 0))],
        out_specs=(pl.BlockSpec((rows, blk), lambda j: (0, j)), pl.BlockSpec((rows, DT_PAD), lambda j: (0, 0))),
        compiler_params=pltpu.CompilerParams(dimension_semantics=("arbitrary",)),
        name="split_w_in",
    )(w_t, w_t)


def _inproj_kernel(x_ref, g_ref, w_ref, wdt_ref, o_ref):
    hn = _rms(x_ref[...].reshape(-1, D_MODEL), g_ref[...]).astype(BF16)
    o_ref[:, 0:PROJ_MAIN] = jnp.dot(hn, w_ref[...], preferred_element_type=F32)
    o_ref[:, PROJ_MAIN:PROJ_PAD] = jnp.dot(hn, wdt_ref[...], preferred_element_type=F32)


def _in_proj(x, g_mix, w_main, w_dt):
    nseq, ntok, _ = x.shape
    n = nseq * ntok
    return pl.pallas_call(
        _inproj_kernel,
        out_shape=jax.ShapeDtypeStruct((n, PROJ_PAD), F32),
        grid=(n // ROW_TILE,),
        in_specs=[
            pl.BlockSpec((ROW_TILE // ntok, ntok, D_MODEL), lambda i: (i, 0, 0)),
            pl.BlockSpec((1, D_MODEL), lambda i: (0, 0)),
            pl.BlockSpec((D_MODEL, PROJ_MAIN), lambda i: (0, 0), pipeline_mode=pl.Buffered(1)),
            pl.BlockSpec((D_MODEL, DT_PAD), lambda i: (0, 0), pipeline_mode=pl.Buffered(1)),
        ],
        out_specs=pl.BlockSpec((ROW_TILE, PROJ_PAD), lambda i: (i, 0)),
        compiler_params=pltpu.CompilerParams(
            dimension_semantics=("parallel",), vmem_limit_bytes=VMEM_LIMIT),
        name="in_proj",
    )(x, g_mix, w_main, w_dt)


def _inproj_prompt_kernel(x_ref, g_ref, w_ref, wdt_ref, lcw_ref, lcb_ref, scw_ref, scb_ref, dtb_ref,
                          o_ref, olc_ref, osc_ref, ext_l, ext_s, *, steps_per_seq):
    t = lax.rem(pl.program_id(0), steps_per_seq)
    rows = ROW_TILE
    hist = SUBLANES
    o1, o2, o3 = LRU_WIDTH, 2 * LRU_WIDTH, 2 * LRU_WIDTH + SSD_WIDTH

    @pl.when(t == 0)
    def _init():
        ext_l[:, 0:hist, :] = jnp.zeros((ext_l.shape[0], hist, LANES), F32)
        ext_s[:, 0:hist, :] = jnp.zeros((ext_s.shape[0], hist, LANES), F32)

    hn = _rms(x_ref[...], g_ref[...]).astype(BF16)
    lx = jnp.dot(hn, w_ref[:, 0:o1], preferred_element_type=F32)
    for s in range(ext_l.shape[0]):
        ext_l[s, hist:hist + rows, :] = lx[:, LANES * s:LANES * (s + 1)]
    xbc_in = jnp.dot(hn, w_ref[:, o3:PROJ_MAIN], preferred_element_type=F32)
    for s in range(ext_s.shape[0]):
        ext_s[s, hist:hist + rows, :] = xbc_in[:, LANES * s:LANES * (s + 1)]
    o_ref[:, o1:o2] = _gelu_tanh(jnp.dot(hn, w_ref[:, o1:o2], preferred_element_type=F32))
    z = jnp.dot(hn, w_ref[:, o2:o3], preferred_element_type=F32)
    o_ref[:, o2:o3] = z * _sigmoid(z)
    o_ref[:, PROJ_MAIN:PROJ_PAD] = _softplus(
        jnp.dot(hn, wdt_ref[...], preferred_element_type=F32) + dtb_ref[...])
    o_ref[:, 0:o1] = _conv_slabs(ext_l, lcw_ref, lcb_ref, rows, hist - (CONV_WIDTH - 1))
    xbc = _conv_slabs(ext_s, scw_ref, scb_ref, rows, hist - (CONV_WIDTH - 1))
    o_ref[:, o3:PROJ_MAIN] = xbc * _sigmoid(xbc)

    @pl.when(t == steps_per_seq - 1)
    def _final():
        last = slice(hist + rows - (CONV_WIDTH - 1), hist + rows)
        for s in range(ext_l.shape[0]):
            olc_ref[:, LANES * s:LANES * (s + 1)] = ext_l[s, last, :]
        for s in range(ext_s.shape[0]):
            osc_ref[:, LANES * s:LANES * (s + 1)] = ext_s[s, last, :]

    tail_l = ext_l[:, rows:rows + hist, :]
    tail_s = ext_s[:, rows:rows + hist, :]
    ext_l[:, 0:hist, :] = tail_l
    ext_s[:, 0:hist, :] = tail_s


def _in_proj_prompt(x2d, bsz, g_mix, w_main, w_dt, lcw, lcb, scw, scb, dtb):
    n = x2d.shape[0]
    steps_per_seq = n // bsz // ROW_TILE
    const = lambda i: (0, 0)
    return pl.pallas_call(
        functools.partial(_inproj_prompt_kernel, steps_per_seq=steps_per_seq),
        out_shape=(
            jax.ShapeDtypeStruct((n, PROJ_PAD), F32),
            jax.ShapeDtypeStruct((bsz, CONV_WIDTH - 1, LRU_WIDTH), F32),
            jax.ShapeDtypeStruct((bsz, CONV_WIDTH - 1, SSD_CONV_DIM), F32),
        ),
        grid=(n // ROW_TILE,),
        in_specs=[
            pl.BlockSpec((ROW_TILE, D_MODEL), lambda i: (i, 0)),
            pl.BlockSpec((1, D_MODEL), const),
            pl.BlockSpec((D_MODEL, PROJ_MAIN), const, pipeline_mode=pl.Buffered(1)),
            pl.BlockSpec((D_MODEL, DT_PAD), const, pipeline_mode=pl.Buffered(1)),
            pl.BlockSpec((CONV_WIDTH, LRU_WIDTH), const),
            pl.BlockSpec((1, LRU_WIDTH), const),
            pl.BlockSpec((CONV_WIDTH, SSD_CONV_DIM), const),
            pl.BlockSpec((1, SSD_CONV_DIM), const),
            pl.BlockSpec((1, DT_PAD), const),
        ],
        out_specs=(
            pl.BlockSpec((ROW_TILE, PROJ_PAD), lambda i: (i, 0)),
            pl.BlockSpec((None, CONV_WIDTH - 1, LRU_WIDTH), lambda i: (i // steps_per_seq, 0, 0)),
            pl.BlockSpec((None, CONV_WIDTH - 1, SSD_CONV_DIM), lambda i: (i // steps_per_seq, 0, 0)),
        ),
        scratch_shapes=[
            pltpu.VMEM((LRU_WIDTH // LANES, SUBLANES + ROW_TILE, LANES), F32),
            pltpu.VMEM((SSD_CONV_DIM // LANES, SUBLANES + ROW_TILE, LANES), F32),
        ],
        compiler_params=pltpu.CompilerParams(
            dimension_semantics=("arbitrary",), vmem_limit_bytes=VMEM_LIMIT),
        name="in_proj_prompt",
    )(x2d, g_mix, w_main, w_dt, lcw, lcb, scw, scb, dtb)


def _outmlp_kernel(xp_ref, yp_ref, xs_ref, ys_ref, wo_ref, gm_ref, wu_ref, wd_ref, gf_ref, op_ref, os_ref):
    last = pl.program_id(0) == pl.num_programs(0) - 1

    def run(x_ref, y_ref, o_ref):
        x = x_ref[...].reshape(-1, D_MODEL)
        x1 = x + jnp.dot(y_ref[...].astype(BF16), wo_ref[...], preferred_element_type=F32)
        m = _rms(x1, gm_ref[...]).astype(BF16)
        u = jnp.dot(m, wu_ref[...], preferred_element_type=F32)
        u = jnp.square(jnp.maximum(u, 0.0)).astype(BF16)
        x2 = x1 + jnp.dot(u, wd_ref[...], preferred_element_type=F32)
        o_ref[...] = _rms(x2, gf_ref[...]).reshape(o_ref.shape)

    @pl.when(jnp.logical_not(last))
    def _prompt():
        run(xp_ref, yp_ref, op_ref)

    @pl.when(last)
    def _sample():
        run(xs_ref, ys_ref, os_ref)


def _out_mlp(xp, ymix_p, xs, ymix_s, w_out_b, g_mlp, w_up_b, w_down_b, g_final):
    n_p = xp.shape[0]
    steps_p = n_p // ROW_TILE
    assert ymix_s.shape[0] == ROW_TILE
    const = lambda i: (0, 0)
    tile = lambda i: (jnp.minimum(i, steps_p - 1), 0)
    once = dict(pipeline_mode=pl.Buffered(1))
    return pl.pallas_call(
        _outmlp_kernel,
        out_shape=(jax.ShapeDtypeStruct((n_p, D_MODEL), F32), jax.ShapeDtypeStruct(xs.shape, F32)),
        grid=(steps_p + 1,),
        in_specs=[
            pl.BlockSpec((ROW_TILE, D_MODEL), tile),
            pl.BlockSpec((ROW_TILE, MIX_WIDTH), tile),
            pl.BlockSpec(xs.shape, lambda i: (0, 0, 0), **once),
            pl.BlockSpec((ROW_TILE, MIX_WIDTH), const, **once),
            pl.BlockSpec((MIX_WIDTH, D_MODEL), const, **once),
            pl.BlockSpec((1, D_MODEL), const),
            pl.BlockSpec((D_MODEL, D_FF), const, **once),
            pl.BlockSpec((D_FF, D_MODEL), const, **once),
            pl.BlockSpec((1, D_MODEL), const),
        ],
        out_specs=(pl.BlockSpec((ROW_TILE, D_MODEL), tile), pl.BlockSpec(xs.shape, lambda i: (0, 0, 0))),
        compiler_params=pltpu.CompilerParams(
            dimension_semantics=("arbitrary",), vmem_limit_bytes=VMEM_LIMIT_OUT),
        name="out_mlp",
    )(xp, ymix_p, xs, ymix_s, w_out_b, g_mlp, w_up_b, w_down_b, g_final)


def _mixer_prompt_kernel(u_ref, gl_ref, zact_ref, xbc_ref, dt_ref,
                         wg_ref, ba_ref, bx_ref, lam_ref, glru_ref, alog_ref, dskip_ref, gssd_ref,
                         selt_ref, selp_ref,
                         y_ref, olh_ref, osh_ref,
                         a_pad, b_pad, h_pad, hcar, ht):
    t = pl.program_id(1)
    nt = pl.num_programs(1)
    tc = PROMPT_TC

    @pl.when(t == 0)
    def _init():
        hcar[...] = jnp.zeros_like(hcar)
        ht[...] = jnp.zeros_like(ht)

    neg_c_sp = (-LRU_C) * _softplus(-lam_ref[...])
    lane1 = lax.broadcasted_iota(jnp.int32, (1, LANES), 1)
    a2_row = jnp.where(lane1 < N_SSD_HEADS, -LOG2E * jnp.exp(alog_ref[...]), 0.0)
    L = SSD_CHUNK
    rr = lax.broadcasted_iota(jnp.int32, (L, L), 0)
    cc = lax.broadcasted_iota(jnp.int32, (L, L), 1)
    causal = cc <= rr
    tri = jnp.where(causal, 1.0, 0.0).astype(F32)
    mask_add = jnp.where(causal, 0.0, NEG_BIG).astype(F32)
    half = SSD_WIDTH // N_SSD_GROUPS

    chunks = [(n, c) for n in range(PROMPT_NB) for c in range(tc // L)]
    cums = [_ssd_cumdecay(dt_ref[n, L * c:L * (c + 1), :], a2_row, tri) for n, c in chunks]
    dts = [dt_ref[n, L * c:L * (c + 1), :] for n, c in chunks]
    cum_all = jnp.concatenate([cum2 for cum2, _ in cums], axis=0)
    cols_all = _spread(cum_all, selt_ref)
    ecol_all = _spread(jnp.exp2(cum_all), selp_ref)
    sdt_all = _spread(jnp.concatenate(
        [jnp.exp2(cum2[L - 1:L, :] - cum2) * dt for (cum2, _), dt in zip(cums, dts)], axis=0), selp_ref)

    for n in range(PROMPT_NB):
        a, b = _lru_coeffs(u_ref[n], wg_ref, ba_ref[...], bx_ref[...], neg_c_sp)
        hseq = _lru_scan_strided(a, b, hcar.at[n], a_pad.at[n], b_pad.at[n], h_pad.at[n])
        y_ref[n, :, 0:LRU_WIDTH] = _rms(hseq * gl_ref[n], glru_ref[...])

        for c in range(tc // L):
            k = chunks.index((n, c))
            rows = slice(L * c, L * (c + 1))
            krows = slice(L * k, L * (k + 1))
            xs = xbc_ref[n, rows, 0:SSD_WIDTH]
            bm = xbc_ref[n, rows, SSD_WIDTH:SSD_WIDTH + N_SSD_GROUPS * D_STATE]
            cm = xbc_ref[n, rows, SSD_WIDTH + N_SSD_GROUPS * D_STATE:SSD_CONV_DIM]
            y_diag = _ssd_diag(xs, bm, cm, dts[k], cols_all[krows, :], cums[k][1], mask_add)
            ecol = ecol_all[krows, :]
            xw = xs * sdt_all[krows, :]
            dec = ecol[L - 1:L, :]
            y_off_parts = []
            for g in range(N_SSD_GROUPS):
                htg = ht[n, g]
                cg = cm[:, D_STATE * g:D_STATE * (g + 1)].astype(BF16)
                y_off_parts.append(jnp.dot(cg, htg.astype(BF16), preferred_element_type=F32))
                bg_t = bm[:, D_STATE * g:D_STATE * (g + 1)].T.astype(BF16)
                st = jnp.dot(bg_t, xw[:, half * g:half * (g + 1)].astype(BF16), preferred_element_type=F32)
                ht[n, g] = htg * dec[:, half * g:half * (g + 1)] + st
            ys = y_diag + jnp.concatenate(y_off_parts, axis=1) * ecol
            y_ref[n, rows, LRU_WIDTH:MIX_WIDTH] = _ssd_gate_norm(
                ys, xs, zact_ref[n, rows, :], dskip_ref[...], gssd_ref[...])

    @pl.when(t == nt - 1)
    def _final():
        for n in range(PROMPT_NB):
            olh_ref[n] = hcar[n, 0:1, :]
            for g in range(N_SSD_GROUPS):
                osh_ref[n, half * g:half * (g + 1), :] = ht[n, g].T


def _param_specs(const):
    return [
        pl.BlockSpec((CONV_WIDTH, LRU_WIDTH), const),
        pl.BlockSpec((1, LRU_WIDTH), const),
        pl.BlockSpec((LRU_WIDTH // MXU_DIM, MXU_DIM, 2 * MXU_DIM), lambda *_: (0, 0, 0)),
        pl.BlockSpec((1, LRU_WIDTH), const),
        pl.BlockSpec((1, LRU_WIDTH), const),
        pl.BlockSpec((1, LRU_WIDTH), const),
        pl.BlockSpec((1, LRU_WIDTH), const),
        pl.BlockSpec((CONV_WIDTH, SSD_CONV_DIM), const),
        pl.BlockSpec((1, SSD_CONV_DIM), const),
        pl.BlockSpec((1, DT_PAD), const),
        pl.BlockSpec((1, DT_PAD), const),
        pl.BlockSpec((1, SSD_WIDTH), const),
        pl.BlockSpec((1, SSD_WIDTH), const),
    ]


def _head_selectors():
    k = np.arange(2 * LANES)[:, None] % LANES
    sel_t = (k == np.arange(N_SSD_HEADS * LANES)[None, :] // LANES).astype(np.float32)
    sel_p = (k == np.arange(SSD_WIDTH)[None, :] // SSD_HEAD_DIM).astype(np.float32)
    return jnp.asarray(sel_t, BF16), jnp.asarray(sel_p, BF16)


def _mixer_prompt(act, wg, b_a, b_x, lam, g_lru, a_log, d_skip, g_ssd, sel_t, sel_p):
    bsz, seq, _ = act.shape
    tc = PROMPT_TC
    nb = PROMPT_NB
    const = lambda b, t: (0, 0)
    in_specs = [
        pl.BlockSpec((nb, tc, LRU_WIDTH), lambda b, t: (b, t, 0)),
        pl.BlockSpec((nb, tc, LRU_WIDTH), lambda b, t: (b, t, 1)),
        pl.BlockSpec((nb, tc, SSD_WIDTH), lambda b, t: (b, t, 2)),
        pl.BlockSpec((nb, tc, SSD_CONV_DIM), lambda b, t: (b, t, 2)),
        pl.BlockSpec((nb, tc, DT_PAD), lambda b, t: (b, t, PROJ_MAIN // DT_PAD)),
        pl.BlockSpec((LRU_WIDTH // MXU_DIM, MXU_DIM, 2 * MXU_DIM), lambda b, t: (0, 0, 0)),
        pl.BlockSpec((1, LRU_WIDTH), const),
        pl.BlockSpec((1, LRU_WIDTH), const),
        pl.BlockSpec((1, LRU_WIDTH), const),
        pl.BlockSpec((1, LRU_WIDTH), const),
        pl.BlockSpec((1, DT_PAD), const),
        pl.BlockSpec((1, SSD_WIDTH), const),
        pl.BlockSpec((1, SSD_WIDTH), const),
        pl.BlockSpec((2 * LANES, N_SSD_HEADS * LANES), const),
        pl.BlockSpec((2 * LANES, SSD_WIDTH), const),
    ]
    out_shape = (
        jax.ShapeDtypeStruct((bsz, seq, MIX_WIDTH), F32),
        jax.ShapeDtypeStruct((bsz, 1, LRU_WIDTH), F32),
        jax.ShapeDtypeStruct((bsz, SSD_WIDTH, D_STATE), F32),
    )
    out_specs = (
        pl.BlockSpec((nb, tc, MIX_WIDTH), lambda b, t: (b, t, 0)),
        pl.BlockSpec((nb, 1, LRU_WIDTH), lambda b, t: (b, 0, 0)),
        pl.BlockSpec((nb, SSD_WIDTH, D_STATE), lambda b, t: (b, 0, 0)),
    )
    scratch = [
        pltpu.VMEM((nb, LRU_WIDTH // LANES, SUBLANES * SCAN_PITCH, LANES), F32),
        pltpu.VMEM((nb, LRU_WIDTH // LANES, SUBLANES * SCAN_PITCH, LANES), F32),
        pltpu.VMEM((nb, LRU_WIDTH // LANES, SUBLANES * SCAN_PITCH, LANES), F32),
        pltpu.VMEM((nb, SUBLANES, LRU_WIDTH), F32),
        pltpu.VMEM((nb, N_SSD_GROUPS, D_STATE, SSD_WIDTH // N_SSD_GROUPS), F32),
    ]
    return pl.pallas_call(
        _mixer_prompt_kernel,
        out_shape=out_shape,
        grid=(bsz // nb, seq // tc),
        in_specs=in_specs,
        out_specs=out_specs,
        scratch_shapes=scratch,
        compiler_params=pltpu.CompilerParams(
            dimension_semantics=("parallel", "arbitrary"), vmem_limit_bytes=VMEM_LIMIT),
        name="mixer_prompt",
    )(act, act, act, act, act, wg, b_a, b_x, lam, g_lru, a_log, d_skip, g_ssd, sel_t, sel_p)


def _mixer_sample_kernel(lx_ref, gate_ref, z_ref, xbc_ref, dt_ref,
                         slc_ref, slh_ref, ssc_ref, ssh_ref,
                         lcw_ref, lcb_ref, wg_ref, ba_ref, bx_ref, lam_ref, glru_ref,
                         scw_ref, scb_ref, dtb_ref, alog_ref, dskip_ref, gssd_ref, selt_ref, selp_ref,
                         y_ref, olc_ref, olh_ref, osc_ref, osh_ref,
                         ext_l, ext_s, pad_scr, yoff_scr, *, T):
    S = SAMPLE_SEQS
    P = SUBLANES
    K1 = CONV_WIDTH - 1
    R = S * P
    row_i = lax.broadcasted_iota(jnp.int32, (R, 1), 0) & (P - 1)
    valid = row_i < T

    def pad_rows(ref):
        width = ref.shape[-1]
        pad_scr[:, :, 0:width] = jnp.zeros((S, P, width), F32)
        pad_scr[:, 0:T, 0:width] = ref[...].reshape(S, T, width)
        return pad_scr[:, :, 0:width].reshape(R, width)

    ext_l[...] = jnp.zeros_like(ext_l)
    ext_s[...] = jnp.zeros_like(ext_s)
    for k in range(K1):
        ext_l[:, k, :] = slc_ref[k]
    ext_l[:, K1:K1 + T, :] = lx_ref[...].reshape(S, T, LRU_WIDTH)
    for k in range(K1):
        ext_s[:, k, :] = ssc_ref[k]
    ext_s[:, K1:K1 + T, :] = xbc_ref[...].reshape(S, T, SSD_CONV_DIM)
    for k in range(K1):
        olc_ref[k] = ext_l[:, T + k, :]
        osc_ref[k] = ext_s[:, T + k, :]

    el = ext_l[...].reshape(R, LRU_WIDTH)
    es = ext_s[...].reshape(R, SSD_CONV_DIM)

    def conv(e, w_ref, b_ref):
        out = b_ref[...] + e * w_ref[0:1, :]
        for k in range(1, CONV_WIDTH):
            out = out + pltpu.roll(e, R - k, axis=0) * w_ref[k:k + 1, :]
        return out

    u = conv(el, lcw_ref, lcb_ref)
    neg_c_sp = (-LRU_C) * _softplus(-lam_ref[...])
    a, b = _lru_coeffs(u, wg_ref, ba_ref[...], bx_ref[...], neg_c_sp)
    a, b = _scan_within_8(a, b)
    h0 = jnp.broadcast_to(slh_ref[...][:, None, :], (S, P, LRU_WIDTH)).reshape(R, LRU_WIDTH)
    hseq = a * h0 + b
    olh_ref[...] = hseq.reshape(S, P, LRU_WIDTH)[:, T - 1, :]
    gate = pad_rows(gate_ref)
    y_lru = _rms(hseq * _gelu_tanh(gate), glru_ref[...])

    xbc = conv(es, scw_ref, scb_ref)
    xbc = xbc * _sigmoid(xbc)
    xs = xbc[:, 0:SSD_WIDTH]
    bm = xbc[:, SSD_WIDTH:SSD_WIDTH + N_SSD_GROUPS * D_STATE]
    cm = xbc[:, SSD_WIDTH + N_SSD_GROUPS * D_STATE:]
    dt_raw = pad_rows(dt_ref)
    dt = jnp.where(valid, _softplus(dt_raw + dtb_ref[...]), 0.0)
    lane1 = lax.broadcasted_iota(jnp.int32, (1, LANES), 1)
    a2_row = jnp.where(lane1 < N_SSD_HEADS, -LOG2E * jnp.exp(alog_ref[...]), 0.0)

    rr = lax.broadcasted_iota(jnp.int32, (R, R), 0)
    cc = lax.broadcasted_iota(jnp.int32, (R, R), 1)
    allowed = (cc <= rr) & ((rr - cc) <= (rr & (P - 1)))
    tri = jnp.where(allowed, 1.0, 0.0).astype(F32)
    mask_add = jnp.where(allowed, 0.0, NEG_BIG).astype(F32)

    cum2, cum2_t = _ssd_cumdecay(dt, a2_row, tri)
    y_diag = _ssd_diag(xs, bm, cm, dt, _spread(cum2, selt_ref), cum2_t, mask_add)
    ecol = _spread(jnp.exp2(cum2), selp_ref)
    end2 = jnp.broadcast_to(cum2.reshape(S, P, LANES)[:, P - 1:P, :], (S, P, LANES)).reshape(R, LANES)
    xw = xs * _spread(jnp.exp2(end2 - cum2) * dt, selp_ref)
    ecum_t = jnp.exp2(cum2_t)

    half = SSD_WIDTH // N_SSD_GROUPS
    for q in range(S):
        r0 = P * q
        vq = jnp.broadcast_to(ecum_t[:, r0 + P - 1:r0 + P], (N_SSD_HEADS, LANES))
        for g in range(N_SSD_GROUPS):
            hqg = ssh_ref[q, half * g:half * (g + 1), :]
            cq = cm[r0:r0 + P, D_STATE * g:D_STATE * (g + 1)].astype(BF16)
            yoff_scr[r0:r0 + P, half * g:half * (g + 1)] = lax.dot_general(
                cq, hqg.astype(BF16), (((1,), (1,)), ((), ())), preferred_element_type=F32)
            bq = bm[r0:r0 + P, D_STATE * g:D_STATE * (g + 1)].astype(BF16)
            xq = xw[r0:r0 + P, half * g:half * (g + 1)].astype(BF16)
            st = lax.dot_general(xq, bq, (((0,), (0,)), ((), ())), preferred_element_type=F32)
            for e in range(N_SSD_HEADS // N_SSD_GROUPS):
                h = (N_SSD_HEADS // N_SSD_GROUPS) * g + e
                lo_r = SSD_HEAD_DIM * e
                osh_ref[q, SSD_HEAD_DIM * h:SSD_HEAD_DIM * (h + 1), :] = (
                    vq[h:h + 1, :] * hqg[lo_r:lo_r + SSD_HEAD_DIM, :] + st[lo_r:lo_r + SSD_HEAD_DIM, :])

    ys = y_diag + yoff_scr[...] * ecol
    z = pad_rows(z_ref)
    y_ssd = _ssd_gate_norm(ys, xs, z * _sigmoid(z), dskip_ref[...], gssd_ref[...])
    y_ref[:, 0:LRU_WIDTH] = y_lru.reshape(S, P, LRU_WIDTH)[:, 0:T, :].reshape(S * T, LRU_WIDTH)
    y_ref[:, LRU_WIDTH:MIX_WIDTH] = y_ssd.reshape(S, P, SSD_WIDTH)[:, 0:T, :].reshape(S * T, SSD_WIDTH)


def _mixer_sample(proj, T, st_lc, st_lh, st_sc, st_sh, params, sel_t, sel_p):
    nseq = proj.shape[0] // T
    S = SAMPLE_SEQS
    const = lambda i: (0, 0)
    in_specs = [
        pl.BlockSpec((S * T, LRU_WIDTH), lambda i: (i, 0)),
        pl.BlockSpec((S * T, LRU_WIDTH), lambda i: (i, 1)),
        pl.BlockSpec((S * T, SSD_WIDTH), lambda i: (i, 2)),
        pl.BlockSpec((S * T, SSD_CONV_DIM), lambda i: (i, 2)),
        pl.BlockSpec((S * T, DT_PAD), lambda i: (i, PROJ_MAIN // DT_PAD)),
        pl.BlockSpec((CONV_WIDTH - 1, S, LRU_WIDTH), lambda i: (0, i, 0)),
        pl.BlockSpec((S, LRU_WIDTH), lambda i: (i, 0)),
        pl.BlockSpec((CONV_WIDTH - 1, S, SSD_CONV_DIM), lambda i: (0, i, 0)),
        pl.BlockSpec((S, SSD_WIDTH, D_STATE), lambda i: (i, 0, 0)),
    ] + _param_specs(const) + [
        pl.BlockSpec((2 * LANES, N_SSD_HEADS * LANES), const),
        pl.BlockSpec((2 * LANES, SSD_WIDTH), const),
    ]
    out_shape = (
        jax.ShapeDtypeStruct((nseq * T, MIX_WIDTH), F32),
        jax.ShapeDtypeStruct((CONV_WIDTH - 1, nseq, LRU_WIDTH), F32),
        jax.ShapeDtypeStruct((nseq, LRU_WIDTH), F32),
        jax.ShapeDtypeStruct((CONV_WIDTH - 1, nseq, SSD_CONV_DIM), F32),
        jax.ShapeDtypeStruct((nseq, SSD_WIDTH, D_STATE), F32),
    )
    out_specs = (
        pl.BlockSpec((S * T, MIX_WIDTH), lambda i: (i, 0)),
        pl.BlockSpec((CONV_WIDTH - 1, S, LRU_WIDTH), lambda i: (0, i, 0)),
        pl.BlockSpec((S, LRU_WIDTH), lambda i: (i, 0)),
        pl.BlockSpec((CONV_WIDTH - 1, S, SSD_CONV_DIM), lambda i: (0, i, 0)),
        pl.BlockSpec((S, SSD_WIDTH, D_STATE), lambda i: (i, 0, 0)),
    )
    scratch = [
        pltpu.VMEM((S, SUBLANES, LRU_WIDTH), F32),
        pltpu.VMEM((S, SUBLANES, SSD_CONV_DIM), F32),
        pltpu.VMEM((S, SUBLANES, LRU_WIDTH), F32),
        pltpu.VMEM((S * SUBLANES, SSD_WIDTH), F32),
    ]
    return pl.pallas_call(
        functools.partial(_mixer_sample_kernel, T=T),
        out_shape=out_shape,
        grid=(nseq // S,),
        in_specs=in_specs,
        out_specs=out_specs,
        scratch_shapes=scratch,
        compiler_params=pltpu.CompilerParams(
            dimension_semantics=("parallel",), vmem_limit_bytes=VMEM_LIMIT),
        name="mixer_sample",
    )(proj, proj, proj, proj, proj, st_lc, st_lh, st_sc, st_sh, *params, sel_t, sel_p)


def _gate_weights(w_a, w_x):
    def tiles(w):
        per = MXU_DIM // LRU_BLOCK
        w4 = w.reshape(N_LRU_HEADS // per, per, LRU_BLOCK, LRU_BLOCK)
        eye = jnp.eye(per, dtype=w.dtype)
        t = jnp.einsum('jaik,ab->jaibk', w4, eye)
        return t.reshape(N_LRU_HEADS // per, MXU_DIM, MXU_DIM)
    return jnp.concatenate([tiles(w_a), tiles(w_x)], axis=2).astype(BF16)


def kernel(x_prompt, x_sample, state_lru_conv, state_lru_h, state_ssd_conv, state_ssd_h, g_mix, w_in,
           lru_conv_w, lru_conv_b, w_a, b_a, w_x, b_x, lam, g_lru_out, ssd_conv_w, ssd_conv_b, dt_bias,
           a_log, d_skip, g_ssd_out, w_out, g_mlp, w_up, w_down, g_final):
    depth = w_in.shape[0]
    assert depth == 1
    bp, seq, _ = x_prompt.shape
    bs, dseq, _ = x_sample.shape
    l = 0
    row = lambda v: v.reshape(1, -1)
    w_main, w_dt = _split_w_in(jnp.swapaxes(w_in, 1, 2)[l])
    params = (
        lru_conv_w[l], row(lru_conv_b[l]), _gate_weights(w_a[l], w_x[l]),
        row(b_a[l]), row(b_x[l]), row(lam[l]), row(g_lru_out[l]),
        ssd_conv_w[l], row(ssd_conv_b[l]),
        jnp.pad(row(dt_bias[l]), ((0, 0), (0, DT_PAD - N_SSD_HEADS))),
        jnp.pad(row(a_log[l]), ((0, 0), (0, DT_PAD - N_SSD_HEADS))),
        row(jnp.repeat(d_skip[l], SSD_HEAD_DIM)), row(g_ssd_out[l]),
    )
    w_out_b = w_out[l].astype(BF16)
    w_up_b = w_up[l].astype(BF16)
    w_down_b = w_down[l].astype(BF16)
    gmix = row(g_mix[l])
    gmlp = row(g_mlp[l])
    gfin = row(g_final)

    xp2 = x_prompt.reshape(bp * seq, D_MODEL)
    (lcw, lcb, wg, ba, bx, lam_r, glru, scw, scb, dtb, alog, dskip, gssd) = params
    act_p, p_lc, p_sc = _in_proj_prompt(xp2, bp, gmix, w_main, w_dt, lcw, lcb, scw, scb, dtb)
    sel_t, sel_p = _head_selectors()
    ymix_p, p_lh, p_sh = _mixer_prompt(
        act_p.reshape(bp, seq, PROJ_PAD), wg, ba, bx, lam_r, glru, alog, dskip, gssd, sel_t, sel_p)

    proj_s = _in_proj(x_sample, gmix, w_main, w_dt)
    ymix_s, s_lc, s_lh, s_sc, s_sh = _mixer_sample(
        proj_s, dseq, jnp.swapaxes(state_lru_conv[l], 0, 1), state_lru_h[l],
        jnp.swapaxes(state_ssd_conv[l], 0, 1),
        state_ssd_h[l].reshape(bs, SSD_WIDTH, D_STATE), params, sel_t, sel_p)
    y_prompt, y_sample = _out_mlp(xp2, ymix_p.reshape(bp * seq, MIX_WIDTH), x_sample, ymix_s,
                                  w_out_b, gmlp, w_up_b, w_down_b, gfin)

    hshape = (N_SSD_HEADS, SSD_HEAD_DIM, D_STATE)
    return (
        y_prompt.reshape(bp, seq, D_MODEL), y_sample,
        p_lc[None], p_lh.reshape(1, bp, LRU_WIDTH), p_sc[None], p_sh.reshape(1, bp, *hshape),
        jnp.swapaxes(s_lc, 0, 1)[None], s_lh[None], jnp.swapaxes(s_sc, 0, 1)[None],
        s_sh.reshape(1, bs, *hshape),
    )
```

```python
import functools
import math

import jax
import jax.numpy as jnp
import numpy as np
from jax import lax
from jax.experimental import pallas as pl
from jax.experimental.pallas import tpu as pltpu

F32 = jnp.float32
BF16 = jnp.bfloat16

D_MODEL = 1024
LRU_WIDTH = 1024
N_LRU_HEADS = 16
LRU_BLOCK = 64
LRU_C = 8.0
SSD_WIDTH = 1024
SSD_HEAD_DIM = 64
N_SSD_HEADS = 16
N_SSD_GROUPS = 2
D_STATE = 128
CONV_WIDTH = 4
SSD_CONV_DIM = SSD_WIDTH + 2 * N_SSD_GROUPS * D_STATE
D_FF = 4 * D_MODEL
EPS = 1e-6

LANES = 128
SUBLANES = 8
MXU_DIM = 256
DT_PAD = LANES
PROJ_MAIN = 2 * LRU_WIDTH + SSD_WIDTH + SSD_CONV_DIM
PROJ_PAD = PROJ_MAIN + DT_PAD
MIX_WIDTH = LRU_WIDTH + SSD_WIDTH
SSD_CHUNK = 128
PROMPT_TC = 256
PROMPT_NB = 2
ROW_TILE = 512
SPLIT_COLS = 1536
SAMPLE_SEQS = SSD_CHUNK // SUBLANES
SCAN_RUN = PROMPT_TC // SUBLANES
SCAN_PITCH = SCAN_RUN + 4
NEG_BIG = -1e30
LOG2E = 1.4426950408889634
VMEM_LIMIT = 56 * 1024 * 1024
VMEM_LIMIT_OUT = 60 * 1024 * 1024
HI = lax.Precision.HIGHEST


def _rms(x, g):
    ms = jnp.mean(x * x, axis=-1, keepdims=True)
    return x * lax.rsqrt(ms + EPS) * g


def _sigmoid(x):
    return 1.0 / (1.0 + jnp.exp(-x))


def _softplus(x):
    return jnp.maximum(x, 0.0) + jnp.log1p(jnp.exp(-jnp.abs(x)))


def _gelu_tanh(x):
    c = math.sqrt(2.0 / math.pi)
    return 0.5 * x * (1.0 + jnp.tanh(c * (x + 0.044715 * (x * x * x))))


def _lru_coeffs(u, wg_ref, b_a, b_x, neg_c_sp):
    ub = u.astype(BF16)
    r_parts, i_parts = [], []
    for j in range(LRU_WIDTH // MXU_DIM):
        g = jnp.dot(ub[:, MXU_DIM * j:MXU_DIM * (j + 1)], wg_ref[j], preferred_element_type=F32)
        r_parts.append(g[:, :MXU_DIM])
        i_parts.append(g[:, MXU_DIM:])
    r = _sigmoid(jnp.concatenate(r_parts, axis=1) + b_a)
    i = _sigmoid(jnp.concatenate(i_parts, axis=1) + b_x)
    log_a = r * neg_c_sp
    a = jnp.exp(log_a)
    th = jnp.tanh(log_a)
    v = (th + th) / (th - 1.0)
    mult = jnp.where(v > 0.0, v * lax.rsqrt(v), 0.0)
    return a, mult * (i * u)


def _scan_within_8(a, b):
    ridx = lax.broadcasted_iota(jnp.int32, a.shape, 0) & (SUBLANES - 1)
    for k in (1, 2, 4):
        a_s = pltpu.roll(a, k, axis=0)
        b_s = pltpu.roll(b, k, axis=0)
        m = ridx >= k
        b = jnp.where(m, a * b_s + b, b)
        a = jnp.where(m, a * a_s, a)
    return a, b


def _conv_slabs(ext, w_ref, b_ref, rows, first):
    parts = []
    for s in range(ext.shape[0]):
        cols = slice(LANES * s, LANES * (s + 1))
        acc = b_ref[:, cols] + ext[s, pl.ds(first, rows), :] * w_ref[0:1, cols]
        for k in range(1, CONV_WIDTH):
            acc = acc + ext[s, pl.ds(first + k, rows), :] * w_ref[k:k + 1, cols]
        parts.append(acc)
    return jnp.concatenate(parts, axis=1)


def _lru_scan_strided(a, b, hcar, a_pad, b_pad, h_pad):
    rows = a.shape[0]
    S = rows // SUBLANES
    nslab = LRU_WIDTH // LANES
    ridx = lax.broadcasted_iota(jnp.int32, (SUBLANES, LANES), 0)
    step = lambda ref, s, i: ref[s, pl.ds(i, SUBLANES, stride=SCAN_PITCH), :]
    for s in range(nslab):
        cols = slice(LANES * s, LANES * (s + 1))
        for j in range(SUBLANES):
            a_pad[s, SCAN_PITCH * j:SCAN_PITCH * j + S, :] = a[S * j:S * (j + 1), cols]
            b_pad[s, SCAN_PITCH * j:SCAN_PITCH * j + S, :] = b[S * j:S * (j + 1), cols]
    h = [jnp.zeros((SUBLANES, LANES), F32)] * nslab
    prod = [jnp.ones((SUBLANES, LANES), F32)] * nslab
    for i in range(S):
        for s in range(nslab):
            av = step(a_pad, s, i)
            h[s] = av * h[s] + step(b_pad, s, i)
            prod[s] = av * prod[s]
    for s in range(nslab):
        cols = slice(LANES * s, LANES * (s + 1))
        pcum, hcum = _scan_within_8(prod[s], h[s])
        cin = hcar[:, cols]
        ends = hcum + pcum * cin
        h[s] = jnp.where(ridx == 0, cin, pltpu.roll(ends, 1, axis=0))
        hcar[:, cols] = jnp.broadcast_to(ends[SUBLANES - 1:SUBLANES, :], (SUBLANES, LANES))
    for i in range(S):
        for s in range(nslab):
            h[s] = step(a_pad, s, i) * h[s] + step(b_pad, s, i)
            h_pad[s, pl.ds(i, SUBLANES, stride=SCAN_PITCH), :] = h[s]
    return jnp.concatenate(
        [jnp.concatenate([h_pad[s, SCAN_PITCH * j:SCAN_PITCH * j + S, :] for j in range(SUBLANES)], axis=0)
         for s in range(nslab)], axis=1)


def _ssd_cumdecay(dt, a2_row, tri):
    cum2 = jnp.dot(tri, dt * a2_row, precision=HI, preferred_element_type=F32)
    return cum2, cum2.T[0:N_SSD_HEADS, :]


def _ssd_diag(xs, bm, cm, dt, cols, cum2_t, mask_add):
    L = xs.shape[0]
    c2_t = cum2_t - jnp.log2(dt.T[0:N_SSD_HEADS, :])
    lane = lax.broadcasted_iota(jnp.int32, (L, LANES), 1)
    lo = lane < SSD_HEAD_DIM
    y_parts = []
    for g in range(N_SSD_GROUPS):
        bg = bm[:, D_STATE * g:D_STATE * (g + 1)].astype(BF16)
        cg = cm[:, D_STATE * g:D_STATE * (g + 1)].astype(BF16)
        cb = lax.dot_general(cg, bg, (((1,), (1,)), ((), ())), preferred_element_type=F32)
        for jj in range(N_SSD_HEADS // N_SSD_GROUPS // 2):
            j = (N_SSD_HEADS // N_SSD_GROUPS // 2) * g + jj
            h0, h1 = 2 * j, 2 * j + 1
            col0 = cols[:, LANES * h0:LANES * (h0 + 1)]
            col1 = cols[:, LANES * h1:LANES * (h1 + 1)]
            m0 = cb * jnp.exp2(col0 - c2_t[h0:h0 + 1, :] + mask_add)
            m1 = cb * jnp.exp2(col1 - c2_t[h1:h1 + 1, :] + mask_add)
            lhs = jnp.concatenate([m0, m1], axis=1).astype(BF16)
            xp = xs[:, LANES * j:LANES * (j + 1)]
            rhs = jnp.concatenate([jnp.where(lo, xp, 0.0), jnp.where(lo, 0.0, xp)], axis=0).astype(BF16)
            y_parts.append(jnp.dot(lhs, rhs, preferred_element_type=F32))
    return jnp.concatenate(y_parts, axis=1)


def _spread(v, sel_ref):
    p0 = v.astype(BF16)
    p1 = (v - p0.astype(F32)).astype(BF16)
    return jnp.dot(jnp.concatenate([p0, p1], axis=1), sel_ref[...], preferred_element_type=F32)


def _ssd_gate_norm(ys, xs, z_act, dskip, g_ssd):
    ys = ys + dskip * xs
    gated = ys * z_act
    half = SSD_WIDTH // N_SSD_GROUPS
    outs = []
    for g in range(N_SSD_GROUPS):
        outs.append(_rms(gated[:, half * g:half * (g + 1)], g_ssd[:, half * g:half * (g + 1)]))
    return jnp.concatenate(outs, axis=1)


def _split_w_in_kernel(wt_ref, wdt_t_ref, main_ref, dt_ref):
    main_ref[...] = wt_ref[...].T.astype(BF16)

    @pl.when(pl.program_id(0) == 0)
    def _dt():
        dt_ref[...] = jnp.zeros_like(dt_ref)
        dt_ref[:, 0:N_SSD_HEADS] = wdt_t_ref[...].T.astype(BF16)


def _split_w_in(w_t):
    cols, rows = w_t.shape
    blk = SPLIT_COLS
    return pl.pallas_call(
        _split_w_in_kernel,
        out_shape=(jax.ShapeDtypeStruct((rows, PROJ_MAIN), BF16), jax.ShapeDtypeStruct((rows, DT_PAD), BF16)),
        grid=(PROJ_MAIN // blk,),
        in_specs=[pl.BlockSpec((blk, rows), lambda j: (j, 0)),
                  pl.BlockSpec((N_SSD_HEADS, rows), lambda j: (PROJ_MAIN // N_SSD_HEADS, 0))],
        out_specs=(pl.BlockSpec((rows, blk), lambda j: (0, j)), pl.BlockSpec((rows, DT_PAD), lambda j: (0, 0))),
        compiler_params=pltpu.CompilerParams(dimension_semantics=("arbitrary",)),
        name="split_w_in",
    )(w_t, w_t)


def _inproj_kernel(x_ref, g_ref, w_ref, wdt_ref, o_ref):
    hn = _rms(x_ref[...].reshape(-1, D_MODEL), g_ref[...]).astype(BF16)
    o_ref[:, 0:PROJ_MAIN] = jnp.dot(hn, w_ref[...], preferred_element_type=F32)
    o_ref[:, PROJ_MAIN:PROJ_PAD] = jnp.dot(hn, wdt_ref[...], preferred_element_type=F32)


def _in_proj(x, g_mix, w_main, w_dt):
    nseq, ntok, _ = x.shape
    n = nseq * ntok
    return pl.pallas_call(
        _inproj_kernel,
        out_shape=jax.ShapeDtypeStruct((n, PROJ_PAD), F32),
        grid=(n // ROW_TILE,),
        in_specs=[
            pl.BlockSpec((ROW_TILE // ntok, ntok, D_MODEL), lambda i: (i, 0, 0)),
            pl.BlockSpec((1, D_MODEL), lambda i: (0, 0)),
            pl.BlockSpec((D_MODEL, PROJ_MAIN), lambda i: (0, 0), pipeline_mode=pl.Buffered(1)),
            pl.BlockSpec((D_MODEL, DT_PAD), lambda i: (0, 0), pipeline_mode=pl.Buffered(1)),
        ],
        out_specs=pl.BlockSpec((ROW_TILE, PROJ_PAD), lambda i: (i, 0)),
        compiler_params=pltpu.CompilerParams(
            dimension_semantics=("parallel",), vmem_limit_bytes=VMEM_LIMIT),
        name="in_proj",
    )(x, g_mix, w_main, w_dt)


def _inproj_prompt_kernel(x_ref, g_ref, w_ref, wdt_ref, lcw_ref, lcb_ref, scw_ref, scb_ref, dtb_ref,
                          o_ref, olc_ref, osc_ref, ext_l, ext_s, *, steps_per_seq):
    t = lax.rem(pl.program_id(0), steps_per_seq)
    rows = ROW_TILE
    hist = SUBLANES
    o1, o2, o3 = LRU_WIDTH, 2 * LRU_WIDTH, 2 * LRU_WIDTH + SSD_WIDTH

    @pl.when(t == 0)
    def _init():
        ext_l[:, 0:hist, :] = jnp.zeros((ext_l.shape[0], hist, LANES), F32)
        ext_s[:, 0:hist, :] = jnp.zeros((ext_s.shape[0], hist, LANES), F32)

    hn = _rms(x_ref[...], g_ref[...]).astype(BF16)
    lx = jnp.dot(hn, w_ref[:, 0:o1], preferred_element_type=F32)
    for s in range(ext_l.shape[0]):
        ext_l[s, hist:hist + rows, :] = lx[:, LANES * s:LANES * (s + 1)]
    xbc_in = jnp.dot(hn, w_ref[:, o3:PROJ_MAIN], preferred_element_type=F32)
    for s in range(ext_s.shape[0]):
        ext_s[s, hist:hist + rows, :] = xbc_in[:, LANES * s:LANES * (s + 1)]
    o_ref[:, o1:o2] = _gelu_tanh(jnp.dot(hn, w_ref[:, o1:o2], preferred_element_type=F32))
    z = jnp.dot(hn, w_ref[:, o2:o3], preferred_element_type=F32)
    o_ref[:, o2:o3] = z * _sigmoid(z)
    o_ref[:, PROJ_MAIN:PROJ_PAD] = _softplus(
        jnp.dot(hn, wdt_ref[...], preferred_element_type=F32) + dtb_ref[...])
    o_ref[:, 0:o1] = _conv_slabs(ext_l, lcw_ref, lcb_ref, rows, hist - (CONV_WIDTH - 1))
    xbc = _conv_slabs(ext_s, scw_ref, scb_ref, rows, hist - (CONV_WIDTH - 1))
    o_ref[:, o3:PROJ_MAIN] = xbc * _sigmoid(xbc)

    @pl.when(t == steps_per_seq - 1)
    def _final():
        last = slice(hist + rows - (CONV_WIDTH - 1), hist + rows)
        for s in range(ext_l.shape[0]):
            olc_ref[:, LANES * s:LANES * (s + 1)] = ext_l[s, last, :]
        for s in range(ext_s.shape[0]):
            osc_ref[:, LANES * s:LANES * (s + 1)] = ext_s[s, last, :]

    tail_l = ext_l[:, rows:rows + hist, :]
    tail_s = ext_s[:, rows:rows + hist, :]
    ext_l[:, 0:hist, :] = tail_l
    ext_s[:, 0:hist, :] = tail_s


def _in_proj_prompt(x2d, bsz, g_mix, w_main, w_dt, lcw, lcb, scw, scb, dtb):
    n = x2d.shape[0]
    steps_per_seq = n // bsz // ROW_TILE
    const = lambda i: (0, 0)
    return pl.pallas_call(
        functools.partial(_inproj_prompt_kernel, steps_per_seq=steps_per_seq),
        out_shape=(
            jax.ShapeDtypeStruct((n, PROJ_PAD), F32),
            jax.ShapeDtypeStruct((bsz, CONV_WIDTH - 1, LRU_WIDTH), F32),
            jax.ShapeDtypeStruct((bsz, CONV_WIDTH - 1, SSD_CONV_DIM), F32),
        ),
        grid=(n // ROW_TILE,),
        in_specs=[
            pl.BlockSpec((ROW_TILE, D_MODEL), lambda i: (i, 0)),
            pl.BlockSpec((1, D_MODEL), const),
            pl.BlockSpec((D_MODEL, PROJ_MAIN), const, pipeline_mode=pl.Buffered(1)),
            pl.BlockSpec((D_MODEL, DT_PAD), const, pipeline_mode=pl.Buffered(1)),
            pl.BlockSpec((CONV_WIDTH, LRU_WIDTH), const),
            pl.BlockSpec((1, LRU_WIDTH), const),
            pl.BlockSpec((CONV_WIDTH, SSD_CONV_DIM), const),
            pl.BlockSpec((1, SSD_CONV_DIM), const),
            pl.BlockSpec((1, DT_PAD), const),
        ],
        out_specs=(
            pl.BlockSpec((ROW_TILE, PROJ_PAD), lambda i: (i, 0)),
            pl.BlockSpec((None, CONV_WIDTH - 1, LRU_WIDTH), lambda i: (i // steps_per_seq, 0, 0)),
            pl.BlockSpec((None, CONV_WIDTH - 1, SSD_CONV_DIM), lambda i: (i // steps_per_seq, 0, 0)),
        ),
        scratch_shapes=[
            pltpu.VMEM((LRU_WIDTH // LANES, SUBLANES + ROW_TILE, LANES), F32),
            pltpu.VMEM((SSD_CONV_DIM // LANES, SUBLANES + ROW_TILE, LANES), F32),
        ],
        compiler_params=pltpu.CompilerParams(
            dimension_semantics=("arbitrary",), vmem_limit_bytes=VMEM_LIMIT),
        name="in_proj_prompt",
    )(x2d, g_mix, w_main, w_dt, lcw, lcb, scw, scb, dtb)


def _outmlp_kernel(xp_ref, yp_ref, xs_ref, ys_ref, wo_ref, gm_ref, wu_ref, wd_ref, gf_ref, op_ref, os_ref):
    last = pl.program_id(0) == pl.num_programs(0) - 1

    def run(x_ref, y_ref, o_ref):
        x = x_ref[...].reshape(-1, D_MODEL)
        x1 = x + jnp.dot(y_ref[...].astype(BF16), wo_ref[...], preferred_element_type=F32)
        m = _rms(x1, gm_ref[...]).astype(BF16)
        u = jnp.dot(m, wu_ref[...], preferred_element_type=F32)
        u = jnp.square(jnp.maximum(u, 0.0)).astype(BF16)
        x2 = x1 + jnp.dot(u, wd_ref[...], preferred_element_type=F32)
        o_ref[...] = _rms(x2, gf_ref[...]).reshape(o_ref.shape)

    @pl.when(jnp.logical_not(last))
    def _prompt():
        run(xp_ref, yp_ref, op_ref)

    @pl.when(last)
    def _sample():
        run(xs_ref, ys_ref, os_ref)


def _out_mlp(xp, ymix_p, xs, ymix_s, w_out_b, g_mlp, w_up_b, w_down_b, g_final):
    n_p = xp.shape[0]
    steps_p = n_p // ROW_TILE
    assert ymix_s.shape[0] == ROW_TILE
    const = lambda i: (0, 0)
    tile = lambda i: (jnp.minimum(i, steps_p - 1), 0)
    once = dict(pipeline_mode=pl.Buffered(1))
    return pl.pallas_call(
        _outmlp_kernel,
        out_shape=(jax.ShapeDtypeStruct((n_p, D_MODEL), F32), jax.ShapeDtypeStruct(xs.shape, F32)),
        grid=(steps_p + 1,),
        in_specs=[
            pl.BlockSpec((ROW_TILE, D_MODEL), tile),
            pl.BlockSpec((ROW_TILE, MIX_WIDTH), tile),
            pl.BlockSpec(xs.shape, lambda i: (0, 0, 0), **once),
            pl.BlockSpec((ROW_TILE, MIX_WIDTH), const, **once),
            pl.BlockSpec((MIX_WIDTH, D_MODEL), const, **once),
            pl.BlockSpec((1, D_MODEL), const),
            pl.BlockSpec((D_MODEL, D_FF), const, **once),
            pl.BlockSpec((D_FF, D_MODEL), const, **once),
            pl.BlockSpec((1, D_MODEL), const),
        ],
        out_specs=(pl.BlockSpec((ROW_TILE, D_MODEL), tile), pl.BlockSpec(xs.shape, lambda i: (0, 0, 0))),
        compiler_params=pltpu.CompilerParams(
            dimension_semantics=("arbitrary",), vmem_limit_bytes=VMEM_LIMIT_OUT),
        name="out_mlp",
    )(xp, ymix_p, xs, ymix_s, w_out_b, g_mlp, w_up_b, w_down_b, g_final)


def _mixer_prompt_kernel(u_ref, gl_ref, zact_ref, xbc_ref, dt_ref,
                         wg_ref, ba_ref, bx_ref, lam_ref, glru_ref, alog_ref, dskip_ref, gssd_ref,
                         selt_ref, selp_ref, wo_ref, wu_ref, wd_ref,
                         y_ref, olh_ref, osh_ref, wo_b_ref, wu_b_ref, wd_b_ref,
                         a_pad, b_pad, h_pad, hcar, ht):
    t = pl.program_id(1)
    nt = pl.num_programs(1)
    tc = PROMPT_TC
    wo_b_ref[...] = wo_ref[...].astype(BF16)
    wu_b_ref[...] = wu_ref[...].astype(BF16)
    wd_b_ref[...] = wd_ref[...].astype(BF16)

    @pl.when(t == 0)
    def _init():
        hcar[...] = jnp.zeros_like(hcar)
        ht[...] = jnp.zeros_like(ht)

    neg_c_sp = (-LRU_C) * _softplus(-lam_ref[...])
    lane1 = lax.broadcasted_iota(jnp.int32, (1, LANES), 1)
    a2_row = jnp.where(lane1 < N_SSD_HEADS, -LOG2E * jnp.exp(alog_ref[...]), 0.0)
    L = SSD_CHUNK
    rr = lax.broadcasted_iota(jnp.int32, (L, L), 0)
    cc = lax.broadcasted_iota(jnp.int32, (L, L), 1)
    causal = cc <= rr
    tri = jnp.where(causal, 1.0, 0.0).astype(F32)
    mask_add = jnp.where(causal, 0.0, NEG_BIG).astype(F32)
    half = SSD_WIDTH // N_SSD_GROUPS

    chunks = [(n, c) for n in range(PROMPT_NB) for c in range(tc // L)]
    cums = [_ssd_cumdecay(dt_ref[n, L * c:L * (c + 1), :], a2_row, tri) for n, c in chunks]
    dts = [dt_ref[n, L * c:L * (c + 1), :] for n, c in chunks]
    cum_all = jnp.concatenate([cum2 for cum2, _ in cums], axis=0)
    cols_all = _spread(cum_all, selt_ref)
    ecol_all = _spread(jnp.exp2(cum_all), selp_ref)
    sdt_all = _spread(jnp.concatenate(
        [jnp.exp2(cum2[L - 1:L, :] - cum2) * dt for (cum2, _), dt in zip(cums, dts)], axis=0), selp_ref)

    for n in range(PROMPT_NB):
        a, b = _lru_coeffs(u_ref[n], wg_ref, ba_ref[...], bx_ref[...], neg_c_sp)
        hseq = _lru_scan_strided(a, b, hcar.at[n], a_pad.at[n], b_pad.at[n], h_pad.at[n])
        y_ref[n, :, 0:LRU_WIDTH] = _rms(hseq * gl_ref[n], glru_ref[...])

        for c in range(tc // L):
            k = chunks.index((n, c))
            rows = slice(L * c, L * (c + 1))
            krows = slice(L * k, L * (k + 1))
            xs = xbc_ref[n, rows, 0:SSD_WIDTH]
            bm = xbc_ref[n, rows, SSD_WIDTH:SSD_WIDTH + N_SSD_GROUPS * D_STATE]
            cm = xbc_ref[n, rows, SSD_WIDTH + N_SSD_GROUPS * D_STATE:SSD_CONV_DIM]
            y_diag = _ssd_diag(xs, bm, cm, dts[k], cols_all[krows, :], cums[k][1], mask_add)
            ecol = ecol_all[krows, :]
            xw = xs * sdt_all[krows, :]
            dec = ecol[L - 1:L, :]
            y_off_parts = []
            for g in range(N_SSD_GROUPS):
                htg = ht[n, g]
                cg = cm[:, D_STATE * g:D_STATE * (g + 1)].astype(BF16)
                y_off_parts.append(jnp.dot(cg, htg.astype(BF16), preferred_element_type=F32))
                bg_t = bm[:, D_STATE * g:D_STATE * (g + 1)].T.astype(BF16)
                st = jnp.dot(bg_t, xw[:, half * g:half * (g + 1)].astype(BF16), preferred_element_type=F32)
                ht[n, g] = htg * dec[:, half * g:half * (g + 1)] + st
            ys = y_diag + jnp.concatenate(y_off_parts, axis=1) * ecol
            y_ref[n, rows, LRU_WIDTH:MIX_WIDTH] = _ssd_gate_norm(
                ys, xs, zact_ref[n, rows, :], dskip_ref[...], gssd_ref[...])

    @pl.when(t == nt - 1)
    def _final():
        for n in range(PROMPT_NB):
            olh_ref[n] = hcar[n, 0:1, :]
            for g in range(N_SSD_GROUPS):
                osh_ref[n, half * g:half * (g + 1), :] = ht[n, g].T


def _param_specs(const):
    return [
        pl.BlockSpec((CONV_WIDTH, LRU_WIDTH), const),
        pl.BlockSpec((1, LRU_WIDTH), const),
        pl.BlockSpec((LRU_WIDTH // MXU_DIM, MXU_DIM, 2 * MXU_DIM), lambda *_: (0, 0, 0)),
        pl.BlockSpec((1, LRU_WIDTH), const),
        pl.BlockSpec((1, LRU_WIDTH), const),
        pl.BlockSpec((1, LRU_WIDTH), const),
        pl.BlockSpec((1, LRU_WIDTH), const),
        pl.BlockSpec((CONV_WIDTH, SSD_CONV_DIM), const),
        pl.BlockSpec((1, SSD_CONV_DIM), const),
        pl.BlockSpec((1, DT_PAD), const),
        pl.BlockSpec((1, DT_PAD), const),
        pl.BlockSpec((1, SSD_WIDTH), const),
        pl.BlockSpec((1, SSD_WIDTH), const),
    ]


def _head_selectors():
    k = np.arange(2 * LANES)[:, None] % LANES
    sel_t = (k == np.arange(N_SSD_HEADS * LANES)[None, :] // LANES).astype(np.float32)
    sel_p = (k == np.arange(SSD_WIDTH)[None, :] // SSD_HEAD_DIM).astype(np.float32)
    return jnp.asarray(sel_t, BF16), jnp.asarray(sel_p, BF16)


def _mixer_prompt(act, wg, b_a, b_x, lam, g_lru, a_log, d_skip, g_ssd, sel_t, sel_p, w_out, w_up, w_down):
    bsz, seq, _ = act.shape
    tc = PROMPT_TC
    nb = PROMPT_NB
    steps = (bsz // nb) * (seq // tc)
    const = lambda b, t: (0, 0)
    w_slice = lambda b, t: (b * (seq // tc) + t, 0)
    assert all(w.shape[0] % (steps * 2 * SUBLANES) == 0 for w in (w_out, w_up, w_down))
    w_specs = [pl.BlockSpec((w.shape[0] // steps, w.shape[1]), w_slice) for w in (w_out, w_up, w_down)]
    in_specs = [
        pl.BlockSpec((nb, tc, LRU_WIDTH), lambda b, t: (b, t, 0)),
        pl.BlockSpec((nb, tc, LRU_WIDTH), lambda b, t: (b, t, 1)),
        pl.BlockSpec((nb, tc, SSD_WIDTH), lambda b, t: (b, t, 2)),
        pl.BlockSpec((nb, tc, SSD_CONV_DIM), lambda b, t: (b, t, 2)),
        pl.BlockSpec((nb, tc, DT_PAD), lambda b, t: (b, t, PROJ_MAIN // DT_PAD)),
        pl.BlockSpec((LRU_WIDTH // MXU_DIM, MXU_DIM, 2 * MXU_DIM), lambda b, t: (0, 0, 0)),
        pl.BlockSpec((1, LRU_WIDTH), const),
        pl.BlockSpec((1, LRU_WIDTH), const),
        pl.BlockSpec((1, LRU_WIDTH), const),
        pl.BlockSpec((1, LRU_WIDTH), const),
        pl.BlockSpec((1, DT_PAD), const),
        pl.BlockSpec((1, SSD_WIDTH), const),
        pl.BlockSpec((1, SSD_WIDTH), const),
        pl.BlockSpec((2 * LANES, N_SSD_HEADS * LANES), const),
        pl.BlockSpec((2 * LANES, SSD_WIDTH), const),
    ] + w_specs
    out_shape = (
        jax.ShapeDtypeStruct((bsz, seq, MIX_WIDTH), F32),
        jax.ShapeDtypeStruct((bsz, 1, LRU_WIDTH), F32),
        jax.ShapeDtypeStruct((bsz, SSD_WIDTH, D_STATE), F32),
    ) + tuple(jax.ShapeDtypeStruct(w.shape, BF16) for w in (w_out, w_up, w_down))
    out_specs = (
        pl.BlockSpec((nb, tc, MIX_WIDTH), lambda b, t: (b, t, 0)),
        pl.BlockSpec((nb, 1, LRU_WIDTH), lambda b, t: (b, 0, 0)),
        pl.BlockSpec((nb, SSD_WIDTH, D_STATE), lambda b, t: (b, 0, 0)),
    ) + tuple(w_specs)
    scratch = [
        pltpu.VMEM((nb, LRU_WIDTH // LANES, SUBLANES * SCAN_PITCH, LANES), F32),
        pltpu.VMEM((nb, LRU_WIDTH // LANES, SUBLANES * SCAN_PITCH, LANES), F32),
        pltpu.VMEM((nb, LRU_WIDTH // LANES, SUBLANES * SCAN_PITCH, LANES), F32),
        pltpu.VMEM((nb, SUBLANES, LRU_WIDTH), F32),
        pltpu.VMEM((nb, N_SSD_GROUPS, D_STATE, SSD_WIDTH // N_SSD_GROUPS), F32),
    ]
    return pl.pallas_call(
        _mixer_prompt_kernel,
        out_shape=out_shape,
        grid=(bsz // nb, seq // tc),
        in_specs=in_specs,
        out_specs=out_specs,
        scratch_shapes=scratch,
        compiler_params=pltpu.CompilerParams(
            dimension_semantics=("parallel", "arbitrary"), vmem_limit_bytes=VMEM_LIMIT),
        name="mixer_prompt",
    )(act, act, act, act, act, wg, b_a, b_x, lam, g_lru, a_log, d_skip, g_ssd, sel_t, sel_p,
      w_out, w_up, w_down)


def _mixer_sample_kernel(lx_ref, gate_ref, z_ref, xbc_ref, dt_ref,
                         slc_ref, slh_ref, ssc_ref, ssh_ref,
                         lcw_ref, lcb_ref, wg_ref, ba_ref, bx_ref, lam_ref, glru_ref,
                         scw_ref, scb_ref, dtb_ref, alog_ref, dskip_ref, gssd_ref, selt_ref, selp_ref,
                         y_ref, olc_ref, olh_ref, osc_ref, osh_ref,
                         ext_l, ext_s, pad_scr, yoff_scr, *, T):
    S = SAMPLE_SEQS
    P = SUBLANES
    K1 = CONV_WIDTH - 1
    R = S * P
    row_i = lax.broadcasted_iota(jnp.int32, (R, 1), 0) & (P - 1)
    valid = row_i < T

    def pad_rows(ref):
        width = ref.shape[-1]
        pad_scr[:, :, 0:width] = jnp.zeros((S, P, width), F32)
        pad_scr[:, 0:T, 0:width] = ref[...].reshape(S, T, width)
        return pad_scr[:, :, 0:width].reshape(R, width)

    ext_l[...] = jnp.zeros_like(ext_l)
    ext_s[...] = jnp.zeros_like(ext_s)
    for k in range(K1):
        ext_l[:, k, :] = slc_ref[k]
    ext_l[:, K1:K1 + T, :] = lx_ref[...].reshape(S, T, LRU_WIDTH)
    for k in range(K1):
        ext_s[:, k, :] = ssc_ref[k]
    ext_s[:, K1:K1 + T, :] = xbc_ref[...].reshape(S, T, SSD_CONV_DIM)
    for k in range(K1):
        olc_ref[k] = ext_l[:, T + k, :]
        osc_ref[k] = ext_s[:, T + k, :]

    el = ext_l[...].reshape(R, LRU_WIDTH)
    es = ext_s[...].reshape(R, SSD_CONV_DIM)

    def conv(e, w_ref, b_ref):
        out = b_ref[...] + e * w_ref[0:1, :]
        for k in range(1, CONV_WIDTH):
            out = out + pltpu.roll(e, R - k, axis=0) * w_ref[k:k + 1, :]
        return out

    u = conv(el, lcw_ref, lcb_ref)
    neg_c_sp = (-LRU_C) * _softplus(-lam_ref[...])
    a, b = _lru_coeffs(u, wg_ref, ba_ref[...], bx_ref[...], neg_c_sp)
    a, b = _scan_within_8(a, b)
    h0 = jnp.broadcast_to(slh_ref[...][:, None, :], (S, P, LRU_WIDTH)).reshape(R, LRU_WIDTH)
    hseq = a * h0 + b
    olh_ref[...] = hseq.reshape(S, P, LRU_WIDTH)[:, T - 1, :]
    gate = pad_rows(gate_ref)
    y_lru = _rms(hseq * _gelu_tanh(gate), glru_ref[...])

    xbc = conv(es, scw_ref, scb_ref)
    xbc = xbc * _sigmoid(xbc)
    xs = xbc[:, 0:SSD_WIDTH]
    bm = xbc[:, SSD_WIDTH:SSD_WIDTH + N_SSD_GROUPS * D_STATE]
    cm = xbc[:, SSD_WIDTH + N_SSD_GROUPS * D_STATE:]
    dt_raw = pad_rows(dt_ref)
    dt = jnp.where(valid, _softplus(dt_raw + dtb_ref[...]), 0.0)
    lane1 = lax.broadcasted_iota(jnp.int32, (1, LANES), 1)
    a2_row = jnp.where(lane1 < N_SSD_HEADS, -LOG2E * jnp.exp(alog_ref[...]), 0.0)

    rr = lax.broadcasted_iota(jnp.int32, (R, R), 0)
    cc = lax.broadcasted_iota(jnp.int32, (R, R), 1)
    allowed = (cc <= rr) & ((rr - cc) <= (rr & (P - 1)))
    tri = jnp.where(allowed, 1.0, 0.0).astype(F32)
    mask_add = jnp.where(allowed, 0.0, NEG_BIG).astype(F32)

    cum2, cum2_t = _ssd_cumdecay(dt, a2_row, tri)
    y_diag = _ssd_diag(xs, bm, cm, dt, _spread(cum2, selt_ref), cum2_t, mask_add)
    ecol = _spread(jnp.exp2(cum2), selp_ref)
    end2 = jnp.broadcast_to(cum2.reshape(S, P, LANES)[:, P - 1:P, :], (S, P, LANES)).reshape(R, LANES)
    xw = xs * _spread(jnp.exp2(end2 - cum2) * dt, selp_ref)
    ecum_t = jnp.exp2(cum2_t)

    half = SSD_WIDTH // N_SSD_GROUPS
    for q in range(S):
        r0 = P * q
        vq = jnp.broadcast_to(ecum_t[:, r0 + P - 1:r0 + P], (N_SSD_HEADS, LANES))
        for g in range(N_SSD_GROUPS):
            hqg = ssh_ref[q, half * g:half * (g + 1), :]
            cq = cm[r0:r0 + P, D_STATE * g:D_STATE * (g + 1)].astype(BF16)
            yoff_scr[r0:r0 + P, half * g:half * (g + 1)] = lax.dot_general(
                cq, hqg.astype(BF16), (((1,), (1,)), ((), ())), preferred_element_type=F32)
            bq = bm[r0:r0 + P, D_STATE * g:D_STATE * (g + 1)].astype(BF16)
            xq = xw[r0:r0 + P, half * g:half * (g + 1)].astype(BF16)
            st = lax.dot_general(xq, bq, (((0,), (0,)), ((), ())), preferred_element_type=F32)
            for e in range(N_SSD_HEADS // N_SSD_GROUPS):
                h = (N_SSD_HEADS // N_SSD_GROUPS) * g + e
                lo_r = SSD_HEAD_DIM * e
                osh_ref[q, SSD_HEAD_DIM * h:SSD_HEAD_DIM * (h + 1), :] = (
                    vq[h:h + 1, :] * hqg[lo_r:lo_r + SSD_HEAD_DIM, :] + st[lo_r:lo_r + SSD_HEAD_DIM, :])

    ys = y_diag + yoff_scr[...] * ecol
    z = pad_rows(z_ref)
    y_ssd = _ssd_gate_norm(ys, xs, z * _sigmoid(z), dskip_ref[...], gssd_ref[...])
    y_ref[:, 0:LRU_WIDTH] = y_lru.reshape(S, P, LRU_WIDTH)[:, 0:T, :].reshape(S * T, LRU_WIDTH)
    y_ref[:, LRU_WIDTH:MIX_WIDTH] = y_ssd.reshape(S, P, SSD_WIDTH)[:, 0:T, :].reshape(S * T, SSD_WIDTH)


def _mixer_sample(proj, T, st_lc, st_lh, st_sc, st_sh, params, sel_t, sel_p):
    nseq = proj.shape[0] // T
    S = SAMPLE_SEQS
    const = lambda i: (0, 0)
    in_specs = [
        pl.BlockSpec((S * T, LRU_WIDTH), lambda i: (i, 0)),
        pl.BlockSpec((S * T, LRU_WIDTH), lambda i: (i, 1)),
        pl.BlockSpec((S * T, SSD_WIDTH), lambda i: (i, 2)),
        pl.BlockSpec((S * T, SSD_CONV_DIM), lambda i: (i, 2)),
        pl.BlockSpec((S * T, DT_PAD), lambda i: (i, PROJ_MAIN // DT_PAD)),
        pl.BlockSpec((CONV_WIDTH - 1, S, LRU_WIDTH), lambda i: (0, i, 0)),
        pl.BlockSpec((S, LRU_WIDTH), lambda i: (i, 0)),
        pl.BlockSpec((CONV_WIDTH - 1, S, SSD_CONV_DIM), lambda i: (0, i, 0)),
        pl.BlockSpec((S, SSD_WIDTH, D_STATE), lambda i: (i, 0, 0)),
    ] + _param_specs(const) + [
        pl.BlockSpec((2 * LANES, N_SSD_HEADS * LANES), const),
        pl.BlockSpec((2 * LANES, SSD_WIDTH), const),
    ]
    out_shape = (
        jax.ShapeDtypeStruct((nseq * T, MIX_WIDTH), F32),
        jax.ShapeDtypeStruct((CONV_WIDTH - 1, nseq, LRU_WIDTH), F32),
        jax.ShapeDtypeStruct((nseq, LRU_WIDTH), F32),
        jax.ShapeDtypeStruct((CONV_WIDTH - 1, nseq, SSD_CONV_DIM), F32),
        jax.ShapeDtypeStruct((nseq, SSD_WIDTH, D_STATE), F32),
    )
    out_specs = (
        pl.BlockSpec((S * T, MIX_WIDTH), lambda i: (i, 0)),
        pl.BlockSpec((CONV_WIDTH - 1, S, LRU_WIDTH), lambda i: (0, i, 0)),
        pl.BlockSpec((S, LRU_WIDTH), lambda i: (i, 0)),
        pl.BlockSpec((CONV_WIDTH - 1, S, SSD_CONV_DIM), lambda i: (0, i, 0)),
        pl.BlockSpec((S, SSD_WIDTH, D_STATE), lambda i: (i, 0, 0)),
    )
    scratch = [
        pltpu.VMEM((S, SUBLANES, LRU_WIDTH), F32),
        pltpu.VMEM((S, SUBLANES, SSD_CONV_DIM), F32),
        pltpu.VMEM((S, SUBLANES, LRU_WIDTH), F32),
        pltpu.VMEM((S * SUBLANES, SSD_WIDTH), F32),
    ]
    return pl.pallas_call(
        functools.partial(_mixer_sample_kernel, T=T),
        out_shape=out_shape,
        grid=(nseq // S,),
        in_specs=in_specs,
        out_specs=out_specs,
        scratch_shapes=scratch,
        compiler_params=pltpu.CompilerParams(
            dimension_semantics=("parallel",), vmem_limit_bytes=VMEM_LIMIT),
        name="mixer_sample",
    )(proj, proj, proj, proj, proj, st_lc, st_lh, st_sc, st_sh, *params, sel_t, sel_p)


def _gate_weights(w_a, w_x):
    def tiles(w):
        per = MXU_DIM // LRU_BLOCK
        w4 = w.reshape(N_LRU_HEADS // per, per, LRU_BLOCK, LRU_BLOCK)
        eye = jnp.eye(per, dtype=w.dtype)
        t = jnp.einsum('jaik,ab->jaibk', w4, eye)
        return t.reshape(N_LRU_HEADS // per, MXU_DIM, MXU_DIM)
    return jnp.concatenate([tiles(w_a), tiles(w_x)], axis=2).astype(BF16)


def kernel(x_prompt, x_sample, state_lru_conv, state_lru_h, state_ssd_conv, state_ssd_h, g_mix, w_in,
           lru_conv_w, lru_conv_b, w_a, b_a, w_x, b_x, lam, g_lru_out, ssd_conv_w, ssd_conv_b, dt_bias,
           a_log, d_skip, g_ssd_out, w_out, g_mlp, w_up, w_down, g_final):
    depth = w_in.shape[0]
    assert depth == 1
    bp, seq, _ = x_prompt.shape
    bs, dseq, _ = x_sample.shape
    l = 0
    row = lambda v: v.reshape(1, -1)
    w_main, w_dt = _split_w_in(jnp.swapaxes(w_in, 1, 2)[l])
    params = (
        lru_conv_w[l], row(lru_conv_b[l]), _gate_weights(w_a[l], w_x[l]),
        row(b_a[l]), row(b_x[l]), row(lam[l]), row(g_lru_out[l]),
        ssd_conv_w[l], row(ssd_conv_b[l]),
        jnp.pad(row(dt_bias[l]), ((0, 0), (0, DT_PAD - N_SSD_HEADS))),
        jnp.pad(row(a_log[l]), ((0, 0), (0, DT_PAD - N_SSD_HEADS))),
        row(jnp.repeat(d_skip[l], SSD_HEAD_DIM)), row(g_ssd_out[l]),
    )
    gmix = row(g_mix[l])
    gmlp = row(g_mlp[l])
    gfin = row(g_final)

    xp2 = x_prompt.reshape(bp * seq, D_MODEL)
    (lcw, lcb, wg, ba, bx, lam_r, glru, scw, scb, dtb, alog, dskip, gssd) = params
    act_p, p_lc, p_sc = _in_proj_prompt(xp2, bp, gmix, w_main, w_dt, lcw, lcb, scw, scb, dtb)
    sel_t, sel_p = _head_selectors()
    ymix_p, p_lh, p_sh, w_out_b, w_up_b, w_down_b = _mixer_prompt(
        act_p.reshape(bp, seq, PROJ_PAD), wg, ba, bx, lam_r, glru, alog, dskip, gssd, sel_t, sel_p,
        w_out[l], w_up[l], w_down[l])

    proj_s = _in_proj(x_sample, gmix, w_main, w_dt)
    ymix_s, s_lc, s_lh, s_sc, s_sh = _mixer_sample(
        proj_s, dseq, jnp.swapaxes(state_lru_conv[l], 0, 1), state_lru_h[l],
        jnp.swapaxes(state_ssd_conv[l], 0, 1),
        state_ssd_h[l].reshape(bs, SSD_WIDTH, D_STATE), params, sel_t, sel_p)
    y_prompt, y_sample = _out_mlp(xp2, ymix_p.reshape(bp * seq, MIX_WIDTH), x_sample, ymix_s,
                                  w_out_b, gmlp, w_up_b, w_down_b, gfin)

    hshape = (N_SSD_HEADS, SSD_HEAD_DIM, D_STATE)
    return (
        y_prompt.reshape(bp, seq, D_MODEL), y_sample,
        p_lc[None], p_lh.reshape(1, bp, LRU_WIDTH), p_sc[None], p_sh.reshape(1, bp, *hshape),
        jnp.swapaxes(s_lc, 0, 1)[None], s_lh[None], jnp.swapaxes(s_sc, 0, 1)[None],
        s_sh.reshape(1, bs, *hshape),
    )
```

```python
import functools
import math

import jax
import jax.numpy as jnp
import numpy as np
from jax import lax
from jax.experimental import pallas as pl
from jax.experimental.pallas import tpu as pltpu

F32 = jnp.float32
BF16 = jnp.bfloat16

D_MODEL = 1024
LRU_WIDTH = 1024
N_LRU_HEADS = 16
LRU_BLOCK = 64
LRU_C = 8.0
SSD_WIDTH = 1024
SSD_HEAD_DIM = 64
N_SSD_HEADS = 16
N_SSD_GROUPS = 2
D_STATE = 128
CONV_WIDTH = 4
SSD_CONV_DIM = SSD_WIDTH + 2 * N_SSD_GROUPS * D_STATE
D_FF = 4 * D_MODEL
EPS = 1e-6

LANES = 128
SUBLANES = 8
MXU_DIM = 256
DT_PAD = LANES
PROJ_MAIN = 2 * LRU_WIDTH + SSD_WIDTH + SSD_CONV_DIM
PROJ_PAD = PROJ_MAIN + DT_PAD
MIX_WIDTH = LRU_WIDTH + SSD_WIDTH
SSD_CHUNK = 128
PROMPT_TC = 256
PROMPT_NB = 2
ROW_TILE = 512
SPLIT_COLS = 1536
SAMPLE_SEQS = SSD_CHUNK // SUBLANES
SCAN_RUN = PROMPT_TC // SUBLANES
SCAN_PITCH = SCAN_RUN + 4
NEG_BIG = -1e30
LOG2E = 1.4426950408889634
VMEM_LIMIT = 56 * 1024 * 1024
VMEM_LIMIT_OUT = 60 * 1024 * 1024
HI = lax.Precision.HIGHEST


def _rms(x, g):
    ms = jnp.mean(x * x, axis=-1, keepdims=True)
    return x * lax.rsqrt(ms + EPS) * g


def _sigmoid(x):
    return 1.0 / (1.0 + jnp.exp(-x))


def _softplus(x):
    return jnp.maximum(x, 0.0) + jnp.log1p(jnp.exp(-jnp.abs(x)))


def _gelu_tanh(x):
    c = math.sqrt(2.0 / math.pi)
    return 0.5 * x * (1.0 + jnp.tanh(c * (x + 0.044715 * (x * x * x))))


def _lru_coeffs(u, wg_ref, b_a, b_x, neg_c_sp):
    ub = u.astype(BF16)
    r_parts, i_parts = [], []
    for j in range(LRU_WIDTH // MXU_DIM):
        g = jnp.dot(ub[:, MXU_DIM * j:MXU_DIM * (j + 1)], wg_ref[j], preferred_element_type=F32)
        r_parts.append(g[:, :MXU_DIM])
        i_parts.append(g[:, MXU_DIM:])
    r = _sigmoid(jnp.concatenate(r_parts, axis=1) + b_a)
    i = _sigmoid(jnp.concatenate(i_parts, axis=1) + b_x)
    log_a = r * neg_c_sp
    a = jnp.exp(log_a)
    th = jnp.tanh(log_a)
    v = (th + th) / (th - 1.0)
    mult = jnp.where(v > 0.0, v * lax.rsqrt(v), 0.0)
    return a, mult * (i * u)


def _scan_within_8(a, b):
    ridx = lax.broadcasted_iota(jnp.int32, a.shape, 0) & (SUBLANES - 1)
    for k in (1, 2, 4):
        a_s = pltpu.roll(a, k, axis=0)
        b_s = pltpu.roll(b, k, axis=0)
        m = ridx >= k
        b = jnp.where(m, a * b_s + b, b)
        a = jnp.where(m, a * a_s, a)
    return a, b


def _conv_slabs(ext, w_ref, b_ref, rows, first):
    parts = []
    for s in range(ext.shape[0]):
        cols = slice(LANES * s, LANES * (s + 1))
        acc = b_ref[:, cols] + ext[s, pl.ds(first, rows), :] * w_ref[0:1, cols]
        for k in range(1, CONV_WIDTH):
            acc = acc + ext[s, pl.ds(first + k, rows), :] * w_ref[k:k + 1, cols]
        parts.append(acc)
    return jnp.concatenate(parts, axis=1)


def _lru_scan_strided(a, b, hcar, a_pad, b_pad, h_pad):
    rows = a.shape[0]
    S = rows // SUBLANES
    nslab = LRU_WIDTH // LANES
    ridx = lax.broadcasted_iota(jnp.int32, (SUBLANES, LANES), 0)
    step = lambda ref, s, i: ref[s, pl.ds(i, SUBLANES, stride=SCAN_PITCH), :]
    for s in range(nslab):
        cols = slice(LANES * s, LANES * (s + 1))
        for j in range(SUBLANES):
            a_pad[s, SCAN_PITCH * j:SCAN_PITCH * j + S, :] = a[S * j:S * (j + 1), cols]
            b_pad[s, SCAN_PITCH * j:SCAN_PITCH * j + S, :] = b[S * j:S * (j + 1), cols]
    h = [jnp.zeros((SUBLANES, LANES), F32)] * nslab
    prod = [jnp.ones((SUBLANES, LANES), F32)] * nslab
    for i in range(S):
        for s in range(nslab):
            av = step(a_pad, s, i)
            h[s] = av * h[s] + step(b_pad, s, i)
            prod[s] = av * prod[s]
    for s in range(nslab):
        cols = slice(LANES * s, LANES * (s + 1))
        pcum, hcum = _scan_within_8(prod[s], h[s])
        cin = hcar[:, cols]
        ends = hcum + pcum * cin
        h[s] = jnp.where(ridx == 0, cin, pltpu.roll(ends, 1, axis=0))
        hcar[:, cols] = jnp.broadcast_to(ends[SUBLANES - 1:SUBLANES, :], (SUBLANES, LANES))
    for i in range(S):
        for s in range(nslab):
            h[s] = step(a_pad, s, i) * h[s] + step(b_pad, s, i)
            h_pad[s, pl.ds(i, SUBLANES, stride=SCAN_PITCH), :] = h[s]
    return jnp.concatenate(
        [jnp.concatenate([h_pad[s, SCAN_PITCH * j:SCAN_PITCH * j + S, :] for j in range(SUBLANES)], axis=0)
         for s in range(nslab)], axis=1)


def _ssd_cumdecay(dt, a2_row, tri):
    cum2 = jnp.dot(tri, dt * a2_row, precision=HI, preferred_element_type=F32)
    return cum2, cum2.T[0:N_SSD_HEADS, :]


def _ssd_diag(xs, bm, cm, dt, cols, cum2_t, mask_add):
    L = xs.shape[0]
    c2_t = cum2_t - jnp.log2(dt.T[0:N_SSD_HEADS, :])
    lane = lax.broadcasted_iota(jnp.int32, (L, LANES), 1)
    lo = lane < SSD_HEAD_DIM
    y_parts = []
    for g in range(N_SSD_GROUPS):
        bg = bm[:, D_STATE * g:D_STATE * (g + 1)].astype(BF16)
        cg = cm[:, D_STATE * g:D_STATE * (g + 1)].astype(BF16)
        cb = lax.dot_general(cg, bg, (((1,), (1,)), ((), ())), preferred_element_type=F32)
        for jj in range(N_SSD_HEADS // N_SSD_GROUPS // 2):
            j = (N_SSD_HEADS // N_SSD_GROUPS // 2) * g + jj
            h0, h1 = 2 * j, 2 * j + 1
            col0 = cols[:, LANES * h0:LANES * (h0 + 1)]
            col1 = cols[:, LANES * h1:LANES * (h1 + 1)]
            m0 = cb * jnp.exp2(col0 - c2_t[h0:h0 + 1, :] + mask_add)
            m1 = cb * jnp.exp2(col1 - c2_t[h1:h1 + 1, :] + mask_add)
            lhs = jnp.concatenate([m0, m1], axis=1).astype(BF16)
            xp = xs[:, LANES * j:LANES * (j + 1)]
            rhs = jnp.concatenate([jnp.where(lo, xp, 0.0), jnp.where(lo, 0.0, xp)], axis=0).astype(BF16)
            y_parts.append(jnp.dot(lhs, rhs, preferred_element_type=F32))
    return jnp.concatenate(y_parts, axis=1)


def _spread(v, sel_ref):
    p0 = v.astype(BF16)
    p1 = (v - p0.astype(F32)).astype(BF16)
    return jnp.dot(jnp.concatenate([p0, p1], axis=1), sel_ref[...], preferred_element_type=F32)


def _ssd_gate_norm(ys, xs, z_act, dskip, g_ssd):
    ys = ys + dskip * xs
    gated = ys * z_act
    half = SSD_WIDTH // N_SSD_GROUPS
    outs = []
    for g in range(N_SSD_GROUPS):
        outs.append(_rms(gated[:, half * g:half * (g + 1)], g_ssd[:, half * g:half * (g + 1)]))
    return jnp.concatenate(outs, axis=1)


def _split_w_in_kernel(wt_ref, wdt_t_ref, main_ref, dt_ref):
    main_ref[...] = wt_ref[...].T.astype(BF16)

    @pl.when(pl.program_id(0) == 0)
    def _dt():
        dt_ref[...] = jnp.zeros_like(dt_ref)
        dt_ref[:, 0:N_SSD_HEADS] = wdt_t_ref[...].T.astype(BF16)


def _split_w_in(w_t):
    cols, rows = w_t.shape
    blk = SPLIT_COLS
    return pl.pallas_call(
        _split_w_in_kernel,
        out_shape=(jax.ShapeDtypeStruct((rows, PROJ_MAIN), BF16), jax.ShapeDtypeStruct((rows, DT_PAD), BF16)),
        grid=(PROJ_MAIN // blk,),
        in_specs=[pl.BlockSpec((blk, rows), lambda j: (j, 0)),
                  pl.BlockSpec((N_SSD_HEADS, rows), lambda j: (PROJ_MAIN // N_SSD_HEADS, 0))],
        out_specs=(pl.BlockSpec((rows, blk), lambda j: (0, j)), pl.BlockSpec((rows, DT_PAD), lambda j: (0, 0))),
        compiler_params=pltpu.CompilerParams(dimension_semantics=("arbitrary",)),
        name="split_w_in",
    )(w_t, w_t)


def _inproj_kernel(x_ref, xs_ref, g_ref, w_ref, wdt_ref, lcw_ref, lcb_ref, scw_ref, scb_ref, dtb_ref,
                   o_ref, olc_ref, osc_ref, os_ref, ext_l, ext_s, *, steps_per_seq):
    last_step = pl.program_id(0) == pl.num_programs(0) - 1
    t = lax.rem(pl.program_id(0), steps_per_seq)
    rows = ROW_TILE
    hist = SUBLANES
    o1, o2, o3 = LRU_WIDTH, 2 * LRU_WIDTH, 2 * LRU_WIDTH + SSD_WIDTH

    @pl.when(last_step)
    def _sample():
        hn = _rms(xs_ref[...].reshape(-1, D_MODEL), g_ref[...]).astype(BF16)
        os_ref[:, 0:PROJ_MAIN] = jnp.dot(hn, w_ref[...], preferred_element_type=F32)
        os_ref[:, PROJ_MAIN:PROJ_PAD] = jnp.dot(hn, wdt_ref[...], preferred_element_type=F32)

    @pl.when(jnp.logical_not(last_step))
    def _prompt():
        @pl.when(t == 0)
        def _init():
            ext_l[:, 0:hist, :] = jnp.zeros((ext_l.shape[0], hist, LANES), F32)
            ext_s[:, 0:hist, :] = jnp.zeros((ext_s.shape[0], hist, LANES), F32)

        hn = _rms(x_ref[...], g_ref[...]).astype(BF16)
        lx = jnp.dot(hn, w_ref[:, 0:o1], preferred_element_type=F32)
        for s in range(ext_l.shape[0]):
            ext_l[s, hist:hist + rows, :] = lx[:, LANES * s:LANES * (s + 1)]
        xbc_in = jnp.dot(hn, w_ref[:, o3:PROJ_MAIN], preferred_element_type=F32)
        for s in range(ext_s.shape[0]):
            ext_s[s, hist:hist + rows, :] = xbc_in[:, LANES * s:LANES * (s + 1)]
        o_ref[:, o1:o2] = _gelu_tanh(jnp.dot(hn, w_ref[:, o1:o2], preferred_element_type=F32))
        z = jnp.dot(hn, w_ref[:, o2:o3], preferred_element_type=F32)
        o_ref[:, o2:o3] = z * _sigmoid(z)
        o_ref[:, PROJ_MAIN:PROJ_PAD] = _softplus(
            jnp.dot(hn, wdt_ref[...], preferred_element_type=F32) + dtb_ref[...])
        o_ref[:, 0:o1] = _conv_slabs(ext_l, lcw_ref, lcb_ref, rows, hist - (CONV_WIDTH - 1))
        xbc = _conv_slabs(ext_s, scw_ref, scb_ref, rows, hist - (CONV_WIDTH - 1))
        o_ref[:, o3:PROJ_MAIN] = xbc * _sigmoid(xbc)

        @pl.when(t == steps_per_seq - 1)
        def _final():
            last = slice(hist + rows - (CONV_WIDTH - 1), hist + rows)
            for s in range(ext_l.shape[0]):
                olc_ref[:, LANES * s:LANES * (s + 1)] = ext_l[s, last, :]
            for s in range(ext_s.shape[0]):
                osc_ref[:, LANES * s:LANES * (s + 1)] = ext_s[s, last, :]

        tail_l = ext_l[:, rows:rows + hist, :]
        tail_s = ext_s[:, rows:rows + hist, :]
        ext_l[:, 0:hist, :] = tail_l
        ext_s[:, 0:hist, :] = tail_s


def _in_proj(x2d, bsz, xs, g_mix, w_main, w_dt, lcw, lcb, scw, scb, dtb):
    n = x2d.shape[0]
    steps_p = n // ROW_TILE
    steps_per_seq = steps_p // bsz
    assert xs.shape[0] * xs.shape[1] == ROW_TILE
    const = lambda i: (0, 0)
    tile = lambda i: (jnp.minimum(i, steps_p - 1), 0)
    seq_of = lambda i: (jnp.minimum(i, steps_p - 1) // steps_per_seq, 0, 0)
    once = dict(pipeline_mode=pl.Buffered(1))
    return pl.pallas_call(
        functools.partial(_inproj_kernel, steps_per_seq=steps_per_seq),
        out_shape=(
            jax.ShapeDtypeStruct((n, PROJ_PAD), F32),
            jax.ShapeDtypeStruct((bsz, CONV_WIDTH - 1, LRU_WIDTH), F32),
            jax.ShapeDtypeStruct((bsz, CONV_WIDTH - 1, SSD_CONV_DIM), F32),
            jax.ShapeDtypeStruct((ROW_TILE, PROJ_PAD), F32),
        ),
        grid=(steps_p + 1,),
        in_specs=[
            pl.BlockSpec((ROW_TILE, D_MODEL), tile),
            pl.BlockSpec(xs.shape, lambda i: (0, 0, 0), **once),
            pl.BlockSpec((1, D_MODEL), const),
            pl.BlockSpec((D_MODEL, PROJ_MAIN), const, **once),
            pl.BlockSpec((D_MODEL, DT_PAD), const, **once),
            pl.BlockSpec((CONV_WIDTH, LRU_WIDTH), const),
            pl.BlockSpec((1, LRU_WIDTH), const),
            pl.BlockSpec((CONV_WIDTH, SSD_CONV_DIM), const),
            pl.BlockSpec((1, SSD_CONV_DIM), const),
            pl.BlockSpec((1, DT_PAD), const),
        ],
        out_specs=(
            pl.BlockSpec((ROW_TILE, PROJ_PAD), tile),
            pl.BlockSpec((None, CONV_WIDTH - 1, LRU_WIDTH), seq_of),
            pl.BlockSpec((None, CONV_WIDTH - 1, SSD_CONV_DIM), seq_of),
            pl.BlockSpec((ROW_TILE, PROJ_PAD), const, **once),
        ),
        scratch_shapes=[
            pltpu.VMEM((LRU_WIDTH // LANES, SUBLANES + ROW_TILE, LANES), F32),
            pltpu.VMEM((SSD_CONV_DIM // LANES, SUBLANES + ROW_TILE, LANES), F32),
        ],
        compiler_params=pltpu.CompilerParams(
            dimension_semantics=("arbitrary",), vmem_limit_bytes=VMEM_LIMIT),
        name="in_proj",
    )(x2d, xs, g_mix, w_main, w_dt, lcw, lcb, scw, scb, dtb)


def _outmlp_kernel(xp_ref, yp_ref, xs_ref, ys_ref, wo_ref, gm_ref, wu_ref, wd_ref, gf_ref, op_ref, os_ref):
    last = pl.program_id(0) == pl.num_programs(0) - 1

    def run(x_ref, y_ref, o_ref):
        x = x_ref[...].reshape(-1, D_MODEL)
        x1 = x + jnp.dot(y_ref[...].astype(BF16), wo_ref[...], preferred_element_type=F32)
        m = _rms(x1, gm_ref[...]).astype(BF16)
        u = jnp.dot(m, wu_ref[...], preferred_element_type=F32)
        u = jnp.square(jnp.maximum(u, 0.0)).astype(BF16)
        x2 = x1 + jnp.dot(u, wd_ref[...], preferred_element_type=F32)
        o_ref[...] = _rms(x2, gf_ref[...]).reshape(o_ref.shape)

    @pl.when(jnp.logical_not(last))
    def _prompt():
        run(xp_ref, yp_ref, op_ref)

    @pl.when(last)
    def _sample():
        run(xs_ref, ys_ref, os_ref)


def _out_mlp(xp, ymix_p, xs, ymix_s, w_out_b, g_mlp, w_up_b, w_down_b, g_final):
    n_p = xp.shape[0]
    steps_p = n_p // ROW_TILE
    assert ymix_s.shape[0] == ROW_TILE
    const = lambda i: (0, 0)
    tile = lambda i: (jnp.minimum(i, steps_p - 1), 0)
    once = dict(pipeline_mode=pl.Buffered(1))
    return pl.pallas_call(
        _outmlp_kernel,
        out_shape=(jax.ShapeDtypeStruct((n_p, D_MODEL), F32), jax.ShapeDtypeStruct(xs.shape, F32)),
        grid=(steps_p + 1,),
        in_specs=[
            pl.BlockSpec((ROW_TILE, D_MODEL), tile),
            pl.BlockSpec((ROW_TILE, MIX_WIDTH), tile),
            pl.BlockSpec(xs.shape, lambda i: (0, 0, 0), **once),
            pl.BlockSpec((ROW_TILE, MIX_WIDTH), const, **once),
            pl.BlockSpec((MIX_WIDTH, D_MODEL), const, **once),
            pl.BlockSpec((1, D_MODEL), const),
            pl.BlockSpec((D_MODEL, D_FF), const, **once),
            pl.BlockSpec((D_FF, D_MODEL), const, **once),
            pl.BlockSpec((1, D_MODEL), const),
        ],
        out_specs=(pl.BlockSpec((ROW_TILE, D_MODEL), tile), pl.BlockSpec(xs.shape, lambda i: (0, 0, 0))),
        compiler_params=pltpu.CompilerParams(
            dimension_semantics=("arbitrary",), vmem_limit_bytes=VMEM_LIMIT_OUT),
        name="out_mlp",
    )(xp, ymix_p, xs, ymix_s, w_out_b, g_mlp, w_up_b, w_down_b, g_final)


def _mixer_prompt_kernel(u_ref, gl_ref, zact_ref, xbc_ref, dt_ref,
                         wg_ref, ba_ref, bx_ref, lam_ref, glru_ref, alog_ref, dskip_ref, gssd_ref,
                         selt_ref, selp_ref, wo_ref, wu_ref, wd_ref,
                         y_ref, olh_ref, osh_ref, wo_b_ref, wu_b_ref, wd_b_ref,
                         a_pad, b_pad, h_pad, hcar, ht):
    t = pl.program_id(1)
    nt = pl.num_programs(1)
    tc = PROMPT_TC
    wo_b_ref[...] = wo_ref[...].astype(BF16)
    wu_b_ref[...] = wu_ref[...].astype(BF16)
    wd_b_ref[...] = wd_ref[...].astype(BF16)

    @pl.when(t == 0)
    def _init():
        hcar[...] = jnp.zeros_like(hcar)
        ht[...] = jnp.zeros_like(ht)

    neg_c_sp = (-LRU_C) * _softplus(-lam_ref[...])
    lane1 = lax.broadcasted_iota(jnp.int32, (1, LANES), 1)
    a2_row = jnp.where(lane1 < N_SSD_HEADS, -LOG2E * jnp.exp(alog_ref[...]), 0.0)
    L = SSD_CHUNK
    rr = lax.broadcasted_iota(jnp.int32, (L, L), 0)
    cc = lax.broadcasted_iota(jnp.int32, (L, L), 1)
    causal = cc <= rr
    tri = jnp.where(causal, 1.0, 0.0).astype(F32)
    mask_add = jnp.where(causal, 0.0, NEG_BIG).astype(F32)
    half = SSD_WIDTH // N_SSD_GROUPS

    chunks = [(n, c) for n in range(PROMPT_NB) for c in range(tc // L)]
    cums = [_ssd_cumdecay(dt_ref[n, L * c:L * (c + 1), :], a2_row, tri) for n, c in chunks]
    dts = [dt_ref[n, L * c:L * (c + 1), :] for n, c in chunks]
    cum_all = jnp.concatenate([cum2 for cum2, _ in cums], axis=0)
    cols_all = _spread(cum_all, selt_ref)
    ecol_all = _spread(jnp.exp2(cum_all), selp_ref)
    sdt_all = _spread(jnp.concatenate(
        [jnp.exp2(cum2[L - 1:L, :] - cum2) * dt for (cum2, _), dt in zip(cums, dts)], axis=0), selp_ref)

    for n in range(PROMPT_NB):
        a, b = _lru_coeffs(u_ref[n], wg_ref, ba_ref[...], bx_ref[...], neg_c_sp)
        hseq = _lru_scan_strided(a, b, hcar.at[n], a_pad.at[n], b_pad.at[n], h_pad.at[n])
        y_ref[n, :, 0:LRU_WIDTH] = _rms(hseq * gl_ref[n], glru_ref[...])

        for c in range(tc // L):
            k = chunks.index((n, c))
            rows = slice(L * c, L * (c + 1))
            krows = slice(L * k, L * (k + 1))
            xs = xbc_ref[n, rows, 0:SSD_WIDTH]
            bm = xbc_ref[n, rows, SSD_WIDTH:SSD_WIDTH + N_SSD_GROUPS * D_STATE]
            cm = xbc_ref[n, rows, SSD_WIDTH + N_SSD_GROUPS * D_STATE:SSD_CONV_DIM]
            y_diag = _ssd_diag(xs, bm, cm, dts[k], cols_all[krows, :], cums[k][1], mask_add)
            ecol = ecol_all[krows, :]
            xw = xs * sdt_all[krows, :]
            dec = ecol[L - 1:L, :]
            y_off_parts = []
            for g in range(N_SSD_GROUPS):
                htg = ht[n, g]
                cg = cm[:, D_STATE * g:D_STATE * (g + 1)].astype(BF16)
                y_off_parts.append(jnp.dot(cg, htg.astype(BF16), preferred_element_type=F32))
                bg_t = bm[:, D_STATE * g:D_STATE * (g + 1)].T.astype(BF16)
                st = jnp.dot(bg_t, xw[:, half * g:half * (g + 1)].astype(BF16), preferred_element_type=F32)
                ht[n, g] = htg * dec[:, half * g:half * (g + 1)] + st
            ys = y_diag + jnp.concatenate(y_off_parts, axis=1) * ecol
            y_ref[n, rows, LRU_WIDTH:MIX_WIDTH] = _ssd_gate_norm(
                ys, xs, zact_ref[n, rows, :], dskip_ref[...], gssd_ref[...])

    @pl.when(t == nt - 1)
    def _final():
        for n in range(PROMPT_NB):
            olh_ref[n] = hcar[n, 0:1, :]
            for g in range(N_SSD_GROUPS):
                osh_ref[n, half * g:half * (g + 1), :] = ht[n, g].T


def _param_specs(const):
    return [
        pl.BlockSpec((CONV_WIDTH, LRU_WIDTH), const),
        pl.BlockSpec((1, LRU_WIDTH), const),
        pl.BlockSpec((LRU_WIDTH // MXU_DIM, MXU_DIM, 2 * MXU_DIM), lambda *_: (0, 0, 0)),
        pl.BlockSpec((1, LRU_WIDTH), const),
        pl.BlockSpec((1, LRU_WIDTH), const),
        pl.BlockSpec((1, LRU_WIDTH), const),
        pl.BlockSpec((1, LRU_WIDTH), const),
        pl.BlockSpec((CONV_WIDTH, SSD_CONV_DIM), const),
        pl.BlockSpec((1, SSD_CONV_DIM), const),
        pl.BlockSpec((1, DT_PAD), const),
        pl.BlockSpec((1, DT_PAD), const),
        pl.BlockSpec((1, SSD_WIDTH), const),
        pl.BlockSpec((1, SSD_WIDTH), const),
    ]


def _head_selectors():
    k = np.arange(2 * LANES)[:, None] % LANES
    sel_t = (k == np.arange(N_SSD_HEADS * LANES)[None, :] // LANES).astype(np.float32)
    sel_p = (k == np.arange(SSD_WIDTH)[None, :] // SSD_HEAD_DIM).astype(np.float32)
    return jnp.asarray(sel_t, BF16), jnp.asarray(sel_p, BF16)


def _mixer_prompt(act, wg, b_a, b_x, lam, g_lru, a_log, d_skip, g_ssd, sel_t, sel_p, w_out, w_up, w_down):
    bsz, seq, _ = act.shape
    tc = PROMPT_TC
    nb = PROMPT_NB
    steps = (bsz // nb) * (seq // tc)
    const = lambda b, t: (0, 0)
    w_slice = lambda b, t: (b * (seq // tc) + t, 0)
    assert all(w.shape[0] % (steps * 2 * SUBLANES) == 0 for w in (w_out, w_up, w_down))
    w_specs = [pl.BlockSpec((w.shape[0] // steps, w.shape[1]), w_slice) for w in (w_out, w_up, w_down)]
    in_specs = [
        pl.BlockSpec((nb, tc, LRU_WIDTH), lambda b, t: (b, t, 0)),
        pl.BlockSpec((nb, tc, LRU_WIDTH), lambda b, t: (b, t, 1)),
        pl.BlockSpec((nb, tc, SSD_WIDTH), lambda b, t: (b, t, 2)),
        pl.BlockSpec((nb, tc, SSD_CONV_DIM), lambda b, t: (b, t, 2)),
        pl.BlockSpec((nb, tc, DT_PAD), lambda b, t: (b, t, PROJ_MAIN // DT_PAD)),
        pl.BlockSpec((LRU_WIDTH // MXU_DIM, MXU_DIM, 2 * MXU_DIM), lambda b, t: (0, 0, 0)),
        pl.BlockSpec((1, LRU_WIDTH), const),
        pl.BlockSpec((1, LRU_WIDTH), const),
        pl.BlockSpec((1, LRU_WIDTH), const),
        pl.BlockSpec((1, LRU_WIDTH), const),
        pl.BlockSpec((1, DT_PAD), const),
        pl.BlockSpec((1, SSD_WIDTH), const),
        pl.BlockSpec((1, SSD_WIDTH), const),
        pl.BlockSpec((2 * LANES, N_SSD_HEADS * LANES), const),
        pl.BlockSpec((2 * LANES, SSD_WIDTH), const),
    ] + w_specs
    out_shape = (
        jax.ShapeDtypeStruct((bsz, seq, MIX_WIDTH), F32),
        jax.ShapeDtypeStruct((bsz, 1, LRU_WIDTH), F32),
        jax.ShapeDtypeStruct((bsz, SSD_WIDTH, D_STATE), F32),
    ) + tuple(jax.ShapeDtypeStruct(w.shape, BF16) for w in (w_out, w_up, w_down))
    out_specs = (
        pl.BlockSpec((nb, tc, MIX_WIDTH), lambda b, t: (b, t, 0)),
        pl.BlockSpec((nb, 1, LRU_WIDTH), lambda b, t: (b, 0, 0)),
        pl.BlockSpec((nb, SSD_WIDTH, D_STATE), lambda b, t: (b, 0, 0)),
    ) + tuple(w_specs)
    scratch = [
        pltpu.VMEM((nb, LRU_WIDTH // LANES, SUBLANES * SCAN_PITCH, LANES), F32),
        pltpu.VMEM((nb, LRU_WIDTH // LANES, SUBLANES * SCAN_PITCH, LANES), F32),
        pltpu.VMEM((nb, LRU_WIDTH // LANES, SUBLANES * SCAN_PITCH, LANES), F32),
        pltpu.VMEM((nb, SUBLANES, LRU_WIDTH), F32),
        pltpu.VMEM((nb, N_SSD_GROUPS, D_STATE, SSD_WIDTH // N_SSD_GROUPS), F32),
    ]
    return pl.pallas_call(
        _mixer_prompt_kernel,
        out_shape=out_shape,
        grid=(bsz // nb, seq // tc),
        in_specs=in_specs,
        out_specs=out_specs,
        scratch_shapes=scratch,
        compiler_params=pltpu.CompilerParams(
            dimension_semantics=("parallel", "arbitrary"), vmem_limit_bytes=VMEM_LIMIT),
        name="mixer_prompt",
    )(act, act, act, act, act, wg, b_a, b_x, lam, g_lru, a_log, d_skip, g_ssd, sel_t, sel_p,
      w_out, w_up, w_down)


def _mixer_sample_kernel(lx_ref, gate_ref, z_ref, xbc_ref, dt_ref,
                         slc_ref, slh_ref, ssc_ref, ssh_ref,
                         lcw_ref, lcb_ref, wg_ref, ba_ref, bx_ref, lam_ref, glru_ref,
                         scw_ref, scb_ref, dtb_ref, alog_ref, dskip_ref, gssd_ref, selt_ref, selp_ref,
                         y_ref, olc_ref, olh_ref, osc_ref, osh_ref,
                         ext_l, ext_s, pad_scr, yoff_scr, *, T):
    S = SAMPLE_SEQS
    P = SUBLANES
    K1 = CONV_WIDTH - 1
    R = S * P
    row_i = lax.broadcasted_iota(jnp.int32, (R, 1), 0) & (P - 1)
    valid = row_i < T

    def pad_rows(ref):
        width = ref.shape[-1]
        pad_scr[:, :, 0:width] = jnp.zeros((S, P, width), F32)
        pad_scr[:, 0:T, 0:width] = ref[...].reshape(S, T, width)
        return pad_scr[:, :, 0:width].reshape(R, width)

    ext_l[...] = jnp.zeros_like(ext_l)
    ext_s[...] = jnp.zeros_like(ext_s)
    for k in range(K1):
        ext_l[:, k, :] = slc_ref[k]
    ext_l[:, K1:K1 + T, :] = lx_ref[...].reshape(S, T, LRU_WIDTH)
    for k in range(K1):
        ext_s[:, k, :] = ssc_ref[k]
    ext_s[:, K1:K1 + T, :] = xbc_ref[...].reshape(S, T, SSD_CONV_DIM)
    for k in range(K1):
        olc_ref[k] = ext_l[:, T + k, :]
        osc_ref[k] = ext_s[:, T + k, :]

    el = ext_l[...].reshape(R, LRU_WIDTH)
    es = ext_s[...].reshape(R, SSD_CONV_DIM)

    def conv(e, w_ref, b_ref):
        out = b_ref[...] + e * w_ref[0:1, :]
        for k in range(1, CONV_WIDTH):
            out = out + pltpu.roll(e, R - k, axis=0) * w_ref[k:k + 1, :]
        return out

    u = conv(el, lcw_ref, lcb_ref)
    neg_c_sp = (-LRU_C) * _softplus(-lam_ref[...])
    a, b = _lru_coeffs(u, wg_ref, ba_ref[...], bx_ref[...], neg_c_sp)
    a, b = _scan_within_8(a, b)
    h0 = jnp.broadcast_to(slh_ref[...][:, None, :], (S, P, LRU_WIDTH)).reshape(R, LRU_WIDTH)
    hseq = a * h0 + b
    olh_ref[...] = hseq.reshape(S, P, LRU_WIDTH)[:, T - 1, :]
    gate = pad_rows(gate_ref)
    y_lru = _rms(hseq * _gelu_tanh(gate), glru_ref[...])

    xbc = conv(es, scw_ref, scb_ref)
    xbc = xbc * _sigmoid(xbc)
    xs = xbc[:, 0:SSD_WIDTH]
    bm = xbc[:, SSD_WIDTH:SSD_WIDTH + N_SSD_GROUPS * D_STATE]
    cm = xbc[:, SSD_WIDTH + N_SSD_GROUPS * D_STATE:]
    dt_raw = pad_rows(dt_ref)
    dt = jnp.where(valid, _softplus(dt_raw + dtb_ref[...]), 0.0)
    lane1 = lax.broadcasted_iota(jnp.int32, (1, LANES), 1)
    a2_row = jnp.where(lane1 < N_SSD_HEADS, -LOG2E * jnp.exp(alog_ref[...]), 0.0)

    rr = lax.broadcasted_iota(jnp.int32, (R, R), 0)
    cc = lax.broadcasted_iota(jnp.int32, (R, R), 1)
    allowed = (cc <= rr) & ((rr - cc) <= (rr & (P - 1)))
    tri = jnp.where(allowed, 1.0, 0.0).astype(F32)
    mask_add = jnp.where(allowed, 0.0, NEG_BIG).astype(F32)

    cum2, cum2_t = _ssd_cumdecay(dt, a2_row, tri)
    y_diag = _ssd_diag(xs, bm, cm, dt, _spread(cum2, selt_ref), cum2_t, mask_add)
    ecol = _spread(jnp.exp2(cum2), selp_ref)
    end2 = jnp.broadcast_to(cum2.reshape(S, P, LANES)[:, P - 1:P, :], (S, P, LANES)).reshape(R, LANES)
    xw = xs * _spread(jnp.exp2(end2 - cum2) * dt, selp_ref)
    ecum_t = jnp.exp2(cum2_t)

    half = SSD_WIDTH // N_SSD_GROUPS
    for q in range(S):
        r0 = P * q
        vq = jnp.broadcast_to(ecum_t[:, r0 + P - 1:r0 + P], (N_SSD_HEADS, LANES))
        for g in range(N_SSD_GROUPS):
            hqg = ssh_ref[q, half * g:half * (g + 1), :]
            cq = cm[r0:r0 + P, D_STATE * g:D_STATE * (g + 1)].astype(BF16)
            yoff_scr[r0:r0 + P, half * g:half * (g + 1)] = lax.dot_general(
                cq, hqg.astype(BF16), (((1,), (1,)), ((), ())), preferred_element_type=F32)
            bq = bm[r0:r0 + P, D_STATE * g:D_STATE * (g + 1)].astype(BF16)
            xq = xw[r0:r0 + P, half * g:half * (g + 1)].astype(BF16)
            st = lax.dot_general(xq, bq, (((0,), (0,)), ((), ())), preferred_element_type=F32)
            for e in range(N_SSD_HEADS // N_SSD_GROUPS):
                h = (N_SSD_HEADS // N_SSD_GROUPS) * g + e
                lo_r = SSD_HEAD_DIM * e
                osh_ref[q, SSD_HEAD_DIM * h:SSD_HEAD_DIM * (h + 1), :] = (
                    vq[h:h + 1, :] * hqg[lo_r:lo_r + SSD_HEAD_DIM, :] + st[lo_r:lo_r + SSD_HEAD_DIM, :])

    ys = y_diag + yoff_scr[...] * ecol
    z = pad_rows(z_ref)
    y_ssd = _ssd_gate_norm(ys, xs, z * _sigmoid(z), dskip_ref[...], gssd_ref[...])
    y_ref[:, 0:LRU_WIDTH] = y_lru.reshape(S, P, LRU_WIDTH)[:, 0:T, :].reshape(S * T, LRU_WIDTH)
    y_ref[:, LRU_WIDTH:MIX_WIDTH] = y_ssd.reshape(S, P, SSD_WIDTH)[:, 0:T, :].reshape(S * T, SSD_WIDTH)


def _mixer_sample(proj, T, st_lc, st_lh, st_sc, st_sh, params, sel_t, sel_p):
    nseq = proj.shape[0] // T
    S = SAMPLE_SEQS
    const = lambda i: (0, 0)
    in_specs = [
        pl.BlockSpec((S * T, LRU_WIDTH), lambda i: (i, 0)),
        pl.BlockSpec((S * T, LRU_WIDTH), lambda i: (i, 1)),
        pl.BlockSpec((S * T, SSD_WIDTH), lambda i: (i, 2)),
        pl.BlockSpec((S * T, SSD_CONV_DIM), lambda i: (i, 2)),
        pl.BlockSpec((S * T, DT_PAD), lambda i: (i, PROJ_MAIN // DT_PAD)),
        pl.BlockSpec((CONV_WIDTH - 1, S, LRU_WIDTH), lambda i: (0, i, 0)),
        pl.BlockSpec((S, LRU_WIDTH), lambda i: (i, 0)),
        pl.BlockSpec((CONV_WIDTH - 1, S, SSD_CONV_DIM), lambda i: (0, i, 0)),
        pl.BlockSpec((S, SSD_WIDTH, D_STATE), lambda i: (i, 0, 0)),
    ] + _param_specs(const) + [
        pl.BlockSpec((2 * LANES, N_SSD_HEADS * LANES), const),
        pl.BlockSpec((2 * LANES, SSD_WIDTH), const),
    ]
    out_shape = (
        jax.ShapeDtypeStruct((nseq * T, MIX_WIDTH), F32),
        jax.ShapeDtypeStruct((CONV_WIDTH - 1, nseq, LRU_WIDTH), F32),
        jax.ShapeDtypeStruct((nseq, LRU_WIDTH), F32),
        jax.ShapeDtypeStruct((CONV_WIDTH - 1, nseq, SSD_CONV_DIM), F32),
        jax.ShapeDtypeStruct((nseq, SSD_WIDTH, D_STATE), F32),
    )
    out_specs = (
        pl.BlockSpec((S * T, MIX_WIDTH), lambda i: (i, 0)),
        pl.BlockSpec((CONV_WIDTH - 1, S, LRU_WIDTH), lambda i: (0, i, 0)),
        pl.BlockSpec((S, LRU_WIDTH), lambda i: (i, 0)),
        pl.BlockSpec((CONV_WIDTH - 1, S, SSD_CONV_DIM), lambda i: (0, i, 0)),
        pl.BlockSpec((S, SSD_WIDTH, D_STATE), lambda i: (i, 0, 0)),
    )
    scratch = [
        pltpu.VMEM((S, SUBLANES, LRU_WIDTH), F32),
        pltpu.VMEM((S, SUBLANES, SSD_CONV_DIM), F32),
        pltpu.VMEM((S, SUBLANES, LRU_WIDTH), F32),
        pltpu.VMEM((S * SUBLANES, SSD_WIDTH), F32),
    ]
    return pl.pallas_call(
        functools.partial(_mixer_sample_kernel, T=T),
        out_shape=out_shape,
        grid=(nseq // S,),
        in_specs=in_specs,
        out_specs=out_specs,
        scratch_shapes=scratch,
        compiler_params=pltpu.CompilerParams(
            dimension_semantics=("parallel",), vmem_limit_bytes=VMEM_LIMIT),
        name="mixer_sample",
    )(proj, proj, proj, proj, proj, st_lc, st_lh, st_sc, st_sh, *params, sel_t, sel_p)


def _gate_weights(w_a, w_x):
    def tiles(w):
        per = MXU_DIM // LRU_BLOCK
        w4 = w.reshape(N_LRU_HEADS // per, per, LRU_BLOCK, LRU_BLOCK)
        eye = jnp.eye(per, dtype=w.dtype)
        t = jnp.einsum('jaik,ab->jaibk', w4, eye)
        return t.reshape(N_LRU_HEADS // per, MXU_DIM, MXU_DIM)
    return jnp.concatenate([tiles(w_a), tiles(w_x)], axis=2).astype(BF16)


def kernel(x_prompt, x_sample, state_lru_conv, state_lru_h, state_ssd_conv, state_ssd_h, g_mix, w_in,
           lru_conv_w, lru_conv_b, w_a, b_a, w_x, b_x, lam, g_lru_out, ssd_conv_w, ssd_conv_b, dt_bias,
           a_log, d_skip, g_ssd_out, w_out, g_mlp, w_up, w_down, g_final):
    depth = w_in.shape[0]
    assert depth == 1
    bp, seq, _ = x_prompt.shape
    bs, dseq, _ = x_sample.shape
    l = 0
    row = lambda v: v.reshape(1, -1)
    w_main, w_dt = _split_w_in(jnp.swapaxes(w_in, 1, 2)[l])
    params = (
        lru_conv_w[l], row(lru_conv_b[l]), _gate_weights(w_a[l], w_x[l]),
        row(b_a[l]), row(b_x[l]), row(lam[l]), row(g_lru_out[l]),
        ssd_conv_w[l], row(ssd_conv_b[l]),
        jnp.pad(row(dt_bias[l]), ((0, 0), (0, DT_PAD - N_SSD_HEADS))),
        jnp.pad(row(a_log[l]), ((0, 0), (0, DT_PAD - N_SSD_HEADS))),
        row(jnp.repeat(d_skip[l], SSD_HEAD_DIM)), row(g_ssd_out[l]),
    )
    gmix = row(g_mix[l])
    gmlp = row(g_mlp[l])
    gfin = row(g_final)

    xp2 = x_prompt.reshape(bp * seq, D_MODEL)
    (lcw, lcb, wg, ba, bx, lam_r, glru, scw, scb, dtb, alog, dskip, gssd) = params
    act_p, p_lc, p_sc, proj_s = _in_proj(xp2, bp, x_sample, gmix, w_main, w_dt, lcw, lcb, scw, scb, dtb)
    sel_t, sel_p = _head_selectors()
    ymix_p, p_lh, p_sh, w_out_b, w_up_b, w_down_b = _mixer_prompt(
        act_p.reshape(bp, seq, PROJ_PAD), wg, ba, bx, lam_r, glru, alog, dskip, gssd, sel_t, sel_p,
        w_out[l], w_up[l], w_down[l])

    ymix_s, s_lc, s_lh, s_sc, s_sh = _mixer_sample(
        proj_s, dseq, jnp.swapaxes(state_lru_conv[l], 0, 1), state_lru_h[l],
        jnp.swapaxes(state_ssd_conv[l], 0, 1),
        state_ssd_h[l].reshape(bs, SSD_WIDTH, D_STATE), params, sel_t, sel_p)
    y_prompt, y_sample = _out_mlp(xp2, ymix_p.reshape(bp * seq, MIX_WIDTH), x_sample, ymix_s,
                                  w_out_b, gmlp, w_up_b, w_down_b, gfin)

    hshape = (N_SSD_HEADS, SSD_HEAD_DIM, D_STATE)
    return (
        y_prompt.reshape(bp, seq, D_MODEL), y_sample,
        p_lc[None], p_lh.reshape(1, bp, LRU_WIDTH), p_sc[None], p_sh.reshape(1, bp, *hshape),
        jnp.swapaxes(s_lc, 0, 1)[None], s_lh[None], jnp.swapaxes(s_sc, 0, 1)[None],
        s_sh.reshape(1, bs, *hshape),
    )
```

```python
import functools
import math

import jax
import jax.numpy as jnp
import numpy as np
from jax import lax
from jax.experimental import pallas as pl
from jax.experimental.pallas import tpu as pltpu

F32 = jnp.float32
BF16 = jnp.bfloat16

D_MODEL = 1024
LRU_WIDTH = 1024
N_LRU_HEADS = 16
LRU_BLOCK = 64
LRU_C = 8.0
SSD_WIDTH = 1024
SSD_HEAD_DIM = 64
N_SSD_HEADS = 16
N_SSD_GROUPS = 2
D_STATE = 128
CONV_WIDTH = 4
SSD_CONV_DIM = SSD_WIDTH + 2 * N_SSD_GROUPS * D_STATE
D_FF = 4 * D_MODEL
EPS = 1e-6

LANES = 128
SUBLANES = 8
MXU_DIM = 256
DT_PAD = LANES
PROJ_MAIN = 2 * LRU_WIDTH + SSD_WIDTH + SSD_CONV_DIM
PROJ_PAD = PROJ_MAIN + DT_PAD
MIX_WIDTH = LRU_WIDTH + SSD_WIDTH
SSD_CHUNK = 128
PROMPT_TC = 256
PROMPT_NB = 2
ROW_TILE = 512
SPLIT_COLS = 1536
SAMPLE_SEQS = SSD_CHUNK // SUBLANES
SCAN_RUN = PROMPT_TC // SUBLANES
SCAN_PITCH = SCAN_RUN + 4
NEG_BIG = -1e30
LOG2E = 1.4426950408889634
VMEM_LIMIT = 56 * 1024 * 1024
VMEM_LIMIT_OUT = 60 * 1024 * 1024
HI = lax.Precision.HIGHEST


def _rms(x, g):
    ms = jnp.mean(x * x, axis=-1, keepdims=True)
    return x * lax.rsqrt(ms + EPS) * g


def _sigmoid(x):
    return 0.5 * jnp.tanh(0.5 * x) + 0.5


def _silu(x):
    h = 0.5 * x
    return h * jnp.tanh(h) + h


def _softplus(x):
    return jnp.maximum(x, 0.0) + jnp.log1p(jnp.exp(-jnp.abs(x)))


def _gelu_tanh(x):
    c = math.sqrt(2.0 / math.pi)
    return 0.5 * x * (1.0 + jnp.tanh(c * (x + 0.044715 * (x * x * x))))


def _lru_coeffs(u, wg_ref, b_a, b_x, neg_c_sp):
    ub = u.astype(BF16)
    r_parts, i_parts = [], []
    for j in range(LRU_WIDTH // MXU_DIM):
        g = jnp.dot(ub[:, MXU_DIM * j:MXU_DIM * (j + 1)], wg_ref[j], preferred_element_type=F32)
        r_parts.append(g[:, :MXU_DIM])
        i_parts.append(g[:, MXU_DIM:])
    r = _sigmoid(jnp.concatenate(r_parts, axis=1) + b_a)
    i = _sigmoid(jnp.concatenate(i_parts, axis=1) + b_x)
    log_a = r * neg_c_sp
    a = jnp.exp(log_a)
    th = jnp.tanh(log_a)
    v = (th + th) / (th - 1.0)
    mult = jnp.where(v > 0.0, v * lax.rsqrt(v), 0.0)
    return a, mult * (i * u)


def _scan_within_8(a, b):
    ridx = lax.broadcasted_iota(jnp.int32, a.shape, 0) & (SUBLANES - 1)
    for k in (1, 2, 4):
        a_s = pltpu.roll(a, k, axis=0)
        b_s = pltpu.roll(b, k, axis=0)
        m = ridx >= k
        b = jnp.where(m, a * b_s + b, b)
        a = jnp.where(m, a * a_s, a)
    return a, b


def _conv_slabs(ext, w_ref, b_ref, rows, first):
    parts = []
    for s in range(ext.shape[0]):
        cols = slice(LANES * s, LANES * (s + 1))
        acc = b_ref[:, cols] + ext[s, pl.ds(first, rows), :] * w_ref[0:1, cols]
        for k in range(1, CONV_WIDTH):
            acc = acc + ext[s, pl.ds(first + k, rows), :] * w_ref[k:k + 1, cols]
        parts.append(acc)
    return jnp.concatenate(parts, axis=1)


def _lru_scan_strided(a, b, hcar, a_pad, b_pad, h_pad):
    rows = a.shape[0]
    S = rows // SUBLANES
    nslab = LRU_WIDTH // LANES
    ridx = lax.broadcasted_iota(jnp.int32, (SUBLANES, LANES), 0)
    step = lambda ref, s, i: ref[s, pl.ds(i, SUBLANES, stride=SCAN_PITCH), :]
    for s in range(nslab):
        cols = slice(LANES * s, LANES * (s + 1))
        for j in range(SUBLANES):
            a_pad[s, SCAN_PITCH * j:SCAN_PITCH * j + S, :] = a[S * j:S * (j + 1), cols]
            b_pad[s, SCAN_PITCH * j:SCAN_PITCH * j + S, :] = b[S * j:S * (j + 1), cols]
    h = [jnp.zeros((SUBLANES, LANES), F32)] * nslab
    prod = [jnp.ones((SUBLANES, LANES), F32)] * nslab
    for i in range(S):
        for s in range(nslab):
            av = step(a_pad, s, i)
            h[s] = av * h[s] + step(b_pad, s, i)
            prod[s] = av * prod[s]
    for s in range(nslab):
        cols = slice(LANES * s, LANES * (s + 1))
        pcum, hcum = _scan_within_8(prod[s], h[s])
        cin = hcar[:, cols]
        ends = hcum + pcum * cin
        h[s] = jnp.where(ridx == 0, cin, pltpu.roll(ends, 1, axis=0))
        hcar[:, cols] = jnp.broadcast_to(ends[SUBLANES - 1:SUBLANES, :], (SUBLANES, LANES))
    for i in range(S):
        for s in range(nslab):
            h[s] = step(a_pad, s, i) * h[s] + step(b_pad, s, i)
            h_pad[s, pl.ds(i, SUBLANES, stride=SCAN_PITCH), :] = h[s]
    return jnp.concatenate(
        [jnp.concatenate([h_pad[s, SCAN_PITCH * j:SCAN_PITCH * j + S, :] for j in range(SUBLANES)], axis=0)
         for s in range(nslab)], axis=1)


def _ssd_cumdecay(dt, a2_row, tri):
    cum2 = jnp.dot(tri, dt * a2_row, precision=HI, preferred_element_type=F32)
    return cum2, cum2.T[0:N_SSD_HEADS, :]


def _ssd_diag(xs, bm, cm, dt, cols, cum2_t, mask_add):
    L = xs.shape[0]
    c2_t = cum2_t - jnp.log2(dt.T[0:N_SSD_HEADS, :])
    lane = lax.broadcasted_iota(jnp.int32, (L, LANES), 1)
    lo = lane < SSD_HEAD_DIM
    y_parts = []
    for g in range(N_SSD_GROUPS):
        bg = bm[:, D_STATE * g:D_STATE * (g + 1)].astype(BF16)
        cg = cm[:, D_STATE * g:D_STATE * (g + 1)].astype(BF16)
        cb = lax.dot_general(cg, bg, (((1,), (1,)), ((), ())), preferred_element_type=F32)
        for jj in range(N_SSD_HEADS // N_SSD_GROUPS // 2):
            j = (N_SSD_HEADS // N_SSD_GROUPS // 2) * g + jj
            h0, h1 = 2 * j, 2 * j + 1
            col0 = cols[:, LANES * h0:LANES * (h0 + 1)]
            col1 = cols[:, LANES * h1:LANES * (h1 + 1)]
            m0 = cb * jnp.exp2(col0 - c2_t[h0:h0 + 1, :] + mask_add)
            m1 = cb * jnp.exp2(col1 - c2_t[h1:h1 + 1, :] + mask_add)
            lhs = jnp.concatenate([m0, m1], axis=1).astype(BF16)
            xp = xs[:, LANES * j:LANES * (j + 1)]
            rhs = jnp.concatenate([jnp.where(lo, xp, 0.0), jnp.where(lo, 0.0, xp)], axis=0).astype(BF16)
            y_parts.append(jnp.dot(lhs, rhs, preferred_element_type=F32))
    return jnp.concatenate(y_parts, axis=1)


def _spread(v, sel_ref):
    p0 = v.astype(BF16)
    p1 = (v - p0.astype(F32)).astype(BF16)
    return jnp.dot(jnp.concatenate([p0, p1], axis=1), sel_ref[...], preferred_element_type=F32)


def _ssd_gate_norm(ys, xs, z_act, dskip, g_ssd):
    ys = ys + dskip * xs
    gated = ys * z_act
    half = SSD_WIDTH // N_SSD_GROUPS
    outs = []
    for g in range(N_SSD_GROUPS):
        outs.append(_rms(gated[:, half * g:half * (g + 1)], g_ssd[:, half * g:half * (g + 1)]))
    return jnp.concatenate(outs, axis=1)


def _split_w_in_kernel(wt_ref, wdt_t_ref, main_ref, dt_ref):
    main_ref[...] = wt_ref[...].T.astype(BF16)

    @pl.when(pl.program_id(0) == 0)
    def _dt():
        dt_ref[...] = jnp.zeros_like(dt_ref)
        dt_ref[:, 0:N_SSD_HEADS] = wdt_t_ref[...].T.astype(BF16)


def _split_w_in(w_t):
    cols, rows = w_t.shape
    blk = SPLIT_COLS
    return pl.pallas_call(
        _split_w_in_kernel,
        out_shape=(jax.ShapeDtypeStruct((rows, PROJ_MAIN), BF16), jax.ShapeDtypeStruct((rows, DT_PAD), BF16)),
        grid=(PROJ_MAIN // blk,),
        in_specs=[pl.BlockSpec((blk, rows), lambda j: (j, 0)),
                  pl.BlockSpec((N_SSD_HEADS, rows), lambda j: (PROJ_MAIN // N_SSD_HEADS, 0))],
        out_specs=(pl.BlockSpec((rows, blk), lambda j: (0, j)), pl.BlockSpec((rows, DT_PAD), lambda j: (0, 0))),
        compiler_params=pltpu.CompilerParams(dimension_semantics=("arbitrary",)),
        name="split_w_in",
    )(w_t, w_t)


def _inproj_kernel(x_ref, xs_ref, g_ref, w_ref, wdt_ref, lcw_ref, lcb_ref, scw_ref, scb_ref, dtb_ref,
                   o_ref, olc_ref, osc_ref, os_ref, ext_l, ext_s, *, steps_per_seq):
    last_step = pl.program_id(0) == pl.num_programs(0) - 1
    t = lax.rem(pl.program_id(0), steps_per_seq)
    rows = ROW_TILE
    hist = SUBLANES
    o1, o2, o3 = LRU_WIDTH, 2 * LRU_WIDTH, 2 * LRU_WIDTH + SSD_WIDTH

    @pl.when(last_step)
    def _sample():
        hn = _rms(xs_ref[...].reshape(-1, D_MODEL), g_ref[...]).astype(BF16)
        os_ref[:, 0:PROJ_MAIN] = jnp.dot(hn, w_ref[...], preferred_element_type=F32)
        os_ref[:, PROJ_MAIN:PROJ_PAD] = jnp.dot(hn, wdt_ref[...], preferred_element_type=F32)

    @pl.when(jnp.logical_not(last_step))
    def _prompt():
        @pl.when(t == 0)
        def _init():
            ext_l[:, 0:hist, :] = jnp.zeros((ext_l.shape[0], hist, LANES), F32)
            ext_s[:, 0:hist, :] = jnp.zeros((ext_s.shape[0], hist, LANES), F32)

        hn = _rms(x_ref[...], g_ref[...]).astype(BF16)
        lx = jnp.dot(hn, w_ref[:, 0:o1], preferred_element_type=F32)
        for s in range(ext_l.shape[0]):
            ext_l[s, hist:hist + rows, :] = lx[:, LANES * s:LANES * (s + 1)]
        xbc_in = jnp.dot(hn, w_ref[:, o3:PROJ_MAIN], preferred_element_type=F32)
        for s in range(ext_s.shape[0]):
            ext_s[s, hist:hist + rows, :] = xbc_in[:, LANES * s:LANES * (s + 1)]
        o_ref[:, o1:o2] = _gelu_tanh(jnp.dot(hn, w_ref[:, o1:o2], preferred_element_type=F32))
        z = jnp.dot(hn, w_ref[:, o2:o3], preferred_element_type=F32)
        o_ref[:, o2:o3] = _silu(z)
        o_ref[:, PROJ_MAIN:PROJ_PAD] = _softplus(
            jnp.dot(hn, wdt_ref[...], preferred_element_type=F32) + dtb_ref[...])
        o_ref[:, 0:o1] = _conv_slabs(ext_l, lcw_ref, lcb_ref, rows, hist - (CONV_WIDTH - 1))
        xbc = _conv_slabs(ext_s, scw_ref, scb_ref, rows, hist - (CONV_WIDTH - 1))
        o_ref[:, o3:PROJ_MAIN] = _silu(xbc)

        @pl.when(t == steps_per_seq - 1)
        def _final():
            last = slice(hist + rows - (CONV_WIDTH - 1), hist + rows)
            for s in range(ext_l.shape[0]):
                olc_ref[:, LANES * s:LANES * (s + 1)] = ext_l[s, last, :]
            for s in range(ext_s.shape[0]):
                osc_ref[:, LANES * s:LANES * (s + 1)] = ext_s[s, last, :]

        tail_l = ext_l[:, rows:rows + hist, :]
        tail_s = ext_s[:, rows:rows + hist, :]
        ext_l[:, 0:hist, :] = tail_l
        ext_s[:, 0:hist, :] = tail_s


def _in_proj(x2d, bsz, xs, g_mix, w_main, w_dt, lcw, lcb, scw, scb, dtb):
    n = x2d.shape[0]
    steps_p = n // ROW_TILE
    steps_per_seq = steps_p // bsz
    assert xs.shape[0] * xs.shape[1] == ROW_TILE
    const = lambda i: (0, 0)
    tile = lambda i: (jnp.minimum(i, steps_p - 1), 0)
    seq_of = lambda i: (jnp.minimum(i, steps_p - 1) // steps_per_seq, 0, 0)
    once = dict(pipeline_mode=pl.Buffered(1))
    return pl.pallas_call(
        functools.partial(_inproj_kernel, steps_per_seq=steps_per_seq),
        out_shape=(
            jax.ShapeDtypeStruct((n, PROJ_PAD), F32),
            jax.ShapeDtypeStruct((bsz, CONV_WIDTH - 1, LRU_WIDTH), F32),
            jax.ShapeDtypeStruct((bsz, CONV_WIDTH - 1, SSD_CONV_DIM), F32),
            jax.ShapeDtypeStruct((ROW_TILE, PROJ_PAD), F32),
        ),
        grid=(steps_p + 1,),
        in_specs=[
            pl.BlockSpec((ROW_TILE, D_MODEL), tile),
            pl.BlockSpec(xs.shape, lambda i: (0, 0, 0), **once),
            pl.BlockSpec((1, D_MODEL), const),
            pl.BlockSpec((D_MODEL, PROJ_MAIN), const, **once),
            pl.BlockSpec((D_MODEL, DT_PAD), const, **once),
            pl.BlockSpec((CONV_WIDTH, LRU_WIDTH), const),
            pl.BlockSpec((1, LRU_WIDTH), const),
            pl.BlockSpec((CONV_WIDTH, SSD_CONV_DIM), const),
            pl.BlockSpec((1, SSD_CONV_DIM), const),
            pl.BlockSpec((1, DT_PAD), const),
        ],
        out_specs=(
            pl.BlockSpec((ROW_TILE, PROJ_PAD), tile),
            pl.BlockSpec((None, CONV_WIDTH - 1, LRU_WIDTH), seq_of),
            pl.BlockSpec((None, CONV_WIDTH - 1, SSD_CONV_DIM), seq_of),
            pl.BlockSpec((ROW_TILE, PROJ_PAD), const, **once),
        ),
        scratch_shapes=[
            pltpu.VMEM((LRU_WIDTH // LANES, SUBLANES + ROW_TILE, LANES), F32),
            pltpu.VMEM((SSD_CONV_DIM // LANES, SUBLANES + ROW_TILE, LANES), F32),
        ],
        compiler_params=pltpu.CompilerParams(
            dimension_semantics=("arbitrary",), vmem_limit_bytes=VMEM_LIMIT),
        name="in_proj",
    )(x2d, xs, g_mix, w_main, w_dt, lcw, lcb, scw, scb, dtb)


def _outmlp_kernel(xp_ref, yp_ref, xs_ref, ys_ref, wo_ref, gm_ref, wu_ref, wd_ref, gf_ref, op_ref, os_ref):
    last = pl.program_id(0) == pl.num_programs(0) - 1

    def run(x_ref, y_ref, o_ref):
        x = x_ref[...].reshape(-1, D_MODEL)
        x1 = x + jnp.dot(y_ref[...].astype(BF16), wo_ref[...], preferred_element_type=F32)
        m = _rms(x1, gm_ref[...]).astype(BF16)
        u = jnp.dot(m, wu_ref[...], preferred_element_type=F32)
        u = jnp.square(jnp.maximum(u, 0.0)).astype(BF16)
        x2 = x1 + jnp.dot(u, wd_ref[...], preferred_element_type=F32)
        o_ref[...] = _rms(x2, gf_ref[...]).reshape(o_ref.shape)

    @pl.when(jnp.logical_not(last))
    def _prompt():
        run(xp_ref, yp_ref, op_ref)

    @pl.when(last)
    def _sample():
        run(xs_ref, ys_ref, os_ref)


def _out_mlp(xp, ymix_p, xs, ymix_s, w_out_b, g_mlp, w_up_b, w_down_b, g_final):
    n_p = xp.shape[0]
    steps_p = n_p // ROW_TILE
    assert ymix_s.shape[0] == ROW_TILE
    const = lambda i: (0, 0)
    tile = lambda i: (jnp.minimum(i, steps_p - 1), 0)
    once = dict(pipeline_mode=pl.Buffered(1))
    return pl.pallas_call(
        _outmlp_kernel,
        out_shape=(jax.ShapeDtypeStruct((n_p, D_MODEL), F32), jax.ShapeDtypeStruct(xs.shape, F32)),
        grid=(steps_p + 1,),
        in_specs=[
            pl.BlockSpec((ROW_TILE, D_MODEL), tile),
            pl.BlockSpec((ROW_TILE, MIX_WIDTH), tile),
            pl.BlockSpec(xs.shape, lambda i: (0, 0, 0), **once),
            pl.BlockSpec((ROW_TILE, MIX_WIDTH), const, **once),
            pl.BlockSpec((MIX_WIDTH, D_MODEL), const, **once),
            pl.BlockSpec((1, D_MODEL), const),
            pl.BlockSpec((D_MODEL, D_FF), const, **once),
            pl.BlockSpec((D_FF, D_MODEL), const, **once),
            pl.BlockSpec((1, D_MODEL), const),
        ],
        out_specs=(pl.BlockSpec((ROW_TILE, D_MODEL), tile), pl.BlockSpec(xs.shape, lambda i: (0, 0, 0))),
        compiler_params=pltpu.CompilerParams(
            dimension_semantics=("arbitrary",), vmem_limit_bytes=VMEM_LIMIT_OUT),
        name="out_mlp",
    )(xp, ymix_p, xs, ymix_s, w_out_b, g_mlp, w_up_b, w_down_b, g_final)


def _mixer_prompt_kernel(u_ref, gl_ref, zact_ref, xbc_ref, dt_ref,
                         wg_ref, ba_ref, bx_ref, lam_ref, glru_ref, alog_ref, dskip_ref, gssd_ref,
                         selt_ref, selp_ref, wo_ref, wu_ref, wd_ref,
                         y_ref, olh_ref, osh_ref, wo_b_ref, wu_b_ref, wd_b_ref,
                         a_pad, b_pad, h_pad, hcar, ht):
    t = pl.program_id(1)
    nt = pl.num_programs(1)
    tc = PROMPT_TC
    wo_b_ref[...] = wo_ref[...].astype(BF16)
    wu_b_ref[...] = wu_ref[...].astype(BF16)
    wd_b_ref[...] = wd_ref[...].astype(BF16)

    @pl.when(t == 0)
    def _init():
        hcar[...] = jnp.zeros_like(hcar)
        ht[...] = jnp.zeros_like(ht)

    neg_c_sp = (-LRU_C) * _softplus(-lam_ref[...])
    lane1 = lax.broadcasted_iota(jnp.int32, (1, LANES), 1)
    a2_row = jnp.where(lane1 < N_SSD_HEADS, -LOG2E * jnp.exp(alog_ref[...]), 0.0)
    L = SSD_CHUNK
    rr = lax.broadcasted_iota(jnp.int32, (L, L), 0)
    cc = lax.broadcasted_iota(jnp.int32, (L, L), 1)
    causal = cc <= rr
    tri = jnp.where(causal, 1.0, 0.0).astype(F32)
    mask_add = jnp.where(causal, 0.0, NEG_BIG).astype(F32)
    half = SSD_WIDTH // N_SSD_GROUPS

    chunks = [(n, c) for n in range(PROMPT_NB) for c in range(tc // L)]
    cums = [_ssd_cumdecay(dt_ref[n, L * c:L * (c + 1), :], a2_row, tri) for n, c in chunks]
    dts = [dt_ref[n, L * c:L * (c + 1), :] for n, c in chunks]
    cum_all = jnp.concatenate([cum2 for cum2, _ in cums], axis=0)
    cols_all = _spread(cum_all, selt_ref)
    ecol_all = _spread(jnp.exp2(cum_all), selp_ref)
    sdt_all = _spread(jnp.concatenate(
        [jnp.exp2(cum2[L - 1:L, :] - cum2) * dt for (cum2, _), dt in zip(cums, dts)], axis=0), selp_ref)

    for n in range(PROMPT_NB):
        a, b = _lru_coeffs(u_ref[n], wg_ref, ba_ref[...], bx_ref[...], neg_c_sp)
        hseq = _lru_scan_strided(a, b, hcar.at[n], a_pad.at[n], b_pad.at[n], h_pad.at[n])
        y_ref[n, :, 0:LRU_WIDTH] = _rms(hseq * gl_ref[n], glru_ref[...])

        for c in range(tc // L):
            k = chunks.index((n, c))
            rows = slice(L * c, L * (c + 1))
            krows = slice(L * k, L * (k + 1))
            xs = xbc_ref[n, rows, 0:SSD_WIDTH]
            bm = xbc_ref[n, rows, SSD_WIDTH:SSD_WIDTH + N_SSD_GROUPS * D_STATE]
            cm = xbc_ref[n, rows, SSD_WIDTH + N_SSD_GROUPS * D_STATE:SSD_CONV_DIM]
            y_diag = _ssd_diag(xs, bm, cm, dts[k], cols_all[krows, :], cums[k][1], mask_add)
            ecol = ecol_all[krows, :]
            xw = xs * sdt_all[krows, :]
            dec = ecol[L - 1:L, :]
            y_off_parts = []
            for g in range(N_SSD_GROUPS):
                htg = ht[n, g]
                cg = cm[:, D_STATE * g:D_STATE * (g + 1)].astype(BF16)
                y_off_parts.append(jnp.dot(cg, htg.astype(BF16), preferred_element_type=F32))
                bg_t = bm[:, D_STATE * g:D_STATE * (g + 1)].T.astype(BF16)
                st = jnp.dot(bg_t, xw[:, half * g:half * (g + 1)].astype(BF16), preferred_element_type=F32)
                ht[n, g] = htg * dec[:, half * g:half * (g + 1)] + st
            ys = y_diag + jnp.concatenate(y_off_parts, axis=1) * ecol
            y_ref[n, rows, LRU_WIDTH:MIX_WIDTH] = _ssd_gate_norm(
                ys, xs, zact_ref[n, rows, :], dskip_ref[...], gssd_ref[...])

    @pl.when(t == nt - 1)
    def _final():
        for n in range(PROMPT_NB):
            olh_ref[n] = hcar[n, 0:1, :]
            for g in range(N_SSD_GROUPS):
                osh_ref[n, half * g:half * (g + 1), :] = ht[n, g].T


def _param_specs(const):
    return [
        pl.BlockSpec((CONV_WIDTH, LRU_WIDTH), const),
        pl.BlockSpec((1, LRU_WIDTH), const),
        pl.BlockSpec((LRU_WIDTH // MXU_DIM, MXU_DIM, 2 * MXU_DIM), lambda *_: (0, 0, 0)),
        pl.BlockSpec((1, LRU_WIDTH), const),
        pl.BlockSpec((1, LRU_WIDTH), const),
        pl.BlockSpec((1, LRU_WIDTH), const),
        pl.BlockSpec((1, LRU_WIDTH), const),
        pl.BlockSpec((CONV_WIDTH, SSD_CONV_DIM), const),
        pl.BlockSpec((1, SSD_CONV_DIM), const),
        pl.BlockSpec((1, DT_PAD), const),
        pl.BlockSpec((1, DT_PAD), const),
        pl.BlockSpec((1, SSD_WIDTH), const),
        pl.BlockSpec((1, SSD_WIDTH), const),
    ]


def _head_selectors():
    k = np.arange(2 * LANES)[:, None] % LANES
    sel_t = (k == np.arange(N_SSD_HEADS * LANES)[None, :] // LANES).astype(np.float32)
    sel_p = (k == np.arange(SSD_WIDTH)[None, :] // SSD_HEAD_DIM).astype(np.float32)
    return jnp.asarray(sel_t, BF16), jnp.asarray(sel_p, BF16)


def _mixer_prompt(act, wg, b_a, b_x, lam, g_lru, a_log, d_skip, g_ssd, sel_t, sel_p, w_out, w_up, w_down):
    bsz, seq, _ = act.shape
    tc = PROMPT_TC
    nb = PROMPT_NB
    steps = (bsz // nb) * (seq // tc)
    const = lambda b, t: (0, 0)
    w_slice = lambda b, t: (b * (seq // tc) + t, 0)
    assert all(w.shape[0] % (steps * 2 * SUBLANES) == 0 for w in (w_out, w_up, w_down))
    w_specs = [pl.BlockSpec((w.shape[0] // steps, w.shape[1]), w_slice) for w in (w_out, w_up, w_down)]
    in_specs = [
        pl.BlockSpec((nb, tc, LRU_WIDTH), lambda b, t: (b, t, 0)),
        pl.BlockSpec((nb, tc, LRU_WIDTH), lambda b, t: (b, t, 1)),
        pl.BlockSpec((nb, tc, SSD_WIDTH), lambda b, t: (b, t, 2)),
        pl.BlockSpec((nb, tc, SSD_CONV_DIM), lambda b, t: (b, t, 2)),
        pl.BlockSpec((nb, tc, DT_PAD), lambda b, t: (b, t, PROJ_MAIN // DT_PAD)),
        pl.BlockSpec((LRU_WIDTH // MXU_DIM, MXU_DIM, 2 * MXU_DIM), lambda b, t: (0, 0, 0)),
        pl.BlockSpec((1, LRU_WIDTH), const),
        pl.BlockSpec((1, LRU_WIDTH), const),
        pl.BlockSpec((1, LRU_WIDTH), const),
        pl.BlockSpec((1, LRU_WIDTH), const),
        pl.BlockSpec((1, DT_PAD), const),
        pl.BlockSpec((1, SSD_WIDTH), const),
        pl.BlockSpec((1, SSD_WIDTH), const),
        pl.BlockSpec((2 * LANES, N_SSD_HEADS * LANES), const),
        pl.BlockSpec((2 * LANES, SSD_WIDTH), const),
    ] + w_specs
    out_shape = (
        jax.ShapeDtypeStruct((bsz, seq, MIX_WIDTH), F32),
        jax.ShapeDtypeStruct((bsz, 1, LRU_WIDTH), F32),
        jax.ShapeDtypeStruct((bsz, SSD_WIDTH, D_STATE), F32),
    ) + tuple(jax.ShapeDtypeStruct(w.shape, BF16) for w in (w_out, w_up, w_down))
    out_specs = (
        pl.BlockSpec((nb, tc, MIX_WIDTH), lambda b, t: (b, t, 0)),
        pl.BlockSpec((nb, 1, LRU_WIDTH), lambda b, t: (b, 0, 0)),
        pl.BlockSpec((nb, SSD_WIDTH, D_STATE), lambda b, t: (b, 0, 0)),
    ) + tuple(w_specs)
    scratch = [
        pltpu.VMEM((nb, LRU_WIDTH // LANES, SUBLANES * SCAN_PITCH, LANES), F32),
        pltpu.VMEM((nb, LRU_WIDTH // LANES, SUBLANES * SCAN_PITCH, LANES), F32),
        pltpu.VMEM((nb, LRU_WIDTH // LANES, SUBLANES * SCAN_PITCH, LANES), F32),
        pltpu.VMEM((nb, SUBLANES, LRU_WIDTH), F32),
        pltpu.VMEM((nb, N_SSD_GROUPS, D_STATE, SSD_WIDTH // N_SSD_GROUPS), F32),
    ]
    return pl.pallas_call(
        _mixer_prompt_kernel,
        out_shape=out_shape,
        grid=(bsz // nb, seq // tc),
        in_specs=in_specs,
        out_specs=out_specs,
        scratch_shapes=scratch,
        compiler_params=pltpu.CompilerParams(
            dimension_semantics=("parallel", "arbitrary"), vmem_limit_bytes=VMEM_LIMIT),
        name="mixer_prompt",
    )(act, act, act, act, act, wg, b_a, b_x, lam, g_lru, a_log, d_skip, g_ssd, sel_t, sel_p,
      w_out, w_up, w_down)


def _mixer_sample_kernel(lx_ref, gate_ref, z_ref, xbc_ref, dt_ref,
                         slc_ref, slh_ref, ssc_ref, ssh_ref,
                         lcw_ref, lcb_ref, wg_ref, ba_ref, bx_ref, lam_ref, glru_ref,
                         scw_ref, scb_ref, dtb_ref, alog_ref, dskip_ref, gssd_ref, selt_ref, selp_ref,
                         y_ref, olc_ref, olh_ref, osc_ref, osh_ref,
                         ext_l, ext_s, pad_scr, yoff_scr, *, T):
    S = SAMPLE_SEQS
    P = SUBLANES
    K1 = CONV_WIDTH - 1
    R = S * P
    row_i = lax.broadcasted_iota(jnp.int32, (R, 1), 0) & (P - 1)
    valid = row_i < T

    def pad_rows(ref):
        width = ref.shape[-1]
        pad_scr[:, :, 0:width] = jnp.zeros((S, P, width), F32)
        pad_scr[:, 0:T, 0:width] = ref[...].reshape(S, T, width)
        return pad_scr[:, :, 0:width].reshape(R, width)

    ext_l[...] = jnp.zeros_like(ext_l)
    ext_s[...] = jnp.zeros_like(ext_s)
    for k in range(K1):
        ext_l[:, k, :] = slc_ref[k]
    ext_l[:, K1:K1 + T, :] = lx_ref[...].reshape(S, T, LRU_WIDTH)
    for k in range(K1):
        ext_s[:, k, :] = ssc_ref[k]
    ext_s[:, K1:K1 + T, :] = xbc_ref[...].reshape(S, T, SSD_CONV_DIM)
    for k in range(K1):
        olc_ref[k] = ext_l[:, T + k, :]
        osc_ref[k] = ext_s[:, T + k, :]

    el = ext_l[...].reshape(R, LRU_WIDTH)
    es = ext_s[...].reshape(R, SSD_CONV_DIM)

    def conv(e, w_ref, b_ref):
        out = b_ref[...] + e * w_ref[0:1, :]
        for k in range(1, CONV_WIDTH):
            out = out + pltpu.roll(e, R - k, axis=0) * w_ref[k:k + 1, :]
        return out

    u = conv(el, lcw_ref, lcb_ref)
    neg_c_sp = (-LRU_C) * _softplus(-lam_ref[...])
    a, b = _lru_coeffs(u, wg_ref, ba_ref[...], bx_ref[...], neg_c_sp)
    a, b = _scan_within_8(a, b)
    h0 = jnp.broadcast_to(slh_ref[...][:, None, :], (S, P, LRU_WIDTH)).reshape(R, LRU_WIDTH)
    hseq = a * h0 + b
    olh_ref[...] = hseq.reshape(S, P, LRU_WIDTH)[:, T - 1, :]
    gate = pad_rows(gate_ref)
    y_lru = _rms(hseq * _gelu_tanh(gate), glru_ref[...])

    xbc = conv(es, scw_ref, scb_ref)
    xbc = _silu(xbc)
    xs = xbc[:, 0:SSD_WIDTH]
    bm = xbc[:, SSD_WIDTH:SSD_WIDTH + N_SSD_GROUPS * D_STATE]
    cm = xbc[:, SSD_WIDTH + N_SSD_GROUPS * D_STATE:]
    dt_raw = pad_rows(dt_ref)
    dt = jnp.where(valid, _softplus(dt_raw + dtb_ref[...]), 0.0)
    lane1 = lax.broadcasted_iota(jnp.int32, (1, LANES), 1)
    a2_row = jnp.where(lane1 < N_SSD_HEADS, -LOG2E * jnp.exp(alog_ref[...]), 0.0)

    rr = lax.broadcasted_iota(jnp.int32, (R, R), 0)
    cc = lax.broadcasted_iota(jnp.int32, (R, R), 1)
    allowed = (cc <= rr) & ((rr - cc) <= (rr & (P - 1)))
    tri = jnp.where(allowed, 1.0, 0.0).astype(F32)
    mask_add = jnp.where(allowed, 0.0, NEG_BIG).astype(F32)

    cum2, cum2_t = _ssd_cumdecay(dt, a2_row, tri)
    y_diag = _ssd_diag(xs, bm, cm, dt, _spread(cum2, selt_ref), cum2_t, mask_add)
    ecol = _spread(jnp.exp2(cum2), selp_ref)
    end2 = jnp.broadcast_to(cum2.reshape(S, P, LANES)[:, P - 1:P, :], (S, P, LANES)).reshape(R, LANES)
    xw = xs * _spread(jnp.exp2(end2 - cum2) * dt, selp_ref)
    ecum_t = jnp.exp2(cum2_t)

    half = SSD_WIDTH // N_SSD_GROUPS
    for q in range(S):
        r0 = P * q
        vq = jnp.broadcast_to(ecum_t[:, r0 + P - 1:r0 + P], (N_SSD_HEADS, LANES))
        for g in range(N_SSD_GROUPS):
            hqg = ssh_ref[q, half * g:half * (g + 1), :]
            cq = cm[r0:r0 + P, D_STATE * g:D_STATE * (g + 1)].astype(BF16)
            yoff_scr[r0:r0 + P, half * g:half * (g + 1)] = lax.dot_general(
                cq, hqg.astype(BF16), (((1,), (1,)), ((), ())), preferred_element_type=F32)
            bq = bm[r0:r0 + P, D_STATE * g:D_STATE * (g + 1)].astype(BF16)
            xq = xw[r0:r0 + P, half * g:half * (g + 1)].astype(BF16)
            st = lax.dot_general(xq, bq, (((0,), (0,)), ((), ())), preferred_element_type=F32)
            for e in range(N_SSD_HEADS // N_SSD_GROUPS):
                h = (N_SSD_HEADS // N_SSD_GROUPS) * g + e
                lo_r = SSD_HEAD_DIM * e
                osh_ref[q, SSD_HEAD_DIM * h:SSD_HEAD_DIM * (h + 1), :] = (
                    vq[h:h + 1, :] * hqg[lo_r:lo_r + SSD_HEAD_DIM, :] + st[lo_r:lo_r + SSD_HEAD_DIM, :])

    ys = y_diag + yoff_scr[...] * ecol
    z = pad_rows(z_ref)
    y_ssd = _ssd_gate_norm(ys, xs, _silu(z), dskip_ref[...], gssd_ref[...])
    y_ref[:, 0:LRU_WIDTH] = y_lru.reshape(S, P, LRU_WIDTH)[:, 0:T, :].reshape(S * T, LRU_WIDTH)
    y_ref[:, LRU_WIDTH:MIX_WIDTH] = y_ssd.reshape(S, P, SSD_WIDTH)[:, 0:T, :].reshape(S * T, SSD_WIDTH)


def _mixer_sample(proj, T, st_lc, st_lh, st_sc, st_sh, params, sel_t, sel_p):
    nseq = proj.shape[0] // T
    S = SAMPLE_SEQS
    const = lambda i: (0, 0)
    in_specs = [
        pl.BlockSpec((S * T, LRU_WIDTH), lambda i: (i, 0)),
        pl.BlockSpec((S * T, LRU_WIDTH), lambda i: (i, 1)),
        pl.BlockSpec((S * T, SSD_WIDTH), lambda i: (i, 2)),
        pl.BlockSpec((S * T, SSD_CONV_DIM), lambda i: (i, 2)),
        pl.BlockSpec((S * T, DT_PAD), lambda i: (i, PROJ_MAIN // DT_PAD)),
        pl.BlockSpec((CONV_WIDTH - 1, S, LRU_WIDTH), lambda i: (0, i, 0)),
        pl.BlockSpec((S, LRU_WIDTH), lambda i: (i, 0)),
        pl.BlockSpec((CONV_WIDTH - 1, S, SSD_CONV_DIM), lambda i: (0, i, 0)),
        pl.BlockSpec((S, SSD_WIDTH, D_STATE), lambda i: (i, 0, 0)),
    ] + _param_specs(const) + [
        pl.BlockSpec((2 * LANES, N_SSD_HEADS * LANES), const),
        pl.BlockSpec((2 * LANES, SSD_WIDTH), const),
    ]
    out_shape = (
        jax.ShapeDtypeStruct((nseq * T, MIX_WIDTH), F32),
        jax.ShapeDtypeStruct((CONV_WIDTH - 1, nseq, LRU_WIDTH), F32),
        jax.ShapeDtypeStruct((nseq, LRU_WIDTH), F32),
        jax.ShapeDtypeStruct((CONV_WIDTH - 1, nseq, SSD_CONV_DIM), F32),
        jax.ShapeDtypeStruct((nseq, SSD_WIDTH, D_STATE), F32),
    )
    out_specs = (
        pl.BlockSpec((S * T, MIX_WIDTH), lambda i: (i, 0)),
        pl.BlockSpec((CONV_WIDTH - 1, S, LRU_WIDTH), lambda i: (0, i, 0)),
        pl.BlockSpec((S, LRU_WIDTH), lambda i: (i, 0)),
        pl.BlockSpec((CONV_WIDTH - 1, S, SSD_CONV_DIM), lambda i: (0, i, 0)),
        pl.BlockSpec((S, SSD_WIDTH, D_STATE), lambda i: (i, 0, 0)),
    )
    scratch = [
        pltpu.VMEM((S, SUBLANES, LRU_WIDTH), F32),
        pltpu.VMEM((S, SUBLANES, SSD_CONV_DIM), F32),
        pltpu.VMEM((S, SUBLANES, LRU_WIDTH), F32),
        pltpu.VMEM((S * SUBLANES, SSD_WIDTH), F32),
    ]
    return pl.pallas_call(
        functools.partial(_mixer_sample_kernel, T=T),
        out_shape=out_shape,
        grid=(nseq // S,),
        in_specs=in_specs,
        out_specs=out_specs,
        scratch_shapes=scratch,
        compiler_params=pltpu.CompilerParams(
            dimension_semantics=("parallel",), vmem_limit_bytes=VMEM_LIMIT),
        name="mixer_sample",
    )(proj, proj, proj, proj, proj, st_lc, st_lh, st_sc, st_sh, *params, sel_t, sel_p)


def _gate_weights(w_a, w_x):
    def tiles(w):
        per = MXU_DIM // LRU_BLOCK
        w4 = w.reshape(N_LRU_HEADS // per, per, LRU_BLOCK, LRU_BLOCK)
        eye = jnp.eye(per, dtype=w.dtype)
        t = jnp.einsum('jaik,ab->jaibk', w4, eye)
        return t.reshape(N_LRU_HEADS // per, MXU_DIM, MXU_DIM)
    return jnp.concatenate([tiles(w_a), tiles(w_x)], axis=2).astype(BF16)


def kernel(x_prompt, x_sample, state_lru_conv, state_lru_h, state_ssd_conv, state_ssd_h, g_mix, w_in,
           lru_conv_w, lru_conv_b, w_a, b_a, w_x, b_x, lam, g_lru_out, ssd_conv_w, ssd_conv_b, dt_bias,
           a_log, d_skip, g_ssd_out, w_out, g_mlp, w_up, w_down, g_final):
    depth = w_in.shape[0]
    assert depth == 1
    bp, seq, _ = x_prompt.shape
    bs, dseq, _ = x_sample.shape
    l = 0
    row = lambda v: v.reshape(1, -1)
    w_main, w_dt = _split_w_in(jnp.swapaxes(w_in, 1, 2)[l])
    params = (
        lru_conv_w[l], row(lru_conv_b[l]), _gate_weights(w_a[l], w_x[l]),
        row(b_a[l]), row(b_x[l]), row(lam[l]), row(g_lru_out[l]),
        ssd_conv_w[l], row(ssd_conv_b[l]),
        jnp.pad(row(dt_bias[l]), ((0, 0), (0, DT_PAD - N_SSD_HEADS))),
        jnp.pad(row(a_log[l]), ((0, 0), (0, DT_PAD - N_SSD_HEADS))),
        row(jnp.repeat(d_skip[l], SSD_HEAD_DIM)), row(g_ssd_out[l]),
    )
    gmix = row(g_mix[l])
    gmlp = row(g_mlp[l])
    gfin = row(g_final)

    xp2 = x_prompt.reshape(bp * seq, D_MODEL)
    (lcw, lcb, wg, ba, bx, lam_r, glru, scw, scb, dtb, alog, dskip, gssd) = params
    act_p, p_lc, p_sc, proj_s = _in_proj(xp2, bp, x_sample, gmix, w_main, w_dt, lcw, lcb, scw, scb, dtb)
    sel_t, sel_p = _head_selectors()
    ymix_p, p_lh, p_sh, w_out_b, w_up_b, w_down_b = _mixer_prompt(
        act_p.reshape(bp, seq, PROJ_PAD), wg, ba, bx, lam_r, glru, alog, dskip, gssd, sel_t, sel_p,
        w_out[l], w_up[l], w_down[l])

    ymix_s, s_lc, s_lh, s_sc, s_sh = _mixer_sample(
        proj_s, dseq, jnp.swapaxes(state_lru_conv[l], 0, 1), state_lru_h[l],
        jnp.swapaxes(state_ssd_conv[l], 0, 1),
        state_ssd_h[l].reshape(bs, SSD_WIDTH, D_STATE), params, sel_t, sel_p)
    y_prompt, y_sample = _out_mlp(xp2, ymix_p.reshape(bp * seq, MIX_WIDTH), x_sample, ymix_s,
                                  w_out_b, gmlp, w_up_b, w_down_b, gfin)

    hshape = (N_SSD_HEADS, SSD_HEAD_DIM, D_STATE)
    return (
        y_prompt.reshape(bp, seq, D_MODEL), y_sample,
        p_lc[None], p_lh.reshape(1, bp, LRU_WIDTH), p_sc[None], p_sh.reshape(1, bp, *hshape),
        jnp.swapaxes(s_lc, 0, 1)[None], s_lh[None], jnp.swapaxes(s_sc, 0, 1)[None],
        s_sh.reshape(1, bs, *hshape),
    )
```

```python
import functools
import math

import jax
import jax.numpy as jnp
import numpy as np
from jax import lax
from jax.experimental import pallas as pl
from jax.experimental.pallas import tpu as pltpu

F32 = jnp.float32
BF16 = jnp.bfloat16

D_MODEL = 1024
LRU_WIDTH = 1024
N_LRU_HEADS = 16
LRU_BLOCK = 64
LRU_C = 8.0
SSD_WIDTH = 1024
SSD_HEAD_DIM = 64
N_SSD_HEADS = 16
N_SSD_GROUPS = 2
D_STATE = 128
CONV_WIDTH = 4
SSD_CONV_DIM = SSD_WIDTH + 2 * N_SSD_GROUPS * D_STATE
D_FF = 4 * D_MODEL
EPS = 1e-6

LANES = 128
SUBLANES = 8
MXU_DIM = 256
DT_PAD = LANES
PROJ_MAIN = 2 * LRU_WIDTH + SSD_WIDTH + SSD_CONV_DIM
PROJ_PAD = PROJ_MAIN + DT_PAD
MIX_WIDTH = LRU_WIDTH + SSD_WIDTH
SSD_CHUNK = 128
PROMPT_TC = 256
PROMPT_NB = 2
ROW_TILE = 512
SPLIT_COLS = 1536
SAMPLE_SEQS = SSD_CHUNK // SUBLANES
SCAN_RUN = PROMPT_TC // SUBLANES
SCAN_PITCH = SCAN_RUN + 4
NEG_BIG = -1e30
LOG2E = 1.4426950408889634
VMEM_LIMIT = 56 * 1024 * 1024
VMEM_LIMIT_OUT = 60 * 1024 * 1024
HI = lax.Precision.HIGHEST


def _rms(x, g):
    ms = jnp.mean(x * x, axis=-1, keepdims=True)
    return x * lax.rsqrt(ms + EPS) * g


def _silu(x):
    h = 0.5 * x
    return h * jnp.tanh(h) + h


def _softplus(x):
    return jnp.maximum(x, 0.0) + jnp.log1p(jnp.exp(-jnp.abs(x)))


def _gelu_tanh(x):
    c = math.sqrt(2.0 / math.pi)
    return 0.5 * x * (1.0 + jnp.tanh(c * (x + 0.044715 * (x * x * x))))


def _lru_coeffs(u, wg_ref, b_a, b_x, neg_c_sp):
    ub = u.astype(BF16)
    r_parts, i_parts = [], []
    for j in range(LRU_WIDTH // MXU_DIM):
        g = jnp.dot(ub[:, MXU_DIM * j:MXU_DIM * (j + 1)], wg_ref[j], preferred_element_type=F32)
        r_parts.append(g[:, :MXU_DIM])
        i_parts.append(g[:, MXU_DIM:])
    t_r = jnp.tanh(jnp.concatenate(r_parts, axis=1) + 0.5 * b_a)
    t_i = jnp.tanh(jnp.concatenate(i_parts, axis=1) + 0.5 * b_x)
    half_sp = 0.5 * neg_c_sp
    log_a = t_r * half_sp + half_sp
    a = jnp.exp(log_a)
    th = jnp.tanh(log_a)
    v = (0.5 * th) / (th - 1.0)
    half_mult = jnp.where(v > 0.0, v * lax.rsqrt(v), 0.0)
    return a, half_mult * (u * t_i + u)


def _scan_within_8(a, b):
    ridx = lax.broadcasted_iota(jnp.int32, a.shape, 0) & (SUBLANES - 1)
    for k in (1, 2, 4):
        a_s = pltpu.roll(a, k, axis=0)
        b_s = pltpu.roll(b, k, axis=0)
        m = ridx >= k
        b = jnp.where(m, a * b_s + b, b)
        a = jnp.where(m, a * a_s, a)
    return a, b


def _conv_slabs(ext, w_ref, b_ref, rows, first):
    parts = []
    for s in range(ext.shape[0]):
        cols = slice(LANES * s, LANES * (s + 1))
        acc = b_ref[:, cols] + ext[s, pl.ds(first, rows), :] * w_ref[0:1, cols]
        for k in range(1, CONV_WIDTH):
            acc = acc + ext[s, pl.ds(first + k, rows), :] * w_ref[k:k + 1, cols]
        parts.append(acc)
    return jnp.concatenate(parts, axis=1)


def _lru_scan_strided(a, b, hcar, a_pad, b_pad, h_pad):
    rows = a.shape[0]
    S = rows // SUBLANES
    nslab = LRU_WIDTH // LANES
    ridx = lax.broadcasted_iota(jnp.int32, (SUBLANES, LANES), 0)
    step = lambda ref, s, i: ref[s, pl.ds(i, SUBLANES, stride=SCAN_PITCH), :]
    for s in range(nslab):
        cols = slice(LANES * s, LANES * (s + 1))
        for j in range(SUBLANES):
            a_pad[s, SCAN_PITCH * j:SCAN_PITCH * j + S, :] = a[S * j:S * (j + 1), cols]
            b_pad[s, SCAN_PITCH * j:SCAN_PITCH * j + S, :] = b[S * j:S * (j + 1), cols]
    h = [jnp.zeros((SUBLANES, LANES), F32)] * nslab
    prod = [jnp.ones((SUBLANES, LANES), F32)] * nslab
    for i in range(S):
        for s in range(nslab):
            av = step(a_pad, s, i)
            h[s] = av * h[s] + step(b_pad, s, i)
            prod[s] = av * prod[s]
    for s in range(nslab):
        cols = slice(LANES * s, LANES * (s + 1))
        pcum, hcum = _scan_within_8(prod[s], h[s])
        cin = hcar[:, cols]
        ends = hcum + pcum * cin
        h[s] = jnp.where(ridx == 0, cin, pltpu.roll(ends, 1, axis=0))
        hcar[:, cols] = jnp.broadcast_to(ends[SUBLANES - 1:SUBLANES, :], (SUBLANES, LANES))
    for i in range(S):
        for s in range(nslab):
            h[s] = step(a_pad, s, i) * h[s] + step(b_pad, s, i)
            h_pad[s, pl.ds(i, SUBLANES, stride=SCAN_PITCH), :] = h[s]
    return jnp.concatenate(
        [jnp.concatenate([h_pad[s, SCAN_PITCH * j:SCAN_PITCH * j + S, :] for j in range(SUBLANES)], axis=0)
         for s in range(nslab)], axis=1)


def _ssd_cumdecay(dt, a2_row, tri):
    cum2 = jnp.dot(tri, dt * a2_row, precision=HI, preferred_element_type=F32)
    return cum2, cum2.T[0:N_SSD_HEADS, :]


def _ssd_diag(xs, bm, cm, dt, cols, cum2_t, mask_add):
    L = xs.shape[0]
    c2_t = cum2_t - jnp.log2(dt.T[0:N_SSD_HEADS, :])
    lane = lax.broadcasted_iota(jnp.int32, (L, LANES), 1)
    lo = lane < SSD_HEAD_DIM
    y_parts = []
    for g in range(N_SSD_GROUPS):
        bg = bm[:, D_STATE * g:D_STATE * (g + 1)].astype(BF16)
        cg = cm[:, D_STATE * g:D_STATE * (g + 1)].astype(BF16)
        cb = lax.dot_general(cg, bg, (((1,), (1,)), ((), ())), preferred_element_type=F32)
        for jj in range(N_SSD_HEADS // N_SSD_GROUPS // 2):
            j = (N_SSD_HEADS // N_SSD_GROUPS // 2) * g + jj
            h0, h1 = 2 * j, 2 * j + 1
            col0 = cols[:, LANES * h0:LANES * (h0 + 1)]
            col1 = cols[:, LANES * h1:LANES * (h1 + 1)]
            m0 = cb * jnp.exp2(col0 - c2_t[h0:h0 + 1, :] + mask_add)
            m1 = cb * jnp.exp2(col1 - c2_t[h1:h1 + 1, :] + mask_add)
            lhs = jnp.concatenate([m0, m1], axis=1).astype(BF16)
            xp = xs[:, LANES * j:LANES * (j + 1)]
            rhs = jnp.concatenate([jnp.where(lo, xp, 0.0), jnp.where(lo, 0.0, xp)], axis=0).astype(BF16)
            y_parts.append(jnp.dot(lhs, rhs, preferred_element_type=F32))
    return jnp.concatenate(y_parts, axis=1)


def _spread(v, sel_ref):
    p0 = v.astype(BF16)
    p1 = (v - p0.astype(F32)).astype(BF16)
    return jnp.dot(jnp.concatenate([p0, p1], axis=1), sel_ref[...], preferred_element_type=F32)


def _ssd_gate_norm(ys, xs, z_act, dskip, g_ssd):
    ys = ys + dskip * xs
    gated = ys * z_act
    half = SSD_WIDTH // N_SSD_GROUPS
    outs = []
    for g in range(N_SSD_GROUPS):
        outs.append(_rms(gated[:, half * g:half * (g + 1)], g_ssd[:, half * g:half * (g + 1)]))
    return jnp.concatenate(outs, axis=1)


def _split_w_in_kernel(wt_ref, wdt_t_ref, main_ref, dt_ref):
    main_ref[...] = wt_ref[...].T.astype(BF16)

    @pl.when(pl.program_id(0) == 0)
    def _dt():
        dt_ref[...] = jnp.zeros_like(dt_ref)
        dt_ref[:, 0:N_SSD_HEADS] = wdt_t_ref[...].T.astype(BF16)


def _split_w_in(w_t):
    cols, rows = w_t.shape
    blk = SPLIT_COLS
    return pl.pallas_call(
        _split_w_in_kernel,
        out_shape=(jax.ShapeDtypeStruct((rows, PROJ_MAIN), BF16), jax.ShapeDtypeStruct((rows, DT_PAD), BF16)),
        grid=(PROJ_MAIN // blk,),
        in_specs=[pl.BlockSpec((blk, rows), lambda j: (j, 0)),
                  pl.BlockSpec((N_SSD_HEADS, rows), lambda j: (PROJ_MAIN // N_SSD_HEADS, 0))],
        out_specs=(pl.BlockSpec((rows, blk), lambda j: (0, j)), pl.BlockSpec((rows, DT_PAD), lambda j: (0, 0))),
        compiler_params=pltpu.CompilerParams(dimension_semantics=("arbitrary",)),
        name="split_w_in",
    )(w_t, w_t)


def _inproj_kernel(x_ref, xs_ref, g_ref, w_ref, wdt_ref, lcw_ref, lcb_ref, scw_ref, scb_ref, dtb_ref,
                   o_ref, olc_ref, osc_ref, os_ref, ext_l, ext_s, *, steps_per_seq):
    last_step = pl.program_id(0) == pl.num_programs(0) - 1
    t = lax.rem(pl.program_id(0), steps_per_seq)
    rows = ROW_TILE
    hist = SUBLANES
    o1, o2, o3 = LRU_WIDTH, 2 * LRU_WIDTH, 2 * LRU_WIDTH + SSD_WIDTH

    @pl.when(last_step)
    def _sample():
        hn = _rms(xs_ref[...].reshape(-1, D_MODEL), g_ref[...]).astype(BF16)
        os_ref[:, 0:PROJ_MAIN] = jnp.dot(hn, w_ref[...], preferred_element_type=F32)
        os_ref[:, PROJ_MAIN:PROJ_PAD] = jnp.dot(hn, wdt_ref[...], preferred_element_type=F32)

    @pl.when(jnp.logical_not(last_step))
    def _prompt():
        @pl.when(t == 0)
        def _init():
            ext_l[:, 0:hist, :] = jnp.zeros((ext_l.shape[0], hist, LANES), F32)
            ext_s[:, 0:hist, :] = jnp.zeros((ext_s.shape[0], hist, LANES), F32)

        hn = _rms(x_ref[...], g_ref[...]).astype(BF16)
        lx = jnp.dot(hn, w_ref[:, 0:o1], preferred_element_type=F32)
        for s in range(ext_l.shape[0]):
            ext_l[s, hist:hist + rows, :] = lx[:, LANES * s:LANES * (s + 1)]
        xbc_in = jnp.dot(hn, w_ref[:, o3:PROJ_MAIN], preferred_element_type=F32)
        for s in range(ext_s.shape[0]):
            ext_s[s, hist:hist + rows, :] = xbc_in[:, LANES * s:LANES * (s + 1)]
        o_ref[:, o1:o2] = _gelu_tanh(jnp.dot(hn, w_ref[:, o1:o2], preferred_element_type=F32))
        z = jnp.dot(hn, w_ref[:, o2:o3], preferred_element_type=F32)
        o_ref[:, o2:o3] = _silu(z)
        o_ref[:, PROJ_MAIN:PROJ_PAD] = _softplus(
            jnp.dot(hn, wdt_ref[...], preferred_element_type=F32) + dtb_ref[...])
        o_ref[:, 0:o1] = _conv_slabs(ext_l, lcw_ref, lcb_ref, rows, hist - (CONV_WIDTH - 1))
        xbc = _conv_slabs(ext_s, scw_ref, scb_ref, rows, hist - (CONV_WIDTH - 1))
        o_ref[:, o3:PROJ_MAIN] = _silu(xbc)

        @pl.when(t == steps_per_seq - 1)
        def _final():
            last = slice(hist + rows - (CONV_WIDTH - 1), hist + rows)
            for s in range(ext_l.shape[0]):
                olc_ref[:, LANES * s:LANES * (s + 1)] = ext_l[s, last, :]
            for s in range(ext_s.shape[0]):
                osc_ref[:, LANES * s:LANES * (s + 1)] = ext_s[s, last, :]

        tail_l = ext_l[:, rows:rows + hist, :]
        tail_s = ext_s[:, rows:rows + hist, :]
        ext_l[:, 0:hist, :] = tail_l
        ext_s[:, 0:hist, :] = tail_s


def _in_proj(x2d, bsz, xs, g_mix, w_main, w_dt, lcw, lcb, scw, scb, dtb):
    n = x2d.shape[0]
    steps_p = n // ROW_TILE
    steps_per_seq = steps_p // bsz
    assert xs.shape[0] * xs.shape[1] == ROW_TILE
    const = lambda i: (0, 0)
    tile = lambda i: (jnp.minimum(i, steps_p - 1), 0)
    seq_of = lambda i: (jnp.minimum(i, steps_p - 1) // steps_per_seq, 0, 0)
    once = dict(pipeline_mode=pl.Buffered(1))
    return pl.pallas_call(
        functools.partial(_inproj_kernel, steps_per_seq=steps_per_seq),
        out_shape=(
            jax.ShapeDtypeStruct((n, PROJ_PAD), F32),
            jax.ShapeDtypeStruct((bsz, CONV_WIDTH - 1, LRU_WIDTH), F32),
            jax.ShapeDtypeStruct((bsz, CONV_WIDTH - 1, SSD_CONV_DIM), F32),
            jax.ShapeDtypeStruct((ROW_TILE, PROJ_PAD), F32),
        ),
        grid=(steps_p + 1,),
        in_specs=[
            pl.BlockSpec((ROW_TILE, D_MODEL), tile),
            pl.BlockSpec(xs.shape, lambda i: (0, 0, 0), **once),
            pl.BlockSpec((1, D_MODEL), const),
            pl.BlockSpec((D_MODEL, PROJ_MAIN), const, **once),
            pl.BlockSpec((D_MODEL, DT_PAD), const, **once),
            pl.BlockSpec((CONV_WIDTH, LRU_WIDTH), const),
            pl.BlockSpec((1, LRU_WIDTH), const),
            pl.BlockSpec((CONV_WIDTH, SSD_CONV_DIM), const),
            pl.BlockSpec((1, SSD_CONV_DIM), const),
            pl.BlockSpec((1, DT_PAD), const),
        ],
        out_specs=(
            pl.BlockSpec((ROW_TILE, PROJ_PAD), tile),
            pl.BlockSpec((None, CONV_WIDTH - 1, LRU_WIDTH), seq_of),
            pl.BlockSpec((None, CONV_WIDTH - 1, SSD_CONV_DIM), seq_of),
            pl.BlockSpec((ROW_TILE, PROJ_PAD), const, **once),
        ),
        scratch_shapes=[
            pltpu.VMEM((LRU_WIDTH // LANES, SUBLANES + ROW_TILE, LANES), F32),
            pltpu.VMEM((SSD_CONV_DIM // LANES, SUBLANES + ROW_TILE, LANES), F32),
        ],
        compiler_params=pltpu.CompilerParams(
            dimension_semantics=("arbitrary",), vmem_limit_bytes=VMEM_LIMIT),
        name="in_proj",
    )(x2d, xs, g_mix, w_main, w_dt, lcw, lcb, scw, scb, dtb)


def _outmlp_kernel(xp_ref, yp_ref, xs_ref, ys_ref, wo_ref, gm_ref, wu_ref, wd_ref, gf_ref, op_ref, os_ref):
    last = pl.program_id(0) == pl.num_programs(0) - 1

    def run(x_ref, y_ref, o_ref):
        x = x_ref[...].reshape(-1, D_MODEL)
        x1 = x + jnp.dot(y_ref[...].astype(BF16), wo_ref[...], preferred_element_type=F32)
        m = _rms(x1, gm_ref[...]).astype(BF16)
        u = jnp.dot(m, wu_ref[...], preferred_element_type=F32)
        u = jnp.square(jnp.maximum(u, 0.0)).astype(BF16)
        x2 = x1 + jnp.dot(u, wd_ref[...], preferred_element_type=F32)
        o_ref[...] = _rms(x2, gf_ref[...]).reshape(o_ref.shape)

    @pl.when(jnp.logical_not(last))
    def _prompt():
        run(xp_ref, yp_ref, op_ref)

    @pl.when(last)
    def _sample():
        run(xs_ref, ys_ref, os_ref)


def _out_mlp(xp, ymix_p, xs, ymix_s, w_out_b, g_mlp, w_up_b, w_down_b, g_final):
    n_p = xp.shape[0]
    steps_p = n_p // ROW_TILE
    assert ymix_s.shape[0] == ROW_TILE
    const = lambda i: (0, 0)
    tile = lambda i: (jnp.minimum(i, steps_p - 1), 0)
    once = dict(pipeline_mode=pl.Buffered(1))
    return pl.pallas_call(
        _outmlp_kernel,
        out_shape=(jax.ShapeDtypeStruct((n_p, D_MODEL), F32), jax.ShapeDtypeStruct(xs.shape, F32)),
        grid=(steps_p + 1,),
        in_specs=[
            pl.BlockSpec((ROW_TILE, D_MODEL), tile),
            pl.BlockSpec((ROW_TILE, MIX_WIDTH), tile),
            pl.BlockSpec(xs.shape, lambda i: (0, 0, 0), **once),
            pl.BlockSpec((ROW_TILE, MIX_WIDTH), const, **once),
            pl.BlockSpec((MIX_WIDTH, D_MODEL), const, **once),
            pl.BlockSpec((1, D_MODEL), const),
            pl.BlockSpec((D_MODEL, D_FF), const, **once),
            pl.BlockSpec((D_FF, D_MODEL), const, **once),
            pl.BlockSpec((1, D_MODEL), const),
        ],
        out_specs=(pl.BlockSpec((ROW_TILE, D_MODEL), tile), pl.BlockSpec(xs.shape, lambda i: (0, 0, 0))),
        compiler_params=pltpu.CompilerParams(
            dimension_semantics=("arbitrary",), vmem_limit_bytes=VMEM_LIMIT_OUT),
        name="out_mlp",
    )(xp, ymix_p, xs, ymix_s, w_out_b, g_mlp, w_up_b, w_down_b, g_final)


def _mixer_prompt_kernel(u_ref, gl_ref, zact_ref, xbc_ref, dt_ref,
                         wg_ref, ba_ref, bx_ref, lam_ref, glru_ref, alog_ref, dskip_ref, gssd_ref,
                         selt_ref, selp_ref, wo_ref, wu_ref, wd_ref,
                         y_ref, olh_ref, osh_ref, wo_b_ref, wu_b_ref, wd_b_ref,
                         a_pad, b_pad, h_pad, hcar, ht):
    t = pl.program_id(1)
    nt = pl.num_programs(1)
    tc = PROMPT_TC
    wo_b_ref[...] = wo_ref[...].astype(BF16)
    wu_b_ref[...] = wu_ref[...].astype(BF16)
    wd_b_ref[...] = wd_ref[...].astype(BF16)

    @pl.when(t == 0)
    def _init():
        hcar[...] = jnp.zeros_like(hcar)
        ht[...] = jnp.zeros_like(ht)

    neg_c_sp = (-LRU_C) * _softplus(-lam_ref[...])
    lane1 = lax.broadcasted_iota(jnp.int32, (1, LANES), 1)
    a2_row = jnp.where(lane1 < N_SSD_HEADS, -LOG2E * jnp.exp(alog_ref[...]), 0.0)
    L = SSD_CHUNK
    rr = lax.broadcasted_iota(jnp.int32, (L, L), 0)
    cc = lax.broadcasted_iota(jnp.int32, (L, L), 1)
    causal = cc <= rr
    tri = jnp.where(causal, 1.0, 0.0).astype(F32)
    mask_add = jnp.where(causal, 0.0, NEG_BIG).astype(F32)
    half = SSD_WIDTH // N_SSD_GROUPS

    chunks = [(n, c) for n in range(PROMPT_NB) for c in range(tc // L)]
    cums = [_ssd_cumdecay(dt_ref[n, L * c:L * (c + 1), :], a2_row, tri) for n, c in chunks]
    dts = [dt_ref[n, L * c:L * (c + 1), :] for n, c in chunks]
    cum_all = jnp.concatenate([cum2 for cum2, _ in cums], axis=0)
    cols_all = _spread(cum_all, selt_ref)
    ecol_all = _spread(jnp.exp2(cum_all), selp_ref)
    sdt_all = _spread(jnp.concatenate(
        [jnp.exp2(cum2[L - 1:L, :] - cum2) * dt for (cum2, _), dt in zip(cums, dts)], axis=0), selp_ref)

    for n in range(PROMPT_NB):
        a, b = _lru_coeffs(u_ref[n], wg_ref, ba_ref[...], bx_ref[...], neg_c_sp)
        hseq = _lru_scan_strided(a, b, hcar.at[n], a_pad.at[n], b_pad.at[n], h_pad.at[n])
        y_ref[n, :, 0:LRU_WIDTH] = _rms(hseq * gl_ref[n], glru_ref[...])

        for c in range(tc // L):
            k = chunks.index((n, c))
            rows = slice(L * c, L * (c + 1))
            krows = slice(L * k, L * (k + 1))
            xs = xbc_ref[n, rows, 0:SSD_WIDTH]
            bm = xbc_ref[n, rows, SSD_WIDTH:SSD_WIDTH + N_SSD_GROUPS * D_STATE]
            cm = xbc_ref[n, rows, SSD_WIDTH + N_SSD_GROUPS * D_STATE:SSD_CONV_DIM]
            y_diag = _ssd_diag(xs, bm, cm, dts[k], cols_all[krows, :], cums[k][1], mask_add)
            ecol = ecol_all[krows, :]
            xw = xs * sdt_all[krows, :]
            dec = ecol[L - 1:L, :]
            y_off_parts = []
            for g in range(N_SSD_GROUPS):
                htg = ht[n, g]
                cg = cm[:, D_STATE * g:D_STATE * (g + 1)].astype(BF16)
                y_off_parts.append(jnp.dot(cg, htg.astype(BF16), preferred_element_type=F32))
                bg_t = bm[:, D_STATE * g:D_STATE * (g + 1)].T.astype(BF16)
                st = jnp.dot(bg_t, xw[:, half * g:half * (g + 1)].astype(BF16), preferred_element_type=F32)
                ht[n, g] = htg * dec[:, half * g:half * (g + 1)] + st
            ys = y_diag + jnp.concatenate(y_off_parts, axis=1) * ecol
            y_ref[n, rows, LRU_WIDTH:MIX_WIDTH] = _ssd_gate_norm(
                ys, xs, zact_ref[n, rows, :], dskip_ref[...], gssd_ref[...])

    @pl.when(t == nt - 1)
    def _final():
        for n in range(PROMPT_NB):
            olh_ref[n] = hcar[n, 0:1, :]
            for g in range(N_SSD_GROUPS):
                osh_ref[n, half * g:half * (g + 1), :] = ht[n, g].T


def _param_specs(const):
    return [
        pl.BlockSpec((CONV_WIDTH, LRU_WIDTH), const),
        pl.BlockSpec((1, LRU_WIDTH), const),
        pl.BlockSpec((LRU_WIDTH // MXU_DIM, MXU_DIM, 2 * MXU_DIM), lambda *_: (0, 0, 0)),
        pl.BlockSpec((1, LRU_WIDTH), const),
        pl.BlockSpec((1, LRU_WIDTH), const),
        pl.BlockSpec((1, LRU_WIDTH), const),
        pl.BlockSpec((1, LRU_WIDTH), const),
        pl.BlockSpec((CONV_WIDTH, SSD_CONV_DIM), const),
        pl.BlockSpec((1, SSD_CONV_DIM), const),
        pl.BlockSpec((1, DT_PAD), const),
        pl.BlockSpec((1, DT_PAD), const),
        pl.BlockSpec((1, SSD_WIDTH), const),
        pl.BlockSpec((1, SSD_WIDTH), const),
    ]


def _head_selectors():
    k = np.arange(2 * LANES)[:, None] % LANES
    sel_t = (k == np.arange(N_SSD_HEADS * LANES)[None, :] // LANES).astype(np.float32)
    sel_p = (k == np.arange(SSD_WIDTH)[None, :] // SSD_HEAD_DIM).astype(np.float32)
    return jnp.asarray(sel_t, BF16), jnp.asarray(sel_p, BF16)


def _mixer_prompt(act, wg, b_a, b_x, lam, g_lru, a_log, d_skip, g_ssd, sel_t, sel_p, w_out, w_up, w_down):
    bsz, seq, _ = act.shape
    tc = PROMPT_TC
    nb = PROMPT_NB
    steps = (bsz // nb) * (seq // tc)
    const = lambda b, t: (0, 0)
    w_slice = lambda b, t: (b * (seq // tc) + t, 0)
    assert all(w.shape[0] % (steps * 2 * SUBLANES) == 0 for w in (w_out, w_up, w_down))
    w_specs = [pl.BlockSpec((w.shape[0] // steps, w.shape[1]), w_slice) for w in (w_out, w_up, w_down)]
    in_specs = [
        pl.BlockSpec((nb, tc, LRU_WIDTH), lambda b, t: (b, t, 0)),
        pl.BlockSpec((nb, tc, LRU_WIDTH), lambda b, t: (b, t, 1)),
        pl.BlockSpec((nb, tc, SSD_WIDTH), lambda b, t: (b, t, 2)),
        pl.BlockSpec((nb, tc, SSD_CONV_DIM), lambda b, t: (b, t, 2)),
        pl.BlockSpec((nb, tc, DT_PAD), lambda b, t: (b, t, PROJ_MAIN // DT_PAD)),
        pl.BlockSpec((LRU_WIDTH // MXU_DIM, MXU_DIM, 2 * MXU_DIM), lambda b, t: (0, 0, 0)),
        pl.BlockSpec((1, LRU_WIDTH), const),
        pl.BlockSpec((1, LRU_WIDTH), const),
        pl.BlockSpec((1, LRU_WIDTH), const),
        pl.BlockSpec((1, LRU_WIDTH), const),
        pl.BlockSpec((1, DT_PAD), const),
        pl.BlockSpec((1, SSD_WIDTH), const),
        pl.BlockSpec((1, SSD_WIDTH), const),
        pl.BlockSpec((2 * LANES, N_SSD_HEADS * LANES), const),
        pl.BlockSpec((2 * LANES, SSD_WIDTH), const),
    ] + w_specs
    out_shape = (
        jax.ShapeDtypeStruct((bsz, seq, MIX_WIDTH), F32),
        jax.ShapeDtypeStruct((bsz, 1, LRU_WIDTH), F32),
        jax.ShapeDtypeStruct((bsz, SSD_WIDTH, D_STATE), F32),
    ) + tuple(jax.ShapeDtypeStruct(w.shape, BF16) for w in (w_out, w_up, w_down))
    out_specs = (
        pl.BlockSpec((nb, tc, MIX_WIDTH), lambda b, t: (b, t, 0)),
        pl.BlockSpec((nb, 1, LRU_WIDTH), lambda b, t: (b, 0, 0)),
        pl.BlockSpec((nb, SSD_WIDTH, D_STATE), lambda b, t: (b, 0, 0)),
    ) + tuple(w_specs)
    scratch = [
        pltpu.VMEM((nb, LRU_WIDTH // LANES, SUBLANES * SCAN_PITCH, LANES), F32),
        pltpu.VMEM((nb, LRU_WIDTH // LANES, SUBLANES * SCAN_PITCH, LANES), F32),
        pltpu.VMEM((nb, LRU_WIDTH // LANES, SUBLANES * SCAN_PITCH, LANES), F32),
        pltpu.VMEM((nb, SUBLANES, LRU_WIDTH), F32),
        pltpu.VMEM((nb, N_SSD_GROUPS, D_STATE, SSD_WIDTH // N_SSD_GROUPS), F32),
    ]
    return pl.pallas_call(
        _mixer_prompt_kernel,
        out_shape=out_shape,
        grid=(bsz // nb, seq // tc),
        in_specs=in_specs,
        out_specs=out_specs,
        scratch_shapes=scratch,
        compiler_params=pltpu.CompilerParams(
            dimension_semantics=("parallel", "arbitrary"), vmem_limit_bytes=VMEM_LIMIT),
        name="mixer_prompt",
    )(act, act, act, act, act, wg, b_a, b_x, lam, g_lru, a_log, d_skip, g_ssd, sel_t, sel_p,
      w_out, w_up, w_down)


def _mixer_sample_kernel(lx_ref, gate_ref, z_ref, xbc_ref, dt_ref,
                         slc_ref, slh_ref, ssc_ref, ssh_ref,
                         lcw_ref, lcb_ref, wg_ref, ba_ref, bx_ref, lam_ref, glru_ref,
                         scw_ref, scb_ref, dtb_ref, alog_ref, dskip_ref, gssd_ref, selt_ref, selp_ref,
                         y_ref, olc_ref, olh_ref, osc_ref, osh_ref,
                         ext_l, ext_s, pad_scr, yoff_scr, *, T):
    S = SAMPLE_SEQS
    P = SUBLANES
    K1 = CONV_WIDTH - 1
    R = S * P
    row_i = lax.broadcasted_iota(jnp.int32, (R, 1), 0) & (P - 1)
    valid = row_i < T

    def pad_rows(ref):
        width = ref.shape[-1]
        pad_scr[:, :, 0:width] = jnp.zeros((S, P, width), F32)
        pad_scr[:, 0:T, 0:width] = ref[...].reshape(S, T, width)
        return pad_scr[:, :, 0:width].reshape(R, width)

    ext_l[...] = jnp.zeros_like(ext_l)
    ext_s[...] = jnp.zeros_like(ext_s)
    for k in range(K1):
        ext_l[:, k, :] = slc_ref[k]
    ext_l[:, K1:K1 + T, :] = lx_ref[...].reshape(S, T, LRU_WIDTH)
    for k in range(K1):
        ext_s[:, k, :] = ssc_ref[k]
    ext_s[:, K1:K1 + T, :] = xbc_ref[...].reshape(S, T, SSD_CONV_DIM)
    for k in range(K1):
        olc_ref[k] = ext_l[:, T + k, :]
        osc_ref[k] = ext_s[:, T + k, :]

    el = ext_l[...].reshape(R, LRU_WIDTH)
    es = ext_s[...].reshape(R, SSD_CONV_DIM)

    def conv(e, w_ref, b_ref):
        out = b_ref[...] + e * w_ref[0:1, :]
        for k in range(1, CONV_WIDTH):
            out = out + pltpu.roll(e, R - k, axis=0) * w_ref[k:k + 1, :]
        return out

    u = conv(el, lcw_ref, lcb_ref)
    neg_c_sp = (-LRU_C) * _softplus(-lam_ref[...])
    a, b = _lru_coeffs(u, wg_ref, ba_ref[...], bx_ref[...], neg_c_sp)
    a, b = _scan_within_8(a, b)
    h0 = jnp.broadcast_to(slh_ref[...][:, None, :], (S, P, LRU_WIDTH)).reshape(R, LRU_WIDTH)
    hseq = a * h0 + b
    olh_ref[...] = hseq.reshape(S, P, LRU_WIDTH)[:, T - 1, :]
    gate = pad_rows(gate_ref)
    y_lru = _rms(hseq * _gelu_tanh(gate), glru_ref[...])

    xbc = conv(es, scw_ref, scb_ref)
    xbc = _silu(xbc)
    xs = xbc[:, 0:SSD_WIDTH]
    bm = xbc[:, SSD_WIDTH:SSD_WIDTH + N_SSD_GROUPS * D_STATE]
    cm = xbc[:, SSD_WIDTH + N_SSD_GROUPS * D_STATE:]
    dt_raw = pad_rows(dt_ref)
    dt = jnp.where(valid, _softplus(dt_raw + dtb_ref[...]), 0.0)
    lane1 = lax.broadcasted_iota(jnp.int32, (1, LANES), 1)
    a2_row = jnp.where(lane1 < N_SSD_HEADS, -LOG2E * jnp.exp(alog_ref[...]), 0.0)

    rr = lax.broadcasted_iota(jnp.int32, (R, R), 0)
    cc = lax.broadcasted_iota(jnp.int32, (R, R), 1)
    allowed = (cc <= rr) & ((rr - cc) <= (rr & (P - 1)))
    tri = jnp.where(allowed, 1.0, 0.0).astype(F32)
    mask_add = jnp.where(allowed, 0.0, NEG_BIG).astype(F32)

    cum2, cum2_t = _ssd_cumdecay(dt, a2_row, tri)
    y_diag = _ssd_diag(xs, bm, cm, dt, _spread(cum2, selt_ref), cum2_t, mask_add)
    ecol = _spread(jnp.exp2(cum2), selp_ref)
    end2 = jnp.broadcast_to(cum2.reshape(S, P, LANES)[:, P - 1:P, :], (S, P, LANES)).reshape(R, LANES)
    xw = xs * _spread(jnp.exp2(end2 - cum2) * dt, selp_ref)
    ecum_t = jnp.exp2(cum2_t)

    half = SSD_WIDTH // N_SSD_GROUPS
    for q in range(S):
        r0 = P * q
        vq = jnp.broadcast_to(ecum_t[:, r0 + P - 1:r0 + P], (N_SSD_HEADS, LANES))
        for g in range(N_SSD_GROUPS):
            hqg = ssh_ref[q, half * g:half * (g + 1), :]
            cq = cm[r0:r0 + P, D_STATE * g:D_STATE * (g + 1)].astype(BF16)
            yoff_scr[r0:r0 + P, half * g:half * (g + 1)] = lax.dot_general(
                cq, hqg.astype(BF16), (((1,), (1,)), ((), ())), preferred_element_type=F32)
            bq = bm[r0:r0 + P, D_STATE * g:D_STATE * (g + 1)].astype(BF16)
            xq = xw[r0:r0 + P, half * g:half * (g + 1)].astype(BF16)
            st = lax.dot_general(xq, bq, (((0,), (0,)), ((), ())), preferred_element_type=F32)
            for e in range(N_SSD_HEADS // N_SSD_GROUPS):
                h = (N_SSD_HEADS // N_SSD_GROUPS) * g + e
                lo_r = SSD_HEAD_DIM * e
                osh_ref[q, SSD_HEAD_DIM * h:SSD_HEAD_DIM * (h + 1), :] = (
                    vq[h:h + 1, :] * hqg[lo_r:lo_r + SSD_HEAD_DIM, :] + st[lo_r:lo_r + SSD_HEAD_DIM, :])

    ys = y_diag + yoff_scr[...] * ecol
    z = pad_rows(z_ref)
    y_ssd = _ssd_gate_norm(ys, xs, _silu(z), dskip_ref[...], gssd_ref[...])
    y_ref[:, 0:LRU_WIDTH] = y_lru.reshape(S, P, LRU_WIDTH)[:, 0:T, :].reshape(S * T, LRU_WIDTH)
    y_ref[:, LRU_WIDTH:MIX_WIDTH] = y_ssd.reshape(S, P, SSD_WIDTH)[:, 0:T, :].reshape(S * T, SSD_WIDTH)


def _mixer_sample(proj, T, st_lc, st_lh, st_sc, st_sh, params, sel_t, sel_p):
    nseq = proj.shape[0] // T
    S = SAMPLE_SEQS
    const = lambda i: (0, 0)
    in_specs = [
        pl.BlockSpec((S * T, LRU_WIDTH), lambda i: (i, 0)),
        pl.BlockSpec((S * T, LRU_WIDTH), lambda i: (i, 1)),
        pl.BlockSpec((S * T, SSD_WIDTH), lambda i: (i, 2)),
        pl.BlockSpec((S * T, SSD_CONV_DIM), lambda i: (i, 2)),
        pl.BlockSpec((S * T, DT_PAD), lambda i: (i, PROJ_MAIN // DT_PAD)),
        pl.BlockSpec((CONV_WIDTH - 1, S, LRU_WIDTH), lambda i: (0, i, 0)),
        pl.BlockSpec((S, LRU_WIDTH), lambda i: (i, 0)),
        pl.BlockSpec((CONV_WIDTH - 1, S, SSD_CONV_DIM), lambda i: (0, i, 0)),
        pl.BlockSpec((S, SSD_WIDTH, D_STATE), lambda i: (i, 0, 0)),
    ] + _param_specs(const) + [
        pl.BlockSpec((2 * LANES, N_SSD_HEADS * LANES), const),
        pl.BlockSpec((2 * LANES, SSD_WIDTH), const),
    ]
    out_shape = (
        jax.ShapeDtypeStruct((nseq * T, MIX_WIDTH), F32),
        jax.ShapeDtypeStruct((CONV_WIDTH - 1, nseq, LRU_WIDTH), F32),
        jax.ShapeDtypeStruct((nseq, LRU_WIDTH), F32),
        jax.ShapeDtypeStruct((CONV_WIDTH - 1, nseq, SSD_CONV_DIM), F32),
        jax.ShapeDtypeStruct((nseq, SSD_WIDTH, D_STATE), F32),
    )
    out_specs = (
        pl.BlockSpec((S * T, MIX_WIDTH), lambda i: (i, 0)),
        pl.BlockSpec((CONV_WIDTH - 1, S, LRU_WIDTH), lambda i: (0, i, 0)),
        pl.BlockSpec((S, LRU_WIDTH), lambda i: (i, 0)),
        pl.BlockSpec((CONV_WIDTH - 1, S, SSD_CONV_DIM), lambda i: (0, i, 0)),
        pl.BlockSpec((S, SSD_WIDTH, D_STATE), lambda i: (i, 0, 0)),
    )
    scratch = [
        pltpu.VMEM((S, SUBLANES, LRU_WIDTH), F32),
        pltpu.VMEM((S, SUBLANES, SSD_CONV_DIM), F32),
        pltpu.VMEM((S, SUBLANES, LRU_WIDTH), F32),
        pltpu.VMEM((S * SUBLANES, SSD_WIDTH), F32),
    ]
    return pl.pallas_call(
        functools.partial(_mixer_sample_kernel, T=T),
        out_shape=out_shape,
        grid=(nseq // S,),
        in_specs=in_specs,
        out_specs=out_specs,
        scratch_shapes=scratch,
        compiler_params=pltpu.CompilerParams(
            dimension_semantics=("parallel",), vmem_limit_bytes=VMEM_LIMIT),
        name="mixer_sample",
    )(proj, proj, proj, proj, proj, st_lc, st_lh, st_sc, st_sh, *params, sel_t, sel_p)


def _gate_weights(w_a, w_x):
    def tiles(w):
        per = MXU_DIM // LRU_BLOCK
        w4 = w.reshape(N_LRU_HEADS // per, per, LRU_BLOCK, LRU_BLOCK)
        eye = 0.5 * jnp.eye(per, dtype=w.dtype)
        t = jnp.einsum('jaik,ab->jaibk', w4, eye)
        return t.reshape(N_LRU_HEADS // per, MXU_DIM, MXU_DIM)
    return jnp.concatenate([tiles(w_a), tiles(w_x)], axis=2).astype(BF16)


def kernel(x_prompt, x_sample, state_lru_conv, state_lru_h, state_ssd_conv, state_ssd_h, g_mix, w_in,
           lru_conv_w, lru_conv_b, w_a, b_a, w_x, b_x, lam, g_lru_out, ssd_conv_w, ssd_conv_b, dt_bias,
           a_log, d_skip, g_ssd_out, w_out, g_mlp, w_up, w_down, g_final):
    depth = w_in.shape[0]
    assert depth == 1
    bp, seq, _ = x_prompt.shape
    bs, dseq, _ = x_sample.shape
    l = 0
    row = lambda v: v.reshape(1, -1)
    w_main, w_dt = _split_w_in(jnp.swapaxes(w_in, 1, 2)[l])
    params = (
        lru_conv_w[l], row(lru_conv_b[l]), _gate_weights(w_a[l], w_x[l]),
        row(b_a[l]), row(b_x[l]), row(lam[l]), row(g_lru_out[l]),
        ssd_conv_w[l], row(ssd_conv_b[l]),
        jnp.pad(row(dt_bias[l]), ((0, 0), (0, DT_PAD - N_SSD_HEADS))),
        jnp.pad(row(a_log[l]), ((0, 0), (0, DT_PAD - N_SSD_HEADS))),
        row(jnp.repeat(d_skip[l], SSD_HEAD_DIM)), row(g_ssd_out[l]),
    )
    gmix = row(g_mix[l])
    gmlp = row(g_mlp[l])
    gfin = row(g_final)

    xp2 = x_prompt.reshape(bp * seq, D_MODEL)
    (lcw, lcb, wg, ba, bx, lam_r, glru, scw, scb, dtb, alog, dskip, gssd) = params
    act_p, p_lc, p_sc, proj_s = _in_proj(xp2, bp, x_sample, gmix, w_main, w_dt, lcw, lcb, scw, scb, dtb)
    sel_t, sel_p = _head_selectors()
    ymix_p, p_lh, p_sh, w_out_b, w_up_b, w_down_b = _mixer_prompt(
        act_p.reshape(bp, seq, PROJ_PAD), wg, ba, bx, lam_r, glru, alog, dskip, gssd, sel_t, sel_p,
        w_out[l], w_up[l], w_down[l])

    ymix_s, s_lc, s_lh, s_sc, s_sh = _mixer_sample(
        proj_s, dseq, jnp.swapaxes(state_lru_conv[l], 0, 1), state_lru_h[l],
        jnp.swapaxes(state_ssd_conv[l], 0, 1),
        state_ssd_h[l].reshape(bs, SSD_WIDTH, D_STATE), params, sel_t, sel_p)
    y_prompt, y_sample = _out_mlp(xp2, ymix_p.reshape(bp * seq, MIX_WIDTH), x_sample, ymix_s,
                                  w_out_b, gmlp, w_up_b, w_down_b, gfin)

    hshape = (N_SSD_HEADS, SSD_HEAD_DIM, D_STATE)
    return (
        y_prompt.reshape(bp, seq, D_MODEL), y_sample,
        p_lc[None], p_lh.reshape(1, bp, LRU_WIDTH), p_sc[None], p_sh.reshape(1, bp, *hshape),
        jnp.swapaxes(s_lc, 0, 1)[None], s_lh[None], jnp.swapaxes(s_sc, 0, 1)[None],
        s_sh.reshape(1, bs, *hshape),
    )
```

```python
import functools
import math

import jax
import jax.numpy as jnp
import numpy as np
from jax import lax
from jax.experimental import pallas as pl
from jax.experimental.pallas import tpu as pltpu

F32 = jnp.float32
BF16 = jnp.bfloat16

D_MODEL = 1024
LRU_WIDTH = 1024
N_LRU_HEADS = 16
LRU_BLOCK = 64
LRU_C = 8.0
SSD_WIDTH = 1024
SSD_HEAD_DIM = 64
N_SSD_HEADS = 16
N_SSD_GROUPS = 2
D_STATE = 128
CONV_WIDTH = 4
SSD_CONV_DIM = SSD_WIDTH + 2 * N_SSD_GROUPS * D_STATE
D_FF = 4 * D_MODEL
EPS = 1e-6

LANES = 128
SUBLANES = 8
MXU_DIM = 256
DT_PAD = LANES
PROJ_MAIN = 2 * LRU_WIDTH + SSD_WIDTH + SSD_CONV_DIM
PROJ_PAD = PROJ_MAIN + DT_PAD
MIX_WIDTH = LRU_WIDTH + SSD_WIDTH
SSD_CHUNK = 128
PROMPT_TC = 256
PROMPT_NB = 2
ROW_TILE = 512
SPLIT_COLS = 1536
SAMPLE_SEQS = SSD_CHUNK // SUBLANES
SCAN_RUN = PROMPT_TC // SUBLANES
SCAN_PITCH = SCAN_RUN + 4
NEG_BIG = -1e30
LOG2E = 1.4426950408889634
VMEM_LIMIT = 56 * 1024 * 1024
VMEM_LIMIT_OUT = 60 * 1024 * 1024
HI = lax.Precision.HIGHEST


def _rms(x, g):
    ms = jnp.mean(x * x, axis=-1, keepdims=True)
    return x * lax.rsqrt(ms + EPS) * g


def _silu(x):
    h = 0.5 * x
    return h * jnp.tanh(h) + h


def _softplus(x):
    return jnp.maximum(x, 0.0) + jnp.log1p(jnp.exp(-jnp.abs(x)))


def _gelu_tanh(x):
    c = math.sqrt(2.0 / math.pi)
    return 0.5 * x * (1.0 + jnp.tanh(c * (x + 0.044715 * (x * x * x))))


def _lru_coeffs(u, wg_ref, b_a, b_x, neg_c_sp):
    ub = u.astype(BF16)
    r_parts, i_parts = [], []
    for j in range(LRU_WIDTH // MXU_DIM):
        g = jnp.dot(ub[:, MXU_DIM * j:MXU_DIM * (j + 1)], wg_ref[j], preferred_element_type=F32)
        r_parts.append(g[:, :MXU_DIM])
        i_parts.append(g[:, MXU_DIM:])
    t_r = jnp.tanh(jnp.concatenate(r_parts, axis=1) + 0.5 * b_a)
    t_i = jnp.tanh(jnp.concatenate(i_parts, axis=1) + 0.5 * b_x)
    half_sp = 0.5 * neg_c_sp
    log_a = t_r * half_sp + half_sp
    a = jnp.exp(log_a)
    th = jnp.tanh(log_a)
    n = -0.5 * th
    nd = n * (1.0 - th)
    half_mult = jnp.where(nd > 0.0, n * lax.rsqrt(nd), 0.0)
    return a, half_mult * (u * t_i + u)


def _scan_within_8(a, b):
    ridx = lax.broadcasted_iota(jnp.int32, a.shape, 0) & (SUBLANES - 1)
    for k in (1, 2, 4):
        a_s = pltpu.roll(a, k, axis=0)
        b_s = pltpu.roll(b, k, axis=0)
        m = ridx >= k
        b = jnp.where(m, a * b_s + b, b)
        a = jnp.where(m, a * a_s, a)
    return a, b


def _conv_slabs(ext, w_ref, b_ref, rows, first):
    parts = []
    for s in range(ext.shape[0]):
        cols = slice(LANES * s, LANES * (s + 1))
        acc = b_ref[:, cols] + ext[s, pl.ds(first, rows), :] * w_ref[0:1, cols]
        for k in range(1, CONV_WIDTH):
            acc = acc + ext[s, pl.ds(first + k, rows), :] * w_ref[k:k + 1, cols]
        parts.append(acc)
    return jnp.concatenate(parts, axis=1)


def _lru_scan_strided(a, b, hcar, a_pad, b_pad, h_pad):
    rows = a.shape[0]
    S = rows // SUBLANES
    nslab = LRU_WIDTH // LANES
    ridx = lax.broadcasted_iota(jnp.int32, (SUBLANES, LANES), 0)
    step = lambda ref, s, i: ref[s, pl.ds(i, SUBLANES, stride=SCAN_PITCH), :]
    for s in range(nslab):
        cols = slice(LANES * s, LANES * (s + 1))
        for j in range(SUBLANES):
            a_pad[s, SCAN_PITCH * j:SCAN_PITCH * j + S, :] = a[S * j:S * (j + 1), cols]
            b_pad[s, SCAN_PITCH * j:SCAN_PITCH * j + S, :] = b[S * j:S * (j + 1), cols]
    h = [jnp.zeros((SUBLANES, LANES), F32)] * nslab
    prod = [jnp.ones((SUBLANES, LANES), F32)] * nslab
    for i in range(S):
        for s in range(nslab):
            av = step(a_pad, s, i)
            h[s] = av * h[s] + step(b_pad, s, i)
            prod[s] = av * prod[s]
    for s in range(nslab):
        cols = slice(LANES * s, LANES * (s + 1))
        pcum, hcum = _scan_within_8(prod[s], h[s])
        cin = hcar[:, cols]
        ends = hcum + pcum * cin
        h[s] = jnp.where(ridx == 0, cin, pltpu.roll(ends, 1, axis=0))
        hcar[:, cols] = jnp.broadcast_to(ends[SUBLANES - 1:SUBLANES, :], (SUBLANES, LANES))
    for i in range(S):
        for s in range(nslab):
            h[s] = step(a_pad, s, i) * h[s] + step(b_pad, s, i)
            h_pad[s, pl.ds(i, SUBLANES, stride=SCAN_PITCH), :] = h[s]
    return jnp.concatenate(
        [jnp.concatenate([h_pad[s, SCAN_PITCH * j:SCAN_PITCH * j + S, :] for j in range(SUBLANES)], axis=0)
         for s in range(nslab)], axis=1)


def _ssd_cumdecay(dt, a2_row, tri):
    cum2 = jnp.dot(tri, dt * a2_row, precision=HI, preferred_element_type=F32)
    return cum2, cum2.T[0:N_SSD_HEADS, :]


def _ssd_diag(xs, bm, cm, dt, cols, cum2_t, mask_add):
    L = xs.shape[0]
    c2_t = cum2_t - jnp.log2(dt.T[0:N_SSD_HEADS, :])
    lane = lax.broadcasted_iota(jnp.int32, (L, LANES), 1)
    lo = lane < SSD_HEAD_DIM
    y_parts = []
    for g in range(N_SSD_GROUPS):
        bg = bm[:, D_STATE * g:D_STATE * (g + 1)].astype(BF16)
        cg = cm[:, D_STATE * g:D_STATE * (g + 1)].astype(BF16)
        cb = lax.dot_general(cg, bg, (((1,), (1,)), ((), ())), preferred_element_type=F32)
        for jj in range(N_SSD_HEADS // N_SSD_GROUPS // 2):
            j = (N_SSD_HEADS // N_SSD_GROUPS // 2) * g + jj
            h0, h1 = 2 * j, 2 * j + 1
            col0 = cols[:, LANES * h0:LANES * (h0 + 1)]
            col1 = cols[:, LANES * h1:LANES * (h1 + 1)]
            m0 = cb * jnp.exp2(col0 - c2_t[h0:h0 + 1, :] + mask_add)
            m1 = cb * jnp.exp2(col1 - c2_t[h1:h1 + 1, :] + mask_add)
            lhs = jnp.concatenate([m0, m1], axis=1).astype(BF16)
            xp = xs[:, LANES * j:LANES * (j + 1)]
            rhs = jnp.concatenate([jnp.where(lo, xp, 0.0), jnp.where(lo, 0.0, xp)], axis=0).astype(BF16)
            y_parts.append(jnp.dot(lhs, rhs, preferred_element_type=F32))
    return jnp.concatenate(y_parts, axis=1)


def _spread(v, sel_ref):
    p0 = v.astype(BF16)
    p1 = (v - p0.astype(F32)).astype(BF16)
    return jnp.dot(jnp.concatenate([p0, p1], axis=1), sel_ref[...], preferred_element_type=F32)


def _ssd_gate_norm(ys, xs, z_act, dskip, g_ssd):
    ys = ys + dskip * xs
    gated = ys * z_act
    half = SSD_WIDTH // N_SSD_GROUPS
    outs = []
    for g in range(N_SSD_GROUPS):
        outs.append(_rms(gated[:, half * g:half * (g + 1)], g_ssd[:, half * g:half * (g + 1)]))
    return jnp.concatenate(outs, axis=1)


def _split_w_in_kernel(wt_ref, wdt_t_ref, main_ref, dt_ref):
    main_ref[...] = wt_ref[...].T.astype(BF16)

    @pl.when(pl.program_id(0) == 0)
    def _dt():
        dt_ref[...] = jnp.zeros_like(dt_ref)
        dt_ref[:, 0:N_SSD_HEADS] = wdt_t_ref[...].T.astype(BF16)


def _split_w_in(w_t):
    cols, rows = w_t.shape
    blk = SPLIT_COLS
    return pl.pallas_call(
        _split_w_in_kernel,
        out_shape=(jax.ShapeDtypeStruct((rows, PROJ_MAIN), BF16), jax.ShapeDtypeStruct((rows, DT_PAD), BF16)),
        grid=(PROJ_MAIN // blk,),
        in_specs=[pl.BlockSpec((blk, rows), lambda j: (j, 0)),
                  pl.BlockSpec((N_SSD_HEADS, rows), lambda j: (PROJ_MAIN // N_SSD_HEADS, 0))],
        out_specs=(pl.BlockSpec((rows, blk), lambda j: (0, j)), pl.BlockSpec((rows, DT_PAD), lambda j: (0, 0))),
        compiler_params=pltpu.CompilerParams(dimension_semantics=("arbitrary",)),
        name="split_w_in",
    )(w_t, w_t)


def _inproj_kernel(x_ref, xs_ref, g_ref, w_ref, wdt_ref, lcw_ref, lcb_ref, scw_ref, scb_ref, dtb_ref,
                   o_ref, olc_ref, osc_ref, os_ref, ext_l, ext_s, *, steps_per_seq):
    last_step = pl.program_id(0) == pl.num_programs(0) - 1
    t = lax.rem(pl.program_id(0), steps_per_seq)
    rows = ROW_TILE
    hist = SUBLANES
    o1, o2, o3 = LRU_WIDTH, 2 * LRU_WIDTH, 2 * LRU_WIDTH + SSD_WIDTH

    @pl.when(last_step)
    def _sample():
        hn = _rms(xs_ref[...].reshape(-1, D_MODEL), g_ref[...]).astype(BF16)
        os_ref[:, 0:PROJ_MAIN] = jnp.dot(hn, w_ref[...], preferred_element_type=F32)
        os_ref[:, PROJ_MAIN:PROJ_PAD] = jnp.dot(hn, wdt_ref[...], preferred_element_type=F32)

    @pl.when(jnp.logical_not(last_step))
    def _prompt():
        @pl.when(t == 0)
        def _init():
            ext_l[:, 0:hist, :] = jnp.zeros((ext_l.shape[0], hist, LANES), F32)
            ext_s[:, 0:hist, :] = jnp.zeros((ext_s.shape[0], hist, LANES), F32)

        hn = _rms(x_ref[...], g_ref[...]).astype(BF16)
        lx = jnp.dot(hn, w_ref[:, 0:o1], preferred_element_type=F32)
        for s in range(ext_l.shape[0]):
            ext_l[s, hist:hist + rows, :] = lx[:, LANES * s:LANES * (s + 1)]
        xbc_in = jnp.dot(hn, w_ref[:, o3:PROJ_MAIN], preferred_element_type=F32)
        for s in range(ext_s.shape[0]):
            ext_s[s, hist:hist + rows, :] = xbc_in[:, LANES * s:LANES * (s + 1)]
        o_ref[:, o1:o2] = _gelu_tanh(jnp.dot(hn, w_ref[:, o1:o2], preferred_element_type=F32))
        z = jnp.dot(hn, w_ref[:, o2:o3], preferred_element_type=F32)
        o_ref[:, o2:o3] = _silu(z)
        o_ref[:, PROJ_MAIN:PROJ_PAD] = _softplus(
            jnp.dot(hn, wdt_ref[...], preferred_element_type=F32) + dtb_ref[...])
        o_ref[:, 0:o1] = _conv_slabs(ext_l, lcw_ref, lcb_ref, rows, hist - (CONV_WIDTH - 1))
        xbc = _conv_slabs(ext_s, scw_ref, scb_ref, rows, hist - (CONV_WIDTH - 1))
        o_ref[:, o3:PROJ_MAIN] = _silu(xbc)

        @pl.when(t == steps_per_seq - 1)
        def _final():
            last = slice(hist + rows - (CONV_WIDTH - 1), hist + rows)
            for s in range(ext_l.shape[0]):
                olc_ref[:, LANES * s:LANES * (s + 1)] = ext_l[s, last, :]
            for s in range(ext_s.shape[0]):
                osc_ref[:, LANES * s:LANES * (s + 1)] = ext_s[s, last, :]

        tail_l = ext_l[:, rows:rows + hist, :]
        tail_s = ext_s[:, rows:rows + hist, :]
        ext_l[:, 0:hist, :] = tail_l
        ext_s[:, 0:hist, :] = tail_s


def _in_proj(x2d, bsz, xs, g_mix, w_main, w_dt, lcw, lcb, scw, scb, dtb):
    n = x2d.shape[0]
    steps_p = n // ROW_TILE
    steps_per_seq = steps_p // bsz
    assert xs.shape[0] * xs.shape[1] == ROW_TILE
    const = lambda i: (0, 0)
    tile = lambda i: (jnp.minimum(i, steps_p - 1), 0)
    seq_of = lambda i: (jnp.minimum(i, steps_p - 1) // steps_per_seq, 0, 0)
    once = dict(pipeline_mode=pl.Buffered(1))
    return pl.pallas_call(
        functools.partial(_inproj_kernel, steps_per_seq=steps_per_seq),
        out_shape=(
            jax.ShapeDtypeStruct((n, PROJ_PAD), F32),
            jax.ShapeDtypeStruct((bsz, CONV_WIDTH - 1, LRU_WIDTH), F32),
            jax.ShapeDtypeStruct((bsz, CONV_WIDTH - 1, SSD_CONV_DIM), F32),
            jax.ShapeDtypeStruct((ROW_TILE, PROJ_PAD), F32),
        ),
        grid=(steps_p + 1,),
        in_specs=[
            pl.BlockSpec((ROW_TILE, D_MODEL), tile),
            pl.BlockSpec(xs.shape, lambda i: (0, 0, 0), **once),
            pl.BlockSpec((1, D_MODEL), const),
            pl.BlockSpec((D_MODEL, PROJ_MAIN), const, **once),
            pl.BlockSpec((D_MODEL, DT_PAD), const, **once),
            pl.BlockSpec((CONV_WIDTH, LRU_WIDTH), const),
            pl.BlockSpec((1, LRU_WIDTH), const),
            pl.BlockSpec((CONV_WIDTH, SSD_CONV_DIM), const),
            pl.BlockSpec((1, SSD_CONV_DIM), const),
            pl.BlockSpec((1, DT_PAD), const),
        ],
        out_specs=(
            pl.BlockSpec((ROW_TILE, PROJ_PAD), tile),
            pl.BlockSpec((None, CONV_WIDTH - 1, LRU_WIDTH), seq_of),
            pl.BlockSpec((None, CONV_WIDTH - 1, SSD_CONV_DIM), seq_of),
            pl.BlockSpec((ROW_TILE, PROJ_PAD), const, **once),
        ),
        scratch_shapes=[
            pltpu.VMEM((LRU_WIDTH // LANES, SUBLANES + ROW_TILE, LANES), F32),
            pltpu.VMEM((SSD_CONV_DIM // LANES, SUBLANES + ROW_TILE, LANES), F32),
        ],
        compiler_params=pltpu.CompilerParams(
            dimension_semantics=("arbitrary",), vmem_limit_bytes=VMEM_LIMIT),
        name="in_proj",
    )(x2d, xs, g_mix, w_main, w_dt, lcw, lcb, scw, scb, dtb)


def _outmlp_kernel(xp_ref, yp_ref, xs_ref, ys_ref, wo_ref, gm_ref, wu_ref, wd_ref, gf_ref, op_ref, os_ref):
    last = pl.program_id(0) == pl.num_programs(0) - 1

    def run(x_ref, y_ref, o_ref):
        x = x_ref[...].reshape(-1, D_MODEL)
        x1 = x + jnp.dot(y_ref[...].astype(BF16), wo_ref[...], preferred_element_type=F32)
        m = _rms(x1, gm_ref[...]).astype(BF16)
        u = jnp.dot(m, wu_ref[...], preferred_element_type=F32)
        u = jnp.square(jnp.maximum(u, 0.0)).astype(BF16)
        x2 = x1 + jnp.dot(u, wd_ref[...], preferred_element_type=F32)
        o_ref[...] = _rms(x2, gf_ref[...]).reshape(o_ref.shape)

    @pl.when(jnp.logical_not(last))
    def _prompt():
        run(xp_ref, yp_ref, op_ref)

    @pl.when(last)
    def _sample():
        run(xs_ref, ys_ref, os_ref)


def _out_mlp(xp, ymix_p, xs, ymix_s, w_out_b, g_mlp, w_up_b, w_down_b, g_final):
    n_p = xp.shape[0]
    steps_p = n_p // ROW_TILE
    assert ymix_s.shape[0] == ROW_TILE
    const = lambda i: (0, 0)
    tile = lambda i: (jnp.minimum(i, steps_p - 1), 0)
    once = dict(pipeline_mode=pl.Buffered(1))
    return pl.pallas_call(
        _outmlp_kernel,
        out_shape=(jax.ShapeDtypeStruct((n_p, D_MODEL), F32), jax.ShapeDtypeStruct(xs.shape, F32)),
        grid=(steps_p + 1,),
        in_specs=[
            pl.BlockSpec((ROW_TILE, D_MODEL), tile),
            pl.BlockSpec((ROW_TILE, MIX_WIDTH), tile),
            pl.BlockSpec(xs.shape, lambda i: (0, 0, 0), **once),
            pl.BlockSpec((ROW_TILE, MIX_WIDTH), const, **once),
            pl.BlockSpec((MIX_WIDTH, D_MODEL), const, **once),
            pl.BlockSpec((1, D_MODEL), const),
            pl.BlockSpec((D_MODEL, D_FF), const, **once),
            pl.BlockSpec((D_FF, D_MODEL), const, **once),
            pl.BlockSpec((1, D_MODEL), const),
        ],
        out_specs=(pl.BlockSpec((ROW_TILE, D_MODEL), tile), pl.BlockSpec(xs.shape, lambda i: (0, 0, 0))),
        compiler_params=pltpu.CompilerParams(
            dimension_semantics=("arbitrary",), vmem_limit_bytes=VMEM_LIMIT_OUT),
        name="out_mlp",
    )(xp, ymix_p, xs, ymix_s, w_out_b, g_mlp, w_up_b, w_down_b, g_final)


def _mixer_prompt_kernel(u_ref, gl_ref, zact_ref, xbc_ref, dt_ref,
                         wg_ref, ba_ref, bx_ref, lam_ref, glru_ref, alog_ref, dskip_ref, gssd_ref,
                         selt_ref, selp_ref, wo_ref, wu_ref, wd_ref,
                         y_ref, olh_ref, osh_ref, wo_b_ref, wu_b_ref, wd_b_ref,
                         a_pad, b_pad, h_pad, hcar, ht):
    t = pl.program_id(1)
    nt = pl.num_programs(1)
    tc = PROMPT_TC
    wo_b_ref[...] = wo_ref[...].astype(BF16)
    wu_b_ref[...] = wu_ref[...].astype(BF16)
    wd_b_ref[...] = wd_ref[...].astype(BF16)

    @pl.when(t == 0)
    def _init():
        hcar[...] = jnp.zeros_like(hcar)
        ht[...] = jnp.zeros_like(ht)

    neg_c_sp = (-LRU_C) * _softplus(-lam_ref[...])
    lane1 = lax.broadcasted_iota(jnp.int32, (1, LANES), 1)
    a2_row = jnp.where(lane1 < N_SSD_HEADS, -LOG2E * jnp.exp(alog_ref[...]), 0.0)
    L = SSD_CHUNK
    rr = lax.broadcasted_iota(jnp.int32, (L, L), 0)
    cc = lax.broadcasted_iota(jnp.int32, (L, L), 1)
    causal = cc <= rr
    tri = jnp.where(causal, 1.0, 0.0).astype(F32)
    mask_add = jnp.where(causal, 0.0, NEG_BIG).astype(F32)
    half = SSD_WIDTH // N_SSD_GROUPS

    chunks = [(n, c) for n in range(PROMPT_NB) for c in range(tc // L)]
    cums = [_ssd_cumdecay(dt_ref[n, L * c:L * (c + 1), :], a2_row, tri) for n, c in chunks]
    dts = [dt_ref[n, L * c:L * (c + 1), :] for n, c in chunks]
    cum_all = jnp.concatenate([cum2 for cum2, _ in cums], axis=0)
    cols_all = _spread(cum_all, selt_ref)
    ecol_all = _spread(jnp.exp2(cum_all), selp_ref)
    sdt_all = _spread(jnp.concatenate(
        [jnp.exp2(cum2[L - 1:L, :] - cum2) * dt for (cum2, _), dt in zip(cums, dts)], axis=0), selp_ref)

    for n in range(PROMPT_NB):
        a, b = _lru_coeffs(u_ref[n], wg_ref, ba_ref[...], bx_ref[...], neg_c_sp)
        hseq = _lru_scan_strided(a, b, hcar.at[n], a_pad.at[n], b_pad.at[n], h_pad.at[n])
        y_ref[n, :, 0:LRU_WIDTH] = _rms(hseq * gl_ref[n], glru_ref[...])

        for c in range(tc // L):
            k = chunks.index((n, c))
            rows = slice(L * c, L * (c + 1))
            krows = slice(L * k, L * (k + 1))
            xs = xbc_ref[n, rows, 0:SSD_WIDTH]
            bm = xbc_ref[n, rows, SSD_WIDTH:SSD_WIDTH + N_SSD_GROUPS * D_STATE]
            cm = xbc_ref[n, rows, SSD_WIDTH + N_SSD_GROUPS * D_STATE:SSD_CONV_DIM]
            y_diag = _ssd_diag(xs, bm, cm, dts[k], cols_all[krows, :], cums[k][1], mask_add)
            ecol = ecol_all[krows, :]
            xw = xs * sdt_all[krows, :]
            dec = ecol[L - 1:L, :]
            y_off_parts = []
            for g in range(N_SSD_GROUPS):
                htg = ht[n, g]
                cg = cm[:, D_STATE * g:D_STATE * (g + 1)].astype(BF16)
                y_off_parts.append(jnp.dot(cg, htg.astype(BF16), preferred_element_type=F32))
                bg_t = bm[:, D_STATE * g:D_STATE * (g + 1)].T.astype(BF16)
                st = jnp.dot(bg_t, xw[:, half * g:half * (g + 1)].astype(BF16), preferred_element_type=F32)
                ht[n, g] = htg * dec[:, half * g:half * (g + 1)] + st
            ys = y_diag + jnp.concatenate(y_off_parts, axis=1) * ecol
            y_ref[n, rows, LRU_WIDTH:MIX_WIDTH] = _ssd_gate_norm(
                ys, xs, zact_ref[n, rows, :], dskip_ref[...], gssd_ref[...])

    @pl.when(t == nt - 1)
    def _final():
        for n in range(PROMPT_NB):
            olh_ref[n] = hcar[n, 0:1, :]
            for g in range(N_SSD_GROUPS):
                osh_ref[n, half * g:half * (g + 1), :] = ht[n, g].T


def _param_specs(const):
    return [
        pl.BlockSpec((CONV_WIDTH, LRU_WIDTH), const),
        pl.BlockSpec((1, LRU_WIDTH), const),
        pl.BlockSpec((LRU_WIDTH // MXU_DIM, MXU_DIM, 2 * MXU_DIM), lambda *_: (0, 0, 0)),
        pl.BlockSpec((1, LRU_WIDTH), const),
        pl.BlockSpec((1, LRU_WIDTH), const),
        pl.BlockSpec((1, LRU_WIDTH), const),
        pl.BlockSpec((1, LRU_WIDTH), const),
        pl.BlockSpec((CONV_WIDTH, SSD_CONV_DIM), const),
        pl.BlockSpec((1, SSD_CONV_DIM), const),
        pl.BlockSpec((1, DT_PAD), const),
        pl.BlockSpec((1, DT_PAD), const),
        pl.BlockSpec((1, SSD_WIDTH), const),
        pl.BlockSpec((1, SSD_WIDTH), const),
    ]


def _head_selectors():
    k = np.arange(2 * LANES)[:, None] % LANES
    sel_t = (k == np.arange(N_SSD_HEADS * LANES)[None, :] // LANES).astype(np.float32)
    sel_p = (k == np.arange(SSD_WIDTH)[None, :] // SSD_HEAD_DIM).astype(np.float32)
    return jnp.asarray(sel_t, BF16), jnp.asarray(sel_p, BF16)


def _mixer_prompt(act, wg, b_a, b_x, lam, g_lru, a_log, d_skip, g_ssd, sel_t, sel_p, w_out, w_up, w_down):
    bsz, seq, _ = act.shape
    tc = PROMPT_TC
    nb = PROMPT_NB
    steps = (bsz // nb) * (seq // tc)
    const = lambda b, t: (0, 0)
    w_slice = lambda b, t: (b * (seq // tc) + t, 0)
    assert all(w.shape[0] % (steps * 2 * SUBLANES) == 0 for w in (w_out, w_up, w_down))
    w_specs = [pl.BlockSpec((w.shape[0] // steps, w.shape[1]), w_slice) for w in (w_out, w_up, w_down)]
    in_specs = [
        pl.BlockSpec((nb, tc, LRU_WIDTH), lambda b, t: (b, t, 0)),
        pl.BlockSpec((nb, tc, LRU_WIDTH), lambda b, t: (b, t, 1)),
        pl.BlockSpec((nb, tc, SSD_WIDTH), lambda b, t: (b, t, 2)),
        pl.BlockSpec((nb, tc, SSD_CONV_DIM), lambda b, t: (b, t, 2)),
        pl.BlockSpec((nb, tc, DT_PAD), lambda b, t: (b, t, PROJ_MAIN // DT_PAD)),
        pl.BlockSpec((LRU_WIDTH // MXU_DIM, MXU_DIM, 2 * MXU_DIM), lambda b, t: (0, 0, 0)),
        pl.BlockSpec((1, LRU_WIDTH), const),
        pl.BlockSpec((1, LRU_WIDTH), const),
        pl.BlockSpec((1, LRU_WIDTH), const),
        pl.BlockSpec((1, LRU_WIDTH), const),
        pl.BlockSpec((1, DT_PAD), const),
        pl.BlockSpec((1, SSD_WIDTH), const),
        pl.BlockSpec((1, SSD_WIDTH), const),
        pl.BlockSpec((2 * LANES, N_SSD_HEADS * LANES), const),
        pl.BlockSpec((2 * LANES, SSD_WIDTH), const),
    ] + w_specs
    out_shape = (
        jax.ShapeDtypeStruct((bsz, seq, MIX_WIDTH), F32),
        jax.ShapeDtypeStruct((bsz, 1, LRU_WIDTH), F32),
        jax.ShapeDtypeStruct((bsz, SSD_WIDTH, D_STATE), F32),
    ) + tuple(jax.ShapeDtypeStruct(w.shape, BF16) for w in (w_out, w_up, w_down))
    out_specs = (
        pl.BlockSpec((nb, tc, MIX_WIDTH), lambda b, t: (b, t, 0)),
        pl.BlockSpec((nb, 1, LRU_WIDTH), lambda b, t: (b, 0, 0)),
        pl.BlockSpec((nb, SSD_WIDTH, D_STATE), lambda b, t: (b, 0, 0)),
    ) + tuple(w_specs)
    scratch = [
        pltpu.VMEM((nb, LRU_WIDTH // LANES, SUBLANES * SCAN_PITCH, LANES), F32),
        pltpu.VMEM((nb, LRU_WIDTH // LANES, SUBLANES * SCAN_PITCH, LANES), F32),
        pltpu.VMEM((nb, LRU_WIDTH // LANES, SUBLANES * SCAN_PITCH, LANES), F32),
        pltpu.VMEM((nb, SUBLANES, LRU_WIDTH), F32),
        pltpu.VMEM((nb, N_SSD_GROUPS, D_STATE, SSD_WIDTH // N_SSD_GROUPS), F32),
    ]
    return pl.pallas_call(
        _mixer_prompt_kernel,
        out_shape=out_shape,
        grid=(bsz // nb, seq // tc),
        in_specs=in_specs,
        out_specs=out_specs,
        scratch_shapes=scratch,
        compiler_params=pltpu.CompilerParams(
            dimension_semantics=("parallel", "arbitrary"), vmem_limit_bytes=VMEM_LIMIT),
        name="mixer_prompt",
    )(act, act, act, act, act, wg, b_a, b_x, lam, g_lru, a_log, d_skip, g_ssd, sel_t, sel_p,
      w_out, w_up, w_down)


def _mixer_sample_kernel(lx_ref, gate_ref, z_ref, xbc_ref, dt_ref,
                         slc_ref, slh_ref, ssc_ref, ssh_ref,
                         lcw_ref, lcb_ref, wg_ref, ba_ref, bx_ref, lam_ref, glru_ref,
                         scw_ref, scb_ref, dtb_ref, alog_ref, dskip_ref, gssd_ref, selt_ref, selp_ref,
                         y_ref, olc_ref, olh_ref, osc_ref, osh_ref,
                         ext_l, ext_s, pad_scr, yoff_scr, *, T):
    S = SAMPLE_SEQS
    P = SUBLANES
    K1 = CONV_WIDTH - 1
    R = S * P
    row_i = lax.broadcasted_iota(jnp.int32, (R, 1), 0) & (P - 1)
    valid = row_i < T

    def pad_rows(ref):
        width = ref.shape[-1]
        pad_scr[:, :, 0:width] = jnp.zeros((S, P, width), F32)
        pad_scr[:, 0:T, 0:width] = ref[...].reshape(S, T, width)
        return pad_scr[:, :, 0:width].reshape(R, width)

    ext_l[...] = jnp.zeros_like(ext_l)
    ext_s[...] = jnp.zeros_like(ext_s)
    for k in range(K1):
        ext_l[:, k, :] = slc_ref[k]
    ext_l[:, K1:K1 + T, :] = lx_ref[...].reshape(S, T, LRU_WIDTH)
    for k in range(K1):
        ext_s[:, k, :] = ssc_ref[k]
    ext_s[:, K1:K1 + T, :] = xbc_ref[...].reshape(S, T, SSD_CONV_DIM)
    for k in range(K1):
        olc_ref[k] = ext_l[:, T + k, :]
        osc_ref[k] = ext_s[:, T + k, :]

    el = ext_l[...].reshape(R, LRU_WIDTH)
    es = ext_s[...].reshape(R, SSD_CONV_DIM)

    def conv(e, w_ref, b_ref):
        out = b_ref[...] + e * w_ref[0:1, :]
        for k in range(1, CONV_WIDTH):
            out = out + pltpu.roll(e, R - k, axis=0) * w_ref[k:k + 1, :]
        return out

    u = conv(el, lcw_ref, lcb_ref)
    neg_c_sp = (-LRU_C) * _softplus(-lam_ref[...])
    a, b = _lru_coeffs(u, wg_ref, ba_ref[...], bx_ref[...], neg_c_sp)
    a, b = _scan_within_8(a, b)
    h0 = jnp.broadcast_to(slh_ref[...][:, None, :], (S, P, LRU_WIDTH)).reshape(R, LRU_WIDTH)
    hseq = a * h0 + b
    olh_ref[...] = hseq.reshape(S, P, LRU_WIDTH)[:, T - 1, :]
    gate = pad_rows(gate_ref)
    y_lru = _rms(hseq * _gelu_tanh(gate), glru_ref[...])

    xbc = conv(es, scw_ref, scb_ref)
    xbc = _silu(xbc)
    xs = xbc[:, 0:SSD_WIDTH]
    bm = xbc[:, SSD_WIDTH:SSD_WIDTH + N_SSD_GROUPS * D_STATE]
    cm = xbc[:, SSD_WIDTH + N_SSD_GROUPS * D_STATE:]
    dt_raw = pad_rows(dt_ref)
    dt = jnp.where(valid, _softplus(dt_raw + dtb_ref[...]), 0.0)
    lane1 = lax.broadcasted_iota(jnp.int32, (1, LANES), 1)
    a2_row = jnp.where(lane1 < N_SSD_HEADS, -LOG2E * jnp.exp(alog_ref[...]), 0.0)

    rr = lax.broadcasted_iota(jnp.int32, (R, R), 0)
    cc = lax.broadcasted_iota(jnp.int32, (R, R), 1)
    allowed = (cc <= rr) & ((rr - cc) <= (rr & (P - 1)))
    tri = jnp.where(allowed, 1.0, 0.0).astype(F32)
    mask_add = jnp.where(allowed, 0.0, NEG_BIG).astype(F32)

    cum2, cum2_t = _ssd_cumdecay(dt, a2_row, tri)
    y_diag = _ssd_diag(xs, bm, cm, dt, _spread(cum2, selt_ref), cum2_t, mask_add)
    ecol = _spread(jnp.exp2(cum2), selp_ref)
    end2 = jnp.broadcast_to(cum2.reshape(S, P, LANES)[:, P - 1:P, :], (S, P, LANES)).reshape(R, LANES)
    xw = xs * _spread(jnp.exp2(end2 - cum2) * dt, selp_ref)
    ecum_t = jnp.exp2(cum2_t)

    half = SSD_WIDTH // N_SSD_GROUPS
    for q in range(S):
        r0 = P * q
        vq = jnp.broadcast_to(ecum_t[:, r0 + P - 1:r0 + P], (N_SSD_HEADS, LANES))
        for g in range(N_SSD_GROUPS):
            hqg = ssh_ref[q, half * g:half * (g + 1), :]
            cq = cm[r0:r0 + P, D_STATE * g:D_STATE * (g + 1)].astype(BF16)
            yoff_scr[r0:r0 + P, half * g:half * (g + 1)] = lax.dot_general(
                cq, hqg.astype(BF16), (((1,), (1,)), ((), ())), preferred_element_type=F32)
            bq = bm[r0:r0 + P, D_STATE * g:D_STATE * (g + 1)].astype(BF16)
            xq = xw[r0:r0 + P, half * g:half * (g + 1)].astype(BF16)
            st = lax.dot_general(xq, bq, (((0,), (0,)), ((), ())), preferred_element_type=F32)
            for e in range(N_SSD_HEADS // N_SSD_GROUPS):
                h = (N_SSD_HEADS // N_SSD_GROUPS) * g + e
                lo_r = SSD_HEAD_DIM * e
                osh_ref[q, SSD_HEAD_DIM * h:SSD_HEAD_DIM * (h + 1), :] = (
                    vq[h:h + 1, :] * hqg[lo_r:lo_r + SSD_HEAD_DIM, :] + st[lo_r:lo_r + SSD_HEAD_DIM, :])

    ys = y_diag + yoff_scr[...] * ecol
    z = pad_rows(z_ref)
    y_ssd = _ssd_gate_norm(ys, xs, _silu(z), dskip_ref[...], gssd_ref[...])
    y_ref[:, 0:LRU_WIDTH] = y_lru.reshape(S, P, LRU_WIDTH)[:, 0:T, :].reshape(S * T, LRU_WIDTH)
    y_ref[:, LRU_WIDTH:MIX_WIDTH] = y_ssd.reshape(S, P, SSD_WIDTH)[:, 0:T, :].reshape(S * T, SSD_WIDTH)


def _mixer_sample(proj, T, st_lc, st_lh, st_sc, st_sh, params, sel_t, sel_p):
    nseq = proj.shape[0] // T
    S = SAMPLE_SEQS
    const = lambda i: (0, 0)
    in_specs = [
        pl.BlockSpec((S * T, LRU_WIDTH), lambda i: (i, 0)),
        pl.BlockSpec((S * T, LRU_WIDTH), lambda i: (i, 1)),
        pl.BlockSpec((S * T, SSD_WIDTH), lambda i: (i, 2)),
        pl.BlockSpec((S * T, SSD_CONV_DIM), lambda i: (i, 2)),
        pl.BlockSpec((S * T, DT_PAD), lambda i: (i, PROJ_MAIN // DT_PAD)),
        pl.BlockSpec((CONV_WIDTH - 1, S, LRU_WIDTH), lambda i: (0, i, 0)),
        pl.BlockSpec((S, LRU_WIDTH), lambda i: (i, 0)),
        pl.BlockSpec((CONV_WIDTH - 1, S, SSD_CONV_DIM), lambda i: (0, i, 0)),
        pl.BlockSpec((S, SSD_WIDTH, D_STATE), lambda i: (i, 0, 0)),
    ] + _param_specs(const) + [
        pl.BlockSpec((2 * LANES, N_SSD_HEADS * LANES), const),
        pl.BlockSpec((2 * LANES, SSD_WIDTH), const),
    ]
    out_shape = (
        jax.ShapeDtypeStruct((nseq * T, MIX_WIDTH), F32),
        jax.ShapeDtypeStruct((CONV_WIDTH - 1, nseq, LRU_WIDTH), F32),
        jax.ShapeDtypeStruct((nseq, LRU_WIDTH), F32),
        jax.ShapeDtypeStruct((CONV_WIDTH - 1, nseq, SSD_CONV_DIM), F32),
        jax.ShapeDtypeStruct((nseq, SSD_WIDTH, D_STATE), F32),
    )
    out_specs = (
        pl.BlockSpec((S * T, MIX_WIDTH), lambda i: (i, 0)),
        pl.BlockSpec((CONV_WIDTH - 1, S, LRU_WIDTH), lambda i: (0, i, 0)),
        pl.BlockSpec((S, LRU_WIDTH), lambda i: (i, 0)),
        pl.BlockSpec((CONV_WIDTH - 1, S, SSD_CONV_DIM), lambda i: (0, i, 0)),
        pl.BlockSpec((S, SSD_WIDTH, D_STATE), lambda i: (i, 0, 0)),
    )
    scratch = [
        pltpu.VMEM((S, SUBLANES, LRU_WIDTH), F32),
        pltpu.VMEM((S, SUBLANES, SSD_CONV_DIM), F32),
        pltpu.VMEM((S, SUBLANES, LRU_WIDTH), F32),
        pltpu.VMEM((S * SUBLANES, SSD_WIDTH), F32),
    ]
    return pl.pallas_call(
        functools.partial(_mixer_sample_kernel, T=T),
        out_shape=out_shape,
        grid=(nseq // S,),
        in_specs=in_specs,
        out_specs=out_specs,
        scratch_shapes=scratch,
        compiler_params=pltpu.CompilerParams(
            dimension_semantics=("parallel",), vmem_limit_bytes=VMEM_LIMIT),
        name="mixer_sample",
    )(proj, proj, proj, proj, proj, st_lc, st_lh, st_sc, st_sh, *params, sel_t, sel_p)


def _gate_weights(w_a, w_x):
    def tiles(w):
        per = MXU_DIM // LRU_BLOCK
        w4 = w.reshape(N_LRU_HEADS // per, per, LRU_BLOCK, LRU_BLOCK)
        eye = 0.5 * jnp.eye(per, dtype=w.dtype)
        t = jnp.einsum('jaik,ab->jaibk', w4, eye)
        return t.reshape(N_LRU_HEADS // per, MXU_DIM, MXU_DIM)
    return jnp.concatenate([tiles(w_a), tiles(w_x)], axis=2).astype(BF16)


def kernel(x_prompt, x_sample, state_lru_conv, state_lru_h, state_ssd_conv, state_ssd_h, g_mix, w_in,
           lru_conv_w, lru_conv_b, w_a, b_a, w_x, b_x, lam, g_lru_out, ssd_conv_w, ssd_conv_b, dt_bias,
           a_log, d_skip, g_ssd_out, w_out, g_mlp, w_up, w_down, g_final):
    depth = w_in.shape[0]
    assert depth == 1
    bp, seq, _ = x_prompt.shape
    bs, dseq, _ = x_sample.shape
    l = 0
    row = lambda v: v.reshape(1, -1)
    w_main, w_dt = _split_w_in(jnp.swapaxes(w_in, 1, 2)[l])
    params = (
        lru_conv_w[l], row(lru_conv_b[l]), _gate_weights(w_a[l], w_x[l]),
        row(b_a[l]), row(b_x[l]), row(lam[l]), row(g_lru_out[l]),
        ssd_conv_w[l], row(ssd_conv_b[l]),
        jnp.pad(row(dt_bias[l]), ((0, 0), (0, DT_PAD - N_SSD_HEADS))),
        jnp.pad(row(a_log[l]), ((0, 0), (0, DT_PAD - N_SSD_HEADS))),
        row(jnp.repeat(d_skip[l], SSD_HEAD_DIM)), row(g_ssd_out[l]),
    )
    gmix = row(g_mix[l])
    gmlp = row(g_mlp[l])
    gfin = row(g_final)

    xp2 = x_prompt.reshape(bp * seq, D_MODEL)
    (lcw, lcb, wg, ba, bx, lam_r, glru, scw, scb, dtb, alog, dskip, gssd) = params
    act_p, p_lc, p_sc, proj_s = _in_proj(xp2, bp, x_sample, gmix, w_main, w_dt, lcw, lcb, scw, scb, dtb)
    sel_t, sel_p = _head_selectors()
    ymix_p, p_lh, p_sh, w_out_b, w_up_b, w_down_b = _mixer_prompt(
        act_p.reshape(bp, seq, PROJ_PAD), wg, ba, bx, lam_r, glru, alog, dskip, gssd, sel_t, sel_p,
        w_out[l], w_up[l], w_down[l])

    ymix_s, s_lc, s_lh, s_sc, s_sh = _mixer_sample(
        proj_s, dseq, jnp.swapaxes(state_lru_conv[l], 0, 1), state_lru_h[l],
        jnp.swapaxes(state_ssd_conv[l], 0, 1),
        state_ssd_h[l].reshape(bs, SSD_WIDTH, D_STATE), params, sel_t, sel_p)
    y_prompt, y_sample = _out_mlp(xp2, ymix_p.reshape(bp * seq, MIX_WIDTH), x_sample, ymix_s,
                                  w_out_b, gmlp, w_up_b, w_down_b, gfin)

    hshape = (N_SSD_HEADS, SSD_HEAD_DIM, D_STATE)
    return (
        y_prompt.reshape(bp, seq, D_MODEL), y_sample,
        p_lc[None], p_lh.reshape(1, bp, LRU_WIDTH), p_sc[None], p_sh.reshape(1, bp, *hshape),
        jnp.swapaxes(s_lc, 0, 1)[None], s_lh[None], jnp.swapaxes(s_sc, 0, 1)[None],
        s_sh.reshape(1, bs, *hshape),
    )
```

```python
import functools
import math

import jax
import jax.numpy as jnp
import numpy as np
from jax import lax
from jax.experimental import pallas as pl
from jax.experimental.pallas import tpu as pltpu

F32 = jnp.float32
BF16 = jnp.bfloat16

D_MODEL = 1024
LRU_WIDTH = 1024
N_LRU_HEADS = 16
LRU_BLOCK = 64
LRU_C = 8.0
SSD_WIDTH = 1024
SSD_HEAD_DIM = 64
N_SSD_HEADS = 16
N_SSD_GROUPS = 2
D_STATE = 128
CONV_WIDTH = 4
SSD_CONV_DIM = SSD_WIDTH + 2 * N_SSD_GROUPS * D_STATE
D_FF = 4 * D_MODEL
EPS = 1e-6

LANES = 128
SUBLANES = 8
MXU_DIM = 256
DT_PAD = LANES
PROJ_MAIN = 2 * LRU_WIDTH + SSD_WIDTH + SSD_CONV_DIM
PROJ_PAD = PROJ_MAIN + DT_PAD
MIX_WIDTH = LRU_WIDTH + SSD_WIDTH
SSD_CHUNK = 128
PROMPT_TC = 256
PROMPT_NB = 2
ROW_TILE = 512
SPLIT_COLS = 1536
SAMPLE_SEQS = SSD_CHUNK // SUBLANES
SCAN_RUN = PROMPT_TC // SUBLANES
SCAN_PITCH = SCAN_RUN + 4
NEG_BIG = -1e30
LOG2E = 1.4426950408889634
VMEM_LIMIT = 56 * 1024 * 1024
VMEM_LIMIT_OUT = 60 * 1024 * 1024
HI = lax.Precision.HIGHEST


def _rms(x, g):
    ms = jnp.mean(x * x, axis=-1, keepdims=True)
    return x * lax.rsqrt(ms + EPS) * g


def _silu(x):
    h = 0.5 * x
    return h * jnp.tanh(h) + h


def _softplus(x):
    return jnp.maximum(x, 0.0) + jnp.log1p(jnp.exp(-jnp.abs(x)))


def _gelu_tanh(x):
    c = math.sqrt(2.0 / math.pi)
    return 0.5 * x * (1.0 + jnp.tanh(c * (x + 0.044715 * (x * x * x))))


def _lru_coeffs(u, wg_ref, b_a, b_x, neg_c_sp):
    ub = u.astype(BF16)
    r_parts, i_parts = [], []
    for j in range(LRU_WIDTH // MXU_DIM):
        g = jnp.dot(ub[:, MXU_DIM * j:MXU_DIM * (j + 1)], wg_ref[j], preferred_element_type=F32)
        r_parts.append(g[:, :MXU_DIM])
        i_parts.append(g[:, MXU_DIM:])
    t_r = jnp.tanh(jnp.concatenate(r_parts, axis=1) + 0.5 * b_a)
    t_i = jnp.tanh(jnp.concatenate(i_parts, axis=1) + 0.5 * b_x)
    half_sp = 0.5 * neg_c_sp
    log_a = t_r * half_sp + half_sp
    a = jnp.exp(log_a)
    th = jnp.tanh(log_a)
    n = -0.5 * th
    nd = n * (1.0 - th)
    half_mult = jnp.where(nd > 0.0, n * lax.rsqrt(nd), 0.0)
    return a, half_mult * (u * t_i + u)


def _scan_within_8(a, b):
    ridx = lax.broadcasted_iota(jnp.int32, a.shape, 0) & (SUBLANES - 1)
    for k in (1, 2, 4):
        a_s = pltpu.roll(a, k, axis=0)
        b_s = pltpu.roll(b, k, axis=0)
        m = ridx >= k
        b = jnp.where(m, a * b_s + b, b)
        a = jnp.where(m, a * a_s, a)
    return a, b


def _conv_slabs(ext, w_ref, b_ref, rows, first):
    parts = []
    for s in range(ext.shape[0]):
        cols = slice(LANES * s, LANES * (s + 1))
        acc = b_ref[:, cols] + ext[s, pl.ds(first, rows), :] * w_ref[0:1, cols]
        for k in range(1, CONV_WIDTH):
            acc = acc + ext[s, pl.ds(first + k, rows), :] * w_ref[k:k + 1, cols]
        parts.append(acc)
    return jnp.concatenate(parts, axis=1)


def _lru_scan_strided(a, b, hcar, a_pad, b_pad, h_pad):
    rows = a.shape[0]
    S = rows // SUBLANES
    nslab = LRU_WIDTH // LANES
    ridx = lax.broadcasted_iota(jnp.int32, (SUBLANES, LANES), 0)
    step = lambda ref, s, i: ref[s, pl.ds(i, SUBLANES, stride=SCAN_PITCH), :]
    for s in range(nslab):
        cols = slice(LANES * s, LANES * (s + 1))
        for j in range(SUBLANES):
            a_pad[s, SCAN_PITCH * j:SCAN_PITCH * j + S, :] = a[S * j:S * (j + 1), cols]
            b_pad[s, SCAN_PITCH * j:SCAN_PITCH * j + S, :] = b[S * j:S * (j + 1), cols]
    h = [jnp.zeros((SUBLANES, LANES), F32)] * nslab
    prod = [jnp.ones((SUBLANES, LANES), F32)] * nslab
    for i in range(S):
        for s in range(nslab):
            av = step(a_pad, s, i)
            h[s] = av * h[s] + step(b_pad, s, i)
            prod[s] = av * prod[s]
    for s in range(nslab):
        cols = slice(LANES * s, LANES * (s + 1))
        pcum, hcum = _scan_within_8(prod[s], h[s])
        cin = hcar[:, cols]
        ends = hcum + pcum * cin
        h[s] = jnp.where(ridx == 0, cin, pltpu.roll(ends, 1, axis=0))
        hcar[:, cols] = jnp.broadcast_to(ends[SUBLANES - 1:SUBLANES, :], (SUBLANES, LANES))
    for i in range(S):
        for s in range(nslab):
            h[s] = step(a_pad, s, i) * h[s] + step(b_pad, s, i)
            h_pad[s, pl.ds(i, SUBLANES, stride=SCAN_PITCH), :] = h[s]
    return jnp.concatenate(
        [jnp.concatenate([h_pad[s, SCAN_PITCH * j:SCAN_PITCH * j + S, :] for j in range(SUBLANES)], axis=0)
         for s in range(nslab)], axis=1)


def _ssd_cumdecay(dt, a2_row, tri):
    cum2 = jnp.dot(tri, dt * a2_row, precision=HI, preferred_element_type=F32)
    return cum2, cum2.T[0:N_SSD_HEADS, :]


def _ssd_diag(xs, bm, cm, dt, cols, cum2_t, mask_add):
    L = xs.shape[0]
    c2_t = cum2_t - jnp.log2(dt.T[0:N_SSD_HEADS, :])
    lane = lax.broadcasted_iota(jnp.int32, (L, LANES), 1)
    lo = lane < SSD_HEAD_DIM
    y_parts = []
    for g in range(N_SSD_GROUPS):
        bg = bm[:, D_STATE * g:D_STATE * (g + 1)].astype(BF16)
        cg = cm[:, D_STATE * g:D_STATE * (g + 1)].astype(BF16)
        cb = lax.dot_general(cg, bg, (((1,), (1,)), ((), ())), preferred_element_type=F32)
        for jj in range(N_SSD_HEADS // N_SSD_GROUPS // 2):
            j = (N_SSD_HEADS // N_SSD_GROUPS // 2) * g + jj
            h0, h1 = 2 * j, 2 * j + 1
            col0 = cols[:, LANES * h0:LANES * (h0 + 1)]
            col1 = cols[:, LANES * h1:LANES * (h1 + 1)]
            m0 = cb * jnp.exp2(col0 - c2_t[h0:h0 + 1, :] + mask_add)
            m1 = cb * jnp.exp2(col1 - c2_t[h1:h1 + 1, :] + mask_add)
            lhs = jnp.concatenate([m0, m1], axis=1).astype(BF16)
            xp = xs[:, LANES * j:LANES * (j + 1)]
            rhs = jnp.concatenate([jnp.where(lo, xp, 0.0), jnp.where(lo, 0.0, xp)], axis=0).astype(BF16)
            y_parts.append(jnp.dot(lhs, rhs, preferred_element_type=F32))
    return jnp.concatenate(y_parts, axis=1)


def _spread(v, sel_ref):
    p0 = v.astype(BF16)
    p1 = (v - p0.astype(F32)).astype(BF16)
    return jnp.dot(jnp.concatenate([p0, p1], axis=1), sel_ref[...], preferred_element_type=F32)


def _ssd_gate_norm(ys, xs, z_act, dskip, g_ssd):
    ys = ys + dskip * xs
    gated = ys * z_act
    half = SSD_WIDTH // N_SSD_GROUPS
    outs = []
    for g in range(N_SSD_GROUPS):
        outs.append(_rms(gated[:, half * g:half * (g + 1)], g_ssd[:, half * g:half * (g + 1)]))
    return jnp.concatenate(outs, axis=1)


def _split_w_in_kernel(wt_ref, wdt_t_ref, main_ref, dt_ref):
    main_ref[...] = wt_ref[...].T.astype(BF16)

    @pl.when(pl.program_id(0) == 0)
    def _dt():
        dt_ref[...] = jnp.zeros_like(dt_ref)
        dt_ref[:, 0:N_SSD_HEADS] = wdt_t_ref[...].T.astype(BF16)


def _split_w_in(w_t):
    cols, rows = w_t.shape
    blk = SPLIT_COLS
    return pl.pallas_call(
        _split_w_in_kernel,
        out_shape=(jax.ShapeDtypeStruct((rows, PROJ_MAIN), BF16), jax.ShapeDtypeStruct((rows, DT_PAD), BF16)),
        grid=(PROJ_MAIN // blk,),
        in_specs=[pl.BlockSpec((blk, rows), lambda j: (j, 0)),
                  pl.BlockSpec((N_SSD_HEADS, rows), lambda j: (PROJ_MAIN // N_SSD_HEADS, 0))],
        out_specs=(pl.BlockSpec((rows, blk), lambda j: (0, j)), pl.BlockSpec((rows, DT_PAD), lambda j: (0, 0))),
        compiler_params=pltpu.CompilerParams(dimension_semantics=("arbitrary",)),
        name="split_w_in",
    )(w_t, w_t)


def _inproj_kernel(x_ref, xs_ref, g_ref, w_ref, wdt_ref, lcw_ref, lcb_ref, scw_ref, scb_ref, dtb_ref,
                   o_ref, olc_ref, osc_ref, os_ref, ext_l, ext_s, *, steps_per_seq):
    last_step = pl.program_id(0) == pl.num_programs(0) - 1
    t = lax.rem(pl.program_id(0), steps_per_seq)
    rows = ROW_TILE
    hist = SUBLANES
    o1, o2, o3 = LRU_WIDTH, 2 * LRU_WIDTH, 2 * LRU_WIDTH + SSD_WIDTH

    @pl.when(last_step)
    def _sample():
        hn = _rms(xs_ref[...].reshape(-1, D_MODEL), g_ref[...]).astype(BF16)
        os_ref[:, 0:PROJ_MAIN] = jnp.dot(hn, w_ref[...], preferred_element_type=F32)
        os_ref[:, PROJ_MAIN:PROJ_PAD] = jnp.dot(hn, wdt_ref[...], preferred_element_type=F32)

    @pl.when(jnp.logical_not(last_step))
    def _prompt():
        @pl.when(t == 0)
        def _init():
            ext_l[:, 0:hist, :] = jnp.zeros((ext_l.shape[0], hist, LANES), F32)
            ext_s[:, 0:hist, :] = jnp.zeros((ext_s.shape[0], hist, LANES), F32)

        hn = _rms(x_ref[...], g_ref[...]).astype(BF16)
        lx = jnp.dot(hn, w_ref[:, 0:o1], preferred_element_type=F32)
        for s in range(ext_l.shape[0]):
            ext_l[s, hist:hist + rows, :] = lx[:, LANES * s:LANES * (s + 1)]
        xbc_in = jnp.dot(hn, w_ref[:, o3:PROJ_MAIN], preferred_element_type=F32)
        for s in range(ext_s.shape[0]):
            ext_s[s, hist:hist + rows, :] = xbc_in[:, LANES * s:LANES * (s + 1)]
        o_ref[:, o1:o2] = _gelu_tanh(jnp.dot(hn, w_ref[:, o1:o2], preferred_element_type=F32))
        z = jnp.dot(hn, w_ref[:, o2:o3], preferred_element_type=F32)
        o_ref[:, o2:o3] = _silu(z)
        o_ref[:, PROJ_MAIN:PROJ_PAD] = _softplus(
            jnp.dot(hn, wdt_ref[...], preferred_element_type=F32) + dtb_ref[...])
        o_ref[:, 0:o1] = _conv_slabs(ext_l, lcw_ref, lcb_ref, rows, hist - (CONV_WIDTH - 1))
        xbc = _conv_slabs(ext_s, scw_ref, scb_ref, rows, hist - (CONV_WIDTH - 1))
        o_ref[:, o3:PROJ_MAIN] = _silu(xbc)

        @pl.when(t == steps_per_seq - 1)
        def _final():
            last = slice(hist + rows - (CONV_WIDTH - 1), hist + rows)
            for s in range(ext_l.shape[0]):
                olc_ref[:, LANES * s:LANES * (s + 1)] = ext_l[s, last, :]
            for s in range(ext_s.shape[0]):
                osc_ref[:, LANES * s:LANES * (s + 1)] = ext_s[s, last, :]

        tail_l = ext_l[:, rows:rows + hist, :]
        tail_s = ext_s[:, rows:rows + hist, :]
        ext_l[:, 0:hist, :] = tail_l
        ext_s[:, 0:hist, :] = tail_s


def _in_proj(x2d, bsz, xs, g_mix, w_main, w_dt, lcw, lcb, scw, scb, dtb):
    n = x2d.shape[0]
    steps_p = n // ROW_TILE
    steps_per_seq = steps_p // bsz
    assert xs.shape[0] * xs.shape[1] == ROW_TILE
    const = lambda i: (0, 0)
    tile = lambda i: (jnp.minimum(i, steps_p - 1), 0)
    seq_of = lambda i: (jnp.minimum(i, steps_p - 1) // steps_per_seq, 0, 0)
    once = dict(pipeline_mode=pl.Buffered(1))
    return pl.pallas_call(
        functools.partial(_inproj_kernel, steps_per_seq=steps_per_seq),
        out_shape=(
            jax.ShapeDtypeStruct((n, PROJ_PAD), F32),
            jax.ShapeDtypeStruct((bsz, CONV_WIDTH - 1, LRU_WIDTH), F32),
            jax.ShapeDtypeStruct((bsz, CONV_WIDTH - 1, SSD_CONV_DIM), F32),
            jax.ShapeDtypeStruct((ROW_TILE, PROJ_PAD), F32),
        ),
        grid=(steps_p + 1,),
        in_specs=[
            pl.BlockSpec((ROW_TILE, D_MODEL), tile),
            pl.BlockSpec(xs.shape, lambda i: (0, 0, 0), **once),
            pl.BlockSpec((1, D_MODEL), const),
            pl.BlockSpec((D_MODEL, PROJ_MAIN), const, **once),
            pl.BlockSpec((D_MODEL, DT_PAD), const, **once),
            pl.BlockSpec((CONV_WIDTH, LRU_WIDTH), const),
            pl.BlockSpec((1, LRU_WIDTH), const),
            pl.BlockSpec((CONV_WIDTH, SSD_CONV_DIM), const),
            pl.BlockSpec((1, SSD_CONV_DIM), const),
            pl.BlockSpec((1, DT_PAD), const),
        ],
        out_specs=(
            pl.BlockSpec((ROW_TILE, PROJ_PAD), tile),
            pl.BlockSpec((None, CONV_WIDTH - 1, LRU_WIDTH), seq_of),
            pl.BlockSpec((None, CONV_WIDTH - 1, SSD_CONV_DIM), seq_of),
            pl.BlockSpec((ROW_TILE, PROJ_PAD), const, **once),
        ),
        scratch_shapes=[
            pltpu.VMEM((LRU_WIDTH // LANES, SUBLANES + ROW_TILE, LANES), F32),
            pltpu.VMEM((SSD_CONV_DIM // LANES, SUBLANES + ROW_TILE, LANES), F32),
        ],
        compiler_params=pltpu.CompilerParams(
            dimension_semantics=("arbitrary",), vmem_limit_bytes=VMEM_LIMIT),
        name="in_proj",
    )(x2d, xs, g_mix, w_main, w_dt, lcw, lcb, scw, scb, dtb)


def _outmlp_kernel(xp_ref, yp_ref, xs_ref, ys_ref, wo_ref, gm_ref, wu_ref, wd_ref, gf_ref, op_ref, os_ref):
    last = pl.program_id(0) == pl.num_programs(0) - 1

    def run(x_ref, y_ref, o_ref):
        x = x_ref[...].reshape(-1, D_MODEL)
        x1 = x + jnp.dot(y_ref[...], wo_ref[...], preferred_element_type=F32)
        m = _rms(x1, gm_ref[...]).astype(BF16)
        u = jnp.dot(m, wu_ref[...], preferred_element_type=F32)
        u = jnp.square(jnp.maximum(u, 0.0)).astype(BF16)
        x2 = x1 + jnp.dot(u, wd_ref[...], preferred_element_type=F32)
        o_ref[...] = _rms(x2, gf_ref[...]).reshape(o_ref.shape)

    @pl.when(jnp.logical_not(last))
    def _prompt():
        run(xp_ref, yp_ref, op_ref)

    @pl.when(last)
    def _sample():
        run(xs_ref, ys_ref, os_ref)


def _out_mlp(xp, ymix_p, xs, ymix_s, w_out_b, g_mlp, w_up_b, w_down_b, g_final):
    n_p = xp.shape[0]
    steps_p = n_p // ROW_TILE
    assert ymix_s.shape[0] == ROW_TILE
    const = lambda i: (0, 0)
    tile = lambda i: (jnp.minimum(i, steps_p - 1), 0)
    once = dict(pipeline_mode=pl.Buffered(1))
    return pl.pallas_call(
        _outmlp_kernel,
        out_shape=(jax.ShapeDtypeStruct((n_p, D_MODEL), F32), jax.ShapeDtypeStruct(xs.shape, F32)),
        grid=(steps_p + 1,),
        in_specs=[
            pl.BlockSpec((ROW_TILE, D_MODEL), tile),
            pl.BlockSpec((ROW_TILE, MIX_WIDTH), tile),
            pl.BlockSpec(xs.shape, lambda i: (0, 0, 0), **once),
            pl.BlockSpec((ROW_TILE, MIX_WIDTH), const, **once),
            pl.BlockSpec((MIX_WIDTH, D_MODEL), const, **once),
            pl.BlockSpec((1, D_MODEL), const),
            pl.BlockSpec((D_MODEL, D_FF), const, **once),
            pl.BlockSpec((D_FF, D_MODEL), const, **once),
            pl.BlockSpec((1, D_MODEL), const),
        ],
        out_specs=(pl.BlockSpec((ROW_TILE, D_MODEL), tile), pl.BlockSpec(xs.shape, lambda i: (0, 0, 0))),
        compiler_params=pltpu.CompilerParams(
            dimension_semantics=("arbitrary",), vmem_limit_bytes=VMEM_LIMIT_OUT),
        name="out_mlp",
    )(xp, ymix_p, xs, ymix_s, w_out_b, g_mlp, w_up_b, w_down_b, g_final)


def _mixer_prompt_kernel(u_ref, gl_ref, zact_ref, xbc_ref, dt_ref,
                         wg_ref, ba_ref, bx_ref, lam_ref, glru_ref, alog_ref, dskip_ref, gssd_ref,
                         selt_ref, selp_ref, wo_ref, wu_ref, wd_ref,
                         y_ref, olh_ref, osh_ref, wo_b_ref, wu_b_ref, wd_b_ref,
                         a_pad, b_pad, h_pad, hcar, ht):
    t = pl.program_id(1)
    nt = pl.num_programs(1)
    tc = PROMPT_TC
    wo_b_ref[...] = wo_ref[...].astype(BF16)
    wu_b_ref[...] = wu_ref[...].astype(BF16)
    wd_b_ref[...] = wd_ref[...].astype(BF16)

    @pl.when(t == 0)
    def _init():
        hcar[...] = jnp.zeros_like(hcar)
        ht[...] = jnp.zeros_like(ht)

    neg_c_sp = (-LRU_C) * _softplus(-lam_ref[...])
    lane1 = lax.broadcasted_iota(jnp.int32, (1, LANES), 1)
    a2_row = jnp.where(lane1 < N_SSD_HEADS, -LOG2E * jnp.exp(alog_ref[...]), 0.0)
    L = SSD_CHUNK
    rr = lax.broadcasted_iota(jnp.int32, (L, L), 0)
    cc = lax.broadcasted_iota(jnp.int32, (L, L), 1)
    causal = cc <= rr
    tri = jnp.where(causal, 1.0, 0.0).astype(F32)
    mask_add = jnp.where(causal, 0.0, NEG_BIG).astype(F32)
    half = SSD_WIDTH // N_SSD_GROUPS

    chunks = [(n, c) for n in range(PROMPT_NB) for c in range(tc // L)]
    cums = [_ssd_cumdecay(dt_ref[n, L * c:L * (c + 1), :], a2_row, tri) for n, c in chunks]
    dts = [dt_ref[n, L * c:L * (c + 1), :] for n, c in chunks]
    cum_all = jnp.concatenate([cum2 for cum2, _ in cums], axis=0)
    cols_all = _spread(cum_all, selt_ref)
    ecol_all = _spread(jnp.exp2(cum_all), selp_ref)
    sdt_all = _spread(jnp.concatenate(
        [jnp.exp2(cum2[L - 1:L, :] - cum2) * dt for (cum2, _), dt in zip(cums, dts)], axis=0), selp_ref)

    for n in range(PROMPT_NB):
        a, b = _lru_coeffs(u_ref[n], wg_ref, ba_ref[...], bx_ref[...], neg_c_sp)
        hseq = _lru_scan_strided(a, b, hcar.at[n], a_pad.at[n], b_pad.at[n], h_pad.at[n])
        y_ref[n, :, 0:LRU_WIDTH] = _rms(hseq * gl_ref[n], glru_ref[...]).astype(BF16)

        for c in range(tc // L):
            k = chunks.index((n, c))
            rows = slice(L * c, L * (c + 1))
            krows = slice(L * k, L * (k + 1))
            xs = xbc_ref[n, rows, 0:SSD_WIDTH]
            bm = xbc_ref[n, rows, SSD_WIDTH:SSD_WIDTH + N_SSD_GROUPS * D_STATE]
            cm = xbc_ref[n, rows, SSD_WIDTH + N_SSD_GROUPS * D_STATE:SSD_CONV_DIM]
            y_diag = _ssd_diag(xs, bm, cm, dts[k], cols_all[krows, :], cums[k][1], mask_add)
            ecol = ecol_all[krows, :]
            xw = xs * sdt_all[krows, :]
            dec = ecol[L - 1:L, :]
            y_off_parts = []
            for g in range(N_SSD_GROUPS):
                htg = ht[n, g]
                cg = cm[:, D_STATE * g:D_STATE * (g + 1)].astype(BF16)
                y_off_parts.append(jnp.dot(cg, htg.astype(BF16), preferred_element_type=F32))
                bg_t = bm[:, D_STATE * g:D_STATE * (g + 1)].T.astype(BF16)
                st = jnp.dot(bg_t, xw[:, half * g:half * (g + 1)].astype(BF16), preferred_element_type=F32)
                ht[n, g] = htg * dec[:, half * g:half * (g + 1)] + st
            ys = y_diag + jnp.concatenate(y_off_parts, axis=1) * ecol
            y_ref[n, rows, LRU_WIDTH:MIX_WIDTH] = _ssd_gate_norm(
                ys, xs, zact_ref[n, rows, :], dskip_ref[...], gssd_ref[...]).astype(BF16)

    @pl.when(t == nt - 1)
    def _final():
        for n in range(PROMPT_NB):
            olh_ref[n] = hcar[n, 0:1, :]
            for g in range(N_SSD_GROUPS):
                osh_ref[n, half * g:half * (g + 1), :] = ht[n, g].T


def _param_specs(const):
    return [
        pl.BlockSpec((CONV_WIDTH, LRU_WIDTH), const),
        pl.BlockSpec((1, LRU_WIDTH), const),
        pl.BlockSpec((LRU_WIDTH // MXU_DIM, MXU_DIM, 2 * MXU_DIM), lambda *_: (0, 0, 0)),
        pl.BlockSpec((1, LRU_WIDTH), const),
        pl.BlockSpec((1, LRU_WIDTH), const),
        pl.BlockSpec((1, LRU_WIDTH), const),
        pl.BlockSpec((1, LRU_WIDTH), const),
        pl.BlockSpec((CONV_WIDTH, SSD_CONV_DIM), const),
        pl.BlockSpec((1, SSD_CONV_DIM), const),
        pl.BlockSpec((1, DT_PAD), const),
        pl.BlockSpec((1, DT_PAD), const),
        pl.BlockSpec((1, SSD_WIDTH), const),
        pl.BlockSpec((1, SSD_WIDTH), const),
    ]


def _head_selectors():
    k = np.arange(2 * LANES)[:, None] % LANES
    sel_t = (k == np.arange(N_SSD_HEADS * LANES)[None, :] // LANES).astype(np.float32)
    sel_p = (k == np.arange(SSD_WIDTH)[None, :] // SSD_HEAD_DIM).astype(np.float32)
    return jnp.asarray(sel_t, BF16), jnp.asarray(sel_p, BF16)


def _mixer_prompt(act, wg, b_a, b_x, lam, g_lru, a_log, d_skip, g_ssd, sel_t, sel_p, w_out, w_up, w_down):
    bsz, seq, _ = act.shape
    tc = PROMPT_TC
    nb = PROMPT_NB
    steps = (bsz // nb) * (seq // tc)
    const = lambda b, t: (0, 0)
    w_slice = lambda b, t: (b * (seq // tc) + t, 0)
    assert all(w.shape[0] % (steps * 2 * SUBLANES) == 0 for w in (w_out, w_up, w_down))
    w_specs = [pl.BlockSpec((w.shape[0] // steps, w.shape[1]), w_slice) for w in (w_out, w_up, w_down)]
    in_specs = [
        pl.BlockSpec((nb, tc, LRU_WIDTH), lambda b, t: (b, t, 0)),
        pl.BlockSpec((nb, tc, LRU_WIDTH), lambda b, t: (b, t, 1)),
        pl.BlockSpec((nb, tc, SSD_WIDTH), lambda b, t: (b, t, 2)),
        pl.BlockSpec((nb, tc, SSD_CONV_DIM), lambda b, t: (b, t, 2)),
        pl.BlockSpec((nb, tc, DT_PAD), lambda b, t: (b, t, PROJ_MAIN // DT_PAD)),
        pl.BlockSpec((LRU_WIDTH // MXU_DIM, MXU_DIM, 2 * MXU_DIM), lambda b, t: (0, 0, 0)),
        pl.BlockSpec((1, LRU_WIDTH), const),
        pl.BlockSpec((1, LRU_WIDTH), const),
        pl.BlockSpec((1, LRU_WIDTH), const),
        pl.BlockSpec((1, LRU_WIDTH), const),
        pl.BlockSpec((1, DT_PAD), const),
        pl.BlockSpec((1, SSD_WIDTH), const),
        pl.BlockSpec((1, SSD_WIDTH), const),
        pl.BlockSpec((2 * LANES, N_SSD_HEADS * LANES), const),
        pl.BlockSpec((2 * LANES, SSD_WIDTH), const),
    ] + w_specs
    out_shape = (
        jax.ShapeDtypeStruct((bsz, seq, MIX_WIDTH), BF16),
        jax.ShapeDtypeStruct((bsz, 1, LRU_WIDTH), F32),
        jax.ShapeDtypeStruct((bsz, SSD_WIDTH, D_STATE), F32),
    ) + tuple(jax.ShapeDtypeStruct(w.shape, BF16) for w in (w_out, w_up, w_down))
    out_specs = (
        pl.BlockSpec((nb, tc, MIX_WIDTH), lambda b, t: (b, t, 0)),
        pl.BlockSpec((nb, 1, LRU_WIDTH), lambda b, t: (b, 0, 0)),
        pl.BlockSpec((nb, SSD_WIDTH, D_STATE), lambda b, t: (b, 0, 0)),
    ) + tuple(w_specs)
    scratch = [
        pltpu.VMEM((nb, LRU_WIDTH // LANES, SUBLANES * SCAN_PITCH, LANES), F32),
        pltpu.VMEM((nb, LRU_WIDTH // LANES, SUBLANES * SCAN_PITCH, LANES), F32),
        pltpu.VMEM((nb, LRU_WIDTH // LANES, SUBLANES * SCAN_PITCH, LANES), F32),
        pltpu.VMEM((nb, SUBLANES, LRU_WIDTH), F32),
        pltpu.VMEM((nb, N_SSD_GROUPS, D_STATE, SSD_WIDTH // N_SSD_GROUPS), F32),
    ]
    return pl.pallas_call(
        _mixer_prompt_kernel,
        out_shape=out_shape,
        grid=(bsz // nb, seq // tc),
        in_specs=in_specs,
        out_specs=out_specs,
        scratch_shapes=scratch,
        compiler_params=pltpu.CompilerParams(
            dimension_semantics=("parallel", "arbitrary"), vmem_limit_bytes=VMEM_LIMIT),
        name="mixer_prompt",
    )(act, act, act, act, act, wg, b_a, b_x, lam, g_lru, a_log, d_skip, g_ssd, sel_t, sel_p,
      w_out, w_up, w_down)


def _mixer_sample_kernel(lx_ref, gate_ref, z_ref, xbc_ref, dt_ref,
                         slc_ref, slh_ref, ssc_ref, ssh_ref,
                         lcw_ref, lcb_ref, wg_ref, ba_ref, bx_ref, lam_ref, glru_ref,
                         scw_ref, scb_ref, dtb_ref, alog_ref, dskip_ref, gssd_ref, selt_ref, selp_ref,
                         y_ref, olc_ref, olh_ref, osc_ref, osh_ref,
                         ext_l, ext_s, pad_scr, yoff_scr, *, T):
    S = SAMPLE_SEQS
    P = SUBLANES
    K1 = CONV_WIDTH - 1
    R = S * P
    row_i = lax.broadcasted_iota(jnp.int32, (R, 1), 0) & (P - 1)
    valid = row_i < T

    def pad_rows(ref):
        width = ref.shape[-1]
        pad_scr[:, :, 0:width] = jnp.zeros((S, P, width), F32)
        pad_scr[:, 0:T, 0:width] = ref[...].reshape(S, T, width)
        return pad_scr[:, :, 0:width].reshape(R, width)

    ext_l[...] = jnp.zeros_like(ext_l)
    ext_s[...] = jnp.zeros_like(ext_s)
    for k in range(K1):
        ext_l[:, k, :] = slc_ref[k]
    ext_l[:, K1:K1 + T, :] = lx_ref[...].reshape(S, T, LRU_WIDTH)
    for k in range(K1):
        ext_s[:, k, :] = ssc_ref[k]
    ext_s[:, K1:K1 + T, :] = xbc_ref[...].reshape(S, T, SSD_CONV_DIM)
    for k in range(K1):
        olc_ref[k] = ext_l[:, T + k, :]
        osc_ref[k] = ext_s[:, T + k, :]

    el = ext_l[...].reshape(R, LRU_WIDTH)
    es = ext_s[...].reshape(R, SSD_CONV_DIM)

    def conv(e, w_ref, b_ref):
        out = b_ref[...] + e * w_ref[0:1, :]
        for k in range(1, CONV_WIDTH):
            out = out + pltpu.roll(e, R - k, axis=0) * w_ref[k:k + 1, :]
        return out

    u = conv(el, lcw_ref, lcb_ref)
    neg_c_sp = (-LRU_C) * _softplus(-lam_ref[...])
    a, b = _lru_coeffs(u, wg_ref, ba_ref[...], bx_ref[...], neg_c_sp)
    a, b = _scan_within_8(a, b)
    h0 = jnp.broadcast_to(slh_ref[...][:, None, :], (S, P, LRU_WIDTH)).reshape(R, LRU_WIDTH)
    hseq = a * h0 + b
    olh_ref[...] = hseq.reshape(S, P, LRU_WIDTH)[:, T - 1, :]
    gate = pad_rows(gate_ref)
    y_lru = _rms(hseq * _gelu_tanh(gate), glru_ref[...])

    xbc = conv(es, scw_ref, scb_ref)
    xbc = _silu(xbc)
    xs = xbc[:, 0:SSD_WIDTH]
    bm = xbc[:, SSD_WIDTH:SSD_WIDTH + N_SSD_GROUPS * D_STATE]
    cm = xbc[:, SSD_WIDTH + N_SSD_GROUPS * D_STATE:]
    dt_raw = pad_rows(dt_ref)
    dt = jnp.where(valid, _softplus(dt_raw + dtb_ref[...]), 0.0)
    lane1 = lax.broadcasted_iota(jnp.int32, (1, LANES), 1)
    a2_row = jnp.where(lane1 < N_SSD_HEADS, -LOG2E * jnp.exp(alog_ref[...]), 0.0)

    rr = lax.broadcasted_iota(jnp.int32, (R, R), 0)
    cc = lax.broadcasted_iota(jnp.int32, (R, R), 1)
    allowed = (cc <= rr) & ((rr - cc) <= (rr & (P - 1)))
    tri = jnp.where(allowed, 1.0, 0.0).astype(F32)
    mask_add = jnp.where(allowed, 0.0, NEG_BIG).astype(F32)

    cum2, cum2_t = _ssd_cumdecay(dt, a2_row, tri)
    y_diag = _ssd_diag(xs, bm, cm, dt, _spread(cum2, selt_ref), cum2_t, mask_add)
    ecol = _spread(jnp.exp2(cum2), selp_ref)
    end2 = jnp.broadcast_to(cum2.reshape(S, P, LANES)[:, P - 1:P, :], (S, P, LANES)).reshape(R, LANES)
    xw = xs * _spread(jnp.exp2(end2 - cum2) * dt, selp_ref)
    ecum_t = jnp.exp2(cum2_t)

    half = SSD_WIDTH // N_SSD_GROUPS
    for q in range(S):
        r0 = P * q
        vq = jnp.broadcast_to(ecum_t[:, r0 + P - 1:r0 + P], (N_SSD_HEADS, LANES))
        for g in range(N_SSD_GROUPS):
            hqg = ssh_ref[q, half * g:half * (g + 1), :]
            cq = cm[r0:r0 + P, D_STATE * g:D_STATE * (g + 1)].astype(BF16)
            yoff_scr[r0:r0 + P, half * g:half * (g + 1)] = lax.dot_general(
                cq, hqg.astype(BF16), (((1,), (1,)), ((), ())), preferred_element_type=F32)
            bq = bm[r0:r0 + P, D_STATE * g:D_STATE * (g + 1)].astype(BF16)
            xq = xw[r0:r0 + P, half * g:half * (g + 1)].astype(BF16)
            st = lax.dot_general(xq, bq, (((0,), (0,)), ((), ())), preferred_element_type=F32)
            for e in range(N_SSD_HEADS // N_SSD_GROUPS):
                h = (N_SSD_HEADS // N_SSD_GROUPS) * g + e
                lo_r = SSD_HEAD_DIM * e
                osh_ref[q, SSD_HEAD_DIM * h:SSD_HEAD_DIM * (h + 1), :] = (
                    vq[h:h + 1, :] * hqg[lo_r:lo_r + SSD_HEAD_DIM, :] + st[lo_r:lo_r + SSD_HEAD_DIM, :])

    ys = y_diag + yoff_scr[...] * ecol
    z = pad_rows(z_ref)
    y_ssd = _ssd_gate_norm(ys, xs, _silu(z), dskip_ref[...], gssd_ref[...])
    y_ref[:, 0:LRU_WIDTH] = y_lru.reshape(S, P, LRU_WIDTH)[:, 0:T, :].reshape(S * T, LRU_WIDTH).astype(BF16)
    y_ref[:, LRU_WIDTH:MIX_WIDTH] = (
        y_ssd.reshape(S, P, SSD_WIDTH)[:, 0:T, :].reshape(S * T, SSD_WIDTH).astype(BF16))


def _mixer_sample(proj, T, st_lc, st_lh, st_sc, st_sh, params, sel_t, sel_p):
    nseq = proj.shape[0] // T
    S = SAMPLE_SEQS
    const = lambda i: (0, 0)
    in_specs = [
        pl.BlockSpec((S * T, LRU_WIDTH), lambda i: (i, 0)),
        pl.BlockSpec((S * T, LRU_WIDTH), lambda i: (i, 1)),
        pl.BlockSpec((S * T, SSD_WIDTH), lambda i: (i, 2)),
        pl.BlockSpec((S * T, SSD_CONV_DIM), lambda i: (i, 2)),
        pl.BlockSpec((S * T, DT_PAD), lambda i: (i, PROJ_MAIN // DT_PAD)),
        pl.BlockSpec((CONV_WIDTH - 1, S, LRU_WIDTH), lambda i: (0, i, 0)),
        pl.BlockSpec((S, LRU_WIDTH), lambda i: (i, 0)),
        pl.BlockSpec((CONV_WIDTH - 1, S, SSD_CONV_DIM), lambda i: (0, i, 0)),
        pl.BlockSpec((S, SSD_WIDTH, D_STATE), lambda i: (i, 0, 0)),
    ] + _param_specs(const) + [
        pl.BlockSpec((2 * LANES, N_SSD_HEADS * LANES), const),
        pl.BlockSpec((2 * LANES, SSD_WIDTH), const),
    ]
    out_shape = (
        jax.ShapeDtypeStruct((nseq * T, MIX_WIDTH), BF16),
        jax.ShapeDtypeStruct((CONV_WIDTH - 1, nseq, LRU_WIDTH), F32),
        jax.ShapeDtypeStruct((nseq, LRU_WIDTH), F32),
        jax.ShapeDtypeStruct((CONV_WIDTH - 1, nseq, SSD_CONV_DIM), F32),
        jax.ShapeDtypeStruct((nseq, SSD_WIDTH, D_STATE), F32),
    )
    out_specs = (
        pl.BlockSpec((S * T, MIX_WIDTH), lambda i: (i, 0)),
        pl.BlockSpec((CONV_WIDTH - 1, S, LRU_WIDTH), lambda i: (0, i, 0)),
        pl.BlockSpec((S, LRU_WIDTH), lambda i: (i, 0)),
        pl.BlockSpec((CONV_WIDTH - 1, S, SSD_CONV_DIM), lambda i: (0, i, 0)),
        pl.BlockSpec((S, SSD_WIDTH, D_STATE), lambda i: (i, 0, 0)),
    )
    scratch = [
        pltpu.VMEM((S, SUBLANES, LRU_WIDTH), F32),
        pltpu.VMEM((S, SUBLANES, SSD_CONV_DIM), F32),
        pltpu.VMEM((S, SUBLANES, LRU_WIDTH), F32),
        pltpu.VMEM((S * SUBLANES, SSD_WIDTH), F32),
    ]
    return pl.pallas_call(
        functools.partial(_mixer_sample_kernel, T=T),
        out_shape=out_shape,
        grid=(nseq // S,),
        in_specs=in_specs,
        out_specs=out_specs,
        scratch_shapes=scratch,
        compiler_params=pltpu.CompilerParams(
            dimension_semantics=("parallel",), vmem_limit_bytes=VMEM_LIMIT),
        name="mixer_sample",
    )(proj, proj, proj, proj, proj, st_lc, st_lh, st_sc, st_sh, *params, sel_t, sel_p)


def _gate_weights(w_a, w_x):
    def tiles(w):
        per = MXU_DIM // LRU_BLOCK
        w4 = w.reshape(N_LRU_HEADS // per, per, LRU_BLOCK, LRU_BLOCK)
        eye = 0.5 * jnp.eye(per, dtype=w.dtype)
        t = jnp.einsum('jaik,ab->jaibk', w4, eye)
        return t.reshape(N_LRU_HEADS // per, MXU_DIM, MXU_DIM)
    return jnp.concatenate([tiles(w_a), tiles(w_x)], axis=2).astype(BF16)


def kernel(x_prompt, x_sample, state_lru_conv, state_lru_h, state_ssd_conv, state_ssd_h, g_mix, w_in,
           lru_conv_w, lru_conv_b, w_a, b_a, w_x, b_x, lam, g_lru_out, ssd_conv_w, ssd_conv_b, dt_bias,
           a_log, d_skip, g_ssd_out, w_out, g_mlp, w_up, w_down, g_final):
    depth = w_in.shape[0]
    assert depth == 1
    bp, seq, _ = x_prompt.shape
    bs, dseq, _ = x_sample.shape
    l = 0
    row = lambda v: v.reshape(1, -1)
    w_main, w_dt = _split_w_in(jnp.swapaxes(w_in, 1, 2)[l])
    params = (
        lru_conv_w[l], row(lru_conv_b[l]), _gate_weights(w_a[l], w_x[l]),
        row(b_a[l]), row(b_x[l]), row(lam[l]), row(g_lru_out[l]),
        ssd_conv_w[l], row(ssd_conv_b[l]),
        jnp.pad(row(dt_bias[l]), ((0, 0), (0, DT_PAD - N_SSD_HEADS))),
        jnp.pad(row(a_log[l]), ((0, 0), (0, DT_PAD - N_SSD_HEADS))),
        row(jnp.repeat(d_skip[l], SSD_HEAD_DIM)), row(g_ssd_out[l]),
    )
    gmix = row(g_mix[l])
    gmlp = row(g_mlp[l])
    gfin = row(g_final)

    xp2 = x_prompt.reshape(bp * seq, D_MODEL)
    (lcw, lcb, wg, ba, bx, lam_r, glru, scw, scb, dtb, alog, dskip, gssd) = params
    act_p, p_lc, p_sc, proj_s = _in_proj(xp2, bp, x_sample, gmix, w_main, w_dt, lcw, lcb, scw, scb, dtb)
    sel_t, sel_p = _head_selectors()
    ymix_p, p_lh, p_sh, w_out_b, w_up_b, w_down_b = _mixer_prompt(
        act_p.reshape(bp, seq, PROJ_PAD), wg, ba, bx, lam_r, glru, alog, dskip, gssd, sel_t, sel_p,
        w_out[l], w_up[l], w_down[l])

    ymix_s, s_lc, s_lh, s_sc, s_sh = _mixer_sample(
        proj_s, dseq, jnp.swapaxes(state_lru_conv[l], 0, 1), state_lru_h[l],
        jnp.swapaxes(state_ssd_conv[l], 0, 1),
        state_ssd_h[l].reshape(bs, SSD_WIDTH, D_STATE), params, sel_t, sel_p)
    y_prompt, y_sample = _out_mlp(xp2, ymix_p.reshape(bp * seq, MIX_WIDTH), x_sample, ymix_s,
                                  w_out_b, gmlp, w_up_b, w_down_b, gfin)

    hshape = (N_SSD_HEADS, SSD_HEAD_DIM, D_STATE)
    return (
        y_prompt.reshape(bp, seq, D_MODEL), y_sample,
        p_lc[None], p_lh.reshape(1, bp, LRU_WIDTH), p_sc[None], p_sh.reshape(1, bp, *hshape),
        jnp.swapaxes(s_lc, 0, 1)[None], s_lh[None], jnp.swapaxes(s_sc, 0, 1)[None],
        s_sh.reshape(1, bs, *hshape),
    )
```

```python
import functools
import math

import jax
import jax.numpy as jnp
import numpy as np
from jax import lax
from jax.experimental import pallas as pl
from jax.experimental.pallas import tpu as pltpu

F32 = jnp.float32
BF16 = jnp.bfloat16

D_MODEL = 1024
LRU_WIDTH = 1024
N_LRU_HEADS = 16
LRU_BLOCK = 64
LRU_C = 8.0
SSD_WIDTH = 1024
SSD_HEAD_DIM = 64
N_SSD_HEADS = 16
N_SSD_GROUPS = 2
D_STATE = 128
CONV_WIDTH = 4
SSD_CONV_DIM = SSD_WIDTH + 2 * N_SSD_GROUPS * D_STATE
D_FF = 4 * D_MODEL
EPS = 1e-6

LANES = 128
SUBLANES = 8
MXU_DIM = 256
DT_PAD = LANES
PROJ_MAIN = 2 * LRU_WIDTH + SSD_WIDTH + SSD_CONV_DIM
PROJ_PAD = PROJ_MAIN + DT_PAD
MIX_WIDTH = LRU_WIDTH + SSD_WIDTH
SSD_CHUNK = 128
PROMPT_TC = 256
PROMPT_NB = 2
ROW_TILE = 512
SPLIT_COLS = 1536
SAMPLE_SEQS = SSD_CHUNK // SUBLANES
SCAN_RUN = PROMPT_TC // SUBLANES
SCAN_PITCH = SCAN_RUN + 4
NEG_BIG = -1e30
LOG2E = 1.4426950408889634
VMEM_LIMIT = 56 * 1024 * 1024
VMEM_LIMIT_OUT = 60 * 1024 * 1024
VMEM_LIMIT_SAMPLE = 62 * 1024 * 1024
HI = lax.Precision.HIGHEST


def _rms(x, g):
    ms = jnp.mean(x * x, axis=-1, keepdims=True)
    return x * lax.rsqrt(ms + EPS) * g


def _silu(x):
    h = 0.5 * x
    return h * jnp.tanh(h) + h


def _softplus(x):
    return jnp.maximum(x, 0.0) + jnp.log1p(jnp.exp(-jnp.abs(x)))


def _gelu_tanh(x):
    c = math.sqrt(2.0 / math.pi)
    return 0.5 * x * (1.0 + jnp.tanh(c * (x + 0.044715 * (x * x * x))))


def _lru_coeffs(u, wg_ref, b_a, b_x, neg_c_sp):
    ub = u.astype(BF16)
    r_parts, i_parts = [], []
    for j in range(LRU_WIDTH // MXU_DIM):
        g = jnp.dot(ub[:, MXU_DIM * j:MXU_DIM * (j + 1)], wg_ref[j], preferred_element_type=F32)
        r_parts.append(g[:, :MXU_DIM])
        i_parts.append(g[:, MXU_DIM:])
    t_r = jnp.tanh(jnp.concatenate(r_parts, axis=1) + 0.5 * b_a)
    t_i = jnp.tanh(jnp.concatenate(i_parts, axis=1) + 0.5 * b_x)
    half_sp = 0.5 * neg_c_sp
    log_a = t_r * half_sp + half_sp
    a = jnp.exp(log_a)
    th = jnp.tanh(log_a)
    n = -0.5 * th
    nd = n * (1.0 - th)
    half_mult = jnp.where(nd > 0.0, n * lax.rsqrt(nd), 0.0)
    return a, half_mult * (u * t_i + u)


def _scan_within_8(a, b):
    ridx = lax.broadcasted_iota(jnp.int32, a.shape, 0) & (SUBLANES - 1)
    for k in (1, 2, 4):
        a_s = pltpu.roll(a, k, axis=0)
        b_s = pltpu.roll(b, k, axis=0)
        m = ridx >= k
        b = jnp.where(m, a * b_s + b, b)
        a = jnp.where(m, a * a_s, a)
    return a, b


def _conv_slabs(ext, w_ref, b_ref, rows, first):
    parts = []
    for s in range(ext.shape[0]):
        cols = slice(LANES * s, LANES * (s + 1))
        acc = b_ref[:, cols] + ext[s, pl.ds(first, rows), :] * w_ref[0:1, cols]
        for k in range(1, CONV_WIDTH):
            acc = acc + ext[s, pl.ds(first + k, rows), :] * w_ref[k:k + 1, cols]
        parts.append(acc)
    return jnp.concatenate(parts, axis=1)


def _lru_scan_strided(a, b, hcar, a_pad, b_pad, h_pad):
    rows = a.shape[0]
    S = rows // SUBLANES
    nslab = LRU_WIDTH // LANES
    ridx = lax.broadcasted_iota(jnp.int32, (SUBLANES, LANES), 0)
    step = lambda ref, s, i: ref[s, pl.ds(i, SUBLANES, stride=SCAN_PITCH), :]
    for s in range(nslab):
        cols = slice(LANES * s, LANES * (s + 1))
        for j in range(SUBLANES):
            a_pad[s, SCAN_PITCH * j:SCAN_PITCH * j + S, :] = a[S * j:S * (j + 1), cols]
            b_pad[s, SCAN_PITCH * j:SCAN_PITCH * j + S, :] = b[S * j:S * (j + 1), cols]
    h = [jnp.zeros((SUBLANES, LANES), F32)] * nslab
    prod = [jnp.ones((SUBLANES, LANES), F32)] * nslab
    for i in range(S):
        for s in range(nslab):
            av = step(a_pad, s, i)
            h[s] = av * h[s] + step(b_pad, s, i)
            prod[s] = av * prod[s]
    for s in range(nslab):
        cols = slice(LANES * s, LANES * (s + 1))
        pcum, hcum = _scan_within_8(prod[s], h[s])
        cin = hcar[:, cols]
        ends = hcum + pcum * cin
        h[s] = jnp.where(ridx == 0, cin, pltpu.roll(ends, 1, axis=0))
        hcar[:, cols] = jnp.broadcast_to(ends[SUBLANES - 1:SUBLANES, :], (SUBLANES, LANES))
    for i in range(S):
        for s in range(nslab):
            h[s] = step(a_pad, s, i) * h[s] + step(b_pad, s, i)
            h_pad[s, pl.ds(i, SUBLANES, stride=SCAN_PITCH), :] = h[s]
    return jnp.concatenate(
        [jnp.concatenate([h_pad[s, SCAN_PITCH * j:SCAN_PITCH * j + S, :] for j in range(SUBLANES)], axis=0)
         for s in range(nslab)], axis=1)


def _ssd_cumdecay(dt, a2_row, tri):
    cum2 = jnp.dot(tri, dt * a2_row, precision=HI, preferred_element_type=F32)
    return cum2, cum2.T[0:N_SSD_HEADS, :]


def _ssd_diag(xs, bm, cm, dt, cols, cum2_t, mask_add):
    L = xs.shape[0]
    c2_t = cum2_t - jnp.log2(dt.T[0:N_SSD_HEADS, :])
    lane = lax.broadcasted_iota(jnp.int32, (L, LANES), 1)
    lo = lane < SSD_HEAD_DIM
    y_parts = []
    for g in range(N_SSD_GROUPS):
        bg = bm[:, D_STATE * g:D_STATE * (g + 1)].astype(BF16)
        cg = cm[:, D_STATE * g:D_STATE * (g + 1)].astype(BF16)
        cb = lax.dot_general(cg, bg, (((1,), (1,)), ((), ())), preferred_element_type=F32)
        for jj in range(N_SSD_HEADS // N_SSD_GROUPS // 2):
            j = (N_SSD_HEADS // N_SSD_GROUPS // 2) * g + jj
            h0, h1 = 2 * j, 2 * j + 1
            col0 = cols[:, LANES * h0:LANES * (h0 + 1)]
            col1 = cols[:, LANES * h1:LANES * (h1 + 1)]
            m0 = cb * jnp.exp2(col0 - c2_t[h0:h0 + 1, :] + mask_add)
            m1 = cb * jnp.exp2(col1 - c2_t[h1:h1 + 1, :] + mask_add)
            lhs = jnp.concatenate([m0, m1], axis=1).astype(BF16)
            xp = xs[:, LANES * j:LANES * (j + 1)]
            rhs = jnp.concatenate([jnp.where(lo, xp, 0.0), jnp.where(lo, 0.0, xp)], axis=0).astype(BF16)
            y_parts.append(jnp.dot(lhs, rhs, preferred_element_type=F32))
    return jnp.concatenate(y_parts, axis=1)


def _spread(v, sel_ref):
    p0 = v.astype(BF16)
    p1 = (v - p0.astype(F32)).astype(BF16)
    return jnp.dot(jnp.concatenate([p0, p1], axis=1), sel_ref[...], preferred_element_type=F32)


def _ssd_gate_norm(ys, xs, z_act, dskip, g_ssd):
    ys = ys + dskip * xs
    gated = ys * z_act
    half = SSD_WIDTH // N_SSD_GROUPS
    outs = []
    for g in range(N_SSD_GROUPS):
        outs.append(_rms(gated[:, half * g:half * (g + 1)], g_ssd[:, half * g:half * (g + 1)]))
    return jnp.concatenate(outs, axis=1)


def _split_w_in_kernel(wt_ref, wdt_t_ref, main_ref, dt_ref):
    main_ref[...] = wt_ref[...].T.astype(BF16)

    @pl.when(pl.program_id(0) == 0)
    def _dt():
        dt_ref[...] = jnp.zeros_like(dt_ref)
        dt_ref[:, 0:N_SSD_HEADS] = wdt_t_ref[...].T.astype(BF16)


def _split_w_in(w_t):
    cols, rows = w_t.shape
    blk = SPLIT_COLS
    return pl.pallas_call(
        _split_w_in_kernel,
        out_shape=(jax.ShapeDtypeStruct((rows, PROJ_MAIN), BF16), jax.ShapeDtypeStruct((rows, DT_PAD), BF16)),
        grid=(PROJ_MAIN // blk,),
        in_specs=[pl.BlockSpec((blk, rows), lambda j: (j, 0)),
                  pl.BlockSpec((N_SSD_HEADS, rows), lambda j: (PROJ_MAIN // N_SSD_HEADS, 0))],
        out_specs=(pl.BlockSpec((rows, blk), lambda j: (0, j)), pl.BlockSpec((rows, DT_PAD), lambda j: (0, 0))),
        compiler_params=pltpu.CompilerParams(dimension_semantics=("arbitrary",)),
        name="split_w_in",
    )(w_t, w_t)


def _inproj_kernel(x_ref, xs_ref, g_ref, w_ref, wdt_ref, lcw_ref, lcb_ref, scw_ref, scb_ref, dtb_ref,
                   o_ref, olc_ref, osc_ref, os_ref, ext_l, ext_s, *, steps_per_seq):
    last_step = pl.program_id(0) == pl.num_programs(0) - 1
    t = lax.rem(pl.program_id(0), steps_per_seq)
    rows = ROW_TILE
    hist = SUBLANES
    o1, o2, o3 = LRU_WIDTH, 2 * LRU_WIDTH, 2 * LRU_WIDTH + SSD_WIDTH

    @pl.when(last_step)
    def _sample():
        hn = _rms(xs_ref[...].reshape(-1, D_MODEL), g_ref[...]).astype(BF16)
        os_ref[:, 0:PROJ_MAIN] = jnp.dot(hn, w_ref[...], preferred_element_type=F32)
        os_ref[:, PROJ_MAIN:PROJ_PAD] = jnp.dot(hn, wdt_ref[...], preferred_element_type=F32)

    @pl.when(jnp.logical_not(last_step))
    def _prompt():
        @pl.when(t == 0)
        def _init():
            ext_l[:, 0:hist, :] = jnp.zeros((ext_l.shape[0], hist, LANES), F32)
            ext_s[:, 0:hist, :] = jnp.zeros((ext_s.shape[0], hist, LANES), F32)

        hn = _rms(x_ref[...], g_ref[...]).astype(BF16)
        lx = jnp.dot(hn, w_ref[:, 0:o1], preferred_element_type=F32)
        for s in range(ext_l.shape[0]):
            ext_l[s, hist:hist + rows, :] = lx[:, LANES * s:LANES * (s + 1)]
        xbc_in = jnp.dot(hn, w_ref[:, o3:PROJ_MAIN], preferred_element_type=F32)
        for s in range(ext_s.shape[0]):
            ext_s[s, hist:hist + rows, :] = xbc_in[:, LANES * s:LANES * (s + 1)]
        o_ref[:, o1:o2] = _gelu_tanh(jnp.dot(hn, w_ref[:, o1:o2], preferred_element_type=F32))
        z = jnp.dot(hn, w_ref[:, o2:o3], preferred_element_type=F32)
        o_ref[:, o2:o3] = _silu(z)
        o_ref[:, PROJ_MAIN:PROJ_PAD] = _softplus(
            jnp.dot(hn, wdt_ref[...], preferred_element_type=F32) + dtb_ref[...])
        o_ref[:, 0:o1] = _conv_slabs(ext_l, lcw_ref, lcb_ref, rows, hist - (CONV_WIDTH - 1))
        xbc = _conv_slabs(ext_s, scw_ref, scb_ref, rows, hist - (CONV_WIDTH - 1))
        o_ref[:, o3:PROJ_MAIN] = _silu(xbc)

        @pl.when(t == steps_per_seq - 1)
        def _final():
            last = slice(hist + rows - (CONV_WIDTH - 1), hist + rows)
            for s in range(ext_l.shape[0]):
                olc_ref[:, LANES * s:LANES * (s + 1)] = ext_l[s, last, :]
            for s in range(ext_s.shape[0]):
                osc_ref[:, LANES * s:LANES * (s + 1)] = ext_s[s, last, :]

        tail_l = ext_l[:, rows:rows + hist, :]
        tail_s = ext_s[:, rows:rows + hist, :]
        ext_l[:, 0:hist, :] = tail_l
        ext_s[:, 0:hist, :] = tail_s


def _in_proj(x2d, bsz, xs, g_mix, w_main, w_dt, lcw, lcb, scw, scb, dtb):
    n = x2d.shape[0]
    steps_p = n // ROW_TILE
    steps_per_seq = steps_p // bsz
    assert xs.shape[0] * xs.shape[1] == ROW_TILE
    const = lambda i: (0, 0)
    tile = lambda i: (jnp.minimum(i, steps_p - 1), 0)
    seq_of = lambda i: (jnp.minimum(i, steps_p - 1) // steps_per_seq, 0, 0)
    once = dict(pipeline_mode=pl.Buffered(1))
    return pl.pallas_call(
        functools.partial(_inproj_kernel, steps_per_seq=steps_per_seq),
        out_shape=(
            jax.ShapeDtypeStruct((n, PROJ_PAD), F32),
            jax.ShapeDtypeStruct((bsz, CONV_WIDTH - 1, LRU_WIDTH), F32),
            jax.ShapeDtypeStruct((bsz, CONV_WIDTH - 1, SSD_CONV_DIM), F32),
            jax.ShapeDtypeStruct((ROW_TILE, PROJ_PAD), F32),
        ),
        grid=(steps_p + 1,),
        in_specs=[
            pl.BlockSpec((ROW_TILE, D_MODEL), tile),
            pl.BlockSpec(xs.shape, lambda i: (0, 0, 0), **once),
            pl.BlockSpec((1, D_MODEL), const),
            pl.BlockSpec((D_MODEL, PROJ_MAIN), const, **once),
            pl.BlockSpec((D_MODEL, DT_PAD), const, **once),
            pl.BlockSpec((CONV_WIDTH, LRU_WIDTH), const),
            pl.BlockSpec((1, LRU_WIDTH), const),
            pl.BlockSpec((CONV_WIDTH, SSD_CONV_DIM), const),
            pl.BlockSpec((1, SSD_CONV_DIM), const),
            pl.BlockSpec((1, DT_PAD), const),
        ],
        out_specs=(
            pl.BlockSpec((ROW_TILE, PROJ_PAD), tile),
            pl.BlockSpec((None, CONV_WIDTH - 1, LRU_WIDTH), seq_of),
            pl.BlockSpec((None, CONV_WIDTH - 1, SSD_CONV_DIM), seq_of),
            pl.BlockSpec((ROW_TILE, PROJ_PAD), const, **once),
        ),
        scratch_shapes=[
            pltpu.VMEM((LRU_WIDTH // LANES, SUBLANES + ROW_TILE, LANES), F32),
            pltpu.VMEM((SSD_CONV_DIM // LANES, SUBLANES + ROW_TILE, LANES), F32),
        ],
        compiler_params=pltpu.CompilerParams(
            dimension_semantics=("arbitrary",), vmem_limit_bytes=VMEM_LIMIT),
        name="in_proj",
    )(x2d, xs, g_mix, w_main, w_dt, lcw, lcb, scw, scb, dtb)


def _outmlp_kernel(xp_ref, yp_ref, xs_ref, ys_ref, wo_ref, gm_ref, wu_ref, wd_ref, gf_ref, op_ref, os_ref):
    last = pl.program_id(0) == pl.num_programs(0) - 1

    def run(x_ref, y_ref, o_ref):
        x = x_ref[...].reshape(-1, D_MODEL)
        x1 = x + jnp.dot(y_ref[...], wo_ref[...], preferred_element_type=F32)
        m = _rms(x1, gm_ref[...]).astype(BF16)
        u = jnp.dot(m, wu_ref[...], preferred_element_type=F32)
        u = jnp.square(jnp.maximum(u, 0.0)).astype(BF16)
        x2 = x1 + jnp.dot(u, wd_ref[...], preferred_element_type=F32)
        o_ref[...] = _rms(x2, gf_ref[...]).reshape(o_ref.shape)

    @pl.when(jnp.logical_not(last))
    def _prompt():
        run(xp_ref, yp_ref, op_ref)

    @pl.when(last)
    def _sample():
        run(xs_ref, ys_ref, os_ref)


def _out_mlp(xp, ymix_p, xs, ymix_s, w_out_b, g_mlp, w_up_b, w_down_b, g_final):
    n_p = xp.shape[0]
    steps_p = n_p // ROW_TILE
    assert ymix_s.shape[0] == ROW_TILE
    const = lambda i: (0, 0)
    tile = lambda i: (jnp.minimum(i, steps_p - 1), 0)
    once = dict(pipeline_mode=pl.Buffered(1))
    return pl.pallas_call(
        _outmlp_kernel,
        out_shape=(jax.ShapeDtypeStruct((n_p, D_MODEL), F32), jax.ShapeDtypeStruct(xs.shape, F32)),
        grid=(steps_p + 1,),
        in_specs=[
            pl.BlockSpec((ROW_TILE, D_MODEL), tile),
            pl.BlockSpec((ROW_TILE, MIX_WIDTH), tile),
            pl.BlockSpec(xs.shape, lambda i: (0, 0, 0), **once),
            pl.BlockSpec((ROW_TILE, MIX_WIDTH), const, **once),
            pl.BlockSpec((MIX_WIDTH, D_MODEL), const, **once),
            pl.BlockSpec((1, D_MODEL), const),
            pl.BlockSpec((D_MODEL, D_FF), const, **once),
            pl.BlockSpec((D_FF, D_MODEL), const, **once),
            pl.BlockSpec((1, D_MODEL), const),
        ],
        out_specs=(pl.BlockSpec((ROW_TILE, D_MODEL), tile), pl.BlockSpec(xs.shape, lambda i: (0, 0, 0))),
        compiler_params=pltpu.CompilerParams(
            dimension_semantics=("arbitrary",), vmem_limit_bytes=VMEM_LIMIT_OUT),
        name="out_mlp",
    )(xp, ymix_p, xs, ymix_s, w_out_b, g_mlp, w_up_b, w_down_b, g_final)


def _mixer_prompt_kernel(u_ref, gl_ref, zact_ref, xbc_ref, dt_ref,
                         wg_ref, ba_ref, bx_ref, lam_ref, glru_ref, alog_ref, dskip_ref, gssd_ref,
                         selt_ref, selp_ref, wo_ref, wu_ref, wd_ref,
                         y_ref, olh_ref, osh_ref, wo_b_ref, wu_b_ref, wd_b_ref,
                         a_pad, b_pad, h_pad, hcar, ht):
    t = pl.program_id(1)
    nt = pl.num_programs(1)
    tc = PROMPT_TC
    wo_b_ref[...] = wo_ref[...].astype(BF16)
    wu_b_ref[...] = wu_ref[...].astype(BF16)
    wd_b_ref[...] = wd_ref[...].astype(BF16)

    @pl.when(t == 0)
    def _init():
        hcar[...] = jnp.zeros_like(hcar)
        ht[...] = jnp.zeros_like(ht)

    neg_c_sp = (-LRU_C) * _softplus(-lam_ref[...])
    lane1 = lax.broadcasted_iota(jnp.int32, (1, LANES), 1)
    a2_row = jnp.where(lane1 < N_SSD_HEADS, -LOG2E * jnp.exp(alog_ref[...]), 0.0)
    L = SSD_CHUNK
    rr = lax.broadcasted_iota(jnp.int32, (L, L), 0)
    cc = lax.broadcasted_iota(jnp.int32, (L, L), 1)
    causal = cc <= rr
    tri = jnp.where(causal, 1.0, 0.0).astype(F32)
    mask_add = jnp.where(causal, 0.0, NEG_BIG).astype(F32)
    half = SSD_WIDTH // N_SSD_GROUPS

    chunks = [(n, c) for n in range(PROMPT_NB) for c in range(tc // L)]
    cums = [_ssd_cumdecay(dt_ref[n, L * c:L * (c + 1), :], a2_row, tri) for n, c in chunks]
    dts = [dt_ref[n, L * c:L * (c + 1), :] for n, c in chunks]
    cum_all = jnp.concatenate([cum2 for cum2, _ in cums], axis=0)
    cols_all = _spread(cum_all, selt_ref)
    ecol_all = _spread(jnp.exp2(cum_all), selp_ref)
    sdt_all = _spread(jnp.concatenate(
        [jnp.exp2(cum2[L - 1:L, :] - cum2) * dt for (cum2, _), dt in zip(cums, dts)], axis=0), selp_ref)

    for n in range(PROMPT_NB):
        a, b = _lru_coeffs(u_ref[n], wg_ref, ba_ref[...], bx_ref[...], neg_c_sp)
        hseq = _lru_scan_strided(a, b, hcar.at[n], a_pad.at[n], b_pad.at[n], h_pad.at[n])
        y_ref[n, :, 0:LRU_WIDTH] = _rms(hseq * gl_ref[n], glru_ref[...]).astype(BF16)

        for c in range(tc // L):
            k = chunks.index((n, c))
            rows = slice(L * c, L * (c + 1))
            krows = slice(L * k, L * (k + 1))
            xs = xbc_ref[n, rows, 0:SSD_WIDTH]
            bm = xbc_ref[n, rows, SSD_WIDTH:SSD_WIDTH + N_SSD_GROUPS * D_STATE]
            cm = xbc_ref[n, rows, SSD_WIDTH + N_SSD_GROUPS * D_STATE:SSD_CONV_DIM]
            y_diag = _ssd_diag(xs, bm, cm, dts[k], cols_all[krows, :], cums[k][1], mask_add)
            ecol = ecol_all[krows, :]
            xw = xs * sdt_all[krows, :]
            dec = ecol[L - 1:L, :]
            y_off_parts = []
            for g in range(N_SSD_GROUPS):
                htg = ht[n, g]
                cg = cm[:, D_STATE * g:D_STATE * (g + 1)].astype(BF16)
                y_off_parts.append(jnp.dot(cg, htg.astype(BF16), preferred_element_type=F32))
                bg_t = bm[:, D_STATE * g:D_STATE * (g + 1)].T.astype(BF16)
                st = jnp.dot(bg_t, xw[:, half * g:half * (g + 1)].astype(BF16), preferred_element_type=F32)
                ht[n, g] = htg * dec[:, half * g:half * (g + 1)] + st
            ys = y_diag + jnp.concatenate(y_off_parts, axis=1) * ecol
            y_ref[n, rows, LRU_WIDTH:MIX_WIDTH] = _ssd_gate_norm(
                ys, xs, zact_ref[n, rows, :], dskip_ref[...], gssd_ref[...]).astype(BF16)

    @pl.when(t == nt - 1)
    def _final():
        for n in range(PROMPT_NB):
            olh_ref[n] = hcar[n, 0:1, :]
            for g in range(N_SSD_GROUPS):
                osh_ref[n, half * g:half * (g + 1), :] = ht[n, g].T


def _param_specs(const):
    return [
        pl.BlockSpec((CONV_WIDTH, LRU_WIDTH), const),
        pl.BlockSpec((1, LRU_WIDTH), const),
        pl.BlockSpec((LRU_WIDTH // MXU_DIM, MXU_DIM, 2 * MXU_DIM), lambda *_: (0, 0, 0)),
        pl.BlockSpec((1, LRU_WIDTH), const),
        pl.BlockSpec((1, LRU_WIDTH), const),
        pl.BlockSpec((1, LRU_WIDTH), const),
        pl.BlockSpec((1, LRU_WIDTH), const),
        pl.BlockSpec((CONV_WIDTH, SSD_CONV_DIM), const),
        pl.BlockSpec((1, SSD_CONV_DIM), const),
        pl.BlockSpec((1, DT_PAD), const),
        pl.BlockSpec((1, DT_PAD), const),
        pl.BlockSpec((1, SSD_WIDTH), const),
        pl.BlockSpec((1, SSD_WIDTH), const),
    ]


def _head_selectors():
    k = np.arange(2 * LANES)[:, None] % LANES
    sel_t = (k == np.arange(N_SSD_HEADS * LANES)[None, :] // LANES).astype(np.float32)
    sel_p = (k == np.arange(SSD_WIDTH)[None, :] // SSD_HEAD_DIM).astype(np.float32)
    return jnp.asarray(sel_t, BF16), jnp.asarray(sel_p, BF16)


def _mixer_prompt(act, wg, b_a, b_x, lam, g_lru, a_log, d_skip, g_ssd, sel_t, sel_p, w_out, w_up, w_down):
    bsz, seq, _ = act.shape
    tc = PROMPT_TC
    nb = PROMPT_NB
    steps = (bsz // nb) * (seq // tc)
    const = lambda b, t: (0, 0)
    w_slice = lambda b, t: (b * (seq // tc) + t, 0)
    assert all(w.shape[0] % (steps * 2 * SUBLANES) == 0 for w in (w_out, w_up, w_down))
    w_specs = [pl.BlockSpec((w.shape[0] // steps, w.shape[1]), w_slice) for w in (w_out, w_up, w_down)]
    in_specs = [
        pl.BlockSpec((nb, tc, LRU_WIDTH), lambda b, t: (b, t, 0)),
        pl.BlockSpec((nb, tc, LRU_WIDTH), lambda b, t: (b, t, 1)),
        pl.BlockSpec((nb, tc, SSD_WIDTH), lambda b, t: (b, t, 2)),
        pl.BlockSpec((nb, tc, SSD_CONV_DIM), lambda b, t: (b, t, 2)),
        pl.BlockSpec((nb, tc, DT_PAD), lambda b, t: (b, t, PROJ_MAIN // DT_PAD)),
        pl.BlockSpec((LRU_WIDTH // MXU_DIM, MXU_DIM, 2 * MXU_DIM), lambda b, t: (0, 0, 0)),
        pl.BlockSpec((1, LRU_WIDTH), const),
        pl.BlockSpec((1, LRU_WIDTH), const),
        pl.BlockSpec((1, LRU_WIDTH), const),
        pl.BlockSpec((1, LRU_WIDTH), const),
        pl.BlockSpec((1, DT_PAD), const),
        pl.BlockSpec((1, SSD_WIDTH), const),
        pl.BlockSpec((1, SSD_WIDTH), const),
        pl.BlockSpec((2 * LANES, N_SSD_HEADS * LANES), const),
        pl.BlockSpec((2 * LANES, SSD_WIDTH), const),
    ] + w_specs
    out_shape = (
        jax.ShapeDtypeStruct((bsz, seq, MIX_WIDTH), BF16),
        jax.ShapeDtypeStruct((bsz, 1, LRU_WIDTH), F32),
        jax.ShapeDtypeStruct((bsz, SSD_WIDTH, D_STATE), F32),
    ) + tuple(jax.ShapeDtypeStruct(w.shape, BF16) for w in (w_out, w_up, w_down))
    out_specs = (
        pl.BlockSpec((nb, tc, MIX_WIDTH), lambda b, t: (b, t, 0)),
        pl.BlockSpec((nb, 1, LRU_WIDTH), lambda b, t: (b, 0, 0)),
        pl.BlockSpec((nb, SSD_WIDTH, D_STATE), lambda b, t: (b, 0, 0)),
    ) + tuple(w_specs)
    scratch = [
        pltpu.VMEM((nb, LRU_WIDTH // LANES, SUBLANES * SCAN_PITCH, LANES), F32),
        pltpu.VMEM((nb, LRU_WIDTH // LANES, SUBLANES * SCAN_PITCH, LANES), F32),
        pltpu.VMEM((nb, LRU_WIDTH // LANES, SUBLANES * SCAN_PITCH, LANES), F32),
        pltpu.VMEM((nb, SUBLANES, LRU_WIDTH), F32),
        pltpu.VMEM((nb, N_SSD_GROUPS, D_STATE, SSD_WIDTH // N_SSD_GROUPS), F32),
    ]
    return pl.pallas_call(
        _mixer_prompt_kernel,
        out_shape=out_shape,
        grid=(bsz // nb, seq // tc),
        in_specs=in_specs,
        out_specs=out_specs,
        scratch_shapes=scratch,
        compiler_params=pltpu.CompilerParams(
            dimension_semantics=("parallel", "arbitrary"), vmem_limit_bytes=VMEM_LIMIT),
        name="mixer_prompt",
    )(act, act, act, act, act, wg, b_a, b_x, lam, g_lru, a_log, d_skip, g_ssd, sel_t, sel_p,
      w_out, w_up, w_down)


def _mixer_sample_kernel(lx_ref, gate_ref, z_ref, xbc_ref, dt_ref,
                         slc_ref, slh_ref, ssc_ref, ssh_ref,
                         lcw_ref, lcb_ref, wg_ref, ba_ref, bx_ref, lam_ref, glru_ref,
                         scw_ref, scb_ref, dtb_ref, alog_ref, dskip_ref, gssd_ref, selt_ref, selp_ref,
                         y_ref, olc_ref, olh_ref, osc_ref, osh_ref,
                         ext_l, ext_s, pad_scr, yoff_scr, *, T):
    S = SAMPLE_SEQS
    P = SUBLANES
    K1 = CONV_WIDTH - 1
    R = S * P
    row_i = lax.broadcasted_iota(jnp.int32, (R, 1), 0) & (P - 1)
    valid = row_i < T

    def pad_rows(ref):
        width = ref.shape[-1]
        pad_scr[:, :, 0:width] = jnp.zeros((S, P, width), F32)
        pad_scr[:, 0:T, 0:width] = ref[...].reshape(S, T, width)
        return pad_scr[:, :, 0:width].reshape(R, width)

    ext_l[...] = jnp.zeros_like(ext_l)
    ext_s[...] = jnp.zeros_like(ext_s)
    for k in range(K1):
        ext_l[:, k, :] = slc_ref[k]
    ext_l[:, K1:K1 + T, :] = lx_ref[...].reshape(S, T, LRU_WIDTH)
    for k in range(K1):
        ext_s[:, k, :] = ssc_ref[k]
    ext_s[:, K1:K1 + T, :] = xbc_ref[...].reshape(S, T, SSD_CONV_DIM)
    for k in range(K1):
        olc_ref[k] = ext_l[:, T + k, :]
        osc_ref[k] = ext_s[:, T + k, :]

    el = ext_l[...].reshape(R, LRU_WIDTH)
    es = ext_s[...].reshape(R, SSD_CONV_DIM)

    def conv(e, w_ref, b_ref):
        out = b_ref[...] + e * w_ref[0:1, :]
        for k in range(1, CONV_WIDTH):
            out = out + pltpu.roll(e, R - k, axis=0) * w_ref[k:k + 1, :]
        return out

    u = conv(el, lcw_ref, lcb_ref)
    neg_c_sp = (-LRU_C) * _softplus(-lam_ref[...])
    a, b = _lru_coeffs(u, wg_ref, ba_ref[...], bx_ref[...], neg_c_sp)
    a, b = _scan_within_8(a, b)
    h0 = jnp.broadcast_to(slh_ref[...][:, None, :], (S, P, LRU_WIDTH)).reshape(R, LRU_WIDTH)
    hseq = a * h0 + b
    olh_ref[...] = hseq.reshape(S, P, LRU_WIDTH)[:, T - 1, :]
    gate = pad_rows(gate_ref)
    y_lru = _rms(hseq * _gelu_tanh(gate), glru_ref[...])

    xbc = conv(es, scw_ref, scb_ref)
    xbc = _silu(xbc)
    xs = xbc[:, 0:SSD_WIDTH]
    bm = xbc[:, SSD_WIDTH:SSD_WIDTH + N_SSD_GROUPS * D_STATE]
    cm = xbc[:, SSD_WIDTH + N_SSD_GROUPS * D_STATE:]
    dt_raw = pad_rows(dt_ref)
    dt = jnp.where(valid, _softplus(dt_raw + dtb_ref[...]), 0.0)
    lane1 = lax.broadcasted_iota(jnp.int32, (1, LANES), 1)
    a2_row = jnp.where(lane1 < N_SSD_HEADS, -LOG2E * jnp.exp(alog_ref[...]), 0.0)

    rr = lax.broadcasted_iota(jnp.int32, (R, R), 0)
    cc = lax.broadcasted_iota(jnp.int32, (R, R), 1)
    allowed = (cc <= rr) & ((rr - cc) <= (rr & (P - 1)))
    tri = jnp.where(allowed, 1.0, 0.0).astype(F32)
    mask_add = jnp.where(allowed, 0.0, NEG_BIG).astype(F32)

    cum2, cum2_t = _ssd_cumdecay(dt, a2_row, tri)
    y_diag = _ssd_diag(xs, bm, cm, dt, _spread(cum2, selt_ref), cum2_t, mask_add)
    ecol = _spread(jnp.exp2(cum2), selp_ref)
    end2 = jnp.broadcast_to(cum2.reshape(S, P, LANES)[:, P - 1:P, :], (S, P, LANES)).reshape(R, LANES)
    xw = xs * _spread(jnp.exp2(end2 - cum2) * dt, selp_ref)
    ecum_t = jnp.exp2(cum2_t)

    half = SSD_WIDTH // N_SSD_GROUPS
    for q in range(S):
        r0 = P * q
        vq = jnp.broadcast_to(ecum_t[:, r0 + P - 1:r0 + P], (N_SSD_HEADS, LANES))
        for g in range(N_SSD_GROUPS):
            hqg = ssh_ref[q, half * g:half * (g + 1), :]
            cq = cm[r0:r0 + P, D_STATE * g:D_STATE * (g + 1)].astype(BF16)
            yoff_scr[r0:r0 + P, half * g:half * (g + 1)] = lax.dot_general(
                cq, hqg.astype(BF16), (((1,), (1,)), ((), ())), preferred_element_type=F32)
            bq = bm[r0:r0 + P, D_STATE * g:D_STATE * (g + 1)].astype(BF16)
            xq = xw[r0:r0 + P, half * g:half * (g + 1)].astype(BF16)
            st = lax.dot_general(xq, bq, (((0,), (0,)), ((), ())), preferred_element_type=F32)
            for e in range(N_SSD_HEADS // N_SSD_GROUPS):
                h = (N_SSD_HEADS // N_SSD_GROUPS) * g + e
                lo_r = SSD_HEAD_DIM * e
                osh_ref[q, SSD_HEAD_DIM * h:SSD_HEAD_DIM * (h + 1), :] = (
                    vq[h:h + 1, :] * hqg[lo_r:lo_r + SSD_HEAD_DIM, :] + st[lo_r:lo_r + SSD_HEAD_DIM, :])

    ys = y_diag + yoff_scr[...] * ecol
    z = pad_rows(z_ref)
    y_ssd = _ssd_gate_norm(ys, xs, _silu(z), dskip_ref[...], gssd_ref[...])
    y_ref[:, 0:LRU_WIDTH] = y_lru.reshape(S, P, LRU_WIDTH)[:, 0:T, :].reshape(S * T, LRU_WIDTH).astype(BF16)
    y_ref[:, LRU_WIDTH:MIX_WIDTH] = (
        y_ssd.reshape(S, P, SSD_WIDTH)[:, 0:T, :].reshape(S * T, SSD_WIDTH).astype(BF16))


def _mixer_sample(proj, T, st_lc, st_lh, st_sc, st_sh, params, sel_t, sel_p):
    nseq = proj.shape[0] // T
    S = SAMPLE_SEQS
    const = lambda i: (0, 0)
    in_specs = [
        pl.BlockSpec((S * T, LRU_WIDTH), lambda i: (i, 0)),
        pl.BlockSpec((S * T, LRU_WIDTH), lambda i: (i, 1)),
        pl.BlockSpec((S * T, SSD_WIDTH), lambda i: (i, 2)),
        pl.BlockSpec((S * T, SSD_CONV_DIM), lambda i: (i, 2)),
        pl.BlockSpec((S * T, DT_PAD), lambda i: (i, PROJ_MAIN // DT_PAD)),
        pl.BlockSpec((CONV_WIDTH - 1, S, LRU_WIDTH), lambda i: (0, i, 0)),
        pl.BlockSpec((S, LRU_WIDTH), lambda i: (i, 0)),
        pl.BlockSpec((CONV_WIDTH - 1, S, SSD_CONV_DIM), lambda i: (0, i, 0)),
        pl.BlockSpec((S, SSD_WIDTH, D_STATE), lambda i: (i, 0, 0), pipeline_mode=pl.Buffered(3)),
    ] + _param_specs(const) + [
        pl.BlockSpec((2 * LANES, N_SSD_HEADS * LANES), const),
        pl.BlockSpec((2 * LANES, SSD_WIDTH), const),
    ]
    out_shape = (
        jax.ShapeDtypeStruct((nseq * T, MIX_WIDTH), BF16),
        jax.ShapeDtypeStruct((CONV_WIDTH - 1, nseq, LRU_WIDTH), F32),
        jax.ShapeDtypeStruct((nseq, LRU_WIDTH), F32),
        jax.ShapeDtypeStruct((CONV_WIDTH - 1, nseq, SSD_CONV_DIM), F32),
        jax.ShapeDtypeStruct((nseq, SSD_WIDTH, D_STATE), F32),
    )
    out_specs = (
        pl.BlockSpec((S * T, MIX_WIDTH), lambda i: (i, 0)),
        pl.BlockSpec((CONV_WIDTH - 1, S, LRU_WIDTH), lambda i: (0, i, 0)),
        pl.BlockSpec((S, LRU_WIDTH), lambda i: (i, 0)),
        pl.BlockSpec((CONV_WIDTH - 1, S, SSD_CONV_DIM), lambda i: (0, i, 0)),
        pl.BlockSpec((S, SSD_WIDTH, D_STATE), lambda i: (i, 0, 0)),
    )
    scratch = [
        pltpu.VMEM((S, SUBLANES, LRU_WIDTH), F32),
        pltpu.VMEM((S, SUBLANES, SSD_CONV_DIM), F32),
        pltpu.VMEM((S, SUBLANES, LRU_WIDTH), F32),
        pltpu.VMEM((S * SUBLANES, SSD_WIDTH), F32),
    ]
    operands = (proj, proj, proj, proj, proj, st_lc, st_lh, st_sc, st_sh, *params, sel_t, sel_p)
    n_in = len(operands)
    n_out = len(out_shape)

    def outer(*refs):
        scr = refs[n_in + n_out:]

        def body(*blocks):
            _mixer_sample_kernel(*blocks, *scr, T=T)

        pltpu.emit_pipeline(body, grid=(nseq // S,), in_specs=in_specs, out_specs=list(out_specs))(
            *refs[:n_in + n_out])

    anyspec = pl.BlockSpec(memory_space=pl.ANY)
    return pl.pallas_call(
        outer,
        out_shape=out_shape,
        in_specs=[anyspec] * n_in,
        out_specs=tuple([anyspec] * n_out),
        scratch_shapes=scratch,
        compiler_params=pltpu.CompilerParams(vmem_limit_bytes=VMEM_LIMIT_SAMPLE),
        name="mixer_sample",
    )(*operands)


def _gate_weights(w_a, w_x):
    def tiles(w):
        per = MXU_DIM // LRU_BLOCK
        w4 = w.reshape(N_LRU_HEADS // per, per, LRU_BLOCK, LRU_BLOCK)
        eye = 0.5 * jnp.eye(per, dtype=w.dtype)
        t = jnp.einsum('jaik,ab->jaibk', w4, eye)
        return t.reshape(N_LRU_HEADS // per, MXU_DIM, MXU_DIM)
    return jnp.concatenate([tiles(w_a), tiles(w_x)], axis=2).astype(BF16)


def kernel(x_prompt, x_sample, state_lru_conv, state_lru_h, state_ssd_conv, state_ssd_h, g_mix, w_in,
           lru_conv_w, lru_conv_b, w_a, b_a, w_x, b_x, lam, g_lru_out, ssd_conv_w, ssd_conv_b, dt_bias,
           a_log, d_skip, g_ssd_out, w_out, g_mlp, w_up, w_down, g_final):
    depth = w_in.shape[0]
    assert depth == 1
    bp, seq, _ = x_prompt.shape
    bs, dseq, _ = x_sample.shape
    l = 0
    row = lambda v: v.reshape(1, -1)
    w_main, w_dt = _split_w_in(jnp.swapaxes(w_in, 1, 2)[l])
    params = (
        lru_conv_w[l], row(lru_conv_b[l]), _gate_weights(w_a[l], w_x[l]),
        row(b_a[l]), row(b_x[l]), row(lam[l]), row(g_lru_out[l]),
        ssd_conv_w[l], row(ssd_conv_b[l]),
        jnp.pad(row(dt_bias[l]), ((0, 0), (0, DT_PAD - N_SSD_HEADS))),
        jnp.pad(row(a_log[l]), ((0, 0), (0, DT_PAD - N_SSD_HEADS))),
        row(jnp.repeat(d_skip[l], SSD_HEAD_DIM)), row(g_ssd_out[l]),
    )
    gmix = row(g_mix[l])
    gmlp = row(g_mlp[l])
    gfin = row(g_final)

    xp2 = x_prompt.reshape(bp * seq, D_MODEL)
    (lcw, lcb, wg, ba, bx, lam_r, glru, scw, scb, dtb, alog, dskip, gssd) = params
    act_p, p_lc, p_sc, proj_s = _in_proj(xp2, bp, x_sample, gmix, w_main, w_dt, lcw, lcb, scw, scb, dtb)
    sel_t, sel_p = _head_selectors()
    ymix_p, p_lh, p_sh, w_out_b, w_up_b, w_down_b = _mixer_prompt(
        act_p.reshape(bp, seq, PROJ_PAD), wg, ba, bx, lam_r, glru, alog, dskip, gssd, sel_t, sel_p,
        w_out[l], w_up[l], w_down[l])

    ymix_s, s_lc, s_lh, s_sc, s_sh = _mixer_sample(
        proj_s, dseq, jnp.swapaxes(state_lru_conv[l], 0, 1), state_lru_h[l],
        jnp.swapaxes(state_ssd_conv[l], 0, 1),
        state_ssd_h[l].reshape(bs, SSD_WIDTH, D_STATE), params, sel_t, sel_p)
    y_prompt, y_sample = _out_mlp(xp2, ymix_p.reshape(bp * seq, MIX_WIDTH), x_sample, ymix_s,
                                  w_out_b, gmlp, w_up_b, w_down_b, gfin)

    hshape = (N_SSD_HEADS, SSD_HEAD_DIM, D_STATE)
    return (
        y_prompt.reshape(bp, seq, D_MODEL), y_sample,
        p_lc[None], p_lh.reshape(1, bp, LRU_WIDTH), p_sc[None], p_sh.reshape(1, bp, *hshape),
        jnp.swapaxes(s_lc, 0, 1)[None], s_lh[None], jnp.swapaxes(s_sc, 0, 1)[None],
        s_sh.reshape(1, bs, *hshape),
    )
```

```python
import functools
import math

import jax
import jax.numpy as jnp
import numpy as np
from jax import lax
from jax.experimental import pallas as pl
from jax.experimental.pallas import tpu as pltpu

F32 = jnp.float32
BF16 = jnp.bfloat16

D_MODEL = 1024
LRU_WIDTH = 1024
N_LRU_HEADS = 16
LRU_BLOCK = 64
LRU_C = 8.0
SSD_WIDTH = 1024
SSD_HEAD_DIM = 64
N_SSD_HEADS = 16
N_SSD_GROUPS = 2
D_STATE = 128
CONV_WIDTH = 4
SSD_CONV_DIM = SSD_WIDTH + 2 * N_SSD_GROUPS * D_STATE
D_FF = 4 * D_MODEL
EPS = 1e-6

LANES = 128
SUBLANES = 8
MXU_DIM = 256
DT_PAD = LANES
PROJ_MAIN = 2 * LRU_WIDTH + SSD_WIDTH + SSD_CONV_DIM
PROJ_PAD = PROJ_MAIN + DT_PAD
MIX_WIDTH = LRU_WIDTH + SSD_WIDTH
SSD_CHUNK = 128
PROMPT_TC = 256
PROMPT_NB = 2
ROW_TILE = 512
SPLIT_COLS = 1536
SAMPLE_SEQS = SSD_CHUNK // SUBLANES
SCAN_RUN = PROMPT_TC // SUBLANES
SCAN_PITCH = SCAN_RUN + 4
NEG_BIG = -1e30
LOG2E = 1.4426950408889634
VMEM_LIMIT = 56 * 1024 * 1024
VMEM_LIMIT_OUT = 60 * 1024 * 1024
HI = lax.Precision.HIGHEST


def _rms(x, g):
    ms = jnp.mean(x * x, axis=-1, keepdims=True)
    return x * lax.rsqrt(ms + EPS) * g


def _silu(x):
    h = 0.5 * x
    return h * jnp.tanh(h) + h


def _softplus(x):
    return jnp.maximum(x, 0.0) + jnp.log1p(jnp.exp(-jnp.abs(x)))


def _gelu_tanh(x):
    c = math.sqrt(2.0 / math.pi)
    h = 0.5 * x
    return h * jnp.tanh(x * (c + (c * 0.044715) * (x * x))) + h


def _lru_coeffs(u, wg_ref, b_a, b_x, neg_c_sp):
    ub = u.astype(BF16)
    r_parts, i_parts = [], []
    for j in range(LRU_WIDTH // MXU_DIM):
        g = jnp.dot(ub[:, MXU_DIM * j:MXU_DIM * (j + 1)], wg_ref[j], preferred_element_type=F32)
        r_parts.append(g[:, :MXU_DIM])
        i_parts.append(g[:, MXU_DIM:])
    t_r = jnp.tanh(jnp.concatenate(r_parts, axis=1) + 0.5 * b_a)
    t_i = jnp.tanh(jnp.concatenate(i_parts, axis=1) + 0.5 * b_x)
    half_sp = 0.5 * neg_c_sp
    log_a = t_r * half_sp + half_sp
    a = jnp.exp(log_a)
    th = jnp.tanh(log_a)
    n = -0.5 * th
    nd = n * (1.0 - th)
    half_mult = jnp.where(nd > 0.0, n * lax.rsqrt(nd), 0.0)
    return a, half_mult * (u * t_i + u)


def _scan_within_8(a, b):
    ridx = lax.broadcasted_iota(jnp.int32, a.shape, 0) & (SUBLANES - 1)
    for k in (1, 2, 4):
        a_s = pltpu.roll(a, k, axis=0)
        b_s = pltpu.roll(b, k, axis=0)
        m = ridx >= k
        b = jnp.where(m, a * b_s + b, b)
        a = jnp.where(m, a * a_s, a)
    return a, b


def _conv_slabs(ext, w_ref, b_ref, rows, first, scale=1.0):
    parts = []
    for s in range(ext.shape[0]):
        cols = slice(LANES * s, LANES * (s + 1))
        acc = b_ref[:, cols] * scale + ext[s, pl.ds(first, rows), :] * (w_ref[0:1, cols] * scale)
        for k in range(1, CONV_WIDTH):
            acc = acc + ext[s, pl.ds(first + k, rows), :] * (w_ref[k:k + 1, cols] * scale)
        parts.append(acc)
    return jnp.concatenate(parts, axis=1)


def _lru_scan_strided(a, b, hcar, a_pad, b_pad, h_pad):
    rows = a.shape[0]
    S = rows // SUBLANES
    nslab = LRU_WIDTH // LANES
    ridx = lax.broadcasted_iota(jnp.int32, (SUBLANES, LANES), 0)
    step = lambda ref, s, i: ref[s, pl.ds(i, SUBLANES, stride=SCAN_PITCH), :]
    for s in range(nslab):
        cols = slice(LANES * s, LANES * (s + 1))
        for j in range(SUBLANES):
            a_pad[s, SCAN_PITCH * j:SCAN_PITCH * j + S, :] = a[S * j:S * (j + 1), cols]
            b_pad[s, SCAN_PITCH * j:SCAN_PITCH * j + S, :] = b[S * j:S * (j + 1), cols]
    h = [jnp.zeros((SUBLANES, LANES), F32)] * nslab
    prod = [jnp.ones((SUBLANES, LANES), F32)] * nslab
    for i in range(S):
        for s in range(nslab):
            av = step(a_pad, s, i)
            h[s] = av * h[s] + step(b_pad, s, i)
            prod[s] = av * prod[s]
    for s in range(nslab):
        cols = slice(LANES * s, LANES * (s + 1))
        pcum, hcum = _scan_within_8(prod[s], h[s])
        cin = hcar[:, cols]
        ends = hcum + pcum * cin
        h[s] = jnp.where(ridx == 0, cin, pltpu.roll(ends, 1, axis=0))
        hcar[:, cols] = jnp.broadcast_to(ends[SUBLANES - 1:SUBLANES, :], (SUBLANES, LANES))
    for i in range(S):
        for s in range(nslab):
            h[s] = step(a_pad, s, i) * h[s] + step(b_pad, s, i)
            h_pad[s, pl.ds(i, SUBLANES, stride=SCAN_PITCH), :] = h[s]
    return jnp.concatenate(
        [jnp.concatenate([h_pad[s, SCAN_PITCH * j:SCAN_PITCH * j + S, :] for j in range(SUBLANES)], axis=0)
         for s in range(nslab)], axis=1)


def _ssd_cumdecay(dt, a2_row, tri):
    cum2 = jnp.dot(tri, dt * a2_row, precision=HI, preferred_element_type=F32)
    return cum2, cum2.T[0:N_SSD_HEADS, :]


def _ssd_diag(xs, bm, cm, dt, cols, cum2_t, mask_add):
    L = xs.shape[0]
    c2_t = cum2_t - jnp.log2(dt.T[0:N_SSD_HEADS, :])
    lane = lax.broadcasted_iota(jnp.int32, (L, LANES), 1)
    lo = lane < SSD_HEAD_DIM
    y_parts = []
    for g in range(N_SSD_GROUPS):
        bg = bm[:, D_STATE * g:D_STATE * (g + 1)].astype(BF16)
        cg = cm[:, D_STATE * g:D_STATE * (g + 1)].astype(BF16)
        cb = lax.dot_general(cg, bg, (((1,), (1,)), ((), ())), preferred_element_type=F32)
        for jj in range(N_SSD_HEADS // N_SSD_GROUPS // 2):
            j = (N_SSD_HEADS // N_SSD_GROUPS // 2) * g + jj
            h0, h1 = 2 * j, 2 * j + 1
            col0 = cols[:, LANES * h0:LANES * (h0 + 1)]
            col1 = cols[:, LANES * h1:LANES * (h1 + 1)]
            m0 = cb * jnp.exp2(col0 - c2_t[h0:h0 + 1, :] + mask_add)
            m1 = cb * jnp.exp2(col1 - c2_t[h1:h1 + 1, :] + mask_add)
            lhs = jnp.concatenate([m0, m1], axis=1).astype(BF16)
            xp = xs[:, LANES * j:LANES * (j + 1)]
            rhs = jnp.concatenate([jnp.where(lo, xp, 0.0), jnp.where(lo, 0.0, xp)], axis=0).astype(BF16)
            y_parts.append(jnp.dot(lhs, rhs, preferred_element_type=F32))
    return jnp.concatenate(y_parts, axis=1)


def _spread(v, sel_ref):
    p0 = v.astype(BF16)
    p1 = (v - p0.astype(F32)).astype(BF16)
    return jnp.dot(jnp.concatenate([p0, p1], axis=1), sel_ref[...], preferred_element_type=F32)


def _ssd_gate_norm(ys, xs, z_act, dskip, g_ssd):
    ys = ys + dskip * xs
    gated = ys * z_act
    half = SSD_WIDTH // N_SSD_GROUPS
    outs = []
    for g in range(N_SSD_GROUPS):
        outs.append(_rms(gated[:, half * g:half * (g + 1)], g_ssd[:, half * g:half * (g + 1)]))
    return jnp.concatenate(outs, axis=1)


def _split_w_in_kernel(wt_ref, wdt_t_ref, main_ref, dt_ref):
    main_ref[...] = wt_ref[...].T.astype(BF16)

    @pl.when(pl.program_id(0) == 0)
    def _dt():
        dt_ref[...] = jnp.zeros_like(dt_ref)
        dt_ref[:, 0:N_SSD_HEADS] = wdt_t_ref[...].T.astype(BF16)


def _split_w_in(w_t):
    cols, rows = w_t.shape
    blk = SPLIT_COLS
    return pl.pallas_call(
        _split_w_in_kernel,
        out_shape=(jax.ShapeDtypeStruct((rows, PROJ_MAIN), BF16), jax.ShapeDtypeStruct((rows, DT_PAD), BF16)),
        grid=(PROJ_MAIN // blk,),
        in_specs=[pl.BlockSpec((blk, rows), lambda j: (j, 0)),
                  pl.BlockSpec((N_SSD_HEADS, rows), lambda j: (PROJ_MAIN // N_SSD_HEADS, 0))],
        out_specs=(pl.BlockSpec((rows, blk), lambda j: (0, j)), pl.BlockSpec((rows, DT_PAD), lambda j: (0, 0))),
        compiler_params=pltpu.CompilerParams(dimension_semantics=("arbitrary",)),
        name="split_w_in",
    )(w_t, w_t)


def _inproj_kernel(x_ref, xs_ref, g_ref, w_ref, wdt_ref, lcw_ref, lcb_ref, scw_ref, scb_ref, dtb_ref,
                   o_ref, olc_ref, osc_ref, os_ref, ext_l, ext_s, *, steps_per_seq):
    last_step = pl.program_id(0) == pl.num_programs(0) - 1
    t = lax.rem(pl.program_id(0), steps_per_seq)
    rows = ROW_TILE
    hist = SUBLANES
    o1, o2, o3 = LRU_WIDTH, 2 * LRU_WIDTH, 2 * LRU_WIDTH + SSD_WIDTH

    @pl.when(last_step)
    def _sample():
        hn = _rms(xs_ref[...].reshape(-1, D_MODEL), g_ref[...]).astype(BF16)
        os_ref[:, 0:PROJ_MAIN] = jnp.dot(hn, w_ref[...], preferred_element_type=F32)
        os_ref[:, PROJ_MAIN:PROJ_PAD] = jnp.dot(hn, wdt_ref[...], preferred_element_type=F32)

    @pl.when(jnp.logical_not(last_step))
    def _prompt():
        @pl.when(t == 0)
        def _init():
            ext_l[:, 0:hist, :] = jnp.zeros((ext_l.shape[0], hist, LANES), F32)
            ext_s[:, 0:hist, :] = jnp.zeros((ext_s.shape[0], hist, LANES), F32)

        hn = _rms(x_ref[...], g_ref[...]).astype(BF16)
        lx = jnp.dot(hn, w_ref[:, 0:o1], preferred_element_type=F32)
        for s in range(ext_l.shape[0]):
            ext_l[s, hist:hist + rows, :] = lx[:, LANES * s:LANES * (s + 1)]
        xbc_in = jnp.dot(hn, w_ref[:, o3:PROJ_MAIN], preferred_element_type=F32)
        for s in range(ext_s.shape[0]):
            ext_s[s, hist:hist + rows, :] = xbc_in[:, LANES * s:LANES * (s + 1)]
        o_ref[:, o1:o2] = _gelu_tanh(jnp.dot(hn, w_ref[:, o1:o2], preferred_element_type=F32))
        z = jnp.dot(hn, w_ref[:, o2:o3], preferred_element_type=F32)
        o_ref[:, o2:o3] = _silu(z)
        o_ref[:, PROJ_MAIN:PROJ_PAD] = _softplus(
            jnp.dot(hn, wdt_ref[...], preferred_element_type=F32) + dtb_ref[...])
        o_ref[:, 0:o1] = _conv_slabs(ext_l, lcw_ref, lcb_ref, rows, hist - (CONV_WIDTH - 1))
        hx = _conv_slabs(ext_s, scw_ref, scb_ref, rows, hist - (CONV_WIDTH - 1), scale=0.5)
        o_ref[:, o3:PROJ_MAIN] = hx * jnp.tanh(hx) + hx

        @pl.when(t == steps_per_seq - 1)
        def _final():
            last = slice(hist + rows - (CONV_WIDTH - 1), hist + rows)
            for s in range(ext_l.shape[0]):
                olc_ref[:, LANES * s:LANES * (s + 1)] = ext_l[s, last, :]
            for s in range(ext_s.shape[0]):
                osc_ref[:, LANES * s:LANES * (s + 1)] = ext_s[s, last, :]

        tail_l = ext_l[:, rows:rows + hist, :]
        tail_s = ext_s[:, rows:rows + hist, :]
        ext_l[:, 0:hist, :] = tail_l
        ext_s[:, 0:hist, :] = tail_s


def _in_proj(x2d, bsz, xs, g_mix, w_main, w_dt, lcw, lcb, scw, scb, dtb):
    n = x2d.shape[0]
    steps_p = n // ROW_TILE
    steps_per_seq = steps_p // bsz
    assert xs.shape[0] * xs.shape[1] == ROW_TILE
    const = lambda i: (0, 0)
    tile = lambda i: (jnp.minimum(i, steps_p - 1), 0)
    seq_of = lambda i: (jnp.minimum(i, steps_p - 1) // steps_per_seq, 0, 0)
    once = dict(pipeline_mode=pl.Buffered(1))
    return pl.pallas_call(
        functools.partial(_inproj_kernel, steps_per_seq=steps_per_seq),
        out_shape=(
            jax.ShapeDtypeStruct((n, PROJ_PAD), F32),
            jax.ShapeDtypeStruct((bsz, CONV_WIDTH - 1, LRU_WIDTH), F32),
            jax.ShapeDtypeStruct((bsz, CONV_WIDTH - 1, SSD_CONV_DIM), F32),
            jax.ShapeDtypeStruct((ROW_TILE, PROJ_PAD), F32),
        ),
        grid=(steps_p + 1,),
        in_specs=[
            pl.BlockSpec((ROW_TILE, D_MODEL), tile),
            pl.BlockSpec(xs.shape, lambda i: (0, 0, 0), **once),
            pl.BlockSpec((1, D_MODEL), const),
            pl.BlockSpec((D_MODEL, PROJ_MAIN), const, **once),
            pl.BlockSpec((D_MODEL, DT_PAD), const, **once),
            pl.BlockSpec((CONV_WIDTH, LRU_WIDTH), const),
            pl.BlockSpec((1, LRU_WIDTH), const),
            pl.BlockSpec((CONV_WIDTH, SSD_CONV_DIM), const),
            pl.BlockSpec((1, SSD_CONV_DIM), const),
            pl.BlockSpec((1, DT_PAD), const),
        ],
        out_specs=(
            pl.BlockSpec((ROW_TILE, PROJ_PAD), tile),
            pl.BlockSpec((None, CONV_WIDTH - 1, LRU_WIDTH), seq_of),
            pl.BlockSpec((None, CONV_WIDTH - 1, SSD_CONV_DIM), seq_of),
            pl.BlockSpec((ROW_TILE, PROJ_PAD), const, **once),
        ),
        scratch_shapes=[
            pltpu.VMEM((LRU_WIDTH // LANES, SUBLANES + ROW_TILE, LANES), F32),
            pltpu.VMEM((SSD_CONV_DIM // LANES, SUBLANES + ROW_TILE, LANES), F32),
        ],
        compiler_params=pltpu.CompilerParams(
            dimension_semantics=("arbitrary",), vmem_limit_bytes=VMEM_LIMIT),
        name="in_proj",
    )(x2d, xs, g_mix, w_main, w_dt, lcw, lcb, scw, scb, dtb)


def _outmlp_kernel(xp_ref, yp_ref, xs_ref, ys_ref, wo_ref, gm_ref, wu_ref, wd_ref, gf_ref, op_ref, os_ref):
    last = pl.program_id(0) == pl.num_programs(0) - 1

    def run(x_ref, y_ref, o_ref):
        x = x_ref[...].reshape(-1, D_MODEL)
        x1 = x + jnp.dot(y_ref[...], wo_ref[...], preferred_element_type=F32)
        m = _rms(x1, gm_ref[...]).astype(BF16)
        u = jnp.dot(m, wu_ref[...], preferred_element_type=F32)
        u = jnp.square(jnp.maximum(u, 0.0)).astype(BF16)
        x2 = x1 + jnp.dot(u, wd_ref[...], preferred_element_type=F32)
        o_ref[...] = _rms(x2, gf_ref[...]).reshape(o_ref.shape)

    @pl.when(jnp.logical_not(last))
    def _prompt():
        run(xp_ref, yp_ref, op_ref)

    @pl.when(last)
    def _sample():
        run(xs_ref, ys_ref, os_ref)


def _out_mlp(xp, ymix_p, xs, ymix_s, w_out_b, g_mlp, w_up_b, w_down_b, g_final):
    n_p = xp.shape[0]
    steps_p = n_p // ROW_TILE
    assert ymix_s.shape[0] == ROW_TILE
    const = lambda i: (0, 0)
    tile = lambda i: (jnp.minimum(i, steps_p - 1), 0)
    once = dict(pipeline_mode=pl.Buffered(1))
    return pl.pallas_call(
        _outmlp_kernel,
        out_shape=(jax.ShapeDtypeStruct((n_p, D_MODEL), F32), jax.ShapeDtypeStruct(xs.shape, F32)),
        grid=(steps_p + 1,),
        in_specs=[
            pl.BlockSpec((ROW_TILE, D_MODEL), tile),
            pl.BlockSpec((ROW_TILE, MIX_WIDTH), tile),
            pl.BlockSpec(xs.shape, lambda i: (0, 0, 0), **once),
            pl.BlockSpec((ROW_TILE, MIX_WIDTH), const, **once),
            pl.BlockSpec((MIX_WIDTH, D_MODEL), const, **once),
            pl.BlockSpec((1, D_MODEL), const),
            pl.BlockSpec((D_MODEL, D_FF), const, **once),
            pl.BlockSpec((D_FF, D_MODEL), const, **once),
            pl.BlockSpec((1, D_MODEL), const),
        ],
        out_specs=(pl.BlockSpec((ROW_TILE, D_MODEL), tile), pl.BlockSpec(xs.shape, lambda i: (0, 0, 0))),
        compiler_params=pltpu.CompilerParams(
            dimension_semantics=("arbitrary",), vmem_limit_bytes=VMEM_LIMIT_OUT),
        name="out_mlp",
    )(xp, ymix_p, xs, ymix_s, w_out_b, g_mlp, w_up_b, w_down_b, g_final)


def _mixer_prompt_kernel(u_ref, gl_ref, zact_ref, xbc_ref, dt_ref,
                         wg_ref, ba_ref, bx_ref, lam_ref, glru_ref, alog_ref, dskip_ref, gssd_ref,
                         selt_ref, selp_ref, wo_ref, wu_ref, wd_ref,
                         y_ref, olh_ref, osh_ref, wo_b_ref, wu_b_ref, wd_b_ref,
                         a_pad, b_pad, h_pad, hcar, ht):
    t = pl.program_id(1)
    nt = pl.num_programs(1)
    tc = PROMPT_TC
    wo_b_ref[...] = wo_ref[...].astype(BF16)
    wu_b_ref[...] = wu_ref[...].astype(BF16)
    wd_b_ref[...] = wd_ref[...].astype(BF16)

    @pl.when(t == 0)
    def _init():
        hcar[...] = jnp.zeros_like(hcar)
        ht[...] = jnp.zeros_like(ht)

    neg_c_sp = (-LRU_C) * _softplus(-lam_ref[...])
    lane1 = lax.broadcasted_iota(jnp.int32, (1, LANES), 1)
    a2_row = jnp.where(lane1 < N_SSD_HEADS, -LOG2E * jnp.exp(alog_ref[...]), 0.0)
    L = SSD_CHUNK
    rr = lax.broadcasted_iota(jnp.int32, (L, L), 0)
    cc = lax.broadcasted_iota(jnp.int32, (L, L), 1)
    causal = cc <= rr
    tri = jnp.where(causal, 1.0, 0.0).astype(F32)
    mask_add = jnp.where(causal, 0.0, NEG_BIG).astype(F32)
    half = SSD_WIDTH // N_SSD_GROUPS

    chunks = [(n, c) for n in range(PROMPT_NB) for c in range(tc // L)]
    cums = [_ssd_cumdecay(dt_ref[n, L * c:L * (c + 1), :], a2_row, tri) for n, c in chunks]
    dts = [dt_ref[n, L * c:L * (c + 1), :] for n, c in chunks]
    cum_all = jnp.concatenate([cum2 for cum2, _ in cums], axis=0)
    cols_all = _spread(cum_all, selt_ref)
    ecol_all = _spread(jnp.exp2(cum_all), selp_ref)
    sdt_all = _spread(jnp.concatenate(
        [jnp.exp2(cum2[L - 1:L, :] - cum2) * dt for (cum2, _), dt in zip(cums, dts)], axis=0), selp_ref)

    for n in range(PROMPT_NB):
        a, b = _lru_coeffs(u_ref[n], wg_ref, ba_ref[...], bx_ref[...], neg_c_sp)
        hseq = _lru_scan_strided(a, b, hcar.at[n], a_pad.at[n], b_pad.at[n], h_pad.at[n])
        y_ref[n, :, 0:LRU_WIDTH] = _rms(hseq * gl_ref[n], glru_ref[...]).astype(BF16)

        for c in range(tc // L):
            k = chunks.index((n, c))
            rows = slice(L * c, L * (c + 1))
            krows = slice(L * k, L * (k + 1))
            xs = xbc_ref[n, rows, 0:SSD_WIDTH]
            bm = xbc_ref[n, rows, SSD_WIDTH:SSD_WIDTH + N_SSD_GROUPS * D_STATE]
            cm = xbc_ref[n, rows, SSD_WIDTH + N_SSD_GROUPS * D_STATE:SSD_CONV_DIM]
            y_diag = _ssd_diag(xs, bm, cm, dts[k], cols_all[krows, :], cums[k][1], mask_add)
            ecol = ecol_all[krows, :]
            xw = xs * sdt_all[krows, :]
            dec = ecol[L - 1:L, :]
            y_off_parts = []
            for g in range(N_SSD_GROUPS):
                htg = ht[n, g]
                cg = cm[:, D_STATE * g:D_STATE * (g + 1)].astype(BF16)
                y_off_parts.append(jnp.dot(cg, htg.astype(BF16), preferred_element_type=F32))
                bg_t = bm[:, D_STATE * g:D_STATE * (g + 1)].T.astype(BF16)
                st = jnp.dot(bg_t, xw[:, half * g:half * (g + 1)].astype(BF16), preferred_element_type=F32)
                ht[n, g] = htg * dec[:, half * g:half * (g + 1)] + st
            ys = y_diag + jnp.concatenate(y_off_parts, axis=1) * ecol
            y_ref[n, rows, LRU_WIDTH:MIX_WIDTH] = _ssd_gate_norm(
                ys, xs, zact_ref[n, rows, :], dskip_ref[...], gssd_ref[...]).astype(BF16)

    @pl.when(t == nt - 1)
    def _final():
        for n in range(PROMPT_NB):
            olh_ref[n] = hcar[n, 0:1, :]
            for g in range(N_SSD_GROUPS):
                osh_ref[n, half * g:half * (g + 1), :] = ht[n, g].T


def _param_specs(const):
    return [
        pl.BlockSpec((CONV_WIDTH, LRU_WIDTH), const),
        pl.BlockSpec((1, LRU_WIDTH), const),
        pl.BlockSpec((LRU_WIDTH // MXU_DIM, MXU_DIM, 2 * MXU_DIM), lambda *_: (0, 0, 0)),
        pl.BlockSpec((1, LRU_WIDTH), const),
        pl.BlockSpec((1, LRU_WIDTH), const),
        pl.BlockSpec((1, LRU_WIDTH), const),
        pl.BlockSpec((1, LRU_WIDTH), const),
        pl.BlockSpec((CONV_WIDTH, SSD_CONV_DIM), const),
        pl.BlockSpec((1, SSD_CONV_DIM), const),
        pl.BlockSpec((1, DT_PAD), const),
        pl.BlockSpec((1, DT_PAD), const),
        pl.BlockSpec((1, SSD_WIDTH), const),
        pl.BlockSpec((1, SSD_WIDTH), const),
    ]


def _head_selectors():
    k = np.arange(2 * LANES)[:, None] % LANES
    sel_t = (k == np.arange(N_SSD_HEADS * LANES)[None, :] // LANES).astype(np.float32)
    sel_p = (k == np.arange(SSD_WIDTH)[None, :] // SSD_HEAD_DIM).astype(np.float32)
    return jnp.asarray(sel_t, BF16), jnp.asarray(sel_p, BF16)


def _mixer_prompt(act, wg, b_a, b_x, lam, g_lru, a_log, d_skip, g_ssd, sel_t, sel_p, w_out, w_up, w_down):
    bsz, seq, _ = act.shape
    tc = PROMPT_TC
    nb = PROMPT_NB
    steps = (bsz // nb) * (seq // tc)
    const = lambda b, t: (0, 0)
    w_slice = lambda b, t: (b * (seq // tc) + t, 0)
    assert all(w.shape[0] % (steps * 2 * SUBLANES) == 0 for w in (w_out, w_up, w_down))
    w_specs = [pl.BlockSpec((w.shape[0] // steps, w.shape[1]), w_slice) for w in (w_out, w_up, w_down)]
    in_specs = [
        pl.BlockSpec((nb, tc, LRU_WIDTH), lambda b, t: (b, t, 0)),
        pl.BlockSpec((nb, tc, LRU_WIDTH), lambda b, t: (b, t, 1)),
        pl.BlockSpec((nb, tc, SSD_WIDTH), lambda b, t: (b, t, 2)),
        pl.BlockSpec((nb, tc, SSD_CONV_DIM), lambda b, t: (b, t, 2)),
        pl.BlockSpec((nb, tc, DT_PAD), lambda b, t: (b, t, PROJ_MAIN // DT_PAD)),
        pl.BlockSpec((LRU_WIDTH // MXU_DIM, MXU_DIM, 2 * MXU_DIM), lambda b, t: (0, 0, 0)),
        pl.BlockSpec((1, LRU_WIDTH), const),
        pl.BlockSpec((1, LRU_WIDTH), const),
        pl.BlockSpec((1, LRU_WIDTH), const),
        pl.BlockSpec((1, LRU_WIDTH), const),
        pl.BlockSpec((1, DT_PAD), const),
        pl.BlockSpec((1, SSD_WIDTH), const),
        pl.BlockSpec((1, SSD_WIDTH), const),
        pl.BlockSpec((2 * LANES, N_SSD_HEADS * LANES), const),
        pl.BlockSpec((2 * LANES, SSD_WIDTH), const),
    ] + w_specs
    out_shape = (
        jax.ShapeDtypeStruct((bsz, seq, MIX_WIDTH), BF16),
        jax.ShapeDtypeStruct((bsz, 1, LRU_WIDTH), F32),
        jax.ShapeDtypeStruct((bsz, SSD_WIDTH, D_STATE), F32),
    ) + tuple(jax.ShapeDtypeStruct(w.shape, BF16) for w in (w_out, w_up, w_down))
    out_specs = (
        pl.BlockSpec((nb, tc, MIX_WIDTH), lambda b, t: (b, t, 0)),
        pl.BlockSpec((nb, 1, LRU_WIDTH), lambda b, t: (b, 0, 0)),
        pl.BlockSpec((nb, SSD_WIDTH, D_STATE), lambda b, t: (b, 0, 0)),
    ) + tuple(w_specs)
    scratch = [
        pltpu.VMEM((nb, LRU_WIDTH // LANES, SUBLANES * SCAN_PITCH, LANES), F32),
        pltpu.VMEM((nb, LRU_WIDTH // LANES, SUBLANES * SCAN_PITCH, LANES), F32),
        pltpu.VMEM((nb, LRU_WIDTH // LANES, SUBLANES * SCAN_PITCH, LANES), F32),
        pltpu.VMEM((nb, SUBLANES, LRU_WIDTH), F32),
        pltpu.VMEM((nb, N_SSD_GROUPS, D_STATE, SSD_WIDTH // N_SSD_GROUPS), F32),
    ]
    return pl.pallas_call(
        _mixer_prompt_kernel,
        out_shape=out_shape,
        grid=(bsz // nb, seq // tc),
        in_specs=in_specs,
        out_specs=out_specs,
        scratch_shapes=scratch,
        compiler_params=pltpu.CompilerParams(
            dimension_semantics=("parallel", "arbitrary"), vmem_limit_bytes=VMEM_LIMIT),
        name="mixer_prompt",
    )(act, act, act, act, act, wg, b_a, b_x, lam, g_lru, a_log, d_skip, g_ssd, sel_t, sel_p,
      w_out, w_up, w_down)


def _mixer_sample_kernel(lx_ref, gate_ref, z_ref, xbc_ref, dt_ref,
                         slc_ref, slh_ref, ssc_ref, ssh_ref,
                         lcw_ref, lcb_ref, wg_ref, ba_ref, bx_ref, lam_ref, glru_ref,
                         scw_ref, scb_ref, dtb_ref, alog_ref, dskip_ref, gssd_ref, selt_ref, selp_ref,
                         y_ref, olc_ref, olh_ref, osc_ref, osh_ref,
                         ext_l, ext_s, pad_scr, yoff_scr, *, T):
    S = SAMPLE_SEQS
    P = SUBLANES
    K1 = CONV_WIDTH - 1
    R = S * P
    row_i = lax.broadcasted_iota(jnp.int32, (R, 1), 0) & (P - 1)
    valid = row_i < T

    def pad_rows(ref):
        width = ref.shape[-1]
        pad_scr[:, :, 0:width] = jnp.zeros((S, P, width), F32)
        pad_scr[:, 0:T, 0:width] = ref[...].reshape(S, T, width)
        return pad_scr[:, :, 0:width].reshape(R, width)

    ext_l[...] = jnp.zeros_like(ext_l)
    ext_s[...] = jnp.zeros_like(ext_s)
    for k in range(K1):
        ext_l[:, k, :] = slc_ref[k]
    ext_l[:, K1:K1 + T, :] = lx_ref[...].reshape(S, T, LRU_WIDTH)
    for k in range(K1):
        ext_s[:, k, :] = ssc_ref[k]
    ext_s[:, K1:K1 + T, :] = xbc_ref[...].reshape(S, T, SSD_CONV_DIM)
    for k in range(K1):
        olc_ref[k] = ext_l[:, T + k, :]
        osc_ref[k] = ext_s[:, T + k, :]

    el = ext_l[...].reshape(R, LRU_WIDTH)
    es = ext_s[...].reshape(R, SSD_CONV_DIM)

    def conv(e, w_ref, b_ref):
        out = b_ref[...] + e * w_ref[0:1, :]
        for k in range(1, CONV_WIDTH):
            out = out + pltpu.roll(e, R - k, axis=0) * w_ref[k:k + 1, :]
        return out

    u = conv(el, lcw_ref, lcb_ref)
    neg_c_sp = (-LRU_C) * _softplus(-lam_ref[...])
    a, b = _lru_coeffs(u, wg_ref, ba_ref[...], bx_ref[...], neg_c_sp)
    a, b = _scan_within_8(a, b)
    h0 = jnp.broadcast_to(slh_ref[...][:, None, :], (S, P, LRU_WIDTH)).reshape(R, LRU_WIDTH)
    hseq = a * h0 + b
    olh_ref[...] = hseq.reshape(S, P, LRU_WIDTH)[:, T - 1, :]
    gate = pad_rows(gate_ref)
    y_lru = _rms(hseq * _gelu_tanh(gate), glru_ref[...])

    xbc = conv(es, scw_ref, scb_ref)
    xbc = _silu(xbc)
    xs = xbc[:, 0:SSD_WIDTH]
    bm = xbc[:, SSD_WIDTH:SSD_WIDTH + N_SSD_GROUPS * D_STATE]
    cm = xbc[:, SSD_WIDTH + N_SSD_GROUPS * D_STATE:]
    dt_raw = pad_rows(dt_ref)
    dt = jnp.where(valid, _softplus(dt_raw + dtb_ref[...]), 0.0)
    lane1 = lax.broadcasted_iota(jnp.int32, (1, LANES), 1)
    a2_row = jnp.where(lane1 < N_SSD_HEADS, -LOG2E * jnp.exp(alog_ref[...]), 0.0)

    rr = lax.broadcasted_iota(jnp.int32, (R, R), 0)
    cc = lax.broadcasted_iota(jnp.int32, (R, R), 1)
    allowed = (cc <= rr) & ((rr - cc) <= (rr & (P - 1)))
    tri = jnp.where(allowed, 1.0, 0.0).astype(F32)
    mask_add = jnp.where(allowed, 0.0, NEG_BIG).astype(F32)

    cum2, cum2_t = _ssd_cumdecay(dt, a2_row, tri)
    y_diag = _ssd_diag(xs, bm, cm, dt, _spread(cum2, selt_ref), cum2_t, mask_add)
    ecol = _spread(jnp.exp2(cum2), selp_ref)
    end2 = jnp.broadcast_to(cum2.reshape(S, P, LANES)[:, P - 1:P, :], (S, P, LANES)).reshape(R, LANES)
    xw = xs * _spread(jnp.exp2(end2 - cum2) * dt, selp_ref)
    ecum_t = jnp.exp2(cum2_t)

    half = SSD_WIDTH // N_SSD_GROUPS
    for q in range(S):
        r0 = P * q
        vq = jnp.broadcast_to(ecum_t[:, r0 + P - 1:r0 + P], (N_SSD_HEADS, LANES))
        for g in range(N_SSD_GROUPS):
            hqg = ssh_ref[q, half * g:half * (g + 1), :]
            cq = cm[r0:r0 + P, D_STATE * g:D_STATE * (g + 1)].astype(BF16)
            yoff_scr[r0:r0 + P, half * g:half * (g + 1)] = lax.dot_general(
                cq, hqg.astype(BF16), (((1,), (1,)), ((), ())), preferred_element_type=F32)
            bq = bm[r0:r0 + P, D_STATE * g:D_STATE * (g + 1)].astype(BF16)
            xq = xw[r0:r0 + P, half * g:half * (g + 1)].astype(BF16)
            st = lax.dot_general(xq, bq, (((0,), (0,)), ((), ())), preferred_element_type=F32)
            for e in range(N_SSD_HEADS // N_SSD_GROUPS):
                h = (N_SSD_HEADS // N_SSD_GROUPS) * g + e
                lo_r = SSD_HEAD_DIM * e
                osh_ref[q, SSD_HEAD_DIM * h:SSD_HEAD_DIM * (h + 1), :] = (
                    vq[h:h + 1, :] * hqg[lo_r:lo_r + SSD_HEAD_DIM, :] + st[lo_r:lo_r + SSD_HEAD_DIM, :])

    ys = y_diag + yoff_scr[...] * ecol
    z = pad_rows(z_ref)
    y_ssd = _ssd_gate_norm(ys, xs, _silu(z), dskip_ref[...], gssd_ref[...])
    y_ref[:, 0:LRU_WIDTH] = y_lru.reshape(S, P, LRU_WIDTH)[:, 0:T, :].reshape(S * T, LRU_WIDTH).astype(BF16)
    y_ref[:, LRU_WIDTH:MIX_WIDTH] = (
        y_ssd.reshape(S, P, SSD_WIDTH)[:, 0:T, :].reshape(S * T, SSD_WIDTH).astype(BF16))


def _mixer_sample(proj, T, st_lc, st_lh, st_sc, st_sh, params, sel_t, sel_p):
    nseq = proj.shape[0] // T
    S = SAMPLE_SEQS
    const = lambda i: (0, 0)
    in_specs = [
        pl.BlockSpec((S * T, LRU_WIDTH), lambda i: (i, 0)),
        pl.BlockSpec((S * T, LRU_WIDTH), lambda i: (i, 1)),
        pl.BlockSpec((S * T, SSD_WIDTH), lambda i: (i, 2)),
        pl.BlockSpec((S * T, SSD_CONV_DIM), lambda i: (i, 2)),
        pl.BlockSpec((S * T, DT_PAD), lambda i: (i, PROJ_MAIN // DT_PAD)),
        pl.BlockSpec((CONV_WIDTH - 1, S, LRU_WIDTH), lambda i: (0, i, 0)),
        pl.BlockSpec((S, LRU_WIDTH), lambda i: (i, 0)),
        pl.BlockSpec((CONV_WIDTH - 1, S, SSD_CONV_DIM), lambda i: (0, i, 0)),
        pl.BlockSpec((S, SSD_WIDTH, D_STATE), lambda i: (i, 0, 0)),
    ] + _param_specs(const) + [
        pl.BlockSpec((2 * LANES, N_SSD_HEADS * LANES), const),
        pl.BlockSpec((2 * LANES, SSD_WIDTH), const),
    ]
    out_shape = (
        jax.ShapeDtypeStruct((nseq * T, MIX_WIDTH), BF16),
        jax.ShapeDtypeStruct((CONV_WIDTH - 1, nseq, LRU_WIDTH), F32),
        jax.ShapeDtypeStruct((nseq, LRU_WIDTH), F32),
        jax.ShapeDtypeStruct((CONV_WIDTH - 1, nseq, SSD_CONV_DIM), F32),
        jax.ShapeDtypeStruct((nseq, SSD_WIDTH, D_STATE), F32),
    )
    out_specs = (
        pl.BlockSpec((S * T, MIX_WIDTH), lambda i: (i, 0)),
        pl.BlockSpec((CONV_WIDTH - 1, S, LRU_WIDTH), lambda i: (0, i, 0)),
        pl.BlockSpec((S, LRU_WIDTH), lambda i: (i, 0)),
        pl.BlockSpec((CONV_WIDTH - 1, S, SSD_CONV_DIM), lambda i: (0, i, 0)),
        pl.BlockSpec((S, SSD_WIDTH, D_STATE), lambda i: (i, 0, 0)),
    )
    scratch = [
        pltpu.VMEM((S, SUBLANES, LRU_WIDTH), F32),
        pltpu.VMEM((S, SUBLANES, SSD_CONV_DIM), F32),
        pltpu.VMEM((S, SUBLANES, LRU_WIDTH), F32),
        pltpu.VMEM((S * SUBLANES, SSD_WIDTH), F32),
    ]
    return pl.pallas_call(
        functools.partial(_mixer_sample_kernel, T=T),
        out_shape=out_shape,
        grid=(nseq // S,),
        in_specs=in_specs,
        out_specs=out_specs,
        scratch_shapes=scratch,
        compiler_params=pltpu.CompilerParams(
            dimension_semantics=("parallel",), vmem_limit_bytes=VMEM_LIMIT),
        name="mixer_sample",
    )(proj, proj, proj, proj, proj, st_lc, st_lh, st_sc, st_sh, *params, sel_t, sel_p)


def _gate_weights(w_a, w_x):
    def tiles(w):
        per = MXU_DIM // LRU_BLOCK
        w4 = w.reshape(N_LRU_HEADS // per, per, LRU_BLOCK, LRU_BLOCK)
        eye = 0.5 * jnp.eye(per, dtype=w.dtype)
        t = jnp.einsum('jaik,ab->jaibk', w4, eye)
        return t.reshape(N_LRU_HEADS // per, MXU_DIM, MXU_DIM)
    return jnp.concatenate([tiles(w_a), tiles(w_x)], axis=2).astype(BF16)


def kernel(x_prompt, x_sample, state_lru_conv, state_lru_h, state_ssd_conv, state_ssd_h, g_mix, w_in,
           lru_conv_w, lru_conv_b, w_a, b_a, w_x, b_x, lam, g_lru_out, ssd_conv_w, ssd_conv_b, dt_bias,
           a_log, d_skip, g_ssd_out, w_out, g_mlp, w_up, w_down, g_final):
    depth = w_in.shape[0]
    assert depth == 1
    bp, seq, _ = x_prompt.shape
    bs, dseq, _ = x_sample.shape
    l = 0
    row = lambda v: v.reshape(1, -1)
    w_main, w_dt = _split_w_in(jnp.swapaxes(w_in, 1, 2)[l])
    params = (
        lru_conv_w[l], row(lru_conv_b[l]), _gate_weights(w_a[l], w_x[l]),
        row(b_a[l]), row(b_x[l]), row(lam[l]), row(g_lru_out[l]),
        ssd_conv_w[l], row(ssd_conv_b[l]),
        jnp.pad(row(dt_bias[l]), ((0, 0), (0, DT_PAD - N_SSD_HEADS))),
        jnp.pad(row(a_log[l]), ((0, 0), (0, DT_PAD - N_SSD_HEADS))),
        row(jnp.repeat(d_skip[l], SSD_HEAD_DIM)), row(g_ssd_out[l]),
    )
    gmix = row(g_mix[l])
    gmlp = row(g_mlp[l])
    gfin = row(g_final)

    xp2 = x_prompt.reshape(bp * seq, D_MODEL)
    (lcw, lcb, wg, ba, bx, lam_r, glru, scw, scb, dtb, alog, dskip, gssd) = params
    act_p, p_lc, p_sc, proj_s = _in_proj(xp2, bp, x_sample, gmix, w_main, w_dt, lcw, lcb, scw, scb, dtb)
    sel_t, sel_p = _head_selectors()
    ymix_p, p_lh, p_sh, w_out_b, w_up_b, w_down_b = _mixer_prompt(
        act_p.reshape(bp, seq, PROJ_PAD), wg, ba, bx, lam_r, glru, alog, dskip, gssd, sel_t, sel_p,
        w_out[l], w_up[l], w_down[l])

    ymix_s, s_lc, s_lh, s_sc, s_sh = _mixer_sample(
        proj_s, dseq, jnp.swapaxes(state_lru_conv[l], 0, 1), state_lru_h[l],
        jnp.swapaxes(state_ssd_conv[l], 0, 1),
        state_ssd_h[l].reshape(bs, SSD_WIDTH, D_STATE), params, sel_t, sel_p)
    y_prompt, y_sample = _out_mlp(xp2, ymix_p.reshape(bp * seq, MIX_WIDTH), x_sample, ymix_s,
                                  w_out_b, gmlp, w_up_b, w_down_b, gfin)

    hshape = (N_SSD_HEADS, SSD_HEAD_DIM, D_STATE)
    return (
        y_prompt.reshape(bp, seq, D_MODEL), y_sample,
        p_lc[None], p_lh.reshape(1, bp, LRU_WIDTH), p_sc[None], p_sh.reshape(1, bp, *hshape),
        jnp.swapaxes(s_lc, 0, 1)[None], s_lh[None], jnp.swapaxes(s_sc, 0, 1)[None],
        s_sh.reshape(1, bs, *hshape),
    )
```
